```python
import jax
import jax.numpy as jnp
from jax import lax
import numpy as np

D_MODEL = 1024
BATCH = 32
SEQ = 2048
DEPTH = 2

HEAD_DIM = 64
BLOCK = 128
A_HEADS = 8
A_KV_HEADS = 2
WINDOW = 128
B_HEADS = 8
FOX_BIAS_INIT = 3.0
C_HEADS = 8
IDX_HEADS = 4
IDX_DIM = 64
DSA_TOPK_MAX = 256
D_HEADS = 8
MLA_Q_RANK = 256
MLA_KV_RANK = 128
MLA_NOPE = 64
MLA_ROPE = 32
MLA_V = 64
ROPE_THETA = 10000.0
N_EXPERTS = 16
N_GROUPS = 4
EXPERTS_PER_GROUP = N_EXPERTS // N_GROUPS
TOPK_GROUPS = 1
TOPK_EXPERTS = 2
D_FF_EXPERT = 512
LN_EPS = 1e-5
RMS_EPS = 1e-6
ALPHA = (2 * DEPTH) ** 0.25
BETA = (8 * DEPTH) ** -0.25
N_EVEN = (DEPTH + 1) // 2
N_ODD = DEPTH // 2
MIX_WIDTH = A_HEADS * HEAD_DIM + B_HEADS * HEAD_DIM
EVEN_SIZES = (A_HEADS * HEAD_DIM, A_KV_HEADS * HEAD_DIM, A_KV_HEADS * HEAD_DIM,
              B_HEADS * HEAD_DIM, B_HEADS * HEAD_DIM, B_HEADS * HEAD_DIM, B_HEADS)
EVEN_VALUE_SLOTS = (2, 5)
ODD_SIZES = (C_HEADS * HEAD_DIM, HEAD_DIM, HEAD_DIM, IDX_HEADS * IDX_DIM, IDX_DIM, IDX_HEADS,
             MLA_Q_RANK, MLA_KV_RANK, MLA_ROPE)
ODD_VALUE_SLOTS = (2,)

kernel_name = 'hybrid_swa_fox_dsa_mla_grouped_moe_deepnorm'


def split_cols(h, sizes):
    out, start = [], 0
    for n in sizes:
        out.append(h[..., start:start + n])
        start += n
    return out


def layer_norm(x, g, b):
    xf = x.astype(jnp.float32)
    mu = jnp.mean(xf, -1, keepdims=True)
    var = jnp.mean(jnp.square(xf - mu), -1, keepdims=True)
    return ((xf - mu) * lax.rsqrt(var + LN_EPS)).astype(x.dtype) * g + b


def rms_norm(x, g):
    xf = x.astype(jnp.float32)
    return (xf * lax.rsqrt(jnp.mean(jnp.square(xf), -1, keepdims=True) + RMS_EPS)).astype(x.dtype) * g


def alibi_slopes(n):
    return 2.0 ** (-8.0 * jnp.arange(1, n + 1, dtype=jnp.float32) / n)


def rope(x, pos):
    half = x.shape[-1] // 2
    inv = ROPE_THETA ** (-jnp.arange(half, dtype=jnp.float32) / half)
    ang = pos.astype(jnp.float32)[:, None] * inv[None, :]
    shape = (1, x.shape[1]) + (1,) * (x.ndim - 3) + (half,)
    cos = jnp.cos(ang).reshape(shape).astype(x.dtype)
    sin = jnp.sin(ang).reshape(shape).astype(x.dtype)
    x1, x2 = x[..., :half], x[..., half:]
    return jnp.concatenate([x1 * cos - x2 * sin, x1 * sin + x2 * cos], axis=-1)


def sliding_window_sink_attention(q, k, v, sinks):
    B, S, Hq, d = q.shape
    Hkv = k.shape[2]
    G = Hq // Hkv
    nb = S // BLOCK
    qb = q.reshape(B, nb, BLOCK, Hkv, G, d)

    def band(a):
        a = a.reshape(B, nb, BLOCK, Hkv, d)
        prev = jnp.concatenate([jnp.zeros_like(a[:, :1]), a[:, :-1]], axis=1)
        return jnp.concatenate([prev, a], axis=2)

    k_band, v_band = band(k), band(v)
    logits = jnp.einsum('bnqkgd,bnskd->bnkgqs', qb, k_band,
                        preferred_element_type=jnp.float32) * d ** -0.5
    blk = jnp.arange(nb)[:, None, None]
    t = blk * BLOCK + jnp.arange(BLOCK)[None, :, None]
    s = (blk - 1) * BLOCK + jnp.arange(2 * BLOCK)[None, None, :]
    dist = t - s
    valid = (dist >= 0) & (dist < WINDOW) & (s >= 0)
    slopes = alibi_slopes(Hq).reshape(Hkv, G)
    logits = logits - slopes[:, :, None, None] * dist[:, None, None].astype(jnp.float32)
    logits = jnp.where(valid[:, None, None], logits, -jnp.inf)
    sink = jnp.broadcast_to(sinks.astype(jnp.float32).reshape(Hkv, G, 1, 1), logits.shape[:-1] + (1,))
    p = jax.nn.softmax(jnp.concatenate([logits, sink], axis=-1), axis=-1)[..., :-1].astype(v.dtype)
    out = jnp.einsum('bnkgqs,bnskd->bnqkgd', p, v_band)
    return out.reshape(B, S, Hq * d)


def causal_block_attention(q, k, v, log_forget=None):
    B, S, H, dk = q.shape
    dv = v.shape[-1]
    nb = S // BLOCK
    scale = dk ** -0.5
    pos = jnp.arange(S)
    cum = None if log_forget is None else jnp.cumsum(log_forget, axis=1).swapaxes(1, 2)

    def one_block(i):
        q_i = lax.dynamic_slice_in_dim(q, i * BLOCK, BLOCK, axis=1)
        t = i * BLOCK + jnp.arange(BLOCK)
        logits = jnp.einsum('bqhd,bshd->bhqs', q_i, k, preferred_element_type=jnp.float32) * scale
        if cum is not None:
            c_i = lax.dynamic_slice_in_dim(cum, i * BLOCK, BLOCK, axis=2)
            logits = logits + c_i[..., :, None] - cum[:, :, None, :]
        logits = jnp.where(pos[None, :] <= t[:, None], logits, -jnp.inf)
        p = jax.nn.softmax(logits, axis=-1).astype(v.dtype)
        return jnp.einsum('bhqs,bshd->bqhd', p, v)

    out = lax.map(one_block, jnp.arange(nb))
    return out.swapaxes(0, 1).reshape(B, S, H * dv)


def dsa_attention(q, k, v, q_idx, k_idx, w_idx):
    B, S, H, d = q.shape
    nb = S // BLOCK
    top_k = min(DSA_TOPK_MAX, S // 4)
    slopes = alibi_slopes(H)
    kv = jnp.concatenate([k, v], axis=-1)
    pos = jnp.arange(S)
    w_idx = w_idx.astype(jnp.float32) * IDX_HEADS ** -0.5

    def one_block(i):
        sl = lambda a: lax.dynamic_slice_in_dim(a, i * BLOCK, BLOCK, axis=1)
        q_i, qi_i, w_i = sl(q), sl(q_idx), sl(w_idx)
        t = i * BLOCK + jnp.arange(BLOCK)
        dots = jnp.einsum('bqhe,bse->bqhs', qi_i, k_idx,
                          preferred_element_type=jnp.float32) * IDX_DIM ** -0.5
        score = jnp.einsum('bqh,bqhs->bqs', w_i, jax.nn.relu(dots))
        score = jnp.where(pos[None, None, :] <= t[None, :, None], score, -jnp.inf)
        _, idx = lax.top_k(score, top_k)
        kv_sel = jax.vmap(lambda a, j: a[j])(kv, idx)
        k_sel, v_sel = kv_sel[..., :d], kv_sel[..., d:]
        dist = (t[None, :, None] - idx).astype(jnp.float32)
        logits = jnp.einsum('bqhd,bqkd->bhqk', q_i, k_sel,
                            preferred_element_type=jnp.float32) * d ** -0.5
        logits = logits - slopes[None, :, None, None] * dist[:, None]
        logits = jnp.where(dist[:, None] >= 0, logits, -jnp.inf)
        p = jax.nn.softmax(logits, axis=-1).astype(v.dtype)
        return jnp.einsum('bhqk,bqkd->bqhd', p, v_sel)

    out = lax.map(one_block, jnp.arange(nb))
    return out.swapaxes(0, 1).reshape(B, S, H * d)


def even_mixer(x, w_in, b_f, sinks, w_o):
    B, S, _ = x.shape
    qa, ka, va, qb, kb, vb, fg = split_cols(x @ w_in, EVEN_SIZES)
    qa = qa.reshape(B, S, A_HEADS, HEAD_DIM)
    ka = ka.reshape(B, S, A_KV_HEADS, HEAD_DIM)
    va = va.reshape(B, S, A_KV_HEADS, HEAD_DIM)
    qb = qb.reshape(B, S, B_HEADS, HEAD_DIM)
    kb = kb.reshape(B, S, B_HEADS, HEAD_DIM)
    vb = vb.reshape(B, S, B_HEADS, HEAD_DIM)
    log_f = jax.nn.log_sigmoid((fg + b_f).astype(jnp.float32))
    out_a = sliding_window_sink_attention(qa, ka, va, sinks)
    out_b = causal_block_attention(qb, kb, vb, log_f)
    return jnp.concatenate([out_a, out_b], axis=-1) @ w_o


def odd_mixer(x, w_in, q_norm_g, kv_norm_g, w_uq, w_ukv, w_o):
    B, S, _ = x.shape
    qc, kc, vc, qi, ki, wi, cq, ckv, kr = split_cols(x @ w_in, ODD_SIZES)
    out_c = dsa_attention(qc.reshape(B, S, C_HEADS, HEAD_DIM), kc, vc,
                          qi.reshape(B, S, IDX_HEADS, IDX_DIM), ki, wi)
    pos = jnp.arange(S)
    q = (rms_norm(cq, q_norm_g) @ w_uq).reshape(B, S, D_HEADS, MLA_NOPE + MLA_ROPE)
    kv = (rms_norm(ckv, kv_norm_g) @ w_ukv).reshape(B, S, D_HEADS, MLA_NOPE + MLA_V)
    q = jnp.concatenate([q[..., :MLA_NOPE], rope(q[..., MLA_NOPE:], pos)], axis=-1)
    k_rope = jnp.broadcast_to(rope(kr, pos)[:, :, None, :], (B, S, D_HEADS, MLA_ROPE))
    k = jnp.concatenate([kv[..., :MLA_NOPE], k_rope], axis=-1)
    out_d = causal_block_attention(q, k, kv[..., MLA_NOPE:])
    return jnp.concatenate([out_c, out_d], axis=-1) @ w_o


def routed_moe(x, router_w, router_b, w_gate, w_up, w_down):
    B, S, D = x.shape
    xt = x.reshape(B * S, D)
    aff = jax.nn.sigmoid((xt @ router_w).astype(jnp.float32))
    sel = aff + router_b.astype(jnp.float32)
    grp_score = lax.top_k(sel.reshape(-1, N_GROUPS, EXPERTS_PER_GROUP), 2)[0].sum(-1)
    _, g_idx = lax.top_k(grp_score, TOPK_GROUPS)
    g_mask = jax.nn.one_hot(g_idx, N_GROUPS, dtype=jnp.float32).sum(1) > 0
    e_mask = jnp.repeat(g_mask, EXPERTS_PER_GROUP, axis=1)
    _, e_idx = lax.top_k(jnp.where(e_mask, sel, -jnp.inf), TOPK_EXPERTS)
    w = jnp.take_along_axis(aff, e_idx, axis=1)
    w = w / jnp.sum(w, axis=-1, keepdims=True)
    gates = jnp.einsum('nk,nke->ne', w, jax.nn.one_hot(e_idx, N_EXPERTS, dtype=jnp.float32)).astype(x.dtype)
    y = jnp.zeros_like(xt)
    for e in range(N_EXPERTS):
        h = jax.nn.silu(xt @ w_gate[e]) * (xt @ w_up[e])
        y = y + gates[:, e:e + 1] * (h @ w_down[e])
    return y.reshape(B, S, D)


def _col_scale(sizes, value_slots):
    return jnp.concatenate([jnp.full((n,), BETA if i in value_slots else 1.0, jnp.float32)
                            for i, n in enumerate(sizes)])


def setup_inputs(seed: int = 0) -> dict:
    key = jax.random.key(seed)
    ks = jax.random.split(key, 20)
    nrm = lambda k, shape, scale: jax.random.normal(k, shape, jnp.float32) * scale
    even_in, odd_in = sum(EVEN_SIZES), sum(ODD_SIZES)
    ukv_scale = jnp.tile(jnp.concatenate([jnp.ones((MLA_NOPE,), jnp.float32),
                                          jnp.full((MLA_V,), BETA, jnp.float32)]), D_HEADS)
    return {
        'x': nrm(ks[0], (BATCH, SEQ, D_MODEL), 1.0),
        'even_w_in': nrm(ks[1], (N_EVEN, D_MODEL, even_in), D_MODEL ** -0.5) * _col_scale(EVEN_SIZES, EVEN_VALUE_SLOTS),
        'even_b_f': FOX_BIAS_INIT + nrm(ks[2], (N_EVEN, B_HEADS), 0.1),
        'even_sinks': nrm(ks[3], (N_EVEN, A_HEADS), 1.0),
        'odd_w_in': nrm(ks[4], (N_ODD, D_MODEL, odd_in), D_MODEL ** -0.5) * _col_scale(ODD_SIZES, ODD_VALUE_SLOTS),
        'odd_q_norm': 1.0 + nrm(ks[5], (N_ODD, MLA_Q_RANK), 0.02),
        'odd_kv_norm': 1.0 + nrm(ks[6], (N_ODD, MLA_KV_RANK), 0.02),
        'odd_w_uq': nrm(ks[7], (N_ODD, MLA_Q_RANK, D_HEADS * (MLA_NOPE + MLA_ROPE)), MLA_Q_RANK ** -0.5),
        'odd_w_ukv': nrm(ks[8], (N_ODD, MLA_KV_RANK, D_HEADS * (MLA_NOPE + MLA_V)), MLA_KV_RANK ** -0.5) * ukv_scale,
        'w_o': nrm(ks[9], (DEPTH, MIX_WIDTH, D_MODEL), MIX_WIDTH ** -0.5 * BETA),
        'ln_g': 1.0 + nrm(ks[10], (DEPTH, 2, D_MODEL), 0.02),
        'ln_b': nrm(ks[11], (DEPTH, 2, D_MODEL), 0.02),
        'router_w': nrm(ks[12], (D_MODEL, N_EXPERTS), D_MODEL ** -0.5),
        'router_b': nrm(ks[13], (N_EXPERTS,), 0.01),
        'moe_w_gate': nrm(ks[14], (DEPTH, N_EXPERTS, D_MODEL, D_FF_EXPERT), D_MODEL ** -0.5),
        'moe_w_up': nrm(ks[15], (DEPTH, N_EXPERTS, D_MODEL, D_FF_EXPERT), D_MODEL ** -0.5 * BETA),
        'moe_w_down': nrm(ks[16], (DEPTH, N_EXPERTS, D_FF_EXPERT, D_MODEL), D_FF_EXPERT ** -0.5 * BETA),
    }


def reference(x, even_w_in, even_b_f, even_sinks, odd_w_in, odd_q_norm, odd_kv_norm, odd_w_uq,
              odd_w_ukv, w_o, ln_g, ln_b, router_w, router_b, moe_w_gate, moe_w_up, moe_w_down):
    for l in range(DEPTH):
        j = l // 2
        if l % 2 == 0:
            mix = even_mixer(x, even_w_in[j], even_b_f[j], even_sinks[j], w_o[l])
        else:
            mix = odd_mixer(x, odd_w_in[j], odd_q_norm[j], odd_kv_norm[j], odd_w_uq[j], odd_w_ukv[j], w_o[l])
        x = layer_norm(ALPHA * x + mix, ln_g[l, 0], ln_b[l, 0])
        ffn = routed_moe(x, router_w, router_b, moe_w_gate[l], moe_w_up[l], moe_w_down[l])
        x = layer_norm(ALPHA * x + ffn, ln_g[l, 1], ln_b[l, 1])
    return x
```

```python
import functools
import math

import jax
import jax.numpy as jnp
from jax import lax
from jax.experimental import pallas as pl
from jax.experimental.pallas import tpu as pltpu

F32 = jnp.float32
BF16 = jnp.bfloat16
I32 = jnp.int32

LANE = 128
HEAD_DIM = 64
BLOCK = 128
A_HEADS, A_KV_HEADS, WINDOW = 8, 2, 128
B_HEADS = 8
C_HEADS, IDX_HEADS, IDX_DIM, DSA_TOPK_MAX = 8, 4, 64, 256
D_HEADS, MLA_Q_RANK, MLA_KV_RANK, MLA_NOPE, MLA_ROPE, MLA_V = 8, 256, 128, 64, 32, 64
ROPE_THETA = 10000.0
N_EXPERTS, N_GROUPS = 16, 4
EXPERTS_PER_GROUP = N_EXPERTS // N_GROUPS
LN_EPS, RMS_EPS = 1e-5, 1e-6
NEG_INF = float("-inf")
VMEM_LIMIT = 56 * 1024 * 1024

HIGHEST = lax.Precision.HIGHEST


def _cparams(sem):
    return pltpu.CompilerParams(dimension_semantics=sem, vmem_limit_bytes=VMEM_LIMIT)


def _dot(a, b):
    return jnp.dot(a, b, preferred_element_type=F32)


def _dot_nt(a, b):
    return lax.dot_general(a, b, (((1,), (1,)), ((), ())), preferred_element_type=F32)


def _lane_lo(shape):
    return lax.broadcasted_iota(I32, shape, len(shape) - 1) % LANE < HEAD_DIM


def _layer_norm(y, g, b):
    mu = jnp.mean(y, axis=-1, keepdims=True)
    yc = y - mu
    var = jnp.mean(yc * yc, axis=-1, keepdims=True)
    return yc * lax.rsqrt(var + LN_EPS) * g + b


def _even_proj_kernel(x_ref, w_ref, bf_ref, qa_ref, ka_ref, va_ref, qb_ref, kb_ref, vb_ref,
                      cum_ref, carry_ref, *, tiles_per_seq):
    t = pl.program_id(0)
    tm = x_ref.shape[0]
    h = _dot(x_ref[...].astype(BF16), w_ref[...])
    qa_ref[...] = h[:, 0:512].astype(BF16)
    ka_ref[...] = h[:, 512:768].astype(BF16)
    va_ref[...] = h[:, 768:1024].astype(BF16)
    qb_ref[...] = h[:, 1024:1536].astype(BF16)
    kb_ref[...] = h[:, 1536:2048].astype(BF16)
    vb_ref[...] = h[:, 2048:2560].astype(BF16)
    z = h[:, 2560:2688] + bf_ref[...]
    logf = jnp.minimum(z, 0.0) - jnp.log1p(jnp.exp(-jnp.abs(z)))

    @pl.when(t % tiles_per_seq == 0)
    def _():
        carry_ref[...] = jnp.zeros_like(carry_ref)

    row = lax.broadcasted_iota(I32, (tm, tm), 0)
    col = lax.broadcasted_iota(I32, (tm, tm), 1)
    tri = jnp.where(row >= col, 1.0, 0.0).astype(F32)
    c = jnp.dot(tri, logf, precision=HIGHEST, preferred_element_type=F32) + carry_ref[...]
    cum_ref[...] = c[:, 0:B_HEADS]
    carry_ref[...] = c[tm - 1:tm, :]


def _even_proj(xt, w_cat, bf_pad, seq, tm):
    n, d = xt.shape
    wcols = w_cat.shape[1]
    widths = (512, 256, 256, 512, 512, 512)
    out_shape = [jax.ShapeDtypeStruct((n, w), BF16) for w in widths]
    out_shape.append(jax.ShapeDtypeStruct((n, B_HEADS), F32))
    out_specs = [pl.BlockSpec((tm, w), lambda i: (i, 0)) for w in widths]
    out_specs.append(pl.BlockSpec((tm, B_HEADS), lambda i: (i, 0)))
    return pl.pallas_call(
        functools.partial(_even_proj_kernel, tiles_per_seq=seq // tm),
        out_shape=out_shape,
        grid=(n // tm,),
        in_specs=[pl.BlockSpec((tm, d), lambda i: (i, 0)),
                  pl.BlockSpec((d, wcols), lambda i: (0, 0)),
                  pl.BlockSpec((1, LANE), lambda i: (0, 0))],
        out_specs=out_specs,
        scratch_shapes=[pltpu.VMEM((1, LANE), F32)],
        compiler_params=_cparams(("arbitrary",)),
        name="even_proj",
    )(xt, w_cat, bf_pad)


def _swa_kernel(sink_ref, q_ref, kp_ref, kc_ref, vp_ref, vc_ref, o_ref):
    i = pl.program_id(1)
    group = A_HEADS // A_KV_HEADS
    tl = lax.broadcasted_iota(I32, (BLOCK, 2 * BLOCK), 0)
    sl = lax.broadcasted_iota(I32, (BLOCK, 2 * BLOCK), 1)
    dist = BLOCK + tl - sl
    valid = (dist >= 0) & (dist < WINDOW) & ((sl >= BLOCK) | (i > 0))
    distf = dist.astype(F32)
    lo = _lane_lo((BLOCK, LANE))
    outs = []
    for h in range(A_HEADS):
        g = h // group
        qp = q_ref[:, (h // 2) * LANE:(h // 2 + 1) * LANE]
        qh = jnp.where(lo if h % 2 == 0 else ~lo, qp, jnp.zeros_like(qp))
        kband = jnp.concatenate([kp_ref[:, g * LANE:(g + 1) * LANE],
                                 kc_ref[:, g * LANE:(g + 1) * LANE]], axis=0)
        vband = jnp.concatenate([vp_ref[:, g * LANE:(g + 1) * LANE],
                                 vc_ref[:, g * LANE:(g + 1) * LANE]], axis=0)
        slope = 2.0 ** (-8.0 * (h + 1) / A_HEADS)
        logits = _dot_nt(qh, kband) - slope * distf
        logits = jnp.where(valid, logits, NEG_INF)
        sink = sink_ref[h]
        m = jnp.maximum(jnp.max(logits, axis=-1, keepdims=True), sink)
        e = jnp.exp(logits - m)
        denom = jnp.sum(e, axis=-1, keepdims=True) + jnp.exp(sink - m)
        outs.append(_dot(e.astype(BF16), vband) / denom)
    for j in range(A_HEADS // 2):
        o_ref[:, j * LANE:(j + 1) * LANE] = jnp.where(lo, outs[2 * j], outs[2 * j + 1]).astype(BF16)


def _swa(sinks, qa, ka, va):
    b, s, _ = qa.shape
    nb = s // BLOCK
    prev = lambda bi, i: (bi, jnp.maximum(i - 1, 0), 0)
    cur = lambda bi, i: (bi, i, 0)
    return pl.pallas_call(
        _swa_kernel,
        out_shape=jax.ShapeDtypeStruct((b, s, A_HEADS * HEAD_DIM), BF16),
        grid=(b, nb),
        in_specs=[pl.BlockSpec(memory_space=pltpu.SMEM),
                  pl.BlockSpec((None, BLOCK, 512), cur),
                  pl.BlockSpec((None, BLOCK, 256), prev),
                  pl.BlockSpec((None, BLOCK, 256), cur),
                  pl.BlockSpec((None, BLOCK, 256), prev),
                  pl.BlockSpec((None, BLOCK, 256), cur)],
        out_specs=pl.BlockSpec((None, BLOCK, 512), cur),
        compiler_params=_cparams(("parallel", "arbitrary")),
        name="swa_attn",
    )(sinks, qa, ka, ka, va, va)


def _flash_kernel(*refs, fox, tq):
    if fox:
        q_ref, k_ref, v_ref, cc_ref, cr_ref, o_ref = refs
    else:
        q_ref, k_ref, v_ref, o_ref = refs
    qi = pl.program_id(2)
    lo = _lane_lo((tq, LANE))
    row = lax.broadcasted_iota(I32, (tq, tq), 0)
    col = lax.broadcasted_iota(I32, (tq, tq), 1)
    causal = col <= row
    outs = []
    for hh in range(2):
        if fox:
            qp = q_ref[...]
            q = jnp.where(lo if hh == 0 else ~lo, qp, jnp.zeros_like(qp))
            c_t = cc_ref[:, hh:hh + 1]
        else:
            q = q_ref[:, hh * LANE:(hh + 1) * LANE]

        def logits_of(kb, masked, q=q, hh=hh):
            start = pl.multiple_of(kb * tq, tq)
            if fox:
                k = k_ref[pl.ds(start, tq), :]
            else:
                k = k_ref[pl.ds(start, tq), hh * LANE:(hh + 1) * LANE]
            s = _dot_nt(q, k)
            if fox:
                s = s + c_t - cr_ref[hh:hh + 1, pl.ds(start, tq)]
            if masked:
                s = jnp.where(causal, s, NEG_INF)
            return s, v_ref[pl.ds(start, tq), :]

        def update(carry, s, v):
            m, l, acc = carry
            m_new = jnp.maximum(m, jnp.max(s, axis=-1, keepdims=True))
            alpha = jnp.exp(m - m_new)
            p = jnp.exp(s - m_new)
            l = alpha * l + jnp.sum(p, axis=-1, keepdims=True)
            acc = alpha * acc + _dot(p.astype(BF16), v)
            return m_new, l, acc

        def body(kb, carry):
            s, v = logits_of(kb, False)
            return update(carry, s, v)

        init = (jnp.full((tq, 1), NEG_INF, F32), jnp.zeros((tq, 1), F32), jnp.zeros((tq, LANE), F32))
        carry = lax.fori_loop(0, qi, body, init)
        s, v = logits_of(qi, True)
        m, l, acc = update(carry, s, v)
        outs.append(acc / l)
    o_ref[...] = jnp.where(lo, outs[0], outs[1]).astype(BF16)


def _flash(q, k, v, cum_col=None, cum_row=None, *, tq):
    b, s, _ = v.shape
    fox = cum_col is not None
    qw = LANE if fox else 2 * LANE
    in_specs = [pl.BlockSpec((None, tq, qw), lambda bi, j, i: (bi, i, j)),
                pl.BlockSpec((None, s, qw), lambda bi, j, i: (bi, 0, j)),
                pl.BlockSpec((None, s, LANE), lambda bi, j, i: (bi, 0, j))]
    args = [q, k, v]
    if fox:
        in_specs += [pl.BlockSpec((None, None, tq, 2), lambda bi, j, i: (bi, j, i, 0)),
                     pl.BlockSpec((None, None, 2, s), lambda bi, j, i: (bi, j, 0, 0))]
        args += [cum_col, cum_row]
    return pl.pallas_call(
        functools.partial(_flash_kernel, fox=fox, tq=tq),
        out_shape=jax.ShapeDtypeStruct(v.shape, BF16),
        grid=(b, v.shape[2] // LANE, s // tq),
        in_specs=in_specs,
        out_specs=pl.BlockSpec((None, tq, LANE), lambda bi, j, i: (bi, i, j)),
        compiler_params=_cparams(("parallel", "parallel", "arbitrary")),
        name="fox_attn" if fox else "mla_attn",
    )(*args)


def _route_rows(aff, sel):
    gs = []
    for g in range(N_GROUPS):
        r = sel[g * EXPERTS_PER_GROUP:(g + 1) * EXPERTS_PER_GROUP]
        best = None
        for a in range(EXPERTS_PER_GROUP):
            for c in range(a + 1, EXPERTS_PER_GROUP):
                pair = r[a] + r[c]
                best = pair if best is None else jnp.maximum(best, pair)
        gs.append(best)
    gmax = functools.reduce(jnp.maximum, gs)
    gidx = jnp.full_like(gmax, float(N_GROUPS - 1))
    for g in range(N_GROUPS - 2, -1, -1):
        gidx = jnp.where(gs[g] == gmax, float(g), gidx)

    def pick(rows, k):
        out = rows[(N_GROUPS - 1) * EXPERTS_PER_GROUP + k]
        for g in range(N_GROUPS - 2, -1, -1):
            out = jnp.where(gidx == float(g), rows[g * EXPERTS_PER_GROUP + k], out)
        return out

    s = [pick(sel, k) for k in range(EXPERTS_PER_GROUP)]
    a = [pick(aff, k) for k in range(EXPERTS_PER_GROUP)]

    def first_max(vals):
        mx = functools.reduce(jnp.maximum, vals)
        idx = jnp.full_like(mx, float(len(vals) - 1))
        for k in range(len(vals) - 2, -1, -1):
            idx = jnp.where(vals[k] == mx, float(k), idx)
        return idx

    def take(vals, idx):
        out = vals[-1]
        for k in range(len(vals) - 2, -1, -1):
            out = jnp.where(idx == float(k), vals[k], out)
        return out

    i1 = first_max(s)
    s2 = [jnp.where(i1 == float(k), NEG_INF, s[k]) for k in range(EXPERTS_PER_GROUP)]
    i2 = first_max(s2)
    a1, a2 = take(a, i1), take(a, i2)
    tot = a1 + a2
    base = gidx * float(EXPERTS_PER_GROUP)
    return base + i1, base + i2, a1 / tot, a2 / tot


def _outproj_kernel(a_ref, b_ref, x_ref, wa_ref, wb_ref, g_ref, be_ref, rwt_ref, rb_ref,
                    x1_ref, route_ref, *, alpha):
    mix = _dot(a_ref[...], wa_ref[...]) + _dot(b_ref[...], wb_ref[...])
    x1 = _layer_norm(alpha * x_ref[...] + mix, g_ref[...], be_ref[...])
    x1_ref[...] = x1
    zt = lax.dot_general(rwt_ref[...], x1, (((1,), (1,)), ((), ())),
                         precision=HIGHEST, preferred_element_type=F32)
    afft = 1.0 / (1.0 + jnp.exp(-zt))
    selt = afft + rb_ref[...]
    aff = [afft[e:e + 1, :] for e in range(N_EXPERTS)]
    sel = [selt[e:e + 1, :] for e in range(N_EXPERTS)]
    e1, e2, w1, w2 = _route_rows(aff, sel)
    zero = jnp.zeros_like(e1)
    route_ref[...] = jnp.concatenate([e1, e2, w1, w2, zero, zero, zero, zero], axis=0)


def _outproj(a, b, xt, wa, wb, g, be, rwt, rb, alpha, tm):
    n, d = xt.shape
    full = lambda i: (0, 0)
    return pl.pallas_call(
        functools.partial(_outproj_kernel, alpha=alpha),
        out_shape=[jax.ShapeDtypeStruct((n, d), F32), jax.ShapeDtypeStruct((8, n), F32)],
        grid=(n // tm,),
        in_specs=[pl.BlockSpec((tm, a.shape[1]), lambda i: (i, 0)),
                  pl.BlockSpec((tm, b.shape[1]), lambda i: (i, 0)),
                  pl.BlockSpec((tm, d), lambda i: (i, 0)),
                  pl.BlockSpec(wa.shape, full), pl.BlockSpec(wb.shape, full),
                  pl.BlockSpec((1, d), full), pl.BlockSpec((1, d), full),
                  pl.BlockSpec(rwt.shape, full), pl.BlockSpec(rb.shape, full)],
        out_specs=[pl.BlockSpec((tm, d), lambda i: (i, 0)), pl.BlockSpec((8, tm), lambda i: (0, i))],
        compiler_params=_cparams(("parallel",)),
        name="outproj_ln_route",
    )(a, b, xt, wa, wb, g, be, rwt, rb)


def _moe_kernel(bexp_ref, bbase_ref, bend_ref, nact_ref, src_ref,
                x_hbm, wg_ref, wu_ref, wd_ref, o_ref, xbuf, sem, *, tm):
    i = pl.program_id(0)
    nact = nact_ref[0]
    last = src_ref.shape[0] - 1

    def row_copy(tok, slot, r):
        return pltpu.make_async_copy(x_hbm.at[pl.ds(tok, 1), :], xbuf.at[slot, pl.ds(r, 1), :],
                                     sem.at[slot])

    def issue(blk, slot):
        base = bbase_ref[blk]
        end = bend_ref[blk]

        def body(r, _):
            pos = jnp.minimum(base + r, end)
            row_copy(src_ref[jnp.minimum(pos, last)], slot, r).start()
            return 0

        lax.fori_loop(0, tm, body, 0, unroll=8)

    @pl.when((i == 0) & (nact > 0))
    def _():
        issue(0, 0)

    @pl.when(i + 1 < nact)
    def _():
        issue(i + 1, (i + 1) % 2)

    @pl.when(i < nact)
    def _():
        slot = i % 2
        pltpu.make_async_copy(x_hbm.at[pl.ds(0, tm), :], xbuf.at[slot], sem.at[slot]).wait()
        xb = xbuf[slot].astype(BF16)
        gate = _dot(xb, wg_ref[...])
        up = _dot(xb, wu_ref[...])
        h = gate * (1.0 / (1.0 + jnp.exp(-gate))) * up
        o_ref[...] = _dot(h.astype(BF16), wd_ref[...])

    @pl.when(i >= nact)
    def _():
        o_ref[...] = jnp.zeros_like(o_ref)


def _moe_experts(x1, wg, wu, wd, bexp, bbase, bend, nact, src, tm):
    n, d = x1.shape
    nblk = bexp.shape[0]
    f = wg.shape[2]
    wmap = lambda i, bexp, *_: (bexp[i], 0, 0)
    grid_spec = pltpu.PrefetchScalarGridSpec(
        num_scalar_prefetch=5,
        grid=(nblk,),
        in_specs=[pl.BlockSpec(memory_space=pl.ANY),
                  pl.BlockSpec((None, d, f), wmap),
                  pl.BlockSpec((None, d, f), wmap),
                  pl.BlockSpec((None, f, d), wmap)],
        out_specs=pl.BlockSpec((tm, d), lambda i, *_: (i, 0)),
        scratch_shapes=[pltpu.VMEM((2, tm, d), F32), pltpu.SemaphoreType.DMA((2,))],
    )
    return pl.pallas_call(
        functools.partial(_moe_kernel, tm=tm),
        out_shape=jax.ShapeDtypeStruct((nblk * tm, d), F32),
        grid_spec=grid_spec,
        compiler_params=_cparams(("arbitrary",)),
        name="moe_experts",
    )(bexp, bbase, bend, nact, src, x1, wg, wu, wd)


def _combine_kernel(p0_ref, p1_ref, x_ref, r_ref, g_ref, be_ref, o_hbm, y_ref, obuf, sem, *, alpha, tm):
    i = pl.program_id(0)
    nsteps = pl.num_programs(0)

    def row_copy(pos, slot, k, r):
        return pltpu.make_async_copy(o_hbm.at[pl.ds(pos, 1), :], obuf.at[slot, k, pl.ds(r, 1), :],
                                     sem.at[slot])

    def issue(blk, slot):
        base = blk * tm

        def body(r, _):
            row_copy(p0_ref[base + r], slot, 0, r).start()
            row_copy(p1_ref[base + r], slot, 1, r).start()
            return 0

        lax.fori_loop(0, tm, body, 0, unroll=8)

    @pl.when(i == 0)
    def _():
        issue(0, 0)

    @pl.when(i + 1 < nsteps)
    def _():
        issue(i + 1, (i + 1) % 2)

    slot = i % 2
    for k in range(2):
        pltpu.make_async_copy(o_hbm.at[pl.ds(0, tm), :], obuf.at[slot, k], sem.at[slot]).wait()
    r = r_ref[...]
    y = alpha * x_ref[...] + r[:, 2:3] * obuf[slot, 0] + r[:, 3:4] * obuf[slot, 1]
    y_ref[...] = _layer_norm(y, g_ref[...], be_ref[...])


def _combine(x1, route, g, be, o, p0, p1, alpha, tm):
    n, d = x1.shape
    grid_spec = pltpu.PrefetchScalarGridSpec(
        num_scalar_prefetch=2,
        grid=(n // tm,),
        in_specs=[pl.BlockSpec((tm, d), lambda i, *_: (i, 0)),
                  pl.BlockSpec((tm, 8), lambda i, *_: (i, 0)),
                  pl.BlockSpec((1, d), lambda i, *_: (0, 0)),
                  pl.BlockSpec((1, d), lambda i, *_: (0, 0)),
                  pl.BlockSpec(memory_space=pl.ANY)],
        out_specs=pl.BlockSpec((tm, d), lambda i, *_: (i, 0)),
        scratch_shapes=[pltpu.VMEM((2, 2, tm, d), F32), pltpu.SemaphoreType.DMA((2,))],
    )
    return pl.pallas_call(
        functools.partial(_combine_kernel, alpha=alpha, tm=tm),
        out_shape=jax.ShapeDtypeStruct((n, d), F32),
        grid_spec=grid_spec,
        compiler_params=_cparams(("arbitrary",)),
        name="moe_combine_ln",
    )(p0, p1, x1, route, g, be, o)


def _moe_layer(x1, route_t, wg, wu, wd, g, be, alpha, tm_e, tm_c):
    n, d = x1.shape
    route = route_t.T
    e = route[:, 0:2].astype(I32).reshape(2 * n)
    onehot = (e[:, None] == jnp.arange(N_EXPERTS, dtype=I32)[None, :]).astype(I32)
    rank = jnp.take_along_axis(jnp.cumsum(onehot, axis=0) - onehot, e[:, None], axis=1)[:, 0]
    counts = jnp.sum(onehot, axis=0)
    nblk_e = (counts + tm_e - 1) // tm_e
    blk_start = jnp.cumsum(nblk_e) - nblk_e
    seg_start = jnp.cumsum(counts) - counts
    pos = blk_start[e] * tm_e + rank
    sorted_pos = seg_start[e] + rank
    src = jnp.zeros((2 * n,), I32).at[sorted_pos].set(jnp.arange(2 * n, dtype=I32) // 2)
    nblk = (2 * n) // tm_e + N_EXPERTS
    blk = jnp.arange(nblk, dtype=I32)
    bexp = jnp.clip(jnp.searchsorted(jnp.cumsum(nblk_e), blk, side="right"), 0, N_EXPERTS - 1).astype(I32)
    bbase = (seg_start[bexp] + (blk - blk_start[bexp]) * tm_e).astype(I32)
    bend = (seg_start[bexp] + counts[bexp] - 1).astype(I32)
    nact = jnp.sum(nblk_e).astype(I32).reshape(1)
    o = _moe_experts(x1, wg, wu, wd, bexp, bbase, bend, nact, src, tm_e)
    p = pos.reshape(n, 2).astype(I32)
    return _combine(x1, route, g, be, o, p[:, 0], p[:, 1], alpha, tm_c)


def _rms(x, g):
    return x * lax.rsqrt(jnp.mean(x * x, axis=-1, keepdims=True) + RMS_EPS) * g

ODD_OFF = dict(qc=0, kc=512, vc=640, qi=768, ki=1024, wi=1152, cq=1280, ckv=1536, kra=1664, krb=1792, end=1920)


def _odd_proj_kernel(x_ref, w_ref, qng_ref, kvng_ref, wqa_ref, wqb_ref, wk_ref, wv_ref, cos_ref, sin_ref,
                     qc_ref, kc_ref, vc_ref, qi_ref, ki_ref, wi_ref, q8_ref, k8_ref, v8_ref, *, q_scale):
    o = ODD_OFF
    h = _dot(x_ref[...].astype(BF16), w_ref[...])
    qc_ref[...] = h[:, o["qc"]:o["kc"]].astype(BF16)
    kc_ref[...] = h[:, o["kc"]:o["vc"]].astype(BF16)
    vc_ref[...] = h[:, o["vc"]:o["qi"]].astype(BF16)
    qi_ref[...] = h[:, o["qi"]:o["ki"]].astype(BF16)
    ki_ref[...] = h[:, o["ki"]:o["wi"]].astype(BF16)
    wi_ref[...] = h[:, o["wi"]:o["wi"] + IDX_HEADS]
    cos = cos_ref[...]
    sin = sin_ref[...]
    cqn = _rms(h[:, o["cq"]:o["ckv"]], qng_ref[...]).astype(BF16)
    qa = _dot(cqn, wqa_ref[...])
    qb = _dot(cqn, wqb_ref[...])
    ckvn = _rms(h[:, o["ckv"]:o["kra"]], kvng_ref[...]).astype(BF16)
    kn = _dot(ckvn, wk_ref[...])
    v8_ref[...] = _dot(ckvn, wv_ref[...]).astype(BF16)
    kr = h[:, o["kra"]:o["krb"]] * cos + h[:, o["krb"]:o["end"]] * sin
    for hd in range(D_HEADS):
        sl = slice(hd * LANE, (hd + 1) * LANE)
        q8_ref[:, sl] = ((qa[:, sl] * cos + qb[:, sl] * sin) * q_scale).astype(BF16)
        k8_ref[:, sl] = (kn[:, sl] + kr).astype(BF16)


def _odd_proj(xt, w_cat, qng, kvng, wqa, wqb, wk, wv, cos_t, sin_t, seq, tm):
    n, d = xt.shape
    full = lambda i: (0, 0)
    tps = seq // tm
    widths = (512, 128, 128, 256, 128)
    out_shape = [jax.ShapeDtypeStruct((n, w), BF16) for w in widths]
    out_specs = [pl.BlockSpec((tm, w), lambda i: (i, 0)) for w in widths]
    out_shape.append(jax.ShapeDtypeStruct((n, IDX_HEADS), F32))
    out_specs.append(pl.BlockSpec((tm, IDX_HEADS), lambda i: (i, 0)))
    for w in (1024, 1024, 512):
        out_shape.append(jax.ShapeDtypeStruct((n, w), BF16))
        out_specs.append(pl.BlockSpec((tm, w), lambda i: (i, 0)))
    return pl.pallas_call(
        functools.partial(_odd_proj_kernel, q_scale=(MLA_NOPE + MLA_ROPE) ** -0.5),
        out_shape=out_shape,
        grid=(n // tm,),
        in_specs=[pl.BlockSpec((tm, d), lambda i: (i, 0)),
                  pl.BlockSpec(w_cat.shape, full),
                  pl.BlockSpec(qng.shape, full), pl.BlockSpec(kvng.shape, full),
                  pl.BlockSpec(wqa.shape, full), pl.BlockSpec(wqb.shape, full),
                  pl.BlockSpec(wk.shape, full), pl.BlockSpec(wv.shape, full),
                  pl.BlockSpec((tm, LANE), lambda i: (i % tps, 0)),
                  pl.BlockSpec((tm, LANE), lambda i: (i % tps, 0))],
        out_specs=out_specs,
        compiler_params=_cparams(("parallel",)),
        name="odd_proj",
    )(xt, w_cat, qng, kvng, wqa, wqb, wk, wv, cos_t, sin_t)


def _dsa_kernel(q_ref, k_ref, v_ref, qi_ref, ki_ref, wi_ref, o_ref, *, top_k, chunk):
    i = pl.program_id(1)
    s = k_ref.shape[0]
    t = i * BLOCK + lax.broadcasted_iota(I32, (BLOCK, s), 0)
    sp = lax.broadcasted_iota(I32, (BLOCK, s), 1)
    causal = sp <= t
    lo = _lane_lo((BLOCK, LANE))

    ki = ki_ref[...]
    w = wi_ref[...]
    score = jnp.zeros((BLOCK, s), F32)
    for h in range(IDX_HEADS):
        qp = qi_ref[:, (h // 2) * LANE:(h // 2 + 1) * LANE]
        qh = jnp.where(lo if h % 2 == 0 else ~lo, qp, jnp.zeros_like(qp))
        score = score + w[:, h:h + 1] * jnp.maximum(_dot_nt(qh, ki), 0.0)
    score = jnp.where(score == 0.0, 0.0, score)
    score = jnp.where(causal, score, NEG_INF)

    bits = pltpu.bitcast(score, I32)
    key = jnp.where(bits < 0, bits ^ jnp.int32(0x7FFFFFFF), bits)

    def count_ge(trial):
        return jnp.sum(jnp.where(key >= trial, 1.0, 0.0), axis=-1, keepdims=True)

    kf = float(top_k)
    int_min = jnp.int32(-2 ** 31)
    cand = jnp.where(count_ge(jnp.zeros((BLOCK, 1), I32)) >= kf, 0, int_min).astype(I32)

    def bit_step(j, cand):
        trial = cand + lax.shift_left(jnp.int32(1), 30 - j)
        return jnp.where(count_ge(trial) >= kf, trial, cand)

    thr = lax.fori_loop(0, 31, bit_step, cand)
    gt = key > thr
    eq = key == thr
    need = kf - jnp.sum(jnp.where(gt, 1.0, 0.0), axis=-1, keepdims=True)
    crow = lax.broadcasted_iota(I32, (chunk, chunk), 0)
    ccol = lax.broadcasted_iota(I32, (chunk, chunk), 1)
    upper = jnp.where(crow < ccol, 1.0, 0.0).astype(BF16)
    eqf = jnp.where(eq, 1.0, 0.0)
    offset = jnp.zeros((BLOCK, 1), F32)
    take = []
    for c in range(s // chunk):
        e_c = eqf[:, c * chunk:(c + 1) * chunk]
        prefix = _dot(e_c.astype(BF16), upper) + offset
        take.append(jnp.where(prefix < need, e_c, 0.0))
        offset = offset + jnp.sum(e_c, axis=-1, keepdims=True)
    take = jnp.concatenate(take, axis=1)
    selected = jnp.where(gt, 1.0, take)
    mbias = jnp.where(causal, jnp.where(selected > 0.0, 0.0, NEG_INF), NEG_INF)

    distf = (t - sp).astype(F32)
    kc = k_ref[...]
    vc = v_ref[...]
    outs = []
    for h in range(C_HEADS):
        qp = q_ref[:, (h // 2) * LANE:(h // 2 + 1) * LANE]
        qh = jnp.where(lo if h % 2 == 0 else ~lo, qp, jnp.zeros_like(qp))
        slope = 2.0 ** (-8.0 * (h + 1) / C_HEADS)
        logits = _dot_nt(qh, kc) - slope * distf + mbias
        m = jnp.max(logits, axis=-1, keepdims=True)
        e = jnp.exp(logits - m)
        outs.append(_dot(e.astype(BF16), vc) / jnp.sum(e, axis=-1, keepdims=True))
    for j in range(C_HEADS // 2):
        o_ref[:, j * LANE:(j + 1) * LANE] = jnp.where(lo, outs[2 * j], outs[2 * j + 1]).astype(BF16)


def _dsa(qc, kc, vc, qi, ki, wi):
    b, s, _ = qc.shape
    top_k = min(DSA_TOPK_MAX, s // 4)
    blk = lambda bi, i: (bi, i, 0)
    seq = lambda bi, i: (bi, 0, 0)
    return pl.pallas_call(
        functools.partial(_dsa_kernel, top_k=top_k, chunk=min(256, s)),
        out_shape=jax.ShapeDtypeStruct((b, s, C_HEADS * HEAD_DIM), BF16),
        grid=(b, s // BLOCK),
        in_specs=[pl.BlockSpec((None, BLOCK, 512), blk),
                  pl.BlockSpec((None, s, LANE), seq),
                  pl.BlockSpec((None, s, LANE), seq),
                  pl.BlockSpec((None, BLOCK, 256), blk),
                  pl.BlockSpec((None, s, LANE), seq),
                  pl.BlockSpec((None, BLOCK, IDX_HEADS), blk)],
        out_specs=pl.BlockSpec((None, BLOCK, 512), blk),
        compiler_params=_cparams(("parallel", "arbitrary")),
        name="dsa_attn",
    )(qc, kc, vc, qi, ki, wi)


def _dup64(w):
    d, c = w.shape
    w = w.reshape(d, c // HEAD_DIM, 1, HEAD_DIM)
    return jnp.broadcast_to(w, (d, c // HEAD_DIM, 2, HEAD_DIM)).reshape(d, 2 * c)


def _pad_cols(w, width):
    return jnp.pad(w, ((0, 0), (0, width - w.shape[1])))


def _split_cols(w, sizes):
    out, start = [], 0
    for n in sizes:
        out.append(w[:, start:start + n])
        start += n
    return out


def _rope_partner(w):
    half = MLA_ROPE // 2
    return jnp.concatenate([-w[..., half:], w[..., :half]], axis=-1)


def _rope_block(w_rope):
    d = w_rope.shape[0]
    return jnp.concatenate([jnp.zeros((d, MLA_NOPE), F32), w_rope,
                            jnp.zeros((d, LANE - MLA_NOPE - MLA_ROPE), F32)], axis=1)


def kernel(x, even_w_in, even_b_f, even_sinks, odd_w_in, odd_q_norm, odd_kv_norm, odd_w_uq, odd_w_ukv,
           w_o, ln_g, ln_b, router_w, router_b, moe_w_gate, moe_w_up, moe_w_down):
    b, s, d = x.shape
    n = b * s
    depth = w_o.shape[0]
    alpha = (2 * depth) ** 0.25
    tm = min(512, s)
    tq = min(256, s)
    tm_e = min(256, n)
    tm_c = min(256, n)
    att_scale = HEAD_DIM ** -0.5
    xt = x.reshape(n, d)
    rwt = router_w.T
    rb = router_b.reshape(N_EXPERTS, 1)

    for l in range(depth):
        j = l // 2
        if l % 2 == 0:
            qa, ka, va, qb, kb, vb, fg = _split_cols(
                even_w_in[j], (512, 128, 128, 512, 512, 512, B_HEADS))
            w_cat = jnp.concatenate([qa * att_scale, _dup64(ka), _dup64(va), qb * att_scale, kb, vb,
                                     _pad_cols(fg, LANE)], axis=1).astype(BF16)
            bf_pad = _pad_cols(even_b_f[j].reshape(1, B_HEADS), LANE)
            qa_, ka_, va_, qb_, kb_, vb_, cum = _even_proj(xt, w_cat, bf_pad, s, tm)
            r3 = lambda a: a.reshape(b, s, a.shape[1])
            out_a = _swa(even_sinks[j], r3(qa_), r3(ka_), r3(va_))
            cum4 = cum.reshape(b, s, B_HEADS // 2, 2)
            cum_col = cum4.transpose(0, 2, 1, 3)
            cum_row = cum4.transpose(0, 2, 3, 1)
            out_b = _flash(r3(qb_), r3(kb_), r3(vb_), cum_col, cum_row, tq=tq)
        else:
            (qc, kc, vc, qi, ki, wi, cq, ckv, kr) = _split_cols(
                odd_w_in[j], (512, 64, 64, IDX_HEADS * IDX_DIM, IDX_DIM, IDX_HEADS,
                              MLA_Q_RANK, MLA_KV_RANK, MLA_ROPE))
            w_cat = jnp.concatenate([
                qc * att_scale, _dup64(kc), _dup64(vc), qi * (IDX_DIM ** -0.5), _dup64(ki),
                _pad_cols(wi * (IDX_HEADS ** -0.5), LANE), cq, ckv,
                _rope_block(kr), _rope_block(_rope_partner(kr))], axis=1).astype(BF16)
            dq = MLA_NOPE + MLA_ROPE
            wuq = odd_w_uq[j].reshape(MLA_Q_RANK, D_HEADS, dq)
            pad_q = jnp.zeros((MLA_Q_RANK, D_HEADS, LANE - dq), F32)
            wqa = jnp.concatenate([wuq, pad_q], axis=2).reshape(MLA_Q_RANK, D_HEADS * LANE).astype(BF16)
            wqb = jnp.concatenate([jnp.zeros((MLA_Q_RANK, D_HEADS, MLA_NOPE), F32),
                                   _rope_partner(wuq[..., MLA_NOPE:]), pad_q],
                                  axis=2).reshape(MLA_Q_RANK, D_HEADS * LANE).astype(BF16)
            wukv = odd_w_ukv[j].reshape(MLA_KV_RANK, D_HEADS, MLA_NOPE + MLA_V)
            wk = jnp.concatenate([wukv[..., :MLA_NOPE], jnp.zeros((MLA_KV_RANK, D_HEADS, LANE - MLA_NOPE), F32)],
                                 axis=2).reshape(MLA_KV_RANK, D_HEADS * LANE).astype(BF16)
            wv = wukv[..., MLA_NOPE:].reshape(MLA_KV_RANK, D_HEADS * MLA_V).astype(BF16)
            half = MLA_ROPE // 2
            inv = ROPE_THETA ** (-jnp.arange(half, dtype=F32) / half)
            ang = jnp.arange(s, dtype=F32)[:, None] * inv[None, :]
            cos_h, sin_h = jnp.cos(ang), jnp.sin(ang)
            pad_t = jnp.zeros((s, LANE - dq), F32)
            cos_t = jnp.concatenate([jnp.ones((s, MLA_NOPE), F32), cos_h, cos_h, pad_t], axis=1)
            sin_t = jnp.concatenate([jnp.zeros((s, MLA_NOPE), F32), sin_h, sin_h, pad_t], axis=1)
            qc_, kc_, vc_, qi_, ki_, wi_, q8, k8, v8 = _odd_proj(
                xt, w_cat, odd_q_norm[j].reshape(1, -1), odd_kv_norm[j].reshape(1, -1),
                wqa, wqb, wk, wv, cos_t, sin_t, s, tm)
            r3 = lambda a: a.reshape(b, s, a.shape[1])
            out_a = _dsa(r3(qc_), r3(kc_), r3(vc_), r3(qi_), r3(ki_), r3(wi_))
            out_b = _flash(r3(q8), r3(k8), r3(v8), tq=tq)
        half_w = out_a.shape[2]
        wo = w_o[l].astype(BF16)
        x1, route_t = _outproj(out_a.reshape(n, half_w), out_b.reshape(n, -1), xt,
                               wo[:half_w], wo[half_w:], ln_g[l, 0].reshape(1, d), ln_b[l, 0].reshape(1, d),
                               rwt, rb, alpha, tm)
        xt = _moe_layer(x1, route_t, moe_w_gate[l].astype(BF16), moe_w_up[l].astype(BF16),
                        moe_w_down[l].astype(BF16), ln_g[l, 1].reshape(1, d), ln_b[l, 1].reshape(1, d),
                        alpha, tm_e, tm_c)
    return xt.reshape(b, s, d)
```

```python
import functools
import math

import jax
import jax.numpy as jnp
from jax import lax
from jax.experimental import pallas as pl
from jax.experimental.pallas import tpu as pltpu

F32 = jnp.float32
BF16 = jnp.bfloat16
I32 = jnp.int32

LANE = 128
HEAD_DIM = 64
BLOCK = 128
A_HEADS, A_KV_HEADS, WINDOW = 8, 2, 128
B_HEADS = 8
C_HEADS, IDX_HEADS, IDX_DIM, DSA_TOPK_MAX = 8, 4, 64, 256
D_HEADS, MLA_Q_RANK, MLA_KV_RANK, MLA_NOPE, MLA_ROPE, MLA_V = 8, 256, 128, 64, 32, 64
ROPE_THETA = 10000.0
N_EXPERTS, N_GROUPS = 16, 4
EXPERTS_PER_GROUP = N_EXPERTS // N_GROUPS
LN_EPS, RMS_EPS = 1e-5, 1e-6
NEG_INF = float("-inf")
LOG2E = math.log2(math.e)
VMEM_LIMIT = 56 * 1024 * 1024

HIGHEST = lax.Precision.HIGHEST


def _cparams(sem):
    return pltpu.CompilerParams(dimension_semantics=sem, vmem_limit_bytes=VMEM_LIMIT)


def _dot(a, b):
    return jnp.dot(a, b, preferred_element_type=F32)


def _dot_nt(a, b):
    return lax.dot_general(a, b, (((1,), (1,)), ((), ())), preferred_element_type=F32)


def _lane_lo(shape):
    return lax.broadcasted_iota(I32, shape, len(shape) - 1) % LANE < HEAD_DIM


def _layer_norm(y, g, b):
    mu = jnp.mean(y, axis=-1, keepdims=True)
    yc = y - mu
    var = jnp.mean(yc * yc, axis=-1, keepdims=True)
    return yc * lax.rsqrt(var + LN_EPS) * g + b


def _split3(x):
    def top(v):
        bits = lax.bitcast_convert_type(v, I32) & jnp.int32(-65536)
        return lax.bitcast_convert_type(bits, F32)

    s1 = top(x)
    r = x - s1
    s2 = top(r)
    return s1, s2, r - s2


def _aug_rel(shape, h):
    lane = lax.broadcasted_iota(I32, shape, 1)
    return lane - (HEAD_DIM if h % 2 == 0 else 0)


def _even_proj_kernel(x_ref, w_ref, bf_ref, qa_ref, ka_ref, va_ref, qf_ref, kf_ref, vf_ref,
                      carry_ref, *, tiles_per_seq):
    t = pl.program_id(0)
    tm = x_ref.shape[0]
    h = _dot(x_ref[...].astype(BF16), w_ref[...])
    qa_ref[...] = h[:, 0:512].astype(BF16)
    ka_ref[...] = h[:, 512:768].astype(BF16)
    va_ref[...] = h[:, 768:1024].astype(BF16)
    z = h[:, 2560:2688] + bf_ref[...]
    logf = jnp.minimum(z, 0.0) - jnp.log1p(jnp.exp(-jnp.abs(z)))

    @pl.when(t % tiles_per_seq == 0)
    def _():
        carry_ref[...] = jnp.zeros_like(carry_ref)

    row = lax.broadcasted_iota(I32, (tm, tm), 0)
    col = lax.broadcasted_iota(I32, (tm, tm), 1)
    tri = jnp.where(row >= col, 1.0, 0.0).astype(F32)
    c = jnp.dot(tri, logf, precision=HIGHEST, preferred_element_type=F32) + carry_ref[...]
    carry_ref[...] = c[tm - 1:tm, :]

    s1, s2, s3 = _split3(c * LOG2E)
    lo = _lane_lo((tm, LANE))
    for hd in range(B_HEADS):
        own = lo if hd % 2 == 0 else ~lo
        rel = _aug_rel((tm, LANE), hd)
        b1, b2, b3 = s1[:, hd:hd + 1], s2[:, hd:hd + 1], s3[:, hd:hd + 1]
        q_aug = jnp.where(rel == 3, b1, jnp.where(rel == 4, b2, jnp.where(rel == 5, b3,
                          jnp.where((rel >= 0) & (rel < 3), -1.0, 0.0))))
        k_aug = jnp.where(rel == 0, b1, jnp.where(rel == 1, b2, jnp.where(rel == 2, b3,
                          jnp.where((rel >= 3) & (rel < 6), 1.0, 0.0))))
        pj = (hd // 2) * LANE
        sl = slice(hd * LANE, (hd + 1) * LANE)
        qf_ref[:, sl] = jnp.where(own, h[:, 1024 + pj:1024 + pj + LANE], q_aug).astype(BF16)
        kf_ref[:, sl] = jnp.where(own, h[:, 1536 + pj:1536 + pj + LANE], k_aug).astype(BF16)
        vf_ref[:, sl] = jnp.where(own, h[:, 2048 + pj:2048 + pj + LANE], 1.0).astype(BF16)


def _even_proj(xt, w_cat, bf_pad, seq, tm):
    n, d = xt.shape
    wcols = w_cat.shape[1]
    widths = (512, 256, 256, 1024, 1024, 1024)
    out_shape = [jax.ShapeDtypeStruct((n, w), BF16) for w in widths]
    out_specs = [pl.BlockSpec((tm, w), lambda i: (i, 0)) for w in widths]
    return pl.pallas_call(
        functools.partial(_even_proj_kernel, tiles_per_seq=seq // tm),
        out_shape=out_shape,
        grid=(n // tm,),
        in_specs=[pl.BlockSpec((tm, d), lambda i: (i, 0)),
                  pl.BlockSpec((d, wcols), lambda i: (0, 0)),
                  pl.BlockSpec((1, LANE), lambda i: (0, 0))],
        out_specs=out_specs,
        scratch_shapes=[pltpu.VMEM((1, LANE), F32)],
        compiler_params=_cparams(("arbitrary",)),
        name="even_proj",
    )(xt, w_cat, bf_pad)


def _swa_kernel(sink_ref, q_ref, kp_ref, kc_ref, vp_ref, vc_ref, o_ref):
    i = pl.program_id(1)
    group = A_HEADS // A_KV_HEADS
    tl = lax.broadcasted_iota(I32, (BLOCK, 2 * BLOCK), 0)
    sl = lax.broadcasted_iota(I32, (BLOCK, 2 * BLOCK), 1)
    dist = BLOCK + tl - sl
    valid = (dist >= 0) & (dist < WINDOW) & ((sl >= BLOCK) | (i > 0))
    distf = dist.astype(F32)
    lo = _lane_lo((BLOCK, LANE))
    outs = []
    for h in range(A_HEADS):
        g = h // group
        qp = q_ref[:, (h // 2) * LANE:(h // 2 + 1) * LANE]
        qh = jnp.where(lo if h % 2 == 0 else ~lo, qp, jnp.zeros_like(qp))
        kband = jnp.concatenate([kp_ref[:, g * LANE:(g + 1) * LANE],
                                 kc_ref[:, g * LANE:(g + 1) * LANE]], axis=0)
        vband = jnp.concatenate([vp_ref[:, g * LANE:(g + 1) * LANE],
                                 vc_ref[:, g * LANE:(g + 1) * LANE]], axis=0)
        slope = 2.0 ** (-8.0 * (h + 1) / A_HEADS)
        logits = _dot_nt(qh, kband) - slope * distf
        logits = jnp.where(valid, logits, NEG_INF)
        sink = sink_ref[h]
        m = jnp.maximum(jnp.max(logits, axis=-1, keepdims=True), sink)
        e = jnp.exp(logits - m)
        denom = jnp.sum(e, axis=-1, keepdims=True) + jnp.exp(sink - m)
        outs.append(_dot(e.astype(BF16), vband) / denom)
    for j in range(A_HEADS // 2):
        o_ref[:, j * LANE:(j + 1) * LANE] = jnp.where(lo, outs[2 * j], outs[2 * j + 1]).astype(BF16)


def _swa(sinks, qa, ka, va):
    b, s, _ = qa.shape
    nb = s // BLOCK
    prev = lambda bi, i: (bi, jnp.maximum(i - 1, 0), 0)
    cur = lambda bi, i: (bi, i, 0)
    return pl.pallas_call(
        _swa_kernel,
        out_shape=jax.ShapeDtypeStruct((b, s, A_HEADS * HEAD_DIM), BF16),
        grid=(b, nb),
        in_specs=[pl.BlockSpec(memory_space=pltpu.SMEM),
                  pl.BlockSpec((None, BLOCK, 512), cur),
                  pl.BlockSpec((None, BLOCK, 256), prev),
                  pl.BlockSpec((None, BLOCK, 256), cur),
                  pl.BlockSpec((None, BLOCK, 256), prev),
                  pl.BlockSpec((None, BLOCK, 256), cur)],
        out_specs=pl.BlockSpec((None, BLOCK, 512), cur),
        compiler_params=_cparams(("parallel", "arbitrary")),
        name="swa_attn",
    )(sinks, qa, ka, ka, va, va)


def _online_softmax_step(q, k, v, m, acc, mask=None):
    s = _dot_nt(q, k)
    if mask is not None:
        s = jnp.where(mask, s, NEG_INF)
    m_new = jnp.maximum(m, jnp.max(s, axis=-1, keepdims=True))
    p = jnp.exp2(s - m_new).astype(BF16)
    acc = jnp.exp2(m - m_new) * acc + _dot(p, v)
    return m_new, acc


def _normalize_pairs(accs, o_ref):
    lo = _lane_lo(accs[0].shape)
    outs = [a / pltpu.roll(a, HEAD_DIM, 1) for a in accs]
    for j in range(len(accs) // 2):
        o_ref[:, j * LANE:(j + 1) * LANE] = jnp.where(lo, outs[2 * j], outs[2 * j + 1]).astype(BF16)


def _flash_kernel(q_ref, k_ref, v_ref, o_ref, *, tq, nh):
    qi = pl.program_id(2)
    row = lax.broadcasted_iota(I32, (tq, tq), 0)
    col = lax.broadcasted_iota(I32, (tq, tq), 1)
    causal = col <= row
    qs = [q_ref[:, hh * LANE:(hh + 1) * LANE] for hh in range(nh)]

    def step(kb, carry, mask):
        start = pl.multiple_of(kb * tq, tq)
        out = []
        for hh in range(nh):
            sl = slice(hh * LANE, (hh + 1) * LANE)
            out.append(_online_softmax_step(qs[hh], k_ref[pl.ds(start, tq), sl], v_ref[pl.ds(start, tq), sl],
                                            carry[hh][0], carry[hh][1], mask))
        return tuple(out)

    init = tuple((jnp.full((tq, 1), NEG_INF, F32), jnp.zeros((tq, LANE), F32)) for _ in range(nh))
    carry = lax.fori_loop(0, qi, lambda kb, c: step(kb, c, None), init)
    carry = step(qi, carry, causal)
    _normalize_pairs([c[1] for c in carry], o_ref)


def _flash(q, k, v, *, tq, nh, name):
    b, s, w = v.shape
    return pl.pallas_call(
        functools.partial(_flash_kernel, tq=tq, nh=nh),
        out_shape=jax.ShapeDtypeStruct((b, s, w // 2), BF16),
        grid=(b, w // (nh * LANE), s // tq),
        in_specs=[pl.BlockSpec((None, tq, nh * LANE), lambda bi, j, i: (bi, i, j)),
                  pl.BlockSpec((None, s, nh * LANE), lambda bi, j, i: (bi, 0, j)),
                  pl.BlockSpec((None, s, nh * LANE), lambda bi, j, i: (bi, 0, j))],
        out_specs=pl.BlockSpec((None, tq, nh * LANE // 2), lambda bi, j, i: (bi, i, j)),
        compiler_params=_cparams(("parallel", "parallel", "arbitrary")),
        name=name,
    )(q, k, v)


def _route_rows(aff, sel):
    gs = []
    for g in range(N_GROUPS):
        r = sel[g * EXPERTS_PER_GROUP:(g + 1) * EXPERTS_PER_GROUP]
        best = None
        for a in range(EXPERTS_PER_GROUP):
            for c in range(a + 1, EXPERTS_PER_GROUP):
                pair = r[a] + r[c]
                best = pair if best is None else jnp.maximum(best, pair)
        gs.append(best)
    gmax = functools.reduce(jnp.maximum, gs)
    gidx = jnp.full_like(gmax, float(N_GROUPS - 1))
    for g in range(N_GROUPS - 2, -1, -1):
        gidx = jnp.where(gs[g] == gmax, float(g), gidx)

    def pick(rows, k):
        out = rows[(N_GROUPS - 1) * EXPERTS_PER_GROUP + k]
        for g in range(N_GROUPS - 2, -1, -1):
            out = jnp.where(gidx == float(g), rows[g * EXPERTS_PER_GROUP + k], out)
        return out

    s = [pick(sel, k) for k in range(EXPERTS_PER_GROUP)]
    a = [pick(aff, k) for k in range(EXPERTS_PER_GROUP)]

    def first_max(vals):
        mx = functools.reduce(jnp.maximum, vals)
        idx = jnp.full_like(mx, float(len(vals) - 1))
        for k in range(len(vals) - 2, -1, -1):
            idx = jnp.where(vals[k] == mx, float(k), idx)
        return idx

    def take(vals, idx):
        out = vals[-1]
        for k in range(len(vals) - 2, -1, -1):
            out = jnp.where(idx == float(k), vals[k], out)
        return out

    i1 = first_max(s)
    s2 = [jnp.where(i1 == float(k), NEG_INF, s[k]) for k in range(EXPERTS_PER_GROUP)]
    i2 = first_max(s2)
    a1, a2 = take(a, i1), take(a, i2)
    tot = a1 + a2
    base = gidx * float(EXPERTS_PER_GROUP)
    return base + i1, base + i2, a1 / tot, a2 / tot


def _outproj_kernel(a_ref, b_ref, x_ref, wa_ref, wb_ref, g_ref, be_ref, rwt_ref, rb_ref,
                    x1_ref, route_ref, *, alpha):
    mix = _dot(a_ref[...], wa_ref[...]) + _dot(b_ref[...], wb_ref[...])
    x1 = _layer_norm(alpha * x_ref[...] + mix, g_ref[...], be_ref[...])
    x1_ref[...] = x1
    zt = lax.dot_general(rwt_ref[...], x1, (((1,), (1,)), ((), ())),
                         precision=HIGHEST, preferred_element_type=F32)
    afft = 1.0 / (1.0 + jnp.exp(-zt))
    selt = afft + rb_ref[...]
    aff = [afft[e:e + 1, :] for e in range(N_EXPERTS)]
    sel = [selt[e:e + 1, :] for e in range(N_EXPERTS)]
    e1, e2, w1, w2 = _route_rows(aff, sel)
    zero = jnp.zeros_like(e1)
    route_ref[...] = jnp.concatenate([e1, e2, w1, w2, zero, zero, zero, zero], axis=0)


def _outproj(a, b, xt, wa, wb, g, be, rwt, rb, alpha, tm):
    n, d = xt.shape
    full = lambda i: (0, 0)
    return pl.pallas_call(
        functools.partial(_outproj_kernel, alpha=alpha),
        out_shape=[jax.ShapeDtypeStruct((n, d), F32), jax.ShapeDtypeStruct((8, n), F32)],
        grid=(n // tm,),
        in_specs=[pl.BlockSpec((tm, a.shape[1]), lambda i: (i, 0)),
                  pl.BlockSpec((tm, b.shape[1]), lambda i: (i, 0)),
                  pl.BlockSpec((tm, d), lambda i: (i, 0)),
                  pl.BlockSpec(wa.shape, full), pl.BlockSpec(wb.shape, full),
                  pl.BlockSpec((1, d), full), pl.BlockSpec((1, d), full),
                  pl.BlockSpec(rwt.shape, full), pl.BlockSpec(rb.shape, full)],
        out_specs=[pl.BlockSpec((tm, d), lambda i: (i, 0)), pl.BlockSpec((8, tm), lambda i: (0, i))],
        compiler_params=_cparams(("parallel",)),
        name="outproj_ln_route",
    )(a, b, xt, wa, wb, g, be, rwt, rb)


def _moe_kernel(bexp_ref, bbase_ref, bend_ref, nact_ref, src_ref,
                x_hbm, wg_ref, wu_ref, wd_ref, o_ref, xbuf, sem, *, tm):
    i = pl.program_id(0)
    nact = nact_ref[0]
    last = src_ref.shape[0] - 1

    def row_copy(tok, slot, r):
        return pltpu.make_async_copy(x_hbm.at[pl.ds(tok, 1), :], xbuf.at[slot, pl.ds(r, 1), :],
                                     sem.at[slot])

    def issue(blk, slot):
        base = bbase_ref[blk]
        end = bend_ref[blk]

        def body(r, _):
            pos = jnp.minimum(base + r, end)
            row_copy(src_ref[jnp.minimum(pos, last)], slot, r).start()
            return 0

        lax.fori_loop(0, tm, body, 0, unroll=8)

    @pl.when((i == 0) & (nact > 0))
    def _():
        issue(0, 0)

    @pl.when(i + 1 < nact)
    def _():
        issue(i + 1, (i + 1) % 2)

    @pl.when(i < nact)
    def _():
        slot = i % 2
        pltpu.make_async_copy(x_hbm.at[pl.ds(0, tm), :], xbuf.at[slot], sem.at[slot]).wait()
        xb = xbuf[slot].astype(BF16)
        gate = _dot(xb, wg_ref[...])
        up = _dot(xb, wu_ref[...])
        h = gate * (1.0 / (1.0 + jnp.exp(-gate))) * up
        o_ref[...] = _dot(h.astype(BF16), wd_ref[...])

    @pl.when(i >= nact)
    def _():
        o_ref[...] = jnp.zeros_like(o_ref)


def _moe_experts(x1, wg, wu, wd, bexp, bbase, bend, nact, src, tm):
    n, d = x1.shape
    nblk = bexp.shape[0]
    f = wg.shape[2]
    wmap = lambda i, bexp, *_: (bexp[i], 0, 0)
    grid_spec = pltpu.PrefetchScalarGridSpec(
        num_scalar_prefetch=5,
        grid=(nblk,),
        in_specs=[pl.BlockSpec(memory_space=pl.ANY),
                  pl.BlockSpec((None, d, f), wmap),
                  pl.BlockSpec((None, d, f), wmap),
                  pl.BlockSpec((None, f, d), wmap)],
        out_specs=pl.BlockSpec((tm, d), lambda i, *_: (i, 0)),
        scratch_shapes=[pltpu.VMEM((2, tm, d), F32), pltpu.SemaphoreType.DMA((2,))],
    )
    return pl.pallas_call(
        functools.partial(_moe_kernel, tm=tm),
        out_shape=jax.ShapeDtypeStruct((nblk * tm, d), F32),
        grid_spec=grid_spec,
        compiler_params=_cparams(("arbitrary",)),
        name="moe_experts",
    )(bexp, bbase, bend, nact, src, x1, wg, wu, wd)


def _combine_kernel(p0_ref, p1_ref, x_ref, r_ref, g_ref, be_ref, o_hbm, y_ref, obuf, sem, *, alpha, tm):
    i = pl.program_id(0)
    nsteps = pl.num_programs(0)

    def row_copy(pos, slot, k, r):
        return pltpu.make_async_copy(o_hbm.at[pl.ds(pos, 1), :], obuf.at[slot, k, pl.ds(r, 1), :],
                                     sem.at[slot])

    def issue(blk, slot):
        base = blk * tm

        def body(r, _):
            row_copy(p0_ref[base + r], slot, 0, r).start()
            row_copy(p1_ref[base + r], slot, 1, r).start()
            return 0

        lax.fori_loop(0, tm, body, 0, unroll=8)

    @pl.when(i == 0)
    def _():
        issue(0, 0)

    @pl.when(i + 1 < nsteps)
    def _():
        issue(i + 1, (i + 1) % 2)

    slot = i % 2
    for k in range(2):
        pltpu.make_async_copy(o_hbm.at[pl.ds(0, tm), :], obuf.at[slot, k], sem.at[slot]).wait()
    r = r_ref[...]
    y = alpha * x_ref[...] + r[:, 2:3] * obuf[slot, 0] + r[:, 3:4] * obuf[slot, 1]
    y_ref[...] = _layer_norm(y, g_ref[...], be_ref[...])


def _combine(x1, route, g, be, o, p0, p1, alpha, tm):
    n, d = x1.shape
    grid_spec = pltpu.PrefetchScalarGridSpec(
        num_scalar_prefetch=2,
        grid=(n // tm,),
        in_specs=[pl.BlockSpec((tm, d), lambda i, *_: (i, 0)),
                  pl.BlockSpec((tm, 8), lambda i, *_: (i, 0)),
                  pl.BlockSpec((1, d), lambda i, *_: (0, 0)),
                  pl.BlockSpec((1, d), lambda i, *_: (0, 0)),
                  pl.BlockSpec(memory_space=pl.ANY)],
        out_specs=pl.BlockSpec((tm, d), lambda i, *_: (i, 0)),
        scratch_shapes=[pltpu.VMEM((2, 2, tm, d), F32), pltpu.SemaphoreType.DMA((2,))],
    )
    return pl.pallas_call(
        functools.partial(_combine_kernel, alpha=alpha, tm=tm),
        out_shape=jax.ShapeDtypeStruct((n, d), F32),
        grid_spec=grid_spec,
        compiler_params=_cparams(("arbitrary",)),
        name="moe_combine_ln",
    )(p0, p1, x1, route, g, be, o)


def _moe_layer(x1, route_t, wg, wu, wd, g, be, alpha, tm_e, tm_c):
    n, d = x1.shape
    route = route_t.T
    e = route[:, 0:2].astype(I32).reshape(2 * n)
    onehot = (e[:, None] == jnp.arange(N_EXPERTS, dtype=I32)[None, :]).astype(I32)
    rank = jnp.take_along_axis(jnp.cumsum(onehot, axis=0) - onehot, e[:, None], axis=1)[:, 0]
    counts = jnp.sum(onehot, axis=0)
    nblk_e = (counts + tm_e - 1) // tm_e
    blk_start = jnp.cumsum(nblk_e) - nblk_e
    seg_start = jnp.cumsum(counts) - counts
    pos = blk_start[e] * tm_e + rank
    sorted_pos = seg_start[e] + rank
    src = jnp.zeros((2 * n,), I32).at[sorted_pos].set(jnp.arange(2 * n, dtype=I32) // 2)
    nblk = (2 * n) // tm_e + N_EXPERTS
    blk = jnp.arange(nblk, dtype=I32)
    bexp = jnp.clip(jnp.searchsorted(jnp.cumsum(nblk_e), blk, side="right"), 0, N_EXPERTS - 1).astype(I32)
    bbase = (seg_start[bexp] + (blk - blk_start[bexp]) * tm_e).astype(I32)
    bend = (seg_start[bexp] + counts[bexp] - 1).astype(I32)
    nact = jnp.sum(nblk_e).astype(I32).reshape(1)
    o = _moe_experts(x1, wg, wu, wd, bexp, bbase, bend, nact, src, tm_e)
    p = pos.reshape(n, 2).astype(I32)
    return _combine(x1, route, g, be, o, p[:, 0], p[:, 1], alpha, tm_c)


def _rms(x, g):
    return x * lax.rsqrt(jnp.mean(x * x, axis=-1, keepdims=True) + RMS_EPS) * g

ODD_OFF = dict(qc=0, kc=512, vc=640, qi=768, ki=1024, wi=1152, cq=1280, ckv=1536, kra=1664, krb=1792, end=1920)


def _odd_proj_kernel(x_ref, w_ref, qng_ref, kvng_ref, wqa_ref, wqb_ref, wk_ref, wv_ref, cos_ref, sin_ref,
                     tq_ref, tk_ref,
                     qc_ref, kc_ref, vc_ref, qi_ref, ki_ref, wi_ref, q8_ref, k8_ref, v8_ref, *, q_scale):
    o = ODD_OFF
    tm = x_ref.shape[0]
    lo = _lane_lo((tm, LANE))
    h = _dot(x_ref[...].astype(BF16), w_ref[...])
    kdup = h[:, o["kc"]:o["vc"]]
    vdup = h[:, o["vc"]:o["qi"]]
    kc_ref[:, 0:LANE] = jnp.where(lo, kdup, tk_ref[:, 0:LANE].astype(F32)).astype(BF16)
    kc_ref[:, LANE:2 * LANE] = jnp.where(lo, tk_ref[:, LANE:2 * LANE].astype(F32), kdup).astype(BF16)
    vc_ref[:, 0:LANE] = jnp.where(lo, vdup, 1.0).astype(BF16)
    vc_ref[:, LANE:2 * LANE] = jnp.where(lo, 1.0, vdup).astype(BF16)
    qi_ref[...] = h[:, o["qi"]:o["ki"]].astype(BF16)
    ki_ref[...] = h[:, o["ki"]:o["wi"]].astype(BF16)
    wi_ref[...] = h[:, o["wi"]:o["wi"] + IDX_HEADS]
    cos = cos_ref[...]
    sin = sin_ref[...]
    cqn = _rms(h[:, o["cq"]:o["ckv"]], qng_ref[...]).astype(BF16)
    qa = _dot(cqn, wqa_ref[...])
    qb = _dot(cqn, wqb_ref[...])
    ckvn = _rms(h[:, o["ckv"]:o["kra"]], kvng_ref[...]).astype(BF16)
    kn = _dot(ckvn, wk_ref[...])
    v8 = _dot(ckvn, wv_ref[...])
    kr = h[:, o["kra"]:o["krb"]] * cos + h[:, o["krb"]:o["end"]] * sin
    for hd in range(D_HEADS):
        sl = slice(hd * LANE, (hd + 1) * LANE)
        pj = slice((hd // 2) * LANE, (hd // 2 + 1) * LANE)
        own = lo if hd % 2 == 0 else ~lo
        qc_ref[:, sl] = jnp.where(own, h[:, pj], tq_ref[:, sl].astype(F32)).astype(BF16)
        q8_ref[:, sl] = ((qa[:, sl] * cos + qb[:, sl] * sin) * q_scale).astype(BF16)
        k8_ref[:, sl] = (kn[:, sl] + kr).astype(BF16)
        v8_ref[:, sl] = jnp.where(own, v8[:, pj], 1.0).astype(BF16)


def _odd_proj(xt, w_cat, qng, kvng, wqa, wqb, wk, wv, cos_t, sin_t, tq_t, tk_t, seq, tm):
    n, d = xt.shape
    full = lambda i: (0, 0)
    tps = seq // tm
    pos = lambda i: (i % tps, 0)
    widths = (1024, 256, 256, 256, 128)
    out_shape = [jax.ShapeDtypeStruct((n, w), BF16) for w in widths]
    out_specs = [pl.BlockSpec((tm, w), lambda i: (i, 0)) for w in widths]
    out_shape.append(jax.ShapeDtypeStruct((n, IDX_HEADS), F32))
    out_specs.append(pl.BlockSpec((tm, IDX_HEADS), lambda i: (i, 0)))
    for w in (1024, 1024, 1024):
        out_shape.append(jax.ShapeDtypeStruct((n, w), BF16))
        out_specs.append(pl.BlockSpec((tm, w), lambda i: (i, 0)))
    return pl.pallas_call(
        functools.partial(_odd_proj_kernel, q_scale=(MLA_NOPE + MLA_ROPE) ** -0.5 * LOG2E),
        out_shape=out_shape,
        grid=(n // tm,),
        in_specs=[pl.BlockSpec((tm, d), lambda i: (i, 0)),
                  pl.BlockSpec(w_cat.shape, full),
                  pl.BlockSpec(qng.shape, full), pl.BlockSpec(kvng.shape, full),
                  pl.BlockSpec(wqa.shape, full), pl.BlockSpec(wqb.shape, full),
                  pl.BlockSpec(wk.shape, full), pl.BlockSpec(wv.shape, full),
                  pl.BlockSpec((tm, LANE), pos), pl.BlockSpec((tm, LANE), pos),
                  pl.BlockSpec((tm, C_HEADS * LANE), pos), pl.BlockSpec((tm, 2 * LANE), pos)],
        out_specs=out_specs,
        compiler_params=_cparams(("parallel",)),
        name="odd_proj",
    )(xt, w_cat, qng, kvng, wqa, wqb, wk, wv, cos_t, sin_t, tq_t, tk_t)


def _dsa_kernel(q_ref, k_ref, v_ref, qi_ref, ki_ref, wt_ref, o_ref, key_ref, mb_ref, *, top_k, chunk):
    i = pl.program_id(1)
    nc = (i * BLOCK + BLOCK + chunk - 1) // chunk
    lo = _lane_lo((BLOCK, LANE))
    s_row = lax.broadcasted_iota(I32, (chunk, BLOCK), 0)
    t_lane = i * BLOCK + lax.broadcasted_iota(I32, (chunk, BLOCK), 1)

    def chunk_start(c):
        return pl.multiple_of(c * chunk, chunk)

    qidx = []
    for h in range(IDX_HEADS):
        qp = qi_ref[:, (h // 2) * LANE:(h // 2 + 1) * LANE]
        qidx.append(jnp.where(lo if h % 2 == 0 else ~lo, qp, jnp.zeros_like(qp)))
    q_stack = jnp.concatenate(qidx, axis=0)
    w = wt_ref[...]

    def score_chunk(c, _):
        start = chunk_start(c)
        d = _dot_nt(ki_ref[pl.ds(start, chunk), :], q_stack)
        sc = jnp.zeros((chunk, BLOCK), F32)
        for h in range(IDX_HEADS):
            sc = sc + w[h:h + 1, :] * jnp.maximum(d[:, h * BLOCK:(h + 1) * BLOCK], 0.0)
        sc = jnp.where(sc == 0.0, 0.0, sc)
        sc = jnp.where(start + s_row <= t_lane, sc, NEG_INF)
        bits = pltpu.bitcast(sc, I32)
        key_ref[pl.ds(start, chunk), :] = jnp.where(bits < 0, bits ^ jnp.int32(0x7FFFFFFF), bits)
        return 0

    lax.fori_loop(0, nc, score_chunk, 0)

    def count(trial, strict):
        def body(c, acc):
            key = key_ref[pl.ds(chunk_start(c), chunk), :]
            hit = (key > trial) if strict else (key >= trial)
            return acc + jnp.sum(jnp.where(hit, 1.0, 0.0).reshape(chunk // 8, 8, BLOCK), axis=0)

        acc = lax.fori_loop(0, nc, body, jnp.zeros((8, BLOCK), F32))
        return jnp.sum(acc, axis=0, keepdims=True)

    kf = float(top_k)
    cand = jnp.where(count(jnp.zeros((1, BLOCK), I32), False) >= kf, 0, jnp.int32(-2 ** 31)).astype(I32)

    def bit_step(j, cand):
        trial = cand + lax.shift_left(jnp.int32(1), 30 - j)
        return jnp.where(count(trial, False) >= kf, trial, cand)

    thr = lax.fori_loop(0, 31, bit_step, cand)
    need = kf - count(thr, True)

    r0 = lax.broadcasted_iota(I32, (chunk, chunk), 0)
    r1 = lax.broadcasted_iota(I32, (chunk, chunk), 1)
    below = jnp.where(r1 < r0, 1.0, 0.0).astype(BF16)

    def mask_chunk(c, offset):
        start = chunk_start(c)
        key = key_ref[pl.ds(start, chunk), :]
        eqf = jnp.where(key == thr, 1.0, 0.0)
        prefix = _dot(below, eqf.astype(BF16)) + offset
        sel = jnp.where(key > thr, 1.0, jnp.where(prefix < need, eqf, 0.0))
        keep = jnp.where(start + s_row <= t_lane, sel, 0.0)
        mb_ref[:, pl.ds(start, chunk)] = jnp.where(keep > 0.0, 0.0, NEG_INF).T
        return offset + jnp.sum(eqf, axis=0, keepdims=True)

    lax.fori_loop(0, nc, mask_chunk, jnp.zeros((1, BLOCK), F32))

    stacks = [jnp.concatenate([q_ref[:, (2 * j + par) * LANE:(2 * j + par + 1) * LANE]
                               for j in range(C_HEADS // 2)], axis=0) for par in range(2)]
    rows = (C_HEADS // 2) * BLOCK

    def att_chunk(c, carry):
        start = chunk_start(c)
        mb = mb_ref[:, pl.ds(start, chunk)]
        mb = jnp.concatenate([mb] * (C_HEADS // 2), axis=0)
        out = []
        for par in range(2):
            m, acc = carry[par]
            sl = slice(par * LANE, (par + 1) * LANE)
            s = _dot_nt(stacks[par], k_ref[pl.ds(start, chunk), sl]) + mb
            m_new = jnp.maximum(m, jnp.max(s, axis=-1, keepdims=True))
            m_safe = jnp.where(m_new == NEG_INF, 0.0, m_new)
            p = jnp.exp2(s - m_safe).astype(BF16)
            acc = jnp.exp2(m - m_safe) * acc + _dot(p, v_ref[pl.ds(start, chunk), sl])
            out.append((m_new, acc))
        return tuple(out)

    init = tuple((jnp.full((rows, 1), NEG_INF, F32), jnp.zeros((rows, LANE), F32)) for _ in range(2))
    carry = lax.fori_loop(0, nc, att_chunk, init)
    outs = [a / pltpu.roll(a, HEAD_DIM, 1) for a in (carry[0][1], carry[1][1])]
    for j in range(C_HEADS // 2):
        rs = slice(j * BLOCK, (j + 1) * BLOCK)
        o_ref[:, j * LANE:(j + 1) * LANE] = jnp.where(lo, outs[0][rs], outs[1][rs]).astype(BF16)


def _dsa(qc, kc, vc, qi, ki, wit):
    b, s, _ = qc.shape
    top_k = min(DSA_TOPK_MAX, s // 4)
    chunk = 256
    assert s % chunk == 0 and top_k <= chunk
    blk = lambda bi, i: (bi, i, 0)
    seq = lambda bi, i: (bi, 0, 0)
    return pl.pallas_call(
        functools.partial(_dsa_kernel, top_k=top_k, chunk=chunk),
        out_shape=jax.ShapeDtypeStruct((b, s, C_HEADS * HEAD_DIM), BF16),
        grid=(b, s // BLOCK),
        in_specs=[pl.BlockSpec((None, BLOCK, C_HEADS * LANE), blk),
                  pl.BlockSpec((None, s, 2 * LANE), seq),
                  pl.BlockSpec((None, s, 2 * LANE), seq),
                  pl.BlockSpec((None, BLOCK, 256), blk),
                  pl.BlockSpec((None, s, LANE), seq),
                  pl.BlockSpec((None, IDX_HEADS, BLOCK), lambda bi, i: (bi, 0, i))],
        out_specs=pl.BlockSpec((None, BLOCK, 512), blk),
        scratch_shapes=[pltpu.VMEM((s, BLOCK), I32), pltpu.VMEM((BLOCK, s), F32)],
        compiler_params=_cparams(("parallel", "arbitrary")),
        name="dsa_attn",
    )(qc, kc, vc, qi, ki, wit)


def _dup64(w):
    d, c = w.shape
    w = w.reshape(d, c // HEAD_DIM, 1, HEAD_DIM)
    return jnp.broadcast_to(w, (d, c // HEAD_DIM, 2, HEAD_DIM)).reshape(d, 2 * c)


def _pad_cols(w, width):
    return jnp.pad(w, ((0, 0), (0, width - w.shape[1])))


def _split_cols(w, sizes):
    out, start = [], 0
    for n in sizes:
        out.append(w[:, start:start + n])
        start += n
    return out


def _rope_partner(w):
    half = MLA_ROPE // 2
    return jnp.concatenate([-w[..., half:], w[..., :half]], axis=-1)


def _rope_block(w_rope):
    d = w_rope.shape[0]
    return jnp.concatenate([jnp.zeros((d, MLA_NOPE), F32), w_rope,
                            jnp.zeros((d, LANE - MLA_NOPE - MLA_ROPE), F32)], axis=1)


def _alibi_tables(s):
    slopes = 2.0 ** (-8.0 * jnp.arange(1, C_HEADS + 1, dtype=F32) / C_HEADS)
    a = slopes * LOG2E
    pos = jnp.arange(s, dtype=F32)
    a_sp = _split3(a)
    r_sp = _split3(-(a[None, :] * pos[:, None]))
    ones = jnp.ones((s, C_HEADS), F32)
    q_terms = [128.0 * t * ones for t in a_sp] + [t * ones for t in a_sp] + list(r_sp)
    q_aug = jnp.stack(q_terms, axis=-1)
    q_aug = jnp.pad(q_aug, ((0, 0), (0, 0), (0, HEAD_DIM - q_aug.shape[-1])))
    zeros = jnp.zeros_like(q_aug)
    even = jnp.concatenate([zeros, q_aug], axis=-1)
    odd = jnp.concatenate([q_aug, zeros], axis=-1)
    parity = (jnp.arange(C_HEADS) % 2 == 0)[None, :, None]
    tq_t = jnp.where(parity, even, odd).reshape(s, C_HEADS * LANE)
    s_hi = jnp.floor(pos / LANE)
    s_lo = pos - LANE * s_hi
    k_aug = jnp.stack([s_hi] * 3 + [s_lo] * 3 + [jnp.ones_like(pos)] * 3, axis=-1)
    k_aug = jnp.pad(k_aug, ((0, 0), (0, HEAD_DIM - k_aug.shape[-1])))
    kz = jnp.zeros_like(k_aug)
    tk_t = jnp.concatenate([kz, k_aug, k_aug, kz], axis=-1)
    return tq_t.astype(BF16), tk_t.astype(BF16)


def kernel(x, even_w_in, even_b_f, even_sinks, odd_w_in, odd_q_norm, odd_kv_norm, odd_w_uq, odd_w_ukv,
           w_o, ln_g, ln_b, router_w, router_b, moe_w_gate, moe_w_up, moe_w_down):
    b, s, d = x.shape
    n = b * s
    depth = w_o.shape[0]
    alpha = (2 * depth) ** 0.25
    tm = min(512, s)
    tq = min(256, s)
    tm_e = min(256, n)
    tm_c = min(256, n)
    att_scale = HEAD_DIM ** -0.5
    xt = x.reshape(n, d)
    rwt = router_w.T
    rb = router_b.reshape(N_EXPERTS, 1)

    for l in range(depth):
        j = l // 2
        if l % 2 == 0:
            qa, ka, va, qb, kb, vb, fg = _split_cols(
                even_w_in[j], (512, 128, 128, 512, 512, 512, B_HEADS))
            w_cat = jnp.concatenate([qa * att_scale, _dup64(ka), _dup64(va), qb * (att_scale * LOG2E), kb, vb,
                                     _pad_cols(fg, LANE)], axis=1).astype(BF16)
            bf_pad = _pad_cols(even_b_f[j].reshape(1, B_HEADS), LANE)
            qa_, ka_, va_, qf, kf, vf = _even_proj(xt, w_cat, bf_pad, s, tm)
            r3 = lambda a: a.reshape(b, s, a.shape[1])
            out_a = _swa(even_sinks[j], r3(qa_), r3(ka_), r3(va_))
            out_b = _flash(r3(qf), r3(kf), r3(vf), tq=tq, nh=2, name="fox_attn")
        else:
            (qc, kc, vc, qi, ki, wi, cq, ckv, kr) = _split_cols(
                odd_w_in[j], (512, 64, 64, IDX_HEADS * IDX_DIM, IDX_DIM, IDX_HEADS,
                              MLA_Q_RANK, MLA_KV_RANK, MLA_ROPE))
            w_cat = jnp.concatenate([
                qc * (att_scale * LOG2E), _dup64(kc), _dup64(vc), qi * (IDX_DIM ** -0.5), _dup64(ki),
                _pad_cols(wi * (IDX_HEADS ** -0.5), LANE), cq, ckv,
                _rope_block(kr), _rope_block(_rope_partner(kr))], axis=1).astype(BF16)
            dq = MLA_NOPE + MLA_ROPE
            wuq = odd_w_uq[j].reshape(MLA_Q_RANK, D_HEADS, dq)
            pad_q = jnp.zeros((MLA_Q_RANK, D_HEADS, LANE - dq), F32)
            wqa = jnp.concatenate([wuq, pad_q], axis=2).reshape(MLA_Q_RANK, D_HEADS * LANE).astype(BF16)
            wqb = jnp.concatenate([jnp.zeros((MLA_Q_RANK, D_HEADS, MLA_NOPE), F32),
                                   _rope_partner(wuq[..., MLA_NOPE:]), pad_q],
                                  axis=2).reshape(MLA_Q_RANK, D_HEADS * LANE).astype(BF16)
            wukv = odd_w_ukv[j].reshape(MLA_KV_RANK, D_HEADS, MLA_NOPE + MLA_V)
            wk = jnp.concatenate([wukv[..., :MLA_NOPE], jnp.zeros((MLA_KV_RANK, D_HEADS, LANE - MLA_NOPE), F32)],
                                 axis=2).reshape(MLA_KV_RANK, D_HEADS * LANE).astype(BF16)
            wv = wukv[..., MLA_NOPE:].reshape(MLA_KV_RANK, D_HEADS * MLA_V).astype(BF16)
            half = MLA_ROPE // 2
            inv = ROPE_THETA ** (-jnp.arange(half, dtype=F32) / half)
            ang = jnp.arange(s, dtype=F32)[:, None] * inv[None, :]
            cos_h, sin_h = jnp.cos(ang), jnp.sin(ang)
            pad_t = jnp.zeros((s, LANE - dq), F32)
            cos_t = jnp.concatenate([jnp.ones((s, MLA_NOPE), F32), cos_h, cos_h, pad_t], axis=1)
            sin_t = jnp.concatenate([jnp.zeros((s, MLA_NOPE), F32), sin_h, sin_h, pad_t], axis=1)
            tq_t, tk_t = _alibi_tables(s)
            qc_, kc_, vc_, qi_, ki_, wi_, q8, k8, v8 = _odd_proj(
                xt, w_cat, odd_q_norm[j].reshape(1, -1), odd_kv_norm[j].reshape(1, -1),
                wqa, wqb, wk, wv, cos_t, sin_t, tq_t, tk_t, s, tm)
            r3 = lambda a: a.reshape(b, s, a.shape[1])
            out_a = _dsa(r3(qc_), r3(kc_), r3(vc_), r3(qi_), r3(ki_), r3(wi_).transpose(0, 2, 1))
            out_b = _flash(r3(q8), r3(k8), r3(v8), tq=tq, nh=2, name="mla_attn")
        half_w = out_a.shape[2]
        wo = w_o[l].astype(BF16)
        x1, route_t = _outproj(out_a.reshape(n, half_w), out_b.reshape(n, -1), xt,
                               wo[:half_w], wo[half_w:], ln_g[l, 0].reshape(1, d), ln_b[l, 0].reshape(1, d),
                               rwt, rb, alpha, tm)
        xt = _moe_layer(x1, route_t, moe_w_gate[l].astype(BF16), moe_w_up[l].astype(BF16),
                        moe_w_down[l].astype(BF16), ln_g[l, 1].reshape(1, d), ln_b[l, 1].reshape(1, d),
                        alpha, tm_e, tm_c)
    return xt.reshape(b, s, d)
```

```python
import functools
import math

import jax
import jax.numpy as jnp
from jax import lax
from jax.experimental import pallas as pl
from jax.experimental.pallas import tpu as pltpu

F32 = jnp.float32
BF16 = jnp.bfloat16
I32 = jnp.int32

LANE = 128
HEAD_DIM = 64
BLOCK = 128
A_HEADS, A_KV_HEADS, WINDOW = 8, 2, 128
B_HEADS = 8
C_HEADS, IDX_HEADS, IDX_DIM, DSA_TOPK_MAX = 8, 4, 64, 256
D_HEADS, MLA_Q_RANK, MLA_KV_RANK, MLA_NOPE, MLA_ROPE, MLA_V = 8, 256, 128, 64, 32, 64
ROPE_THETA = 10000.0
N_EXPERTS, N_GROUPS = 16, 4
EXPERTS_PER_GROUP = N_EXPERTS // N_GROUPS
LN_EPS, RMS_EPS = 1e-5, 1e-6
NEG_INF = float("-inf")
LOG2E = math.log2(math.e)
VMEM_LIMIT = 56 * 1024 * 1024

HIGHEST = lax.Precision.HIGHEST


def _cparams(sem):
    return pltpu.CompilerParams(dimension_semantics=sem, vmem_limit_bytes=VMEM_LIMIT)


def _dot(a, b):
    return jnp.dot(a, b, preferred_element_type=F32)


def _dot_nt(a, b):
    return lax.dot_general(a, b, (((1,), (1,)), ((), ())), preferred_element_type=F32)


def _lane_lo(shape):
    return lax.broadcasted_iota(I32, shape, len(shape) - 1) % LANE < HEAD_DIM


def _layer_norm(y, g, b):
    mu = jnp.mean(y, axis=-1, keepdims=True)
    yc = y - mu
    var = jnp.mean(yc * yc, axis=-1, keepdims=True)
    return yc * lax.rsqrt(var + LN_EPS) * g + b


def _split3(x):
    def top(v):
        bits = lax.bitcast_convert_type(v, I32) & jnp.int32(-65536)
        return lax.bitcast_convert_type(bits, F32)

    s1 = top(x)
    r = x - s1
    s2 = top(r)
    return s1, s2, r - s2


def _aug_rel(shape, h):
    lane = lax.broadcasted_iota(I32, shape, 1)
    return lane - (HEAD_DIM if h % 2 == 0 else 0)


def _even_proj_kernel(x_ref, w_ref, bf_ref, qa_ref, ka_ref, va_ref, qf_ref, kf_ref, vf_ref,
                      carry_ref, *, tiles_per_seq):
    t = pl.program_id(0)
    tm = x_ref.shape[0]
    h = _dot(x_ref[...].astype(BF16), w_ref[...])
    qa_ref[...] = h[:, 0:512].astype(BF16)
    ka_ref[...] = h[:, 512:768].astype(BF16)
    va_ref[...] = h[:, 768:1024].astype(BF16)
    z = h[:, 2560:2688] + bf_ref[...]
    logf = jnp.minimum(z, 0.0) - jnp.log1p(jnp.exp(-jnp.abs(z)))

    @pl.when(t % tiles_per_seq == 0)
    def _():
        carry_ref[...] = jnp.zeros_like(carry_ref)

    row = lax.broadcasted_iota(I32, (tm, tm), 0)
    col = lax.broadcasted_iota(I32, (tm, tm), 1)
    tri = jnp.where(row >= col, 1.0, 0.0).astype(F32)
    c = jnp.dot(tri, logf, precision=HIGHEST, preferred_element_type=F32) + carry_ref[...]
    carry_ref[...] = c[tm - 1:tm, :]

    s1, s2, s3 = _split3(c * LOG2E)
    lo = _lane_lo((tm, LANE))
    for hd in range(B_HEADS):
        own = lo if hd % 2 == 0 else ~lo
        rel = _aug_rel((tm, LANE), hd)
        b1, b2, b3 = s1[:, hd:hd + 1], s2[:, hd:hd + 1], s3[:, hd:hd + 1]
        q_aug = jnp.where(rel == 3, b1, jnp.where(rel == 4, b2, jnp.where(rel == 5, b3,
                          jnp.where((rel >= 0) & (rel < 3), -1.0, 0.0))))
        k_aug = jnp.where(rel == 0, b1, jnp.where(rel == 1, b2, jnp.where(rel == 2, b3,
                          jnp.where((rel >= 3) & (rel < 6), 1.0, 0.0))))
        pj = (hd // 2) * LANE
        sl = slice(hd * LANE, (hd + 1) * LANE)
        qf_ref[:, sl] = jnp.where(own, h[:, 1024 + pj:1024 + pj + LANE], q_aug).astype(BF16)
        kf_ref[:, sl] = jnp.where(own, h[:, 1536 + pj:1536 + pj + LANE], k_aug).astype(BF16)
        vf_ref[:, sl] = jnp.where(own, h[:, 2048 + pj:2048 + pj + LANE], 1.0).astype(BF16)


def _even_proj(xt, w_cat, bf_pad, seq, tm):
    n, d = xt.shape
    wcols = w_cat.shape[1]
    widths = (512, 256, 256, 1024, 1024, 1024)
    out_shape = [jax.ShapeDtypeStruct((n, w), BF16) for w in widths]
    out_specs = [pl.BlockSpec((tm, w), lambda i: (i, 0)) for w in widths]
    return pl.pallas_call(
        functools.partial(_even_proj_kernel, tiles_per_seq=seq // tm),
        out_shape=out_shape,
        grid=(n // tm,),
        in_specs=[pl.BlockSpec((tm, d), lambda i: (i, 0)),
                  pl.BlockSpec((d, wcols), lambda i: (0, 0)),
                  pl.BlockSpec((1, LANE), lambda i: (0, 0))],
        out_specs=out_specs,
        scratch_shapes=[pltpu.VMEM((1, LANE), F32)],
        compiler_params=_cparams(("arbitrary",)),
        name="even_proj",
    )(xt, w_cat, bf_pad)


def _swa_kernel(sink_ref, q_ref, kp_ref, kc_ref, vp_ref, vc_ref, o_ref):
    i = pl.program_id(1)
    group = A_HEADS // A_KV_HEADS
    tl = lax.broadcasted_iota(I32, (BLOCK, 2 * BLOCK), 0)
    sl = lax.broadcasted_iota(I32, (BLOCK, 2 * BLOCK), 1)
    dist = BLOCK + tl - sl
    valid = (dist >= 0) & (dist < WINDOW) & ((sl >= BLOCK) | (i > 0))
    distf = dist.astype(F32)
    lo = _lane_lo((BLOCK, LANE))
    outs = []
    for h in range(A_HEADS):
        g = h // group
        qp = q_ref[:, (h // 2) * LANE:(h // 2 + 1) * LANE]
        qh = jnp.where(lo if h % 2 == 0 else ~lo, qp, jnp.zeros_like(qp))
        kband = jnp.concatenate([kp_ref[:, g * LANE:(g + 1) * LANE],
                                 kc_ref[:, g * LANE:(g + 1) * LANE]], axis=0)
        vband = jnp.concatenate([vp_ref[:, g * LANE:(g + 1) * LANE],
                                 vc_ref[:, g * LANE:(g + 1) * LANE]], axis=0)
        slope = 2.0 ** (-8.0 * (h + 1) / A_HEADS)
        logits = _dot_nt(qh, kband) - slope * distf
        logits = jnp.where(valid, logits, NEG_INF)
        sink = sink_ref[h]
        m = jnp.maximum(jnp.max(logits, axis=-1, keepdims=True), sink)
        e = jnp.exp(logits - m)
        denom = jnp.sum(e, axis=-1, keepdims=True) + jnp.exp(sink - m)
        outs.append(_dot(e.astype(BF16), vband) / denom)
    for j in range(A_HEADS // 2):
        o_ref[:, j * LANE:(j + 1) * LANE] = jnp.where(lo, outs[2 * j], outs[2 * j + 1]).astype(BF16)


def _swa(sinks, qa, ka, va):
    b, s, _ = qa.shape
    nb = s // BLOCK
    prev = lambda bi, i: (bi, jnp.maximum(i - 1, 0), 0)
    cur = lambda bi, i: (bi, i, 0)
    return pl.pallas_call(
        _swa_kernel,
        out_shape=jax.ShapeDtypeStruct((b, s, A_HEADS * HEAD_DIM), BF16),
        grid=(b, nb),
        in_specs=[pl.BlockSpec(memory_space=pltpu.SMEM),
                  pl.BlockSpec((None, BLOCK, 512), cur),
                  pl.BlockSpec((None, BLOCK, 256), prev),
                  pl.BlockSpec((None, BLOCK, 256), cur),
                  pl.BlockSpec((None, BLOCK, 256), prev),
                  pl.BlockSpec((None, BLOCK, 256), cur)],
        out_specs=pl.BlockSpec((None, BLOCK, 512), cur),
        compiler_params=_cparams(("parallel", "arbitrary")),
        name="swa_attn",
    )(sinks, qa, ka, ka, va, va)


def _online_softmax_step(q, k, v, m, acc, mask=None):
    s = _dot_nt(q, k)
    if mask is not None:
        s = jnp.where(mask, s, NEG_INF)
    m_new = jnp.maximum(m, jnp.max(s, axis=-1, keepdims=True))
    p = jnp.exp2(s - m_new).astype(BF16)
    acc = jnp.exp2(m - m_new) * acc + _dot(p, v)
    return m_new, acc


def _normalize_pairs(accs, o_ref):
    lo = _lane_lo(accs[0].shape)
    outs = [a / pltpu.roll(a, HEAD_DIM, 1) for a in accs]
    for j in range(len(accs) // 2):
        o_ref[:, j * LANE:(j + 1) * LANE] = jnp.where(lo, outs[2 * j], outs[2 * j + 1]).astype(BF16)


def _flash_kernel(q_ref, k_ref, v_ref, o_ref, *, tq, nh):
    qi = pl.program_id(2)
    nslab = tq // BLOCK
    row = lax.broadcasted_iota(I32, (BLOCK, tq), 0)
    col = lax.broadcasted_iota(I32, (BLOCK, tq), 1)

    def step(kb, carry, diagonal):
        start = pl.multiple_of(kb * tq, tq)
        out = []
        for hh in range(nh):
            sl = slice(hh * LANE, (hh + 1) * LANE)
            k = k_ref[pl.ds(start, tq), sl]
            v = v_ref[pl.ds(start, tq), sl]
            for r in range(nslab):
                m, acc = carry[hh * nslab + r]
                mask = (col <= row + r * BLOCK) if diagonal else None
                out.append(_online_softmax_step(q_ref[r * BLOCK:(r + 1) * BLOCK, sl], k, v, m, acc, mask))
        return tuple(out)

    init = tuple((jnp.full((BLOCK, 1), NEG_INF, F32), jnp.zeros((BLOCK, LANE), F32))
                 for _ in range(nh * nslab))
    carry = lax.fori_loop(0, qi, lambda kb, c: step(kb, c, False), init)
    carry = step(qi, carry, True)
    accs = [jnp.concatenate([carry[hh * nslab + r][1] for r in range(nslab)], axis=0) for hh in range(nh)]
    _normalize_pairs(accs, o_ref)


def _flash(q, k, v, *, tq, nh, name):
    b, s, w = v.shape
    return pl.pallas_call(
        functools.partial(_flash_kernel, tq=tq, nh=nh),
        out_shape=jax.ShapeDtypeStruct((b, s, w // 2), BF16),
        grid=(b, w // (nh * LANE), s // tq),
        in_specs=[pl.BlockSpec((None, tq, nh * LANE), lambda bi, j, i: (bi, i, j)),
                  pl.BlockSpec((None, s, nh * LANE), lambda bi, j, i: (bi, 0, j)),
                  pl.BlockSpec((None, s, nh * LANE), lambda bi, j, i: (bi, 0, j))],
        out_specs=pl.BlockSpec((None, tq, nh * LANE // 2), lambda bi, j, i: (bi, i, j)),
        compiler_params=_cparams(("parallel", "parallel", "arbitrary")),
        name=name,
    )(q, k, v)


def _route_rows(aff, sel):
    gs = []
    for g in range(N_GROUPS):
        r = sel[g * EXPERTS_PER_GROUP:(g + 1) * EXPERTS_PER_GROUP]
        best = None
        for a in range(EXPERTS_PER_GROUP):
            for c in range(a + 1, EXPERTS_PER_GROUP):
                pair = r[a] + r[c]
                best = pair if best is None else jnp.maximum(best, pair)
        gs.append(best)
    gmax = functools.reduce(jnp.maximum, gs)
    gidx = jnp.full_like(gmax, float(N_GROUPS - 1))
    for g in range(N_GROUPS - 2, -1, -1):
        gidx = jnp.where(gs[g] == gmax, float(g), gidx)

    def pick(rows, k):
        out = rows[(N_GROUPS - 1) * EXPERTS_PER_GROUP + k]
        for g in range(N_GROUPS - 2, -1, -1):
            out = jnp.where(gidx == float(g), rows[g * EXPERTS_PER_GROUP + k], out)
        return out

    s = [pick(sel, k) for k in range(EXPERTS_PER_GROUP)]
    a = [pick(aff, k) for k in range(EXPERTS_PER_GROUP)]

    def first_max(vals):
        mx = functools.reduce(jnp.maximum, vals)
        idx = jnp.full_like(mx, float(len(vals) - 1))
        for k in range(len(vals) - 2, -1, -1):
            idx = jnp.where(vals[k] == mx, float(k), idx)
        return idx

    def take(vals, idx):
        out = vals[-1]
        for k in range(len(vals) - 2, -1, -1):
            out = jnp.where(idx == float(k), vals[k], out)
        return out

    i1 = first_max(s)
    s2 = [jnp.where(i1 == float(k), NEG_INF, s[k]) for k in range(EXPERTS_PER_GROUP)]
    i2 = first_max(s2)
    a1, a2 = take(a, i1), take(a, i2)
    tot = a1 + a2
    base = gidx * float(EXPERTS_PER_GROUP)
    return base + i1, base + i2, a1 / tot, a2 / tot


def _outproj_kernel(a_ref, b_ref, x_ref, wa_ref, wb_ref, g_ref, be_ref, rwt_ref, rb_ref,
                    x1_ref, route_ref, *, alpha):
    mix = _dot(a_ref[...], wa_ref[...]) + _dot(b_ref[...], wb_ref[...])
    x1 = _layer_norm(alpha * x_ref[...] + mix, g_ref[...], be_ref[...])
    x1_ref[...] = x1
    zt = lax.dot_general(rwt_ref[...], x1, (((1,), (1,)), ((), ())),
                         precision=HIGHEST, preferred_element_type=F32)
    afft = 1.0 / (1.0 + jnp.exp(-zt))
    selt = afft + rb_ref[...]
    aff = [afft[e:e + 1, :] for e in range(N_EXPERTS)]
    sel = [selt[e:e + 1, :] for e in range(N_EXPERTS)]
    e1, e2, w1, w2 = _route_rows(aff, sel)
    zero = jnp.zeros_like(e1)
    route_ref[...] = jnp.concatenate([e1, e2, w1, w2, zero, zero, zero, zero], axis=0)


def _outproj(a, b, xt, wa, wb, g, be, rwt, rb, alpha, tm):
    n, d = xt.shape
    full = lambda i: (0, 0)
    return pl.pallas_call(
        functools.partial(_outproj_kernel, alpha=alpha),
        out_shape=[jax.ShapeDtypeStruct((n, d), F32), jax.ShapeDtypeStruct((8, n), F32)],
        grid=(n // tm,),
        in_specs=[pl.BlockSpec((tm, a.shape[1]), lambda i: (i, 0)),
                  pl.BlockSpec((tm, b.shape[1]), lambda i: (i, 0)),
                  pl.BlockSpec((tm, d), lambda i: (i, 0)),
                  pl.BlockSpec(wa.shape, full), pl.BlockSpec(wb.shape, full),
                  pl.BlockSpec((1, d), full), pl.BlockSpec((1, d), full),
                  pl.BlockSpec(rwt.shape, full), pl.BlockSpec(rb.shape, full)],
        out_specs=[pl.BlockSpec((tm, d), lambda i: (i, 0)), pl.BlockSpec((8, tm), lambda i: (0, i))],
        compiler_params=_cparams(("parallel",)),
        name="outproj_ln_route",
    )(a, b, xt, wa, wb, g, be, rwt, rb)


def _moe_kernel(bexp_ref, bbase_ref, bend_ref, nact_ref, src_ref,
                x_hbm, wg_ref, wu_ref, wd_ref, o_ref, xbuf, sem, *, tm):
    i = pl.program_id(0)
    nact = nact_ref[0]
    last = src_ref.shape[0] - 1

    def row_copy(tok, slot, r):
        return pltpu.make_async_copy(x_hbm.at[pl.ds(tok, 1), :], xbuf.at[slot, pl.ds(r, 1), :],
                                     sem.at[slot])

    def issue(blk, slot):
        base = bbase_ref[blk]
        end = bend_ref[blk]

        def body(r, _):
            pos = jnp.minimum(base + r, end)
            row_copy(src_ref[jnp.minimum(pos, last)], slot, r).start()
            return 0

        lax.fori_loop(0, tm, body, 0, unroll=8)

    @pl.when((i == 0) & (nact > 0))
    def _():
        issue(0, 0)

    @pl.when(i + 1 < nact)
    def _():
        issue(i + 1, (i + 1) % 2)

    @pl.when(i < nact)
    def _():
        slot = i % 2
        pltpu.make_async_copy(x_hbm.at[pl.ds(0, tm), :], xbuf.at[slot], sem.at[slot]).wait()
        xb = xbuf[slot].astype(BF16)
        gate = _dot(xb, wg_ref[...])
        up = _dot(xb, wu_ref[...])
        h = gate * (1.0 / (1.0 + jnp.exp(-gate))) * up
        o_ref[...] = _dot(h.astype(BF16), wd_ref[...])

    @pl.when(i >= nact)
    def _():
        o_ref[...] = jnp.zeros_like(o_ref)


def _moe_experts(x1, wg, wu, wd, bexp, bbase, bend, nact, src, tm):
    n, d = x1.shape
    nblk = bexp.shape[0]
    f = wg.shape[2]
    wmap = lambda i, bexp, *_: (bexp[i], 0, 0)
    grid_spec = pltpu.PrefetchScalarGridSpec(
        num_scalar_prefetch=5,
        grid=(nblk,),
        in_specs=[pl.BlockSpec(memory_space=pl.ANY),
                  pl.BlockSpec((None, d, f), wmap),
                  pl.BlockSpec((None, d, f), wmap),
                  pl.BlockSpec((None, f, d), wmap)],
        out_specs=pl.BlockSpec((tm, d), lambda i, *_: (i, 0)),
        scratch_shapes=[pltpu.VMEM((2, tm, d), F32), pltpu.SemaphoreType.DMA((2,))],
    )
    return pl.pallas_call(
        functools.partial(_moe_kernel, tm=tm),
        out_shape=jax.ShapeDtypeStruct((nblk * tm, d), F32),
        grid_spec=grid_spec,
        compiler_params=_cparams(("arbitrary",)),
        name="moe_experts",
    )(bexp, bbase, bend, nact, src, x1, wg, wu, wd)


def _combine_kernel(p0_ref, p1_ref, x_ref, r_ref, g_ref, be_ref, o_hbm, y_ref, obuf, sem, *, alpha, tm):
    i = pl.program_id(0)
    nsteps = pl.num_programs(0)

    def row_copy(pos, slot, k, r):
        return pltpu.make_async_copy(o_hbm.at[pl.ds(pos, 1), :], obuf.at[slot, k, pl.ds(r, 1), :],
                                     sem.at[slot])

    def issue(blk, slot):
        base = blk * tm

        def body(r, _):
            row_copy(p0_ref[base + r], slot, 0, r).start()
            row_copy(p1_ref[base + r], slot, 1, r).start()
            return 0

        lax.fori_loop(0, tm, body, 0, unroll=8)

    @pl.when(i == 0)
    def _():
        issue(0, 0)

    @pl.when(i + 1 < nsteps)
    def _():
        issue(i + 1, (i + 1) % 2)

    slot = i % 2
    for k in range(2):
        pltpu.make_async_copy(o_hbm.at[pl.ds(0, tm), :], obuf.at[slot, k], sem.at[slot]).wait()
    r = r_ref[...]
    y = alpha * x_ref[...] + r[:, 2:3] * obuf[slot, 0] + r[:, 3:4] * obuf[slot, 1]
    y_ref[...] = _layer_norm(y, g_ref[...], be_ref[...])


def _combine(x1, route, g, be, o, p0, p1, alpha, tm):
    n, d = x1.shape
    grid_spec = pltpu.PrefetchScalarGridSpec(
        num_scalar_prefetch=2,
        grid=(n // tm,),
        in_specs=[pl.BlockSpec((tm, d), lambda i, *_: (i, 0)),
                  pl.BlockSpec((tm, 8), lambda i, *_: (i, 0)),
                  pl.BlockSpec((1, d), lambda i, *_: (0, 0)),
                  pl.BlockSpec((1, d), lambda i, *_: (0, 0)),
                  pl.BlockSpec(memory_space=pl.ANY)],
        out_specs=pl.BlockSpec((tm, d), lambda i, *_: (i, 0)),
        scratch_shapes=[pltpu.VMEM((2, 2, tm, d), F32), pltpu.SemaphoreType.DMA((2,))],
    )
    return pl.pallas_call(
        functools.partial(_combine_kernel, alpha=alpha, tm=tm),
        out_shape=jax.ShapeDtypeStruct((n, d), F32),
        grid_spec=grid_spec,
        compiler_params=_cparams(("arbitrary",)),
        name="moe_combine_ln",
    )(p0, p1, x1, route, g, be, o)


def _moe_layer(x1, route_t, wg, wu, wd, g, be, alpha, tm_e, tm_c):
    n, d = x1.shape
    route = route_t.T
    e = route[:, 0:2].astype(I32).reshape(2 * n)
    onehot = (e[:, None] == jnp.arange(N_EXPERTS, dtype=I32)[None, :]).astype(I32)
    rank = jnp.take_along_axis(jnp.cumsum(onehot, axis=0) - onehot, e[:, None], axis=1)[:, 0]
    counts = jnp.sum(onehot, axis=0)
    nblk_e = (counts + tm_e - 1) // tm_e
    blk_start = jnp.cumsum(nblk_e) - nblk_e
    seg_start = jnp.cumsum(counts) - counts
    pos = blk_start[e] * tm_e + rank
    sorted_pos = seg_start[e] + rank
    src = jnp.zeros((2 * n,), I32).at[sorted_pos].set(jnp.arange(2 * n, dtype=I32) // 2)
    nblk = (2 * n) // tm_e + N_EXPERTS
    blk = jnp.arange(nblk, dtype=I32)
    bexp = jnp.clip(jnp.searchsorted(jnp.cumsum(nblk_e), blk, side="right"), 0, N_EXPERTS - 1).astype(I32)
    bbase = (seg_start[bexp] + (blk - blk_start[bexp]) * tm_e).astype(I32)
    bend = (seg_start[bexp] + counts[bexp] - 1).astype(I32)
    nact = jnp.sum(nblk_e).astype(I32).reshape(1)
    o = _moe_experts(x1, wg, wu, wd, bexp, bbase, bend, nact, src, tm_e)
    p = pos.reshape(n, 2).astype(I32)
    return _combine(x1, route, g, be, o, p[:, 0], p[:, 1], alpha, tm_c)


def _rms(x, g):
    return x * lax.rsqrt(jnp.mean(x * x, axis=-1, keepdims=True) + RMS_EPS) * g

ODD_OFF = dict(qc=0, kc=512, vc=640, qi=768, ki=1024, wi=1152, cq=1280, ckv=1536, kra=1664, krb=1792, end=1920)


def _odd_proj_kernel(x_ref, w_ref, qng_ref, kvng_ref, wqa_ref, wqb_ref, wk_ref, wv_ref, cos_ref, sin_ref,
                     tq_ref, tk_ref,
                     qc_ref, kc_ref, vc_ref, qi_ref, ki_ref, wi_ref, q8_ref, k8_ref, v8_ref, *, q_scale):
    o = ODD_OFF
    tm = x_ref.shape[0]
    lo = _lane_lo((tm, LANE))
    h = _dot(x_ref[...].astype(BF16), w_ref[...])
    kdup = h[:, o["kc"]:o["vc"]]
    vdup = h[:, o["vc"]:o["qi"]]
    kc_ref[:, 0:LANE] = jnp.where(lo, kdup, tk_ref[:, 0:LANE].astype(F32)).astype(BF16)
    kc_ref[:, LANE:2 * LANE] = jnp.where(lo, tk_ref[:, LANE:2 * LANE].astype(F32), kdup).astype(BF16)
    vc_ref[:, 0:LANE] = jnp.where(lo, vdup, 1.0).astype(BF16)
    vc_ref[:, LANE:2 * LANE] = jnp.where(lo, 1.0, vdup).astype(BF16)
    qi_ref[...] = h[:, o["qi"]:o["ki"]].astype(BF16)
    ki_ref[...] = h[:, o["ki"]:o["wi"]].astype(BF16)
    wi_ref[...] = h[:, o["wi"]:o["wi"] + IDX_HEADS]
    cos = cos_ref[...]
    sin = sin_ref[...]
    cqn = _rms(h[:, o["cq"]:o["ckv"]], qng_ref[...]).astype(BF16)
    qa = _dot(cqn, wqa_ref[...])
    qb = _dot(cqn, wqb_ref[...])
    ckvn = _rms(h[:, o["ckv"]:o["kra"]], kvng_ref[...]).astype(BF16)
    kn = _dot(ckvn, wk_ref[...])
    v8 = _dot(ckvn, wv_ref[...])
    kr = h[:, o["kra"]:o["krb"]] * cos + h[:, o["krb"]:o["end"]] * sin
    for hd in range(D_HEADS):
        sl = slice(hd * LANE, (hd + 1) * LANE)
        pj = slice((hd // 2) * LANE, (hd // 2 + 1) * LANE)
        own = lo if hd % 2 == 0 else ~lo
        qc_ref[:, sl] = jnp.where(own, h[:, pj], tq_ref[:, sl].astype(F32)).astype(BF16)
        q8_ref[:, sl] = ((qa[:, sl] * cos + qb[:, sl] * sin) * q_scale).astype(BF16)
        k8_ref[:, sl] = (kn[:, sl] + kr).astype(BF16)
        v8_ref[:, sl] = jnp.where(own, v8[:, pj], 1.0).astype(BF16)


def _odd_proj(xt, w_cat, qng, kvng, wqa, wqb, wk, wv, cos_t, sin_t, tq_t, tk_t, seq, tm):
    n, d = xt.shape
    full = lambda i: (0, 0)
    tps = seq // tm
    pos = lambda i: (i % tps, 0)
    widths = (1024, 256, 256, 256, 128)
    out_shape = [jax.ShapeDtypeStruct((n, w), BF16) for w in widths]
    out_specs = [pl.BlockSpec((tm, w), lambda i: (i, 0)) for w in widths]
    out_shape.append(jax.ShapeDtypeStruct((n, IDX_HEADS), F32))
    out_specs.append(pl.BlockSpec((tm, IDX_HEADS), lambda i: (i, 0)))
    for w in (1024, 1024, 1024):
        out_shape.append(jax.ShapeDtypeStruct((n, w), BF16))
        out_specs.append(pl.BlockSpec((tm, w), lambda i: (i, 0)))
    return pl.pallas_call(
        functools.partial(_odd_proj_kernel, q_scale=(MLA_NOPE + MLA_ROPE) ** -0.5 * LOG2E),
        out_shape=out_shape,
        grid=(n // tm,),
        in_specs=[pl.BlockSpec((tm, d), lambda i: (i, 0)),
                  pl.BlockSpec(w_cat.shape, full),
                  pl.BlockSpec(qng.shape, full), pl.BlockSpec(kvng.shape, full),
                  pl.BlockSpec(wqa.shape, full), pl.BlockSpec(wqb.shape, full),
                  pl.BlockSpec(wk.shape, full), pl.BlockSpec(wv.shape, full),
                  pl.BlockSpec((tm, LANE), pos), pl.BlockSpec((tm, LANE), pos),
                  pl.BlockSpec((tm, C_HEADS * LANE), pos), pl.BlockSpec((tm, 2 * LANE), pos)],
        out_specs=out_specs,
        compiler_params=_cparams(("parallel",)),
        name="odd_proj",
    )(xt, w_cat, qng, kvng, wqa, wqb, wk, wv, cos_t, sin_t, tq_t, tk_t)


def _dsa_kernel(q_ref, k_ref, v_ref, qi_ref, ki_ref, wt_ref, o_ref, key_ref, mb_ref, *, top_k, chunk):
    i = pl.program_id(1)
    nc = (i * BLOCK + BLOCK + chunk - 1) // chunk
    lo = _lane_lo((BLOCK, LANE))
    s_row = lax.broadcasted_iota(I32, (chunk, BLOCK), 0)
    t_lane = i * BLOCK + lax.broadcasted_iota(I32, (chunk, BLOCK), 1)

    def chunk_start(c):
        return pl.multiple_of(c * chunk, chunk)

    qidx = []
    for h in range(IDX_HEADS):
        qp = qi_ref[:, (h // 2) * LANE:(h // 2 + 1) * LANE]
        qidx.append(jnp.where(lo if h % 2 == 0 else ~lo, qp, jnp.zeros_like(qp)))
    q_stack = jnp.concatenate(qidx, axis=0)
    w = wt_ref[...]

    npair = (nc + 1) // 2

    def score_pair(cp, _):
        for u in range(2):
            start = chunk_start(2 * cp + u)
            d = _dot_nt(ki_ref[pl.ds(start, chunk), :], q_stack)
            sc = jnp.zeros((chunk, BLOCK), F32)
            for h in range(IDX_HEADS):
                sc = sc + w[h:h + 1, :] * jnp.maximum(d[:, h * BLOCK:(h + 1) * BLOCK], 0.0)
            sc = jnp.where(sc == 0.0, 0.0, sc)
            sc = jnp.where(start + s_row <= t_lane, sc, NEG_INF)
            bits = pltpu.bitcast(sc, I32)
            key_ref[pl.ds(start, chunk), :] = jnp.where(bits < 0, bits ^ jnp.int32(0x7FFFFFFF), bits)
        return 0

    lax.fori_loop(0, npair, score_pair, 0)

    nacc = 4 * 8

    def count(trial, strict):
        def body(cp, acc):
            for u in range(2):
                key = key_ref[pl.ds(chunk_start(2 * cp + u), chunk), :]
                hit = (key > trial) if strict else (key >= trial)
                acc = acc + jnp.sum(jnp.where(hit, 1.0, 0.0).reshape(chunk // nacc, nacc, BLOCK), axis=0)
            return acc

        acc = lax.fori_loop(0, npair, body, jnp.zeros((nacc, BLOCK), F32))
        return jnp.sum(acc, axis=0, keepdims=True)

    kf = float(top_k)
    cand = jnp.where(count(jnp.zeros((1, BLOCK), I32), False) >= kf, 0, jnp.int32(-2 ** 31)).astype(I32)

    def bit_step(j, cand):
        trial = cand + lax.shift_left(jnp.int32(1), 30 - j)
        return jnp.where(count(trial, False) >= kf, trial, cand)

    thr = lax.fori_loop(0, 31, bit_step, cand)
    need = kf - count(thr, True)

    r0 = lax.broadcasted_iota(I32, (chunk, chunk), 0)
    r1 = lax.broadcasted_iota(I32, (chunk, chunk), 1)
    below = jnp.where(r1 < r0, 1.0, 0.0).astype(BF16)

    def mask_pair(cp, offset):
        for u in range(2):
            start = chunk_start(2 * cp + u)
            key = key_ref[pl.ds(start, chunk), :]
            eqf = jnp.where(key == thr, 1.0, 0.0)
            prefix = _dot(below, eqf.astype(BF16)) + offset
            sel = jnp.where(key > thr, 1.0, jnp.where(prefix < need, eqf, 0.0))
            keep = jnp.where(start + s_row <= t_lane, sel, 0.0)
            mb_ref[:, pl.ds(start, chunk)] = jnp.where(keep > 0.0, 0.0, NEG_INF).T
            offset = offset + jnp.sum(eqf, axis=0, keepdims=True)
        return offset

    lax.fori_loop(0, npair, mask_pair, jnp.zeros((1, BLOCK), F32))

    def att_chunk(c, carry):
        start = chunk_start(c)
        mb = mb_ref[:, pl.ds(start, chunk)]
        out = []
        for h in range(C_HEADS):
            m, acc = carry[h]
            sl = slice((h % 2) * LANE, (h % 2 + 1) * LANE)
            s = _dot_nt(q_ref[:, h * LANE:(h + 1) * LANE], k_ref[pl.ds(start, chunk), sl]) + mb
            m_new = jnp.maximum(m, jnp.max(s, axis=-1, keepdims=True))
            m_safe = jnp.where(m_new == NEG_INF, 0.0, m_new)
            p = jnp.exp2(s - m_safe).astype(BF16)
            acc = jnp.exp2(m - m_safe) * acc + _dot(p, v_ref[pl.ds(start, chunk), sl])
            out.append((m_new, acc))
        return tuple(out)

    init = tuple((jnp.full((BLOCK, 1), NEG_INF, F32), jnp.zeros((BLOCK, LANE), F32)) for _ in range(C_HEADS))
    carry = lax.fori_loop(0, nc, att_chunk, init)
    _normalize_pairs([c[1] for c in carry], o_ref)


def _dsa(qc, kc, vc, qi, ki, wit):
    b, s, _ = qc.shape
    top_k = min(DSA_TOPK_MAX, s // 4)
    chunk = 256
    assert s % (2 * chunk) == 0 and top_k <= chunk
    blk = lambda bi, i: (bi, i, 0)
    seq = lambda bi, i: (bi, 0, 0)
    return pl.pallas_call(
        functools.partial(_dsa_kernel, top_k=top_k, chunk=chunk),
        out_shape=jax.ShapeDtypeStruct((b, s, C_HEADS * HEAD_DIM), BF16),
        grid=(b, s // BLOCK),
        in_specs=[pl.BlockSpec((None, BLOCK, C_HEADS * LANE), blk),
                  pl.BlockSpec((None, s, 2 * LANE), seq),
                  pl.BlockSpec((None, s, 2 * LANE), seq),
                  pl.BlockSpec((None, BLOCK, 256), blk),
                  pl.BlockSpec((None, s, LANE), seq),
                  pl.BlockSpec((None, IDX_HEADS, BLOCK), lambda bi, i: (bi, 0, i))],
        out_specs=pl.BlockSpec((None, BLOCK, 512), blk),
        scratch_shapes=[pltpu.VMEM((s, BLOCK), I32), pltpu.VMEM((BLOCK, s), F32)],
        compiler_params=_cparams(("parallel", "arbitrary")),
        name="dsa_attn",
    )(qc, kc, vc, qi, ki, wit)


def _dup64(w):
    d, c = w.shape
    w = w.reshape(d, c // HEAD_DIM, 1, HEAD_DIM)
    return jnp.broadcast_to(w, (d, c // HEAD_DIM, 2, HEAD_DIM)).reshape(d, 2 * c)


def _pad_cols(w, width):
    return jnp.pad(w, ((0, 0), (0, width - w.shape[1])))


def _split_cols(w, sizes):
    out, start = [], 0
    for n in sizes:
        out.append(w[:, start:start + n])
        start += n
    return out


def _rope_partner(w):
    half = MLA_ROPE // 2
    return jnp.concatenate([-w[..., half:], w[..., :half]], axis=-1)


def _rope_block(w_rope):
    d = w_rope.shape[0]
    return jnp.concatenate([jnp.zeros((d, MLA_NOPE), F32), w_rope,
                            jnp.zeros((d, LANE - MLA_NOPE - MLA_ROPE), F32)], axis=1)


def _alibi_tables(s):
    slopes = 2.0 ** (-8.0 * jnp.arange(1, C_HEADS + 1, dtype=F32) / C_HEADS)
    a = slopes * LOG2E
    pos = jnp.arange(s, dtype=F32)
    a_sp = _split3(a)
    r_sp = _split3(-(a[None, :] * pos[:, None]))
    ones = jnp.ones((s, C_HEADS), F32)
    q_terms = [128.0 * t * ones for t in a_sp] + [t * ones for t in a_sp] + list(r_sp)
    q_aug = jnp.stack(q_terms, axis=-1)
    q_aug = jnp.pad(q_aug, ((0, 0), (0, 0), (0, HEAD_DIM - q_aug.shape[-1])))
    zeros = jnp.zeros_like(q_aug)
    even = jnp.concatenate([zeros, q_aug], axis=-1)
    odd = jnp.concatenate([q_aug, zeros], axis=-1)
    parity = (jnp.arange(C_HEADS) % 2 == 0)[None, :, None]
    tq_t = jnp.where(parity, even, odd).reshape(s, C_HEADS * LANE)
    s_hi = jnp.floor(pos / LANE)
    s_lo = pos - LANE * s_hi
    k_aug = jnp.stack([s_hi] * 3 + [s_lo] * 3 + [jnp.ones_like(pos)] * 3, axis=-1)
    k_aug = jnp.pad(k_aug, ((0, 0), (0, HEAD_DIM - k_aug.shape[-1])))
    kz = jnp.zeros_like(k_aug)
    tk_t = jnp.concatenate([kz, k_aug, k_aug, kz], axis=-1)
    return tq_t.astype(BF16), tk_t.astype(BF16)


def kernel(x, even_w_in, even_b_f, even_sinks, odd_w_in, odd_q_norm, odd_kv_norm, odd_w_uq, odd_w_ukv,
           w_o, ln_g, ln_b, router_w, router_b, moe_w_gate, moe_w_up, moe_w_down):
    b, s, d = x.shape
    n = b * s
    depth = w_o.shape[0]
    alpha = (2 * depth) ** 0.25
    tm = min(512, s)
    tq = min(256, s)
    tm_e = min(256, n)
    tm_c = min(256, n)
    att_scale = HEAD_DIM ** -0.5
    xt = x.reshape(n, d)
    rwt = router_w.T
    rb = router_b.reshape(N_EXPERTS, 1)

    for l in range(depth):
        j = l // 2
        if l % 2 == 0:
            qa, ka, va, qb, kb, vb, fg = _split_cols(
                even_w_in[j], (512, 128, 128, 512, 512, 512, B_HEADS))
            w_cat = jnp.concatenate([qa * att_scale, _dup64(ka), _dup64(va), qb * (att_scale * LOG2E), kb, vb,
                                     _pad_cols(fg, LANE)], axis=1).astype(BF16)
            bf_pad = _pad_cols(even_b_f[j].reshape(1, B_HEADS), LANE)
            qa_, ka_, va_, qf, kf, vf = _even_proj(xt, w_cat, bf_pad, s, tm)
            r3 = lambda a: a.reshape(b, s, a.shape[1])
            out_a = _swa(even_sinks[j], r3(qa_), r3(ka_), r3(va_))
            out_b = _flash(r3(qf), r3(kf), r3(vf), tq=tq, nh=2, name="fox_attn")
        else:
            (qc, kc, vc, qi, ki, wi, cq, ckv, kr) = _split_cols(
                odd_w_in[j], (512, 64, 64, IDX_HEADS * IDX_DIM, IDX_DIM, IDX_HEADS,
                              MLA_Q_RANK, MLA_KV_RANK, MLA_ROPE))
            w_cat = jnp.concatenate([
                qc * (att_scale * LOG2E), _dup64(kc), _dup64(vc), qi * (IDX_DIM ** -0.5), _dup64(ki),
                _pad_cols(wi * (IDX_HEADS ** -0.5), LANE), cq, ckv,
                _rope_block(kr), _rope_block(_rope_partner(kr))], axis=1).astype(BF16)
            dq = MLA_NOPE + MLA_ROPE
            wuq = odd_w_uq[j].reshape(MLA_Q_RANK, D_HEADS, dq)
            pad_q = jnp.zeros((MLA_Q_RANK, D_HEADS, LANE - dq), F32)
            wqa = jnp.concatenate([wuq, pad_q], axis=2).reshape(MLA_Q_RANK, D_HEADS * LANE).astype(BF16)
            wqb = jnp.concatenate([jnp.zeros((MLA_Q_RANK, D_HEADS, MLA_NOPE), F32),
                                   _rope_partner(wuq[..., MLA_NOPE:]), pad_q],
                                  axis=2).reshape(MLA_Q_RANK, D_HEADS * LANE).astype(BF16)
            wukv = odd_w_ukv[j].reshape(MLA_KV_RANK, D_HEADS, MLA_NOPE + MLA_V)
            wk = jnp.concatenate([wukv[..., :MLA_NOPE], jnp.zeros((MLA_KV_RANK, D_HEADS, LANE - MLA_NOPE), F32)],
                                 axis=2).reshape(MLA_KV_RANK, D_HEADS * LANE).astype(BF16)
            wv = wukv[..., MLA_NOPE:].reshape(MLA_KV_RANK, D_HEADS * MLA_V).astype(BF16)
            half = MLA_ROPE // 2
            inv = ROPE_THETA ** (-jnp.arange(half, dtype=F32) / half)
            ang = jnp.arange(s, dtype=F32)[:, None] * inv[None, :]
            cos_h, sin_h = jnp.cos(ang), jnp.sin(ang)
            pad_t = jnp.zeros((s, LANE - dq), F32)
            cos_t = jnp.concatenate([jnp.ones((s, MLA_NOPE), F32), cos_h, cos_h, pad_t], axis=1)
            sin_t = jnp.concatenate([jnp.zeros((s, MLA_NOPE), F32), sin_h, sin_h, pad_t], axis=1)
            tq_t, tk_t = _alibi_tables(s)
            qc_, kc_, vc_, qi_, ki_, wi_, q8, k8, v8 = _odd_proj(
                xt, w_cat, odd_q_norm[j].reshape(1, -1), odd_kv_norm[j].reshape(1, -1),
                wqa, wqb, wk, wv, cos_t, sin_t, tq_t, tk_t, s, tm)
            r3 = lambda a: a.reshape(b, s, a.shape[1])
            out_a = _dsa(r3(qc_), r3(kc_), r3(vc_), r3(qi_), r3(ki_), r3(wi_).transpose(0, 2, 1))
            out_b = _flash(r3(q8), r3(k8), r3(v8), tq=tq, nh=2, name="mla_attn")
        half_w = out_a.shape[2]
        wo = w_o[l].astype(BF16)
        x1, route_t = _outproj(out_a.reshape(n, half_w), out_b.reshape(n, -1), xt,
                               wo[:half_w], wo[half_w:], ln_g[l, 0].reshape(1, d), ln_b[l, 0].reshape(1, d),
                               rwt, rb, alpha, tm)
        xt = _moe_layer(x1, route_t, moe_w_gate[l].astype(BF16), moe_w_up[l].astype(BF16),
                        moe_w_down[l].astype(BF16), ln_g[l, 1].reshape(1, d), ln_b[l, 1].reshape(1, d),
                        alpha, tm_e, tm_c)
    return xt.reshape(b, s, d)
```

```python
import functools
import math

import jax
import jax.numpy as jnp
from jax import lax
from jax.experimental import pallas as pl
from jax.experimental.pallas import tpu as pltpu

F32 = jnp.float32
BF16 = jnp.bfloat16
I32 = jnp.int32

LANE = 128
HEAD_DIM = 64
BLOCK = 128
A_HEADS, A_KV_HEADS, WINDOW = 8, 2, 128
B_HEADS = 8
C_HEADS, IDX_HEADS, IDX_DIM, DSA_TOPK_MAX = 8, 4, 64, 256
D_HEADS, MLA_Q_RANK, MLA_KV_RANK, MLA_NOPE, MLA_ROPE, MLA_V = 8, 256, 128, 64, 32, 64
ROPE_THETA = 10000.0
N_EXPERTS, N_GROUPS = 16, 4
EXPERTS_PER_GROUP = N_EXPERTS // N_GROUPS
LN_EPS, RMS_EPS = 1e-5, 1e-6
NEG_INF = float("-inf")
LOG2E = math.log2(math.e)
VMEM_LIMIT = 56 * 1024 * 1024

HIGHEST = lax.Precision.HIGHEST


def _cparams(sem):
    return pltpu.CompilerParams(dimension_semantics=sem, vmem_limit_bytes=VMEM_LIMIT)


def _dot(a, b):
    return jnp.dot(a, b, preferred_element_type=F32)


def _dot_nt(a, b):
    return lax.dot_general(a, b, (((1,), (1,)), ((), ())), preferred_element_type=F32)


def _lane_lo(shape):
    return lax.broadcasted_iota(I32, shape, len(shape) - 1) % LANE < HEAD_DIM


def _layer_norm(y, g, b):
    mu = jnp.mean(y, axis=-1, keepdims=True)
    yc = y - mu
    var = jnp.mean(yc * yc, axis=-1, keepdims=True)
    return yc * lax.rsqrt(var + LN_EPS) * g + b


def _split3(x):
    def top(v):
        bits = lax.bitcast_convert_type(v, I32) & jnp.int32(-65536)
        return lax.bitcast_convert_type(bits, F32)

    s1 = top(x)
    r = x - s1
    s2 = top(r)
    return s1, s2, r - s2


def _aug_rel(shape, h):
    lane = lax.broadcasted_iota(I32, shape, 1)
    return lane - (HEAD_DIM if h % 2 == 0 else 0)


def _even_proj_kernel(x_ref, w_ref, bf_ref, qa_ref, ka_ref, va_ref, qf_ref, kf_ref, vf_ref,
                      carry_ref, *, tiles_per_seq):
    t = pl.program_id(0)
    tm = x_ref.shape[0]
    h = _dot(x_ref[...].astype(BF16), w_ref[...])
    qa_ref[...] = h[:, 0:512].astype(BF16)
    ka_ref[...] = h[:, 512:768].astype(BF16)
    va_ref[...] = h[:, 768:1024].astype(BF16)
    z = h[:, 2560:2688] + bf_ref[...]
    logf = jnp.minimum(z, 0.0) - jnp.log1p(jnp.exp(-jnp.abs(z)))

    @pl.when(t % tiles_per_seq == 0)
    def _():
        carry_ref[...] = jnp.zeros_like(carry_ref)

    row = lax.broadcasted_iota(I32, (tm, tm), 0)
    col = lax.broadcasted_iota(I32, (tm, tm), 1)
    tri = jnp.where(row >= col, 1.0, 0.0).astype(F32)
    c = jnp.dot(tri, logf, precision=HIGHEST, preferred_element_type=F32) + carry_ref[...]
    carry_ref[...] = c[tm - 1:tm, :]

    s1, s2, s3 = _split3(c * LOG2E)
    lo = _lane_lo((tm, LANE))
    for hd in range(B_HEADS):
        own = lo if hd % 2 == 0 else ~lo
        rel = _aug_rel((tm, LANE), hd)
        b1, b2, b3 = s1[:, hd:hd + 1], s2[:, hd:hd + 1], s3[:, hd:hd + 1]
        q_aug = jnp.where(rel == 3, b1, jnp.where(rel == 4, b2, jnp.where(rel == 5, b3,
                          jnp.where((rel >= 0) & (rel < 3), -1.0, 0.0))))
        k_aug = jnp.where(rel == 0, b1, jnp.where(rel == 1, b2, jnp.where(rel == 2, b3,
                          jnp.where((rel >= 3) & (rel < 6), 1.0, 0.0))))
        pj = (hd // 2) * LANE
        sl = slice(hd * LANE, (hd + 1) * LANE)
        qf_ref[:, sl] = jnp.where(own, h[:, 1024 + pj:1024 + pj + LANE], q_aug).astype(BF16)
        kf_ref[:, sl] = jnp.where(own, h[:, 1536 + pj:1536 + pj + LANE], k_aug).astype(BF16)
        vf_ref[:, sl] = jnp.where(own, h[:, 2048 + pj:2048 + pj + LANE], 1.0).astype(BF16)


def _even_proj(xt, w_cat, bf_pad, seq, tm):
    n, d = xt.shape
    wcols = w_cat.shape[1]
    widths = (512, 256, 256, 1024, 1024, 1024)
    out_shape = [jax.ShapeDtypeStruct((n, w), BF16) for w in widths]
    out_specs = [pl.BlockSpec((tm, w), lambda i: (i, 0)) for w in widths]
    return pl.pallas_call(
        functools.partial(_even_proj_kernel, tiles_per_seq=seq // tm),
        out_shape=out_shape,
        grid=(n // tm,),
        in_specs=[pl.BlockSpec((tm, d), lambda i: (i, 0)),
                  pl.BlockSpec((d, wcols), lambda i: (0, 0)),
                  pl.BlockSpec((1, LANE), lambda i: (0, 0))],
        out_specs=out_specs,
        scratch_shapes=[pltpu.VMEM((1, LANE), F32)],
        compiler_params=_cparams(("arbitrary",)),
        name="even_proj",
    )(xt, w_cat, bf_pad)


def _swa_kernel(sink_ref, q_ref, kp_ref, kc_ref, vp_ref, vc_ref, o_ref):
    i = pl.program_id(1)
    group = A_HEADS // A_KV_HEADS
    tl = lax.broadcasted_iota(I32, (BLOCK, 2 * BLOCK), 0)
    sl = lax.broadcasted_iota(I32, (BLOCK, 2 * BLOCK), 1)
    dist = BLOCK + tl - sl
    valid = (dist >= 0) & (dist < WINDOW) & ((sl >= BLOCK) | (i > 0))
    distf = dist.astype(F32)
    lo = _lane_lo((BLOCK, LANE))
    outs = []
    for h in range(A_HEADS):
        g = h // group
        qp = q_ref[:, (h // 2) * LANE:(h // 2 + 1) * LANE]
        qh = jnp.where(lo if h % 2 == 0 else ~lo, qp, jnp.zeros_like(qp))
        kband = jnp.concatenate([kp_ref[:, g * LANE:(g + 1) * LANE],
                                 kc_ref[:, g * LANE:(g + 1) * LANE]], axis=0)
        vband = jnp.concatenate([vp_ref[:, g * LANE:(g + 1) * LANE],
                                 vc_ref[:, g * LANE:(g + 1) * LANE]], axis=0)
        slope = 2.0 ** (-8.0 * (h + 1) / A_HEADS)
        logits = _dot_nt(qh, kband) - slope * distf
        logits = jnp.where(valid, logits, NEG_INF)
        sink = sink_ref[h]
        m = jnp.maximum(jnp.max(logits, axis=-1, keepdims=True), sink)
        e = jnp.exp(logits - m)
        denom = jnp.sum(e, axis=-1, keepdims=True) + jnp.exp(sink - m)
        outs.append(_dot(e.astype(BF16), vband) / denom)
    for j in range(A_HEADS // 2):
        o_ref[:, j * LANE:(j + 1) * LANE] = jnp.where(lo, outs[2 * j], outs[2 * j + 1]).astype(BF16)


def _swa(sinks, qa, ka, va):
    b, s, _ = qa.shape
    nb = s // BLOCK
    prev = lambda bi, i: (bi, jnp.maximum(i - 1, 0), 0)
    cur = lambda bi, i: (bi, i, 0)
    return pl.pallas_call(
        _swa_kernel,
        out_shape=jax.ShapeDtypeStruct((b, s, A_HEADS * HEAD_DIM), BF16),
        grid=(b, nb),
        in_specs=[pl.BlockSpec(memory_space=pltpu.SMEM),
                  pl.BlockSpec((None, BLOCK, 512), cur),
                  pl.BlockSpec((None, BLOCK, 256), prev),
                  pl.BlockSpec((None, BLOCK, 256), cur),
                  pl.BlockSpec((None, BLOCK, 256), prev),
                  pl.BlockSpec((None, BLOCK, 256), cur)],
        out_specs=pl.BlockSpec((None, BLOCK, 512), cur),
        compiler_params=_cparams(("parallel", "arbitrary")),
        name="swa_attn",
    )(sinks, qa, ka, ka, va, va)


def _normalize_pairs(accs, o_ref):
    lo = _lane_lo(accs[0].shape)
    outs = [a / pltpu.roll(a, HEAD_DIM, 1) for a in accs]
    for j in range(len(accs) // 2):
        o_ref[:, j * LANE:(j + 1) * LANE] = jnp.where(lo, outs[2 * j], outs[2 * j + 1]).astype(BF16)


def _online_softmax_chains(qs, ks, vs, carry, bias=None, masks=None, guard_empty=False):
    n = len(qs)
    ss = [_dot_nt(qs[i], ks[i]) for i in range(n)]
    if bias is not None:
        ss = [s + bias for s in ss]
    if masks is not None:
        ss = [jnp.where(masks[i], ss[i], NEG_INF) for i in range(n)]
    ms, ps, alphas = [], [], []
    for i in range(n):
        m = carry[i][0]
        m_new = jnp.maximum(m, jnp.max(ss[i], axis=-1, keepdims=True))
        m_ref = jnp.where(m_new == NEG_INF, 0.0, m_new) if guard_empty else m_new
        ps.append(jnp.exp2(ss[i] - m_ref).astype(BF16))
        alphas.append(jnp.exp2(m - m_ref))
        ms.append(m_new)
    pvs = [_dot(ps[i], vs[i]) for i in range(n)]
    return tuple((ms[i], alphas[i] * carry[i][1] + pvs[i]) for i in range(n))


def _flash_kernel(q_ref, k_ref, v_ref, o_ref, *, tq, nh):
    qi = pl.program_id(2)
    nslab = tq // BLOCK
    chains = [(hh, r) for hh in range(nh) for r in range(nslab)]
    row = lax.broadcasted_iota(I32, (BLOCK, tq), 0)
    col = lax.broadcasted_iota(I32, (BLOCK, tq), 1)

    def lanes(hh):
        return slice(hh * LANE, (hh + 1) * LANE)

    def step(kb, carry, diagonal):
        start = pl.multiple_of(kb * tq, tq)
        qs = [q_ref[r * BLOCK:(r + 1) * BLOCK, lanes(hh)] for hh, r in chains]
        ks = [k_ref[pl.ds(start, tq), lanes(hh)] for hh, r in chains]
        vs = [v_ref[pl.ds(start, tq), lanes(hh)] for hh, r in chains]
        masks = [col <= row + r * BLOCK for hh, r in chains] if diagonal else None
        return _online_softmax_chains(qs, ks, vs, carry, masks=masks)

    init = tuple((jnp.full((BLOCK, 1), NEG_INF, F32), jnp.zeros((BLOCK, LANE), F32))
                 for _ in range(nh * nslab))
    carry = lax.fori_loop(0, qi, lambda kb, c: step(kb, c, False), init)
    carry = step(qi, carry, True)
    accs = [jnp.concatenate([carry[hh * nslab + r][1] for r in range(nslab)], axis=0) for hh in range(nh)]
    _normalize_pairs(accs, o_ref)


def _flash(q, k, v, *, tq, nh, name):
    b, s, w = v.shape
    return pl.pallas_call(
        functools.partial(_flash_kernel, tq=tq, nh=nh),
        out_shape=jax.ShapeDtypeStruct((b, s, w // 2), BF16),
        grid=(b, w // (nh * LANE), s // tq),
        in_specs=[pl.BlockSpec((None, tq, nh * LANE), lambda bi, j, i: (bi, i, j)),
                  pl.BlockSpec((None, s, nh * LANE), lambda bi, j, i: (bi, 0, j)),
                  pl.BlockSpec((None, s, nh * LANE), lambda bi, j, i: (bi, 0, j))],
        out_specs=pl.BlockSpec((None, tq, nh * LANE // 2), lambda bi, j, i: (bi, i, j)),
        compiler_params=_cparams(("parallel", "parallel", "arbitrary")),
        name=name,
    )(q, k, v)


def _route_rows(aff, sel):
    gs = []
    for g in range(N_GROUPS):
        r = sel[g * EXPERTS_PER_GROUP:(g + 1) * EXPERTS_PER_GROUP]
        best = None
        for a in range(EXPERTS_PER_GROUP):
            for c in range(a + 1, EXPERTS_PER_GROUP):
                pair = r[a] + r[c]
                best = pair if best is None else jnp.maximum(best, pair)
        gs.append(best)
    gmax = functools.reduce(jnp.maximum, gs)
    gidx = jnp.full_like(gmax, float(N_GROUPS - 1))
    for g in range(N_GROUPS - 2, -1, -1):
        gidx = jnp.where(gs[g] == gmax, float(g), gidx)

    def pick(rows, k):
        out = rows[(N_GROUPS - 1) * EXPERTS_PER_GROUP + k]
        for g in range(N_GROUPS - 2, -1, -1):
            out = jnp.where(gidx == float(g), rows[g * EXPERTS_PER_GROUP + k], out)
        return out

    s = [pick(sel, k) for k in range(EXPERTS_PER_GROUP)]
    a = [pick(aff, k) for k in range(EXPERTS_PER_GROUP)]

    def first_max(vals):
        mx = functools.reduce(jnp.maximum, vals)
        idx = jnp.full_like(mx, float(len(vals) - 1))
        for k in range(len(vals) - 2, -1, -1):
            idx = jnp.where(vals[k] == mx, float(k), idx)
        return idx

    def take(vals, idx):
        out = vals[-1]
        for k in range(len(vals) - 2, -1, -1):
            out = jnp.where(idx == float(k), vals[k], out)
        return out

    i1 = first_max(s)
    s2 = [jnp.where(i1 == float(k), NEG_INF, s[k]) for k in range(EXPERTS_PER_GROUP)]
    i2 = first_max(s2)
    a1, a2 = take(a, i1), take(a, i2)
    tot = a1 + a2
    base = gidx * float(EXPERTS_PER_GROUP)
    return base + i1, base + i2, a1 / tot, a2 / tot


def _outproj_kernel(a_ref, b_ref, x_ref, wa_ref, wb_ref, g_ref, be_ref, rwt_ref, rb_ref,
                    x1_ref, route_ref, *, alpha):
    mix = _dot(a_ref[...], wa_ref[...]) + _dot(b_ref[...], wb_ref[...])
    x1 = _layer_norm(alpha * x_ref[...] + mix, g_ref[...], be_ref[...])
    x1_ref[...] = x1
    zt = lax.dot_general(rwt_ref[...], x1, (((1,), (1,)), ((), ())),
                         precision=HIGHEST, preferred_element_type=F32)
    afft = 1.0 / (1.0 + jnp.exp(-zt))
    selt = afft + rb_ref[...]
    aff = [afft[e:e + 1, :] for e in range(N_EXPERTS)]
    sel = [selt[e:e + 1, :] for e in range(N_EXPERTS)]
    e1, e2, w1, w2 = _route_rows(aff, sel)
    zero = jnp.zeros_like(e1)
    route_ref[...] = jnp.concatenate([e1, e2, w1, w2, zero, zero, zero, zero], axis=0)


def _outproj(a, b, xt, wa, wb, g, be, rwt, rb, alpha, tm):
    n, d = xt.shape
    full = lambda i: (0, 0)
    return pl.pallas_call(
        functools.partial(_outproj_kernel, alpha=alpha),
        out_shape=[jax.ShapeDtypeStruct((n, d), F32), jax.ShapeDtypeStruct((8, n), F32)],
        grid=(n // tm,),
        in_specs=[pl.BlockSpec((tm, a.shape[1]), lambda i: (i, 0)),
                  pl.BlockSpec((tm, b.shape[1]), lambda i: (i, 0)),
                  pl.BlockSpec((tm, d), lambda i: (i, 0)),
                  pl.BlockSpec(wa.shape, full), pl.BlockSpec(wb.shape, full),
                  pl.BlockSpec((1, d), full), pl.BlockSpec((1, d), full),
                  pl.BlockSpec(rwt.shape, full), pl.BlockSpec(rb.shape, full)],
        out_specs=[pl.BlockSpec((tm, d), lambda i: (i, 0)), pl.BlockSpec((8, tm), lambda i: (0, i))],
        compiler_params=_cparams(("parallel",)),
        name="outproj_ln_route",
    )(a, b, xt, wa, wb, g, be, rwt, rb)


def _moe_kernel(bexp_ref, bbase_ref, bend_ref, nact_ref, src_ref,
                x_hbm, wg_ref, wu_ref, wd_ref, o_ref, xbuf, sem, *, tm):
    i = pl.program_id(0)
    nact = nact_ref[0]
    last = src_ref.shape[0] - 1

    def row_copy(tok, slot, r):
        return pltpu.make_async_copy(x_hbm.at[pl.ds(tok, 1), :], xbuf.at[slot, pl.ds(r, 1), :],
                                     sem.at[slot])

    def issue(blk, slot):
        base = bbase_ref[blk]
        end = bend_ref[blk]

        def body(r, _):
            pos = jnp.minimum(base + r, end)
            row_copy(src_ref[jnp.minimum(pos, last)], slot, r).start()
            return 0

        lax.fori_loop(0, tm, body, 0, unroll=8)

    @pl.when((i == 0) & (nact > 0))
    def _():
        issue(0, 0)

    @pl.when(i + 1 < nact)
    def _():
        issue(i + 1, (i + 1) % 2)

    @pl.when(i < nact)
    def _():
        slot = i % 2
        pltpu.make_async_copy(x_hbm.at[pl.ds(0, tm), :], xbuf.at[slot], sem.at[slot]).wait()
        xb = xbuf[slot].astype(BF16)
        gate = _dot(xb, wg_ref[...])
        up = _dot(xb, wu_ref[...])
        h = gate * (1.0 / (1.0 + jnp.exp(-gate))) * up
        o_ref[...] = _dot(h.astype(BF16), wd_ref[...])

    @pl.when(i >= nact)
    def _():
        o_ref[...] = jnp.zeros_like(o_ref)


def _moe_experts(x1, wg, wu, wd, bexp, bbase, bend, nact, src, tm):
    n, d = x1.shape
    nblk = bexp.shape[0]
    f = wg.shape[2]
    wmap = lambda i, bexp, *_: (bexp[i], 0, 0)
    grid_spec = pltpu.PrefetchScalarGridSpec(
        num_scalar_prefetch=5,
        grid=(nblk,),
        in_specs=[pl.BlockSpec(memory_space=pl.ANY),
                  pl.BlockSpec((None, d, f), wmap),
                  pl.BlockSpec((None, d, f), wmap),
                  pl.BlockSpec((None, f, d), wmap)],
        out_specs=pl.BlockSpec((tm, d), lambda i, *_: (i, 0)),
        scratch_shapes=[pltpu.VMEM((2, tm, d), F32), pltpu.SemaphoreType.DMA((2,))],
    )
    return pl.pallas_call(
        functools.partial(_moe_kernel, tm=tm),
        out_shape=jax.ShapeDtypeStruct((nblk * tm, d), F32),
        grid_spec=grid_spec,
        compiler_params=_cparams(("arbitrary",)),
        name="moe_experts",
    )(bexp, bbase, bend, nact, src, x1, wg, wu, wd)


def _combine_kernel(p0_ref, p1_ref, x_ref, r_ref, g_ref, be_ref, o_hbm, y_ref, obuf, sem, *, alpha, tm):
    i = pl.program_id(0)
    nsteps = pl.num_programs(0)

    def row_copy(pos, slot, k, r):
        return pltpu.make_async_copy(o_hbm.at[pl.ds(pos, 1), :], obuf.at[slot, k, pl.ds(r, 1), :],
                                     sem.at[slot])

    def issue(blk, slot):
        base = blk * tm

        def body(r, _):
            row_copy(p0_ref[base + r], slot, 0, r).start()
            row_copy(p1_ref[base + r], slot, 1, r).start()
            return 0

        lax.fori_loop(0, tm, body, 0, unroll=8)

    @pl.when(i == 0)
    def _():
        issue(0, 0)

    @pl.when(i + 1 < nsteps)
    def _():
        issue(i + 1, (i + 1) % 2)

    slot = i % 2
    for k in range(2):
        pltpu.make_async_copy(o_hbm.at[pl.ds(0, tm), :], obuf.at[slot, k], sem.at[slot]).wait()
    r = r_ref[...]
    y = alpha * x_ref[...] + r[:, 2:3] * obuf[slot, 0] + r[:, 3:4] * obuf[slot, 1]
    y_ref[...] = _layer_norm(y, g_ref[...], be_ref[...])


def _combine(x1, route, g, be, o, p0, p1, alpha, tm):
    n, d = x1.shape
    grid_spec = pltpu.PrefetchScalarGridSpec(
        num_scalar_prefetch=2,
        grid=(n // tm,),
        in_specs=[pl.BlockSpec((tm, d), lambda i, *_: (i, 0)),
                  pl.BlockSpec((tm, 8), lambda i, *_: (i, 0)),
                  pl.BlockSpec((1, d), lambda i, *_: (0, 0)),
                  pl.BlockSpec((1, d), lambda i, *_: (0, 0)),
                  pl.BlockSpec(memory_space=pl.ANY)],
        out_specs=pl.BlockSpec((tm, d), lambda i, *_: (i, 0)),
        scratch_shapes=[pltpu.VMEM((2, 2, tm, d), F32), pltpu.SemaphoreType.DMA((2,))],
    )
    return pl.pallas_call(
        functools.partial(_combine_kernel, alpha=alpha, tm=tm),
        out_shape=jax.ShapeDtypeStruct((n, d), F32),
        grid_spec=grid_spec,
        compiler_params=_cparams(("arbitrary",)),
        name="moe_combine_ln",
    )(p0, p1, x1, route, g, be, o)


def _moe_layer(x1, route_t, wg, wu, wd, g, be, alpha, tm_e, tm_c):
    n, d = x1.shape
    route = route_t.T
    e = route[:, 0:2].astype(I32).reshape(2 * n)
    onehot = (e[:, None] == jnp.arange(N_EXPERTS, dtype=I32)[None, :]).astype(I32)
    rank = jnp.take_along_axis(jnp.cumsum(onehot, axis=0) - onehot, e[:, None], axis=1)[:, 0]
    counts = jnp.sum(onehot, axis=0)
    nblk_e = (counts + tm_e - 1) // tm_e
    blk_start = jnp.cumsum(nblk_e) - nblk_e
    seg_start = jnp.cumsum(counts) - counts
    pos = blk_start[e] * tm_e + rank
    sorted_pos = seg_start[e] + rank
    src = jnp.zeros((2 * n,), I32).at[sorted_pos].set(jnp.arange(2 * n, dtype=I32) // 2)
    nblk = (2 * n) // tm_e + N_EXPERTS
    blk = jnp.arange(nblk, dtype=I32)
    blk_end = jnp.cumsum(nblk_e)
    bexp = jnp.minimum(jnp.sum((blk[:, None] >= blk_end[None, :]).astype(I32), axis=1), N_EXPERTS - 1)
    bbase = (seg_start[bexp] + (blk - blk_start[bexp]) * tm_e).astype(I32)
    bend = (seg_start[bexp] + counts[bexp] - 1).astype(I32)
    nact = jnp.sum(nblk_e).astype(I32).reshape(1)
    o = _moe_experts(x1, wg, wu, wd, bexp, bbase, bend, nact, src, tm_e)
    p = pos.reshape(n, 2).astype(I32)
    return _combine(x1, route, g, be, o, p[:, 0], p[:, 1], alpha, tm_c)


def _rms(x, g):
    return x * lax.rsqrt(jnp.mean(x * x, axis=-1, keepdims=True) + RMS_EPS) * g

ODD_OFF = dict(qc=0, kc=512, vc=640, qi=768, ki=1024, wi=1152, cq=1280, ckv=1536, kra=1664, krb=1792, end=1920)


def _odd_proj_kernel(x_ref, w_ref, qng_ref, kvng_ref, wqa_ref, wqb_ref, wk_ref, wv_ref, cos_ref, sin_ref,
                     tq_ref, tk_ref,
                     qc_ref, kc_ref, vc_ref, qi_ref, ki_ref, wi_ref, q8_ref, k8_ref, v8_ref, *, q_scale):
    o = ODD_OFF
    tm = x_ref.shape[0]
    lo = _lane_lo((tm, LANE))
    h = _dot(x_ref[...].astype(BF16), w_ref[...])
    kdup = h[:, o["kc"]:o["vc"]]
    vdup = h[:, o["vc"]:o["qi"]]
    kc_ref[:, 0:LANE] = jnp.where(lo, kdup, tk_ref[:, 0:LANE].astype(F32)).astype(BF16)
    kc_ref[:, LANE:2 * LANE] = jnp.where(lo, tk_ref[:, LANE:2 * LANE].astype(F32), kdup).astype(BF16)
    vc_ref[:, 0:LANE] = jnp.where(lo, vdup, 1.0).astype(BF16)
    vc_ref[:, LANE:2 * LANE] = jnp.where(lo, 1.0, vdup).astype(BF16)
    qi_ref[...] = h[:, o["qi"]:o["ki"]].astype(BF16)
    ki_ref[...] = h[:, o["ki"]:o["wi"]].astype(BF16)
    wi_ref[...] = h[:, o["wi"]:o["wi"] + IDX_HEADS]
    cos = cos_ref[...]
    sin = sin_ref[...]
    cqn = _rms(h[:, o["cq"]:o["ckv"]], qng_ref[...]).astype(BF16)
    qa = _dot(cqn, wqa_ref[...])
    qb = _dot(cqn, wqb_ref[...])
    ckvn = _rms(h[:, o["ckv"]:o["kra"]], kvng_ref[...]).astype(BF16)
    kn = _dot(ckvn, wk_ref[...])
    v8 = _dot(ckvn, wv_ref[...])
    kr = h[:, o["kra"]:o["krb"]] * cos + h[:, o["krb"]:o["end"]] * sin
    for hd in range(D_HEADS):
        sl = slice(hd * LANE, (hd + 1) * LANE)
        pj = slice((hd // 2) * LANE, (hd // 2 + 1) * LANE)
        own = lo if hd % 2 == 0 else ~lo
        qc_ref[:, sl] = jnp.where(own, h[:, pj], tq_ref[:, sl].astype(F32)).astype(BF16)
        q8_ref[:, sl] = ((qa[:, sl] * cos + qb[:, sl] * sin) * q_scale).astype(BF16)
        k8_ref[:, sl] = (kn[:, sl] + kr).astype(BF16)
        v8_ref[:, sl] = jnp.where(own, v8[:, pj], 1.0).astype(BF16)


def _odd_proj(xt, w_cat, qng, kvng, wqa, wqb, wk, wv, cos_t, sin_t, tq_t, tk_t, seq, tm):
    n, d = xt.shape
    full = lambda i: (0, 0)
    tps = seq // tm
    pos = lambda i: (i % tps, 0)
    widths = (1024, 256, 256, 256, 128)
    out_shape = [jax.ShapeDtypeStruct((n, w), BF16) for w in widths]
    out_specs = [pl.BlockSpec((tm, w), lambda i: (i, 0)) for w in widths]
    out_shape.append(jax.ShapeDtypeStruct((n, IDX_HEADS), F32))
    out_specs.append(pl.BlockSpec((tm, IDX_HEADS), lambda i: (i, 0)))
    for w in (1024, 1024, 1024):
        out_shape.append(jax.ShapeDtypeStruct((n, w), BF16))
        out_specs.append(pl.BlockSpec((tm, w), lambda i: (i, 0)))
    return pl.pallas_call(
        functools.partial(_odd_proj_kernel, q_scale=(MLA_NOPE + MLA_ROPE) ** -0.5 * LOG2E),
        out_shape=out_shape,
        grid=(n // tm,),
        in_specs=[pl.BlockSpec((tm, d), lambda i: (i, 0)),
                  pl.BlockSpec(w_cat.shape, full),
                  pl.BlockSpec(qng.shape, full), pl.BlockSpec(kvng.shape, full),
                  pl.BlockSpec(wqa.shape, full), pl.BlockSpec(wqb.shape, full),
                  pl.BlockSpec(wk.shape, full), pl.BlockSpec(wv.shape, full),
                  pl.BlockSpec((tm, LANE), pos), pl.BlockSpec((tm, LANE), pos),
                  pl.BlockSpec((tm, C_HEADS * LANE), pos), pl.BlockSpec((tm, 2 * LANE), pos)],
        out_specs=out_specs,
        compiler_params=_cparams(("parallel",)),
        name="odd_proj",
    )(xt, w_cat, qng, kvng, wqa, wqb, wk, wv, cos_t, sin_t, tq_t, tk_t)


def _dsa_kernel(q_ref, k_ref, v_ref, qi_ref, ki_ref, wt_ref, o_ref, key_ref, mb_ref, *, top_k, chunk):
    i = pl.program_id(1)
    nc = (i * BLOCK + BLOCK + chunk - 1) // chunk
    lo = _lane_lo((BLOCK, LANE))
    s_row = lax.broadcasted_iota(I32, (chunk, BLOCK), 0)
    t_lane = i * BLOCK + lax.broadcasted_iota(I32, (chunk, BLOCK), 1)

    def chunk_start(c):
        return pl.multiple_of(c * chunk, chunk)

    qidx = []
    for h in range(IDX_HEADS):
        qp = qi_ref[:, (h // 2) * LANE:(h // 2 + 1) * LANE]
        qidx.append(jnp.where(lo if h % 2 == 0 else ~lo, qp, jnp.zeros_like(qp)))
    q_stack = jnp.concatenate(qidx, axis=0)
    w = wt_ref[...]

    npair = (nc + 1) // 2

    def score_pair(cp, _):
        for u in range(2):
            start = chunk_start(2 * cp + u)
            d = _dot_nt(ki_ref[pl.ds(start, chunk), :], q_stack)
            sc = jnp.zeros((chunk, BLOCK), F32)
            for h in range(IDX_HEADS):
                sc = sc + w[h:h + 1, :] * jnp.maximum(d[:, h * BLOCK:(h + 1) * BLOCK], 0.0)
            sc = jnp.where(sc == 0.0, 0.0, sc)
            sc = jnp.where(start + s_row <= t_lane, sc, NEG_INF)
            bits = pltpu.bitcast(sc, I32)
            key_ref[pl.ds(start, chunk), :] = jnp.where(bits < 0, bits ^ jnp.int32(0x7FFFFFFF), bits)
        return 0

    lax.fori_loop(0, npair, score_pair, 0)

    nacc = 4 * 8

    def count(trial, strict):
        def body(cp, acc):
            for u in range(2):
                key = key_ref[pl.ds(chunk_start(2 * cp + u), chunk), :]
                hit = (key > trial) if strict else (key >= trial)
                acc = acc + jnp.sum(jnp.where(hit, 1.0, 0.0).reshape(chunk // nacc, nacc, BLOCK), axis=0)
            return acc

        acc = lax.fori_loop(0, npair, body, jnp.zeros((nacc, BLOCK), F32))
        return jnp.sum(acc, axis=0, keepdims=True)

    kf = float(top_k)
    cand = jnp.where(count(jnp.zeros((1, BLOCK), I32), False) >= kf, 0, jnp.int32(-2 ** 31)).astype(I32)

    def bit_step(j, cand):
        trial = cand + lax.shift_left(jnp.int32(1), 30 - j)
        return jnp.where(count(trial, False) >= kf, trial, cand)

    thr = lax.fori_loop(0, 31, bit_step, cand)
    need = kf - count(thr, True)

    r0 = lax.broadcasted_iota(I32, (chunk, chunk), 0)
    r1 = lax.broadcasted_iota(I32, (chunk, chunk), 1)
    below = jnp.where(r1 < r0, 1.0, 0.0).astype(BF16)

    def mask_pair(cp, offset):
        for u in range(2):
            start = chunk_start(2 * cp + u)
            key = key_ref[pl.ds(start, chunk), :]
            eqf = jnp.where(key == thr, 1.0, 0.0)
            prefix = _dot(below, eqf.astype(BF16)) + offset
            sel = jnp.where(key > thr, 1.0, jnp.where(prefix < need, eqf, 0.0))
            keep = jnp.where(start + s_row <= t_lane, sel, 0.0)
            mb_ref[:, pl.ds(start, chunk)] = jnp.where(keep > 0.0, 0.0, NEG_INF).T
            offset = offset + jnp.sum(eqf, axis=0, keepdims=True)
        return offset

    lax.fori_loop(0, npair, mask_pair, jnp.zeros((1, BLOCK), F32))

    def att_chunk(c, carry):
        start = chunk_start(c)
        qs = [q_ref[:, h * LANE:(h + 1) * LANE] for h in range(C_HEADS)]
        ks = [k_ref[pl.ds(start, chunk), (h % 2) * LANE:(h % 2 + 1) * LANE] for h in range(C_HEADS)]
        vs = [v_ref[pl.ds(start, chunk), (h % 2) * LANE:(h % 2 + 1) * LANE] for h in range(C_HEADS)]
        return _online_softmax_chains(qs, ks, vs, carry, bias=mb_ref[:, pl.ds(start, chunk)], guard_empty=True)

    init = tuple((jnp.full((BLOCK, 1), NEG_INF, F32), jnp.zeros((BLOCK, LANE), F32)) for _ in range(C_HEADS))
    carry = lax.fori_loop(0, nc, att_chunk, init)
    _normalize_pairs([c[1] for c in carry], o_ref)


def _dsa(qc, kc, vc, qi, ki, wit):
    b, s, _ = qc.shape
    top_k = min(DSA_TOPK_MAX, s // 4)
    chunk = 256
    assert s % (2 * chunk) == 0 and top_k <= chunk
    blk = lambda bi, i: (bi, i, 0)
    seq = lambda bi, i: (bi, 0, 0)
    return pl.pallas_call(
        functools.partial(_dsa_kernel, top_k=top_k, chunk=chunk),
        out_shape=jax.ShapeDtypeStruct((b, s, C_HEADS * HEAD_DIM), BF16),
        grid=(b, s // BLOCK),
        in_specs=[pl.BlockSpec((None, BLOCK, C_HEADS * LANE), blk),
                  pl.BlockSpec((None, s, 2 * LANE), seq),
                  pl.BlockSpec((None, s, 2 * LANE), seq),
                  pl.BlockSpec((None, BLOCK, 256), blk),
                  pl.BlockSpec((None, s, LANE), seq),
                  pl.BlockSpec((None, IDX_HEADS, BLOCK), lambda bi, i: (bi, 0, i))],
        out_specs=pl.BlockSpec((None, BLOCK, 512), blk),
        scratch_shapes=[pltpu.VMEM((s, BLOCK), I32), pltpu.VMEM((BLOCK, s), F32)],
        compiler_params=_cparams(("parallel", "arbitrary")),
        name="dsa_attn",
    )(qc, kc, vc, qi, ki, wit)


def _dup64(w):
    d, c = w.shape
    w = w.reshape(d, c // HEAD_DIM, 1, HEAD_DIM)
    return jnp.broadcast_to(w, (d, c // HEAD_DIM, 2, HEAD_DIM)).reshape(d, 2 * c)


def _pad_cols(w, width):
    return jnp.pad(w, ((0, 0), (0, width - w.shape[1])))


def _split_cols(w, sizes):
    out, start = [], 0
    for n in sizes:
        out.append(w[:, start:start + n])
        start += n
    return out


def _rope_partner(w):
    half = MLA_ROPE // 2
    return jnp.concatenate([-w[..., half:], w[..., :half]], axis=-1)


def _rope_block(w_rope):
    d = w_rope.shape[0]
    return jnp.concatenate([jnp.zeros((d, MLA_NOPE), F32), w_rope,
                            jnp.zeros((d, LANE - MLA_NOPE - MLA_ROPE), F32)], axis=1)


def _alibi_tables(s):
    slopes = 2.0 ** (-8.0 * jnp.arange(1, C_HEADS + 1, dtype=F32) / C_HEADS)
    a = slopes * LOG2E
    pos = jnp.arange(s, dtype=F32)
    a_sp = _split3(a)
    r_sp = _split3(-(a[None, :] * pos[:, None]))
    ones = jnp.ones((s, C_HEADS), F32)
    q_terms = [128.0 * t * ones for t in a_sp] + [t * ones for t in a_sp] + list(r_sp)
    q_aug = jnp.stack(q_terms, axis=-1)
    q_aug = jnp.pad(q_aug, ((0, 0), (0, 0), (0, HEAD_DIM - q_aug.shape[-1])))
    zeros = jnp.zeros_like(q_aug)
    even = jnp.concatenate([zeros, q_aug], axis=-1)
    odd = jnp.concatenate([q_aug, zeros], axis=-1)
    parity = (jnp.arange(C_HEADS) % 2 == 0)[None, :, None]
    tq_t = jnp.where(parity, even, odd).reshape(s, C_HEADS * LANE)
    s_hi = jnp.floor(pos / LANE)
    s_lo = pos - LANE * s_hi
    k_aug = jnp.stack([s_hi] * 3 + [s_lo] * 3 + [jnp.ones_like(pos)] * 3, axis=-1)
    k_aug = jnp.pad(k_aug, ((0, 0), (0, HEAD_DIM - k_aug.shape[-1])))
    kz = jnp.zeros_like(k_aug)
    tk_t = jnp.concatenate([kz, k_aug, k_aug, kz], axis=-1)
    return tq_t.astype(BF16), tk_t.astype(BF16)


def kernel(x, even_w_in, even_b_f, even_sinks, odd_w_in, odd_q_norm, odd_kv_norm, odd_w_uq, odd_w_ukv,
           w_o, ln_g, ln_b, router_w, router_b, moe_w_gate, moe_w_up, moe_w_down):
    b, s, d = x.shape
    n = b * s
    depth = w_o.shape[0]
    alpha = (2 * depth) ** 0.25
    tm = min(512, s)
    tq = min(256, s)
    tm_e = min(256, n)
    tm_c = min(256, n)
    att_scale = HEAD_DIM ** -0.5
    xt = x.reshape(n, d)
    rwt = router_w.T
    rb = router_b.reshape(N_EXPERTS, 1)

    for l in range(depth):
        j = l // 2
        if l % 2 == 0:
            qa, ka, va, qb, kb, vb, fg = _split_cols(
                even_w_in[j], (512, 128, 128, 512, 512, 512, B_HEADS))
            w_cat = jnp.concatenate([qa * att_scale, _dup64(ka), _dup64(va), qb * (att_scale * LOG2E), kb, vb,
                                     _pad_cols(fg, LANE)], axis=1).astype(BF16)
            bf_pad = _pad_cols(even_b_f[j].reshape(1, B_HEADS), LANE)
            qa_, ka_, va_, qf, kf, vf = _even_proj(xt, w_cat, bf_pad, s, tm)
            r3 = lambda a: a.reshape(b, s, a.shape[1])
            out_a = _swa(even_sinks[j], r3(qa_), r3(ka_), r3(va_))
            out_b = _flash(r3(qf), r3(kf), r3(vf), tq=tq, nh=4, name="fox_attn")
        else:
            (qc, kc, vc, qi, ki, wi, cq, ckv, kr) = _split_cols(
                odd_w_in[j], (512, 64, 64, IDX_HEADS * IDX_DIM, IDX_DIM, IDX_HEADS,
                              MLA_Q_RANK, MLA_KV_RANK, MLA_ROPE))
            w_cat = jnp.concatenate([
                qc * (att_scale * LOG2E), _dup64(kc), _dup64(vc), qi * (IDX_DIM ** -0.5), _dup64(ki),
                _pad_cols(wi * (IDX_HEADS ** -0.5), LANE), cq, ckv,
                _rope_block(kr), _rope_block(_rope_partner(kr))], axis=1).astype(BF16)
            dq = MLA_NOPE + MLA_ROPE
            wuq = odd_w_uq[j].reshape(MLA_Q_RANK, D_HEADS, dq)
            pad_q = jnp.zeros((MLA_Q_RANK, D_HEADS, LANE - dq), F32)
            wqa = jnp.concatenate([wuq, pad_q], axis=2).reshape(MLA_Q_RANK, D_HEADS * LANE).astype(BF16)
            wqb = jnp.concatenate([jnp.zeros((MLA_Q_RANK, D_HEADS, MLA_NOPE), F32),
                                   _rope_partner(wuq[..., MLA_NOPE:]), pad_q],
                                  axis=2).reshape(MLA_Q_RANK, D_HEADS * LANE).astype(BF16)
            wukv = odd_w_ukv[j].reshape(MLA_KV_RANK, D_HEADS, MLA_NOPE + MLA_V)
            wk = jnp.concatenate([wukv[..., :MLA_NOPE], jnp.zeros((MLA_KV_RANK, D_HEADS, LANE - MLA_NOPE), F32)],
                                 axis=2).reshape(MLA_KV_RANK, D_HEADS * LANE).astype(BF16)
            wv = wukv[..., MLA_NOPE:].reshape(MLA_KV_RANK, D_HEADS * MLA_V).astype(BF16)
            half = MLA_ROPE // 2
            inv = ROPE_THETA ** (-jnp.arange(half, dtype=F32) / half)
            ang = jnp.arange(s, dtype=F32)[:, None] * inv[None, :]
            cos_h, sin_h = jnp.cos(ang), jnp.sin(ang)
            pad_t = jnp.zeros((s, LANE - dq), F32)
            cos_t = jnp.concatenate([jnp.ones((s, MLA_NOPE), F32), cos_h, cos_h, pad_t], axis=1)
            sin_t = jnp.concatenate([jnp.zeros((s, MLA_NOPE), F32), sin_h, sin_h, pad_t], axis=1)
            tq_t, tk_t = _alibi_tables(s)
            qc_, kc_, vc_, qi_, ki_, wi_, q8, k8, v8 = _odd_proj(
                xt, w_cat, odd_q_norm[j].reshape(1, -1), odd_kv_norm[j].reshape(1, -1),
                wqa, wqb, wk, wv, cos_t, sin_t, tq_t, tk_t, s, tm)
            r3 = lambda a: a.reshape(b, s, a.shape[1])
            out_a = _dsa(r3(qc_), r3(kc_), r3(vc_), r3(qi_), r3(ki_), r3(wi_).transpose(0, 2, 1))
            out_b = _flash(r3(q8), r3(k8), r3(v8), tq=tq, nh=4, name="mla_attn")
        half_w = out_a.shape[2]
        wo = w_o[l].astype(BF16)
        x1, route_t = _outproj(out_a.reshape(n, half_w), out_b.reshape(n, -1), xt,
                               wo[:half_w], wo[half_w:], ln_g[l, 0].reshape(1, d), ln_b[l, 0].reshape(1, d),
                               rwt, rb, alpha, tm)
        xt = _moe_layer(x1, route_t, moe_w_gate[l].astype(BF16), moe_w_up[l].astype(BF16),
                        moe_w_down[l].astype(BF16), ln_g[l, 1].reshape(1, d), ln_b[l, 1].reshape(1, d),
                        alpha, tm_e, tm_c)
    return xt.reshape(b, s, d)
```

```python
import functools
import math

import jax
import jax.numpy as jnp
from jax import lax
from jax.experimental import pallas as pl
from jax.experimental.pallas import tpu as pltpu

F32 = jnp.float32
BF16 = jnp.bfloat16
I32 = jnp.int32

LANE = 128
HEAD_DIM = 64
BLOCK = 128
A_HEADS, A_KV_HEADS, WINDOW = 8, 2, 128
B_HEADS = 8
C_HEADS, IDX_HEADS, IDX_DIM, DSA_TOPK_MAX = 8, 4, 64, 256
D_HEADS, MLA_Q_RANK, MLA_KV_RANK, MLA_NOPE, MLA_ROPE, MLA_V = 8, 256, 128, 64, 32, 64
ROPE_THETA = 10000.0
N_EXPERTS, N_GROUPS = 16, 4
EXPERTS_PER_GROUP = N_EXPERTS // N_GROUPS
LN_EPS, RMS_EPS = 1e-5, 1e-6
NEG_INF = float("-inf")
LOG2E = math.log2(math.e)
VMEM_LIMIT = 56 * 1024 * 1024

HIGHEST = lax.Precision.HIGHEST


def _cparams(sem):
    return pltpu.CompilerParams(dimension_semantics=sem, vmem_limit_bytes=VMEM_LIMIT)


def _dot(a, b):
    return jnp.dot(a, b, preferred_element_type=F32)


def _dot_nt(a, b):
    return lax.dot_general(a, b, (((1,), (1,)), ((), ())), preferred_element_type=F32)


def _lane_lo(shape):
    return lax.broadcasted_iota(I32, shape, len(shape) - 1) % LANE < HEAD_DIM


def _layer_norm(y, g, b):
    mu = jnp.mean(y, axis=-1, keepdims=True)
    yc = y - mu
    var = jnp.mean(yc * yc, axis=-1, keepdims=True)
    return yc * lax.rsqrt(var + LN_EPS) * g + b


def _split3(x):
    def top(v):
        bits = lax.bitcast_convert_type(v, I32) & jnp.int32(-65536)
        return lax.bitcast_convert_type(bits, F32)

    s1 = top(x)
    r = x - s1
    s2 = top(r)
    return s1, s2, r - s2


def _aug_rel(shape, h):
    lane = lax.broadcasted_iota(I32, shape, 1)
    return lane - (HEAD_DIM if h % 2 == 0 else 0)


def _even_proj_kernel(x_ref, w_ref, bf_ref, qa_ref, ka_ref, va_ref, qf_ref, kf_ref, vf_ref,
                      carry_ref, *, tiles_per_seq):
    t = pl.program_id(0)
    tm = x_ref.shape[0]
    h = _dot(x_ref[...].astype(BF16), w_ref[...])
    qa_ref[...] = h[:, 0:512].astype(BF16)
    ka_ref[...] = h[:, 512:768].astype(BF16)
    va_ref[...] = h[:, 768:1024].astype(BF16)
    z = h[:, 2560:2688] + bf_ref[...]
    logf = jnp.minimum(z, 0.0) - jnp.log1p(jnp.exp(-jnp.abs(z)))

    @pl.when(t % tiles_per_seq == 0)
    def _():
        carry_ref[...] = jnp.zeros_like(carry_ref)

    row = lax.broadcasted_iota(I32, (tm, tm), 0)
    col = lax.broadcasted_iota(I32, (tm, tm), 1)
    tri = jnp.where(row >= col, 1.0, 0.0).astype(F32)
    c = jnp.dot(tri, logf, precision=HIGHEST, preferred_element_type=F32) + carry_ref[...]
    carry_ref[...] = c[tm - 1:tm, :]

    s1, s2, s3 = _split3(c * LOG2E)
    lo = _lane_lo((tm, LANE))
    for hd in range(B_HEADS):
        own = lo if hd % 2 == 0 else ~lo
        rel = _aug_rel((tm, LANE), hd)
        b1, b2, b3 = s1[:, hd:hd + 1], s2[:, hd:hd + 1], s3[:, hd:hd + 1]
        q_aug = jnp.where(rel == 3, b1, jnp.where(rel == 4, b2, jnp.where(rel == 5, b3,
                          jnp.where((rel >= 0) & (rel < 3), -1.0, 0.0))))
        k_aug = jnp.where(rel == 0, b1, jnp.where(rel == 1, b2, jnp.where(rel == 2, b3,
                          jnp.where((rel >= 3) & (rel < 6), 1.0, 0.0))))
        pj = (hd // 2) * LANE
        sl = slice(hd * LANE, (hd + 1) * LANE)
        qf_ref[:, sl] = jnp.where(own, h[:, 1024 + pj:1024 + pj + LANE], q_aug).astype(BF16)
        kf_ref[:, sl] = jnp.where(own, h[:, 1536 + pj:1536 + pj + LANE], k_aug).astype(BF16)
        vf_ref[:, sl] = jnp.where(own, h[:, 2048 + pj:2048 + pj + LANE], 1.0).astype(BF16)


def _even_proj(xt, w_cat, bf_pad, seq, tm):
    n, d = xt.shape
    wcols = w_cat.shape[1]
    widths = (512, 256, 256, 1024, 1024, 1024)
    out_shape = [jax.ShapeDtypeStruct((n, w), BF16) for w in widths]
    out_specs = [pl.BlockSpec((tm, w), lambda i: (i, 0)) for w in widths]
    return pl.pallas_call(
        functools.partial(_even_proj_kernel, tiles_per_seq=seq // tm),
        out_shape=out_shape,
        grid=(n // tm,),
        in_specs=[pl.BlockSpec((tm, d), lambda i: (i, 0)),
                  pl.BlockSpec((d, wcols), lambda i: (0, 0)),
                  pl.BlockSpec((1, LANE), lambda i: (0, 0))],
        out_specs=out_specs,
        scratch_shapes=[pltpu.VMEM((1, LANE), F32)],
        compiler_params=_cparams(("arbitrary",)),
        name="even_proj",
    )(xt, w_cat, bf_pad)


def _swa_kernel(sink_ref, q_ref, kp_ref, kc_ref, vp_ref, vc_ref, o_ref):
    i = pl.program_id(1)
    group = A_HEADS // A_KV_HEADS
    tl = lax.broadcasted_iota(I32, (BLOCK, 2 * BLOCK), 0)
    sl = lax.broadcasted_iota(I32, (BLOCK, 2 * BLOCK), 1)
    dist = BLOCK + tl - sl
    valid = (dist >= 0) & (dist < WINDOW) & ((sl >= BLOCK) | (i > 0))
    distf = dist.astype(F32)
    lo = _lane_lo((BLOCK, LANE))
    outs = []
    for h in range(A_HEADS):
        g = h // group
        qp = q_ref[:, (h // 2) * LANE:(h // 2 + 1) * LANE]
        qh = jnp.where(lo if h % 2 == 0 else ~lo, qp, jnp.zeros_like(qp))
        kband = jnp.concatenate([kp_ref[:, g * LANE:(g + 1) * LANE],
                                 kc_ref[:, g * LANE:(g + 1) * LANE]], axis=0)
        vband = jnp.concatenate([vp_ref[:, g * LANE:(g + 1) * LANE],
                                 vc_ref[:, g * LANE:(g + 1) * LANE]], axis=0)
        slope = 2.0 ** (-8.0 * (h + 1) / A_HEADS)
        logits = _dot_nt(qh, kband) - slope * distf
        logits = jnp.where(valid, logits, NEG_INF)
        sink = sink_ref[h]
        m = jnp.maximum(jnp.max(logits, axis=-1, keepdims=True), sink)
        e = jnp.exp(logits - m)
        denom = jnp.sum(e, axis=-1, keepdims=True) + jnp.exp(sink - m)
        outs.append(_dot(e.astype(BF16), vband) / denom)
    for j in range(A_HEADS // 2):
        o_ref[:, j * LANE:(j + 1) * LANE] = jnp.where(lo, outs[2 * j], outs[2 * j + 1]).astype(BF16)


def _swa(sinks, qa, ka, va):
    b, s, _ = qa.shape
    nb = s // BLOCK
    prev = lambda bi, i: (bi, jnp.maximum(i - 1, 0), 0)
    cur = lambda bi, i: (bi, i, 0)
    return pl.pallas_call(
        _swa_kernel,
        out_shape=jax.ShapeDtypeStruct((b, s, A_HEADS * HEAD_DIM), BF16),
        grid=(b, nb),
        in_specs=[pl.BlockSpec(memory_space=pltpu.SMEM),
                  pl.BlockSpec((None, BLOCK, 512), cur),
                  pl.BlockSpec((None, BLOCK, 256), prev),
                  pl.BlockSpec((None, BLOCK, 256), cur),
                  pl.BlockSpec((None, BLOCK, 256), prev),
                  pl.BlockSpec((None, BLOCK, 256), cur)],
        out_specs=pl.BlockSpec((None, BLOCK, 512), cur),
        compiler_params=_cparams(("parallel", "arbitrary")),
        name="swa_attn",
    )(sinks, qa, ka, ka, va, va)


def _normalize_pairs(accs, o_ref):
    lo = _lane_lo(accs[0].shape)
    outs = [a / pltpu.roll(a, HEAD_DIM, 1) for a in accs]
    for j in range(len(accs) // 2):
        o_ref[:, j * LANE:(j + 1) * LANE] = jnp.where(lo, outs[2 * j], outs[2 * j + 1]).astype(BF16)


def _online_softmax_chains(qs, ks, vs, carry, bias=None, masks=None, guard_empty=False):
    n = len(qs)
    ss = [_dot_nt(qs[i], ks[i]) for i in range(n)]
    if bias is not None:
        ss = [s + bias for s in ss]
    if masks is not None:
        ss = [jnp.where(masks[i], ss[i], NEG_INF) for i in range(n)]
    ms, ps, alphas = [], [], []
    for i in range(n):
        m = carry[i][0]
        m_new = jnp.maximum(m, jnp.max(ss[i], axis=-1, keepdims=True))
        m_ref = jnp.where(m_new == NEG_INF, 0.0, m_new) if guard_empty else m_new
        ps.append(jnp.exp2(ss[i] - m_ref).astype(BF16))
        alphas.append(jnp.exp2(m - m_ref))
        ms.append(m_new)
    pvs = [_dot(ps[i], vs[i]) for i in range(n)]
    return tuple((ms[i], alphas[i] * carry[i][1] + pvs[i]) for i in range(n))


def _flash_kernel(q_ref, k_ref, v_ref, o_ref, *, tq, nh):
    qi = pl.program_id(2)
    nslab = tq // BLOCK
    chains = [(hh, r) for hh in range(nh) for r in range(nslab)]
    row = lax.broadcasted_iota(I32, (BLOCK, tq), 0)
    col = lax.broadcasted_iota(I32, (BLOCK, tq), 1)

    def lanes(hh):
        return slice(hh * LANE, (hh + 1) * LANE)

    def step(kb, carry, diagonal):
        start = pl.multiple_of(kb * tq, tq)
        qs = [q_ref[r * BLOCK:(r + 1) * BLOCK, lanes(hh)] for hh, r in chains]
        ks = [k_ref[pl.ds(start, tq), lanes(hh)] for hh, r in chains]
        vs = [v_ref[pl.ds(start, tq), lanes(hh)] for hh, r in chains]
        masks = [col <= row + r * BLOCK for hh, r in chains] if diagonal else None
        return _online_softmax_chains(qs, ks, vs, carry, masks=masks)

    init = tuple((jnp.full((BLOCK, 1), NEG_INF, F32), jnp.zeros((BLOCK, LANE), F32))
                 for _ in range(nh * nslab))
    carry = lax.fori_loop(0, qi, lambda kb, c: step(kb, c, False), init)
    carry = step(qi, carry, True)
    accs = [jnp.concatenate([carry[hh * nslab + r][1] for r in range(nslab)], axis=0) for hh in range(nh)]
    _normalize_pairs(accs, o_ref)


def _flash(q, k, v, *, tq, nh, name):
    b, s, w = v.shape
    return pl.pallas_call(
        functools.partial(_flash_kernel, tq=tq, nh=nh),
        out_shape=jax.ShapeDtypeStruct((b, s, w // 2), BF16),
        grid=(b, w // (nh * LANE), s // tq),
        in_specs=[pl.BlockSpec((None, tq, nh * LANE), lambda bi, j, i: (bi, i, j)),
                  pl.BlockSpec((None, s, nh * LANE), lambda bi, j, i: (bi, 0, j)),
                  pl.BlockSpec((None, s, nh * LANE), lambda bi, j, i: (bi, 0, j))],
        out_specs=pl.BlockSpec((None, tq, nh * LANE // 2), lambda bi, j, i: (bi, i, j)),
        compiler_params=_cparams(("parallel", "parallel", "arbitrary")),
        name=name,
    )(q, k, v)


def _route_rows(aff, sel):
    gs = []
    for g in range(N_GROUPS):
        r = sel[g * EXPERTS_PER_GROUP:(g + 1) * EXPERTS_PER_GROUP]
        best = None
        for a in range(EXPERTS_PER_GROUP):
            for c in range(a + 1, EXPERTS_PER_GROUP):
                pair = r[a] + r[c]
                best = pair if best is None else jnp.maximum(best, pair)
        gs.append(best)
    gmax = functools.reduce(jnp.maximum, gs)
    gidx = jnp.full_like(gmax, float(N_GROUPS - 1))
    for g in range(N_GROUPS - 2, -1, -1):
        gidx = jnp.where(gs[g] == gmax, float(g), gidx)

    def pick(rows, k):
        out = rows[(N_GROUPS - 1) * EXPERTS_PER_GROUP + k]
        for g in range(N_GROUPS - 2, -1, -1):
            out = jnp.where(gidx == float(g), rows[g * EXPERTS_PER_GROUP + k], out)
        return out

    s = [pick(sel, k) for k in range(EXPERTS_PER_GROUP)]
    a = [pick(aff, k) for k in range(EXPERTS_PER_GROUP)]

    def first_max(vals):
        mx = functools.reduce(jnp.maximum, vals)
        idx = jnp.full_like(mx, float(len(vals) - 1))
        for k in range(len(vals) - 2, -1, -1):
            idx = jnp.where(vals[k] == mx, float(k), idx)
        return idx

    def take(vals, idx):
        out = vals[-1]
        for k in range(len(vals) - 2, -1, -1):
            out = jnp.where(idx == float(k), vals[k], out)
        return out

    i1 = first_max(s)
    s2 = [jnp.where(i1 == float(k), NEG_INF, s[k]) for k in range(EXPERTS_PER_GROUP)]
    i2 = first_max(s2)
    a1, a2 = take(a, i1), take(a, i2)
    tot = a1 + a2
    base = gidx * float(EXPERTS_PER_GROUP)
    return base + i1, base + i2, a1 / tot, a2 / tot


def _outproj_kernel(a_ref, b_ref, x_ref, wa_ref, wb_ref, g_ref, be_ref, rwt_ref, rb_ref,
                    x1_ref, route_ref, *, alpha):
    mix = _dot(a_ref[...], wa_ref[...]) + _dot(b_ref[...], wb_ref[...])
    x1 = _layer_norm(alpha * x_ref[...] + mix, g_ref[...], be_ref[...])
    x1_ref[...] = x1
    zt = lax.dot_general(rwt_ref[...], x1, (((1,), (1,)), ((), ())),
                         precision=HIGHEST, preferred_element_type=F32)
    afft = 1.0 / (1.0 + jnp.exp(-zt))
    selt = afft + rb_ref[...]
    aff = [afft[e:e + 1, :] for e in range(N_EXPERTS)]
    sel = [selt[e:e + 1, :] for e in range(N_EXPERTS)]
    e1, e2, w1, w2 = _route_rows(aff, sel)
    zero = jnp.zeros_like(e1)
    route_ref[...] = jnp.concatenate([e1, e2, w1, w2, zero, zero, zero, zero], axis=0)


def _outproj(a, b, xt, wa, wb, g, be, rwt, rb, alpha, tm):
    n, d = xt.shape
    full = lambda i: (0, 0)
    return pl.pallas_call(
        functools.partial(_outproj_kernel, alpha=alpha),
        out_shape=[jax.ShapeDtypeStruct((n, d), F32), jax.ShapeDtypeStruct((8, n), F32)],
        grid=(n // tm,),
        in_specs=[pl.BlockSpec((tm, a.shape[1]), lambda i: (i, 0)),
                  pl.BlockSpec((tm, b.shape[1]), lambda i: (i, 0)),
                  pl.BlockSpec((tm, d), lambda i: (i, 0)),
                  pl.BlockSpec(wa.shape, full), pl.BlockSpec(wb.shape, full),
                  pl.BlockSpec((1, d), full), pl.BlockSpec((1, d), full),
                  pl.BlockSpec(rwt.shape, full), pl.BlockSpec(rb.shape, full)],
        out_specs=[pl.BlockSpec((tm, d), lambda i: (i, 0)), pl.BlockSpec((8, tm), lambda i: (0, i))],
        compiler_params=_cparams(("parallel",)),
        name="outproj_ln_route",
    )(a, b, xt, wa, wb, g, be, rwt, rb)


def _moe_kernel(bexp_ref, bbase_ref, nact_ref, src_ref,
                x_hbm, wg_ref, wu_ref, wd_ref, o_ref, xbuf, sem, *, tm):
    i = pl.program_id(0)
    nact = nact_ref[0]
    last = src_ref.shape[0] - 1

    def issue(blk, slot):
        base = bbase_ref[blk]
        for r in range(tm):
            tok = src_ref[jnp.minimum(base + r, last)]
            pltpu.make_async_copy(x_hbm.at[pl.ds(tok, 1), :], xbuf.at[slot, pl.ds(r, 1), :],
                                  sem.at[slot]).start()

    def wait(slot):
        pltpu.make_async_copy(x_hbm.at[pl.ds(0, tm), :], xbuf.at[slot], sem.at[slot]).wait()

    @pl.when((i == 0) & (nact > 0))
    def _():
        issue(0, 0)

    @pl.when(i < nact)
    def _():
        slot = i % 2
        wait(slot)
        xb = xbuf[slot].astype(BF16)
        issue(jnp.minimum(i + 1, nact - 1), 1 - slot)
        gate = _dot(xb, wg_ref[...])
        up = _dot(xb, wu_ref[...])
        h = gate * (1.0 / (1.0 + jnp.exp(-gate))) * up
        o_ref[...] = _dot(h.astype(BF16), wd_ref[...])

        @pl.when(i + 1 >= nact)
        def _():
            wait(1 - slot)

    @pl.when(i >= nact)
    def _():
        o_ref[...] = jnp.zeros_like(o_ref)


def _moe_experts(x1, wg, wu, wd, bexp, bbase, nact, src, tm):
    n, d = x1.shape
    nblk = bexp.shape[0]
    f = wg.shape[2]
    wmap = lambda i, bexp, *_: (bexp[i], 0, 0)
    grid_spec = pltpu.PrefetchScalarGridSpec(
        num_scalar_prefetch=4,
        grid=(nblk,),
        in_specs=[pl.BlockSpec(memory_space=pl.ANY),
                  pl.BlockSpec((None, d, f), wmap),
                  pl.BlockSpec((None, d, f), wmap),
                  pl.BlockSpec((None, f, d), wmap)],
        out_specs=pl.BlockSpec((tm, d), lambda i, *_: (i, 0)),
        scratch_shapes=[pltpu.VMEM((2, tm, d), F32), pltpu.SemaphoreType.DMA((2,))],
    )
    return pl.pallas_call(
        functools.partial(_moe_kernel, tm=tm),
        out_shape=jax.ShapeDtypeStruct((nblk * tm, d), F32),
        grid_spec=grid_spec,
        compiler_params=_cparams(("arbitrary",)),
        name="moe_experts",
    )(bexp, bbase, nact, src, x1, wg, wu, wd)


def _combine_kernel(p0_ref, p1_ref, x_ref, r_ref, g_ref, be_ref, o_hbm, y_ref, obuf, sem, *, alpha, tm):
    i = pl.program_id(0)
    nsteps = pl.num_programs(0)

    def issue(blk, slot):
        base = blk * tm
        for r in range(tm):
            for k, p_ref in enumerate((p0_ref, p1_ref)):
                pltpu.make_async_copy(o_hbm.at[pl.ds(p_ref[base + r], 1), :],
                                      obuf.at[slot, k, pl.ds(r, 1), :], sem.at[slot]).start()

    def wait(slot):
        for k in range(2):
            pltpu.make_async_copy(o_hbm.at[pl.ds(0, tm), :], obuf.at[slot, k], sem.at[slot]).wait()

    @pl.when(i == 0)
    def _():
        issue(0, 0)

    slot = i % 2
    wait(slot)
    o0 = obuf[slot, 0]
    o1 = obuf[slot, 1]
    issue(jnp.minimum(i + 1, nsteps - 1), 1 - slot)
    r = r_ref[...]
    y = alpha * x_ref[...] + r[:, 2:3] * o0 + r[:, 3:4] * o1
    y_ref[...] = _layer_norm(y, g_ref[...], be_ref[...])

    @pl.when(i + 1 >= nsteps)
    def _():
        wait(1 - slot)


def _combine(x1, route, g, be, o, p0, p1, alpha, tm):
    n, d = x1.shape
    grid_spec = pltpu.PrefetchScalarGridSpec(
        num_scalar_prefetch=2,
        grid=(n // tm,),
        in_specs=[pl.BlockSpec((tm, d), lambda i, *_: (i, 0)),
                  pl.BlockSpec((tm, 8), lambda i, *_: (i, 0)),
                  pl.BlockSpec((1, d), lambda i, *_: (0, 0)),
                  pl.BlockSpec((1, d), lambda i, *_: (0, 0)),
                  pl.BlockSpec(memory_space=pl.ANY)],
        out_specs=pl.BlockSpec((tm, d), lambda i, *_: (i, 0)),
        scratch_shapes=[pltpu.VMEM((2, 2, tm, d), F32), pltpu.SemaphoreType.DMA((2,))],
    )
    return pl.pallas_call(
        functools.partial(_combine_kernel, alpha=alpha, tm=tm),
        out_shape=jax.ShapeDtypeStruct((n, d), F32),
        grid_spec=grid_spec,
        compiler_params=_cparams(("arbitrary",)),
        name="moe_combine_ln",
    )(p0, p1, x1, route, g, be, o)


def _moe_layer(x1, route_t, wg, wu, wd, g, be, alpha, tm_e, tm_c):
    n, d = x1.shape
    route = route_t.T
    e = route[:, 0:2].astype(I32).reshape(2 * n)
    onehot = (e[:, None] == jnp.arange(N_EXPERTS, dtype=I32)[None, :]).astype(I32)
    rank = jnp.take_along_axis(jnp.cumsum(onehot, axis=0) - onehot, e[:, None], axis=1)[:, 0]
    counts = jnp.sum(onehot, axis=0)
    nblk_e = (counts + tm_e - 1) // tm_e
    blk_start = jnp.cumsum(nblk_e) - nblk_e
    seg_start = jnp.cumsum(counts) - counts
    pos = blk_start[e] * tm_e + rank
    sorted_pos = seg_start[e] + rank
    src = jnp.zeros((2 * n,), I32).at[sorted_pos].set(jnp.arange(2 * n, dtype=I32) // 2)
    nblk = (2 * n) // tm_e + N_EXPERTS
    blk = jnp.arange(nblk, dtype=I32)
    blk_end = jnp.cumsum(nblk_e)
    bexp = jnp.minimum(jnp.sum((blk[:, None] >= blk_end[None, :]).astype(I32), axis=1), N_EXPERTS - 1)
    bbase = (seg_start[bexp] + (blk - blk_start[bexp]) * tm_e).astype(I32)
    nact = jnp.sum(nblk_e).astype(I32).reshape(1)
    o = _moe_experts(x1, wg, wu, wd, bexp, bbase, nact, src, tm_e)
    p = pos.reshape(n, 2).astype(I32)
    return _combine(x1, route, g, be, o, p[:, 0], p[:, 1], alpha, tm_c)


def _rms(x, g):
    return x * lax.rsqrt(jnp.mean(x * x, axis=-1, keepdims=True) + RMS_EPS) * g

ODD_OFF = dict(qc=0, kc=512, vc=640, qi=768, ki=1024, wi=1152, cq=1280, ckv=1536, kra=1664, krb=1792, end=1920)


def _odd_proj_kernel(x_ref, w_ref, qng_ref, kvng_ref, wqa_ref, wqb_ref, wk_ref, wv_ref, cos_ref, sin_ref,
                     tq_ref, tk_ref,
                     qc_ref, kc_ref, vc_ref, qi_ref, ki_ref, wi_ref, q8_ref, k8_ref, v8_ref, *, q_scale):
    o = ODD_OFF
    tm = x_ref.shape[0]
    lo = _lane_lo((tm, LANE))
    h = _dot(x_ref[...].astype(BF16), w_ref[...])
    kdup = h[:, o["kc"]:o["vc"]]
    vdup = h[:, o["vc"]:o["qi"]]
    kc_ref[:, 0:LANE] = jnp.where(lo, kdup, tk_ref[:, 0:LANE].astype(F32)).astype(BF16)
    kc_ref[:, LANE:2 * LANE] = jnp.where(lo, tk_ref[:, LANE:2 * LANE].astype(F32), kdup).astype(BF16)
    vc_ref[:, 0:LANE] = jnp.where(lo, vdup, 1.0).astype(BF16)
    vc_ref[:, LANE:2 * LANE] = jnp.where(lo, 1.0, vdup).astype(BF16)
    qi_ref[...] = h[:, o["qi"]:o["ki"]].astype(BF16)
    ki_ref[...] = h[:, o["ki"]:o["wi"]].astype(BF16)
    wi_ref[...] = h[:, o["wi"]:o["wi"] + IDX_HEADS]
    cos = cos_ref[...]
    sin = sin_ref[...]
    cqn = _rms(h[:, o["cq"]:o["ckv"]], qng_ref[...]).astype(BF16)
    qa = _dot(cqn, wqa_ref[...])
    qb = _dot(cqn, wqb_ref[...])
    ckvn = _rms(h[:, o["ckv"]:o["kra"]], kvng_ref[...]).astype(BF16)
    kn = _dot(ckvn, wk_ref[...])
    v8 = _dot(ckvn, wv_ref[...])
    kr = h[:, o["kra"]:o["krb"]] * cos + h[:, o["krb"]:o["end"]] * sin
    for hd in range(D_HEADS):
        sl = slice(hd * LANE, (hd + 1) * LANE)
        pj = slice((hd // 2) * LANE, (hd // 2 + 1) * LANE)
        own = lo if hd % 2 == 0 else ~lo
        qc_ref[:, sl] = jnp.where(own, h[:, pj], tq_ref[:, sl].astype(F32)).astype(BF16)
        q8_ref[:, sl] = ((qa[:, sl] * cos + qb[:, sl] * sin) * q_scale).astype(BF16)
        k8_ref[:, sl] = (kn[:, sl] + kr).astype(BF16)
        v8_ref[:, sl] = jnp.where(own, v8[:, pj], 1.0).astype(BF16)


def _odd_proj(xt, w_cat, qng, kvng, wqa, wqb, wk, wv, cos_t, sin_t, tq_t, tk_t, seq, tm):
    n, d = xt.shape
    full = lambda i: (0, 0)
    tps = seq // tm
    pos = lambda i: (i % tps, 0)
    widths = (1024, 256, 256, 256, 128)
    out_shape = [jax.ShapeDtypeStruct((n, w), BF16) for w in widths]
    out_specs = [pl.BlockSpec((tm, w), lambda i: (i, 0)) for w in widths]
    out_shape.append(jax.ShapeDtypeStruct((n, IDX_HEADS), F32))
    out_specs.append(pl.BlockSpec((tm, IDX_HEADS), lambda i: (i, 0)))
    for w in (1024, 1024, 1024):
        out_shape.append(jax.ShapeDtypeStruct((n, w), BF16))
        out_specs.append(pl.BlockSpec((tm, w), lambda i: (i, 0)))
    return pl.pallas_call(
        functools.partial(_odd_proj_kernel, q_scale=(MLA_NOPE + MLA_ROPE) ** -0.5 * LOG2E),
        out_shape=out_shape,
        grid=(n // tm,),
        in_specs=[pl.BlockSpec((tm, d), lambda i: (i, 0)),
                  pl.BlockSpec(w_cat.shape, full),
                  pl.BlockSpec(qng.shape, full), pl.BlockSpec(kvng.shape, full),
                  pl.BlockSpec(wqa.shape, full), pl.BlockSpec(wqb.shape, full),
                  pl.BlockSpec(wk.shape, full), pl.BlockSpec(wv.shape, full),
                  pl.BlockSpec((tm, LANE), pos), pl.BlockSpec((tm, LANE), pos),
                  pl.BlockSpec((tm, C_HEADS * LANE), pos), pl.BlockSpec((tm, 2 * LANE), pos)],
        out_specs=out_specs,
        compiler_params=_cparams(("parallel",)),
        name="odd_proj",
    )(xt, w_cat, qng, kvng, wqa, wqb, wk, wv, cos_t, sin_t, tq_t, tk_t)


def _dsa_kernel(q_ref, k_ref, v_ref, qi_ref, ki_ref, wt_ref, o_ref, sc_ref, mb_ref, *, top_k, chunk):
    i = pl.program_id(1)
    nc = (i * BLOCK + BLOCK + chunk - 1) // chunk
    lo = _lane_lo((BLOCK, LANE))
    s_row = lax.broadcasted_iota(I32, (chunk, BLOCK), 0)
    t_lane = i * BLOCK + lax.broadcasted_iota(I32, (chunk, BLOCK), 1)

    def chunk_start(c):
        return pl.multiple_of(c * chunk, chunk)

    qidx = []
    for h in range(IDX_HEADS):
        qp = qi_ref[:, (h // 2) * LANE:(h // 2 + 1) * LANE]
        qidx.append(jnp.where(lo if h % 2 == 0 else ~lo, qp, jnp.zeros_like(qp)))
    q_stack = jnp.concatenate(qidx, axis=0)
    w = wt_ref[...]

    npair = (nc + 1) // 2

    def score_pair(cp, _):
        for u in range(2):
            start = chunk_start(2 * cp + u)
            d = _dot_nt(ki_ref[pl.ds(start, chunk), :], q_stack)
            sc = jnp.zeros((chunk, BLOCK), F32)
            for h in range(IDX_HEADS):
                sc = sc + w[h:h + 1, :] * jnp.maximum(d[:, h * BLOCK:(h + 1) * BLOCK], 0.0)
            sc = jnp.where(sc == 0.0, 0.0, sc)
            sc_ref[pl.ds(start, chunk), :] = jnp.where(start + s_row <= t_lane, sc, NEG_INF)
        return 0

    lax.fori_loop(0, npair, score_pair, 0)

    nacc = 4 * 8

    def count(trial, strict):
        def body(cp, acc):
            for u in range(2):
                sc = sc_ref[pl.ds(chunk_start(2 * cp + u), chunk), :]
                hit = (sc > trial) if strict else (sc >= trial)
                acc = acc + jnp.sum(jnp.where(hit, 1.0, 0.0).reshape(chunk // nacc, nacc, BLOCK), axis=0)
            return acc

        acc = lax.fori_loop(0, npair, body, jnp.zeros((nacc, BLOCK), F32))
        return jnp.sum(acc, axis=0, keepdims=True)

    def decode(key):
        return lax.bitcast_convert_type(jnp.where(key < 0, key ^ jnp.int32(0x7FFFFFFF), key), F32)

    kf = float(top_k)
    key_neg_inf = jnp.int32(-2139095041)
    cand = jnp.where(count(jnp.zeros((1, BLOCK), F32), False) >= kf, 0, jnp.int32(-2 ** 31)).astype(I32)

    def bit_step(j, cand):
        trial = cand + lax.shift_left(jnp.int32(1), 30 - j)
        ok = (trial <= key_neg_inf) | (count(decode(trial), False) >= kf)
        return jnp.where(ok, trial, cand)

    thr = decode(lax.fori_loop(0, 31, bit_step, cand))
    need = kf - count(thr, True)

    r0 = lax.broadcasted_iota(I32, (chunk, chunk), 0)
    r1 = lax.broadcasted_iota(I32, (chunk, chunk), 1)
    below = jnp.where(r1 < r0, 1.0, 0.0).astype(BF16)

    def mask_pair(cp, offset):
        for u in range(2):
            start = chunk_start(2 * cp + u)
            sc = sc_ref[pl.ds(start, chunk), :]
            eqf = jnp.where(sc == thr, 1.0, 0.0)
            prefix = _dot(below, eqf.astype(BF16)) + offset
            sel = jnp.where(sc > thr, 1.0, jnp.where(prefix < need, eqf, 0.0))
            keep = jnp.where(start + s_row <= t_lane, sel, 0.0)
            mb_ref[:, pl.ds(start, chunk)] = jnp.where(keep > 0.0, 0.0, NEG_INF).T
            offset = offset + jnp.sum(eqf, axis=0, keepdims=True)
        return offset

    lax.fori_loop(0, npair, mask_pair, jnp.zeros((1, BLOCK), F32))

    def att_chunk(c, carry):
        start = chunk_start(c)
        qs = [q_ref[:, h * LANE:(h + 1) * LANE] for h in range(C_HEADS)]
        ks = [k_ref[pl.ds(start, chunk), (h % 2) * LANE:(h % 2 + 1) * LANE] for h in range(C_HEADS)]
        vs = [v_ref[pl.ds(start, chunk), (h % 2) * LANE:(h % 2 + 1) * LANE] for h in range(C_HEADS)]
        return _online_softmax_chains(qs, ks, vs, carry, bias=mb_ref[:, pl.ds(start, chunk)], guard_empty=True)

    init = tuple((jnp.full((BLOCK, 1), NEG_INF, F32), jnp.zeros((BLOCK, LANE), F32)) for _ in range(C_HEADS))
    carry = lax.fori_loop(0, nc, att_chunk, init)
    _normalize_pairs([c[1] for c in carry], o_ref)


def _dsa(qc, kc, vc, qi, ki, wit):
    b, s, _ = qc.shape
    top_k = min(DSA_TOPK_MAX, s // 4)
    chunk = 256
    assert s % (2 * chunk) == 0 and top_k <= chunk
    blk = lambda bi, i: (bi, i, 0)
    seq = lambda bi, i: (bi, 0, 0)
    return pl.pallas_call(
        functools.partial(_dsa_kernel, top_k=top_k, chunk=chunk),
        out_shape=jax.ShapeDtypeStruct((b, s, C_HEADS * HEAD_DIM), BF16),
        grid=(b, s // BLOCK),
        in_specs=[pl.BlockSpec((None, BLOCK, C_HEADS * LANE), blk),
                  pl.BlockSpec((None, s, 2 * LANE), seq),
                  pl.BlockSpec((None, s, 2 * LANE), seq),
                  pl.BlockSpec((None, BLOCK, 256), blk),
                  pl.BlockSpec((None, s, LANE), seq),
                  pl.BlockSpec((None, IDX_HEADS, BLOCK), lambda bi, i: (bi, 0, i))],
        out_specs=pl.BlockSpec((None, BLOCK, 512), blk),
        scratch_shapes=[pltpu.VMEM((s, BLOCK), F32), pltpu.VMEM((BLOCK, s), F32)],
        compiler_params=_cparams(("parallel", "arbitrary")),
        name="dsa_attn",
    )(qc, kc, vc, qi, ki, wit)


def _dup64(w):
    d, c = w.shape
    w = w.reshape(d, c // HEAD_DIM, 1, HEAD_DIM)
    return jnp.broadcast_to(w, (d, c // HEAD_DIM, 2, HEAD_DIM)).reshape(d, 2 * c)


def _pad_cols(w, width):
    return jnp.pad(w, ((0, 0), (0, width - w.shape[1])))


def _split_cols(w, sizes):
    out, start = [], 0
    for n in sizes:
        out.append(w[:, start:start + n])
        start += n
    return out


def _rope_partner(w):
    half = MLA_ROPE // 2
    return jnp.concatenate([-w[..., half:], w[..., :half]], axis=-1)


def _rope_block(w_rope):
    d = w_rope.shape[0]
    return jnp.concatenate([jnp.zeros((d, MLA_NOPE), F32), w_rope,
                            jnp.zeros((d, LANE - MLA_NOPE - MLA_ROPE), F32)], axis=1)


def _alibi_tables(s):
    slopes = 2.0 ** (-8.0 * jnp.arange(1, C_HEADS + 1, dtype=F32) / C_HEADS)
    a = slopes * LOG2E
    pos = jnp.arange(s, dtype=F32)
    a_sp = _split3(a)
    r_sp = _split3(-(a[None, :] * pos[:, None]))
    ones = jnp.ones((s, C_HEADS), F32)
    q_terms = [128.0 * t * ones for t in a_sp] + [t * ones for t in a_sp] + list(r_sp)
    q_aug = jnp.stack(q_terms, axis=-1)
    q_aug = jnp.pad(q_aug, ((0, 0), (0, 0), (0, HEAD_DIM - q_aug.shape[-1])))
    zeros = jnp.zeros_like(q_aug)
    even = jnp.concatenate([zeros, q_aug], axis=-1)
    odd = jnp.concatenate([q_aug, zeros], axis=-1)
    parity = (jnp.arange(C_HEADS) % 2 == 0)[None, :, None]
    tq_t = jnp.where(parity, even, odd).reshape(s, C_HEADS * LANE)
    s_hi = jnp.floor(pos / LANE)
    s_lo = pos - LANE * s_hi
    k_aug = jnp.stack([s_hi] * 3 + [s_lo] * 3 + [jnp.ones_like(pos)] * 3, axis=-1)
    k_aug = jnp.pad(k_aug, ((0, 0), (0, HEAD_DIM - k_aug.shape[-1])))
    kz = jnp.zeros_like(k_aug)
    tk_t = jnp.concatenate([kz, k_aug, k_aug, kz], axis=-1)
    return tq_t.astype(BF16), tk_t.astype(BF16)


def kernel(x, even_w_in, even_b_f, even_sinks, odd_w_in, odd_q_norm, odd_kv_norm, odd_w_uq, odd_w_ukv,
           w_o, ln_g, ln_b, router_w, router_b, moe_w_gate, moe_w_up, moe_w_down):
    b, s, d = x.shape
    n = b * s
    depth = w_o.shape[0]
    alpha = (2 * depth) ** 0.25
    tm = min(512, s)
    tq = min(256, s)
    tm_e = min(256, n)
    tm_c = min(256, n)
    att_scale = HEAD_DIM ** -0.5
    xt = x.reshape(n, d)
    rwt = router_w.T
    rb = router_b.reshape(N_EXPERTS, 1)

    for l in range(depth):
        j = l // 2
        if l % 2 == 0:
            qa, ka, va, qb, kb, vb, fg = _split_cols(
                even_w_in[j], (512, 128, 128, 512, 512, 512, B_HEADS))
            w_cat = jnp.concatenate([qa * att_scale, _dup64(ka), _dup64(va), qb * (att_scale * LOG2E), kb, vb,
                                     _pad_cols(fg, LANE)], axis=1).astype(BF16)
            bf_pad = _pad_cols(even_b_f[j].reshape(1, B_HEADS), LANE)
            qa_, ka_, va_, qf, kf, vf = _even_proj(xt, w_cat, bf_pad, s, tm)
            r3 = lambda a: a.reshape(b, s, a.shape[1])
            out_a = _swa(even_sinks[j], r3(qa_), r3(ka_), r3(va_))
            out_b = _flash(r3(qf), r3(kf), r3(vf), tq=tq, nh=4, name="fox_attn")
        else:
            (qc, kc, vc, qi, ki, wi, cq, ckv, kr) = _split_cols(
                odd_w_in[j], (512, 64, 64, IDX_HEADS * IDX_DIM, IDX_DIM, IDX_HEADS,
                              MLA_Q_RANK, MLA_KV_RANK, MLA_ROPE))
            w_cat = jnp.concatenate([
                qc * (att_scale * LOG2E), _dup64(kc), _dup64(vc), qi * (IDX_DIM ** -0.5), _dup64(ki),
                _pad_cols(wi * (IDX_HEADS ** -0.5), LANE), cq, ckv,
                _rope_block(kr), _rope_block(_rope_partner(kr))], axis=1).astype(BF16)
            dq = MLA_NOPE + MLA_ROPE
            wuq = odd_w_uq[j].reshape(MLA_Q_RANK, D_HEADS, dq)
            pad_q = jnp.zeros((MLA_Q_RANK, D_HEADS, LANE - dq), F32)
            wqa = jnp.concatenate([wuq, pad_q], axis=2).reshape(MLA_Q_RANK, D_HEADS * LANE).astype(BF16)
            wqb = jnp.concatenate([jnp.zeros((MLA_Q_RANK, D_HEADS, MLA_NOPE), F32),
                                   _rope_partner(wuq[..., MLA_NOPE:]), pad_q],
                                  axis=2).reshape(MLA_Q_RANK, D_HEADS * LANE).astype(BF16)
            wukv = odd_w_ukv[j].reshape(MLA_KV_RANK, D_HEADS, MLA_NOPE + MLA_V)
            wk = jnp.concatenate([wukv[..., :MLA_NOPE], jnp.zeros((MLA_KV_RANK, D_HEADS, LANE - MLA_NOPE), F32)],
                                 axis=2).reshape(MLA_KV_RANK, D_HEADS * LANE).astype(BF16)
            wv = wukv[..., MLA_NOPE:].reshape(MLA_KV_RANK, D_HEADS * MLA_V).astype(BF16)
            half = MLA_ROPE // 2
            inv = ROPE_THETA ** (-jnp.arange(half, dtype=F32) / half)
            ang = jnp.arange(s, dtype=F32)[:, None] * inv[None, :]
            cos_h, sin_h = jnp.cos(ang), jnp.sin(ang)
            pad_t = jnp.zeros((s, LANE - dq), F32)
            cos_t = jnp.concatenate([jnp.ones((s, MLA_NOPE), F32), cos_h, cos_h, pad_t], axis=1)
            sin_t = jnp.concatenate([jnp.zeros((s, MLA_NOPE), F32), sin_h, sin_h, pad_t], axis=1)
            tq_t, tk_t = _alibi_tables(s)
            qc_, kc_, vc_, qi_, ki_, wi_, q8, k8, v8 = _odd_proj(
                xt, w_cat, odd_q_norm[j].reshape(1, -1), odd_kv_norm[j].reshape(1, -1),
                wqa, wqb, wk, wv, cos_t, sin_t, tq_t, tk_t, s, tm)
            r3 = lambda a: a.reshape(b, s, a.shape[1])
            out_a = _dsa(r3(qc_), r3(kc_), r3(vc_), r3(qi_), r3(ki_), r3(wi_).transpose(0, 2, 1))
            out_b = _flash(r3(q8), r3(k8), r3(v8), tq=tq, nh=4, name="mla_attn")
        half_w = out_a.shape[2]
        wo = w_o[l].astype(BF16)
        x1, route_t = _outproj(out_a.reshape(n, half_w), out_b.reshape(n, -1), xt,
                               wo[:half_w], wo[half_w:], ln_g[l, 0].reshape(1, d), ln_b[l, 0].reshape(1, d),
                               rwt, rb, alpha, tm)
        xt = _moe_layer(x1, route_t, moe_w_gate[l].astype(BF16), moe_w_up[l].astype(BF16),
                        moe_w_down[l].astype(BF16), ln_g[l, 1].reshape(1, d), ln_b[l, 1].reshape(1, d),
                        alpha, tm_e, tm_c)
    return xt.reshape(b, s, d)
```

```python
import functools
import math

import jax
import jax.numpy as jnp
from jax import lax
from jax.experimental import pallas as pl
from jax.experimental.pallas import tpu as pltpu

F32 = jnp.float32
BF16 = jnp.bfloat16
I32 = jnp.int32

LANE = 128
HEAD_DIM = 64
BLOCK = 128
A_HEADS, A_KV_HEADS, WINDOW = 8, 2, 128
B_HEADS = 8
C_HEADS, IDX_HEADS, IDX_DIM, DSA_TOPK_MAX = 8, 4, 64, 256
D_HEADS, MLA_Q_RANK, MLA_KV_RANK, MLA_NOPE, MLA_ROPE, MLA_V = 8, 256, 128, 64, 32, 64
ROPE_THETA = 10000.0
N_EXPERTS, N_GROUPS = 16, 4
EXPERTS_PER_GROUP = N_EXPERTS // N_GROUPS
LN_EPS, RMS_EPS = 1e-5, 1e-6
NEG_INF = float("-inf")
LOG2E = math.log2(math.e)
VMEM_LIMIT = 56 * 1024 * 1024

HIGHEST = lax.Precision.HIGHEST


def _cparams(sem):
    return pltpu.CompilerParams(dimension_semantics=sem, vmem_limit_bytes=VMEM_LIMIT)


def _dot(a, b):
    return jnp.dot(a, b, preferred_element_type=F32)


def _dot_nt(a, b):
    return lax.dot_general(a, b, (((1,), (1,)), ((), ())), preferred_element_type=F32)


def _lane_lo(shape):
    return lax.broadcasted_iota(I32, shape, len(shape) - 1) % LANE < HEAD_DIM


def _layer_norm(y, g, b):
    mu = jnp.mean(y, axis=-1, keepdims=True)
    yc = y - mu
    var = jnp.mean(yc * yc, axis=-1, keepdims=True)
    return yc * lax.rsqrt(var + LN_EPS) * g + b


def _split3(x):
    def top(v):
        bits = lax.bitcast_convert_type(v, I32) & jnp.int32(-65536)
        return lax.bitcast_convert_type(bits, F32)

    s1 = top(x)
    r = x - s1
    s2 = top(r)
    return s1, s2, r - s2


def _aug_rel(shape, h):
    lane = lax.broadcasted_iota(I32, shape, 1)
    return lane - (HEAD_DIM if h % 2 == 0 else 0)


def _even_proj_kernel(x_ref, w_ref, bf_ref, qa_ref, ka_ref, va_ref, qf_ref, kf_ref, vf_ref,
                      carry_ref, *, tiles_per_seq):
    t = pl.program_id(0)
    tm = x_ref.shape[0]
    h = _dot(x_ref[...].astype(BF16), w_ref[...])
    qa_ref[...] = h[:, 0:512].astype(BF16)
    ka_ref[...] = h[:, 512:768].astype(BF16)
    va_ref[...] = h[:, 768:1024].astype(BF16)
    z = h[:, 2560:2688] + bf_ref[...]
    logf = jnp.minimum(z, 0.0) - jnp.log1p(jnp.exp(-jnp.abs(z)))

    @pl.when(t % tiles_per_seq == 0)
    def _():
        carry_ref[...] = jnp.zeros_like(carry_ref)

    row = lax.broadcasted_iota(I32, (tm, tm), 0)
    col = lax.broadcasted_iota(I32, (tm, tm), 1)
    tri = jnp.where(row >= col, 1.0, 0.0).astype(F32)
    c = jnp.dot(tri, logf, precision=HIGHEST, preferred_element_type=F32) + carry_ref[...]
    carry_ref[...] = c[tm - 1:tm, :]

    s1, s2, s3 = _split3(c * LOG2E)
    lo = _lane_lo((tm, LANE))
    for hd in range(B_HEADS):
        own = lo if hd % 2 == 0 else ~lo
        rel = _aug_rel((tm, LANE), hd)
        b1, b2, b3 = s1[:, hd:hd + 1], s2[:, hd:hd + 1], s3[:, hd:hd + 1]
        q_aug = jnp.where(rel == 3, b1, jnp.where(rel == 4, b2, jnp.where(rel == 5, b3,
                          jnp.where((rel >= 0) & (rel < 3), -1.0, 0.0))))
        k_aug = jnp.where(rel == 0, b1, jnp.where(rel == 1, b2, jnp.where(rel == 2, b3,
                          jnp.where((rel >= 3) & (rel < 6), 1.0, 0.0))))
        pj = (hd // 2) * LANE
        sl = slice(hd * LANE, (hd + 1) * LANE)
        qf_ref[:, sl] = jnp.where(own, h[:, 1024 + pj:1024 + pj + LANE], q_aug).astype(BF16)
        kf_ref[:, sl] = jnp.where(own, h[:, 1536 + pj:1536 + pj + LANE], k_aug).astype(BF16)
        vf_ref[:, sl] = jnp.where(own, h[:, 2048 + pj:2048 + pj + LANE], 1.0).astype(BF16)


def _even_proj(xt, w_cat, bf_pad, seq, tm):
    n, d = xt.shape
    wcols = w_cat.shape[1]
    widths = (512, 256, 256, 1024, 1024, 1024)
    out_shape = [jax.ShapeDtypeStruct((n, w), BF16) for w in widths]
    out_specs = [pl.BlockSpec((tm, w), lambda i: (i, 0)) for w in widths]
    return pl.pallas_call(
        functools.partial(_even_proj_kernel, tiles_per_seq=seq // tm),
        out_shape=out_shape,
        grid=(n // tm,),
        in_specs=[pl.BlockSpec((tm, d), lambda i: (i, 0)),
                  pl.BlockSpec((d, wcols), lambda i: (0, 0)),
                  pl.BlockSpec((1, LANE), lambda i: (0, 0))],
        out_specs=out_specs,
        scratch_shapes=[pltpu.VMEM((1, LANE), F32)],
        compiler_params=_cparams(("arbitrary",)),
        name="even_proj",
    )(xt, w_cat, bf_pad)


def _swa_kernel(sink_ref, q_ref, kp_ref, kc_ref, vp_ref, vc_ref, o_ref):
    i = pl.program_id(1)
    group = A_HEADS // A_KV_HEADS
    tl = lax.broadcasted_iota(I32, (BLOCK, 2 * BLOCK), 0)
    sl = lax.broadcasted_iota(I32, (BLOCK, 2 * BLOCK), 1)
    dist = BLOCK + tl - sl
    valid = (dist >= 0) & (dist < WINDOW) & ((sl >= BLOCK) | (i > 0))
    distf = dist.astype(F32)
    lo = _lane_lo((BLOCK, LANE))
    outs = []
    for h in range(A_HEADS):
        g = h // group
        qp = q_ref[:, (h // 2) * LANE:(h // 2 + 1) * LANE]
        qh = jnp.where(lo if h % 2 == 0 else ~lo, qp, jnp.zeros_like(qp))
        kband = jnp.concatenate([kp_ref[:, g * LANE:(g + 1) * LANE],
                                 kc_ref[:, g * LANE:(g + 1) * LANE]], axis=0)
        vband = jnp.concatenate([vp_ref[:, g * LANE:(g + 1) * LANE],
                                 vc_ref[:, g * LANE:(g + 1) * LANE]], axis=0)
        slope = 2.0 ** (-8.0 * (h + 1) / A_HEADS)
        logits = _dot_nt(qh, kband) - slope * distf
        logits = jnp.where(valid, logits, NEG_INF)
        sink = sink_ref[h]
        m = jnp.maximum(jnp.max(logits, axis=-1, keepdims=True), sink)
        e = jnp.exp(logits - m)
        denom = jnp.sum(e, axis=-1, keepdims=True) + jnp.exp(sink - m)
        outs.append(_dot(e.astype(BF16), vband) / denom)
    for j in range(A_HEADS // 2):
        o_ref[:, j * LANE:(j + 1) * LANE] = jnp.where(lo, outs[2 * j], outs[2 * j + 1]).astype(BF16)


def _swa(sinks, qa, ka, va):
    b, s, _ = qa.shape
    nb = s // BLOCK
    prev = lambda bi, i: (bi, jnp.maximum(i - 1, 0), 0)
    cur = lambda bi, i: (bi, i, 0)
    return pl.pallas_call(
        _swa_kernel,
        out_shape=jax.ShapeDtypeStruct((b, s, A_HEADS * HEAD_DIM), BF16),
        grid=(b, nb),
        in_specs=[pl.BlockSpec(memory_space=pltpu.SMEM),
                  pl.BlockSpec((None, BLOCK, 512), cur),
                  pl.BlockSpec((None, BLOCK, 256), prev),
                  pl.BlockSpec((None, BLOCK, 256), cur),
                  pl.BlockSpec((None, BLOCK, 256), prev),
                  pl.BlockSpec((None, BLOCK, 256), cur)],
        out_specs=pl.BlockSpec((None, BLOCK, 512), cur),
        compiler_params=_cparams(("parallel", "arbitrary")),
        name="swa_attn",
    )(sinks, qa, ka, ka, va, va)


def _normalize_pairs(accs, o_ref):
    lo = _lane_lo(accs[0].shape)
    outs = [a / pltpu.roll(a, HEAD_DIM, 1) for a in accs]
    for j in range(len(accs) // 2):
        o_ref[:, j * LANE:(j + 1) * LANE] = jnp.where(lo, outs[2 * j], outs[2 * j + 1]).astype(BF16)


def _online_softmax_chains(qs, ks, vs, carry, bias=None, masks=None, guard_empty=False):
    n = len(qs)
    ss = [_dot_nt(qs[i], ks[i]) for i in range(n)]
    if bias is not None:
        ss = [s + bias for s in ss]
    if masks is not None:
        ss = [jnp.where(masks[i], ss[i], NEG_INF) for i in range(n)]
    ms, ps, alphas = [], [], []
    for i in range(n):
        m = carry[i][0]
        m_new = jnp.maximum(m, jnp.max(ss[i], axis=-1, keepdims=True))
        m_ref = jnp.where(m_new == NEG_INF, 0.0, m_new) if guard_empty else m_new
        ps.append(jnp.exp2(ss[i] - m_ref).astype(BF16))
        alphas.append(jnp.exp2(m - m_ref))
        ms.append(m_new)
    pvs = [_dot(ps[i], vs[i]) for i in range(n)]
    return tuple((ms[i], alphas[i] * carry[i][1] + pvs[i]) for i in range(n))


def _flash_kernel(q_ref, k_ref, v_ref, o_ref, *, tq, nh):
    qi = pl.program_id(2)
    nslab = tq // BLOCK
    chains = [(hh, r) for hh in range(nh) for r in range(nslab)]
    row = lax.broadcasted_iota(I32, (BLOCK, tq), 0)
    col = lax.broadcasted_iota(I32, (BLOCK, tq), 1)

    def lanes(hh):
        return slice(hh * LANE, (hh + 1) * LANE)

    def step(kb, carry, diagonal):
        start = pl.multiple_of(kb * tq, tq)
        qs = [q_ref[r * BLOCK:(r + 1) * BLOCK, lanes(hh)] for hh, r in chains]
        ks = [k_ref[pl.ds(start, tq), lanes(hh)] for hh, r in chains]
        vs = [v_ref[pl.ds(start, tq), lanes(hh)] for hh, r in chains]
        masks = [col <= row + r * BLOCK for hh, r in chains] if diagonal else None
        return _online_softmax_chains(qs, ks, vs, carry, masks=masks)

    init = tuple((jnp.full((BLOCK, 1), NEG_INF, F32), jnp.zeros((BLOCK, LANE), F32))
                 for _ in range(nh * nslab))
    carry = lax.fori_loop(0, qi, lambda kb, c: step(kb, c, False), init)
    carry = step(qi, carry, True)
    accs = [jnp.concatenate([carry[hh * nslab + r][1] for r in range(nslab)], axis=0) for hh in range(nh)]
    _normalize_pairs(accs, o_ref)


def _flash(q, k, v, *, tq, nh, name):
    b, s, w = v.shape
    return pl.pallas_call(
        functools.partial(_flash_kernel, tq=tq, nh=nh),
        out_shape=jax.ShapeDtypeStruct((b, s, w // 2), BF16),
        grid=(b, w // (nh * LANE), s // tq),
        in_specs=[pl.BlockSpec((None, tq, nh * LANE), lambda bi, j, i: (bi, i, j)),
                  pl.BlockSpec((None, s, nh * LANE), lambda bi, j, i: (bi, 0, j)),
                  pl.BlockSpec((None, s, nh * LANE), lambda bi, j, i: (bi, 0, j))],
        out_specs=pl.BlockSpec((None, tq, nh * LANE // 2), lambda bi, j, i: (bi, i, j)),
        compiler_params=_cparams(("parallel", "parallel", "arbitrary")),
        name=name,
    )(q, k, v)


def _route_rows(aff, sel):
    gs = []
    for g in range(N_GROUPS):
        r = sel[g * EXPERTS_PER_GROUP:(g + 1) * EXPERTS_PER_GROUP]
        best = None
        for a in range(EXPERTS_PER_GROUP):
            for c in range(a + 1, EXPERTS_PER_GROUP):
                pair = r[a] + r[c]
                best = pair if best is None else jnp.maximum(best, pair)
        gs.append(best)
    gmax = functools.reduce(jnp.maximum, gs)
    gidx = jnp.full_like(gmax, float(N_GROUPS - 1))
    for g in range(N_GROUPS - 2, -1, -1):
        gidx = jnp.where(gs[g] == gmax, float(g), gidx)

    def pick(rows, k):
        out = rows[(N_GROUPS - 1) * EXPERTS_PER_GROUP + k]
        for g in range(N_GROUPS - 2, -1, -1):
            out = jnp.where(gidx == float(g), rows[g * EXPERTS_PER_GROUP + k], out)
        return out

    s = [pick(sel, k) for k in range(EXPERTS_PER_GROUP)]
    a = [pick(aff, k) for k in range(EXPERTS_PER_GROUP)]

    def first_max(vals):
        mx = functools.reduce(jnp.maximum, vals)
        idx = jnp.full_like(mx, float(len(vals) - 1))
        for k in range(len(vals) - 2, -1, -1):
            idx = jnp.where(vals[k] == mx, float(k), idx)
        return idx

    def take(vals, idx):
        out = vals[-1]
        for k in range(len(vals) - 2, -1, -1):
            out = jnp.where(idx == float(k), vals[k], out)
        return out

    i1 = first_max(s)
    s2 = [jnp.where(i1 == float(k), NEG_INF, s[k]) for k in range(EXPERTS_PER_GROUP)]
    i2 = first_max(s2)
    a1, a2 = take(a, i1), take(a, i2)
    tot = a1 + a2
    base = gidx * float(EXPERTS_PER_GROUP)
    return base + i1, base + i2, a1 / tot, a2 / tot


def _outproj_kernel(a_ref, b_ref, x_ref, wa_ref, wb_ref, g_ref, be_ref, rwt_ref, rb_ref,
                    x1_ref, route_ref, *, alpha):
    mix = _dot(a_ref[...], wa_ref[...]) + _dot(b_ref[...], wb_ref[...])
    x1 = _layer_norm(alpha * x_ref[...] + mix, g_ref[...], be_ref[...])
    x1_ref[...] = x1
    zt = lax.dot_general(rwt_ref[...], x1, (((1,), (1,)), ((), ())),
                         precision=HIGHEST, preferred_element_type=F32)
    afft = 1.0 / (1.0 + jnp.exp(-zt))
    selt = afft + rb_ref[...]
    aff = [afft[e:e + 1, :] for e in range(N_EXPERTS)]
    sel = [selt[e:e + 1, :] for e in range(N_EXPERTS)]
    e1, e2, w1, w2 = _route_rows(aff, sel)
    zero = jnp.zeros_like(e1)
    route_ref[...] = jnp.concatenate([e1, e2, w1, w2, zero, zero, zero, zero], axis=0)


def _outproj(a, b, xt, wa, wb, g, be, rwt, rb, alpha, tm):
    n, d = xt.shape
    full = lambda i: (0, 0)
    return pl.pallas_call(
        functools.partial(_outproj_kernel, alpha=alpha),
        out_shape=[jax.ShapeDtypeStruct((n, d), F32), jax.ShapeDtypeStruct((8, n), F32)],
        grid=(n // tm,),
        in_specs=[pl.BlockSpec((tm, a.shape[1]), lambda i: (i, 0)),
                  pl.BlockSpec((tm, b.shape[1]), lambda i: (i, 0)),
                  pl.BlockSpec((tm, d), lambda i: (i, 0)),
                  pl.BlockSpec(wa.shape, full), pl.BlockSpec(wb.shape, full),
                  pl.BlockSpec((1, d), full), pl.BlockSpec((1, d), full),
                  pl.BlockSpec(rwt.shape, full), pl.BlockSpec(rb.shape, full)],
        out_specs=[pl.BlockSpec((tm, d), lambda i: (i, 0)), pl.BlockSpec((8, tm), lambda i: (0, i))],
        compiler_params=_cparams(("parallel",)),
        name="outproj_ln_route",
    )(a, b, xt, wa, wb, g, be, rwt, rb)


def _moe_kernel(bexp_ref, bbase_ref, nact_ref, src_ref,
                x_hbm, wg_ref, wu_ref, wd_ref, o_ref, xbuf0, xbuf1, sem, *, tm):
    i = pl.program_id(0)
    nact = nact_ref[0]
    last = src_ref.shape[0] - 1
    bufs = (xbuf0, xbuf1)

    def issue(blk, slot):
        base = bbase_ref[blk]
        for r in range(tm):
            tok = src_ref[jnp.minimum(base + r, last)]
            pltpu.make_async_copy(x_hbm.at[pl.ds(tok, 1), :], bufs[slot].at[pl.ds(r, 1), :],
                                  sem.at[slot]).start()

    def wait(slot):
        pltpu.make_async_copy(x_hbm.at[pl.ds(0, tm), :], bufs[slot], sem.at[slot]).wait()

    @pl.when((i == 0) & (nact > 0))
    def _():
        issue(0, 0)

    def step(slot):
        wait(slot)
        issue(jnp.minimum(i + 1, nact - 1), 1 - slot)
        xb = bufs[slot][...].astype(BF16)
        gate = _dot(xb, wg_ref[...])
        up = _dot(xb, wu_ref[...])
        h = gate * (1.0 / (1.0 + jnp.exp(-gate))) * up
        o_ref[...] = _dot(h.astype(BF16), wd_ref[...])

        @pl.when(i + 1 >= nact)
        def _():
            wait(1 - slot)

    for slot in range(2):
        pl.when((i < nact) & (i % 2 == slot))(functools.partial(step, slot))

    @pl.when(i >= nact)
    def _():
        o_ref[...] = jnp.zeros_like(o_ref)


def _moe_experts(x1, wg, wu, wd, bexp, bbase, nact, src, tm):
    n, d = x1.shape
    nblk = bexp.shape[0]
    f = wg.shape[2]
    wmap = lambda i, bexp, *_: (bexp[i], 0, 0)
    grid_spec = pltpu.PrefetchScalarGridSpec(
        num_scalar_prefetch=4,
        grid=(nblk,),
        in_specs=[pl.BlockSpec(memory_space=pl.ANY),
                  pl.BlockSpec((None, d, f), wmap),
                  pl.BlockSpec((None, d, f), wmap),
                  pl.BlockSpec((None, f, d), wmap)],
        out_specs=pl.BlockSpec((tm, d), lambda i, *_: (i, 0)),
        scratch_shapes=[pltpu.VMEM((tm, d), F32), pltpu.VMEM((tm, d), F32), pltpu.SemaphoreType.DMA((2,))],
    )
    return pl.pallas_call(
        functools.partial(_moe_kernel, tm=tm),
        out_shape=jax.ShapeDtypeStruct((nblk * tm, d), F32),
        grid_spec=grid_spec,
        compiler_params=_cparams(("arbitrary",)),
        name="moe_experts",
    )(bexp, bbase, nact, src, x1, wg, wu, wd)


def _combine_kernel(p0_ref, p1_ref, x_ref, r_ref, g_ref, be_ref, o_hbm, y_ref, obuf0, obuf1, sem, *, alpha, tm):
    i = pl.program_id(0)
    nsteps = pl.num_programs(0)

    bufs = (obuf0, obuf1)

    def issue(blk, slot):
        base = blk * tm
        for r in range(tm):
            for k, p_ref in enumerate((p0_ref, p1_ref)):
                pltpu.make_async_copy(o_hbm.at[pl.ds(p_ref[base + r], 1), :],
                                      bufs[slot].at[k, pl.ds(r, 1), :], sem.at[slot]).start()

    def wait(slot):
        for k in range(2):
            pltpu.make_async_copy(o_hbm.at[pl.ds(0, tm), :], bufs[slot].at[k], sem.at[slot]).wait()

    @pl.when(i == 0)
    def _():
        issue(0, 0)

    def step(slot):
        wait(slot)
        issue(jnp.minimum(i + 1, nsteps - 1), 1 - slot)
        r = r_ref[...]
        y = alpha * x_ref[...] + r[:, 2:3] * bufs[slot][0] + r[:, 3:4] * bufs[slot][1]
        y_ref[...] = _layer_norm(y, g_ref[...], be_ref[...])

        @pl.when(i + 1 >= nsteps)
        def _():
            wait(1 - slot)

    for slot in range(2):
        pl.when(i % 2 == slot)(functools.partial(step, slot))


def _combine(x1, route, g, be, o, p0, p1, alpha, tm):
    n, d = x1.shape
    grid_spec = pltpu.PrefetchScalarGridSpec(
        num_scalar_prefetch=2,
        grid=(n // tm,),
        in_specs=[pl.BlockSpec((tm, d), lambda i, *_: (i, 0)),
                  pl.BlockSpec((tm, 8), lambda i, *_: (i, 0)),
                  pl.BlockSpec((1, d), lambda i, *_: (0, 0)),
                  pl.BlockSpec((1, d), lambda i, *_: (0, 0)),
                  pl.BlockSpec(memory_space=pl.ANY)],
        out_specs=pl.BlockSpec((tm, d), lambda i, *_: (i, 0)),
        scratch_shapes=[pltpu.VMEM((2, tm, d), F32), pltpu.VMEM((2, tm, d), F32), pltpu.SemaphoreType.DMA((2,))],
    )
    return pl.pallas_call(
        functools.partial(_combine_kernel, alpha=alpha, tm=tm),
        out_shape=jax.ShapeDtypeStruct((n, d), F32),
        grid_spec=grid_spec,
        compiler_params=_cparams(("arbitrary",)),
        name="moe_combine_ln",
    )(p0, p1, x1, route, g, be, o)


def _moe_layer(x1, route_t, wg, wu, wd, g, be, alpha, tm_e, tm_c):
    n, d = x1.shape
    route = route_t.T
    e = route[:, 0:2].astype(I32).reshape(2 * n)
    onehot = (e[:, None] == jnp.arange(N_EXPERTS, dtype=I32)[None, :]).astype(I32)
    rank = jnp.take_along_axis(jnp.cumsum(onehot, axis=0) - onehot, e[:, None], axis=1)[:, 0]
    counts = jnp.sum(onehot, axis=0)
    nblk_e = (counts + tm_e - 1) // tm_e
    blk_start = jnp.cumsum(nblk_e) - nblk_e
    seg_start = jnp.cumsum(counts) - counts
    pos = blk_start[e] * tm_e + rank
    sorted_pos = seg_start[e] + rank
    src = jnp.zeros((2 * n,), I32).at[sorted_pos].set(jnp.arange(2 * n, dtype=I32) // 2)
    nblk = (2 * n) // tm_e + N_EXPERTS
    blk = jnp.arange(nblk, dtype=I32)
    blk_end = jnp.cumsum(nblk_e)
    bexp = jnp.minimum(jnp.sum((blk[:, None] >= blk_end[None, :]).astype(I32), axis=1), N_EXPERTS - 1)
    bbase = (seg_start[bexp] + (blk - blk_start[bexp]) * tm_e).astype(I32)
    nact = jnp.sum(nblk_e).astype(I32).reshape(1)
    o = _moe_experts(x1, wg, wu, wd, bexp, bbase, nact, src, tm_e)
    p = pos.reshape(n, 2).astype(I32)
    return _combine(x1, route, g, be, o, p[:, 0], p[:, 1], alpha, tm_c)


def _rms(x, g):
    return x * lax.rsqrt(jnp.mean(x * x, axis=-1, keepdims=True) + RMS_EPS) * g

ODD_OFF = dict(qc=0, kc=512, vc=640, qi=768, ki=1024, wi=1152, cq=1280, ckv=1536, kra=1664, krb=1792, end=1920)


def _odd_proj_kernel(x_ref, w_ref, qng_ref, kvng_ref, wqa_ref, wqb_ref, wk_ref, wv_ref, cos_ref, sin_ref,
                     tq_ref, tk_ref,
                     qc_ref, kc_ref, vc_ref, qi_ref, ki_ref, wi_ref, q8_ref, k8_ref, v8_ref, *, q_scale):
    o = ODD_OFF
    tm = x_ref.shape[0]
    lo = _lane_lo((tm, LANE))
    h = _dot(x_ref[...].astype(BF16), w_ref[...])
    kdup = h[:, o["kc"]:o["vc"]]
    vdup = h[:, o["vc"]:o["qi"]]
    kc_ref[:, 0:LANE] = jnp.where(lo, kdup, tk_ref[:, 0:LANE].astype(F32)).astype(BF16)
    kc_ref[:, LANE:2 * LANE] = jnp.where(lo, tk_ref[:, LANE:2 * LANE].astype(F32), kdup).astype(BF16)
    vc_ref[:, 0:LANE] = jnp.where(lo, vdup, 1.0).astype(BF16)
    vc_ref[:, LANE:2 * LANE] = jnp.where(lo, 1.0, vdup).astype(BF16)
    qi_ref[...] = h[:, o["qi"]:o["ki"]].astype(BF16)
    ki_ref[...] = h[:, o["ki"]:o["wi"]].astype(BF16)
    wi_ref[...] = h[:, o["wi"]:o["wi"] + IDX_HEADS]
    cos = cos_ref[...]
    sin = sin_ref[...]
    cqn = _rms(h[:, o["cq"]:o["ckv"]], qng_ref[...]).astype(BF16)
    qa = _dot(cqn, wqa_ref[...])
    qb = _dot(cqn, wqb_ref[...])
    ckvn = _rms(h[:, o["ckv"]:o["kra"]], kvng_ref[...]).astype(BF16)
    kn = _dot(ckvn, wk_ref[...])
    v8 = _dot(ckvn, wv_ref[...])
    kr = h[:, o["kra"]:o["krb"]] * cos + h[:, o["krb"]:o["end"]] * sin
    for hd in range(D_HEADS):
        sl = slice(hd * LANE, (hd + 1) * LANE)
        pj = slice((hd // 2) * LANE, (hd // 2 + 1) * LANE)
        own = lo if hd % 2 == 0 else ~lo
        qc_ref[:, sl] = jnp.where(own, h[:, pj], tq_ref[:, sl].astype(F32)).astype(BF16)
        q8_ref[:, sl] = ((qa[:, sl] * cos + qb[:, sl] * sin) * q_scale).astype(BF16)
        k8_ref[:, sl] = (kn[:, sl] + kr).astype(BF16)
        v8_ref[:, sl] = jnp.where(own, v8[:, pj], 1.0).astype(BF16)


def _odd_proj(xt, w_cat, qng, kvng, wqa, wqb, wk, wv, cos_t, sin_t, tq_t, tk_t, seq, tm):
    n, d = xt.shape
    full = lambda i: (0, 0)
    tps = seq // tm
    pos = lambda i: (i % tps, 0)
    widths = (1024, 256, 256, 256, 128)
    out_shape = [jax.ShapeDtypeStruct((n, w), BF16) for w in widths]
    out_specs = [pl.BlockSpec((tm, w), lambda i: (i, 0)) for w in widths]
    out_shape.append(jax.ShapeDtypeStruct((n, IDX_HEADS), F32))
    out_specs.append(pl.BlockSpec((tm, IDX_HEADS), lambda i: (i, 0)))
    for w in (1024, 1024, 1024):
        out_shape.append(jax.ShapeDtypeStruct((n, w), BF16))
        out_specs.append(pl.BlockSpec((tm, w), lambda i: (i, 0)))
    return pl.pallas_call(
        functools.partial(_odd_proj_kernel, q_scale=(MLA_NOPE + MLA_ROPE) ** -0.5 * LOG2E),
        out_shape=out_shape,
        grid=(n // tm,),
        in_specs=[pl.BlockSpec((tm, d), lambda i: (i, 0)),
                  pl.BlockSpec(w_cat.shape, full),
                  pl.BlockSpec(qng.shape, full), pl.BlockSpec(kvng.shape, full),
                  pl.BlockSpec(wqa.shape, full), pl.BlockSpec(wqb.shape, full),
                  pl.BlockSpec(wk.shape, full), pl.BlockSpec(wv.shape, full),
                  pl.BlockSpec((tm, LANE), pos), pl.BlockSpec((tm, LANE), pos),
                  pl.BlockSpec((tm, C_HEADS * LANE), pos), pl.BlockSpec((tm, 2 * LANE), pos)],
        out_specs=out_specs,
        compiler_params=_cparams(("parallel",)),
        name="odd_proj",
    )(xt, w_cat, qng, kvng, wqa, wqb, wk, wv, cos_t, sin_t, tq_t, tk_t)


def _dsa_kernel(q_ref, k_ref, v_ref, qi_ref, ki_ref, wt_ref, o_ref, sc_ref, mb_ref, *, top_k, chunk):
    i = pl.program_id(1)
    nc = (i * BLOCK + BLOCK + chunk - 1) // chunk
    lo = _lane_lo((BLOCK, LANE))
    s_row = lax.broadcasted_iota(I32, (chunk, BLOCK), 0)
    t_lane = i * BLOCK + lax.broadcasted_iota(I32, (chunk, BLOCK), 1)

    def chunk_start(c):
        return pl.multiple_of(c * chunk, chunk)

    qidx = []
    for h in range(IDX_HEADS):
        qp = qi_ref[:, (h // 2) * LANE:(h // 2 + 1) * LANE]
        qidx.append(jnp.where(lo if h % 2 == 0 else ~lo, qp, jnp.zeros_like(qp)))
    q_stack = jnp.concatenate(qidx, axis=0)
    w = wt_ref[...]

    npair = (nc + 1) // 2

    def score_pair(cp, _):
        for u in range(2):
            start = chunk_start(2 * cp + u)
            d = _dot_nt(ki_ref[pl.ds(start, chunk), :], q_stack)
            sc = jnp.zeros((chunk, BLOCK), F32)
            for h in range(IDX_HEADS):
                sc = sc + w[h:h + 1, :] * jnp.maximum(d[:, h * BLOCK:(h + 1) * BLOCK], 0.0)
            sc = jnp.where(sc == 0.0, 0.0, sc)
            sc_ref[pl.ds(start, chunk), :] = jnp.where(start + s_row <= t_lane, sc, NEG_INF)
        return 0

    lax.fori_loop(0, npair, score_pair, 0)

    nacc = 4 * 8

    def count(trial, strict):
        def body(cp, acc):
            for u in range(2):
                sc = sc_ref[pl.ds(chunk_start(2 * cp + u), chunk), :]
                hit = (sc > trial) if strict else (sc >= trial)
                acc = acc + jnp.sum(jnp.where(hit, 1.0, 0.0).reshape(chunk // nacc, nacc, BLOCK), axis=0)
            return acc

        acc = lax.fori_loop(0, npair, body, jnp.zeros((nacc, BLOCK), F32))
        return jnp.sum(acc, axis=0, keepdims=True)

    def decode(key):
        return lax.bitcast_convert_type(jnp.where(key < 0, key ^ jnp.int32(0x7FFFFFFF), key), F32)

    kf = float(top_k)
    key_neg_inf = jnp.int32(-2139095041)
    cand = jnp.where(count(jnp.zeros((1, BLOCK), F32), False) >= kf, 0, jnp.int32(-2 ** 31)).astype(I32)

    def bit_step(j, cand):
        trial = cand + lax.shift_left(jnp.int32(1), 30 - j)
        ok = (trial <= key_neg_inf) | (count(decode(trial), False) >= kf)
        return jnp.where(ok, trial, cand)

    thr = decode(lax.fori_loop(0, 31, bit_step, cand))
    need = kf - count(thr, True)

    r0 = lax.broadcasted_iota(I32, (chunk, chunk), 0)
    r1 = lax.broadcasted_iota(I32, (chunk, chunk), 1)
    below = jnp.where(r1 < r0, 1.0, 0.0).astype(BF16)

    def mask_pair(cp, offset):
        for u in range(2):
            start = chunk_start(2 * cp + u)
            sc = sc_ref[pl.ds(start, chunk), :]
            eqf = jnp.where(sc == thr, 1.0, 0.0)
            prefix = _dot(below, eqf.astype(BF16)) + offset
            sel = jnp.where(sc > thr, 1.0, jnp.where(prefix < need, eqf, 0.0))
            keep = jnp.where(start + s_row <= t_lane, sel, 0.0)
            mb_ref[:, pl.ds(start, chunk)] = jnp.where(keep > 0.0, 0.0, NEG_INF).T
            offset = offset + jnp.sum(eqf, axis=0, keepdims=True)
        return offset

    lax.fori_loop(0, npair, mask_pair, jnp.zeros((1, BLOCK), F32))

    def att_chunk(c, carry):
        start = chunk_start(c)
        qs = [q_ref[:, h * LANE:(h + 1) * LANE] for h in range(C_HEADS)]
        ks = [k_ref[pl.ds(start, chunk), (h % 2) * LANE:(h % 2 + 1) * LANE] for h in range(C_HEADS)]
        vs = [v_ref[pl.ds(start, chunk), (h % 2) * LANE:(h % 2 + 1) * LANE] for h in range(C_HEADS)]
        return _online_softmax_chains(qs, ks, vs, carry, bias=mb_ref[:, pl.ds(start, chunk)], guard_empty=True)

    init = tuple((jnp.full((BLOCK, 1), NEG_INF, F32), jnp.zeros((BLOCK, LANE), F32)) for _ in range(C_HEADS))
    carry = lax.fori_loop(0, nc, att_chunk, init)
    _normalize_pairs([c[1] for c in carry], o_ref)


def _dsa(qc, kc, vc, qi, ki, wit):
    b, s, _ = qc.shape
    top_k = min(DSA_TOPK_MAX, s // 4)
    chunk = 256
    assert s % (2 * chunk) == 0 and top_k <= chunk
    blk = lambda bi, i: (bi, i, 0)
    seq = lambda bi, i: (bi, 0, 0)
    return pl.pallas_call(
        functools.partial(_dsa_kernel, top_k=top_k, chunk=chunk),
        out_shape=jax.ShapeDtypeStruct((b, s, C_HEADS * HEAD_DIM), BF16),
        grid=(b, s // BLOCK),
        in_specs=[pl.BlockSpec((None, BLOCK, C_HEADS * LANE), blk),
                  pl.BlockSpec((None, s, 2 * LANE), seq),
                  pl.BlockSpec((None, s, 2 * LANE), seq),
                  pl.BlockSpec((None, BLOCK, 256), blk),
                  pl.BlockSpec((None, s, LANE), seq),
                  pl.BlockSpec((None, IDX_HEADS, BLOCK), lambda bi, i: (bi, 0, i))],
        out_specs=pl.BlockSpec((None, BLOCK, 512), blk),
        scratch_shapes=[pltpu.VMEM((s, BLOCK), F32), pltpu.VMEM((BLOCK, s), F32)],
        compiler_params=_cparams(("parallel", "arbitrary")),
        name="dsa_attn",
    )(qc, kc, vc, qi, ki, wit)


def _dup64(w):
    d, c = w.shape
    w = w.reshape(d, c // HEAD_DIM, 1, HEAD_DIM)
    return jnp.broadcast_to(w, (d, c // HEAD_DIM, 2, HEAD_DIM)).reshape(d, 2 * c)


def _pad_cols(w, width):
    return jnp.pad(w, ((0, 0), (0, width - w.shape[1])))


def _split_cols(w, sizes):
    out, start = [], 0
    for n in sizes:
        out.append(w[:, start:start + n])
        start += n
    return out


def _rope_partner(w):
    half = MLA_ROPE // 2
    return jnp.concatenate([-w[..., half:], w[..., :half]], axis=-1)


def _rope_block(w_rope):
    d = w_rope.shape[0]
    return jnp.concatenate([jnp.zeros((d, MLA_NOPE), F32), w_rope,
                            jnp.zeros((d, LANE - MLA_NOPE - MLA_ROPE), F32)], axis=1)


def _alibi_tables(s):
    slopes = 2.0 ** (-8.0 * jnp.arange(1, C_HEADS + 1, dtype=F32) / C_HEADS)
    a = slopes * LOG2E
    pos = jnp.arange(s, dtype=F32)
    a_sp = _split3(a)
    r_sp = _split3(-(a[None, :] * pos[:, None]))
    ones = jnp.ones((s, C_HEADS), F32)
    q_terms = [128.0 * t * ones for t in a_sp] + [t * ones for t in a_sp] + list(r_sp)
    q_aug = jnp.stack(q_terms, axis=-1)
    q_aug = jnp.pad(q_aug, ((0, 0), (0, 0), (0, HEAD_DIM - q_aug.shape[-1])))
    zeros = jnp.zeros_like(q_aug)
    even = jnp.concatenate([zeros, q_aug], axis=-1)
    odd = jnp.concatenate([q_aug, zeros], axis=-1)
    parity = (jnp.arange(C_HEADS) % 2 == 0)[None, :, None]
    tq_t = jnp.where(parity, even, odd).reshape(s, C_HEADS * LANE)
    s_hi = jnp.floor(pos / LANE)
    s_lo = pos - LANE * s_hi
    k_aug = jnp.stack([s_hi] * 3 + [s_lo] * 3 + [jnp.ones_like(pos)] * 3, axis=-1)
    k_aug = jnp.pad(k_aug, ((0, 0), (0, HEAD_DIM - k_aug.shape[-1])))
    kz = jnp.zeros_like(k_aug)
    tk_t = jnp.concatenate([kz, k_aug, k_aug, kz], axis=-1)
    return tq_t.astype(BF16), tk_t.astype(BF16)


def kernel(x, even_w_in, even_b_f, even_sinks, odd_w_in, odd_q_norm, odd_kv_norm, odd_w_uq, odd_w_ukv,
           w_o, ln_g, ln_b, router_w, router_b, moe_w_gate, moe_w_up, moe_w_down):
    b, s, d = x.shape
    n = b * s
    depth = w_o.shape[0]
    alpha = (2 * depth) ** 0.25
    tm = min(512, s)
    tq = min(256, s)
    tm_e = min(256, n)
    tm_c = min(256, n)
    att_scale = HEAD_DIM ** -0.5
    xt = x.reshape(n, d)
    rwt = router_w.T
    rb = router_b.reshape(N_EXPERTS, 1)

    for l in range(depth):
        j = l // 2
        if l % 2 == 0:
            qa, ka, va, qb, kb, vb, fg = _split_cols(
                even_w_in[j], (512, 128, 128, 512, 512, 512, B_HEADS))
            w_cat = jnp.concatenate([qa * att_scale, _dup64(ka), _dup64(va), qb * (att_scale * LOG2E), kb, vb,
                                     _pad_cols(fg, LANE)], axis=1).astype(BF16)
            bf_pad = _pad_cols(even_b_f[j].reshape(1, B_HEADS), LANE)
            qa_, ka_, va_, qf, kf, vf = _even_proj(xt, w_cat, bf_pad, s, tm)
            r3 = lambda a: a.reshape(b, s, a.shape[1])
            out_a = _swa(even_sinks[j], r3(qa_), r3(ka_), r3(va_))
            out_b = _flash(r3(qf), r3(kf), r3(vf), tq=tq, nh=4, name="fox_attn")
        else:
            (qc, kc, vc, qi, ki, wi, cq, ckv, kr) = _split_cols(
                odd_w_in[j], (512, 64, 64, IDX_HEADS * IDX_DIM, IDX_DIM, IDX_HEADS,
                              MLA_Q_RANK, MLA_KV_RANK, MLA_ROPE))
            w_cat = jnp.concatenate([
                qc * (att_scale * LOG2E), _dup64(kc), _dup64(vc), qi * (IDX_DIM ** -0.5), _dup64(ki),
                _pad_cols(wi * (IDX_HEADS ** -0.5), LANE), cq, ckv,
                _rope_block(kr), _rope_block(_rope_partner(kr))], axis=1).astype(BF16)
            dq = MLA_NOPE + MLA_ROPE
            wuq = odd_w_uq[j].reshape(MLA_Q_RANK, D_HEADS, dq)
            pad_q = jnp.zeros((MLA_Q_RANK, D_HEADS, LANE - dq), F32)
            wqa = jnp.concatenate([wuq, pad_q], axis=2).reshape(MLA_Q_RANK, D_HEADS * LANE).astype(BF16)
            wqb = jnp.concatenate([jnp.zeros((MLA_Q_RANK, D_HEADS, MLA_NOPE), F32),
                                   _rope_partner(wuq[..., MLA_NOPE:]), pad_q],
                                  axis=2).reshape(MLA_Q_RANK, D_HEADS * LANE).astype(BF16)
            wukv = odd_w_ukv[j].reshape(MLA_KV_RANK, D_HEADS, MLA_NOPE + MLA_V)
            wk = jnp.concatenate([wukv[..., :MLA_NOPE], jnp.zeros((MLA_KV_RANK, D_HEADS, LANE - MLA_NOPE), F32)],
                                 axis=2).reshape(MLA_KV_RANK, D_HEADS * LANE).astype(BF16)
            wv = wukv[..., MLA_NOPE:].reshape(MLA_KV_RANK, D_HEADS * MLA_V).astype(BF16)
            half = MLA_ROPE // 2
            inv = ROPE_THETA ** (-jnp.arange(half, dtype=F32) / half)
            ang = jnp.arange(s, dtype=F32)[:, None] * inv[None, :]
            cos_h, sin_h = jnp.cos(ang), jnp.sin(ang)
            pad_t = jnp.zeros((s, LANE - dq), F32)
            cos_t = jnp.concatenate([jnp.ones((s, MLA_NOPE), F32), cos_h, cos_h, pad_t], axis=1)
            sin_t = jnp.concatenate([jnp.zeros((s, MLA_NOPE), F32), sin_h, sin_h, pad_t], axis=1)
            tq_t, tk_t = _alibi_tables(s)
            qc_, kc_, vc_, qi_, ki_, wi_, q8, k8, v8 = _odd_proj(
                xt, w_cat, odd_q_norm[j].reshape(1, -1), odd_kv_norm[j].reshape(1, -1),
                wqa, wqb, wk, wv, cos_t, sin_t, tq_t, tk_t, s, tm)
            r3 = lambda a: a.reshape(b, s, a.shape[1])
            out_a = _dsa(r3(qc_), r3(kc_), r3(vc_), r3(qi_), r3(ki_), r3(wi_).transpose(0, 2, 1))
            out_b = _flash(r3(q8), r3(k8), r3(v8), tq=tq, nh=4, name="mla_attn")
        half_w = out_a.shape[2]
        wo = w_o[l].astype(BF16)
        x1, route_t = _outproj(out_a.reshape(n, half_w), out_b.reshape(n, -1), xt,
                               wo[:half_w], wo[half_w:], ln_g[l, 0].reshape(1, d), ln_b[l, 0].reshape(1, d),
                               rwt, rb, alpha, tm)
        xt = _moe_layer(x1, route_t, moe_w_gate[l].astype(BF16), moe_w_up[l].astype(BF16),
                        moe_w_down[l].astype(BF16), ln_g[l, 1].reshape(1, d), ln_b[l, 1].reshape(1, d),
                        alpha, tm_e, tm_c)
    return xt.reshape(b, s, d)
```

```python
import functools
import math

import jax
import jax.numpy as jnp
from jax import lax
from jax.experimental import pallas as pl
from jax.experimental.pallas import tpu as pltpu

F32 = jnp.float32
BF16 = jnp.bfloat16
I32 = jnp.int32

LANE = 128
HEAD_DIM = 64
BLOCK = 128
A_HEADS, A_KV_HEADS, WINDOW = 8, 2, 128
B_HEADS = 8
C_HEADS, IDX_HEADS, IDX_DIM, DSA_TOPK_MAX = 8, 4, 64, 256
D_HEADS, MLA_Q_RANK, MLA_KV_RANK, MLA_NOPE, MLA_ROPE, MLA_V = 8, 256, 128, 64, 32, 64
ROPE_THETA = 10000.0
N_EXPERTS, N_GROUPS = 16, 4
EXPERTS_PER_GROUP = N_EXPERTS // N_GROUPS
LN_EPS, RMS_EPS = 1e-5, 1e-6
NEG_INF = float("-inf")
LOG2E = math.log2(math.e)
VMEM_LIMIT = 56 * 1024 * 1024

HIGHEST = lax.Precision.HIGHEST


def _cparams(sem):
    return pltpu.CompilerParams(dimension_semantics=sem, vmem_limit_bytes=VMEM_LIMIT)


def _dot(a, b):
    return jnp.dot(a, b, preferred_element_type=F32)


def _dot_nt(a, b):
    return lax.dot_general(a, b, (((1,), (1,)), ((), ())), preferred_element_type=F32)


def _lane_lo(shape):
    return lax.broadcasted_iota(I32, shape, len(shape) - 1) % LANE < HEAD_DIM


def _layer_norm(y, g, b):
    mu = jnp.mean(y, axis=-1, keepdims=True)
    yc = y - mu
    var = jnp.mean(yc * yc, axis=-1, keepdims=True)
    return yc * lax.rsqrt(var + LN_EPS) * g + b


def _split3(x):
    def top(v):
        bits = lax.bitcast_convert_type(v, I32) & jnp.int32(-65536)
        return lax.bitcast_convert_type(bits, F32)

    s1 = top(x)
    r = x - s1
    s2 = top(r)
    return s1, s2, r - s2


def _aug_rel(shape, h):
    lane = lax.broadcasted_iota(I32, shape, 1)
    return lane - (HEAD_DIM if h % 2 == 0 else 0)


def _even_proj_kernel(x_ref, w_ref, bf_ref, qa_ref, ka_ref, va_ref, qf_ref, kf_ref, vf_ref,
                      carry_ref, *, tiles_per_seq):
    t = pl.program_id(0)
    tm = x_ref.shape[0]
    h = _dot(x_ref[...].astype(BF16), w_ref[...])
    qa_ref[...] = h[:, 0:512].astype(BF16)
    ka_ref[...] = h[:, 512:768].astype(BF16)
    va_ref[...] = h[:, 768:1024].astype(BF16)
    z = h[:, 2560:2688] + bf_ref[...]
    logf = jnp.minimum(z, 0.0) - jnp.log1p(jnp.exp(-jnp.abs(z)))

    @pl.when(t % tiles_per_seq == 0)
    def _():
        carry_ref[...] = jnp.zeros_like(carry_ref)

    row = lax.broadcasted_iota(I32, (tm, tm), 0)
    col = lax.broadcasted_iota(I32, (tm, tm), 1)
    tri = jnp.where(row >= col, 1.0, 0.0).astype(F32)
    c = jnp.dot(tri, logf, precision=HIGHEST, preferred_element_type=F32) + carry_ref[...]
    carry_ref[...] = c[tm - 1:tm, :]

    s1, s2, s3 = _split3(c * LOG2E)
    lo = _lane_lo((tm, LANE))
    for hd in range(B_HEADS):
        own = lo if hd % 2 == 0 else ~lo
        rel = _aug_rel((tm, LANE), hd)
        b1, b2, b3 = s1[:, hd:hd + 1], s2[:, hd:hd + 1], s3[:, hd:hd + 1]
        q_aug = jnp.where(rel == 3, b1, jnp.where(rel == 4, b2, jnp.where(rel == 5, b3,
                          jnp.where((rel >= 0) & (rel < 3), -1.0, 0.0))))
        k_aug = jnp.where(rel == 0, b1, jnp.where(rel == 1, b2, jnp.where(rel == 2, b3,
                          jnp.where((rel >= 3) & (rel < 6), 1.0, 0.0))))
        pj = (hd // 2) * LANE
        sl = slice(hd * LANE, (hd + 1) * LANE)
        qf_ref[:, sl] = jnp.where(own, h[:, 1024 + pj:1024 + pj + LANE], q_aug).astype(BF16)
        kf_ref[:, sl] = jnp.where(own, h[:, 1536 + pj:1536 + pj + LANE], k_aug).astype(BF16)
        vf_ref[:, sl] = jnp.where(own, h[:, 2048 + pj:2048 + pj + LANE], 1.0).astype(BF16)


def _even_proj(xt, w_cat, bf_pad, seq, tm):
    n, d = xt.shape
    wcols = w_cat.shape[1]
    widths = (512, 256, 256, 1024, 1024, 1024)
    out_shape = [jax.ShapeDtypeStruct((n, w), BF16) for w in widths]
    out_specs = [pl.BlockSpec((tm, w), lambda i: (i, 0)) for w in widths]
    return pl.pallas_call(
        functools.partial(_even_proj_kernel, tiles_per_seq=seq // tm),
        out_shape=out_shape,
        grid=(n // tm,),
        in_specs=[pl.BlockSpec((tm, d), lambda i: (i, 0)),
                  pl.BlockSpec((d, wcols), lambda i: (0, 0)),
                  pl.BlockSpec((1, LANE), lambda i: (0, 0))],
        out_specs=out_specs,
        scratch_shapes=[pltpu.VMEM((1, LANE), F32)],
        compiler_params=_cparams(("arbitrary",)),
        name="even_proj",
    )(xt, w_cat, bf_pad)


def _swa_kernel(sink_ref, q_ref, kp_ref, kc_ref, vp_ref, vc_ref, o_ref):
    i = pl.program_id(1)
    group = A_HEADS // A_KV_HEADS
    tl = lax.broadcasted_iota(I32, (BLOCK, 2 * BLOCK), 0)
    sl = lax.broadcasted_iota(I32, (BLOCK, 2 * BLOCK), 1)
    dist = BLOCK + tl - sl
    valid = (dist >= 0) & (dist < WINDOW) & ((sl >= BLOCK) | (i > 0))
    distf = dist.astype(F32)
    lo = _lane_lo((BLOCK, LANE))
    outs = []
    for h in range(A_HEADS):
        g = h // group
        qp = q_ref[:, (h // 2) * LANE:(h // 2 + 1) * LANE]
        qh = jnp.where(lo if h % 2 == 0 else ~lo, qp, jnp.zeros_like(qp))
        kband = jnp.concatenate([kp_ref[:, g * LANE:(g + 1) * LANE],
                                 kc_ref[:, g * LANE:(g + 1) * LANE]], axis=0)
        vband = jnp.concatenate([vp_ref[:, g * LANE:(g + 1) * LANE],
                                 vc_ref[:, g * LANE:(g + 1) * LANE]], axis=0)
        slope = 2.0 ** (-8.0 * (h + 1) / A_HEADS)
        logits = _dot_nt(qh, kband) - slope * distf
        logits = jnp.where(valid, logits, NEG_INF)
        sink = sink_ref[h]
        m = jnp.maximum(jnp.max(logits, axis=-1, keepdims=True), sink)
        e = jnp.exp(logits - m)
        denom = jnp.sum(e, axis=-1, keepdims=True) + jnp.exp(sink - m)
        outs.append(_dot(e.astype(BF16), vband) / denom)
    for j in range(A_HEADS // 2):
        o_ref[:, j * LANE:(j + 1) * LANE] = jnp.where(lo, outs[2 * j], outs[2 * j + 1]).astype(BF16)


def _swa(sinks, qa, ka, va):
    b, s, _ = qa.shape
    nb = s // BLOCK
    prev = lambda bi, i: (bi, jnp.maximum(i - 1, 0), 0)
    cur = lambda bi, i: (bi, i, 0)
    return pl.pallas_call(
        _swa_kernel,
        out_shape=jax.ShapeDtypeStruct((b, s, A_HEADS * HEAD_DIM), BF16),
        grid=(b, nb),
        in_specs=[pl.BlockSpec(memory_space=pltpu.SMEM),
                  pl.BlockSpec((None, BLOCK, 512), cur),
                  pl.BlockSpec((None, BLOCK, 256), prev),
                  pl.BlockSpec((None, BLOCK, 256), cur),
                  pl.BlockSpec((None, BLOCK, 256), prev),
                  pl.BlockSpec((None, BLOCK, 256), cur)],
        out_specs=pl.BlockSpec((None, BLOCK, 512), cur),
        compiler_params=_cparams(("parallel", "arbitrary")),
        name="swa_attn",
    )(sinks, qa, ka, ka, va, va)


def _normalize_pairs(accs, o_ref):
    lo = _lane_lo(accs[0].shape)
    outs = [a / pltpu.roll(a, HEAD_DIM, 1) for a in accs]
    for j in range(len(accs) // 2):
        o_ref[:, j * LANE:(j + 1) * LANE] = jnp.where(lo, outs[2 * j], outs[2 * j + 1]).astype(BF16)


def _online_softmax_chains(qs, ks, vs, carry, bias=None, masks=None, guard_empty=False):
    n = len(qs)
    ss = [_dot_nt(qs[i], ks[i]) for i in range(n)]
    if bias is not None:
        ss = [s + bias for s in ss]
    if masks is not None:
        ss = [jnp.where(masks[i], ss[i], NEG_INF) for i in range(n)]
    ms, ps, alphas = [], [], []
    for i in range(n):
        m = carry[i][0]
        m_new = jnp.maximum(m, jnp.max(ss[i], axis=-1, keepdims=True))
        m_ref = jnp.where(m_new == NEG_INF, 0.0, m_new) if guard_empty else m_new
        ps.append(jnp.exp2(ss[i] - m_ref).astype(BF16))
        alphas.append(jnp.exp2(m - m_ref))
        ms.append(m_new)
    pvs = [_dot(ps[i], vs[i]) for i in range(n)]
    return tuple((ms[i], alphas[i] * carry[i][1] + pvs[i]) for i in range(n))


def _flash_kernel(q_ref, k_ref, v_ref, o_ref, *, tq, nh):
    qi = pl.program_id(2)
    nslab = tq // BLOCK
    chains = [(hh, r) for hh in range(nh) for r in range(nslab)]
    row = lax.broadcasted_iota(I32, (BLOCK, tq), 0)
    col = lax.broadcasted_iota(I32, (BLOCK, tq), 1)

    def lanes(hh):
        return slice(hh * LANE, (hh + 1) * LANE)

    def step(kb, carry, diagonal):
        start = pl.multiple_of(kb * tq, tq)
        qs = [q_ref[r * BLOCK:(r + 1) * BLOCK, lanes(hh)] for hh, r in chains]
        ks = [k_ref[pl.ds(start, tq), lanes(hh)] for hh, r in chains]
        vs = [v_ref[pl.ds(start, tq), lanes(hh)] for hh, r in chains]
        masks = [col <= row + r * BLOCK for hh, r in chains] if diagonal else None
        return _online_softmax_chains(qs, ks, vs, carry, masks=masks)

    init = tuple((jnp.full((BLOCK, 1), NEG_INF, F32), jnp.zeros((BLOCK, LANE), F32))
                 for _ in range(nh * nslab))
    carry = lax.fori_loop(0, qi, lambda kb, c: step(kb, c, False), init)
    carry = step(qi, carry, True)
    accs = [jnp.concatenate([carry[hh * nslab + r][1] for r in range(nslab)], axis=0) for hh in range(nh)]
    _normalize_pairs(accs, o_ref)


def _flash(q, k, v, *, tq, nh, name):
    b, s, w = v.shape
    return pl.pallas_call(
        functools.partial(_flash_kernel, tq=tq, nh=nh),
        out_shape=jax.ShapeDtypeStruct((b, s, w // 2), BF16),
        grid=(b, w // (nh * LANE), s // tq),
        in_specs=[pl.BlockSpec((None, tq, nh * LANE), lambda bi, j, i: (bi, i, j)),
                  pl.BlockSpec((None, s, nh * LANE), lambda bi, j, i: (bi, 0, j)),
                  pl.BlockSpec((None, s, nh * LANE), lambda bi, j, i: (bi, 0, j))],
        out_specs=pl.BlockSpec((None, tq, nh * LANE // 2), lambda bi, j, i: (bi, i, j)),
        compiler_params=_cparams(("parallel", "parallel", "arbitrary")),
        name=name,
    )(q, k, v)


def _route_rows(aff, sel):
    gs = []
    for g in range(N_GROUPS):
        r = sel[g * EXPERTS_PER_GROUP:(g + 1) * EXPERTS_PER_GROUP]
        best = None
        for a in range(EXPERTS_PER_GROUP):
            for c in range(a + 1, EXPERTS_PER_GROUP):
                pair = r[a] + r[c]
                best = pair if best is None else jnp.maximum(best, pair)
        gs.append(best)
    gmax = functools.reduce(jnp.maximum, gs)
    gidx = jnp.full_like(gmax, float(N_GROUPS - 1))
    for g in range(N_GROUPS - 2, -1, -1):
        gidx = jnp.where(gs[g] == gmax, float(g), gidx)

    def pick(rows, k):
        out = rows[(N_GROUPS - 1) * EXPERTS_PER_GROUP + k]
        for g in range(N_GROUPS - 2, -1, -1):
            out = jnp.where(gidx == float(g), rows[g * EXPERTS_PER_GROUP + k], out)
        return out

    s = [pick(sel, k) for k in range(EXPERTS_PER_GROUP)]
    a = [pick(aff, k) for k in range(EXPERTS_PER_GROUP)]

    def first_max(vals):
        mx = functools.reduce(jnp.maximum, vals)
        idx = jnp.full_like(mx, float(len(vals) - 1))
        for k in range(len(vals) - 2, -1, -1):
            idx = jnp.where(vals[k] == mx, float(k), idx)
        return idx

    def take(vals, idx):
        out = vals[-1]
        for k in range(len(vals) - 2, -1, -1):
            out = jnp.where(idx == float(k), vals[k], out)
        return out

    i1 = first_max(s)
    s2 = [jnp.where(i1 == float(k), NEG_INF, s[k]) for k in range(EXPERTS_PER_GROUP)]
    i2 = first_max(s2)
    a1, a2 = take(a, i1), take(a, i2)
    tot = a1 + a2
    base = gidx * float(EXPERTS_PER_GROUP)
    return base + i1, base + i2, a1 / tot, a2 / tot


def _outproj_kernel(a_ref, b_ref, x_ref, wa_ref, wb_ref, g_ref, be_ref, rwt_ref, rb_ref,
                    x1_ref, route_ref, *, alpha):
    mix = _dot(a_ref[...], wa_ref[...]) + _dot(b_ref[...], wb_ref[...])
    x1 = _layer_norm(alpha * x_ref[...] + mix, g_ref[...], be_ref[...])
    x1_ref[...] = x1
    zt = lax.dot_general(rwt_ref[...], x1, (((1,), (1,)), ((), ())),
                         precision=HIGHEST, preferred_element_type=F32)
    afft = 1.0 / (1.0 + jnp.exp(-zt))
    selt = afft + rb_ref[...]
    aff = [afft[e:e + 1, :] for e in range(N_EXPERTS)]
    sel = [selt[e:e + 1, :] for e in range(N_EXPERTS)]
    e1, e2, w1, w2 = _route_rows(aff, sel)
    zero = jnp.zeros_like(e1)
    route_ref[...] = jnp.concatenate([e1, e2, w1, w2, zero, zero, zero, zero], axis=0)


def _outproj(a, b, xt, wa, wb, g, be, rwt, rb, alpha, tm):
    n, d = xt.shape
    full = lambda i: (0, 0)
    return pl.pallas_call(
        functools.partial(_outproj_kernel, alpha=alpha),
        out_shape=[jax.ShapeDtypeStruct((n, d), F32), jax.ShapeDtypeStruct((8, n), F32)],
        grid=(n // tm,),
        in_specs=[pl.BlockSpec((tm, a.shape[1]), lambda i: (i, 0)),
                  pl.BlockSpec((tm, b.shape[1]), lambda i: (i, 0)),
                  pl.BlockSpec((tm, d), lambda i: (i, 0)),
                  pl.BlockSpec(wa.shape, full), pl.BlockSpec(wb.shape, full),
                  pl.BlockSpec((1, d), full), pl.BlockSpec((1, d), full),
                  pl.BlockSpec(rwt.shape, full), pl.BlockSpec(rb.shape, full)],
        out_specs=[pl.BlockSpec((tm, d), lambda i: (i, 0)), pl.BlockSpec((8, tm), lambda i: (0, i))],
        compiler_params=_cparams(("parallel",)),
        name="outproj_ln_route",
    )(a, b, xt, wa, wb, g, be, rwt, rb)


def _moe_kernel(bexp_ref, bbase_ref, nact_ref, src_ref,
                x_hbm, wg_ref, wu_ref, wd_ref, o_ref, xbuf0, xbuf1, sem, *, tm):
    i = pl.program_id(0)
    nact = nact_ref[0]
    last = src_ref.shape[0] - 1
    bufs = (xbuf0, xbuf1)

    def issue(blk, slot):
        base = bbase_ref[blk]
        for r in range(tm):
            tok = src_ref[jnp.minimum(base + r, last)]
            pltpu.make_async_copy(x_hbm.at[pl.ds(tok, 1), :], bufs[slot].at[pl.ds(r, 1), :],
                                  sem.at[slot]).start(priority=r % 2)

    def wait(slot):
        pltpu.make_async_copy(x_hbm.at[pl.ds(0, tm), :], bufs[slot], sem.at[slot]).wait()

    @pl.when((i == 0) & (nact > 0))
    def _():
        issue(0, 0)

    def step(slot):
        wait(slot)

        @pl.when(i + 1 < nact)
        def _():
            issue(i + 1, 1 - slot)

        xb = bufs[slot][...].astype(BF16)
        gate = _dot(xb, wg_ref[...])
        up = _dot(xb, wu_ref[...])
        h = gate * (1.0 / (1.0 + jnp.exp(-gate))) * up
        o_ref[...] = _dot(h.astype(BF16), wd_ref[...])

    for slot in range(2):
        pl.when((i < nact) & (i % 2 == slot))(functools.partial(step, slot))

    @pl.when(i >= nact)
    def _():
        o_ref[...] = jnp.zeros_like(o_ref)


def _moe_experts(x1, wg, wu, wd, bexp, bbase, nact, src, tm):
    n, d = x1.shape
    nblk = bexp.shape[0]
    f = wg.shape[2]
    wmap = lambda i, bexp, *_: (bexp[i], 0, 0)
    grid_spec = pltpu.PrefetchScalarGridSpec(
        num_scalar_prefetch=4,
        grid=(nblk,),
        in_specs=[pl.BlockSpec(memory_space=pl.ANY),
                  pl.BlockSpec((None, d, f), wmap),
                  pl.BlockSpec((None, d, f), wmap),
                  pl.BlockSpec((None, f, d), wmap)],
        out_specs=pl.BlockSpec((tm, d), lambda i, *_: (i, 0)),
        scratch_shapes=[pltpu.VMEM((tm, d), F32), pltpu.VMEM((tm, d), F32), pltpu.SemaphoreType.DMA((2,))],
    )
    return pl.pallas_call(
        functools.partial(_moe_kernel, tm=tm),
        out_shape=jax.ShapeDtypeStruct((nblk * tm, d), F32),
        grid_spec=grid_spec,
        compiler_params=_cparams(("arbitrary",)),
        name="moe_experts",
    )(bexp, bbase, nact, src, x1, wg, wu, wd)


def _combine_kernel(p0_ref, p1_ref, x_ref, r_ref, g_ref, be_ref, o_hbm, y_ref, obuf0, obuf1, sem, *, alpha, tm):
    i = pl.program_id(0)
    nsteps = pl.num_programs(0)

    bufs = (obuf0, obuf1)

    def issue(blk, slot):
        base = blk * tm
        for r in range(tm):
            for k, p_ref in enumerate((p0_ref, p1_ref)):
                pltpu.make_async_copy(o_hbm.at[pl.ds(p_ref[base + r], 1), :],
                                      bufs[slot].at[k, pl.ds(r, 1), :], sem.at[slot]).start(priority=k)

    def wait(slot):
        for k in range(2):
            pltpu.make_async_copy(o_hbm.at[pl.ds(0, tm), :], bufs[slot].at[k], sem.at[slot]).wait()

    @pl.when(i == 0)
    def _():
        issue(0, 0)

    def step(slot):
        wait(slot)

        @pl.when(i + 1 < nsteps)
        def _():
            issue(i + 1, 1 - slot)

        r = r_ref[...]
        y = alpha * x_ref[...] + r[:, 2:3] * bufs[slot][0] + r[:, 3:4] * bufs[slot][1]
        y_ref[...] = _layer_norm(y, g_ref[...], be_ref[...])

    for slot in range(2):
        pl.when(i % 2 == slot)(functools.partial(step, slot))


def _combine(x1, route, g, be, o, p0, p1, alpha, tm):
    n, d = x1.shape
    grid_spec = pltpu.PrefetchScalarGridSpec(
        num_scalar_prefetch=2,
        grid=(n // tm,),
        in_specs=[pl.BlockSpec((tm, d), lambda i, *_: (i, 0)),
                  pl.BlockSpec((tm, 8), lambda i, *_: (i, 0)),
                  pl.BlockSpec((1, d), lambda i, *_: (0, 0)),
                  pl.BlockSpec((1, d), lambda i, *_: (0, 0)),
                  pl.BlockSpec(memory_space=pl.ANY)],
        out_specs=pl.BlockSpec((tm, d), lambda i, *_: (i, 0)),
        scratch_shapes=[pltpu.VMEM((2, tm, d), F32), pltpu.VMEM((2, tm, d), F32), pltpu.SemaphoreType.DMA((2,))],
    )
    return pl.pallas_call(
        functools.partial(_combine_kernel, alpha=alpha, tm=tm),
        out_shape=jax.ShapeDtypeStruct((n, d), F32),
        grid_spec=grid_spec,
        compiler_params=_cparams(("arbitrary",)),
        name="moe_combine_ln",
    )(p0, p1, x1, route, g, be, o)


def _moe_layer(x1, route_t, wg, wu, wd, g, be, alpha, tm_e, tm_c):
    n, d = x1.shape
    route = route_t.T
    e = route[:, 0:2].astype(I32).reshape(2 * n)
    onehot = (e[:, None] == jnp.arange(N_EXPERTS, dtype=I32)[None, :]).astype(I32)
    rank = jnp.take_along_axis(jnp.cumsum(onehot, axis=0) - onehot, e[:, None], axis=1)[:, 0]
    counts = jnp.sum(onehot, axis=0)
    nblk_e = (counts + tm_e - 1) // tm_e
    blk_start = jnp.cumsum(nblk_e) - nblk_e
    seg_start = jnp.cumsum(counts) - counts
    pos = blk_start[e] * tm_e + rank
    sorted_pos = seg_start[e] + rank
    src = jnp.zeros((2 * n,), I32).at[sorted_pos].set(jnp.arange(2 * n, dtype=I32) // 2)
    nblk = (2 * n) // tm_e + N_EXPERTS
    blk = jnp.arange(nblk, dtype=I32)
    blk_end = jnp.cumsum(nblk_e)
    bexp = jnp.minimum(jnp.sum((blk[:, None] >= blk_end[None, :]).astype(I32), axis=1), N_EXPERTS - 1)
    bbase = (seg_start[bexp] + (blk - blk_start[bexp]) * tm_e).astype(I32)
    nact = jnp.sum(nblk_e).astype(I32).reshape(1)
    o = _moe_experts(x1, wg, wu, wd, bexp, bbase, nact, src, tm_e)
    p = pos.reshape(n, 2).astype(I32)
    return _combine(x1, route, g, be, o, p[:, 0], p[:, 1], alpha, tm_c)


def _rms(x, g):
    return x * lax.rsqrt(jnp.mean(x * x, axis=-1, keepdims=True) + RMS_EPS) * g

ODD_OFF = dict(qc=0, kc=512, vc=640, qi=768, ki=1024, wi=1152, cq=1280, ckv=1536, kra=1664, krb=1792, end=1920)


def _odd_proj_kernel(x_ref, w_ref, qng_ref, kvng_ref, wqa_ref, wqb_ref, wk_ref, wv_ref, cos_ref, sin_ref,
                     tq_ref, tk_ref,
                     qc_ref, kc_ref, vc_ref, qi_ref, ki_ref, wi_ref, q8_ref, k8_ref, v8_ref, *, q_scale):
    o = ODD_OFF
    tm = x_ref.shape[0]
    lo = _lane_lo((tm, LANE))
    h = _dot(x_ref[...].astype(BF16), w_ref[...])
    kdup = h[:, o["kc"]:o["vc"]]
    vdup = h[:, o["vc"]:o["qi"]]
    kc_ref[:, 0:LANE] = jnp.where(lo, kdup, tk_ref[:, 0:LANE].astype(F32)).astype(BF16)
    kc_ref[:, LANE:2 * LANE] = jnp.where(lo, tk_ref[:, LANE:2 * LANE].astype(F32), kdup).astype(BF16)
    vc_ref[:, 0:LANE] = jnp.where(lo, vdup, 1.0).astype(BF16)
    vc_ref[:, LANE:2 * LANE] = jnp.where(lo, 1.0, vdup).astype(BF16)
    qi_ref[...] = h[:, o["qi"]:o["ki"]].astype(BF16)
    ki_ref[...] = h[:, o["ki"]:o["wi"]].astype(BF16)
    wi_ref[...] = h[:, o["wi"]:o["wi"] + IDX_HEADS]
    cos = cos_ref[...]
    sin = sin_ref[...]
    cqn = _rms(h[:, o["cq"]:o["ckv"]], qng_ref[...]).astype(BF16)
    qa = _dot(cqn, wqa_ref[...])
    qb = _dot(cqn, wqb_ref[...])
    ckvn = _rms(h[:, o["ckv"]:o["kra"]], kvng_ref[...]).astype(BF16)
    kn = _dot(ckvn, wk_ref[...])
    v8 = _dot(ckvn, wv_ref[...])
    kr = h[:, o["kra"]:o["krb"]] * cos + h[:, o["krb"]:o["end"]] * sin
    for hd in range(D_HEADS):
        sl = slice(hd * LANE, (hd + 1) * LANE)
        pj = slice((hd // 2) * LANE, (hd // 2 + 1) * LANE)
        own = lo if hd % 2 == 0 else ~lo
        qc_ref[:, sl] = jnp.where(own, h[:, pj], tq_ref[:, sl].astype(F32)).astype(BF16)
        q8_ref[:, sl] = ((qa[:, sl] * cos + qb[:, sl] * sin) * q_scale).astype(BF16)
        k8_ref[:, sl] = (kn[:, sl] + kr).astype(BF16)
        v8_ref[:, sl] = jnp.where(own, v8[:, pj], 1.0).astype(BF16)


def _odd_proj(xt, w_cat, qng, kvng, wqa, wqb, wk, wv, cos_t, sin_t, tq_t, tk_t, seq, tm):
    n, d = xt.shape
    full = lambda i: (0, 0)
    tps = seq // tm
    pos = lambda i: (i % tps, 0)
    widths = (1024, 256, 256, 256, 128)
    out_shape = [jax.ShapeDtypeStruct((n, w), BF16) for w in widths]
    out_specs = [pl.BlockSpec((tm, w), lambda i: (i, 0)) for w in widths]
    out_shape.append(jax.ShapeDtypeStruct((n, IDX_HEADS), F32))
    out_specs.append(pl.BlockSpec((tm, IDX_HEADS), lambda i: (i, 0)))
    for w in (1024, 1024, 1024):
        out_shape.append(jax.ShapeDtypeStruct((n, w), BF16))
        out_specs.append(pl.BlockSpec((tm, w), lambda i: (i, 0)))
    return pl.pallas_call(
        functools.partial(_odd_proj_kernel, q_scale=(MLA_NOPE + MLA_ROPE) ** -0.5 * LOG2E),
        out_shape=out_shape,
        grid=(n // tm,),
        in_specs=[pl.BlockSpec((tm, d), lambda i: (i, 0)),
                  pl.BlockSpec(w_cat.shape, full),
                  pl.BlockSpec(qng.shape, full), pl.BlockSpec(kvng.shape, full),
                  pl.BlockSpec(wqa.shape, full), pl.BlockSpec(wqb.shape, full),
                  pl.BlockSpec(wk.shape, full), pl.BlockSpec(wv.shape, full),
                  pl.BlockSpec((tm, LANE), pos), pl.BlockSpec((tm, LANE), pos),
                  pl.BlockSpec((tm, C_HEADS * LANE), pos), pl.BlockSpec((tm, 2 * LANE), pos)],
        out_specs=out_specs,
        compiler_params=_cparams(("parallel",)),
        name="odd_proj",
    )(xt, w_cat, qng, kvng, wqa, wqb, wk, wv, cos_t, sin_t, tq_t, tk_t)


def _dsa_kernel(q_ref, k_ref, v_ref, qi_ref, ki_ref, wt_ref, o_ref, sc_ref, mb_ref, *, top_k, chunk):
    i = pl.program_id(1)
    nc = (i * BLOCK + BLOCK + chunk - 1) // chunk
    lo = _lane_lo((BLOCK, LANE))
    s_row = lax.broadcasted_iota(I32, (chunk, BLOCK), 0)
    t_lane = i * BLOCK + lax.broadcasted_iota(I32, (chunk, BLOCK), 1)

    def chunk_start(c):
        return pl.multiple_of(c * chunk, chunk)

    qidx = []
    for h in range(IDX_HEADS):
        qp = qi_ref[:, (h // 2) * LANE:(h // 2 + 1) * LANE]
        qidx.append(jnp.where(lo if h % 2 == 0 else ~lo, qp, jnp.zeros_like(qp)))
    q_stack = jnp.concatenate(qidx, axis=0)
    w = wt_ref[...]

    npair = (nc + 1) // 2

    def score_pair(cp, _):
        for u in range(2):
            start = chunk_start(2 * cp + u)
            d = _dot_nt(ki_ref[pl.ds(start, chunk), :], q_stack)
            sc = jnp.zeros((chunk, BLOCK), F32)
            for h in range(IDX_HEADS):
                sc = sc + w[h:h + 1, :] * jnp.maximum(d[:, h * BLOCK:(h + 1) * BLOCK], 0.0)
            sc = jnp.where(sc == 0.0, 0.0, sc)
            sc_ref[pl.ds(start, chunk), :] = jnp.where(start + s_row <= t_lane, sc, NEG_INF)
        return 0

    lax.fori_loop(0, npair, score_pair, 0)

    nacc = 4 * 8

    def count(trial, strict):
        def body(cp, acc):
            for u in range(2):
                sc = sc_ref[pl.ds(chunk_start(2 * cp + u), chunk), :]
                hit = (sc > trial) if strict else (sc >= trial)
                acc = acc + jnp.sum(jnp.where(hit, 1.0, 0.0).reshape(chunk // nacc, nacc, BLOCK), axis=0)
            return acc

        acc = lax.fori_loop(0, npair, body, jnp.zeros((nacc, BLOCK), F32))
        return jnp.sum(acc, axis=0, keepdims=True)

    def decode(key):
        return lax.bitcast_convert_type(jnp.where(key < 0, key ^ jnp.int32(0x7FFFFFFF), key), F32)

    kf = float(top_k)
    key_neg_inf = jnp.int32(-2139095041)
    cand = jnp.where(count(jnp.zeros((1, BLOCK), F32), False) >= kf, 0, jnp.int32(-2 ** 31)).astype(I32)

    def bit_step(j, cand):
        trial = cand + lax.shift_left(jnp.int32(1), 30 - j)
        ok = (trial <= key_neg_inf) | (count(decode(trial), False) >= kf)
        return jnp.where(ok, trial, cand)

    thr = decode(lax.fori_loop(0, 31, bit_step, cand))
    need = kf - count(thr, True)

    r0 = lax.broadcasted_iota(I32, (chunk, chunk), 0)
    r1 = lax.broadcasted_iota(I32, (chunk, chunk), 1)
    below = jnp.where(r1 < r0, 1.0, 0.0).astype(BF16)

    def mask_pair(cp, offset):
        for u in range(2):
            start = chunk_start(2 * cp + u)
            sc = sc_ref[pl.ds(start, chunk), :]
            eqf = jnp.where(sc == thr, 1.0, 0.0)
            prefix = _dot(below, eqf.astype(BF16)) + offset
            sel = jnp.where(sc > thr, 1.0, jnp.where(prefix < need, eqf, 0.0))
            keep = jnp.where(start + s_row <= t_lane, sel, 0.0)
            mb_ref[:, pl.ds(start, chunk)] = jnp.where(keep > 0.0, 0.0, NEG_INF).T
            offset = offset + jnp.sum(eqf, axis=0, keepdims=True)
        return offset

    lax.fori_loop(0, npair, mask_pair, jnp.zeros((1, BLOCK), F32))

    def att_chunk(c, carry):
        start = chunk_start(c)
        qs = [q_ref[:, h * LANE:(h + 1) * LANE] for h in range(C_HEADS)]
        ks = [k_ref[pl.ds(start, chunk), (h % 2) * LANE:(h % 2 + 1) * LANE] for h in range(C_HEADS)]
        vs = [v_ref[pl.ds(start, chunk), (h % 2) * LANE:(h % 2 + 1) * LANE] for h in range(C_HEADS)]
        return _online_softmax_chains(qs, ks, vs, carry, bias=mb_ref[:, pl.ds(start, chunk)], guard_empty=True)

    init = tuple((jnp.full((BLOCK, 1), NEG_INF, F32), jnp.zeros((BLOCK, LANE), F32)) for _ in range(C_HEADS))
    carry = lax.fori_loop(0, nc, att_chunk, init)
    _normalize_pairs([c[1] for c in carry], o_ref)


def _dsa(qc, kc, vc, qi, ki, wit):
    b, s, _ = qc.shape
    top_k = min(DSA_TOPK_MAX, s // 4)
    chunk = 256
    assert s % (2 * chunk) == 0 and top_k <= chunk
    blk = lambda bi, i: (bi, i, 0)
    seq = lambda bi, i: (bi, 0, 0)
    return pl.pallas_call(
        functools.partial(_dsa_kernel, top_k=top_k, chunk=chunk),
        out_shape=jax.ShapeDtypeStruct((b, s, C_HEADS * HEAD_DIM), BF16),
        grid=(b, s // BLOCK),
        in_specs=[pl.BlockSpec((None, BLOCK, C_HEADS * LANE), blk),
                  pl.BlockSpec((None, s, 2 * LANE), seq),
                  pl.BlockSpec((None, s, 2 * LANE), seq),
                  pl.BlockSpec((None, BLOCK, 256), blk),
                  pl.BlockSpec((None, s, LANE), seq),
                  pl.BlockSpec((None, IDX_HEADS, BLOCK), lambda bi, i: (bi, 0, i))],
        out_specs=pl.BlockSpec((None, BLOCK, 512), blk),
        scratch_shapes=[pltpu.VMEM((s, BLOCK), F32), pltpu.VMEM((BLOCK, s), F32)],
        compiler_params=_cparams(("parallel", "arbitrary")),
        name="dsa_attn",
    )(qc, kc, vc, qi, ki, wit)


def _dup64(w):
    d, c = w.shape
    w = w.reshape(d, c // HEAD_DIM, 1, HEAD_DIM)
    return jnp.broadcast_to(w, (d, c // HEAD_DIM, 2, HEAD_DIM)).reshape(d, 2 * c)


def _pad_cols(w, width):
    return jnp.pad(w, ((0, 0), (0, width - w.shape[1])))


def _split_cols(w, sizes):
    out, start = [], 0
    for n in sizes:
        out.append(w[:, start:start + n])
        start += n
    return out


def _rope_partner(w):
    half = MLA_ROPE // 2
    return jnp.concatenate([-w[..., half:], w[..., :half]], axis=-1)


def _rope_block(w_rope):
    d = w_rope.shape[0]
    return jnp.concatenate([jnp.zeros((d, MLA_NOPE), F32), w_rope,
                            jnp.zeros((d, LANE - MLA_NOPE - MLA_ROPE), F32)], axis=1)


def _alibi_tables(s):
    slopes = 2.0 ** (-8.0 * jnp.arange(1, C_HEADS + 1, dtype=F32) / C_HEADS)
    a = slopes * LOG2E
    pos = jnp.arange(s, dtype=F32)
    a_sp = _split3(a)
    r_sp = _split3(-(a[None, :] * pos[:, None]))
    ones = jnp.ones((s, C_HEADS), F32)
    q_terms = [128.0 * t * ones for t in a_sp] + [t * ones for t in a_sp] + list(r_sp)
    q_aug = jnp.stack(q_terms, axis=-1)
    q_aug = jnp.pad(q_aug, ((0, 0), (0, 0), (0, HEAD_DIM - q_aug.shape[-1])))
    zeros = jnp.zeros_like(q_aug)
    even = jnp.concatenate([zeros, q_aug], axis=-1)
    odd = jnp.concatenate([q_aug, zeros], axis=-1)
    parity = (jnp.arange(C_HEADS) % 2 == 0)[None, :, None]
    tq_t = jnp.where(parity, even, odd).reshape(s, C_HEADS * LANE)
    s_hi = jnp.floor(pos / LANE)
    s_lo = pos - LANE * s_hi
    k_aug = jnp.stack([s_hi] * 3 + [s_lo] * 3 + [jnp.ones_like(pos)] * 3, axis=-1)
    k_aug = jnp.pad(k_aug, ((0, 0), (0, HEAD_DIM - k_aug.shape[-1])))
    kz = jnp.zeros_like(k_aug)
    tk_t = jnp.concatenate([kz, k_aug, k_aug, kz], axis=-1)
    return tq_t.astype(BF16), tk_t.astype(BF16)


def kernel(x, even_w_in, even_b_f, even_sinks, odd_w_in, odd_q_norm, odd_kv_norm, odd_w_uq, odd_w_ukv,
           w_o, ln_g, ln_b, router_w, router_b, moe_w_gate, moe_w_up, moe_w_down):
    b, s, d = x.shape
    n = b * s
    depth = w_o.shape[0]
    alpha = (2 * depth) ** 0.25
    tm = min(512, s)
    tq = min(256, s)
    tm_e = min(256, n)
    tm_c = min(256, n)
    att_scale = HEAD_DIM ** -0.5
    xt = x.reshape(n, d)
    rwt = router_w.T
    rb = router_b.reshape(N_EXPERTS, 1)

    for l in range(depth):
        j = l // 2
        if l % 2 == 0:
            qa, ka, va, qb, kb, vb, fg = _split_cols(
                even_w_in[j], (512, 128, 128, 512, 512, 512, B_HEADS))
            w_cat = jnp.concatenate([qa * att_scale, _dup64(ka), _dup64(va), qb * (att_scale * LOG2E), kb, vb,
                                     _pad_cols(fg, LANE)], axis=1).astype(BF16)
            bf_pad = _pad_cols(even_b_f[j].reshape(1, B_HEADS), LANE)
            qa_, ka_, va_, qf, kf, vf = _even_proj(xt, w_cat, bf_pad, s, tm)
            r3 = lambda a: a.reshape(b, s, a.shape[1])
            out_a = _swa(even_sinks[j], r3(qa_), r3(ka_), r3(va_))
            out_b = _flash(r3(qf), r3(kf), r3(vf), tq=tq, nh=4, name="fox_attn")
        else:
            (qc, kc, vc, qi, ki, wi, cq, ckv, kr) = _split_cols(
                odd_w_in[j], (512, 64, 64, IDX_HEADS * IDX_DIM, IDX_DIM, IDX_HEADS,
                              MLA_Q_RANK, MLA_KV_RANK, MLA_ROPE))
            w_cat = jnp.concatenate([
                qc * (att_scale * LOG2E), _dup64(kc), _dup64(vc), qi * (IDX_DIM ** -0.5), _dup64(ki),
                _pad_cols(wi * (IDX_HEADS ** -0.5), LANE), cq, ckv,
                _rope_block(kr), _rope_block(_rope_partner(kr))], axis=1).astype(BF16)
            dq = MLA_NOPE + MLA_ROPE
            wuq = odd_w_uq[j].reshape(MLA_Q_RANK, D_HEADS, dq)
            pad_q = jnp.zeros((MLA_Q_RANK, D_HEADS, LANE - dq), F32)
            wqa = jnp.concatenate([wuq, pad_q], axis=2).reshape(MLA_Q_RANK, D_HEADS * LANE).astype(BF16)
            wqb = jnp.concatenate([jnp.zeros((MLA_Q_RANK, D_HEADS, MLA_NOPE), F32),
                                   _rope_partner(wuq[..., MLA_NOPE:]), pad_q],
                                  axis=2).reshape(MLA_Q_RANK, D_HEADS * LANE).astype(BF16)
            wukv = odd_w_ukv[j].reshape(MLA_KV_RANK, D_HEADS, MLA_NOPE + MLA_V)
            wk = jnp.concatenate([wukv[..., :MLA_NOPE], jnp.zeros((MLA_KV_RANK, D_HEADS, LANE - MLA_NOPE), F32)],
                                 axis=2).reshape(MLA_KV_RANK, D_HEADS * LANE).astype(BF16)
            wv = wukv[..., MLA_NOPE:].reshape(MLA_KV_RANK, D_HEADS * MLA_V).astype(BF16)
            half = MLA_ROPE // 2
            inv = ROPE_THETA ** (-jnp.arange(half, dtype=F32) / half)
            ang = jnp.arange(s, dtype=F32)[:, None] * inv[None, :]
            cos_h, sin_h = jnp.cos(ang), jnp.sin(ang)
            pad_t = jnp.zeros((s, LANE - dq), F32)
            cos_t = jnp.concatenate([jnp.ones((s, MLA_NOPE), F32), cos_h, cos_h, pad_t], axis=1)
            sin_t = jnp.concatenate([jnp.zeros((s, MLA_NOPE), F32), sin_h, sin_h, pad_t], axis=1)
            tq_t, tk_t = _alibi_tables(s)
            qc_, kc_, vc_, qi_, ki_, wi_, q8, k8, v8 = _odd_proj(
                xt, w_cat, odd_q_norm[j].reshape(1, -1), odd_kv_norm[j].reshape(1, -1),
                wqa, wqb, wk, wv, cos_t, sin_t, tq_t, tk_t, s, tm)
            r3 = lambda a: a.reshape(b, s, a.shape[1])
            out_a = _dsa(r3(qc_), r3(kc_), r3(vc_), r3(qi_), r3(ki_), r3(wi_).transpose(0, 2, 1))
            out_b = _flash(r3(q8), r3(k8), r3(v8), tq=tq, nh=4, name="mla_attn")
        half_w = out_a.shape[2]
        wo = w_o[l].astype(BF16)
        x1, route_t = _outproj(out_a.reshape(n, half_w), out_b.reshape(n, -1), xt,
                               wo[:half_w], wo[half_w:], ln_g[l, 0].reshape(1, d), ln_b[l, 0].reshape(1, d),
                               rwt, rb, alpha, tm)
        xt = _moe_layer(x1, route_t, moe_w_gate[l].astype(BF16), moe_w_up[l].astype(BF16),
                        moe_w_down[l].astype(BF16), ln_g[l, 1].reshape(1, d), ln_b[l, 1].reshape(1, d),
                        alpha, tm_e, tm_c)
    return xt.reshape(b, s, d)
```

```python
import functools
import math

import jax
import jax.numpy as jnp
from jax import lax
from jax.experimental import pallas as pl
from jax.experimental.pallas import tpu as pltpu

F32 = jnp.float32
BF16 = jnp.bfloat16
I32 = jnp.int32

LANE = 128
HEAD_DIM = 64
BLOCK = 128
A_HEADS, A_KV_HEADS, WINDOW = 8, 2, 128
B_HEADS = 8
C_HEADS, IDX_HEADS, IDX_DIM, DSA_TOPK_MAX = 8, 4, 64, 256
D_HEADS, MLA_Q_RANK, MLA_KV_RANK, MLA_NOPE, MLA_ROPE, MLA_V = 8, 256, 128, 64, 32, 64
ROPE_THETA = 10000.0
N_EXPERTS, N_GROUPS = 16, 4
EXPERTS_PER_GROUP = N_EXPERTS // N_GROUPS
LN_EPS, RMS_EPS = 1e-5, 1e-6
NEG_INF = float("-inf")
LOG2E = math.log2(math.e)
VMEM_LIMIT = 56 * 1024 * 1024


def _cparams(sem):
    return pltpu.CompilerParams(dimension_semantics=sem, vmem_limit_bytes=VMEM_LIMIT)


def _dot(a, b):
    return jnp.dot(a, b, preferred_element_type=F32)


def _dot_nt(a, b):
    return lax.dot_general(a, b, (((1,), (1,)), ((), ())), preferred_element_type=F32)


def _lane_lo(shape):
    return lax.broadcasted_iota(I32, shape, len(shape) - 1) % LANE < HEAD_DIM


def _layer_norm(y, g, b):
    mu = jnp.mean(y, axis=-1, keepdims=True)
    yc = y - mu
    var = jnp.mean(yc * yc, axis=-1, keepdims=True)
    return yc * lax.rsqrt(var + LN_EPS) * g + b


def _top16(v):
    bits = lax.bitcast_convert_type(v, I32) & jnp.int32(-65536)
    return lax.bitcast_convert_type(bits, F32)


def _split3(x):
    s1 = _top16(x)
    r = x - s1
    s2 = _top16(r)
    return s1, s2, r - s2


def _aug_rel(shape, h):
    lane = lax.broadcasted_iota(I32, shape, 1)
    return lane - (HEAD_DIM if h % 2 == 0 else 0)


def _even_proj_kernel(x_ref, w_ref, bf_ref, qa_ref, ka_ref, va_ref, qf_ref, kf_ref, vf_ref,
                      carry_ref, *, tiles_per_seq):
    t = pl.program_id(0)
    tm = x_ref.shape[0]
    h = _dot(x_ref[...].astype(BF16), w_ref[...])
    qa_ref[...] = h[:, 0:512].astype(BF16)
    ka_ref[...] = h[:, 512:768].astype(BF16)
    va_ref[...] = h[:, 768:1024].astype(BF16)
    z = h[:, 2560:2688] + bf_ref[...]
    logf = jnp.minimum(z, 0.0) - jnp.log1p(jnp.exp(-jnp.abs(z)))

    @pl.when(t % tiles_per_seq == 0)
    def _():
        carry_ref[...] = jnp.zeros_like(carry_ref)

    row = lax.broadcasted_iota(I32, (tm, tm), 0)
    col = lax.broadcasted_iota(I32, (tm, tm), 1)
    tri = jnp.where(row >= col, 1.0, 0.0).astype(BF16)
    c = carry_ref[...]
    for part in _split3(logf):
        c = c + _dot(tri, part.astype(BF16))
    carry_ref[...] = c[tm - 1:tm, :]

    s1, s2, s3 = _split3(c * LOG2E)
    lo = _lane_lo((tm, LANE))
    for hd in range(B_HEADS):
        own = lo if hd % 2 == 0 else ~lo
        rel = _aug_rel((tm, LANE), hd)
        b1, b2, b3 = s1[:, hd:hd + 1], s2[:, hd:hd + 1], s3[:, hd:hd + 1]
        q_aug = jnp.where(rel == 3, b1, jnp.where(rel == 4, b2, jnp.where(rel == 5, b3,
                          jnp.where((rel >= 0) & (rel < 3), -1.0, 0.0))))
        k_aug = jnp.where(rel == 0, b1, jnp.where(rel == 1, b2, jnp.where(rel == 2, b3,
                          jnp.where((rel >= 3) & (rel < 6), 1.0, 0.0))))
        pj = (hd // 2) * LANE
        sl = slice(hd * LANE, (hd + 1) * LANE)
        qf_ref[:, sl] = jnp.where(own, h[:, 1024 + pj:1024 + pj + LANE], q_aug).astype(BF16)
        kf_ref[:, sl] = jnp.where(own, h[:, 1536 + pj:1536 + pj + LANE], k_aug).astype(BF16)
        vf_ref[:, sl] = jnp.where(own, h[:, 2048 + pj:2048 + pj + LANE], 1.0).astype(BF16)


def _even_proj(xt, w_cat, bf_pad, seq, tm):
    n, d = xt.shape
    wcols = w_cat.shape[1]
    widths = (512, 256, 256, 1024, 1024, 1024)
    out_shape = [jax.ShapeDtypeStruct((n, w), BF16) for w in widths]
    out_specs = [pl.BlockSpec((tm, w), lambda i: (i, 0)) for w in widths]
    return pl.pallas_call(
        functools.partial(_even_proj_kernel, tiles_per_seq=seq // tm),
        out_shape=out_shape,
        grid=(n // tm,),
        in_specs=[pl.BlockSpec((tm, d), lambda i: (i, 0)),
                  pl.BlockSpec((d, wcols), lambda i: (0, 0)),
                  pl.BlockSpec((1, LANE), lambda i: (0, 0))],
        out_specs=out_specs,
        scratch_shapes=[pltpu.VMEM((1, LANE), F32)],
        compiler_params=_cparams(("arbitrary",)),
        name="even_proj",
    )(xt, w_cat, bf_pad)


def _swa_kernel(sink_ref, q_ref, kp_ref, kc_ref, vp_ref, vc_ref, o_ref):
    i = pl.program_id(1)
    group = A_HEADS // A_KV_HEADS
    tl = lax.broadcasted_iota(I32, (BLOCK, 2 * BLOCK), 0)
    sl = lax.broadcasted_iota(I32, (BLOCK, 2 * BLOCK), 1)
    dist = BLOCK + tl - sl
    valid = (dist >= 0) & (dist < WINDOW) & ((sl >= BLOCK) | (i > 0))
    distf = dist.astype(F32)
    lo = _lane_lo((BLOCK, LANE))
    outs = []
    for h in range(A_HEADS):
        g = h // group
        qp = q_ref[:, (h // 2) * LANE:(h // 2 + 1) * LANE]
        qh = jnp.where(lo if h % 2 == 0 else ~lo, qp, jnp.zeros_like(qp))
        kband = jnp.concatenate([kp_ref[:, g * LANE:(g + 1) * LANE],
                                 kc_ref[:, g * LANE:(g + 1) * LANE]], axis=0)
        vband = jnp.concatenate([vp_ref[:, g * LANE:(g + 1) * LANE],
                                 vc_ref[:, g * LANE:(g + 1) * LANE]], axis=0)
        slope = 2.0 ** (-8.0 * (h + 1) / A_HEADS)
        logits = _dot_nt(qh, kband) - slope * distf
        logits = jnp.where(valid, logits, NEG_INF)
        sink = sink_ref[h]
        m = jnp.maximum(jnp.max(logits, axis=-1, keepdims=True), sink)
        e = jnp.exp(logits - m)
        denom = jnp.sum(e, axis=-1, keepdims=True) + jnp.exp(sink - m)
        outs.append(_dot(e.astype(BF16), vband) / denom)
    for j in range(A_HEADS // 2):
        o_ref[:, j * LANE:(j + 1) * LANE] = jnp.where(lo, outs[2 * j], outs[2 * j + 1]).astype(BF16)


def _swa(sinks, qa, ka, va):
    b, s, _ = qa.shape
    nb = s // BLOCK
    prev = lambda bi, i: (bi, jnp.maximum(i - 1, 0), 0)
    cur = lambda bi, i: (bi, i, 0)
    return pl.pallas_call(
        _swa_kernel,
        out_shape=jax.ShapeDtypeStruct((b, s, A_HEADS * HEAD_DIM), BF16),
        grid=(b, nb),
        in_specs=[pl.BlockSpec(memory_space=pltpu.SMEM),
                  pl.BlockSpec((None, BLOCK, 512), cur),
                  pl.BlockSpec((None, BLOCK, 256), prev),
                  pl.BlockSpec((None, BLOCK, 256), cur),
                  pl.BlockSpec((None, BLOCK, 256), prev),
                  pl.BlockSpec((None, BLOCK, 256), cur)],
        out_specs=pl.BlockSpec((None, BLOCK, 512), cur),
        compiler_params=_cparams(("parallel", "arbitrary")),
        name="swa_attn",
    )(sinks, qa, ka, ka, va, va)


def _normalize_pairs(accs, o_ref):
    lo = _lane_lo(accs[0].shape)
    outs = [a / pltpu.roll(a, HEAD_DIM, 1) for a in accs]
    for j in range(len(accs) // 2):
        o_ref[:, j * LANE:(j + 1) * LANE] = jnp.where(lo, outs[2 * j], outs[2 * j + 1]).astype(BF16)


def _online_softmax_chains(qs, ks, vs, carry, bias=None, masks=None, guard_empty=False):
    n = len(qs)
    ss = [_dot_nt(qs[i], ks[i]) for i in range(n)]
    if bias is not None:
        ss = [s + bias for s in ss]
    if masks is not None:
        ss = [jnp.where(masks[i], ss[i], NEG_INF) for i in range(n)]
    ms, ps, alphas = [], [], []
    for i in range(n):
        m = carry[i][0]
        m_new = jnp.maximum(m, jnp.max(ss[i], axis=-1, keepdims=True))
        m_ref = jnp.where(m_new == NEG_INF, 0.0, m_new) if guard_empty else m_new
        ps.append(jnp.exp2(ss[i] - m_ref).astype(BF16))
        alphas.append(jnp.exp2(m - m_ref))
        ms.append(m_new)
    pvs = [_dot(ps[i], vs[i]) for i in range(n)]
    return tuple((ms[i], alphas[i] * carry[i][1] + pvs[i]) for i in range(n))


def _flash_kernel(q_ref, k_ref, v_ref, o_ref, *, tq, nh):
    qi = pl.program_id(2)
    nslab = tq // BLOCK
    chains = [(hh, r) for hh in range(nh) for r in range(nslab)]
    row = lax.broadcasted_iota(I32, (BLOCK, tq), 0)
    col = lax.broadcasted_iota(I32, (BLOCK, tq), 1)

    def lanes(hh):
        return slice(hh * LANE, (hh + 1) * LANE)

    def step(kb, carry, diagonal):
        start = pl.multiple_of(kb * tq, tq)
        qs = [q_ref[r * BLOCK:(r + 1) * BLOCK, lanes(hh)] for hh, r in chains]
        ks = [k_ref[pl.ds(start, tq), lanes(hh)] for hh, r in chains]
        vs = [v_ref[pl.ds(start, tq), lanes(hh)] for hh, r in chains]
        masks = [col <= row + r * BLOCK for hh, r in chains] if diagonal else None
        return _online_softmax_chains(qs, ks, vs, carry, masks=masks)

    init = tuple((jnp.full((BLOCK, 1), NEG_INF, F32), jnp.zeros((BLOCK, LANE), F32))
                 for _ in range(nh * nslab))
    carry = lax.fori_loop(0, qi, lambda kb, c: step(kb, c, False), init)
    carry = step(qi, carry, True)
    accs = [jnp.concatenate([carry[hh * nslab + r][1] for r in range(nslab)], axis=0) for hh in range(nh)]
    _normalize_pairs(accs, o_ref)


def _flash(q, k, v, *, tq, nh, name):
    b, s, w = v.shape
    return pl.pallas_call(
        functools.partial(_flash_kernel, tq=tq, nh=nh),
        out_shape=jax.ShapeDtypeStruct((b, s, w // 2), BF16),
        grid=(b, w // (nh * LANE), s // tq),
        in_specs=[pl.BlockSpec((None, tq, nh * LANE), lambda bi, j, i: (bi, i, j)),
                  pl.BlockSpec((None, s, nh * LANE), lambda bi, j, i: (bi, 0, j)),
                  pl.BlockSpec((None, s, nh * LANE), lambda bi, j, i: (bi, 0, j))],
        out_specs=pl.BlockSpec((None, tq, nh * LANE // 2), lambda bi, j, i: (bi, i, j)),
        compiler_params=_cparams(("parallel", "parallel", "arbitrary")),
        name=name,
    )(q, k, v)


def _route_rows(aff, sel):
    gs = []
    for g in range(N_GROUPS):
        r = sel[g * EXPERTS_PER_GROUP:(g + 1) * EXPERTS_PER_GROUP]
        best = None
        for a in range(EXPERTS_PER_GROUP):
            for c in range(a + 1, EXPERTS_PER_GROUP):
                pair = r[a] + r[c]
                best = pair if best is None else jnp.maximum(best, pair)
        gs.append(best)
    gmax = functools.reduce(jnp.maximum, gs)
    gidx = jnp.full_like(gmax, float(N_GROUPS - 1))
    for g in range(N_GROUPS - 2, -1, -1):
        gidx = jnp.where(gs[g] == gmax, float(g), gidx)

    def pick(rows, k):
        out = rows[(N_GROUPS - 1) * EXPERTS_PER_GROUP + k]
        for g in range(N_GROUPS - 2, -1, -1):
            out = jnp.where(gidx == float(g), rows[g * EXPERTS_PER_GROUP + k], out)
        return out

    s = [pick(sel, k) for k in range(EXPERTS_PER_GROUP)]
    a = [pick(aff, k) for k in range(EXPERTS_PER_GROUP)]

    def first_max(vals):
        mx = functools.reduce(jnp.maximum, vals)
        idx = jnp.full_like(mx, float(len(vals) - 1))
        for k in range(len(vals) - 2, -1, -1):
            idx = jnp.where(vals[k] == mx, float(k), idx)
        return idx

    def take(vals, idx):
        out = vals[-1]
        for k in range(len(vals) - 2, -1, -1):
            out = jnp.where(idx == float(k), vals[k], out)
        return out

    i1 = first_max(s)
    s2 = [jnp.where(i1 == float(k), NEG_INF, s[k]) for k in range(EXPERTS_PER_GROUP)]
    i2 = first_max(s2)
    a1, a2 = take(a, i1), take(a, i2)
    tot = a1 + a2
    base = gidx * float(EXPERTS_PER_GROUP)
    return base + i1, base + i2, a1 / tot, a2 / tot


def _outproj_kernel(a_ref, b_ref, x_ref, wa_ref, wb_ref, g_ref, be_ref, rwt_ref, rb_ref,
                    x1_ref, route_ref, *, alpha):
    mix = _dot(a_ref[...], wa_ref[...]) + _dot(b_ref[...], wb_ref[...])
    x1 = _layer_norm(alpha * x_ref[...] + mix, g_ref[...], be_ref[...])
    x1_ref[...] = x1
    x_hi = _top16(x1)
    x_lo = (x1 - x_hi).astype(BF16)
    z_hi = _dot_nt(rwt_ref[...], x_hi.astype(BF16))
    zt = z_hi[0:N_EXPERTS] + z_hi[N_EXPERTS:2 * N_EXPERTS] + _dot_nt(rwt_ref[0:N_EXPERTS, :], x_lo)
    afft = 1.0 / (1.0 + jnp.exp(-zt))
    selt = afft + rb_ref[...]
    aff = [afft[e:e + 1, :] for e in range(N_EXPERTS)]
    sel = [selt[e:e + 1, :] for e in range(N_EXPERTS)]
    e1, e2, w1, w2 = _route_rows(aff, sel)
    zero = jnp.zeros_like(e1)
    route_ref[...] = jnp.concatenate([e1, e2, w1, w2, zero, zero, zero, zero], axis=0)


def _outproj(a, b, xt, wa, wb, g, be, rwt, rb, alpha, tm):
    n, d = xt.shape
    full = lambda i: (0, 0)
    return pl.pallas_call(
        functools.partial(_outproj_kernel, alpha=alpha),
        out_shape=[jax.ShapeDtypeStruct((n, d), F32), jax.ShapeDtypeStruct((8, n), F32)],
        grid=(n // tm,),
        in_specs=[pl.BlockSpec((tm, a.shape[1]), lambda i: (i, 0)),
                  pl.BlockSpec((tm, b.shape[1]), lambda i: (i, 0)),
                  pl.BlockSpec((tm, d), lambda i: (i, 0)),
                  pl.BlockSpec(wa.shape, full), pl.BlockSpec(wb.shape, full),
                  pl.BlockSpec((1, d), full), pl.BlockSpec((1, d), full),
                  pl.BlockSpec(rwt.shape, full), pl.BlockSpec(rb.shape, full)],
        out_specs=[pl.BlockSpec((tm, d), lambda i: (i, 0)), pl.BlockSpec((8, tm), lambda i: (0, i))],
        compiler_params=_cparams(("parallel",)),
        name="outproj_ln_route",
    )(a, b, xt, wa, wb, g, be, rwt, rb)


def _moe_kernel(bexp_ref, bbase_ref, nact_ref, src_ref,
                x_hbm, wg_ref, wu_ref, wd_ref, o_ref, xbuf0, xbuf1, sem, *, tm):
    i = pl.program_id(0)
    nact = nact_ref[0]
    last = src_ref.shape[0] - 1
    bufs = (xbuf0, xbuf1)

    def issue(blk, slot):
        base = bbase_ref[blk]
        for r in range(tm):
            tok = src_ref[jnp.minimum(base + r, last)]
            pltpu.make_async_copy(x_hbm.at[pl.ds(tok, 1), :], bufs[slot].at[pl.ds(r, 1), :],
                                  sem.at[slot]).start(priority=r % 2)

    def wait(slot):
        pltpu.make_async_copy(x_hbm.at[pl.ds(0, tm), :], bufs[slot], sem.at[slot]).wait()

    @pl.when((i == 0) & (nact > 0))
    def _():
        issue(0, 0)

    def step(slot):
        wait(slot)

        @pl.when(i + 1 < nact)
        def _():
            issue(i + 1, 1 - slot)

        xb = bufs[slot][...].astype(BF16)
        gate = _dot(xb, wg_ref[...])
        up = _dot(xb, wu_ref[...])
        h = gate * (1.0 / (1.0 + jnp.exp(-gate))) * up
        o_ref[...] = _dot(h.astype(BF16), wd_ref[...])

    for slot in range(2):
        pl.when((i < nact) & (i % 2 == slot))(functools.partial(step, slot))

    @pl.when(i >= nact)
    def _():
        o_ref[...] = jnp.zeros_like(o_ref)


def _moe_experts(x1, wg, wu, wd, bexp, bbase, nact, src, tm):
    n, d = x1.shape
    nblk = bexp.shape[0]
    f = wg.shape[2]
    wmap = lambda i, bexp, *_: (bexp[i], 0, 0)
    grid_spec = pltpu.PrefetchScalarGridSpec(
        num_scalar_prefetch=4,
        grid=(nblk,),
        in_specs=[pl.BlockSpec(memory_space=pl.ANY),
                  pl.BlockSpec((None, d, f), wmap),
                  pl.BlockSpec((None, d, f), wmap),
                  pl.BlockSpec((None, f, d), wmap)],
        out_specs=pl.BlockSpec((tm, d), lambda i, *_: (i, 0)),
        scratch_shapes=[pltpu.VMEM((tm, d), F32), pltpu.VMEM((tm, d), F32), pltpu.SemaphoreType.DMA((2,))],
    )
    return pl.pallas_call(
        functools.partial(_moe_kernel, tm=tm),
        out_shape=jax.ShapeDtypeStruct((nblk * tm, d), F32),
        grid_spec=grid_spec,
        compiler_params=_cparams(("arbitrary",)),
        name="moe_experts",
    )(bexp, bbase, nact, src, x1, wg, wu, wd)


def _combine_kernel(p0_ref, p1_ref, x_ref, r_ref, g_ref, be_ref, o_hbm, y_ref, obuf0, obuf1, sem, *, alpha, tm):
    i = pl.program_id(0)
    nsteps = pl.num_programs(0)

    bufs = (obuf0, obuf1)

    def issue(blk, slot):
        base = blk * tm
        for r in range(tm):
            for k, p_ref in enumerate((p0_ref, p1_ref)):
                pltpu.make_async_copy(o_hbm.at[pl.ds(p_ref[base + r], 1), :],
                                      bufs[slot].at[k, pl.ds(r, 1), :], sem.at[slot]).start(priority=k)

    def wait(slot):
        for k in range(2):
            pltpu.make_async_copy(o_hbm.at[pl.ds(0, tm), :], bufs[slot].at[k], sem.at[slot]).wait()

    @pl.when(i == 0)
    def _():
        issue(0, 0)

    def step(slot):
        wait(slot)

        @pl.when(i + 1 < nsteps)
        def _():
            issue(i + 1, 1 - slot)

        r = r_ref[...]
        y = alpha * x_ref[...] + r[:, 2:3] * bufs[slot][0] + r[:, 3:4] * bufs[slot][1]
        y_ref[...] = _layer_norm(y, g_ref[...], be_ref[...])

    for slot in range(2):
        pl.when(i % 2 == slot)(functools.partial(step, slot))


def _combine(x1, route, g, be, o, p0, p1, alpha, tm):
    n, d = x1.shape
    grid_spec = pltpu.PrefetchScalarGridSpec(
        num_scalar_prefetch=2,
        grid=(n // tm,),
        in_specs=[pl.BlockSpec((tm, d), lambda i, *_: (i, 0)),
                  pl.BlockSpec((tm, 8), lambda i, *_: (i, 0)),
                  pl.BlockSpec((1, d), lambda i, *_: (0, 0)),
                  pl.BlockSpec((1, d), lambda i, *_: (0, 0)),
                  pl.BlockSpec(memory_space=pl.ANY)],
        out_specs=pl.BlockSpec((tm, d), lambda i, *_: (i, 0)),
        scratch_shapes=[pltpu.VMEM((2, tm, d), F32), pltpu.VMEM((2, tm, d), F32), pltpu.SemaphoreType.DMA((2,))],
    )
    return pl.pallas_call(
        functools.partial(_combine_kernel, alpha=alpha, tm=tm),
        out_shape=jax.ShapeDtypeStruct((n, d), F32),
        grid_spec=grid_spec,
        compiler_params=_cparams(("arbitrary",)),
        name="moe_combine_ln",
    )(p0, p1, x1, route, g, be, o)


def _moe_layer(x1, route_t, wg, wu, wd, g, be, alpha, tm_e, tm_c):
    n, d = x1.shape
    route = route_t.T
    e = route[:, 0:2].astype(I32).reshape(2 * n)
    onehot = (e[:, None] == jnp.arange(N_EXPERTS, dtype=I32)[None, :]).astype(I32)
    rank = jnp.take_along_axis(jnp.cumsum(onehot, axis=0) - onehot, e[:, None], axis=1)[:, 0]
    counts = jnp.sum(onehot, axis=0)
    nblk_e = (counts + tm_e - 1) // tm_e
    blk_start = jnp.cumsum(nblk_e) - nblk_e
    seg_start = jnp.cumsum(counts) - counts
    pos = blk_start[e] * tm_e + rank
    src = (jnp.argsort(e, stable=True) // 2).astype(I32)
    nblk = (2 * n) // tm_e + N_EXPERTS
    blk = jnp.arange(nblk, dtype=I32)
    blk_end = jnp.cumsum(nblk_e)
    bexp = jnp.minimum(jnp.sum((blk[:, None] >= blk_end[None, :]).astype(I32), axis=1), N_EXPERTS - 1)
    bbase = (seg_start[bexp] + (blk - blk_start[bexp]) * tm_e).astype(I32)
    nact = jnp.sum(nblk_e).astype(I32).reshape(1)
    o = _moe_experts(x1, wg, wu, wd, bexp, bbase, nact, src, tm_e)
    p = pos.reshape(n, 2).astype(I32)
    return _combine(x1, route, g, be, o, p[:, 0], p[:, 1], alpha, tm_c)


def _rms(x, g):
    return x * lax.rsqrt(jnp.mean(x * x, axis=-1, keepdims=True) + RMS_EPS) * g

ODD_OFF = dict(qc=0, kc=512, vc=640, qi=768, ki=1024, wi=1152, cq=1280, ckv=1536, kra=1664, krb=1792, end=1920)


def _odd_proj_kernel(x_ref, w_ref, qng_ref, kvng_ref, wqa_ref, wqb_ref, wk_ref, wv_ref, cos_ref, sin_ref,
                     tq_ref, tk_ref,
                     qc_ref, kc_ref, vc_ref, qi_ref, ki_ref, wi_ref, q8_ref, k8_ref, v8_ref, *, q_scale):
    o = ODD_OFF
    tm = x_ref.shape[0]
    lo = _lane_lo((tm, LANE))
    h = _dot(x_ref[...].astype(BF16), w_ref[...])
    kdup = h[:, o["kc"]:o["vc"]]
    vdup = h[:, o["vc"]:o["qi"]]
    kc_ref[:, 0:LANE] = jnp.where(lo, kdup, tk_ref[:, 0:LANE].astype(F32)).astype(BF16)
    kc_ref[:, LANE:2 * LANE] = jnp.where(lo, tk_ref[:, LANE:2 * LANE].astype(F32), kdup).astype(BF16)
    vc_ref[:, 0:LANE] = jnp.where(lo, vdup, 1.0).astype(BF16)
    vc_ref[:, LANE:2 * LANE] = jnp.where(lo, 1.0, vdup).astype(BF16)
    qi_ref[...] = h[:, o["qi"]:o["ki"]].astype(BF16)
    ki_ref[...] = h[:, o["ki"]:o["wi"]].astype(BF16)
    wi_ref[...] = h[:, o["wi"]:o["wi"] + IDX_HEADS]
    cos = cos_ref[...]
    sin = sin_ref[...]
    cqn = _rms(h[:, o["cq"]:o["ckv"]], qng_ref[...]).astype(BF16)
    qa = _dot(cqn, wqa_ref[...])
    qb = _dot(cqn, wqb_ref[...])
    ckvn = _rms(h[:, o["ckv"]:o["kra"]], kvng_ref[...]).astype(BF16)
    kn = _dot(ckvn, wk_ref[...])
    v8 = _dot(ckvn, wv_ref[...])
    kr = h[:, o["kra"]:o["krb"]] * cos + h[:, o["krb"]:o["end"]] * sin
    for hd in range(D_HEADS):
        sl = slice(hd * LANE, (hd + 1) * LANE)
        pj = slice((hd // 2) * LANE, (hd // 2 + 1) * LANE)
        own = lo if hd % 2 == 0 else ~lo
        qc_ref[:, sl] = jnp.where(own, h[:, pj], tq_ref[:, sl].astype(F32)).astype(BF16)
        q8_ref[:, sl] = ((qa[:, sl] * cos + qb[:, sl] * sin) * q_scale).astype(BF16)
        k8_ref[:, sl] = (kn[:, sl] + kr).astype(BF16)
        v8_ref[:, sl] = jnp.where(own, v8[:, pj], 1.0).astype(BF16)


def _odd_proj(xt, w_cat, qng, kvng, wqa, wqb, wk, wv, cos_t, sin_t, tq_t, tk_t, seq, tm):
    n, d = xt.shape
    full = lambda i: (0, 0)
    tps = seq // tm
    pos = lambda i: (i % tps, 0)
    widths = (1024, 256, 256, 256, 128)
    out_shape = [jax.ShapeDtypeStruct((n, w), BF16) for w in widths]
    out_specs = [pl.BlockSpec((tm, w), lambda i: (i, 0)) for w in widths]
    out_shape.append(jax.ShapeDtypeStruct((n, IDX_HEADS), F32))
    out_specs.append(pl.BlockSpec((tm, IDX_HEADS), lambda i: (i, 0)))
    for w in (1024, 1024, 1024):
        out_shape.append(jax.ShapeDtypeStruct((n, w), BF16))
        out_specs.append(pl.BlockSpec((tm, w), lambda i: (i, 0)))
    return pl.pallas_call(
        functools.partial(_odd_proj_kernel, q_scale=(MLA_NOPE + MLA_ROPE) ** -0.5 * LOG2E),
        out_shape=out_shape,
        grid=(n // tm,),
        in_specs=[pl.BlockSpec((tm, d), lambda i: (i, 0)),
                  pl.BlockSpec(w_cat.shape, full),
                  pl.BlockSpec(qng.shape, full), pl.BlockSpec(kvng.shape, full),
                  pl.BlockSpec(wqa.shape, full), pl.BlockSpec(wqb.shape, full),
                  pl.BlockSpec(wk.shape, full), pl.BlockSpec(wv.shape, full),
                  pl.BlockSpec((tm, LANE), pos), pl.BlockSpec((tm, LANE), pos),
                  pl.BlockSpec((tm, C_HEADS * LANE), pos), pl.BlockSpec((tm, 2 * LANE), pos)],
        out_specs=out_specs,
        compiler_params=_cparams(("parallel",)),
        name="odd_proj",
    )(xt, w_cat, qng, kvng, wqa, wqb, wk, wv, cos_t, sin_t, tq_t, tk_t)


def _dsa_kernel(q_ref, k_ref, v_ref, qi_ref, ki_ref, wt_ref, o_ref, sc_ref, mb_ref, *, top_k, chunk):
    i = pl.program_id(1)
    nc = (i * BLOCK + BLOCK + chunk - 1) // chunk
    lo = _lane_lo((BLOCK, LANE))
    s_row2 = lax.broadcasted_iota(I32, (2 * chunk, BLOCK), 0)
    t_lane2 = i * BLOCK + lax.broadcasted_iota(I32, (2 * chunk, BLOCK), 1)

    def chunk_start(c):
        return pl.multiple_of(c * chunk, chunk)

    qidx = []
    for h in range(IDX_HEADS):
        qp = qi_ref[:, (h // 2) * LANE:(h // 2 + 1) * LANE]
        qidx.append(jnp.where(lo if h % 2 == 0 else ~lo, qp, jnp.zeros_like(qp)))
    q_stack = jnp.concatenate(qidx, axis=0)
    w = wt_ref[...]

    npair = (nc + 1) // 2

    def pair_start(cp):
        return pl.multiple_of(cp * (2 * chunk), 2 * chunk)

    def score_pair(cp, _):
        start = pair_start(cp)
        d = _dot_nt(ki_ref[pl.ds(start, 2 * chunk), :], q_stack)
        sc = w[0:1, :] * jnp.maximum(d[:, 0:BLOCK], 0.0)
        for h in range(1, IDX_HEADS):
            sc = sc + w[h:h + 1, :] * jnp.maximum(d[:, h * BLOCK:(h + 1) * BLOCK], 0.0)
        sc = jnp.where(sc == 0.0, 0.0, sc)
        sc_ref[pl.ds(start, 2 * chunk), :] = jnp.where(start + s_row2 <= t_lane2, sc, NEG_INF)
        return 0

    lax.fori_loop(0, npair, score_pair, 0)

    nacc = 4 * 8

    def count(trial, strict):
        def body(cp, acc):
            sc = sc_ref[pl.ds(pair_start(cp), 2 * chunk), :]
            hit = (sc > trial) if strict else (sc >= trial)
            return acc + jnp.sum(jnp.where(hit, 1.0, 0.0).reshape(2 * chunk // nacc, nacc, BLOCK), axis=0)

        acc = lax.fori_loop(0, npair, body, jnp.zeros((nacc, BLOCK), F32))
        return jnp.sum(acc, axis=0, keepdims=True)

    def decode(key):
        return lax.bitcast_convert_type(jnp.where(key < 0, key ^ jnp.int32(0x7FFFFFFF), key), F32)

    kf = float(top_k)
    key_neg_inf = jnp.int32(-2139095041)
    cand = jnp.where(count(jnp.zeros((1, BLOCK), F32), False) >= kf, 0, jnp.int32(-2 ** 31)).astype(I32)

    def bit_step(j, cand):
        trial = cand + lax.shift_left(jnp.int32(1), 30 - j)
        ok = (trial <= key_neg_inf) | (count(decode(trial), False) >= kf)
        return jnp.where(ok, trial, cand)

    thr = decode(lax.fori_loop(0, 31, bit_step, cand))
    need = kf - count(thr, True)

    r0 = lax.broadcasted_iota(I32, (chunk, chunk), 0)
    r1 = lax.broadcasted_iota(I32, (chunk, chunk), 1)
    below = jnp.where(r1 < r0, 1.0, 0.0).astype(BF16)
    need = jnp.where(thr == NEG_INF, 0.0, need)

    def mask_pair(cp, offset):
        start = pair_start(cp)
        sc = sc_ref[pl.ds(start, 2 * chunk), :]
        eq = sc == thr
        eqf = jnp.where(eq, 1.0, 0.0)
        eqb = eqf.astype(BF16)
        first = jnp.sum(eqf[0:chunk], axis=0, keepdims=True)
        prefix = jnp.concatenate([_dot(below, eqb[0:chunk]) + offset,
                                  _dot(below, eqb[chunk:2 * chunk]) + (offset + first)], axis=0)
        bias = jnp.where(sc > thr, 0.0, jnp.where(eq, jnp.where(prefix < need, 0.0, NEG_INF), NEG_INF))
        mb_ref[:, pl.ds(start, 2 * chunk)] = bias.T
        return offset + first + jnp.sum(eqf[chunk:2 * chunk], axis=0, keepdims=True)

    lax.fori_loop(0, npair, mask_pair, jnp.zeros((1, BLOCK), F32))

    def att_chunk(c, carry):
        start = chunk_start(c)
        qs = [q_ref[:, h * LANE:(h + 1) * LANE] for h in range(C_HEADS)]
        ks = [k_ref[pl.ds(start, chunk), (h % 2) * LANE:(h % 2 + 1) * LANE] for h in range(C_HEADS)]
        vs = [v_ref[pl.ds(start, chunk), (h % 2) * LANE:(h % 2 + 1) * LANE] for h in range(C_HEADS)]
        return _online_softmax_chains(qs, ks, vs, carry, bias=mb_ref[:, pl.ds(start, chunk)], guard_empty=True)

    init = tuple((jnp.full((BLOCK, 1), NEG_INF, F32), jnp.zeros((BLOCK, LANE), F32)) for _ in range(C_HEADS))
    carry = lax.fori_loop(0, nc, att_chunk, init)
    _normalize_pairs([c[1] for c in carry], o_ref)


def _dsa(qc, kc, vc, qi, ki, wit):
    b, s, _ = qc.shape
    top_k = min(DSA_TOPK_MAX, s // 4)
    chunk = 256
    assert s % (2 * chunk) == 0 and top_k <= chunk
    blk = lambda bi, i: (bi, i, 0)
    seq = lambda bi, i: (bi, 0, 0)
    return pl.pallas_call(
        functools.partial(_dsa_kernel, top_k=top_k, chunk=chunk),
        out_shape=jax.ShapeDtypeStruct((b, s, C_HEADS * HEAD_DIM), BF16),
        grid=(b, s // BLOCK),
        in_specs=[pl.BlockSpec((None, BLOCK, C_HEADS * LANE), blk),
                  pl.BlockSpec((None, s, 2 * LANE), seq),
                  pl.BlockSpec((None, s, 2 * LANE), seq),
                  pl.BlockSpec((None, BLOCK, 256), blk),
                  pl.BlockSpec((None, s, LANE), seq),
                  pl.BlockSpec((None, IDX_HEADS, BLOCK), lambda bi, i: (bi, 0, i))],
        out_specs=pl.BlockSpec((None, BLOCK, 512), blk),
        scratch_shapes=[pltpu.VMEM((s, BLOCK), F32), pltpu.VMEM((BLOCK, s), F32)],
        compiler_params=_cparams(("parallel", "arbitrary")),
        name="dsa_attn",
    )(qc, kc, vc, qi, ki, wit)


def _dup64(w):
    d, c = w.shape
    w = w.reshape(d, c // HEAD_DIM, 1, HEAD_DIM)
    return jnp.broadcast_to(w, (d, c // HEAD_DIM, 2, HEAD_DIM)).reshape(d, 2 * c)


def _pad_cols(w, width):
    return jnp.pad(w, ((0, 0), (0, width - w.shape[1])))


def _split_cols(w, sizes):
    out, start = [], 0
    for n in sizes:
        out.append(w[:, start:start + n])
        start += n
    return out


def _rope_partner(w):
    half = MLA_ROPE // 2
    return jnp.concatenate([-w[..., half:], w[..., :half]], axis=-1)


def _rope_block(w_rope):
    d = w_rope.shape[0]
    return jnp.concatenate([jnp.zeros((d, MLA_NOPE), F32), w_rope,
                            jnp.zeros((d, LANE - MLA_NOPE - MLA_ROPE), F32)], axis=1)


def _alibi_tables(s):
    slopes = 2.0 ** (-8.0 * jnp.arange(1, C_HEADS + 1, dtype=F32) / C_HEADS)
    a = slopes * LOG2E
    pos = jnp.arange(s, dtype=F32)
    a_sp = _split3(a)
    r_sp = _split3(-(a[None, :] * pos[:, None]))
    ones = jnp.ones((s, C_HEADS), F32)
    q_terms = [128.0 * t * ones for t in a_sp] + [t * ones for t in a_sp] + list(r_sp)
    q_aug = jnp.stack(q_terms, axis=-1)
    q_aug = jnp.pad(q_aug, ((0, 0), (0, 0), (0, HEAD_DIM - q_aug.shape[-1])))
    zeros = jnp.zeros_like(q_aug)
    even = jnp.concatenate([zeros, q_aug], axis=-1)
    odd = jnp.concatenate([q_aug, zeros], axis=-1)
    parity = (jnp.arange(C_HEADS) % 2 == 0)[None, :, None]
    tq_t = jnp.where(parity, even, odd).reshape(s, C_HEADS * LANE)
    s_hi = jnp.floor(pos / LANE)
    s_lo = pos - LANE * s_hi
    k_aug = jnp.stack([s_hi] * 3 + [s_lo] * 3 + [jnp.ones_like(pos)] * 3, axis=-1)
    k_aug = jnp.pad(k_aug, ((0, 0), (0, HEAD_DIM - k_aug.shape[-1])))
    kz = jnp.zeros_like(k_aug)
    tk_t = jnp.concatenate([kz, k_aug, k_aug, kz], axis=-1)
    return tq_t.astype(BF16), tk_t.astype(BF16)


def kernel(x, even_w_in, even_b_f, even_sinks, odd_w_in, odd_q_norm, odd_kv_norm, odd_w_uq, odd_w_ukv,
           w_o, ln_g, ln_b, router_w, router_b, moe_w_gate, moe_w_up, moe_w_down):
    b, s, d = x.shape
    n = b * s
    depth = w_o.shape[0]
    alpha = (2 * depth) ** 0.25
    tm = min(512, s)
    tq = min(256, s)
    tm_e = min(256, n)
    tm_c = min(256, n)
    att_scale = HEAD_DIM ** -0.5
    xt = x.reshape(n, d)
    rw_hi = _top16(router_w.T)
    rwt = jnp.concatenate([rw_hi, _top16(router_w.T - rw_hi)], axis=0).astype(BF16)
    rb = router_b.reshape(N_EXPERTS, 1)

    for l in range(depth):
        j = l // 2
        if l % 2 == 0:
            qa, ka, va, qb, kb, vb, fg = _split_cols(
                even_w_in[j], (512, 128, 128, 512, 512, 512, B_HEADS))
            w_cat = jnp.concatenate([qa * att_scale, _dup64(ka), _dup64(va), qb * (att_scale * LOG2E), kb, vb,
                                     _pad_cols(fg, LANE)], axis=1).astype(BF16)
            bf_pad = _pad_cols(even_b_f[j].reshape(1, B_HEADS), LANE)
            qa_, ka_, va_, qf, kf, vf = _even_proj(xt, w_cat, bf_pad, s, tm)
            r3 = lambda a: a.reshape(b, s, a.shape[1])
            out_a = _swa(even_sinks[j], r3(qa_), r3(ka_), r3(va_))
            out_b = _flash(r3(qf), r3(kf), r3(vf), tq=tq, nh=4, name="fox_attn")
        else:
            (qc, kc, vc, qi, ki, wi, cq, ckv, kr) = _split_cols(
                odd_w_in[j], (512, 64, 64, IDX_HEADS * IDX_DIM, IDX_DIM, IDX_HEADS,
                              MLA_Q_RANK, MLA_KV_RANK, MLA_ROPE))
            w_cat = jnp.concatenate([
                qc * (att_scale * LOG2E), _dup64(kc), _dup64(vc), qi * (IDX_DIM ** -0.5), _dup64(ki),
                _pad_cols(wi * (IDX_HEADS ** -0.5), LANE), cq, ckv,
                _rope_block(kr), _rope_block(_rope_partner(kr))], axis=1).astype(BF16)
            dq = MLA_NOPE + MLA_ROPE
            wuq = odd_w_uq[j].reshape(MLA_Q_RANK, D_HEADS, dq)
            pad_q = jnp.zeros((MLA_Q_RANK, D_HEADS, LANE - dq), F32)
            wqa = jnp.concatenate([wuq, pad_q], axis=2).reshape(MLA_Q_RANK, D_HEADS * LANE).astype(BF16)
            wqb = jnp.concatenate([jnp.zeros((MLA_Q_RANK, D_HEADS, MLA_NOPE), F32),
                                   _rope_partner(wuq[..., MLA_NOPE:]), pad_q],
                                  axis=2).reshape(MLA_Q_RANK, D_HEADS * LANE).astype(BF16)
            wukv = odd_w_ukv[j].reshape(MLA_KV_RANK, D_HEADS, MLA_NOPE + MLA_V)
            wk = jnp.concatenate([wukv[..., :MLA_NOPE], jnp.zeros((MLA_KV_RANK, D_HEADS, LANE - MLA_NOPE), F32)],
                                 axis=2).reshape(MLA_KV_RANK, D_HEADS * LANE).astype(BF16)
            wv = wukv[..., MLA_NOPE:].reshape(MLA_KV_RANK, D_HEADS * MLA_V).astype(BF16)
            half = MLA_ROPE // 2
            inv = ROPE_THETA ** (-jnp.arange(half, dtype=F32) / half)
            ang = jnp.arange(s, dtype=F32)[:, None] * inv[None, :]
            cos_h, sin_h = jnp.cos(ang), jnp.sin(ang)
            pad_t = jnp.zeros((s, LANE - dq), F32)
            cos_t = jnp.concatenate([jnp.ones((s, MLA_NOPE), F32), cos_h, cos_h, pad_t], axis=1)
            sin_t = jnp.concatenate([jnp.zeros((s, MLA_NOPE), F32), sin_h, sin_h, pad_t], axis=1)
            tq_t, tk_t = _alibi_tables(s)
            qc_, kc_, vc_, qi_, ki_, wi_, q8, k8, v8 = _odd_proj(
                xt, w_cat, odd_q_norm[j].reshape(1, -1), odd_kv_norm[j].reshape(1, -1),
                wqa, wqb, wk, wv, cos_t, sin_t, tq_t, tk_t, s, tm)
            r3 = lambda a: a.reshape(b, s, a.shape[1])
            out_a = _dsa(r3(qc_), r3(kc_), r3(vc_), r3(qi_), r3(ki_), r3(wi_).transpose(0, 2, 1))
            out_b = _flash(r3(q8), r3(k8), r3(v8), tq=tq, nh=4, name="mla_attn")
        half_w = out_a.shape[2]
        wo = w_o[l].astype(BF16)
        x1, route_t = _outproj(out_a.reshape(n, half_w), out_b.reshape(n, -1), xt,
                               wo[:half_w], wo[half_w:], ln_g[l, 0].reshape(1, d), ln_b[l, 0].reshape(1, d),
                               rwt, rb, alpha, tm)
        xt = _moe_layer(x1, route_t, moe_w_gate[l].astype(BF16), moe_w_up[l].astype(BF16),
                        moe_w_down[l].astype(BF16), ln_g[l, 1].reshape(1, d), ln_b[l, 1].reshape(1, d),
                        alpha, tm_e, tm_c)
    return xt.reshape(b, s, d)
```

```python
import functools
import math

import jax
import jax.numpy as jnp
from jax import lax
from jax.experimental import pallas as pl
from jax.experimental.pallas import tpu as pltpu

F32 = jnp.float32
BF16 = jnp.bfloat16
I32 = jnp.int32

LANE = 128
HEAD_DIM = 64
BLOCK = 128
A_HEADS, A_KV_HEADS, WINDOW = 8, 2, 128
B_HEADS = 8
C_HEADS, IDX_HEADS, IDX_DIM, DSA_TOPK_MAX = 8, 4, 64, 256
D_HEADS, MLA_Q_RANK, MLA_KV_RANK, MLA_NOPE, MLA_ROPE, MLA_V = 8, 256, 128, 64, 32, 64
ROPE_THETA = 10000.0
N_EXPERTS, N_GROUPS = 16, 4
EXPERTS_PER_GROUP = N_EXPERTS // N_GROUPS
LN_EPS, RMS_EPS = 1e-5, 1e-6
NEG_INF = float("-inf")
LOG2E = math.log2(math.e)
VMEM_LIMIT = 56 * 1024 * 1024


def _cparams(sem):
    return pltpu.CompilerParams(dimension_semantics=sem, vmem_limit_bytes=VMEM_LIMIT)


def _dot(a, b):
    return jnp.dot(a, b, preferred_element_type=F32)


def _dot_nt(a, b):
    return lax.dot_general(a, b, (((1,), (1,)), ((), ())), preferred_element_type=F32)


def _lane_lo(shape):
    return lax.broadcasted_iota(I32, shape, len(shape) - 1) % LANE < HEAD_DIM


def _layer_norm(y, g, b):
    mu = jnp.mean(y, axis=-1, keepdims=True)
    yc = y - mu
    var = jnp.mean(yc * yc, axis=-1, keepdims=True)
    return yc * lax.rsqrt(var + LN_EPS) * g + b


def _top16(v):
    bits = lax.bitcast_convert_type(v, I32) & jnp.int32(-65536)
    return lax.bitcast_convert_type(bits, F32)


def _split3(x):
    s1 = _top16(x)
    r = x - s1
    s2 = _top16(r)
    return s1, s2, r - s2


def _aug_rel(shape, h):
    lane = lax.broadcasted_iota(I32, shape, 1)
    return lane - (HEAD_DIM if h % 2 == 0 else 0)


def _even_proj_kernel(x_ref, w_ref, bf_ref, qa_ref, ka_ref, va_ref, qf_ref, kf_ref, vf_ref,
                      carry_ref, *, tiles_per_seq):
    t = pl.program_id(0)
    tm = x_ref.shape[0]
    h = _dot(x_ref[...].astype(BF16), w_ref[...])
    qa_ref[...] = h[:, 0:512].astype(BF16)
    ka_ref[...] = h[:, 512:768].astype(BF16)
    va_ref[...] = h[:, 768:1024].astype(BF16)
    z = h[:, 2560:2688] + bf_ref[...]
    logf = jnp.minimum(z, 0.0) - jnp.log1p(jnp.exp(-jnp.abs(z)))

    @pl.when(t % tiles_per_seq == 0)
    def _():
        carry_ref[...] = jnp.zeros_like(carry_ref)

    row = lax.broadcasted_iota(I32, (tm, tm), 0)
    col = lax.broadcasted_iota(I32, (tm, tm), 1)
    tri = jnp.where(row >= col, 1.0, 0.0).astype(BF16)
    c = carry_ref[...]
    for part in _split3(logf):
        c = c + _dot(tri, part.astype(BF16))
    carry_ref[...] = c[tm - 1:tm, :]

    s1, s2, s3 = _split3(c * LOG2E)
    lo = _lane_lo((tm, LANE))
    for hd in range(B_HEADS):
        own = lo if hd % 2 == 0 else ~lo
        rel = _aug_rel((tm, LANE), hd)
        b1, b2, b3 = s1[:, hd:hd + 1], s2[:, hd:hd + 1], s3[:, hd:hd + 1]
        q_aug = jnp.where(rel == 3, b1, jnp.where(rel == 4, b2, jnp.where(rel == 5, b3,
                          jnp.where((rel >= 0) & (rel < 3), -1.0, 0.0))))
        k_aug = jnp.where(rel == 0, b1, jnp.where(rel == 1, b2, jnp.where(rel == 2, b3,
                          jnp.where((rel >= 3) & (rel < 6), 1.0, 0.0))))
        pj = (hd // 2) * LANE
        sl = slice(hd * LANE, (hd + 1) * LANE)
        qf_ref[:, sl] = jnp.where(own, h[:, 1024 + pj:1024 + pj + LANE], q_aug).astype(BF16)
        kf_ref[:, sl] = jnp.where(own, h[:, 1536 + pj:1536 + pj + LANE], k_aug).astype(BF16)
        vf_ref[:, sl] = jnp.where(own, h[:, 2048 + pj:2048 + pj + LANE], 1.0).astype(BF16)


def _even_proj(xt, w_cat, bf_pad, seq, tm):
    n, d = xt.shape
    wcols = w_cat.shape[1]
    widths = (512, 256, 256, 1024, 1024, 1024)
    out_shape = [jax.ShapeDtypeStruct((n, w), BF16) for w in widths]
    out_specs = [pl.BlockSpec((tm, w), lambda i: (i, 0)) for w in widths]
    return pl.pallas_call(
        functools.partial(_even_proj_kernel, tiles_per_seq=seq // tm),
        out_shape=out_shape,
        grid=(n // tm,),
        in_specs=[pl.BlockSpec((tm, d), lambda i: (i, 0)),
                  pl.BlockSpec((d, wcols), lambda i: (0, 0)),
                  pl.BlockSpec((1, LANE), lambda i: (0, 0))],
        out_specs=out_specs,
        scratch_shapes=[pltpu.VMEM((1, LANE), F32)],
        compiler_params=_cparams(("arbitrary",)),
        name="even_proj",
    )(xt, w_cat, bf_pad)


def _swa_kernel(sink_ref, q_ref, kp_ref, kc_ref, vp_ref, vc_ref, o_ref):
    i = pl.program_id(1)
    group = A_HEADS // A_KV_HEADS
    tl = lax.broadcasted_iota(I32, (BLOCK, 2 * BLOCK), 0)
    sl = lax.broadcasted_iota(I32, (BLOCK, 2 * BLOCK), 1)
    dist = BLOCK + tl - sl
    valid = (dist >= 0) & (dist < WINDOW) & ((sl >= BLOCK) | (i > 0))
    distf = dist.astype(F32)
    lo = _lane_lo((BLOCK, LANE))

    def band(prev_ref, cur_ref, g):
        return jnp.concatenate([prev_ref[:, g * LANE:(g + 1) * LANE], cur_ref[:, g * LANE:(g + 1) * LANE]], axis=0)

    stacked = []
    for g in range(A_KV_HEADS):
        qs = []
        for h in range(g * group, (g + 1) * group):
            qp = q_ref[:, (h // 2) * LANE:(h // 2 + 1) * LANE]
            qs.append(jnp.where(lo if h % 2 == 0 else ~lo, qp, jnp.zeros_like(qp)))
        stacked.append(_dot_nt(jnp.concatenate(qs, axis=0), band(kp_ref, kc_ref, g)))
    ps, denoms = [], []
    for h in range(A_HEADS):
        r = h % group
        slope = 2.0 ** (-8.0 * (h + 1) / A_HEADS) * LOG2E
        logits = jnp.where(valid, stacked[h // group][r * BLOCK:(r + 1) * BLOCK] - slope * distf, NEG_INF)
        sink = sink_ref[h] * LOG2E
        m = jnp.maximum(jnp.max(logits, axis=-1, keepdims=True), sink)
        e = jnp.exp2(logits - m)
        denoms.append(jnp.sum(e, axis=-1, keepdims=True) + jnp.exp2(sink - m))
        ps.append(e.astype(BF16))
    outs = []
    for g in range(A_KV_HEADS):
        pv = _dot(jnp.concatenate(ps[g * group:(g + 1) * group], axis=0), band(vp_ref, vc_ref, g))
        for r in range(group):
            outs.append(pv[r * BLOCK:(r + 1) * BLOCK] / denoms[g * group + r])
    for j in range(A_HEADS // 2):
        o_ref[:, j * LANE:(j + 1) * LANE] = jnp.where(lo, outs[2 * j], outs[2 * j + 1]).astype(BF16)


def _swa(sinks, qa, ka, va):
    b, s, _ = qa.shape
    nb = s // BLOCK
    prev = lambda bi, i: (bi, jnp.maximum(i - 1, 0), 0)
    cur = lambda bi, i: (bi, i, 0)
    return pl.pallas_call(
        _swa_kernel,
        out_shape=jax.ShapeDtypeStruct((b, s, A_HEADS * HEAD_DIM), BF16),
        grid=(b, nb),
        in_specs=[pl.BlockSpec(memory_space=pltpu.SMEM),
                  pl.BlockSpec((None, BLOCK, 512), cur),
                  pl.BlockSpec((None, BLOCK, 256), prev),
                  pl.BlockSpec((None, BLOCK, 256), cur),
                  pl.BlockSpec((None, BLOCK, 256), prev),
                  pl.BlockSpec((None, BLOCK, 256), cur)],
        out_specs=pl.BlockSpec((None, BLOCK, 512), cur),
        compiler_params=_cparams(("parallel", "arbitrary")),
        name="swa_attn",
    )(sinks, qa, ka, ka, va, va)


def _normalize_pairs(accs, o_ref):
    lo = _lane_lo(accs[0].shape)
    outs = [a / pltpu.roll(a, HEAD_DIM, 1) for a in accs]
    for j in range(len(accs) // 2):
        o_ref[:, j * LANE:(j + 1) * LANE] = jnp.where(lo, outs[2 * j], outs[2 * j + 1]).astype(BF16)


def _online_softmax_chains(qs, ks, vs, carry, bias=None, masks=None, guard_empty=False):
    n = len(qs)
    ss = [_dot_nt(qs[i], ks[i]) for i in range(n)]
    if bias is not None:
        ss = [s + bias for s in ss]
    if masks is not None:
        ss = [jnp.where(masks[i], ss[i], NEG_INF) for i in range(n)]
    ms, ps, alphas = [], [], []
    for i in range(n):
        m = carry[i][0]
        m_new = jnp.maximum(m, jnp.max(ss[i], axis=-1, keepdims=True))
        m_ref = jnp.where(m_new == NEG_INF, 0.0, m_new) if guard_empty else m_new
        ps.append(jnp.exp2(ss[i] - m_ref).astype(BF16))
        alphas.append(jnp.exp2(m - m_ref))
        ms.append(m_new)
    pvs = [_dot(ps[i], vs[i]) for i in range(n)]
    return tuple((ms[i], alphas[i] * carry[i][1] + pvs[i]) for i in range(n))


def _flash_kernel(q_ref, k_ref, v_ref, o_ref, *, tq, nh):
    qi = pl.program_id(2)
    nslab = tq // BLOCK
    chains = [(hh, r) for hh in range(nh) for r in range(nslab)]
    row = lax.broadcasted_iota(I32, (BLOCK, tq), 0)
    col = lax.broadcasted_iota(I32, (BLOCK, tq), 1)

    def lanes(hh):
        return slice(hh * LANE, (hh + 1) * LANE)

    def step(kb, carry, diagonal):
        start = pl.multiple_of(kb * tq, tq)
        qs = [q_ref[r * BLOCK:(r + 1) * BLOCK, lanes(hh)] for hh, r in chains]
        ks = [k_ref[pl.ds(start, tq), lanes(hh)] for hh, r in chains]
        vs = [v_ref[pl.ds(start, tq), lanes(hh)] for hh, r in chains]
        masks = [col <= row + r * BLOCK for hh, r in chains] if diagonal else None
        return _online_softmax_chains(qs, ks, vs, carry, masks=masks)

    init = tuple((jnp.full((BLOCK, 1), NEG_INF, F32), jnp.zeros((BLOCK, LANE), F32))
                 for _ in range(nh * nslab))
    carry = lax.fori_loop(0, qi, lambda kb, c: step(kb, c, False), init)
    carry = step(qi, carry, True)
    accs = [jnp.concatenate([carry[hh * nslab + r][1] for r in range(nslab)], axis=0) for hh in range(nh)]
    _normalize_pairs(accs, o_ref)


def _flash(q, k, v, *, tq, nh, name):
    b, s, w = v.shape
    return pl.pallas_call(
        functools.partial(_flash_kernel, tq=tq, nh=nh),
        out_shape=jax.ShapeDtypeStruct((b, s, w // 2), BF16),
        grid=(b, w // (nh * LANE), s // tq),
        in_specs=[pl.BlockSpec((None, tq, nh * LANE), lambda bi, j, i: (bi, i, j)),
                  pl.BlockSpec((None, s, nh * LANE), lambda bi, j, i: (bi, 0, j)),
                  pl.BlockSpec((None, s, nh * LANE), lambda bi, j, i: (bi, 0, j))],
        out_specs=pl.BlockSpec((None, tq, nh * LANE // 2), lambda bi, j, i: (bi, i, j)),
        compiler_params=_cparams(("parallel", "parallel", "arbitrary")),
        name=name,
    )(q, k, v)


def _route_rows(aff, sel):
    gs = []
    for g in range(N_GROUPS):
        r = sel[g * EXPERTS_PER_GROUP:(g + 1) * EXPERTS_PER_GROUP]
        best = None
        for a in range(EXPERTS_PER_GROUP):
            for c in range(a + 1, EXPERTS_PER_GROUP):
                pair = r[a] + r[c]
                best = pair if best is None else jnp.maximum(best, pair)
        gs.append(best)
    gmax = functools.reduce(jnp.maximum, gs)
    gidx = jnp.full_like(gmax, float(N_GROUPS - 1))
    for g in range(N_GROUPS - 2, -1, -1):
        gidx = jnp.where(gs[g] == gmax, float(g), gidx)

    def pick(rows, k):
        out = rows[(N_GROUPS - 1) * EXPERTS_PER_GROUP + k]
        for g in range(N_GROUPS - 2, -1, -1):
            out = jnp.where(gidx == float(g), rows[g * EXPERTS_PER_GROUP + k], out)
        return out

    s = [pick(sel, k) for k in range(EXPERTS_PER_GROUP)]
    a = [pick(aff, k) for k in range(EXPERTS_PER_GROUP)]

    def first_max(vals):
        mx = functools.reduce(jnp.maximum, vals)
        idx = jnp.full_like(mx, float(len(vals) - 1))
        for k in range(len(vals) - 2, -1, -1):
            idx = jnp.where(vals[k] == mx, float(k), idx)
        return idx

    def take(vals, idx):
        out = vals[-1]
        for k in range(len(vals) - 2, -1, -1):
            out = jnp.where(idx == float(k), vals[k], out)
        return out

    i1 = first_max(s)
    s2 = [jnp.where(i1 == float(k), NEG_INF, s[k]) for k in range(EXPERTS_PER_GROUP)]
    i2 = first_max(s2)
    a1, a2 = take(a, i1), take(a, i2)
    tot = a1 + a2
    base = gidx * float(EXPERTS_PER_GROUP)
    return base + i1, base + i2, a1 / tot, a2 / tot


def _outproj_kernel(a_ref, b_ref, x_ref, wa_ref, wb_ref, g_ref, be_ref, rwt_ref, rb_ref,
                    x1_ref, route_ref, *, alpha):
    mix = _dot(a_ref[...], wa_ref[...]) + _dot(b_ref[...], wb_ref[...])
    x1 = _layer_norm(alpha * x_ref[...] + mix, g_ref[...], be_ref[...])
    x1_ref[...] = x1
    x_hi = _top16(x1)
    x_lo = (x1 - x_hi).astype(BF16)
    z_hi = _dot_nt(rwt_ref[...], x_hi.astype(BF16))
    zt = z_hi[0:N_EXPERTS] + z_hi[N_EXPERTS:2 * N_EXPERTS] + _dot_nt(rwt_ref[0:N_EXPERTS, :], x_lo)
    afft = 1.0 / (1.0 + jnp.exp(-zt))
    selt = afft + rb_ref[...]
    aff = [afft[e:e + 1, :] for e in range(N_EXPERTS)]
    sel = [selt[e:e + 1, :] for e in range(N_EXPERTS)]
    e1, e2, w1, w2 = _route_rows(aff, sel)
    zero = jnp.zeros_like(e1)
    route_ref[...] = jnp.concatenate([e1, e2, w1, w2, zero, zero, zero, zero], axis=0)


def _outproj(a, b, xt, wa, wb, g, be, rwt, rb, alpha, tm):
    n, d = xt.shape
    full = lambda i: (0, 0)
    return pl.pallas_call(
        functools.partial(_outproj_kernel, alpha=alpha),
        out_shape=[jax.ShapeDtypeStruct((n, d), F32), jax.ShapeDtypeStruct((8, n), F32)],
        grid=(n // tm,),
        in_specs=[pl.BlockSpec((tm, a.shape[1]), lambda i: (i, 0)),
                  pl.BlockSpec((tm, b.shape[1]), lambda i: (i, 0)),
                  pl.BlockSpec((tm, d), lambda i: (i, 0)),
                  pl.BlockSpec(wa.shape, full), pl.BlockSpec(wb.shape, full),
                  pl.BlockSpec((1, d), full), pl.BlockSpec((1, d), full),
                  pl.BlockSpec(rwt.shape, full), pl.BlockSpec(rb.shape, full)],
        out_specs=[pl.BlockSpec((tm, d), lambda i: (i, 0)), pl.BlockSpec((8, tm), lambda i: (0, i))],
        compiler_params=_cparams(("parallel",)),
        name="outproj_ln_route",
    )(a, b, xt, wa, wb, g, be, rwt, rb)


def _moe_kernel(bexp_ref, bbase_ref, nact_ref, src_ref,
                x_hbm, wg_ref, wu_ref, wd_ref, o_ref, xbuf0, xbuf1, sem, *, tm):
    i = pl.program_id(0)
    nact = nact_ref[0]
    last = src_ref.shape[0] - 1
    bufs = (xbuf0, xbuf1)

    def issue(blk, slot):
        base = bbase_ref[blk]
        for r in range(tm):
            tok = src_ref[jnp.minimum(base + r, last)]
            pltpu.make_async_copy(x_hbm.at[pl.ds(tok, 1), :], bufs[slot].at[pl.ds(r, 1), :],
                                  sem.at[slot]).start(priority=r % 2)

    def wait(slot):
        pltpu.make_async_copy(x_hbm.at[pl.ds(0, tm), :], bufs[slot], sem.at[slot]).wait()

    @pl.when((i == 0) & (nact > 0))
    def _():
        issue(0, 0)

    def step(slot):
        wait(slot)

        @pl.when(i + 1 < nact)
        def _():
            issue(i + 1, 1 - slot)

        xb = bufs[slot][...].astype(BF16)
        gate = _dot(xb, wg_ref[...])
        up = _dot(xb, wu_ref[...])
        h = gate * (1.0 / (1.0 + jnp.exp(-gate))) * up
        o_ref[...] = _dot(h.astype(BF16), wd_ref[...])

    for slot in range(2):
        pl.when((i < nact) & (i % 2 == slot))(functools.partial(step, slot))

    @pl.when(i >= nact)
    def _():
        o_ref[...] = jnp.zeros_like(o_ref)


def _moe_experts(x1, wg, wu, wd, bexp, bbase, nact, src, tm):
    n, d = x1.shape
    nblk = bexp.shape[0]
    f = wg.shape[2]
    wmap = lambda i, bexp, *_: (bexp[i], 0, 0)
    grid_spec = pltpu.PrefetchScalarGridSpec(
        num_scalar_prefetch=4,
        grid=(nblk,),
        in_specs=[pl.BlockSpec(memory_space=pl.ANY),
                  pl.BlockSpec((None, d, f), wmap),
                  pl.BlockSpec((None, d, f), wmap),
                  pl.BlockSpec((None, f, d), wmap)],
        out_specs=pl.BlockSpec((tm, d), lambda i, *_: (i, 0)),
        scratch_shapes=[pltpu.VMEM((tm, d), F32), pltpu.VMEM((tm, d), F32), pltpu.SemaphoreType.DMA((2,))],
    )
    return pl.pallas_call(
        functools.partial(_moe_kernel, tm=tm),
        out_shape=jax.ShapeDtypeStruct((nblk * tm, d), F32),
        grid_spec=grid_spec,
        compiler_params=_cparams(("arbitrary",)),
        name="moe_experts",
    )(bexp, bbase, nact, src, x1, wg, wu, wd)


def _combine_kernel(p0_ref, p1_ref, x_ref, r_ref, g_ref, be_ref, o_hbm, y_ref, obuf0, obuf1, sem, *, alpha, tm):
    i = pl.program_id(0)
    nsteps = pl.num_programs(0)

    bufs = (obuf0, obuf1)

    def issue(blk, slot):
        base = blk * tm
        for r in range(tm):
            for k, p_ref in enumerate((p0_ref, p1_ref)):
                pltpu.make_async_copy(o_hbm.at[pl.ds(p_ref[base + r], 1), :],
                                      bufs[slot].at[k, pl.ds(r, 1), :], sem.at[slot]).start(priority=k)

    def wait(slot):
        for k in range(2):
            pltpu.make_async_copy(o_hbm.at[pl.ds(0, tm), :], bufs[slot].at[k], sem.at[slot]).wait()

    @pl.when(i == 0)
    def _():
        issue(0, 0)

    def step(slot):
        wait(slot)

        @pl.when(i + 1 < nsteps)
        def _():
            issue(i + 1, 1 - slot)

        r = r_ref[...]
        y = alpha * x_ref[...] + r[:, 2:3] * bufs[slot][0] + r[:, 3:4] * bufs[slot][1]
        y_ref[...] = _layer_norm(y, g_ref[...], be_ref[...])

    for slot in range(2):
        pl.when(i % 2 == slot)(functools.partial(step, slot))


def _combine(x1, route, g, be, o, p0, p1, alpha, tm):
    n, d = x1.shape
    grid_spec = pltpu.PrefetchScalarGridSpec(
        num_scalar_prefetch=2,
        grid=(n // tm,),
        in_specs=[pl.BlockSpec((tm, d), lambda i, *_: (i, 0)),
                  pl.BlockSpec((tm, 8), lambda i, *_: (i, 0)),
                  pl.BlockSpec((1, d), lambda i, *_: (0, 0)),
                  pl.BlockSpec((1, d), lambda i, *_: (0, 0)),
                  pl.BlockSpec(memory_space=pl.ANY)],
        out_specs=pl.BlockSpec((tm, d), lambda i, *_: (i, 0)),
        scratch_shapes=[pltpu.VMEM((2, tm, d), F32), pltpu.VMEM((2, tm, d), F32), pltpu.SemaphoreType.DMA((2,))],
    )
    return pl.pallas_call(
        functools.partial(_combine_kernel, alpha=alpha, tm=tm),
        out_shape=jax.ShapeDtypeStruct((n, d), F32),
        grid_spec=grid_spec,
        compiler_params=_cparams(("arbitrary",)),
        name="moe_combine_ln",
    )(p0, p1, x1, route, g, be, o)


def _moe_layer(x1, route_t, wg, wu, wd, g, be, alpha, tm_e, tm_c):
    n, d = x1.shape
    route = route_t.T
    e = route[:, 0:2].astype(I32).reshape(2 * n)
    onehot = (e[:, None] == jnp.arange(N_EXPERTS, dtype=I32)[None, :]).astype(I32)
    rank = jnp.take_along_axis(jnp.cumsum(onehot, axis=0) - onehot, e[:, None], axis=1)[:, 0]
    counts = jnp.sum(onehot, axis=0)
    nblk_e = (counts + tm_e - 1) // tm_e
    blk_start = jnp.cumsum(nblk_e) - nblk_e
    seg_start = jnp.cumsum(counts) - counts
    pos = blk_start[e] * tm_e + rank
    src = (jnp.argsort(e, stable=True) // 2).astype(I32)
    nblk = (2 * n) // tm_e + N_EXPERTS
    blk = jnp.arange(nblk, dtype=I32)
    blk_end = jnp.cumsum(nblk_e)
    bexp = jnp.minimum(jnp.sum((blk[:, None] >= blk_end[None, :]).astype(I32), axis=1), N_EXPERTS - 1)
    bbase = (seg_start[bexp] + (blk - blk_start[bexp]) * tm_e).astype(I32)
    nact = jnp.sum(nblk_e).astype(I32).reshape(1)
    o = _moe_experts(x1, wg, wu, wd, bexp, bbase, nact, src, tm_e)
    p = pos.reshape(n, 2).astype(I32)
    return _combine(x1, route, g, be, o, p[:, 0], p[:, 1], alpha, tm_c)


def _rms(x, g):
    return x * lax.rsqrt(jnp.mean(x * x, axis=-1, keepdims=True) + RMS_EPS) * g

ODD_OFF = dict(qc=0, kc=512, vc=640, qi=768, ki=1024, wi=1152, cq=1280, ckv=1536, kra=1664, krb=1792, end=1920)


def _odd_proj_kernel(x_ref, w_ref, qng_ref, kvng_ref, wqa_ref, wqb_ref, wk_ref, wv_ref, cos_ref, sin_ref,
                     tq_ref, tk_ref,
                     qc_ref, kc_ref, vc_ref, qi_ref, ki_ref, wi_ref, q8_ref, k8_ref, v8_ref, *, q_scale):
    o = ODD_OFF
    tm = x_ref.shape[0]
    lo = _lane_lo((tm, LANE))
    h = _dot(x_ref[...].astype(BF16), w_ref[...])
    kdup = h[:, o["kc"]:o["vc"]]
    vdup = h[:, o["vc"]:o["qi"]]
    kc_ref[:, 0:LANE] = jnp.where(lo, kdup, tk_ref[:, 0:LANE].astype(F32)).astype(BF16)
    kc_ref[:, LANE:2 * LANE] = jnp.where(lo, tk_ref[:, LANE:2 * LANE].astype(F32), kdup).astype(BF16)
    vc_ref[:, 0:LANE] = jnp.where(lo, vdup, 1.0).astype(BF16)
    vc_ref[:, LANE:2 * LANE] = jnp.where(lo, 1.0, vdup).astype(BF16)
    qi_ref[...] = h[:, o["qi"]:o["ki"]].astype(BF16)
    ki_ref[...] = h[:, o["ki"]:o["wi"]].astype(BF16)
    wi_ref[...] = h[:, o["wi"]:o["wi"] + IDX_HEADS]
    cos = cos_ref[...]
    sin = sin_ref[...]
    cqn = _rms(h[:, o["cq"]:o["ckv"]], qng_ref[...]).astype(BF16)
    qa = _dot(cqn, wqa_ref[...])
    qb = _dot(cqn, wqb_ref[...])
    ckvn = _rms(h[:, o["ckv"]:o["kra"]], kvng_ref[...]).astype(BF16)
    kn = _dot(ckvn, wk_ref[...])
    v8 = _dot(ckvn, wv_ref[...])
    kr = h[:, o["kra"]:o["krb"]] * cos + h[:, o["krb"]:o["end"]] * sin
    for hd in range(D_HEADS):
        sl = slice(hd * LANE, (hd + 1) * LANE)
        pj = slice((hd // 2) * LANE, (hd // 2 + 1) * LANE)
        own = lo if hd % 2 == 0 else ~lo
        qc_ref[:, sl] = jnp.where(own, h[:, pj], tq_ref[:, sl].astype(F32)).astype(BF16)
        q8_ref[:, sl] = ((qa[:, sl] * cos + qb[:, sl] * sin) * q_scale).astype(BF16)
        k8_ref[:, sl] = (kn[:, sl] + kr).astype(BF16)
        v8_ref[:, sl] = jnp.where(own, v8[:, pj], 1.0).astype(BF16)


def _odd_proj(xt, w_cat, qng, kvng, wqa, wqb, wk, wv, cos_t, sin_t, tq_t, tk_t, seq, tm):
    n, d = xt.shape
    full = lambda i: (0, 0)
    tps = seq // tm
    pos = lambda i: (i % tps, 0)
    widths = (1024, 256, 256, 256, 128)
    out_shape = [jax.ShapeDtypeStruct((n, w), BF16) for w in widths]
    out_specs = [pl.BlockSpec((tm, w), lambda i: (i, 0)) for w in widths]
    out_shape.append(jax.ShapeDtypeStruct((n, IDX_HEADS), F32))
    out_specs.append(pl.BlockSpec((tm, IDX_HEADS), lambda i: (i, 0)))
    for w in (1024, 1024, 1024):
        out_shape.append(jax.ShapeDtypeStruct((n, w), BF16))
        out_specs.append(pl.BlockSpec((tm, w), lambda i: (i, 0)))
    return pl.pallas_call(
        functools.partial(_odd_proj_kernel, q_scale=(MLA_NOPE + MLA_ROPE) ** -0.5 * LOG2E),
        out_shape=out_shape,
        grid=(n // tm,),
        in_specs=[pl.BlockSpec((tm, d), lambda i: (i, 0)),
                  pl.BlockSpec(w_cat.shape, full),
                  pl.BlockSpec(qng.shape, full), pl.BlockSpec(kvng.shape, full),
                  pl.BlockSpec(wqa.shape, full), pl.BlockSpec(wqb.shape, full),
                  pl.BlockSpec(wk.shape, full), pl.BlockSpec(wv.shape, full),
                  pl.BlockSpec((tm, LANE), pos), pl.BlockSpec((tm, LANE), pos),
                  pl.BlockSpec((tm, C_HEADS * LANE), pos), pl.BlockSpec((tm, 2 * LANE), pos)],
        out_specs=out_specs,
        compiler_params=_cparams(("parallel",)),
        name="odd_proj",
    )(xt, w_cat, qng, kvng, wqa, wqb, wk, wv, cos_t, sin_t, tq_t, tk_t)


def _dsa_kernel(q_ref, k_ref, v_ref, qi_ref, ki_ref, wt_ref, o_ref, sc_ref, mb_ref, *, top_k, chunk):
    i = pl.program_id(1)
    nc = (i * BLOCK + BLOCK + chunk - 1) // chunk
    lo = _lane_lo((BLOCK, LANE))
    s_row2 = lax.broadcasted_iota(I32, (2 * chunk, BLOCK), 0)
    t_lane2 = i * BLOCK + lax.broadcasted_iota(I32, (2 * chunk, BLOCK), 1)

    def chunk_start(c):
        return pl.multiple_of(c * chunk, chunk)

    qidx = []
    for h in range(IDX_HEADS):
        qp = qi_ref[:, (h // 2) * LANE:(h // 2 + 1) * LANE]
        qidx.append(jnp.where(lo if h % 2 == 0 else ~lo, qp, jnp.zeros_like(qp)))
    q_stack = jnp.concatenate(qidx, axis=0)
    w = wt_ref[...]

    npair = (nc + 1) // 2

    def pair_start(cp):
        return pl.multiple_of(cp * (2 * chunk), 2 * chunk)

    def score_pair(cp, _):
        start = pair_start(cp)
        d = _dot_nt(ki_ref[pl.ds(start, 2 * chunk), :], q_stack)
        sc = w[0:1, :] * jnp.maximum(d[:, 0:BLOCK], 0.0)
        for h in range(1, IDX_HEADS):
            sc = sc + w[h:h + 1, :] * jnp.maximum(d[:, h * BLOCK:(h + 1) * BLOCK], 0.0)
        sc = jnp.where(sc == 0.0, 0.0, sc)
        sc_ref[pl.ds(start, 2 * chunk), :] = jnp.where(start + s_row2 <= t_lane2, sc, NEG_INF)
        return 0

    lax.fori_loop(0, npair, score_pair, 0)

    nacc = 4 * 8

    def count(trial, strict):
        def body(cp, acc):
            sc = sc_ref[pl.ds(pair_start(cp), 2 * chunk), :]
            hit = (sc > trial) if strict else (sc >= trial)
            return acc + jnp.sum(jnp.where(hit, 1.0, 0.0).reshape(2 * chunk // nacc, nacc, BLOCK), axis=0)

        acc = lax.fori_loop(0, npair, body, jnp.zeros((nacc, BLOCK), F32))
        return jnp.sum(acc, axis=0, keepdims=True)

    def decode(key):
        return lax.bitcast_convert_type(jnp.where(key < 0, key ^ jnp.int32(0x7FFFFFFF), key), F32)

    kf = float(top_k)
    key_neg_inf = jnp.int32(-2139095041)
    cand = jnp.where(count(jnp.zeros((1, BLOCK), F32), False) >= kf, 0, jnp.int32(-2 ** 31)).astype(I32)

    def bit_step(j, cand):
        trial = cand + lax.shift_left(jnp.int32(1), 30 - j)
        ok = (trial <= key_neg_inf) | (count(decode(trial), False) >= kf)
        return jnp.where(ok, trial, cand)

    thr = decode(lax.fori_loop(0, 31, bit_step, cand))
    need = kf - count(thr, True)

    r0 = lax.broadcasted_iota(I32, (chunk, chunk), 0)
    r1 = lax.broadcasted_iota(I32, (chunk, chunk), 1)
    below = jnp.where(r1 < r0, 1.0, 0.0).astype(BF16)
    need = jnp.where(thr == NEG_INF, 0.0, need)

    def mask_pair(cp, offset):
        start = pair_start(cp)
        sc = sc_ref[pl.ds(start, 2 * chunk), :]
        eq = sc == thr
        eqf = jnp.where(eq, 1.0, 0.0)
        eqb = eqf.astype(BF16)
        first = jnp.sum(eqf[0:chunk], axis=0, keepdims=True)
        prefix = jnp.concatenate([_dot(below, eqb[0:chunk]) + offset,
                                  _dot(below, eqb[chunk:2 * chunk]) + (offset + first)], axis=0)
        bias = jnp.where(sc > thr, 0.0, jnp.where(eq, jnp.where(prefix < need, 0.0, NEG_INF), NEG_INF))
        mb_ref[:, pl.ds(start, 2 * chunk)] = bias.T
        return offset + first + jnp.sum(eqf[chunk:2 * chunk], axis=0, keepdims=True)

    lax.fori_loop(0, npair, mask_pair, jnp.zeros((1, BLOCK), F32))

    def att_chunk(c, carry):
        start = chunk_start(c)
        qs = [q_ref[:, h * LANE:(h + 1) * LANE] for h in range(C_HEADS)]
        ks = [k_ref[pl.ds(start, chunk), (h % 2) * LANE:(h % 2 + 1) * LANE] for h in range(C_HEADS)]
        vs = [v_ref[pl.ds(start, chunk), (h % 2) * LANE:(h % 2 + 1) * LANE] for h in range(C_HEADS)]
        return _online_softmax_chains(qs, ks, vs, carry, bias=mb_ref[:, pl.ds(start, chunk)], guard_empty=True)

    init = tuple((jnp.full((BLOCK, 1), NEG_INF, F32), jnp.zeros((BLOCK, LANE), F32)) for _ in range(C_HEADS))
    carry = lax.fori_loop(0, nc, att_chunk, init)
    _normalize_pairs([c[1] for c in carry], o_ref)


def _dsa(qc, kc, vc, qi, ki, wit):
    b, s, _ = qc.shape
    top_k = min(DSA_TOPK_MAX, s // 4)
    chunk = 256
    assert s % (2 * chunk) == 0 and top_k <= chunk
    blk = lambda bi, i: (bi, i, 0)
    seq = lambda bi, i: (bi, 0, 0)
    return pl.pallas_call(
        functools.partial(_dsa_kernel, top_k=top_k, chunk=chunk),
        out_shape=jax.ShapeDtypeStruct((b, s, C_HEADS * HEAD_DIM), BF16),
        grid=(b, s // BLOCK),
        in_specs=[pl.BlockSpec((None, BLOCK, C_HEADS * LANE), blk),
                  pl.BlockSpec((None, s, 2 * LANE), seq),
                  pl.BlockSpec((None, s, 2 * LANE), seq),
                  pl.BlockSpec((None, BLOCK, 256), blk),
                  pl.BlockSpec((None, s, LANE), seq),
                  pl.BlockSpec((None, IDX_HEADS, BLOCK), lambda bi, i: (bi, 0, i))],
        out_specs=pl.BlockSpec((None, BLOCK, 512), blk),
        scratch_shapes=[pltpu.VMEM((s, BLOCK), F32), pltpu.VMEM((BLOCK, s), F32)],
        compiler_params=_cparams(("parallel", "arbitrary")),
        name="dsa_attn",
    )(qc, kc, vc, qi, ki, wit)


def _dup64(w):
    d, c = w.shape
    w = w.reshape(d, c // HEAD_DIM, 1, HEAD_DIM)
    return jnp.broadcast_to(w, (d, c // HEAD_DIM, 2, HEAD_DIM)).reshape(d, 2 * c)


def _pad_cols(w, width):
    return jnp.pad(w, ((0, 0), (0, width - w.shape[1])))


def _split_cols(w, sizes):
    out, start = [], 0
    for n in sizes:
        out.append(w[:, start:start + n])
        start += n
    return out


def _rope_partner(w):
    half = MLA_ROPE // 2
    return jnp.concatenate([-w[..., half:], w[..., :half]], axis=-1)


def _rope_block(w_rope):
    d = w_rope.shape[0]
    return jnp.concatenate([jnp.zeros((d, MLA_NOPE), F32), w_rope,
                            jnp.zeros((d, LANE - MLA_NOPE - MLA_ROPE), F32)], axis=1)


def _alibi_tables(s):
    slopes = 2.0 ** (-8.0 * jnp.arange(1, C_HEADS + 1, dtype=F32) / C_HEADS)
    a = slopes * LOG2E
    pos = jnp.arange(s, dtype=F32)
    a_sp = _split3(a)
    r_sp = _split3(-(a[None, :] * pos[:, None]))
    ones = jnp.ones((s, C_HEADS), F32)
    q_terms = [128.0 * t * ones for t in a_sp] + [t * ones for t in a_sp] + list(r_sp)
    q_aug = jnp.stack(q_terms, axis=-1)
    q_aug = jnp.pad(q_aug, ((0, 0), (0, 0), (0, HEAD_DIM - q_aug.shape[-1])))
    zeros = jnp.zeros_like(q_aug)
    even = jnp.concatenate([zeros, q_aug], axis=-1)
    odd = jnp.concatenate([q_aug, zeros], axis=-1)
    parity = (jnp.arange(C_HEADS) % 2 == 0)[None, :, None]
    tq_t = jnp.where(parity, even, odd).reshape(s, C_HEADS * LANE)
    s_hi = jnp.floor(pos / LANE)
    s_lo = pos - LANE * s_hi
    k_aug = jnp.stack([s_hi] * 3 + [s_lo] * 3 + [jnp.ones_like(pos)] * 3, axis=-1)
    k_aug = jnp.pad(k_aug, ((0, 0), (0, HEAD_DIM - k_aug.shape[-1])))
    kz = jnp.zeros_like(k_aug)
    tk_t = jnp.concatenate([kz, k_aug, k_aug, kz], axis=-1)
    return tq_t.astype(BF16), tk_t.astype(BF16)


def kernel(x, even_w_in, even_b_f, even_sinks, odd_w_in, odd_q_norm, odd_kv_norm, odd_w_uq, odd_w_ukv,
           w_o, ln_g, ln_b, router_w, router_b, moe_w_gate, moe_w_up, moe_w_down):
    b, s, d = x.shape
    n = b * s
    depth = w_o.shape[0]
    alpha = (2 * depth) ** 0.25
    tm = min(512, s)
    tq = min(256, s)
    tm_e = min(256, n)
    tm_c = min(256, n)
    att_scale = HEAD_DIM ** -0.5
    xt = x.reshape(n, d)
    rw_hi = _top16(router_w.T)
    rwt = jnp.concatenate([rw_hi, _top16(router_w.T - rw_hi)], axis=0).astype(BF16)
    rb = router_b.reshape(N_EXPERTS, 1)

    for l in range(depth):
        j = l // 2
        if l % 2 == 0:
            qa, ka, va, qb, kb, vb, fg = _split_cols(
                even_w_in[j], (512, 128, 128, 512, 512, 512, B_HEADS))
            w_cat = jnp.concatenate([qa * (att_scale * LOG2E), _dup64(ka), _dup64(va), qb * (att_scale * LOG2E), kb, vb,
                                     _pad_cols(fg, LANE)], axis=1).astype(BF16)
            bf_pad = _pad_cols(even_b_f[j].reshape(1, B_HEADS), LANE)
            qa_, ka_, va_, qf, kf, vf = _even_proj(xt, w_cat, bf_pad, s, tm)
            r3 = lambda a: a.reshape(b, s, a.shape[1])
            out_a = _swa(even_sinks[j], r3(qa_), r3(ka_), r3(va_))
            out_b = _flash(r3(qf), r3(kf), r3(vf), tq=tq, nh=8, name="fox_attn")
        else:
            (qc, kc, vc, qi, ki, wi, cq, ckv, kr) = _split_cols(
                odd_w_in[j], (512, 64, 64, IDX_HEADS * IDX_DIM, IDX_DIM, IDX_HEADS,
                              MLA_Q_RANK, MLA_KV_RANK, MLA_ROPE))
            w_cat = jnp.concatenate([
                qc * (att_scale * LOG2E), _dup64(kc), _dup64(vc), qi * (IDX_DIM ** -0.5), _dup64(ki),
                _pad_cols(wi * (IDX_HEADS ** -0.5), LANE), cq, ckv,
                _rope_block(kr), _rope_block(_rope_partner(kr))], axis=1).astype(BF16)
            dq = MLA_NOPE + MLA_ROPE
            wuq = odd_w_uq[j].reshape(MLA_Q_RANK, D_HEADS, dq)
            pad_q = jnp.zeros((MLA_Q_RANK, D_HEADS, LANE - dq), F32)
            wqa = jnp.concatenate([wuq, pad_q], axis=2).reshape(MLA_Q_RANK, D_HEADS * LANE).astype(BF16)
            wqb = jnp.concatenate([jnp.zeros((MLA_Q_RANK, D_HEADS, MLA_NOPE), F32),
                                   _rope_partner(wuq[..., MLA_NOPE:]), pad_q],
                                  axis=2).reshape(MLA_Q_RANK, D_HEADS * LANE).astype(BF16)
            wukv = odd_w_ukv[j].reshape(MLA_KV_RANK, D_HEADS, MLA_NOPE + MLA_V)
            wk = jnp.concatenate([wukv[..., :MLA_NOPE], jnp.zeros((MLA_KV_RANK, D_HEADS, LANE - MLA_NOPE), F32)],
                                 axis=2).reshape(MLA_KV_RANK, D_HEADS * LANE).astype(BF16)
            wv = wukv[..., MLA_NOPE:].reshape(MLA_KV_RANK, D_HEADS * MLA_V).astype(BF16)
            half = MLA_ROPE // 2
            inv = ROPE_THETA ** (-jnp.arange(half, dtype=F32) / half)
            ang = jnp.arange(s, dtype=F32)[:, None] * inv[None, :]
            cos_h, sin_h = jnp.cos(ang), jnp.sin(ang)
            pad_t = jnp.zeros((s, LANE - dq), F32)
            cos_t = jnp.concatenate([jnp.ones((s, MLA_NOPE), F32), cos_h, cos_h, pad_t], axis=1)
            sin_t = jnp.concatenate([jnp.zeros((s, MLA_NOPE), F32), sin_h, sin_h, pad_t], axis=1)
            tq_t, tk_t = _alibi_tables(s)
            qc_, kc_, vc_, qi_, ki_, wi_, q8, k8, v8 = _odd_proj(
                xt, w_cat, odd_q_norm[j].reshape(1, -1), odd_kv_norm[j].reshape(1, -1),
                wqa, wqb, wk, wv, cos_t, sin_t, tq_t, tk_t, s, tm)
            r3 = lambda a: a.reshape(b, s, a.shape[1])
            out_a = _dsa(r3(qc_), r3(kc_), r3(vc_), r3(qi_), r3(ki_), r3(wi_).transpose(0, 2, 1))
            out_b = _flash(r3(q8), r3(k8), r3(v8), tq=tq, nh=8, name="mla_attn")
        half_w = out_a.shape[2]
        wo = w_o[l].astype(BF16)
        x1, route_t = _outproj(out_a.reshape(n, half_w), out_b.reshape(n, -1), xt,
                               wo[:half_w], wo[half_w:], ln_g[l, 0].reshape(1, d), ln_b[l, 0].reshape(1, d),
                               rwt, rb, alpha, tm)
        xt = _moe_layer(x1, route_t, moe_w_gate[l].astype(BF16), moe_w_up[l].astype(BF16),
                        moe_w_down[l].astype(BF16), ln_g[l, 1].reshape(1, d), ln_b[l, 1].reshape(1, d),
                        alpha, tm_e, tm_c)
    return xt.reshape(b, s, d)
```

```python
import functools
import math

import jax
import jax.numpy as jnp
from jax import lax
from jax.experimental import pallas as pl
from jax.experimental.pallas import tpu as pltpu

F32 = jnp.float32
BF16 = jnp.bfloat16
I32 = jnp.int32

LANE = 128
HEAD_DIM = 64
BLOCK = 128
A_HEADS, A_KV_HEADS, WINDOW = 8, 2, 128
B_HEADS = 8
C_HEADS, IDX_HEADS, IDX_DIM, DSA_TOPK_MAX = 8, 4, 64, 256
D_HEADS, MLA_Q_RANK, MLA_KV_RANK, MLA_NOPE, MLA_ROPE, MLA_V = 8, 256, 128, 64, 32, 64
ROPE_THETA = 10000.0
N_EXPERTS, N_GROUPS = 16, 4
EXPERTS_PER_GROUP = N_EXPERTS // N_GROUPS
LN_EPS, RMS_EPS = 1e-5, 1e-6
NEG_INF = float("-inf")
LOG2E = math.log2(math.e)
VMEM_LIMIT = 56 * 1024 * 1024


def _cparams(sem):
    return pltpu.CompilerParams(dimension_semantics=sem, vmem_limit_bytes=VMEM_LIMIT)


def _dot(a, b):
    return jnp.dot(a, b, preferred_element_type=F32)


def _dot_nt(a, b):
    return lax.dot_general(a, b, (((1,), (1,)), ((), ())), preferred_element_type=F32)


def _lane_lo(shape):
    return lax.broadcasted_iota(I32, shape, len(shape) - 1) % LANE < HEAD_DIM


def _layer_norm(y, g, b):
    mu = jnp.mean(y, axis=-1, keepdims=True)
    yc = y - mu
    var = jnp.mean(yc * yc, axis=-1, keepdims=True)
    return yc * lax.rsqrt(var + LN_EPS) * g + b


def _top16(v):
    bits = lax.bitcast_convert_type(v, I32) & jnp.int32(-65536)
    return lax.bitcast_convert_type(bits, F32)


def _split3(x):
    s1 = _top16(x)
    r = x - s1
    s2 = _top16(r)
    return s1, s2, r - s2


def _aug_rel(shape, h):
    lane = lax.broadcasted_iota(I32, shape, 1)
    return lane - (HEAD_DIM if h % 2 == 0 else 0)


def _even_proj_kernel(x_ref, w_ref, bf_ref, qa_ref, ka_ref, va_ref, qf_ref, kf_ref, vf_ref,
                      carry_ref, *, tiles_per_seq):
    t = pl.program_id(0)
    tm = x_ref.shape[0]
    h = _dot(x_ref[...].astype(BF16), w_ref[...])
    qa_ref[...] = h[:, 0:512].astype(BF16)
    ka_ref[...] = h[:, 512:768].astype(BF16)
    va_ref[...] = h[:, 768:1024].astype(BF16)
    z = h[:, 2560:2688] + bf_ref[...]
    logf = jnp.minimum(z, 0.0) - jnp.log1p(jnp.exp(-jnp.abs(z)))

    @pl.when(t % tiles_per_seq == 0)
    def _():
        carry_ref[...] = jnp.zeros_like(carry_ref)

    row = lax.broadcasted_iota(I32, (tm, tm), 0)
    col = lax.broadcasted_iota(I32, (tm, tm), 1)
    tri = jnp.where(row >= col, 1.0, 0.0).astype(BF16)
    c = carry_ref[...]
    for part in _split3(logf):
        c = c + _dot(tri, part.astype(BF16))
    carry_ref[...] = c[tm - 1:tm, :]

    s1, s2, s3 = _split3(c * LOG2E)
    lo = _lane_lo((tm, LANE))
    for hd in range(B_HEADS):
        own = lo if hd % 2 == 0 else ~lo
        rel = _aug_rel((tm, LANE), hd)
        b1, b2, b3 = s1[:, hd:hd + 1], s2[:, hd:hd + 1], s3[:, hd:hd + 1]
        q_aug = jnp.where(rel == 3, b1, jnp.where(rel == 4, b2, jnp.where(rel == 5, b3,
                          jnp.where((rel >= 0) & (rel < 3), -1.0, 0.0))))
        k_aug = jnp.where(rel == 0, b1, jnp.where(rel == 1, b2, jnp.where(rel == 2, b3,
                          jnp.where((rel >= 3) & (rel < 6), 1.0, 0.0))))
        pj = (hd // 2) * LANE
        sl = slice(hd * LANE, (hd + 1) * LANE)
        qf_ref[:, sl] = jnp.where(own, h[:, 1024 + pj:1024 + pj + LANE], q_aug).astype(BF16)
        kf_ref[:, sl] = jnp.where(own, h[:, 1536 + pj:1536 + pj + LANE], k_aug).astype(BF16)
        vf_ref[:, sl] = jnp.where(own, h[:, 2048 + pj:2048 + pj + LANE], 1.0).astype(BF16)


def _even_proj(xt, w_cat, bf_pad, seq, tm):
    n, d = xt.shape
    wcols = w_cat.shape[1]
    widths = (512, 256, 256, 1024, 1024, 1024)
    out_shape = [jax.ShapeDtypeStruct((n, w), BF16) for w in widths]
    out_specs = [pl.BlockSpec((tm, w), lambda i: (i, 0)) for w in widths]
    return pl.pallas_call(
        functools.partial(_even_proj_kernel, tiles_per_seq=seq // tm),
        out_shape=out_shape,
        grid=(n // tm,),
        in_specs=[pl.BlockSpec((tm, d), lambda i: (i, 0)),
                  pl.BlockSpec((d, wcols), lambda i: (0, 0)),
                  pl.BlockSpec((1, LANE), lambda i: (0, 0))],
        out_specs=out_specs,
        scratch_shapes=[pltpu.VMEM((1, LANE), F32)],
        compiler_params=_cparams(("arbitrary",)),
        name="even_proj",
    )(xt, w_cat, bf_pad)


def _swa_kernel(sink_ref, q_ref, kp_ref, kc_ref, vp_ref, vc_ref, o_ref):
    i = pl.program_id(1)
    group = A_HEADS // A_KV_HEADS
    tl = lax.broadcasted_iota(I32, (BLOCK, 2 * BLOCK), 0)
    sl = lax.broadcasted_iota(I32, (BLOCK, 2 * BLOCK), 1)
    dist = BLOCK + tl - sl
    valid = (dist >= 0) & (dist < WINDOW) & ((sl >= BLOCK) | (i > 0))
    distf = dist.astype(F32)
    lo = _lane_lo((BLOCK, LANE))

    def band(prev_ref, cur_ref, g):
        return jnp.concatenate([prev_ref[:, g * LANE:(g + 1) * LANE], cur_ref[:, g * LANE:(g + 1) * LANE]], axis=0)

    stacked = []
    for g in range(A_KV_HEADS):
        qs = []
        for h in range(g * group, (g + 1) * group):
            qp = q_ref[:, (h // 2) * LANE:(h // 2 + 1) * LANE]
            qs.append(jnp.where(lo if h % 2 == 0 else ~lo, qp, jnp.zeros_like(qp)))
        stacked.append(_dot_nt(jnp.concatenate(qs, axis=0), band(kp_ref, kc_ref, g)))
    ps, denoms = [], []
    for h in range(A_HEADS):
        r = h % group
        slope = 2.0 ** (-8.0 * (h + 1) / A_HEADS) * LOG2E
        logits = jnp.where(valid, stacked[h // group][r * BLOCK:(r + 1) * BLOCK] - slope * distf, NEG_INF)
        sink = sink_ref[h] * LOG2E
        m = jnp.maximum(jnp.max(logits, axis=-1, keepdims=True), sink)
        e = jnp.exp2(logits - m)
        denoms.append(jnp.sum(e, axis=-1, keepdims=True) + jnp.exp2(sink - m))
        ps.append(e.astype(BF16))
    outs = []
    for g in range(A_KV_HEADS):
        pv = _dot(jnp.concatenate(ps[g * group:(g + 1) * group], axis=0), band(vp_ref, vc_ref, g))
        for r in range(group):
            outs.append(pv[r * BLOCK:(r + 1) * BLOCK] / denoms[g * group + r])
    for j in range(A_HEADS // 2):
        o_ref[:, j * LANE:(j + 1) * LANE] = jnp.where(lo, outs[2 * j], outs[2 * j + 1]).astype(BF16)


def _swa(sinks, qa, ka, va):
    b, s, _ = qa.shape
    nb = s // BLOCK
    prev = lambda bi, i: (bi, jnp.maximum(i - 1, 0), 0)
    cur = lambda bi, i: (bi, i, 0)
    return pl.pallas_call(
        _swa_kernel,
        out_shape=jax.ShapeDtypeStruct((b, s, A_HEADS * HEAD_DIM), BF16),
        grid=(b, nb),
        in_specs=[pl.BlockSpec(memory_space=pltpu.SMEM),
                  pl.BlockSpec((None, BLOCK, 512), cur),
                  pl.BlockSpec((None, BLOCK, 256), prev),
                  pl.BlockSpec((None, BLOCK, 256), cur),
                  pl.BlockSpec((None, BLOCK, 256), prev),
                  pl.BlockSpec((None, BLOCK, 256), cur)],
        out_specs=pl.BlockSpec((None, BLOCK, 512), cur),
        compiler_params=_cparams(("parallel", "arbitrary")),
        name="swa_attn",
    )(sinks, qa, ka, ka, va, va)


def _normalize_pairs(accs, o_ref):
    lo = _lane_lo(accs[0].shape)
    outs = [a / pltpu.roll(a, HEAD_DIM, 1) for a in accs]
    for j in range(len(accs) // 2):
        o_ref[:, j * LANE:(j + 1) * LANE] = jnp.where(lo, outs[2 * j], outs[2 * j + 1]).astype(BF16)


def _online_softmax_chains(qs, ks, vs, carry, bias=None, masks=None, guard_empty=False):
    n = len(qs)
    ss = [_dot_nt(qs[i], ks[i]) for i in range(n)]
    if bias is not None:
        ss = [s + bias for s in ss]
    if masks is not None:
        ss = [jnp.where(masks[i], ss[i], NEG_INF) for i in range(n)]
    ms, ps, alphas = [], [], []
    for i in range(n):
        m = carry[i][0]
        m_new = jnp.maximum(m, jnp.max(ss[i], axis=-1, keepdims=True))
        m_ref = jnp.where(m_new == NEG_INF, 0.0, m_new) if guard_empty else m_new
        ps.append(jnp.exp2(ss[i] - m_ref).astype(BF16))
        alphas.append(jnp.exp2(m - m_ref))
        ms.append(m_new)
    pvs = [_dot(ps[i], vs[i]) for i in range(n)]
    return tuple((ms[i], alphas[i] * carry[i][1] + pvs[i]) for i in range(n))


def _flash_kernel(q_ref, k_ref, v_ref, o_ref, *, tq, nh):
    qi = pl.program_id(2)
    nslab = tq // BLOCK
    chains = [(hh, r) for hh in range(nh) for r in range(nslab)]
    row = lax.broadcasted_iota(I32, (BLOCK, tq), 0)
    col = lax.broadcasted_iota(I32, (BLOCK, tq), 1)

    def lanes(hh):
        return slice(hh * LANE, (hh + 1) * LANE)

    def step(kb, carry, diagonal):
        start = pl.multiple_of(kb * tq, tq)
        qs = [q_ref[r * BLOCK:(r + 1) * BLOCK, lanes(hh)] for hh, r in chains]
        ks = [k_ref[pl.ds(start, tq), lanes(hh)] for hh, r in chains]
        vs = [v_ref[pl.ds(start, tq), lanes(hh)] for hh, r in chains]
        masks = [col <= row + r * BLOCK for hh, r in chains] if diagonal else None
        return _online_softmax_chains(qs, ks, vs, carry, masks=masks)

    init = tuple((jnp.full((BLOCK, 1), NEG_INF, F32), jnp.zeros((BLOCK, LANE), F32))
                 for _ in range(nh * nslab))
    carry = lax.fori_loop(0, qi, lambda kb, c: step(kb, c, False), init)
    carry = step(qi, carry, True)
    accs = [jnp.concatenate([carry[hh * nslab + r][1] for r in range(nslab)], axis=0) for hh in range(nh)]
    _normalize_pairs(accs, o_ref)


def _flash(q, k, v, *, tq, nh, name):
    b, s, w = v.shape
    return pl.pallas_call(
        functools.partial(_flash_kernel, tq=tq, nh=nh),
        out_shape=jax.ShapeDtypeStruct((b, s, w // 2), BF16),
        grid=(b, w // (nh * LANE), s // tq),
        in_specs=[pl.BlockSpec((None, tq, nh * LANE), lambda bi, j, i: (bi, i, j)),
                  pl.BlockSpec((None, s, nh * LANE), lambda bi, j, i: (bi, 0, j)),
                  pl.BlockSpec((None, s, nh * LANE), lambda bi, j, i: (bi, 0, j))],
        out_specs=pl.BlockSpec((None, tq, nh * LANE // 2), lambda bi, j, i: (bi, i, j)),
        compiler_params=_cparams(("parallel", "parallel", "arbitrary")),
        name=name,
    )(q, k, v)


def _route_rows(aff, sel):
    gs = []
    for g in range(N_GROUPS):
        r = sel[g * EXPERTS_PER_GROUP:(g + 1) * EXPERTS_PER_GROUP]
        best = None
        for a in range(EXPERTS_PER_GROUP):
            for c in range(a + 1, EXPERTS_PER_GROUP):
                pair = r[a] + r[c]
                best = pair if best is None else jnp.maximum(best, pair)
        gs.append(best)
    gmax = functools.reduce(jnp.maximum, gs)
    gidx = jnp.full_like(gmax, float(N_GROUPS - 1))
    for g in range(N_GROUPS - 2, -1, -1):
        gidx = jnp.where(gs[g] == gmax, float(g), gidx)

    def pick(rows, k):
        out = rows[(N_GROUPS - 1) * EXPERTS_PER_GROUP + k]
        for g in range(N_GROUPS - 2, -1, -1):
            out = jnp.where(gidx == float(g), rows[g * EXPERTS_PER_GROUP + k], out)
        return out

    s = [pick(sel, k) for k in range(EXPERTS_PER_GROUP)]
    a = [pick(aff, k) for k in range(EXPERTS_PER_GROUP)]

    def first_max(vals):
        mx = functools.reduce(jnp.maximum, vals)
        idx = jnp.full_like(mx, float(len(vals) - 1))
        for k in range(len(vals) - 2, -1, -1):
            idx = jnp.where(vals[k] == mx, float(k), idx)
        return idx

    def take(vals, idx):
        out = vals[-1]
        for k in range(len(vals) - 2, -1, -1):
            out = jnp.where(idx == float(k), vals[k], out)
        return out

    i1 = first_max(s)
    s2 = [jnp.where(i1 == float(k), NEG_INF, s[k]) for k in range(EXPERTS_PER_GROUP)]
    i2 = first_max(s2)
    a1, a2 = take(a, i1), take(a, i2)
    tot = a1 + a2
    base = gidx * float(EXPERTS_PER_GROUP)
    return base + i1, base + i2, a1 / tot, a2 / tot


def _outproj_kernel(a_ref, b_ref, x_ref, wa_ref, wb_ref, g_ref, be_ref, rwt_ref, rb_ref,
                    x1_ref, route_ref, *, alpha):
    mix = _dot(a_ref[...], wa_ref[...]) + _dot(b_ref[...], wb_ref[...])
    x1 = _layer_norm(alpha * x_ref[...] + mix, g_ref[...], be_ref[...])
    x1_ref[...] = x1
    x_hi = x1.astype(BF16)
    x_lo = (x1 - x_hi.astype(F32)).astype(BF16)
    w = rwt_ref[...]
    w_hi = w.astype(BF16)
    w_lo = (w - w_hi.astype(F32)).astype(BF16)
    z_hi = _dot_nt(jnp.concatenate([w_hi, w_lo], axis=0), x_hi)
    zt = z_hi[0:N_EXPERTS] + z_hi[N_EXPERTS:2 * N_EXPERTS] + _dot_nt(w_hi, x_lo)
    afft = 1.0 / (1.0 + jnp.exp(-zt))
    selt = afft + rb_ref[...]
    aff = [afft[e:e + 1, :] for e in range(N_EXPERTS)]
    sel = [selt[e:e + 1, :] for e in range(N_EXPERTS)]
    e1, e2, w1, w2 = _route_rows(aff, sel)
    zero = jnp.zeros_like(e1)
    route_ref[...] = jnp.concatenate([e1, e2, w1, w2, zero, zero, zero, zero], axis=0)


def _outproj(a, b, xt, wa, wb, g, be, rwt, rb, alpha, tm):
    n, d = xt.shape
    full = lambda i: (0, 0)
    return pl.pallas_call(
        functools.partial(_outproj_kernel, alpha=alpha),
        out_shape=[jax.ShapeDtypeStruct((n, d), F32), jax.ShapeDtypeStruct((8, n), F32)],
        grid=(n // tm,),
        in_specs=[pl.BlockSpec((tm, a.shape[1]), lambda i: (i, 0)),
                  pl.BlockSpec((tm, b.shape[1]), lambda i: (i, 0)),
                  pl.BlockSpec((tm, d), lambda i: (i, 0)),
                  pl.BlockSpec(wa.shape, full), pl.BlockSpec(wb.shape, full),
                  pl.BlockSpec((1, d), full), pl.BlockSpec((1, d), full),
                  pl.BlockSpec(rwt.shape, full), pl.BlockSpec(rb.shape, full)],
        out_specs=[pl.BlockSpec((tm, d), lambda i: (i, 0)), pl.BlockSpec((8, tm), lambda i: (0, i))],
        compiler_params=_cparams(("parallel",)),
        name="outproj_ln_route",
    )(a, b, xt, wa, wb, g, be, rwt, rb)


def _moe_kernel(bexp_ref, bbase_ref, nact_ref, src_ref,
                x_hbm, wg_ref, wu_ref, wd_ref, o_ref, xbuf0, xbuf1, sem, *, tm):
    i = pl.program_id(0)
    nact = nact_ref[0]
    last = src_ref.shape[0] - 1
    bufs = (xbuf0, xbuf1)

    def issue(blk, slot):
        base = bbase_ref[blk]
        for r in range(tm):
            tok = src_ref[jnp.minimum(base + r, last)]
            pltpu.make_async_copy(x_hbm.at[pl.ds(tok, 1), :], bufs[slot].at[pl.ds(r, 1), :],
                                  sem.at[slot]).start(priority=r % 2)

    def wait(slot):
        pltpu.make_async_copy(x_hbm.at[pl.ds(0, tm), :], bufs[slot], sem.at[slot]).wait()

    @pl.when((i == 0) & (nact > 0))
    def _():
        issue(0, 0)

    def step(slot):
        wait(slot)

        @pl.when(i + 1 < nact)
        def _():
            issue(i + 1, 1 - slot)

        xb = bufs[slot][...].astype(BF16)
        gate = _dot(xb, wg_ref[...])
        up = _dot(xb, wu_ref[...])
        h = gate * (1.0 / (1.0 + jnp.exp(-gate))) * up
        o_ref[...] = _dot(h.astype(BF16), wd_ref[...])

    for slot in range(2):
        pl.when((i < nact) & (i % 2 == slot))(functools.partial(step, slot))

    @pl.when(i >= nact)
    def _():
        o_ref[...] = jnp.zeros_like(o_ref)


def _moe_experts(x1, wg, wu, wd, bexp, bbase, nact, src, tm):
    n, d = x1.shape
    nblk = bexp.shape[0]
    f = wg.shape[2]
    wmap = lambda i, bexp, *_: (bexp[i], 0, 0)
    grid_spec = pltpu.PrefetchScalarGridSpec(
        num_scalar_prefetch=4,
        grid=(nblk,),
        in_specs=[pl.BlockSpec(memory_space=pl.ANY),
                  pl.BlockSpec((None, d, f), wmap),
                  pl.BlockSpec((None, d, f), wmap),
                  pl.BlockSpec((None, f, d), wmap)],
        out_specs=pl.BlockSpec((tm, d), lambda i, *_: (i, 0)),
        scratch_shapes=[pltpu.VMEM((tm, d), F32), pltpu.VMEM((tm, d), F32), pltpu.SemaphoreType.DMA((2,))],
    )
    return pl.pallas_call(
        functools.partial(_moe_kernel, tm=tm),
        out_shape=jax.ShapeDtypeStruct((nblk * tm, d), F32),
        grid_spec=grid_spec,
        compiler_params=_cparams(("arbitrary",)),
        name="moe_experts",
    )(bexp, bbase, nact, src, x1, wg, wu, wd)


def _combine_kernel(p0_ref, p1_ref, x_ref, r_ref, g_ref, be_ref, o_hbm, y_ref, obuf0, obuf1, sem, *, alpha, tm):
    i = pl.program_id(0)
    nsteps = pl.num_programs(0)

    bufs = (obuf0, obuf1)

    def issue(blk, slot):
        base = blk * tm
        for r in range(tm):
            for k, p_ref in enumerate((p0_ref, p1_ref)):
                pltpu.make_async_copy(o_hbm.at[pl.ds(p_ref[base + r], 1), :],
                                      bufs[slot].at[k, pl.ds(r, 1), :], sem.at[slot]).start(priority=k)

    def wait(slot):
        for k in range(2):
            pltpu.make_async_copy(o_hbm.at[pl.ds(0, tm), :], bufs[slot].at[k], sem.at[slot]).wait()

    @pl.when(i == 0)
    def _():
        issue(0, 0)

    def step(slot):
        wait(slot)

        @pl.when(i + 1 < nsteps)
        def _():
            issue(i + 1, 1 - slot)

        r = r_ref[...]
        y = alpha * x_ref[...] + r[:, 2:3] * bufs[slot][0] + r[:, 3:4] * bufs[slot][1]
        y_ref[...] = _layer_norm(y, g_ref[...], be_ref[...])

    for slot in range(2):
        pl.when(i % 2 == slot)(functools.partial(step, slot))


def _combine(x1, route, g, be, o, p0, p1, alpha, tm):
    n, d = x1.shape
    grid_spec = pltpu.PrefetchScalarGridSpec(
        num_scalar_prefetch=2,
        grid=(n // tm,),
        in_specs=[pl.BlockSpec((tm, d), lambda i, *_: (i, 0)),
                  pl.BlockSpec((tm, 8), lambda i, *_: (i, 0)),
                  pl.BlockSpec((1, d), lambda i, *_: (0, 0)),
                  pl.BlockSpec((1, d), lambda i, *_: (0, 0)),
                  pl.BlockSpec(memory_space=pl.ANY)],
        out_specs=pl.BlockSpec((tm, d), lambda i, *_: (i, 0)),
        scratch_shapes=[pltpu.VMEM((2, tm, d), F32), pltpu.VMEM((2, tm, d), F32), pltpu.SemaphoreType.DMA((2,))],
    )
    return pl.pallas_call(
        functools.partial(_combine_kernel, alpha=alpha, tm=tm),
        out_shape=jax.ShapeDtypeStruct((n, d), F32),
        grid_spec=grid_spec,
        compiler_params=_cparams(("arbitrary",)),
        name="moe_combine_ln",
    )(p0, p1, x1, route, g, be, o)


def _moe_layer(x1, route_t, wg, wu, wd, g, be, alpha, tm_e, tm_c):
    n, d = x1.shape
    route = route_t.T
    e = route[:, 0:2].astype(I32).reshape(2 * n)
    onehot = (e[:, None] == jnp.arange(N_EXPERTS, dtype=I32)[None, :]).astype(I32)
    rank = jnp.take_along_axis(jnp.cumsum(onehot, axis=0) - onehot, e[:, None], axis=1)[:, 0]
    counts = jnp.sum(onehot, axis=0)
    nblk_e = (counts + tm_e - 1) // tm_e
    blk_start = jnp.cumsum(nblk_e) - nblk_e
    seg_start = jnp.cumsum(counts) - counts
    pos = blk_start[e] * tm_e + rank
    src = (jnp.argsort(e, stable=True) // 2).astype(I32)
    nblk = (2 * n) // tm_e + N_EXPERTS
    blk = jnp.arange(nblk, dtype=I32)
    blk_end = jnp.cumsum(nblk_e)
    bexp = jnp.minimum(jnp.sum((blk[:, None] >= blk_end[None, :]).astype(I32), axis=1), N_EXPERTS - 1)
    bbase = (seg_start[bexp] + (blk - blk_start[bexp]) * tm_e).astype(I32)
    nact = jnp.sum(nblk_e).astype(I32).reshape(1)
    o = _moe_experts(x1, wg, wu, wd, bexp, bbase, nact, src, tm_e)
    p = pos.reshape(n, 2).astype(I32)
    return _combine(x1, route, g, be, o, p[:, 0], p[:, 1], alpha, tm_c)


def _rms(x, g):
    return x * lax.rsqrt(jnp.mean(x * x, axis=-1, keepdims=True) + RMS_EPS) * g

ODD_OFF = dict(qc=0, kc=512, vc=640, qi=768, ki=1024, wi=1152, cq=1280, ckv=1536, kra=1664, krb=1792, end=1920)


def _odd_proj_kernel(x_ref, w_ref, qng_ref, kvng_ref, wqa_ref, wqb_ref, wk_ref, wv_ref, cos_ref, sin_ref,
                     tq_ref, tk_ref,
                     qc_ref, kc_ref, vc_ref, qi_ref, ki_ref, wi_ref, q8_ref, k8_ref, v8_ref, *, q_scale):
    o = ODD_OFF
    tm = x_ref.shape[0]
    lo = _lane_lo((tm, LANE))
    h = _dot(x_ref[...].astype(BF16), w_ref[...])
    kdup = h[:, o["kc"]:o["vc"]]
    vdup = h[:, o["vc"]:o["qi"]]
    kc_ref[:, 0:LANE] = jnp.where(lo, kdup, tk_ref[:, 0:LANE].astype(F32)).astype(BF16)
    kc_ref[:, LANE:2 * LANE] = jnp.where(lo, tk_ref[:, LANE:2 * LANE].astype(F32), kdup).astype(BF16)
    vc_ref[:, 0:LANE] = jnp.where(lo, vdup, 1.0).astype(BF16)
    vc_ref[:, LANE:2 * LANE] = jnp.where(lo, 1.0, vdup).astype(BF16)
    qi_ref[...] = h[:, o["qi"]:o["ki"]].astype(BF16)
    ki_ref[...] = h[:, o["ki"]:o["wi"]].astype(BF16)
    wi_ref[...] = h[:, o["wi"]:o["wi"] + IDX_HEADS]
    cos = cos_ref[...]
    sin = sin_ref[...]
    cqn = _rms(h[:, o["cq"]:o["ckv"]], qng_ref[...]).astype(BF16)
    qa = _dot(cqn, wqa_ref[...])
    qb = _dot(cqn, wqb_ref[...])
    ckvn = _rms(h[:, o["ckv"]:o["kra"]], kvng_ref[...]).astype(BF16)
    kn = _dot(ckvn, wk_ref[...])
    v8 = _dot(ckvn, wv_ref[...])
    kr = h[:, o["kra"]:o["krb"]] * cos + h[:, o["krb"]:o["end"]] * sin
    for hd in range(D_HEADS):
        sl = slice(hd * LANE, (hd + 1) * LANE)
        pj = slice((hd // 2) * LANE, (hd // 2 + 1) * LANE)
        own = lo if hd % 2 == 0 else ~lo
        qc_ref[:, sl] = jnp.where(own, h[:, pj], tq_ref[:, sl].astype(F32)).astype(BF16)
        q8_ref[:, sl] = ((qa[:, sl] * cos + qb[:, sl] * sin) * q_scale).astype(BF16)
        k8_ref[:, sl] = (kn[:, sl] + kr).astype(BF16)
        v8_ref[:, sl] = jnp.where(own, v8[:, pj], 1.0).astype(BF16)


def _odd_proj(xt, w_cat, qng, kvng, wqa, wqb, wk, wv, cos_t, sin_t, tq_t, tk_t, seq, tm):
    n, d = xt.shape
    full = lambda i: (0, 0)
    tps = seq // tm
    pos = lambda i: (i % tps, 0)
    widths = (1024, 256, 256, 256, 128)
    out_shape = [jax.ShapeDtypeStruct((n, w), BF16) for w in widths]
    out_specs = [pl.BlockSpec((tm, w), lambda i: (i, 0)) for w in widths]
    out_shape.append(jax.ShapeDtypeStruct((n, IDX_HEADS), F32))
    out_specs.append(pl.BlockSpec((tm, IDX_HEADS), lambda i: (i, 0)))
    for w in (1024, 1024, 1024):
        out_shape.append(jax.ShapeDtypeStruct((n, w), BF16))
        out_specs.append(pl.BlockSpec((tm, w), lambda i: (i, 0)))
    return pl.pallas_call(
        functools.partial(_odd_proj_kernel, q_scale=(MLA_NOPE + MLA_ROPE) ** -0.5 * LOG2E),
        out_shape=out_shape,
        grid=(n // tm,),
        in_specs=[pl.BlockSpec((tm, d), lambda i: (i, 0)),
                  pl.BlockSpec(w_cat.shape, full),
                  pl.BlockSpec(qng.shape, full), pl.BlockSpec(kvng.shape, full),
                  pl.BlockSpec(wqa.shape, full), pl.BlockSpec(wqb.shape, full),
                  pl.BlockSpec(wk.shape, full), pl.BlockSpec(wv.shape, full),
                  pl.BlockSpec((tm, LANE), pos), pl.BlockSpec((tm, LANE), pos),
                  pl.BlockSpec((tm, C_HEADS * LANE), pos), pl.BlockSpec((tm, 2 * LANE), pos)],
        out_specs=out_specs,
        compiler_params=_cparams(("parallel",)),
        name="odd_proj",
    )(xt, w_cat, qng, kvng, wqa, wqb, wk, wv, cos_t, sin_t, tq_t, tk_t)


def _dsa_kernel(q_ref, k_ref, v_ref, qi_ref, ki_ref, wt_ref, o_ref, sc_ref, mb_ref, *, top_k, chunk):
    i = pl.program_id(1)
    nc = (i * BLOCK + BLOCK + chunk - 1) // chunk
    lo = _lane_lo((BLOCK, LANE))
    s_row2 = lax.broadcasted_iota(I32, (2 * chunk, BLOCK), 0)
    t_lane2 = i * BLOCK + lax.broadcasted_iota(I32, (2 * chunk, BLOCK), 1)

    def chunk_start(c):
        return pl.multiple_of(c * chunk, chunk)

    qidx = []
    for h in range(IDX_HEADS):
        qp = qi_ref[:, (h // 2) * LANE:(h // 2 + 1) * LANE]
        qidx.append(jnp.where(lo if h % 2 == 0 else ~lo, qp, jnp.zeros_like(qp)))
    q_stack = jnp.concatenate(qidx, axis=0)
    w = wt_ref[...]

    npair = (nc + 1) // 2

    def pair_start(cp):
        return pl.multiple_of(cp * (2 * chunk), 2 * chunk)

    def score_pair(cp, _):
        start = pair_start(cp)
        d = _dot_nt(ki_ref[pl.ds(start, 2 * chunk), :], q_stack)
        sc = w[0:1, :] * jnp.maximum(d[:, 0:BLOCK], 0.0)
        for h in range(1, IDX_HEADS):
            sc = sc + w[h:h + 1, :] * jnp.maximum(d[:, h * BLOCK:(h + 1) * BLOCK], 0.0)
        sc = jnp.where(sc == 0.0, 0.0, sc)
        sc_ref[pl.ds(start, 2 * chunk), :] = jnp.where(start + s_row2 <= t_lane2, sc, NEG_INF)
        return 0

    lax.fori_loop(0, npair, score_pair, 0)

    nacc = 4 * 8

    def count(trial, strict):
        def body(cp, acc):
            sc = sc_ref[pl.ds(pair_start(cp), 2 * chunk), :]
            hit = (sc > trial) if strict else (sc >= trial)
            return acc + jnp.sum(jnp.where(hit, 1.0, 0.0).reshape(2 * chunk // nacc, nacc, BLOCK), axis=0)

        acc = lax.fori_loop(0, npair, body, jnp.zeros((nacc, BLOCK), F32))
        return jnp.sum(acc, axis=0, keepdims=True)

    def decode(key):
        return lax.bitcast_convert_type(jnp.where(key < 0, key ^ jnp.int32(0x7FFFFFFF), key), F32)

    kf = float(top_k)
    key_neg_inf = jnp.int32(-2139095041)
    cand = jnp.where(count(jnp.zeros((1, BLOCK), F32), False) >= kf, 0, jnp.int32(-2 ** 31)).astype(I32)

    def bit_step(j, cand):
        trial = cand + lax.shift_left(jnp.int32(1), 30 - j)
        ok = (trial <= key_neg_inf) | (count(decode(trial), False) >= kf)
        return jnp.where(ok, trial, cand)

    thr = decode(lax.fori_loop(0, 31, bit_step, cand))
    need = kf - count(thr, True)

    r0 = lax.broadcasted_iota(I32, (chunk, chunk), 0)
    r1 = lax.broadcasted_iota(I32, (chunk, chunk), 1)
    below = jnp.where(r1 < r0, 1.0, 0.0).astype(BF16)
    need = jnp.where(thr == NEG_INF, 0.0, need)

    def mask_pair(cp, offset):
        start = pair_start(cp)
        sc = sc_ref[pl.ds(start, 2 * chunk), :]
        eq = sc == thr
        eqf = jnp.where(eq, 1.0, 0.0)
        eqb = eqf.astype(BF16)
        first = jnp.sum(eqf[0:chunk], axis=0, keepdims=True)
        prefix = jnp.concatenate([_dot(below, eqb[0:chunk]) + offset,
                                  _dot(below, eqb[chunk:2 * chunk]) + (offset + first)], axis=0)
        bias = jnp.where(sc > thr, 0.0, jnp.where(eq, jnp.where(prefix < need, 0.0, NEG_INF), NEG_INF))
        mb_ref[:, pl.ds(start, 2 * chunk)] = bias.T
        return offset + first + jnp.sum(eqf[chunk:2 * chunk], axis=0, keepdims=True)

    lax.fori_loop(0, npair, mask_pair, jnp.zeros((1, BLOCK), F32))

    def att_chunk(c, carry):
        start = chunk_start(c)
        qs = [q_ref[:, h * LANE:(h + 1) * LANE] for h in range(C_HEADS)]
        ks = [k_ref[pl.ds(start, chunk), (h % 2) * LANE:(h % 2 + 1) * LANE] for h in range(C_HEADS)]
        vs = [v_ref[pl.ds(start, chunk), (h % 2) * LANE:(h % 2 + 1) * LANE] for h in range(C_HEADS)]
        return _online_softmax_chains(qs, ks, vs, carry, bias=mb_ref[:, pl.ds(start, chunk)], guard_empty=True)

    init = tuple((jnp.full((BLOCK, 1), NEG_INF, F32), jnp.zeros((BLOCK, LANE), F32)) for _ in range(C_HEADS))
    carry = lax.fori_loop(0, nc, att_chunk, init)
    _normalize_pairs([c[1] for c in carry], o_ref)


def _dsa(qc, kc, vc, qi, ki, wit):
    b, s, _ = qc.shape
    top_k = min(DSA_TOPK_MAX, s // 4)
    chunk = 256
    assert s % (2 * chunk) == 0 and top_k <= chunk
    blk = lambda bi, i: (bi, i, 0)
    seq = lambda bi, i: (bi, 0, 0)
    return pl.pallas_call(
        functools.partial(_dsa_kernel, top_k=top_k, chunk=chunk),
        out_shape=jax.ShapeDtypeStruct((b, s, C_HEADS * HEAD_DIM), BF16),
        grid=(b, s // BLOCK),
        in_specs=[pl.BlockSpec((None, BLOCK, C_HEADS * LANE), blk),
                  pl.BlockSpec((None, s, 2 * LANE), seq),
                  pl.BlockSpec((None, s, 2 * LANE), seq),
                  pl.BlockSpec((None, BLOCK, 256), blk),
                  pl.BlockSpec((None, s, LANE), seq),
                  pl.BlockSpec((None, IDX_HEADS, BLOCK), lambda bi, i: (bi, 0, i))],
        out_specs=pl.BlockSpec((None, BLOCK, 512), blk),
        scratch_shapes=[pltpu.VMEM((s, BLOCK), F32), pltpu.VMEM((BLOCK, s), F32)],
        compiler_params=_cparams(("parallel", "arbitrary")),
        name="dsa_attn",
    )(qc, kc, vc, qi, ki, wit)


def _dup64(w):
    d, c = w.shape
    w = w.reshape(d, c // HEAD_DIM, 1, HEAD_DIM)
    return jnp.broadcast_to(w, (d, c // HEAD_DIM, 2, HEAD_DIM)).reshape(d, 2 * c)


def _pad_cols(w, width):
    return jnp.pad(w, ((0, 0), (0, width - w.shape[1])))


def _split_cols(w, sizes):
    out, start = [], 0
    for n in sizes:
        out.append(w[:, start:start + n])
        start += n
    return out


def _rope_partner(w):
    half = MLA_ROPE // 2
    return jnp.concatenate([-w[..., half:], w[..., :half]], axis=-1)


def _rope_block(w_rope):
    d = w_rope.shape[0]
    return jnp.concatenate([jnp.zeros((d, MLA_NOPE), F32), w_rope,
                            jnp.zeros((d, LANE - MLA_NOPE - MLA_ROPE), F32)], axis=1)


def _alibi_tables(s):
    slopes = 2.0 ** (-8.0 * jnp.arange(1, C_HEADS + 1, dtype=F32) / C_HEADS)
    a = slopes * LOG2E
    pos = jnp.arange(s, dtype=F32)
    a_sp = _split3(a)
    r_sp = _split3(-(a[None, :] * pos[:, None]))
    ones = jnp.ones((s, C_HEADS), F32)
    q_terms = [128.0 * t * ones for t in a_sp] + [t * ones for t in a_sp] + list(r_sp)
    q_aug = jnp.stack(q_terms, axis=-1)
    q_aug = jnp.pad(q_aug, ((0, 0), (0, 0), (0, HEAD_DIM - q_aug.shape[-1])))
    zeros = jnp.zeros_like(q_aug)
    even = jnp.concatenate([zeros, q_aug], axis=-1)
    odd = jnp.concatenate([q_aug, zeros], axis=-1)
    parity = (jnp.arange(C_HEADS) % 2 == 0)[None, :, None]
    tq_t = jnp.where(parity, even, odd).reshape(s, C_HEADS * LANE)
    s_hi = jnp.floor(pos / LANE)
    s_lo = pos - LANE * s_hi
    k_aug = jnp.stack([s_hi] * 3 + [s_lo] * 3 + [jnp.ones_like(pos)] * 3, axis=-1)
    k_aug = jnp.pad(k_aug, ((0, 0), (0, HEAD_DIM - k_aug.shape[-1])))
    kz = jnp.zeros_like(k_aug)
    tk_t = jnp.concatenate([kz, k_aug, k_aug, kz], axis=-1)
    return tq_t.astype(BF16), tk_t.astype(BF16)


def kernel(x, even_w_in, even_b_f, even_sinks, odd_w_in, odd_q_norm, odd_kv_norm, odd_w_uq, odd_w_ukv,
           w_o, ln_g, ln_b, router_w, router_b, moe_w_gate, moe_w_up, moe_w_down):
    b, s, d = x.shape
    n = b * s
    depth = w_o.shape[0]
    alpha = (2 * depth) ** 0.25
    tm = min(512, s)
    tq = min(256, s)
    tm_e = min(256, n)
    tm_c = min(256, n)
    att_scale = HEAD_DIM ** -0.5
    xt = x.reshape(n, d)
    rwt = router_w.T
    rb = router_b.reshape(N_EXPERTS, 1)

    for l in range(depth):
        j = l // 2
        if l % 2 == 0:
            qa, ka, va, qb, kb, vb, fg = _split_cols(
                even_w_in[j], (512, 128, 128, 512, 512, 512, B_HEADS))
            w_cat = jnp.concatenate([qa * (att_scale * LOG2E), _dup64(ka), _dup64(va), qb * (att_scale * LOG2E), kb, vb,
                                     _pad_cols(fg, LANE)], axis=1).astype(BF16)
            bf_pad = _pad_cols(even_b_f[j].reshape(1, B_HEADS), LANE)
            qa_, ka_, va_, qf, kf, vf = _even_proj(xt, w_cat, bf_pad, s, tm)
            r3 = lambda a: a.reshape(b, s, a.shape[1])
            out_a = _swa(even_sinks[j], r3(qa_), r3(ka_), r3(va_))
            out_b = _flash(r3(qf), r3(kf), r3(vf), tq=tq, nh=8, name="fox_attn")
        else:
            (qc, kc, vc, qi, ki, wi, cq, ckv, kr) = _split_cols(
                odd_w_in[j], (512, 64, 64, IDX_HEADS * IDX_DIM, IDX_DIM, IDX_HEADS,
                              MLA_Q_RANK, MLA_KV_RANK, MLA_ROPE))
            w_cat = jnp.concatenate([
                qc * (att_scale * LOG2E), _dup64(kc), _dup64(vc), qi * (IDX_DIM ** -0.5), _dup64(ki),
                _pad_cols(wi * (IDX_HEADS ** -0.5), LANE), cq, ckv,
                _rope_block(kr), _rope_block(_rope_partner(kr))], axis=1).astype(BF16)
            dq = MLA_NOPE + MLA_ROPE
            wuq = odd_w_uq[j].reshape(MLA_Q_RANK, D_HEADS, dq)
            pad_q = jnp.zeros((MLA_Q_RANK, D_HEADS, LANE - dq), F32)
            wqa = jnp.concatenate([wuq, pad_q], axis=2).reshape(MLA_Q_RANK, D_HEADS * LANE).astype(BF16)
            wqb = jnp.concatenate([jnp.zeros((MLA_Q_RANK, D_HEADS, MLA_NOPE), F32),
                                   _rope_partner(wuq[..., MLA_NOPE:]), pad_q],
                                  axis=2).reshape(MLA_Q_RANK, D_HEADS * LANE).astype(BF16)
            wukv = odd_w_ukv[j].reshape(MLA_KV_RANK, D_HEADS, MLA_NOPE + MLA_V)
            wk = jnp.concatenate([wukv[..., :MLA_NOPE], jnp.zeros((MLA_KV_RANK, D_HEADS, LANE - MLA_NOPE), F32)],
                                 axis=2).reshape(MLA_KV_RANK, D_HEADS * LANE).astype(BF16)
            wv = wukv[..., MLA_NOPE:].reshape(MLA_KV_RANK, D_HEADS * MLA_V).astype(BF16)
            half = MLA_ROPE // 2
            inv = ROPE_THETA ** (-jnp.arange(half, dtype=F32) / half)
            ang = jnp.arange(s, dtype=F32)[:, None] * inv[None, :]
            cos_h, sin_h = jnp.cos(ang), jnp.sin(ang)
            pad_t = jnp.zeros((s, LANE - dq), F32)
            cos_t = jnp.concatenate([jnp.ones((s, MLA_NOPE), F32), cos_h, cos_h, pad_t], axis=1)
            sin_t = jnp.concatenate([jnp.zeros((s, MLA_NOPE), F32), sin_h, sin_h, pad_t], axis=1)
            tq_t, tk_t = _alibi_tables(s)
            qc_, kc_, vc_, qi_, ki_, wi_, q8, k8, v8 = _odd_proj(
                xt, w_cat, odd_q_norm[j].reshape(1, -1), odd_kv_norm[j].reshape(1, -1),
                wqa, wqb, wk, wv, cos_t, sin_t, tq_t, tk_t, s, tm)
            r3 = lambda a: a.reshape(b, s, a.shape[1])
            out_a = _dsa(r3(qc_), r3(kc_), r3(vc_), r3(qi_), r3(ki_), r3(wi_).transpose(0, 2, 1))
            out_b = _flash(r3(q8), r3(k8), r3(v8), tq=tq, nh=8, name="mla_attn")
        half_w = out_a.shape[2]
        wo = w_o[l].astype(BF16)
        x1, route_t = _outproj(out_a.reshape(n, half_w), out_b.reshape(n, -1), xt,
                               wo[:half_w], wo[half_w:], ln_g[l, 0].reshape(1, d), ln_b[l, 0].reshape(1, d),
                               rwt, rb, alpha, tm)
        xt = _moe_layer(x1, route_t, moe_w_gate[l].astype(BF16), moe_w_up[l].astype(BF16),
                        moe_w_down[l].astype(BF16), ln_g[l, 1].reshape(1, d), ln_b[l, 1].reshape(1, d),
                        alpha, tm_e, tm_c)
    return xt.reshape(b, s, d)
```

```python
import functools
import math

import jax
import jax.numpy as jnp
from jax import lax
from jax.experimental import pallas as pl
from jax.experimental.pallas import tpu as pltpu

F32 = jnp.float32
BF16 = jnp.bfloat16
I32 = jnp.int32

LANE = 128
HEAD_DIM = 64
BLOCK = 128
A_HEADS, A_KV_HEADS, WINDOW = 8, 2, 128
B_HEADS = 8
C_HEADS, IDX_HEADS, IDX_DIM, DSA_TOPK_MAX = 8, 4, 64, 256
D_HEADS, MLA_Q_RANK, MLA_KV_RANK, MLA_NOPE, MLA_ROPE, MLA_V = 8, 256, 128, 64, 32, 64
ROPE_THETA = 10000.0
N_EXPERTS, N_GROUPS = 16, 4
EXPERTS_PER_GROUP = N_EXPERTS // N_GROUPS
LN_EPS, RMS_EPS = 1e-5, 1e-6
NEG_INF = float("-inf")
LOG2E = math.log2(math.e)
VMEM_LIMIT = 56 * 1024 * 1024


def _cparams(sem):
    return pltpu.CompilerParams(dimension_semantics=sem, vmem_limit_bytes=VMEM_LIMIT)


def _dot(a, b):
    return jnp.dot(a, b, preferred_element_type=F32)


def _dot_nt(a, b):
    return lax.dot_general(a, b, (((1,), (1,)), ((), ())), preferred_element_type=F32)


def _lane_lo(shape):
    return lax.broadcasted_iota(I32, shape, len(shape) - 1) % LANE < HEAD_DIM


def _layer_norm(y, g, b):
    mu = jnp.mean(y, axis=-1, keepdims=True)
    yc = y - mu
    var = jnp.mean(yc * yc, axis=-1, keepdims=True)
    return yc * lax.rsqrt(var + LN_EPS) * g + b


def _top16(v):
    bits = lax.bitcast_convert_type(v, I32) & jnp.int32(-65536)
    return lax.bitcast_convert_type(bits, F32)


def _split3(x):
    s1 = _top16(x)
    r = x - s1
    s2 = _top16(r)
    return s1, s2, r - s2


def _aug_rel(shape, h):
    lane = lax.broadcasted_iota(I32, shape, 1)
    return lane - (HEAD_DIM if h % 2 == 0 else 0)


def _even_proj_kernel(x_ref, w_ref, bf_ref, qa_ref, ka_ref, va_ref, qf_ref, kf_ref, vf_ref,
                      carry_ref, *, tiles_per_seq):
    t = pl.program_id(0)
    tm = x_ref.shape[0]
    h = _dot(x_ref[...].astype(BF16), w_ref[...])
    qa_ref[...] = h[:, 0:512].astype(BF16)
    ka_ref[...] = h[:, 512:768].astype(BF16)
    va_ref[...] = h[:, 768:1024].astype(BF16)
    z = h[:, 2560:2688] + bf_ref[...]
    logf = jnp.minimum(z, 0.0) - jnp.log1p(jnp.exp(-jnp.abs(z)))

    @pl.when(t % tiles_per_seq == 0)
    def _():
        carry_ref[...] = jnp.zeros_like(carry_ref)

    row = lax.broadcasted_iota(I32, (tm, tm), 0)
    col = lax.broadcasted_iota(I32, (tm, tm), 1)
    tri = jnp.where(row >= col, 1.0, 0.0).astype(BF16)
    c = carry_ref[...]
    for part in _split3(logf):
        c = c + _dot(tri, part.astype(BF16))
    carry_ref[...] = c[tm - 1:tm, :]

    s1, s2, s3 = _split3(c * LOG2E)
    lo = _lane_lo((tm, LANE))
    for hd in range(B_HEADS):
        own = lo if hd % 2 == 0 else ~lo
        rel = _aug_rel((tm, LANE), hd)
        b1, b2, b3 = s1[:, hd:hd + 1], s2[:, hd:hd + 1], s3[:, hd:hd + 1]
        q_aug = jnp.where(rel == 3, b1, jnp.where(rel == 4, b2, jnp.where(rel == 5, b3,
                          jnp.where((rel >= 0) & (rel < 3), -1.0, 0.0))))
        k_aug = jnp.where(rel == 0, b1, jnp.where(rel == 1, b2, jnp.where(rel == 2, b3,
                          jnp.where((rel >= 3) & (rel < 6), 1.0, 0.0))))
        pj = (hd // 2) * LANE
        sl = slice(hd * LANE, (hd + 1) * LANE)
        qf_ref[:, sl] = jnp.where(own, h[:, 1024 + pj:1024 + pj + LANE], q_aug).astype(BF16)
        kf_ref[:, sl] = jnp.where(own, h[:, 1536 + pj:1536 + pj + LANE], k_aug).astype(BF16)
        vf_ref[:, sl] = jnp.where(own, h[:, 2048 + pj:2048 + pj + LANE], 1.0).astype(BF16)


def _even_proj(xt, w_cat, bf_pad, seq, tm):
    n, d = xt.shape
    wcols = w_cat.shape[1]
    widths = (512, 256, 256, 1024, 1024, 1024)
    out_shape = [jax.ShapeDtypeStruct((n, w), BF16) for w in widths]
    out_specs = [pl.BlockSpec((tm, w), lambda i: (i, 0)) for w in widths]
    return pl.pallas_call(
        functools.partial(_even_proj_kernel, tiles_per_seq=seq // tm),
        out_shape=out_shape,
        grid=(n // tm,),
        in_specs=[pl.BlockSpec((tm, d), lambda i: (i, 0)),
                  pl.BlockSpec((d, wcols), lambda i: (0, 0)),
                  pl.BlockSpec((1, LANE), lambda i: (0, 0))],
        out_specs=out_specs,
        scratch_shapes=[pltpu.VMEM((1, LANE), F32)],
        compiler_params=_cparams(("arbitrary",)),
        name="even_proj",
    )(xt, w_cat, bf_pad)


def _swa_kernel(sink_ref, q_ref, kp_ref, kc_ref, vp_ref, vc_ref, o_ref):
    i = pl.program_id(1)
    group = A_HEADS // A_KV_HEADS
    tl = lax.broadcasted_iota(I32, (BLOCK, 2 * BLOCK), 0)
    sl = lax.broadcasted_iota(I32, (BLOCK, 2 * BLOCK), 1)
    dist = BLOCK + tl - sl
    valid = (dist >= 0) & (dist < WINDOW) & ((sl >= BLOCK) | (i > 0))
    distf = dist.astype(F32)
    lo = _lane_lo((BLOCK, LANE))

    def band(prev_ref, cur_ref, g):
        return jnp.concatenate([prev_ref[:, g * LANE:(g + 1) * LANE], cur_ref[:, g * LANE:(g + 1) * LANE]], axis=0)

    stacked = []
    for g in range(A_KV_HEADS):
        qs = []
        for h in range(g * group, (g + 1) * group):
            qp = q_ref[:, (h // 2) * LANE:(h // 2 + 1) * LANE]
            qs.append(jnp.where(lo if h % 2 == 0 else ~lo, qp, jnp.zeros_like(qp)))
        stacked.append(_dot_nt(jnp.concatenate(qs, axis=0), band(kp_ref, kc_ref, g)))
    ps, denoms = [], []
    for h in range(A_HEADS):
        r = h % group
        slope = 2.0 ** (-8.0 * (h + 1) / A_HEADS) * LOG2E
        logits = jnp.where(valid, stacked[h // group][r * BLOCK:(r + 1) * BLOCK] - slope * distf, NEG_INF)
        sink = sink_ref[h] * LOG2E
        m = jnp.maximum(jnp.max(logits, axis=-1, keepdims=True), sink)
        e = jnp.exp2(logits - m)
        denoms.append(jnp.sum(e, axis=-1, keepdims=True) + jnp.exp2(sink - m))
        ps.append(e.astype(BF16))
    outs = []
    for g in range(A_KV_HEADS):
        pv = _dot(jnp.concatenate(ps[g * group:(g + 1) * group], axis=0), band(vp_ref, vc_ref, g))
        for r in range(group):
            outs.append(pv[r * BLOCK:(r + 1) * BLOCK] / denoms[g * group + r])
    for j in range(A_HEADS // 2):
        o_ref[:, j * LANE:(j + 1) * LANE] = jnp.where(lo, outs[2 * j], outs[2 * j + 1]).astype(BF16)


def _swa(sinks, qa, ka, va):
    b, s, _ = qa.shape
    nb = s // BLOCK
    prev = lambda bi, i: (bi, jnp.maximum(i - 1, 0), 0)
    cur = lambda bi, i: (bi, i, 0)
    return pl.pallas_call(
        _swa_kernel,
        out_shape=jax.ShapeDtypeStruct((b, s, A_HEADS * HEAD_DIM), BF16),
        grid=(b, nb),
        in_specs=[pl.BlockSpec(memory_space=pltpu.SMEM),
                  pl.BlockSpec((None, BLOCK, 512), cur),
                  pl.BlockSpec((None, BLOCK, 256), prev),
                  pl.BlockSpec((None, BLOCK, 256), cur),
                  pl.BlockSpec((None, BLOCK, 256), prev),
                  pl.BlockSpec((None, BLOCK, 256), cur)],
        out_specs=pl.BlockSpec((None, BLOCK, 512), cur),
        compiler_params=_cparams(("parallel", "arbitrary")),
        name="swa_attn",
    )(sinks, qa, ka, ka, va, va)


def _normalize_pairs(accs, o_ref):
    lo = _lane_lo(accs[0].shape)
    outs = [a / pltpu.roll(a, HEAD_DIM, 1) for a in accs]
    for j in range(len(accs) // 2):
        o_ref[:, j * LANE:(j + 1) * LANE] = jnp.where(lo, outs[2 * j], outs[2 * j + 1]).astype(BF16)


def _online_softmax_chains(qs, ks, vs, carry, bias=None, masks=None, guard_empty=False):
    n = len(qs)
    ss = [_dot_nt(qs[i], ks[i]) for i in range(n)]
    if bias is not None:
        ss = [s + bias for s in ss]
    if masks is not None:
        ss = [jnp.where(masks[i], ss[i], NEG_INF) for i in range(n)]
    ms, ps, alphas = [], [], []
    for i in range(n):
        m = carry[i][0]
        m_new = jnp.maximum(m, jnp.max(ss[i], axis=-1, keepdims=True))
        m_ref = jnp.where(m_new == NEG_INF, 0.0, m_new) if guard_empty else m_new
        ps.append(jnp.exp2(ss[i] - m_ref).astype(BF16))
        alphas.append(jnp.exp2(m - m_ref))
        ms.append(m_new)
    pvs = [_dot(ps[i], vs[i]) for i in range(n)]
    return tuple((ms[i], alphas[i] * carry[i][1] + pvs[i]) for i in range(n))


def _flash_kernel(q_ref, k_ref, v_ref, o_ref, *, tq, nh):
    qi = pl.program_id(2)
    nslab = tq // BLOCK
    chains = [(hh, r) for hh in range(nh) for r in range(nslab)]
    row = lax.broadcasted_iota(I32, (BLOCK, tq), 0)
    col = lax.broadcasted_iota(I32, (BLOCK, tq), 1)

    def lanes(hh):
        return slice(hh * LANE, (hh + 1) * LANE)

    def step(kb, carry, diagonal):
        start = pl.multiple_of(kb * tq, tq)
        qs = [q_ref[r * BLOCK:(r + 1) * BLOCK, lanes(hh)] for hh, r in chains]
        ks = [k_ref[pl.ds(start, tq), lanes(hh)] for hh, r in chains]
        vs = [v_ref[pl.ds(start, tq), lanes(hh)] for hh, r in chains]
        masks = [col <= row + r * BLOCK for hh, r in chains] if diagonal else None
        return _online_softmax_chains(qs, ks, vs, carry, masks=masks)

    init = tuple((jnp.full((BLOCK, 1), NEG_INF, F32), jnp.zeros((BLOCK, LANE), F32))
                 for _ in range(nh * nslab))
    carry = lax.fori_loop(0, qi, lambda kb, c: step(kb, c, False), init)
    carry = step(qi, carry, True)
    accs = [jnp.concatenate([carry[hh * nslab + r][1] for r in range(nslab)], axis=0) for hh in range(nh)]
    _normalize_pairs(accs, o_ref)


def _flash(q, k, v, *, tq, nh, name):
    b, s, w = v.shape
    return pl.pallas_call(
        functools.partial(_flash_kernel, tq=tq, nh=nh),
        out_shape=jax.ShapeDtypeStruct((b, s, w // 2), BF16),
        grid=(b, w // (nh * LANE), s // tq),
        in_specs=[pl.BlockSpec((None, tq, nh * LANE), lambda bi, j, i: (bi, i, j)),
                  pl.BlockSpec((None, s, nh * LANE), lambda bi, j, i: (bi, 0, j)),
                  pl.BlockSpec((None, s, nh * LANE), lambda bi, j, i: (bi, 0, j))],
        out_specs=pl.BlockSpec((None, tq, nh * LANE // 2), lambda bi, j, i: (bi, i, j)),
        compiler_params=_cparams(("parallel", "parallel", "arbitrary")),
        name=name,
    )(q, k, v)


def _route_rows(aff, sel):
    gs = []
    for g in range(N_GROUPS):
        r = sel[g * EXPERTS_PER_GROUP:(g + 1) * EXPERTS_PER_GROUP]
        best = None
        for a in range(EXPERTS_PER_GROUP):
            for c in range(a + 1, EXPERTS_PER_GROUP):
                pair = r[a] + r[c]
                best = pair if best is None else jnp.maximum(best, pair)
        gs.append(best)
    gmax = functools.reduce(jnp.maximum, gs)
    gidx = jnp.full_like(gmax, float(N_GROUPS - 1))
    for g in range(N_GROUPS - 2, -1, -1):
        gidx = jnp.where(gs[g] == gmax, float(g), gidx)

    def pick(rows, k):
        out = rows[(N_GROUPS - 1) * EXPERTS_PER_GROUP + k]
        for g in range(N_GROUPS - 2, -1, -1):
            out = jnp.where(gidx == float(g), rows[g * EXPERTS_PER_GROUP + k], out)
        return out

    s = [pick(sel, k) for k in range(EXPERTS_PER_GROUP)]
    a = [pick(aff, k) for k in range(EXPERTS_PER_GROUP)]

    def first_max(vals):
        mx = functools.reduce(jnp.maximum, vals)
        idx = jnp.full_like(mx, float(len(vals) - 1))
        for k in range(len(vals) - 2, -1, -1):
            idx = jnp.where(vals[k] == mx, float(k), idx)
        return idx

    def take(vals, idx):
        out = vals[-1]
        for k in range(len(vals) - 2, -1, -1):
            out = jnp.where(idx == float(k), vals[k], out)
        return out

    i1 = first_max(s)
    s2 = [jnp.where(i1 == float(k), NEG_INF, s[k]) for k in range(EXPERTS_PER_GROUP)]
    i2 = first_max(s2)
    a1, a2 = take(a, i1), take(a, i2)
    tot = a1 + a2
    base = gidx * float(EXPERTS_PER_GROUP)
    return base + i1, base + i2, a1 / tot, a2 / tot


def _outproj_kernel(a_ref, b_ref, x_ref, wa_ref, wb_ref, g_ref, be_ref, rwt_ref, rb_ref,
                    x1_ref, route_ref, *, alpha):
    d = x_ref.shape[1]
    tm = x_ref.shape[0]
    mix = _dot(a_ref[...], wa_ref[...]) + _dot(b_ref[...], wb_ref[...])
    x1 = _layer_norm(alpha * x_ref[...] + mix, g_ref[...], be_ref[...])
    x1_ref[:, 0:d] = x1
    x_hi = x1.astype(BF16)
    x_lo = (x1 - x_hi.astype(F32)).astype(BF16)
    w = rwt_ref[...]
    w_hi = w.astype(BF16)
    w_lo = (w - w_hi.astype(F32)).astype(BF16)
    z_hi = _dot_nt(jnp.concatenate([w_hi, w_lo], axis=0), x_hi)
    zt = z_hi[0:N_EXPERTS] + z_hi[N_EXPERTS:2 * N_EXPERTS] + _dot_nt(w_hi, x_lo)
    afft = 1.0 / (1.0 + jnp.exp(-zt))
    selt = afft + rb_ref[...]
    aff = [afft[e:e + 1, :] for e in range(N_EXPERTS)]
    sel = [selt[e:e + 1, :] for e in range(N_EXPERTS)]
    e1, e2, w1, w2 = _route_rows(aff, sel)
    rec = jnp.concatenate([e1, e2, w1, w2, jnp.zeros((LANE - 4, tm), F32)], axis=0)
    route_ref[...] = rec[0:8]
    x1_ref[:, d:d + LANE] = rec.T


def _outproj(a, b, xt, wa, wb, g, be, rwt, rb, alpha, tm):
    n, d = xt.shape
    full = lambda i: (0, 0)
    return pl.pallas_call(
        functools.partial(_outproj_kernel, alpha=alpha),
        out_shape=[jax.ShapeDtypeStruct((n, d + LANE), F32), jax.ShapeDtypeStruct((8, n), F32)],
        grid=(n // tm,),
        in_specs=[pl.BlockSpec((tm, a.shape[1]), lambda i: (i, 0)),
                  pl.BlockSpec((tm, b.shape[1]), lambda i: (i, 0)),
                  pl.BlockSpec((tm, d), lambda i: (i, 0)),
                  pl.BlockSpec(wa.shape, full), pl.BlockSpec(wb.shape, full),
                  pl.BlockSpec((1, d), full), pl.BlockSpec((1, d), full),
                  pl.BlockSpec(rwt.shape, full), pl.BlockSpec(rb.shape, full)],
        out_specs=[pl.BlockSpec((tm, d + LANE), lambda i: (i, 0)), pl.BlockSpec((8, tm), lambda i: (0, i))],
        compiler_params=_cparams(("parallel",)),
        name="outproj_ln_route",
    )(a, b, xt, wa, wb, g, be, rwt, rb)


def _moe_kernel(bgrp_ref, bbase_ref, nact_ref, src_ref,
                x_hbm, wg_ref, wu_ref, wd_ref, o_ref, xbuf0, xbuf1, sem, *, tm, d):
    i = pl.program_id(0)
    nact = nact_ref[0]
    last = src_ref.shape[0] - 1
    bufs = (xbuf0, xbuf1)

    def issue(blk, slot):
        base = bbase_ref[blk]
        for r in range(tm):
            tok = src_ref[jnp.minimum(base + r, last)]
            pltpu.make_async_copy(x_hbm.at[pl.ds(tok, 1), :], bufs[slot].at[pl.ds(r, 1), :],
                                  sem.at[slot]).start(priority=r % 2)

    def wait(slot):
        pltpu.make_async_copy(x_hbm.at[pl.ds(0, tm), :], bufs[slot], sem.at[slot]).wait()

    @pl.when((i == 0) & (nact > 0))
    def _():
        issue(0, 0)

    def step(slot):
        wait(slot)

        @pl.when(i + 1 < nact)
        def _():
            issue(i + 1, 1 - slot)

        buf = bufs[slot]
        xb = buf[:, 0:d].astype(BF16)
        rec = buf[:, d:d + LANE]
        e1, e2, w1, w2 = rec[:, 0:1], rec[:, 1:2], rec[:, 2:3], rec[:, 3:4]
        fg = wg_ref.shape[1]
        lane_expert = lax.broadcasted_iota(I32, (tm, fg), 1) // (fg // EXPERTS_PER_GROUP)
        expert = (lane_expert + bgrp_ref[i] * EXPERTS_PER_GROUP).astype(F32)
        gate_w = jnp.where(expert == e1, w1, jnp.where(expert == e2, w2, 0.0))
        gate = _dot(xb, wg_ref[...])
        up = _dot(xb, wu_ref[...])
        h = gate * (1.0 / (1.0 + jnp.exp(-gate))) * up * gate_w
        o_ref[...] = _dot(h.astype(BF16), wd_ref[...])

    for slot in range(2):
        pl.when((i < nact) & (i % 2 == slot))(functools.partial(step, slot))

    @pl.when(i >= nact)
    def _():
        o_ref[...] = jnp.zeros_like(o_ref)


def _moe_experts(x1e, wg, wu, wd, bgrp, bbase, nact, src, tm):
    n, de = x1e.shape
    d = de - LANE
    nblk = bgrp.shape[0]
    fg = wg.shape[2]
    wmap = lambda i, bgrp, *_: (bgrp[i], 0, 0)
    grid_spec = pltpu.PrefetchScalarGridSpec(
        num_scalar_prefetch=4,
        grid=(nblk,),
        in_specs=[pl.BlockSpec(memory_space=pl.ANY),
                  pl.BlockSpec((None, d, fg), wmap),
                  pl.BlockSpec((None, d, fg), wmap),
                  pl.BlockSpec((None, fg, d), wmap)],
        out_specs=pl.BlockSpec((tm, d), lambda i, *_: (i, 0)),
        scratch_shapes=[pltpu.VMEM((tm, de), F32), pltpu.VMEM((tm, de), F32), pltpu.SemaphoreType.DMA((2,))],
    )
    return pl.pallas_call(
        functools.partial(_moe_kernel, tm=tm, d=d),
        out_shape=jax.ShapeDtypeStruct((nblk * tm, d), F32),
        grid_spec=grid_spec,
        compiler_params=_cparams(("arbitrary",)),
        name="moe_experts",
    )(bgrp, bbase, nact, src, x1e, wg, wu, wd)


def _combine_kernel(p_ref, x_ref, g_ref, be_ref, o_hbm, y_ref, obuf0, obuf1, sem, *, alpha, tm, d):
    i = pl.program_id(0)
    nsteps = pl.num_programs(0)

    bufs = (obuf0, obuf1)

    def issue(blk, slot):
        base = blk * tm
        for r in range(tm):
            pltpu.make_async_copy(o_hbm.at[pl.ds(p_ref[base + r], 1), :],
                                  bufs[slot].at[pl.ds(r, 1), :], sem.at[slot]).start(priority=r % 2)

    def wait(slot):
        pltpu.make_async_copy(o_hbm.at[pl.ds(0, tm), :], bufs[slot], sem.at[slot]).wait()

    @pl.when(i == 0)
    def _():
        issue(0, 0)

    def step(slot):
        wait(slot)

        @pl.when(i + 1 < nsteps)
        def _():
            issue(i + 1, 1 - slot)

        y = alpha * x_ref[:, 0:d] + bufs[slot][...]
        y_ref[...] = _layer_norm(y, g_ref[...], be_ref[...])

    for slot in range(2):
        pl.when(i % 2 == slot)(functools.partial(step, slot))


def _combine(x1e, g, be, o, pos, alpha, tm):
    n, de = x1e.shape
    d = de - LANE
    grid_spec = pltpu.PrefetchScalarGridSpec(
        num_scalar_prefetch=1,
        grid=(n // tm,),
        in_specs=[pl.BlockSpec((tm, de), lambda i, *_: (i, 0)),
                  pl.BlockSpec((1, d), lambda i, *_: (0, 0)),
                  pl.BlockSpec((1, d), lambda i, *_: (0, 0)),
                  pl.BlockSpec(memory_space=pl.ANY)],
        out_specs=pl.BlockSpec((tm, d), lambda i, *_: (i, 0)),
        scratch_shapes=[pltpu.VMEM((tm, d), F32), pltpu.VMEM((tm, d), F32), pltpu.SemaphoreType.DMA((2,))],
    )
    return pl.pallas_call(
        functools.partial(_combine_kernel, alpha=alpha, tm=tm, d=d),
        out_shape=jax.ShapeDtypeStruct((n, d), F32),
        grid_spec=grid_spec,
        compiler_params=_cparams(("arbitrary",)),
        name="moe_combine_ln",
    )(pos, x1e, g, be, o)


def _moe_layer(x1e, route_t, wg, wu, wd, g, be, alpha, tm_e, tm_c):
    n = x1e.shape[0]
    grp = route_t[0].astype(I32) // EXPERTS_PER_GROUP
    onehot = (grp[:, None] == jnp.arange(N_GROUPS, dtype=I32)[None, :]).astype(I32)
    rank = jnp.take_along_axis(jnp.cumsum(onehot, axis=0) - onehot, grp[:, None], axis=1)[:, 0]
    counts = jnp.sum(onehot, axis=0)
    nblk_g = (counts + tm_e - 1) // tm_e
    blk_start = jnp.cumsum(nblk_g) - nblk_g
    seg_start = jnp.cumsum(counts) - counts
    pos = (blk_start[grp] * tm_e + rank).astype(I32)
    src = jnp.argsort(grp, stable=True).astype(I32)
    nblk = n // tm_e + N_GROUPS
    blk = jnp.arange(nblk, dtype=I32)
    blk_end = jnp.cumsum(nblk_g)
    bgrp = jnp.minimum(jnp.sum((blk[:, None] >= blk_end[None, :]).astype(I32), axis=1), N_GROUPS - 1)
    bbase = (seg_start[bgrp] + (blk - blk_start[bgrp]) * tm_e).astype(I32)
    nact = jnp.sum(nblk_g).astype(I32).reshape(1)
    o = _moe_experts(x1e, wg, wu, wd, bgrp, bbase, nact, src, tm_e)
    return _combine(x1e, g, be, o, pos, alpha, tm_c)


def _rms(x, g):
    return x * lax.rsqrt(jnp.mean(x * x, axis=-1, keepdims=True) + RMS_EPS) * g

ODD_OFF = dict(qc=0, kc=512, vc=640, qi=768, ki=1024, wi=1152, cq=1280, ckv=1536, kra=1664, krb=1792, end=1920)


def _odd_proj_kernel(x_ref, w_ref, qng_ref, kvng_ref, wqa_ref, wqb_ref, wk_ref, wv_ref, cos_ref, sin_ref,
                     tq_ref, tk_ref,
                     qc_ref, kc_ref, vc_ref, qi_ref, ki_ref, wi_ref, q8_ref, k8_ref, v8_ref, *, q_scale):
    o = ODD_OFF
    tm = x_ref.shape[0]
    lo = _lane_lo((tm, LANE))
    h = _dot(x_ref[...].astype(BF16), w_ref[...])
    kdup = h[:, o["kc"]:o["vc"]]
    vdup = h[:, o["vc"]:o["qi"]]
    kc_ref[:, 0:LANE] = jnp.where(lo, kdup, tk_ref[:, 0:LANE].astype(F32)).astype(BF16)
    kc_ref[:, LANE:2 * LANE] = jnp.where(lo, tk_ref[:, LANE:2 * LANE].astype(F32), kdup).astype(BF16)
    vc_ref[:, 0:LANE] = jnp.where(lo, vdup, 1.0).astype(BF16)
    vc_ref[:, LANE:2 * LANE] = jnp.where(lo, 1.0, vdup).astype(BF16)
    qi_ref[...] = h[:, o["qi"]:o["ki"]].astype(BF16)
    ki_ref[...] = h[:, o["ki"]:o["wi"]].astype(BF16)
    wi_ref[...] = h[:, o["wi"]:o["wi"] + IDX_HEADS]
    cos = cos_ref[...]
    sin = sin_ref[...]
    cqn = _rms(h[:, o["cq"]:o["ckv"]], qng_ref[...]).astype(BF16)
    qa = _dot(cqn, wqa_ref[...])
    qb = _dot(cqn, wqb_ref[...])
    ckvn = _rms(h[:, o["ckv"]:o["kra"]], kvng_ref[...]).astype(BF16)
    kn = _dot(ckvn, wk_ref[...])
    v8 = _dot(ckvn, wv_ref[...])
    kr = h[:, o["kra"]:o["krb"]] * cos + h[:, o["krb"]:o["end"]] * sin
    for hd in range(D_HEADS):
        sl = slice(hd * LANE, (hd + 1) * LANE)
        pj = slice((hd // 2) * LANE, (hd // 2 + 1) * LANE)
        own = lo if hd % 2 == 0 else ~lo
        qc_ref[:, sl] = jnp.where(own, h[:, pj], tq_ref[:, sl].astype(F32)).astype(BF16)
        q8_ref[:, sl] = ((qa[:, sl] * cos + qb[:, sl] * sin) * q_scale).astype(BF16)
        k8_ref[:, sl] = (kn[:, sl] + kr).astype(BF16)
        v8_ref[:, sl] = jnp.where(own, v8[:, pj], 1.0).astype(BF16)


def _odd_proj(xt, w_cat, qng, kvng, wqa, wqb, wk, wv, cos_t, sin_t, tq_t, tk_t, seq, tm):
    n, d = xt.shape
    full = lambda i: (0, 0)
    tps = seq // tm
    pos = lambda i: (i % tps, 0)
    widths = (1024, 256, 256, 256, 128)
    out_shape = [jax.ShapeDtypeStruct((n, w), BF16) for w in widths]
    out_specs = [pl.BlockSpec((tm, w), lambda i: (i, 0)) for w in widths]
    out_shape.append(jax.ShapeDtypeStruct((n, IDX_HEADS), F32))
    out_specs.append(pl.BlockSpec((tm, IDX_HEADS), lambda i: (i, 0)))
    for w in (1024, 1024, 1024):
        out_shape.append(jax.ShapeDtypeStruct((n, w), BF16))
        out_specs.append(pl.BlockSpec((tm, w), lambda i: (i, 0)))
    return pl.pallas_call(
        functools.partial(_odd_proj_kernel, q_scale=(MLA_NOPE + MLA_ROPE) ** -0.5 * LOG2E),
        out_shape=out_shape,
        grid=(n // tm,),
        in_specs=[pl.BlockSpec((tm, d), lambda i: (i, 0)),
                  pl.BlockSpec(w_cat.shape, full),
                  pl.BlockSpec(qng.shape, full), pl.BlockSpec(kvng.shape, full),
                  pl.BlockSpec(wqa.shape, full), pl.BlockSpec(wqb.shape, full),
                  pl.BlockSpec(wk.shape, full), pl.BlockSpec(wv.shape, full),
                  pl.BlockSpec((tm, LANE), pos), pl.BlockSpec((tm, LANE), pos),
                  pl.BlockSpec((tm, C_HEADS * LANE), pos), pl.BlockSpec((tm, 2 * LANE), pos)],
        out_specs=out_specs,
        compiler_params=_cparams(("parallel",)),
        name="odd_proj",
    )(xt, w_cat, qng, kvng, wqa, wqb, wk, wv, cos_t, sin_t, tq_t, tk_t)


def _dsa_kernel(q_ref, k_ref, v_ref, qi_ref, ki_ref, wt_ref, o_ref, sc_ref, mb_ref, *, top_k, chunk):
    i = pl.program_id(1)
    nc = (i * BLOCK + BLOCK + chunk - 1) // chunk
    lo = _lane_lo((BLOCK, LANE))
    s_row2 = lax.broadcasted_iota(I32, (2 * chunk, BLOCK), 0)
    t_lane2 = i * BLOCK + lax.broadcasted_iota(I32, (2 * chunk, BLOCK), 1)

    def chunk_start(c):
        return pl.multiple_of(c * chunk, chunk)

    qidx = []
    for h in range(IDX_HEADS):
        qp = qi_ref[:, (h // 2) * LANE:(h // 2 + 1) * LANE]
        qidx.append(jnp.where(lo if h % 2 == 0 else ~lo, qp, jnp.zeros_like(qp)))
    q_stack = jnp.concatenate(qidx, axis=0)
    w = wt_ref[...]

    npair = (nc + 1) // 2

    def pair_start(cp):
        return pl.multiple_of(cp * (2 * chunk), 2 * chunk)

    def score_pair(cp, _):
        start = pair_start(cp)
        d = _dot_nt(ki_ref[pl.ds(start, 2 * chunk), :], q_stack)
        sc = w[0:1, :] * jnp.maximum(d[:, 0:BLOCK], 0.0)
        for h in range(1, IDX_HEADS):
            sc = sc + w[h:h + 1, :] * jnp.maximum(d[:, h * BLOCK:(h + 1) * BLOCK], 0.0)
        sc = jnp.where(sc == 0.0, 0.0, sc)
        sc_ref[pl.ds(start, 2 * chunk), :] = jnp.where(start + s_row2 <= t_lane2, sc, NEG_INF)
        return 0

    lax.fori_loop(0, npair, score_pair, 0)

    nacc = 4 * 8

    def count(trial, strict):
        def body(cp, acc):
            sc = sc_ref[pl.ds(pair_start(cp), 2 * chunk), :]
            hit = (sc > trial) if strict else (sc >= trial)
            return acc + jnp.sum(jnp.where(hit, 1.0, 0.0).reshape(2 * chunk // nacc, nacc, BLOCK), axis=0)

        acc = lax.fori_loop(0, npair, body, jnp.zeros((nacc, BLOCK), F32))
        return jnp.sum(acc, axis=0, keepdims=True)

    def decode(key):
        return lax.bitcast_convert_type(jnp.where(key < 0, key ^ jnp.int32(0x7FFFFFFF), key), F32)

    kf = float(top_k)
    key_neg_inf = jnp.int32(-2139095041)
    cand = jnp.where(count(jnp.zeros((1, BLOCK), F32), False) >= kf, 0, jnp.int32(-2 ** 31)).astype(I32)

    def bit_step(j, cand):
        trial = cand + lax.shift_left(jnp.int32(1), 30 - j)
        ok = (trial <= key_neg_inf) | (count(decode(trial), False) >= kf)
        return jnp.where(ok, trial, cand)

    thr = decode(lax.fori_loop(0, 31, bit_step, cand))
    need = kf - count(thr, True)

    r0 = lax.broadcasted_iota(I32, (chunk, chunk), 0)
    r1 = lax.broadcasted_iota(I32, (chunk, chunk), 1)
    below = jnp.where(r1 < r0, 1.0, 0.0).astype(BF16)
    need = jnp.where(thr == NEG_INF, 0.0, need)

    def mask_pair(cp, offset):
        start = pair_start(cp)
        sc = sc_ref[pl.ds(start, 2 * chunk), :]
        eq = sc == thr
        eqf = jnp.where(eq, 1.0, 0.0)
        eqb = eqf.astype(BF16)
        first = jnp.sum(eqf[0:chunk], axis=0, keepdims=True)
        prefix = jnp.concatenate([_dot(below, eqb[0:chunk]) + offset,
                                  _dot(below, eqb[chunk:2 * chunk]) + (offset + first)], axis=0)
        bias = jnp.where(sc > thr, 0.0, jnp.where(eq, jnp.where(prefix < need, 0.0, NEG_INF), NEG_INF))
        mb_ref[:, pl.ds(start, 2 * chunk)] = bias.T
        return offset + first + jnp.sum(eqf[chunk:2 * chunk], axis=0, keepdims=True)

    lax.fori_loop(0, npair, mask_pair, jnp.zeros((1, BLOCK), F32))

    def att_chunk(c, carry):
        start = chunk_start(c)
        qs = [q_ref[:, h * LANE:(h + 1) * LANE] for h in range(C_HEADS)]
        ks = [k_ref[pl.ds(start, chunk), (h % 2) * LANE:(h % 2 + 1) * LANE] for h in range(C_HEADS)]
        vs = [v_ref[pl.ds(start, chunk), (h % 2) * LANE:(h % 2 + 1) * LANE] for h in range(C_HEADS)]
        return _online_softmax_chains(qs, ks, vs, carry, bias=mb_ref[:, pl.ds(start, chunk)], guard_empty=True)

    init = tuple((jnp.full((BLOCK, 1), NEG_INF, F32), jnp.zeros((BLOCK, LANE), F32)) for _ in range(C_HEADS))
    carry = lax.fori_loop(0, nc, att_chunk, init)
    _normalize_pairs([c[1] for c in carry], o_ref)


def _dsa(qc, kc, vc, qi, ki, wit):
    b, s, _ = qc.shape
    top_k = min(DSA_TOPK_MAX, s // 4)
    chunk = 256
    assert s % (2 * chunk) == 0 and top_k <= chunk
    blk = lambda bi, i: (bi, i, 0)
    seq = lambda bi, i: (bi, 0, 0)
    return pl.pallas_call(
        functools.partial(_dsa_kernel, top_k=top_k, chunk=chunk),
        out_shape=jax.ShapeDtypeStruct((b, s, C_HEADS * HEAD_DIM), BF16),
        grid=(b, s // BLOCK),
        in_specs=[pl.BlockSpec((None, BLOCK, C_HEADS * LANE), blk),
                  pl.BlockSpec((None, s, 2 * LANE), seq),
                  pl.BlockSpec((None, s, 2 * LANE), seq),
                  pl.BlockSpec((None, BLOCK, 256), blk),
                  pl.BlockSpec((None, s, LANE), seq),
                  pl.BlockSpec((None, IDX_HEADS, BLOCK), lambda bi, i: (bi, 0, i))],
        out_specs=pl.BlockSpec((None, BLOCK, 512), blk),
        scratch_shapes=[pltpu.VMEM((s, BLOCK), F32), pltpu.VMEM((BLOCK, s), F32)],
        compiler_params=_cparams(("parallel", "arbitrary")),
        name="dsa_attn",
    )(qc, kc, vc, qi, ki, wit)


def _dup64(w):
    d, c = w.shape
    w = w.reshape(d, c // HEAD_DIM, 1, HEAD_DIM)
    return jnp.broadcast_to(w, (d, c // HEAD_DIM, 2, HEAD_DIM)).reshape(d, 2 * c)


def _pad_cols(w, width):
    return jnp.pad(w, ((0, 0), (0, width - w.shape[1])))


def _split_cols(w, sizes):
    out, start = [], 0
    for n in sizes:
        out.append(w[:, start:start + n])
        start += n
    return out


def _rope_partner(w):
    half = MLA_ROPE // 2
    return jnp.concatenate([-w[..., half:], w[..., :half]], axis=-1)


def _rope_block(w_rope):
    d = w_rope.shape[0]
    return jnp.concatenate([jnp.zeros((d, MLA_NOPE), F32), w_rope,
                            jnp.zeros((d, LANE - MLA_NOPE - MLA_ROPE), F32)], axis=1)


def _alibi_tables(s):
    slopes = 2.0 ** (-8.0 * jnp.arange(1, C_HEADS + 1, dtype=F32) / C_HEADS)
    a = slopes * LOG2E
    pos = jnp.arange(s, dtype=F32)
    a_sp = _split3(a)
    r_sp = _split3(-(a[None, :] * pos[:, None]))
    ones = jnp.ones((s, C_HEADS), F32)
    q_terms = [128.0 * t * ones for t in a_sp] + [t * ones for t in a_sp] + list(r_sp)
    q_aug = jnp.stack(q_terms, axis=-1)
    q_aug = jnp.pad(q_aug, ((0, 0), (0, 0), (0, HEAD_DIM - q_aug.shape[-1])))
    zeros = jnp.zeros_like(q_aug)
    even = jnp.concatenate([zeros, q_aug], axis=-1)
    odd = jnp.concatenate([q_aug, zeros], axis=-1)
    parity = (jnp.arange(C_HEADS) % 2 == 0)[None, :, None]
    tq_t = jnp.where(parity, even, odd).reshape(s, C_HEADS * LANE)
    s_hi = jnp.floor(pos / LANE)
    s_lo = pos - LANE * s_hi
    k_aug = jnp.stack([s_hi] * 3 + [s_lo] * 3 + [jnp.ones_like(pos)] * 3, axis=-1)
    k_aug = jnp.pad(k_aug, ((0, 0), (0, HEAD_DIM - k_aug.shape[-1])))
    kz = jnp.zeros_like(k_aug)
    tk_t = jnp.concatenate([kz, k_aug, k_aug, kz], axis=-1)
    return tq_t.astype(BF16), tk_t.astype(BF16)


def kernel(x, even_w_in, even_b_f, even_sinks, odd_w_in, odd_q_norm, odd_kv_norm, odd_w_uq, odd_w_ukv,
           w_o, ln_g, ln_b, router_w, router_b, moe_w_gate, moe_w_up, moe_w_down):
    b, s, d = x.shape
    n = b * s
    depth = w_o.shape[0]
    alpha = (2 * depth) ** 0.25
    tm = min(512, s)
    tq = min(256, s)
    tm_e = min(256, n)
    tm_c = min(256, n)
    att_scale = HEAD_DIM ** -0.5
    xt = x.reshape(n, d)
    rwt = router_w.T
    rb = router_b.reshape(N_EXPERTS, 1)

    for l in range(depth):
        j = l // 2
        if l % 2 == 0:
            qa, ka, va, qb, kb, vb, fg = _split_cols(
                even_w_in[j], (512, 128, 128, 512, 512, 512, B_HEADS))
            w_cat = jnp.concatenate([qa * (att_scale * LOG2E), _dup64(ka), _dup64(va), qb * (att_scale * LOG2E), kb, vb,
                                     _pad_cols(fg, LANE)], axis=1).astype(BF16)
            bf_pad = _pad_cols(even_b_f[j].reshape(1, B_HEADS), LANE)
            qa_, ka_, va_, qf, kf, vf = _even_proj(xt, w_cat, bf_pad, s, tm)
            r3 = lambda a: a.reshape(b, s, a.shape[1])
            out_a = _swa(even_sinks[j], r3(qa_), r3(ka_), r3(va_))
            out_b = _flash(r3(qf), r3(kf), r3(vf), tq=tq, nh=8, name="fox_attn")
        else:
            (qc, kc, vc, qi, ki, wi, cq, ckv, kr) = _split_cols(
                odd_w_in[j], (512, 64, 64, IDX_HEADS * IDX_DIM, IDX_DIM, IDX_HEADS,
                              MLA_Q_RANK, MLA_KV_RANK, MLA_ROPE))
            w_cat = jnp.concatenate([
                qc * (att_scale * LOG2E), _dup64(kc), _dup64(vc), qi * (IDX_DIM ** -0.5), _dup64(ki),
                _pad_cols(wi * (IDX_HEADS ** -0.5), LANE), cq, ckv,
                _rope_block(kr), _rope_block(_rope_partner(kr))], axis=1).astype(BF16)
            dq = MLA_NOPE + MLA_ROPE
            wuq = odd_w_uq[j].reshape(MLA_Q_RANK, D_HEADS, dq)
            pad_q = jnp.zeros((MLA_Q_RANK, D_HEADS, LANE - dq), F32)
            wqa = jnp.concatenate([wuq, pad_q], axis=2).reshape(MLA_Q_RANK, D_HEADS * LANE).astype(BF16)
            wqb = jnp.concatenate([jnp.zeros((MLA_Q_RANK, D_HEADS, MLA_NOPE), F32),
                                   _rope_partner(wuq[..., MLA_NOPE:]), pad_q],
                                  axis=2).reshape(MLA_Q_RANK, D_HEADS * LANE).astype(BF16)
            wukv = odd_w_ukv[j].reshape(MLA_KV_RANK, D_HEADS, MLA_NOPE + MLA_V)
            wk = jnp.concatenate([wukv[..., :MLA_NOPE], jnp.zeros((MLA_KV_RANK, D_HEADS, LANE - MLA_NOPE), F32)],
                                 axis=2).reshape(MLA_KV_RANK, D_HEADS * LANE).astype(BF16)
            wv = wukv[..., MLA_NOPE:].reshape(MLA_KV_RANK, D_HEADS * MLA_V).astype(BF16)
            half = MLA_ROPE // 2
            inv = ROPE_THETA ** (-jnp.arange(half, dtype=F32) / half)
            ang = jnp.arange(s, dtype=F32)[:, None] * inv[None, :]
            cos_h, sin_h = jnp.cos(ang), jnp.sin(ang)
            pad_t = jnp.zeros((s, LANE - dq), F32)
            cos_t = jnp.concatenate([jnp.ones((s, MLA_NOPE), F32), cos_h, cos_h, pad_t], axis=1)
            sin_t = jnp.concatenate([jnp.zeros((s, MLA_NOPE), F32), sin_h, sin_h, pad_t], axis=1)
            tq_t, tk_t = _alibi_tables(s)
            qc_, kc_, vc_, qi_, ki_, wi_, q8, k8, v8 = _odd_proj(
                xt, w_cat, odd_q_norm[j].reshape(1, -1), odd_kv_norm[j].reshape(1, -1),
                wqa, wqb, wk, wv, cos_t, sin_t, tq_t, tk_t, s, tm)
            r3 = lambda a: a.reshape(b, s, a.shape[1])
            out_a = _dsa(r3(qc_), r3(kc_), r3(vc_), r3(qi_), r3(ki_), r3(wi_).transpose(0, 2, 1))
            out_b = _flash(r3(q8), r3(k8), r3(v8), tq=tq, nh=8, name="mla_attn")
        half_w = out_a.shape[2]
        wo = w_o[l].astype(BF16)
        x1e, route_t = _outproj(out_a.reshape(n, half_w), out_b.reshape(n, -1), xt,
                                wo[:half_w], wo[half_w:], ln_g[l, 0].reshape(1, d), ln_b[l, 0].reshape(1, d),
                                rwt, rb, alpha, tm)
        f = moe_w_gate.shape[3]

        def group_in(w):
            w = w.astype(BF16).reshape(N_GROUPS, EXPERTS_PER_GROUP, d, f)
            return w.transpose(0, 2, 1, 3).reshape(N_GROUPS, d, EXPERTS_PER_GROUP * f)

        wd_g = moe_w_down[l].astype(BF16).reshape(N_GROUPS, EXPERTS_PER_GROUP * f, d)
        xt = _moe_layer(x1e, route_t, group_in(moe_w_gate[l]), group_in(moe_w_up[l]), wd_g,
                        ln_g[l, 1].reshape(1, d), ln_b[l, 1].reshape(1, d), alpha, tm_e, tm_c)
    return xt.reshape(b, s, d)
```

```python
import functools
import math

import jax
import jax.numpy as jnp
from jax import lax
from jax.experimental import pallas as pl
from jax.experimental.pallas import tpu as pltpu

F32 = jnp.float32
BF16 = jnp.bfloat16
I32 = jnp.int32

LANE = 128
HEAD_DIM = 64
BLOCK = 128
A_HEADS, A_KV_HEADS, WINDOW = 8, 2, 128
B_HEADS = 8
C_HEADS, IDX_HEADS, IDX_DIM, DSA_TOPK_MAX = 8, 4, 64, 256
D_HEADS, MLA_Q_RANK, MLA_KV_RANK, MLA_NOPE, MLA_ROPE, MLA_V = 8, 256, 128, 64, 32, 64
ROPE_THETA = 10000.0
N_EXPERTS, N_GROUPS = 16, 4
EXPERTS_PER_GROUP = N_EXPERTS // N_GROUPS
LN_EPS, RMS_EPS = 1e-5, 1e-6
NEG_INF = float("-inf")
LOG2E = math.log2(math.e)
VMEM_LIMIT = 56 * 1024 * 1024


def _cparams(sem):
    return pltpu.CompilerParams(dimension_semantics=sem, vmem_limit_bytes=VMEM_LIMIT)


def _dot(a, b):
    return jnp.dot(a, b, preferred_element_type=F32)


def _dot_nt(a, b):
    return lax.dot_general(a, b, (((1,), (1,)), ((), ())), preferred_element_type=F32)


def _lane_lo(shape):
    return lax.broadcasted_iota(I32, shape, len(shape) - 1) % LANE < HEAD_DIM


def _layer_norm(y, g, b):
    mu = jnp.mean(y, axis=-1, keepdims=True)
    yc = y - mu
    var = jnp.mean(yc * yc, axis=-1, keepdims=True)
    return yc * lax.rsqrt(var + LN_EPS) * g + b


def _top16(v):
    bits = lax.bitcast_convert_type(v, I32) & jnp.int32(-65536)
    return lax.bitcast_convert_type(bits, F32)


def _split3(x):
    s1 = _top16(x)
    r = x - s1
    s2 = _top16(r)
    return s1, s2, r - s2


def _aug_rel(shape, h):
    lane = lax.broadcasted_iota(I32, shape, 1)
    return lane - (HEAD_DIM if h % 2 == 0 else 0)


def _even_proj_kernel(x_ref, w_ref, bf_ref, qa_ref, ka_ref, va_ref, qf_ref, kf_ref, vf_ref,
                      carry_ref, *, tiles_per_seq):
    t = pl.program_id(0)
    tm = x_ref.shape[0]
    h = _dot(x_ref[...].astype(BF16), w_ref[...])
    qa_ref[...] = h[:, 0:512].astype(BF16)
    ka_ref[...] = h[:, 512:768].astype(BF16)
    va_ref[...] = h[:, 768:1024].astype(BF16)
    z = h[:, 2560:2688] + bf_ref[...]
    logf = jnp.minimum(z, 0.0) - jnp.log1p(jnp.exp(-jnp.abs(z)))

    @pl.when(t % tiles_per_seq == 0)
    def _():
        carry_ref[...] = jnp.zeros_like(carry_ref)

    row = lax.broadcasted_iota(I32, (tm, tm), 0)
    col = lax.broadcasted_iota(I32, (tm, tm), 1)
    tri = jnp.where(row >= col, 1.0, 0.0).astype(BF16)
    c = carry_ref[...]
    for part in _split3(logf):
        c = c + _dot(tri, part.astype(BF16))
    carry_ref[...] = c[tm - 1:tm, :]

    s1, s2, s3 = _split3(c * LOG2E)
    lo = _lane_lo((tm, LANE))
    for hd in range(B_HEADS):
        own = lo if hd % 2 == 0 else ~lo
        rel = _aug_rel((tm, LANE), hd)
        b1, b2, b3 = s1[:, hd:hd + 1], s2[:, hd:hd + 1], s3[:, hd:hd + 1]
        q_aug = jnp.where(rel == 3, b1, jnp.where(rel == 4, b2, jnp.where(rel == 5, b3,
                          jnp.where((rel >= 0) & (rel < 3), -1.0, 0.0))))
        k_aug = jnp.where(rel == 0, b1, jnp.where(rel == 1, b2, jnp.where(rel == 2, b3,
                          jnp.where((rel >= 3) & (rel < 6), 1.0, 0.0))))
        pj = (hd // 2) * LANE
        sl = slice(hd * LANE, (hd + 1) * LANE)
        qf_ref[:, sl] = jnp.where(own, h[:, 1024 + pj:1024 + pj + LANE], q_aug).astype(BF16)
        kf_ref[:, sl] = jnp.where(own, h[:, 1536 + pj:1536 + pj + LANE], k_aug).astype(BF16)
        vf_ref[:, sl] = jnp.where(own, h[:, 2048 + pj:2048 + pj + LANE], 1.0).astype(BF16)


def _even_proj(xt, w_cat, bf_pad, seq, tm):
    n, d = xt.shape
    wcols = w_cat.shape[1]
    widths = (512, 256, 256, 1024, 1024, 1024)
    out_shape = [jax.ShapeDtypeStruct((n, w), BF16) for w in widths]
    out_specs = [pl.BlockSpec((tm, w), lambda i: (i, 0)) for w in widths]
    return pl.pallas_call(
        functools.partial(_even_proj_kernel, tiles_per_seq=seq // tm),
        out_shape=out_shape,
        grid=(n // tm,),
        in_specs=[pl.BlockSpec((tm, d), lambda i: (i, 0)),
                  pl.BlockSpec((d, wcols), lambda i: (0, 0)),
                  pl.BlockSpec((1, LANE), lambda i: (0, 0))],
        out_specs=out_specs,
        scratch_shapes=[pltpu.VMEM((1, LANE), F32)],
        compiler_params=_cparams(("arbitrary",)),
        name="even_proj",
    )(xt, w_cat, bf_pad)


def _swa_kernel(sink_ref, q_ref, kp_ref, kc_ref, vp_ref, vc_ref, o_ref):
    i = pl.program_id(1)
    group = A_HEADS // A_KV_HEADS
    tl = lax.broadcasted_iota(I32, (BLOCK, 2 * BLOCK), 0)
    sl = lax.broadcasted_iota(I32, (BLOCK, 2 * BLOCK), 1)
    dist = BLOCK + tl - sl
    valid = (dist >= 0) & (dist < WINDOW) & ((sl >= BLOCK) | (i > 0))
    distf = dist.astype(F32)
    lo = _lane_lo((BLOCK, LANE))

    def band(prev_ref, cur_ref, g):
        return jnp.concatenate([prev_ref[:, g * LANE:(g + 1) * LANE], cur_ref[:, g * LANE:(g + 1) * LANE]], axis=0)

    stacked = []
    for g in range(A_KV_HEADS):
        qs = []
        for h in range(g * group, (g + 1) * group):
            qp = q_ref[:, (h // 2) * LANE:(h // 2 + 1) * LANE]
            qs.append(jnp.where(lo if h % 2 == 0 else ~lo, qp, jnp.zeros_like(qp)))
        stacked.append(_dot_nt(jnp.concatenate(qs, axis=0), band(kp_ref, kc_ref, g)))
    ps, denoms = [], []
    for h in range(A_HEADS):
        r = h % group
        slope = 2.0 ** (-8.0 * (h + 1) / A_HEADS) * LOG2E
        logits = jnp.where(valid, stacked[h // group][r * BLOCK:(r + 1) * BLOCK] - slope * distf, NEG_INF)
        sink = sink_ref[h] * LOG2E
        m = jnp.maximum(jnp.max(logits, axis=-1, keepdims=True), sink)
        e = jnp.exp2(logits - m)
        denoms.append(jnp.sum(e, axis=-1, keepdims=True) + jnp.exp2(sink - m))
        ps.append(e.astype(BF16))
    outs = []
    for g in range(A_KV_HEADS):
        pv = _dot(jnp.concatenate(ps[g * group:(g + 1) * group], axis=0), band(vp_ref, vc_ref, g))
        for r in range(group):
            outs.append(pv[r * BLOCK:(r + 1) * BLOCK] / denoms[g * group + r])
    for j in range(A_HEADS // 2):
        o_ref[:, j * LANE:(j + 1) * LANE] = jnp.where(lo, outs[2 * j], outs[2 * j + 1]).astype(BF16)


def _swa(sinks, qa, ka, va):
    b, s, _ = qa.shape
    nb = s // BLOCK
    prev = lambda bi, i: (bi, jnp.maximum(i - 1, 0), 0)
    cur = lambda bi, i: (bi, i, 0)
    return pl.pallas_call(
        _swa_kernel,
        out_shape=jax.ShapeDtypeStruct((b, s, A_HEADS * HEAD_DIM), BF16),
        grid=(b, nb),
        in_specs=[pl.BlockSpec(memory_space=pltpu.SMEM),
                  pl.BlockSpec((None, BLOCK, 512), cur),
                  pl.BlockSpec((None, BLOCK, 256), prev),
                  pl.BlockSpec((None, BLOCK, 256), cur),
                  pl.BlockSpec((None, BLOCK, 256), prev),
                  pl.BlockSpec((None, BLOCK, 256), cur)],
        out_specs=pl.BlockSpec((None, BLOCK, 512), cur),
        compiler_params=_cparams(("parallel", "arbitrary")),
        name="swa_attn",
    )(sinks, qa, ka, ka, va, va)


def _normalize_pairs(accs, o_ref):
    lo = _lane_lo(accs[0].shape)
    outs = [a / pltpu.roll(a, HEAD_DIM, 1) for a in accs]
    for j in range(len(accs) // 2):
        o_ref[:, j * LANE:(j + 1) * LANE] = jnp.where(lo, outs[2 * j], outs[2 * j + 1]).astype(BF16)


def _online_softmax_chains(qs, ks, vs, carry, bias=None, masks=None, guard_empty=False):
    n = len(qs)
    ss = [_dot_nt(qs[i], ks[i]) for i in range(n)]
    if bias is not None:
        ss = [s + bias for s in ss]
    if masks is not None:
        ss = [jnp.where(masks[i], ss[i], NEG_INF) for i in range(n)]
    ms, ps, alphas = [], [], []
    for i in range(n):
        m = carry[i][0]
        m_new = jnp.maximum(m, jnp.max(ss[i], axis=-1, keepdims=True))
        m_ref = jnp.where(m_new == NEG_INF, 0.0, m_new) if guard_empty else m_new
        ps.append(jnp.exp2(ss[i] - m_ref).astype(BF16))
        alphas.append(jnp.exp2(m - m_ref))
        ms.append(m_new)
    pvs = [_dot(ps[i], vs[i]) for i in range(n)]
    return tuple((ms[i], alphas[i] * carry[i][1] + pvs[i]) for i in range(n))


def _flash_kernel(q_ref, k_ref, v_ref, o_ref, *, tq, nh):
    qi = pl.program_id(2)
    nslab = tq // BLOCK
    chains = [(hh, r) for hh in range(nh) for r in range(nslab)]
    row = lax.broadcasted_iota(I32, (BLOCK, tq), 0)
    col = lax.broadcasted_iota(I32, (BLOCK, tq), 1)

    def lanes(hh):
        return slice(hh * LANE, (hh + 1) * LANE)

    def step(kb, carry, diagonal):
        start = pl.multiple_of(kb * tq, tq)
        qs = [q_ref[r * BLOCK:(r + 1) * BLOCK, lanes(hh)] for hh, r in chains]
        ks = [k_ref[pl.ds(start, tq), lanes(hh)] for hh, r in chains]
        vs = [v_ref[pl.ds(start, tq), lanes(hh)] for hh, r in chains]
        masks = [col <= row + r * BLOCK for hh, r in chains] if diagonal else None
        return _online_softmax_chains(qs, ks, vs, carry, masks=masks)

    init = tuple((jnp.full((BLOCK, 1), NEG_INF, F32), jnp.zeros((BLOCK, LANE), F32))
                 for _ in range(nh * nslab))
    carry = lax.fori_loop(0, qi, lambda kb, c: step(kb, c, False), init)
    carry = step(qi, carry, True)
    accs = [jnp.concatenate([carry[hh * nslab + r][1] for r in range(nslab)], axis=0) for hh in range(nh)]
    _normalize_pairs(accs, o_ref)


def _flash(q, k, v, *, tq, nh, name):
    b, s, w = v.shape
    return pl.pallas_call(
        functools.partial(_flash_kernel, tq=tq, nh=nh),
        out_shape=jax.ShapeDtypeStruct((b, s, w // 2), BF16),
        grid=(b, w // (nh * LANE), s // tq),
        in_specs=[pl.BlockSpec((None, tq, nh * LANE), lambda bi, j, i: (bi, i, j)),
                  pl.BlockSpec((None, s, nh * LANE), lambda bi, j, i: (bi, 0, j)),
                  pl.BlockSpec((None, s, nh * LANE), lambda bi, j, i: (bi, 0, j))],
        out_specs=pl.BlockSpec((None, tq, nh * LANE // 2), lambda bi, j, i: (bi, i, j)),
        compiler_params=_cparams(("parallel", "parallel", "arbitrary")),
        name=name,
    )(q, k, v)


def _route_rows(aff, sel):
    gs = []
    for g in range(N_GROUPS):
        r = sel[g * EXPERTS_PER_GROUP:(g + 1) * EXPERTS_PER_GROUP]
        best = None
        for a in range(EXPERTS_PER_GROUP):
            for c in range(a + 1, EXPERTS_PER_GROUP):
                pair = r[a] + r[c]
                best = pair if best is None else jnp.maximum(best, pair)
        gs.append(best)
    gmax = functools.reduce(jnp.maximum, gs)
    gidx = jnp.full_like(gmax, float(N_GROUPS - 1))
    for g in range(N_GROUPS - 2, -1, -1):
        gidx = jnp.where(gs[g] == gmax, float(g), gidx)

    def pick(rows, k):
        out = rows[(N_GROUPS - 1) * EXPERTS_PER_GROUP + k]
        for g in range(N_GROUPS - 2, -1, -1):
            out = jnp.where(gidx == float(g), rows[g * EXPERTS_PER_GROUP + k], out)
        return out

    s = [pick(sel, k) for k in range(EXPERTS_PER_GROUP)]
    a = [pick(aff, k) for k in range(EXPERTS_PER_GROUP)]

    def first_max(vals):
        mx = functools.reduce(jnp.maximum, vals)
        idx = jnp.full_like(mx, float(len(vals) - 1))
        for k in range(len(vals) - 2, -1, -1):
            idx = jnp.where(vals[k] == mx, float(k), idx)
        return idx

    def take(vals, idx):
        out = vals[-1]
        for k in range(len(vals) - 2, -1, -1):
            out = jnp.where(idx == float(k), vals[k], out)
        return out

    i1 = first_max(s)
    s2 = [jnp.where(i1 == float(k), NEG_INF, s[k]) for k in range(EXPERTS_PER_GROUP)]
    i2 = first_max(s2)
    a1, a2 = take(a, i1), take(a, i2)
    tot = a1 + a2
    base = gidx * float(EXPERTS_PER_GROUP)
    return base + i1, base + i2, a1 / tot, a2 / tot


def _outproj_kernel(a_ref, b_ref, x_ref, wa_ref, wb_ref, g_ref, be_ref, rwt_ref, rb_ref,
                    x1_ref, route_ref, *, alpha):
    d = x_ref.shape[1]
    tm = x_ref.shape[0]
    mix = _dot(a_ref[...], wa_ref[...]) + _dot(b_ref[...], wb_ref[...])
    x1 = _layer_norm(alpha * x_ref[...] + mix, g_ref[...], be_ref[...])
    x1_ref[:, 0:d] = x1
    x_hi = x1.astype(BF16)
    x_lo = (x1 - x_hi.astype(F32)).astype(BF16)
    w = rwt_ref[...]
    w_hi = w.astype(BF16)
    w_lo = (w - w_hi.astype(F32)).astype(BF16)
    z_hi = _dot_nt(jnp.concatenate([w_hi, w_lo], axis=0), x_hi)
    zt = z_hi[0:N_EXPERTS] + z_hi[N_EXPERTS:2 * N_EXPERTS] + _dot_nt(w_hi, x_lo)
    afft = 1.0 / (1.0 + jnp.exp(-zt))
    selt = afft + rb_ref[...]
    aff = [afft[e:e + 1, :] for e in range(N_EXPERTS)]
    sel = [selt[e:e + 1, :] for e in range(N_EXPERTS)]
    e1, e2, w1, w2 = _route_rows(aff, sel)
    rec = jnp.concatenate([e1, e2, w1, w2, jnp.zeros((LANE - 4, tm), F32)], axis=0)
    route_ref[...] = rec[0:8]
    x1_ref[:, d:d + LANE] = rec.T


def _outproj(a, b, xt, wa, wb, g, be, rwt, rb, alpha, tm):
    n, d = xt.shape
    full = lambda i: (0, 0)
    return pl.pallas_call(
        functools.partial(_outproj_kernel, alpha=alpha),
        out_shape=[jax.ShapeDtypeStruct((n, d + LANE), F32), jax.ShapeDtypeStruct((8, n), F32)],
        grid=(n // tm,),
        in_specs=[pl.BlockSpec((tm, a.shape[1]), lambda i: (i, 0)),
                  pl.BlockSpec((tm, b.shape[1]), lambda i: (i, 0)),
                  pl.BlockSpec((tm, d), lambda i: (i, 0)),
                  pl.BlockSpec(wa.shape, full), pl.BlockSpec(wb.shape, full),
                  pl.BlockSpec((1, d), full), pl.BlockSpec((1, d), full),
                  pl.BlockSpec(rwt.shape, full), pl.BlockSpec(rb.shape, full)],
        out_specs=[pl.BlockSpec((tm, d + LANE), lambda i: (i, 0)), pl.BlockSpec((8, tm), lambda i: (0, i))],
        compiler_params=_cparams(("parallel",)),
        name="outproj_ln_route",
    )(a, b, xt, wa, wb, g, be, rwt, rb)


def _moe_kernel(bgrp_ref, bbase_ref, nact_ref, src_ref,
                x_hbm, wg_ref, wu_ref, wd_ref, o_ref, xbuf0, xbuf1, sem, *, tm, d):
    i = pl.program_id(0)
    nact = nact_ref[0]
    last = src_ref.shape[0] - 1
    bufs = (xbuf0, xbuf1)

    def issue(blk, slot):
        base = bbase_ref[blk]
        for r in range(tm):
            tok = src_ref[jnp.minimum(base + r, last)]
            pltpu.make_async_copy(x_hbm.at[pl.ds(tok, 1), :], bufs[slot].at[pl.ds(r, 1), :],
                                  sem.at[slot]).start(priority=r % 2)

    def wait(slot):
        pltpu.make_async_copy(x_hbm.at[pl.ds(0, tm), :], bufs[slot], sem.at[slot]).wait()

    @pl.when((i == 0) & (nact > 0))
    def _():
        issue(0, 0)

    def step(slot):
        wait(slot)

        @pl.when(i + 1 < nact)
        def _():
            issue(i + 1, 1 - slot)

        buf = bufs[slot]
        xb = buf[:, 0:d].astype(BF16)
        rec = buf[:, d:d + LANE]
        e1, e2, w1, w2 = rec[:, 0:1], rec[:, 1:2], rec[:, 2:3], rec[:, 3:4]
        fg = wg_ref.shape[1]
        lane_expert = lax.broadcasted_iota(I32, (tm, fg), 1) // (fg // EXPERTS_PER_GROUP)
        expert = (lane_expert + bgrp_ref[i] * EXPERTS_PER_GROUP).astype(F32)
        gate_w = jnp.where(expert == e1, w1, jnp.where(expert == e2, w2, 0.0))
        gate = _dot(xb, wg_ref[...])
        up = _dot(xb, wu_ref[...])
        h = gate * (1.0 / (1.0 + jnp.exp(-gate))) * up * gate_w
        o_ref[...] = _dot(h.astype(BF16), wd_ref[...])

    for slot in range(2):
        pl.when((i < nact) & (i % 2 == slot))(functools.partial(step, slot))

    @pl.when(i >= nact)
    def _():
        o_ref[...] = jnp.zeros_like(o_ref)


def _moe_experts(x1e, wg, wu, wd, bgrp, bbase, nact, src, tm):
    n, de = x1e.shape
    d = de - LANE
    nblk = bgrp.shape[0]
    fg = wg.shape[2]
    wmap = lambda i, bgrp, *_: (bgrp[i], 0, 0)
    grid_spec = pltpu.PrefetchScalarGridSpec(
        num_scalar_prefetch=4,
        grid=(nblk,),
        in_specs=[pl.BlockSpec(memory_space=pl.ANY),
                  pl.BlockSpec((None, d, fg), wmap),
                  pl.BlockSpec((None, d, fg), wmap),
                  pl.BlockSpec((None, fg, d), wmap)],
        out_specs=pl.BlockSpec((tm, d), lambda i, *_: (i, 0)),
        scratch_shapes=[pltpu.VMEM((tm, de), F32), pltpu.VMEM((tm, de), F32), pltpu.SemaphoreType.DMA((2,))],
    )
    return pl.pallas_call(
        functools.partial(_moe_kernel, tm=tm, d=d),
        out_shape=jax.ShapeDtypeStruct((nblk * tm, d), F32),
        grid_spec=grid_spec,
        compiler_params=_cparams(("arbitrary",)),
        name="moe_experts",
    )(bgrp, bbase, nact, src, x1e, wg, wu, wd)


def _combine_kernel(p_ref, x_ref, g_ref, be_ref, o_hbm, y_ref, obuf0, obuf1, sem, *, alpha, tm, d):
    i = pl.program_id(0)
    nsteps = pl.num_programs(0)

    bufs = (obuf0, obuf1)

    def issue(blk, slot):
        base = blk * tm
        for r in range(tm):
            pltpu.make_async_copy(o_hbm.at[pl.ds(p_ref[base + r], 1), :],
                                  bufs[slot].at[pl.ds(r, 1), :], sem.at[slot]).start(priority=r % 2)

    def wait(slot):
        pltpu.make_async_copy(o_hbm.at[pl.ds(0, tm), :], bufs[slot], sem.at[slot]).wait()

    @pl.when(i == 0)
    def _():
        issue(0, 0)

    def step(slot):
        wait(slot)

        @pl.when(i + 1 < nsteps)
        def _():
            issue(i + 1, 1 - slot)

        y = alpha * x_ref[:, 0:d] + bufs[slot][...]
        y_ref[...] = _layer_norm(y, g_ref[...], be_ref[...])

    for slot in range(2):
        pl.when(i % 2 == slot)(functools.partial(step, slot))


def _combine(x1e, g, be, o, pos, alpha, tm):
    n, de = x1e.shape
    d = de - LANE
    grid_spec = pltpu.PrefetchScalarGridSpec(
        num_scalar_prefetch=1,
        grid=(n // tm,),
        in_specs=[pl.BlockSpec((tm, de), lambda i, *_: (i, 0)),
                  pl.BlockSpec((1, d), lambda i, *_: (0, 0)),
                  pl.BlockSpec((1, d), lambda i, *_: (0, 0)),
                  pl.BlockSpec(memory_space=pl.ANY)],
        out_specs=pl.BlockSpec((tm, d), lambda i, *_: (i, 0)),
        scratch_shapes=[pltpu.VMEM((tm, d), F32), pltpu.VMEM((tm, d), F32), pltpu.SemaphoreType.DMA((2,))],
    )
    return pl.pallas_call(
        functools.partial(_combine_kernel, alpha=alpha, tm=tm, d=d),
        out_shape=jax.ShapeDtypeStruct((n, d), F32),
        grid_spec=grid_spec,
        compiler_params=_cparams(("arbitrary",)),
        name="moe_combine_ln",
    )(pos, x1e, g, be, o)


def _moe_layer(x1e, route_t, wg, wu, wd, g, be, alpha, tm_e, tm_c):
    n = x1e.shape[0]
    grp = route_t[0].astype(I32) // EXPERTS_PER_GROUP
    onehot = (grp[:, None] == jnp.arange(N_GROUPS, dtype=I32)[None, :]).astype(I32)
    rank = jnp.take_along_axis(jnp.cumsum(onehot, axis=0) - onehot, grp[:, None], axis=1)[:, 0]
    counts = jnp.sum(onehot, axis=0)
    nblk_g = (counts + tm_e - 1) // tm_e
    blk_start = jnp.cumsum(nblk_g) - nblk_g
    seg_start = jnp.cumsum(counts) - counts
    pos = (blk_start[grp] * tm_e + rank).astype(I32)
    src = jnp.argsort(grp, stable=True).astype(I32)
    nblk = n // tm_e + N_GROUPS
    blk = jnp.arange(nblk, dtype=I32)
    blk_end = jnp.cumsum(nblk_g)
    bgrp = jnp.minimum(jnp.sum((blk[:, None] >= blk_end[None, :]).astype(I32), axis=1), N_GROUPS - 1)
    bbase = (seg_start[bgrp] + (blk - blk_start[bgrp]) * tm_e).astype(I32)
    nact = jnp.sum(nblk_g).astype(I32).reshape(1)
    o = _moe_experts(x1e, wg, wu, wd, bgrp, bbase, nact, src, tm_e)
    return _combine(x1e, g, be, o, pos, alpha, tm_c)


def _rms(x, g):
    return x * lax.rsqrt(jnp.mean(x * x, axis=-1, keepdims=True) + RMS_EPS) * g

ODD_OFF = dict(qc=0, kc=512, vc=640, qi=768, ki=1024, wi=1152, cq=1280, ckv=1536, kra=1664, krb=1792, end=1920)


def _odd_proj_kernel(x_ref, w_ref, qng_ref, kvng_ref, wqa_ref, wqb_ref, wk_ref, wv_ref, cos_ref, sin_ref,
                     tq_ref, tk_ref,
                     qc_ref, kc_ref, vc_ref, qi_ref, ki_ref, wi_ref, q8_ref, k8_ref, v8_ref, *, q_scale):
    o = ODD_OFF
    tm = x_ref.shape[0]
    lo = _lane_lo((tm, LANE))
    h = _dot(x_ref[...].astype(BF16), w_ref[...])
    kdup = h[:, o["kc"]:o["vc"]]
    vdup = h[:, o["vc"]:o["qi"]]
    kc_ref[:, 0:LANE] = jnp.where(lo, kdup, tk_ref[:, 0:LANE].astype(F32)).astype(BF16)
    kc_ref[:, LANE:2 * LANE] = jnp.where(lo, tk_ref[:, LANE:2 * LANE].astype(F32), kdup).astype(BF16)
    vc_ref[:, 0:LANE] = jnp.where(lo, vdup, 1.0).astype(BF16)
    vc_ref[:, LANE:2 * LANE] = jnp.where(lo, 1.0, vdup).astype(BF16)
    qi_ref[...] = h[:, o["qi"]:o["ki"]].astype(BF16)
    ki_ref[...] = h[:, o["ki"]:o["wi"]].astype(BF16)
    wi_ref[...] = h[:, o["wi"]:o["wi"] + IDX_HEADS]
    cos = cos_ref[...]
    sin = sin_ref[...]
    cqn = _rms(h[:, o["cq"]:o["ckv"]], qng_ref[...]).astype(BF16)
    qa = _dot(cqn, wqa_ref[...])
    qb = _dot(cqn, wqb_ref[...])
    ckvn = _rms(h[:, o["ckv"]:o["kra"]], kvng_ref[...]).astype(BF16)
    kn = _dot(ckvn, wk_ref[...])
    v8 = _dot(ckvn, wv_ref[...])
    kr = h[:, o["kra"]:o["krb"]] * cos + h[:, o["krb"]:o["end"]] * sin
    for hd in range(D_HEADS):
        sl = slice(hd * LANE, (hd + 1) * LANE)
        pj = slice((hd // 2) * LANE, (hd // 2 + 1) * LANE)
        own = lo if hd % 2 == 0 else ~lo
        qc_ref[:, sl] = jnp.where(own, h[:, pj], tq_ref[:, sl].astype(F32)).astype(BF16)
        q8_ref[:, sl] = ((qa[:, sl] * cos + qb[:, sl] * sin) * q_scale).astype(BF16)
        k8_ref[:, sl] = (kn[:, sl] + kr).astype(BF16)
        v8_ref[:, sl] = jnp.where(own, v8[:, pj], 1.0).astype(BF16)


def _odd_proj(xt, w_cat, qng, kvng, wqa, wqb, wk, wv, cos_t, sin_t, tq_t, tk_t, seq, tm):
    n, d = xt.shape
    full = lambda i: (0, 0)
    tps = seq // tm
    pos = lambda i: (i % tps, 0)
    widths = (1024, 256, 256, 256, 128)
    out_shape = [jax.ShapeDtypeStruct((n, w), BF16) for w in widths]
    out_specs = [pl.BlockSpec((tm, w), lambda i: (i, 0)) for w in widths]
    out_shape.append(jax.ShapeDtypeStruct((n, IDX_HEADS), F32))
    out_specs.append(pl.BlockSpec((tm, IDX_HEADS), lambda i: (i, 0)))
    for w in (1024, 1024, 1024):
        out_shape.append(jax.ShapeDtypeStruct((n, w), BF16))
        out_specs.append(pl.BlockSpec((tm, w), lambda i: (i, 0)))
    return pl.pallas_call(
        functools.partial(_odd_proj_kernel, q_scale=(MLA_NOPE + MLA_ROPE) ** -0.5 * LOG2E),
        out_shape=out_shape,
        grid=(n // tm,),
        in_specs=[pl.BlockSpec((tm, d), lambda i: (i, 0)),
                  pl.BlockSpec(w_cat.shape, full),
                  pl.BlockSpec(qng.shape, full), pl.BlockSpec(kvng.shape, full),
                  pl.BlockSpec(wqa.shape, full), pl.BlockSpec(wqb.shape, full),
                  pl.BlockSpec(wk.shape, full), pl.BlockSpec(wv.shape, full),
                  pl.BlockSpec((tm, LANE), pos), pl.BlockSpec((tm, LANE), pos),
                  pl.BlockSpec((tm, C_HEADS * LANE), pos), pl.BlockSpec((tm, 2 * LANE), pos)],
        out_specs=out_specs,
        compiler_params=_cparams(("parallel",)),
        name="odd_proj",
    )(xt, w_cat, qng, kvng, wqa, wqb, wk, wv, cos_t, sin_t, tq_t, tk_t)


def _dsa_kernel(q_ref, k_ref, v_ref, qi_ref, ki_ref, wt_ref, o_ref, sc_ref, mb_ref, *, top_k, chunk):
    i = pl.program_id(1)
    nc = (i * BLOCK + BLOCK + chunk - 1) // chunk
    lo = _lane_lo((BLOCK, LANE))
    s_row2 = lax.broadcasted_iota(I32, (2 * chunk, BLOCK), 0)
    t_lane2 = i * BLOCK + lax.broadcasted_iota(I32, (2 * chunk, BLOCK), 1)

    def chunk_start(c):
        return pl.multiple_of(c * chunk, chunk)

    qidx = []
    for h in range(IDX_HEADS):
        qp = qi_ref[:, (h // 2) * LANE:(h // 2 + 1) * LANE]
        qidx.append(jnp.where(lo if h % 2 == 0 else ~lo, qp, jnp.zeros_like(qp)))
    q_stack = jnp.concatenate(qidx, axis=0)
    w = wt_ref[...]

    npair = (nc + 1) // 2

    def pair_start(cp):
        return pl.multiple_of(cp * (2 * chunk), 2 * chunk)

    def score_pair(cp, _):
        start = pair_start(cp)
        d = _dot_nt(ki_ref[pl.ds(start, 2 * chunk), :], q_stack)
        sc = w[0:1, :] * jnp.maximum(d[:, 0:BLOCK], 0.0)
        for h in range(1, IDX_HEADS):
            sc = sc + w[h:h + 1, :] * jnp.maximum(d[:, h * BLOCK:(h + 1) * BLOCK], 0.0)
        sc = jnp.where(sc == 0.0, 0.0, sc)
        sc_ref[pl.ds(start, 2 * chunk), :] = jnp.where(start + s_row2 <= t_lane2, sc, NEG_INF)
        return 0

    lax.fori_loop(0, npair, score_pair, 0)

    nacc = 4 * 8

    def count(trial, strict):
        def body(cp, acc):
            sc = sc_ref[pl.ds(pair_start(cp), 2 * chunk), :]
            hit = (sc > trial) if strict else (sc >= trial)
            return acc + jnp.sum(jnp.where(hit, 1.0, 0.0).reshape(2 * chunk // nacc, nacc, BLOCK), axis=0)

        acc = lax.fori_loop(0, npair, body, jnp.zeros((nacc, BLOCK), F32))
        return jnp.sum(acc, axis=0, keepdims=True)

    def decode(key):
        return lax.bitcast_convert_type(jnp.where(key < 0, key ^ jnp.int32(0x7FFFFFFF), key), F32)

    kf = float(top_k)
    key_neg_inf = jnp.int32(-2139095041)
    cand = jnp.where(count(jnp.zeros((1, BLOCK), F32), False) >= kf, 0, jnp.int32(-2 ** 31)).astype(I32)

    def bit_step(j, cand):
        trial = cand + lax.shift_left(jnp.int32(1), 30 - j)
        ok = (trial <= key_neg_inf) | (count(decode(trial), False) >= kf)
        return jnp.where(ok, trial, cand)

    thr = decode(lax.fori_loop(0, 31, bit_step, cand))
    need = kf - count(thr, True)

    r0 = lax.broadcasted_iota(I32, (chunk, chunk), 0)
    r1 = lax.broadcasted_iota(I32, (chunk, chunk), 1)
    below = jnp.where(r1 < r0, 1.0, 0.0).astype(BF16)
    need = jnp.where(thr == NEG_INF, 0.0, need)

    def mask_pair(cp, offset):
        start = pair_start(cp)
        sc = sc_ref[pl.ds(start, 2 * chunk), :]
        eq = sc == thr
        eqf = jnp.where(eq, 1.0, 0.0)
        eqb = eqf.astype(BF16)
        first = jnp.sum(eqf[0:chunk], axis=0, keepdims=True)
        prefix = jnp.concatenate([_dot(below, eqb[0:chunk]) + offset,
                                  _dot(below, eqb[chunk:2 * chunk]) + (offset + first)], axis=0)
        bias = jnp.where(sc > thr, 0.0, jnp.where(eq, jnp.where(prefix < need, 0.0, NEG_INF), NEG_INF))
        mb_ref[:, pl.ds(start, 2 * chunk)] = bias.T
        return offset + first + jnp.sum(eqf[chunk:2 * chunk], axis=0, keepdims=True)

    lax.fori_loop(0, npair, mask_pair, jnp.zeros((1, BLOCK), F32))

    def att_chunk(c, carry):
        start = chunk_start(c)
        qs = [q_ref[:, h * LANE:(h + 1) * LANE] for h in range(C_HEADS)]
        ks = [k_ref[pl.ds(start, chunk), (h % 2) * LANE:(h % 2 + 1) * LANE] for h in range(C_HEADS)]
        vs = [v_ref[pl.ds(start, chunk), (h % 2) * LANE:(h % 2 + 1) * LANE] for h in range(C_HEADS)]
        return _online_softmax_chains(qs, ks, vs, carry, bias=mb_ref[:, pl.ds(start, chunk)], guard_empty=True)

    init = tuple((jnp.full((BLOCK, 1), NEG_INF, F32), jnp.zeros((BLOCK, LANE), F32)) for _ in range(C_HEADS))
    carry = lax.fori_loop(0, nc, att_chunk, init)
    _normalize_pairs([c[1] for c in carry], o_ref)


def _dsa(qc, kc, vc, qi, ki, wit):
    b, s, _ = qc.shape
    top_k = min(DSA_TOPK_MAX, s // 4)
    chunk = 256
    assert s % (2 * chunk) == 0 and top_k <= chunk
    blk = lambda bi, i: (bi, i, 0)
    seq = lambda bi, i: (bi, 0, 0)
    return pl.pallas_call(
        functools.partial(_dsa_kernel, top_k=top_k, chunk=chunk),
        out_shape=jax.ShapeDtypeStruct((b, s, C_HEADS * HEAD_DIM), BF16),
        grid=(b, s // BLOCK),
        in_specs=[pl.BlockSpec((None, BLOCK, C_HEADS * LANE), blk),
                  pl.BlockSpec((None, s, 2 * LANE), seq),
                  pl.BlockSpec((None, s, 2 * LANE), seq),
                  pl.BlockSpec((None, BLOCK, 256), blk),
                  pl.BlockSpec((None, s, LANE), seq),
                  pl.BlockSpec((None, IDX_HEADS, BLOCK), lambda bi, i: (bi, 0, i))],
        out_specs=pl.BlockSpec((None, BLOCK, 512), blk),
        scratch_shapes=[pltpu.VMEM((s, BLOCK), F32), pltpu.VMEM((BLOCK, s), F32)],
        compiler_params=_cparams(("parallel", "arbitrary")),
        name="dsa_attn",
    )(qc, kc, vc, qi, ki, wit)


def _dup64(w):
    d, c = w.shape
    w = w.reshape(d, c // HEAD_DIM, 1, HEAD_DIM)
    return jnp.broadcast_to(w, (d, c // HEAD_DIM, 2, HEAD_DIM)).reshape(d, 2 * c)


def _pad_cols(w, width):
    return jnp.pad(w, ((0, 0), (0, width - w.shape[1])))


def _split_cols(w, sizes):
    out, start = [], 0
    for n in sizes:
        out.append(w[:, start:start + n])
        start += n
    return out


def _rope_partner(w):
    half = MLA_ROPE // 2
    return jnp.concatenate([-w[..., half:], w[..., :half]], axis=-1)


def _rope_block(w_rope):
    d = w_rope.shape[0]
    return jnp.concatenate([jnp.zeros((d, MLA_NOPE), F32), w_rope,
                            jnp.zeros((d, LANE - MLA_NOPE - MLA_ROPE), F32)], axis=1)


def _alibi_tables(s):
    slopes = 2.0 ** (-8.0 * jnp.arange(1, C_HEADS + 1, dtype=F32) / C_HEADS)
    a = slopes * LOG2E
    pos = jnp.arange(s, dtype=F32)
    a_sp = _split3(a)
    r_sp = _split3(-(a[None, :] * pos[:, None]))
    ones = jnp.ones((s, C_HEADS), F32)
    q_terms = [128.0 * t * ones for t in a_sp] + [t * ones for t in a_sp] + list(r_sp)
    q_aug = jnp.stack(q_terms, axis=-1)
    q_aug = jnp.pad(q_aug, ((0, 0), (0, 0), (0, HEAD_DIM - q_aug.shape[-1])))
    zeros = jnp.zeros_like(q_aug)
    even = jnp.concatenate([zeros, q_aug], axis=-1)
    odd = jnp.concatenate([q_aug, zeros], axis=-1)
    parity = (jnp.arange(C_HEADS) % 2 == 0)[None, :, None]
    tq_t = jnp.where(parity, even, odd).reshape(s, C_HEADS * LANE)
    s_hi = jnp.floor(pos / LANE)
    s_lo = pos - LANE * s_hi
    k_aug = jnp.stack([s_hi] * 3 + [s_lo] * 3 + [jnp.ones_like(pos)] * 3, axis=-1)
    k_aug = jnp.pad(k_aug, ((0, 0), (0, HEAD_DIM - k_aug.shape[-1])))
    kz = jnp.zeros_like(k_aug)
    tk_t = jnp.concatenate([kz, k_aug, k_aug, kz], axis=-1)
    return tq_t.astype(BF16), tk_t.astype(BF16)


def kernel(x, even_w_in, even_b_f, even_sinks, odd_w_in, odd_q_norm, odd_kv_norm, odd_w_uq, odd_w_ukv,
           w_o, ln_g, ln_b, router_w, router_b, moe_w_gate, moe_w_up, moe_w_down):
    b, s, d = x.shape
    n = b * s
    depth = w_o.shape[0]
    alpha = (2 * depth) ** 0.25
    tm = min(512, s)
    tq = min(256, s)
    tm_e = min(512, n)
    tm_c = min(256, n)
    att_scale = HEAD_DIM ** -0.5
    xt = x.reshape(n, d)
    rwt = router_w.T
    rb = router_b.reshape(N_EXPERTS, 1)

    for l in range(depth):
        j = l // 2
        if l % 2 == 0:
            qa, ka, va, qb, kb, vb, fg = _split_cols(
                even_w_in[j], (512, 128, 128, 512, 512, 512, B_HEADS))
            w_cat = jnp.concatenate([qa * (att_scale * LOG2E), _dup64(ka), _dup64(va), qb * (att_scale * LOG2E), kb, vb,
                                     _pad_cols(fg, LANE)], axis=1).astype(BF16)
            bf_pad = _pad_cols(even_b_f[j].reshape(1, B_HEADS), LANE)
            qa_, ka_, va_, qf, kf, vf = _even_proj(xt, w_cat, bf_pad, s, tm)
            r3 = lambda a: a.reshape(b, s, a.shape[1])
            out_a = _swa(even_sinks[j], r3(qa_), r3(ka_), r3(va_))
            out_b = _flash(r3(qf), r3(kf), r3(vf), tq=tq, nh=8, name="fox_attn")
        else:
            (qc, kc, vc, qi, ki, wi, cq, ckv, kr) = _split_cols(
                odd_w_in[j], (512, 64, 64, IDX_HEADS * IDX_DIM, IDX_DIM, IDX_HEADS,
                              MLA_Q_RANK, MLA_KV_RANK, MLA_ROPE))
            w_cat = jnp.concatenate([
                qc * (att_scale * LOG2E), _dup64(kc), _dup64(vc), qi * (IDX_DIM ** -0.5), _dup64(ki),
                _pad_cols(wi * (IDX_HEADS ** -0.5), LANE), cq, ckv,
                _rope_block(kr), _rope_block(_rope_partner(kr))], axis=1).astype(BF16)
            dq = MLA_NOPE + MLA_ROPE
            wuq = odd_w_uq[j].reshape(MLA_Q_RANK, D_HEADS, dq)
            pad_q = jnp.zeros((MLA_Q_RANK, D_HEADS, LANE - dq), F32)
            wqa = jnp.concatenate([wuq, pad_q], axis=2).reshape(MLA_Q_RANK, D_HEADS * LANE).astype(BF16)
            wqb = jnp.concatenate([jnp.zeros((MLA_Q_RANK, D_HEADS, MLA_NOPE), F32),
                                   _rope_partner(wuq[..., MLA_NOPE:]), pad_q],
                                  axis=2).reshape(MLA_Q_RANK, D_HEADS * LANE).astype(BF16)
            wukv = odd_w_ukv[j].reshape(MLA_KV_RANK, D_HEADS, MLA_NOPE + MLA_V)
            wk = jnp.concatenate([wukv[..., :MLA_NOPE], jnp.zeros((MLA_KV_RANK, D_HEADS, LANE - MLA_NOPE), F32)],
                                 axis=2).reshape(MLA_KV_RANK, D_HEADS * LANE).astype(BF16)
            wv = wukv[..., MLA_NOPE:].reshape(MLA_KV_RANK, D_HEADS * MLA_V).astype(BF16)
            half = MLA_ROPE // 2
            inv = ROPE_THETA ** (-jnp.arange(half, dtype=F32) / half)
            ang = jnp.arange(s, dtype=F32)[:, None] * inv[None, :]
            cos_h, sin_h = jnp.cos(ang), jnp.sin(ang)
            pad_t = jnp.zeros((s, LANE - dq), F32)
            cos_t = jnp.concatenate([jnp.ones((s, MLA_NOPE), F32), cos_h, cos_h, pad_t], axis=1)
            sin_t = jnp.concatenate([jnp.zeros((s, MLA_NOPE), F32), sin_h, sin_h, pad_t], axis=1)
            tq_t, tk_t = _alibi_tables(s)
            qc_, kc_, vc_, qi_, ki_, wi_, q8, k8, v8 = _odd_proj(
                xt, w_cat, odd_q_norm[j].reshape(1, -1), odd_kv_norm[j].reshape(1, -1),
                wqa, wqb, wk, wv, cos_t, sin_t, tq_t, tk_t, s, tm)
            r3 = lambda a: a.reshape(b, s, a.shape[1])
            out_a = _dsa(r3(qc_), r3(kc_), r3(vc_), r3(qi_), r3(ki_), r3(wi_).transpose(0, 2, 1))
            out_b = _flash(r3(q8), r3(k8), r3(v8), tq=tq, nh=8, name="mla_attn")
        half_w = out_a.shape[2]
        wo = w_o[l].astype(BF16)
        x1e, route_t = _outproj(out_a.reshape(n, half_w), out_b.reshape(n, -1), xt,
                                wo[:half_w], wo[half_w:], ln_g[l, 0].reshape(1, d), ln_b[l, 0].reshape(1, d),
                                rwt, rb, alpha, tm)
        f = moe_w_gate.shape[3]

        def group_in(w):
            w = w.astype(BF16).reshape(N_GROUPS, EXPERTS_PER_GROUP, d, f)
            return w.transpose(0, 2, 1, 3).reshape(N_GROUPS, d, EXPERTS_PER_GROUP * f)

        wd_g = moe_w_down[l].astype(BF16).reshape(N_GROUPS, EXPERTS_PER_GROUP * f, d)
        xt = _moe_layer(x1e, route_t, group_in(moe_w_gate[l]), group_in(moe_w_up[l]), wd_g,
                        ln_g[l, 1].reshape(1, d), ln_b[l, 1].reshape(1, d), alpha, tm_e, tm_c)
    return xt.reshape(b, s, d)
```

```python
import functools
import math

import jax
import jax.numpy as jnp
from jax import lax
from jax.experimental import pallas as pl
from jax.experimental.pallas import tpu as pltpu

F32 = jnp.float32
BF16 = jnp.bfloat16
I32 = jnp.int32

LANE = 128
HEAD_DIM = 64
BLOCK = 128
A_HEADS, A_KV_HEADS, WINDOW = 8, 2, 128
B_HEADS = 8
C_HEADS, IDX_HEADS, IDX_DIM, DSA_TOPK_MAX = 8, 4, 64, 256
D_HEADS, MLA_Q_RANK, MLA_KV_RANK, MLA_NOPE, MLA_ROPE, MLA_V = 8, 256, 128, 64, 32, 64
ROPE_THETA = 10000.0
N_EXPERTS, N_GROUPS = 16, 4
EXPERTS_PER_GROUP = N_EXPERTS // N_GROUPS
LN_EPS, RMS_EPS = 1e-5, 1e-6
NEG_INF = float("-inf")
LOG2E = math.log2(math.e)
VMEM_LIMIT = 56 * 1024 * 1024


def _cparams(sem):
    return pltpu.CompilerParams(dimension_semantics=sem, vmem_limit_bytes=VMEM_LIMIT)


def _dot(a, b):
    return jnp.dot(a, b, preferred_element_type=F32)


def _dot_nt(a, b):
    return lax.dot_general(a, b, (((1,), (1,)), ((), ())), preferred_element_type=F32)


def _lane_lo(shape):
    return lax.broadcasted_iota(I32, shape, len(shape) - 1) % LANE < HEAD_DIM


def _layer_norm(y, g, b):
    mu = jnp.mean(y, axis=-1, keepdims=True)
    yc = y - mu
    var = jnp.mean(yc * yc, axis=-1, keepdims=True)
    return yc * lax.rsqrt(var + LN_EPS) * g + b


def _top16(v):
    bits = lax.bitcast_convert_type(v, I32) & jnp.int32(-65536)
    return lax.bitcast_convert_type(bits, F32)


def _split3(x):
    s1 = _top16(x)
    r = x - s1
    s2 = _top16(r)
    return s1, s2, r - s2


def _aug_rel(shape, h):
    lane = lax.broadcasted_iota(I32, shape, 1)
    return lane - (HEAD_DIM if h % 2 == 0 else 0)


def _even_proj_kernel(x_ref, w_ref, bf_ref, qa_ref, ka_ref, va_ref, qf_ref, kf_ref, vf_ref,
                      carry_ref, *, tiles_per_seq):
    t = pl.program_id(0)
    tm = x_ref.shape[0]
    h = _dot(x_ref[...].astype(BF16), w_ref[...])
    qa_ref[...] = h[:, 0:512].astype(BF16)
    ka_ref[...] = h[:, 512:768].astype(BF16)
    va_ref[...] = h[:, 768:1024].astype(BF16)
    z = h[:, 2560:2688] + bf_ref[...]
    logf = jnp.minimum(z, 0.0) - jnp.log1p(jnp.exp(-jnp.abs(z)))

    @pl.when(t % tiles_per_seq == 0)
    def _():
        carry_ref[...] = jnp.zeros_like(carry_ref)

    row = lax.broadcasted_iota(I32, (tm, tm), 0)
    col = lax.broadcasted_iota(I32, (tm, tm), 1)
    tri = jnp.where(row >= col, 1.0, 0.0).astype(BF16)
    c = carry_ref[...]
    for part in _split3(logf):
        c = c + _dot(tri, part.astype(BF16))
    carry_ref[...] = c[tm - 1:tm, :]

    s1, s2, s3 = _split3(c * LOG2E)
    lo = _lane_lo((tm, LANE))
    for hd in range(B_HEADS):
        own = lo if hd % 2 == 0 else ~lo
        rel = _aug_rel((tm, LANE), hd)
        b1, b2, b3 = s1[:, hd:hd + 1], s2[:, hd:hd + 1], s3[:, hd:hd + 1]
        q_aug = jnp.where(rel == 3, b1, jnp.where(rel == 4, b2, jnp.where(rel == 5, b3,
                          jnp.where((rel >= 0) & (rel < 3), -1.0, 0.0))))
        k_aug = jnp.where(rel == 0, b1, jnp.where(rel == 1, b2, jnp.where(rel == 2, b3,
                          jnp.where((rel >= 3) & (rel < 6), 1.0, 0.0))))
        pj = (hd // 2) * LANE
        sl = slice(hd * LANE, (hd + 1) * LANE)
        qf_ref[:, sl] = jnp.where(own, h[:, 1024 + pj:1024 + pj + LANE], q_aug).astype(BF16)
        kf_ref[:, sl] = jnp.where(own, h[:, 1536 + pj:1536 + pj + LANE], k_aug).astype(BF16)
        vf_ref[:, sl] = jnp.where(own, h[:, 2048 + pj:2048 + pj + LANE], 1.0).astype(BF16)


def _even_proj(xt, w_cat, bf_pad, seq, tm):
    n, d = xt.shape
    wcols = w_cat.shape[1]
    widths = (512, 256, 256, 1024, 1024, 1024)
    out_shape = [jax.ShapeDtypeStruct((n, w), BF16) for w in widths]
    out_specs = [pl.BlockSpec((tm, w), lambda i: (i, 0)) for w in widths]
    return pl.pallas_call(
        functools.partial(_even_proj_kernel, tiles_per_seq=seq // tm),
        out_shape=out_shape,
        grid=(n // tm,),
        in_specs=[pl.BlockSpec((tm, d), lambda i: (i, 0)),
                  pl.BlockSpec((d, wcols), lambda i: (0, 0)),
                  pl.BlockSpec((1, LANE), lambda i: (0, 0))],
        out_specs=out_specs,
        scratch_shapes=[pltpu.VMEM((1, LANE), F32)],
        compiler_params=_cparams(("arbitrary",)),
        name="even_proj",
    )(xt, w_cat, bf_pad)


def _swa_kernel(sink_ref, q_ref, kp_ref, kc_ref, vp_ref, vc_ref, o_ref):
    i = pl.program_id(1)
    group = A_HEADS // A_KV_HEADS
    tl = lax.broadcasted_iota(I32, (BLOCK, 2 * BLOCK), 0)
    sl = lax.broadcasted_iota(I32, (BLOCK, 2 * BLOCK), 1)
    dist = BLOCK + tl - sl
    valid = (dist >= 0) & (dist < WINDOW) & ((sl >= BLOCK) | (i > 0))
    distf = dist.astype(F32)
    lo = _lane_lo((BLOCK, LANE))

    def band(prev_ref, cur_ref, g):
        return jnp.concatenate([prev_ref[:, g * LANE:(g + 1) * LANE], cur_ref[:, g * LANE:(g + 1) * LANE]], axis=0)

    stacked = []
    for g in range(A_KV_HEADS):
        qs = []
        for h in range(g * group, (g + 1) * group):
            qp = q_ref[:, (h // 2) * LANE:(h // 2 + 1) * LANE]
            qs.append(jnp.where(lo if h % 2 == 0 else ~lo, qp, jnp.zeros_like(qp)))
        stacked.append(_dot_nt(jnp.concatenate(qs, axis=0), band(kp_ref, kc_ref, g)))
    ps, denoms = [], []
    for h in range(A_HEADS):
        r = h % group
        slope = 2.0 ** (-8.0 * (h + 1) / A_HEADS) * LOG2E
        logits = jnp.where(valid, stacked[h // group][r * BLOCK:(r + 1) * BLOCK] - slope * distf, NEG_INF)
        sink = sink_ref[h] * LOG2E
        m = jnp.maximum(jnp.max(logits, axis=-1, keepdims=True), sink)
        e = jnp.exp2(logits - m)
        denoms.append(jnp.sum(e, axis=-1, keepdims=True) + jnp.exp2(sink - m))
        ps.append(e.astype(BF16))
    outs = []
    for g in range(A_KV_HEADS):
        pv = _dot(jnp.concatenate(ps[g * group:(g + 1) * group], axis=0), band(vp_ref, vc_ref, g))
        for r in range(group):
            outs.append(pv[r * BLOCK:(r + 1) * BLOCK] / denoms[g * group + r])
    for j in range(A_HEADS // 2):
        o_ref[:, j * LANE:(j + 1) * LANE] = jnp.where(lo, outs[2 * j], outs[2 * j + 1]).astype(BF16)


def _swa(sinks, qa, ka, va):
    b, s, _ = qa.shape
    nb = s // BLOCK
    prev = lambda bi, i: (bi, jnp.maximum(i - 1, 0), 0)
    cur = lambda bi, i: (bi, i, 0)
    return pl.pallas_call(
        _swa_kernel,
        out_shape=jax.ShapeDtypeStruct((b, s, A_HEADS * HEAD_DIM), BF16),
        grid=(b, nb),
        in_specs=[pl.BlockSpec(memory_space=pltpu.SMEM),
                  pl.BlockSpec((None, BLOCK, 512), cur),
                  pl.BlockSpec((None, BLOCK, 256), prev),
                  pl.BlockSpec((None, BLOCK, 256), cur),
                  pl.BlockSpec((None, BLOCK, 256), prev),
                  pl.BlockSpec((None, BLOCK, 256), cur)],
        out_specs=pl.BlockSpec((None, BLOCK, 512), cur),
        compiler_params=_cparams(("parallel", "arbitrary")),
        name="swa_attn",
    )(sinks, qa, ka, ka, va, va)


def _normalize_pairs(accs, o_ref):
    lo = _lane_lo(accs[0].shape)
    outs = [a / pltpu.roll(a, HEAD_DIM, 1) for a in accs]
    for j in range(len(accs) // 2):
        o_ref[:, j * LANE:(j + 1) * LANE] = jnp.where(lo, outs[2 * j], outs[2 * j + 1]).astype(BF16)


def _online_softmax_chains(qs, ks, vs, carry, bias=None, masks=None, guard_empty=False):
    n = len(qs)
    ss = [_dot_nt(qs[i], ks[i]).astype(BF16) for i in range(n)]
    if bias is not None:
        bias16 = bias.astype(BF16)
        ss = [s + bias16 for s in ss]
    if masks is not None:
        ss = [jnp.where(masks[i], ss[i], NEG_INF) for i in range(n)]
    ms, ps, alphas = [], [], []
    for i in range(n):
        m = carry[i][0]
        m_new = jnp.maximum(m, jnp.max(ss[i], axis=-1, keepdims=True).astype(F32))
        m_ref = jnp.where(m_new == NEG_INF, 0.0, m_new) if guard_empty else m_new
        ps.append(jnp.exp2(ss[i] - m_ref.astype(BF16)))
        alphas.append(jnp.exp2(m - m_ref))
        ms.append(m_new)
    pvs = [_dot(ps[i], vs[i]) for i in range(n)]
    return tuple((ms[i], alphas[i] * carry[i][1] + pvs[i]) for i in range(n))


def _flash_kernel(q_ref, k_ref, v_ref, o_ref, *, tq, nh):
    qi = pl.program_id(2)
    nslab = tq // BLOCK
    chains = [(hh, r) for hh in range(nh) for r in range(nslab)]
    row = lax.broadcasted_iota(I32, (BLOCK, tq), 0)
    col = lax.broadcasted_iota(I32, (BLOCK, tq), 1)

    def lanes(hh):
        return slice(hh * LANE, (hh + 1) * LANE)

    def step(kb, carry, diagonal):
        start = pl.multiple_of(kb * tq, tq)
        qs = [q_ref[r * BLOCK:(r + 1) * BLOCK, lanes(hh)] for hh, r in chains]
        ks = [k_ref[pl.ds(start, tq), lanes(hh)] for hh, r in chains]
        vs = [v_ref[pl.ds(start, tq), lanes(hh)] for hh, r in chains]
        masks = [col <= row + r * BLOCK for hh, r in chains] if diagonal else None
        return _online_softmax_chains(qs, ks, vs, carry, masks=masks)

    init = tuple((jnp.full((BLOCK, 1), NEG_INF, F32), jnp.zeros((BLOCK, LANE), F32))
                 for _ in range(nh * nslab))
    carry = lax.fori_loop(0, qi, lambda kb, c: step(kb, c, False), init)
    carry = step(qi, carry, True)
    accs = [jnp.concatenate([carry[hh * nslab + r][1] for r in range(nslab)], axis=0) for hh in range(nh)]
    _normalize_pairs(accs, o_ref)


def _flash(q, k, v, *, tq, nh, name):
    b, s, w = v.shape
    return pl.pallas_call(
        functools.partial(_flash_kernel, tq=tq, nh=nh),
        out_shape=jax.ShapeDtypeStruct((b, s, w // 2), BF16),
        grid=(b, w // (nh * LANE), s // tq),
        in_specs=[pl.BlockSpec((None, tq, nh * LANE), lambda bi, j, i: (bi, i, j)),
                  pl.BlockSpec((None, s, nh * LANE), lambda bi, j, i: (bi, 0, j)),
                  pl.BlockSpec((None, s, nh * LANE), lambda bi, j, i: (bi, 0, j))],
        out_specs=pl.BlockSpec((None, tq, nh * LANE // 2), lambda bi, j, i: (bi, i, j)),
        compiler_params=_cparams(("parallel", "parallel", "arbitrary")),
        name=name,
    )(q, k, v)


def _route_rows(aff, sel):
    gs = []
    for g in range(N_GROUPS):
        r = sel[g * EXPERTS_PER_GROUP:(g + 1) * EXPERTS_PER_GROUP]
        best = None
        for a in range(EXPERTS_PER_GROUP):
            for c in range(a + 1, EXPERTS_PER_GROUP):
                pair = r[a] + r[c]
                best = pair if best is None else jnp.maximum(best, pair)
        gs.append(best)
    gmax = functools.reduce(jnp.maximum, gs)
    gidx = jnp.full_like(gmax, float(N_GROUPS - 1))
    for g in range(N_GROUPS - 2, -1, -1):
        gidx = jnp.where(gs[g] == gmax, float(g), gidx)

    def pick(rows, k):
        out = rows[(N_GROUPS - 1) * EXPERTS_PER_GROUP + k]
        for g in range(N_GROUPS - 2, -1, -1):
            out = jnp.where(gidx == float(g), rows[g * EXPERTS_PER_GROUP + k], out)
        return out

    s = [pick(sel, k) for k in range(EXPERTS_PER_GROUP)]
    a = [pick(aff, k) for k in range(EXPERTS_PER_GROUP)]

    def first_max(vals):
        mx = functools.reduce(jnp.maximum, vals)
        idx = jnp.full_like(mx, float(len(vals) - 1))
        for k in range(len(vals) - 2, -1, -1):
            idx = jnp.where(vals[k] == mx, float(k), idx)
        return idx

    def take(vals, idx):
        out = vals[-1]
        for k in range(len(vals) - 2, -1, -1):
            out = jnp.where(idx == float(k), vals[k], out)
        return out

    i1 = first_max(s)
    s2 = [jnp.where(i1 == float(k), NEG_INF, s[k]) for k in range(EXPERTS_PER_GROUP)]
    i2 = first_max(s2)
    a1, a2 = take(a, i1), take(a, i2)
    tot = a1 + a2
    base = gidx * float(EXPERTS_PER_GROUP)
    return base + i1, base + i2, a1 / tot, a2 / tot


def _outproj_kernel(a_ref, b_ref, x_ref, wa_ref, wb_ref, g_ref, be_ref, rwt_ref, rb_ref,
                    x1_ref, route_ref, *, alpha):
    d = x_ref.shape[1]
    tm = x_ref.shape[0]
    mix = _dot(a_ref[...], wa_ref[...]) + _dot(b_ref[...], wb_ref[...])
    x1 = _layer_norm(alpha * x_ref[...] + mix, g_ref[...], be_ref[...])
    x1_ref[:, 0:d] = x1
    x_hi = x1.astype(BF16)
    x_lo = (x1 - x_hi.astype(F32)).astype(BF16)
    w = rwt_ref[...]
    w_hi = w.astype(BF16)
    w_lo = (w - w_hi.astype(F32)).astype(BF16)
    z_hi = _dot_nt(jnp.concatenate([w_hi, w_lo], axis=0), x_hi)
    zt = z_hi[0:N_EXPERTS] + z_hi[N_EXPERTS:2 * N_EXPERTS] + _dot_nt(w_hi, x_lo)
    afft = 1.0 / (1.0 + jnp.exp(-zt))
    selt = afft + rb_ref[...]
    aff = [afft[e:e + 1, :] for e in range(N_EXPERTS)]
    sel = [selt[e:e + 1, :] for e in range(N_EXPERTS)]
    e1, e2, w1, w2 = _route_rows(aff, sel)
    rec = jnp.concatenate([e1, e2, w1, w2, jnp.zeros((LANE - 4, tm), F32)], axis=0)
    route_ref[...] = rec[0:8]
    x1_ref[:, d:d + LANE] = rec.T


def _outproj(a, b, xt, wa, wb, g, be, rwt, rb, alpha, tm):
    n, d = xt.shape
    full = lambda i: (0, 0)
    return pl.pallas_call(
        functools.partial(_outproj_kernel, alpha=alpha),
        out_shape=[jax.ShapeDtypeStruct((n, d + LANE), F32), jax.ShapeDtypeStruct((8, n), F32)],
        grid=(n // tm,),
        in_specs=[pl.BlockSpec((tm, a.shape[1]), lambda i: (i, 0)),
                  pl.BlockSpec((tm, b.shape[1]), lambda i: (i, 0)),
                  pl.BlockSpec((tm, d), lambda i: (i, 0)),
                  pl.BlockSpec(wa.shape, full), pl.BlockSpec(wb.shape, full),
                  pl.BlockSpec((1, d), full), pl.BlockSpec((1, d), full),
                  pl.BlockSpec(rwt.shape, full), pl.BlockSpec(rb.shape, full)],
        out_specs=[pl.BlockSpec((tm, d + LANE), lambda i: (i, 0)), pl.BlockSpec((8, tm), lambda i: (0, i))],
        compiler_params=_cparams(("parallel",)),
        name="outproj_ln_route",
    )(a, b, xt, wa, wb, g, be, rwt, rb)


def _moe_kernel(bgrp_ref, bbase_ref, nact_ref, src_ref,
                x_hbm, wg_ref, wu_ref, wd_ref, o_ref, xbuf0, xbuf1, sem, *, tm, d):
    i = pl.program_id(0)
    nact = nact_ref[0]
    last = src_ref.shape[0] - 1
    bufs = (xbuf0, xbuf1)

    def issue(blk, slot):
        base = bbase_ref[blk]
        for r in range(tm):
            tok = src_ref[jnp.minimum(base + r, last)]
            pltpu.make_async_copy(x_hbm.at[pl.ds(tok, 1), :], bufs[slot].at[pl.ds(r, 1), :],
                                  sem.at[slot]).start(priority=r % 2)

    def wait(slot):
        pltpu.make_async_copy(x_hbm.at[pl.ds(0, tm), :], bufs[slot], sem.at[slot]).wait()

    @pl.when((i == 0) & (nact > 0))
    def _():
        issue(0, 0)

    def step(slot):
        wait(slot)

        @pl.when(i + 1 < nact)
        def _():
            issue(i + 1, 1 - slot)

        buf = bufs[slot]
        xb = buf[:, 0:d].astype(BF16)
        rec = buf[:, d:d + LANE]
        e1, e2, w1, w2 = rec[:, 0:1], rec[:, 1:2], rec[:, 2:3], rec[:, 3:4]
        fg = wg_ref.shape[1]
        lane_expert = lax.broadcasted_iota(I32, (tm, fg), 1) // (fg // EXPERTS_PER_GROUP)
        expert = (lane_expert + bgrp_ref[i] * EXPERTS_PER_GROUP).astype(F32)
        gate_w = jnp.where(expert == e1, w1, jnp.where(expert == e2, w2, 0.0))
        gate = _dot(xb, wg_ref[...])
        up = _dot(xb, wu_ref[...])
        h = gate * (1.0 / (1.0 + jnp.exp(-gate))) * up * gate_w
        o_ref[...] = _dot(h.astype(BF16), wd_ref[...])

    for slot in range(2):
        pl.when((i < nact) & (i % 2 == slot))(functools.partial(step, slot))

    @pl.when(i >= nact)
    def _():
        o_ref[...] = jnp.zeros_like(o_ref)


def _moe_experts(x1e, wg, wu, wd, bgrp, bbase, nact, src, tm):
    n, de = x1e.shape
    d = de - LANE
    nblk = bgrp.shape[0]
    fg = wg.shape[2]
    wmap = lambda i, bgrp, *_: (bgrp[i], 0, 0)
    grid_spec = pltpu.PrefetchScalarGridSpec(
        num_scalar_prefetch=4,
        grid=(nblk,),
        in_specs=[pl.BlockSpec(memory_space=pl.ANY),
                  pl.BlockSpec((None, d, fg), wmap),
                  pl.BlockSpec((None, d, fg), wmap),
                  pl.BlockSpec((None, fg, d), wmap)],
        out_specs=pl.BlockSpec((tm, d), lambda i, *_: (i, 0)),
        scratch_shapes=[pltpu.VMEM((tm, de), F32), pltpu.VMEM((tm, de), F32), pltpu.SemaphoreType.DMA((2,))],
    )
    return pl.pallas_call(
        functools.partial(_moe_kernel, tm=tm, d=d),
        out_shape=jax.ShapeDtypeStruct((nblk * tm, d), F32),
        grid_spec=grid_spec,
        compiler_params=_cparams(("arbitrary",)),
        name="moe_experts",
    )(bgrp, bbase, nact, src, x1e, wg, wu, wd)


def _combine_kernel(p_ref, x_ref, g_ref, be_ref, o_hbm, y_ref, obuf0, obuf1, sem, *, alpha, tm, d):
    i = pl.program_id(0)
    nsteps = pl.num_programs(0)

    bufs = (obuf0, obuf1)

    def issue(blk, slot):
        base = blk * tm
        for r in range(tm):
            pltpu.make_async_copy(o_hbm.at[pl.ds(p_ref[base + r], 1), :],
                                  bufs[slot].at[pl.ds(r, 1), :], sem.at[slot]).start(priority=r % 2)

    def wait(slot):
        pltpu.make_async_copy(o_hbm.at[pl.ds(0, tm), :], bufs[slot], sem.at[slot]).wait()

    @pl.when(i == 0)
    def _():
        issue(0, 0)

    def step(slot):
        wait(slot)

        @pl.when(i + 1 < nsteps)
        def _():
            issue(i + 1, 1 - slot)

        y = alpha * x_ref[:, 0:d] + bufs[slot][...]
        y_ref[...] = _layer_norm(y, g_ref[...], be_ref[...])

    for slot in range(2):
        pl.when(i % 2 == slot)(functools.partial(step, slot))


def _combine(x1e, g, be, o, pos, alpha, tm):
    n, de = x1e.shape
    d = de - LANE
    grid_spec = pltpu.PrefetchScalarGridSpec(
        num_scalar_prefetch=1,
        grid=(n // tm,),
        in_specs=[pl.BlockSpec((tm, de), lambda i, *_: (i, 0)),
                  pl.BlockSpec((1, d), lambda i, *_: (0, 0)),
                  pl.BlockSpec((1, d), lambda i, *_: (0, 0)),
                  pl.BlockSpec(memory_space=pl.ANY)],
        out_specs=pl.BlockSpec((tm, d), lambda i, *_: (i, 0)),
        scratch_shapes=[pltpu.VMEM((tm, d), F32), pltpu.VMEM((tm, d), F32), pltpu.SemaphoreType.DMA((2,))],
    )
    return pl.pallas_call(
        functools.partial(_combine_kernel, alpha=alpha, tm=tm, d=d),
        out_shape=jax.ShapeDtypeStruct((n, d), F32),
        grid_spec=grid_spec,
        compiler_params=_cparams(("arbitrary",)),
        name="moe_combine_ln",
    )(pos, x1e, g, be, o)


def _moe_layer(x1e, route_t, wg, wu, wd, g, be, alpha, tm_e, tm_c):
    n = x1e.shape[0]
    grp = route_t[0].astype(I32) // EXPERTS_PER_GROUP
    onehot = (grp[:, None] == jnp.arange(N_GROUPS, dtype=I32)[None, :]).astype(I32)
    rank = jnp.take_along_axis(jnp.cumsum(onehot, axis=0) - onehot, grp[:, None], axis=1)[:, 0]
    counts = jnp.sum(onehot, axis=0)
    nblk_g = (counts + tm_e - 1) // tm_e
    blk_start = jnp.cumsum(nblk_g) - nblk_g
    seg_start = jnp.cumsum(counts) - counts
    pos = (blk_start[grp] * tm_e + rank).astype(I32)
    src = jnp.argsort(grp, stable=True).astype(I32)
    nblk = n // tm_e + N_GROUPS
    blk = jnp.arange(nblk, dtype=I32)
    blk_end = jnp.cumsum(nblk_g)
    bgrp = jnp.minimum(jnp.sum((blk[:, None] >= blk_end[None, :]).astype(I32), axis=1), N_GROUPS - 1)
    bbase = (seg_start[bgrp] + (blk - blk_start[bgrp]) * tm_e).astype(I32)
    nact = jnp.sum(nblk_g).astype(I32).reshape(1)
    o = _moe_experts(x1e, wg, wu, wd, bgrp, bbase, nact, src, tm_e)
    return _combine(x1e, g, be, o, pos, alpha, tm_c)


def _rms(x, g):
    return x * lax.rsqrt(jnp.mean(x * x, axis=-1, keepdims=True) + RMS_EPS) * g

ODD_OFF = dict(qc=0, kc=512, vc=640, qi=768, ki=1024, wi=1152, cq=1280, ckv=1536, kra=1664, krb=1792, end=1920)


def _odd_proj_kernel(x_ref, w_ref, qng_ref, kvng_ref, wqa_ref, wqb_ref, wk_ref, wv_ref, cos_ref, sin_ref,
                     tq_ref, tk_ref,
                     qc_ref, kc_ref, vc_ref, qi_ref, ki_ref, wi_ref, q8_ref, k8_ref, v8_ref, *, q_scale):
    o = ODD_OFF
    tm = x_ref.shape[0]
    lo = _lane_lo((tm, LANE))
    h = _dot(x_ref[...].astype(BF16), w_ref[...])
    kdup = h[:, o["kc"]:o["vc"]]
    vdup = h[:, o["vc"]:o["qi"]]
    kc_ref[:, 0:LANE] = jnp.where(lo, kdup, tk_ref[:, 0:LANE].astype(F32)).astype(BF16)
    kc_ref[:, LANE:2 * LANE] = jnp.where(lo, tk_ref[:, LANE:2 * LANE].astype(F32), kdup).astype(BF16)
    vc_ref[:, 0:LANE] = jnp.where(lo, vdup, 1.0).astype(BF16)
    vc_ref[:, LANE:2 * LANE] = jnp.where(lo, 1.0, vdup).astype(BF16)
    qi_ref[...] = h[:, o["qi"]:o["ki"]].astype(BF16)
    ki_ref[...] = h[:, o["ki"]:o["wi"]].astype(BF16)
    wi_ref[...] = h[:, o["wi"]:o["wi"] + IDX_HEADS]
    cos = cos_ref[...]
    sin = sin_ref[...]
    cqn = _rms(h[:, o["cq"]:o["ckv"]], qng_ref[...]).astype(BF16)
    qa = _dot(cqn, wqa_ref[...])
    qb = _dot(cqn, wqb_ref[...])
    ckvn = _rms(h[:, o["ckv"]:o["kra"]], kvng_ref[...]).astype(BF16)
    kn = _dot(ckvn, wk_ref[...])
    v8 = _dot(ckvn, wv_ref[...])
    kr = h[:, o["kra"]:o["krb"]] * cos + h[:, o["krb"]:o["end"]] * sin
    for hd in range(D_HEADS):
        sl = slice(hd * LANE, (hd + 1) * LANE)
        pj = slice((hd // 2) * LANE, (hd // 2 + 1) * LANE)
        own = lo if hd % 2 == 0 else ~lo
        qc_ref[:, sl] = jnp.where(own, h[:, pj], tq_ref[:, sl].astype(F32)).astype(BF16)
        q8_ref[:, sl] = ((qa[:, sl] * cos + qb[:, sl] * sin) * q_scale).astype(BF16)
        k8_ref[:, sl] = (kn[:, sl] + kr).astype(BF16)
        v8_ref[:, sl] = jnp.where(own, v8[:, pj], 1.0).astype(BF16)


def _odd_proj(xt, w_cat, qng, kvng, wqa, wqb, wk, wv, cos_t, sin_t, tq_t, tk_t, seq, tm):
    n, d = xt.shape
    full = lambda i: (0, 0)
    tps = seq // tm
    pos = lambda i: (i % tps, 0)
    widths = (1024, 256, 256, 256, 128)
    out_shape = [jax.ShapeDtypeStruct((n, w), BF16) for w in widths]
    out_specs = [pl.BlockSpec((tm, w), lambda i: (i, 0)) for w in widths]
    out_shape.append(jax.ShapeDtypeStruct((n, IDX_HEADS), F32))
    out_specs.append(pl.BlockSpec((tm, IDX_HEADS), lambda i: (i, 0)))
    for w in (1024, 1024, 1024):
        out_shape.append(jax.ShapeDtypeStruct((n, w), BF16))
        out_specs.append(pl.BlockSpec((tm, w), lambda i: (i, 0)))
    return pl.pallas_call(
        functools.partial(_odd_proj_kernel, q_scale=(MLA_NOPE + MLA_ROPE) ** -0.5 * LOG2E),
        out_shape=out_shape,
        grid=(n // tm,),
        in_specs=[pl.BlockSpec((tm, d), lambda i: (i, 0)),
                  pl.BlockSpec(w_cat.shape, full),
                  pl.BlockSpec(qng.shape, full), pl.BlockSpec(kvng.shape, full),
                  pl.BlockSpec(wqa.shape, full), pl.BlockSpec(wqb.shape, full),
                  pl.BlockSpec(wk.shape, full), pl.BlockSpec(wv.shape, full),
                  pl.BlockSpec((tm, LANE), pos), pl.BlockSpec((tm, LANE), pos),
                  pl.BlockSpec((tm, C_HEADS * LANE), pos), pl.BlockSpec((tm, 2 * LANE), pos)],
        out_specs=out_specs,
        compiler_params=_cparams(("parallel",)),
        name="odd_proj",
    )(xt, w_cat, qng, kvng, wqa, wqb, wk, wv, cos_t, sin_t, tq_t, tk_t)


def _dsa_kernel(q_ref, k_ref, v_ref, qi_ref, ki_ref, wt_ref, o_ref, sc_ref, mb_ref, *, top_k, chunk):
    i = pl.program_id(1)
    nc = (i * BLOCK + BLOCK + chunk - 1) // chunk
    lo = _lane_lo((BLOCK, LANE))
    s_row2 = lax.broadcasted_iota(I32, (2 * chunk, BLOCK), 0)
    t_lane2 = i * BLOCK + lax.broadcasted_iota(I32, (2 * chunk, BLOCK), 1)

    def chunk_start(c):
        return pl.multiple_of(c * chunk, chunk)

    qidx = []
    for h in range(IDX_HEADS):
        qp = qi_ref[:, (h // 2) * LANE:(h // 2 + 1) * LANE]
        qidx.append(jnp.where(lo if h % 2 == 0 else ~lo, qp, jnp.zeros_like(qp)))
    q_stack = jnp.concatenate(qidx, axis=0)
    w = wt_ref[...]

    npair = (nc + 1) // 2

    def pair_start(cp):
        return pl.multiple_of(cp * (2 * chunk), 2 * chunk)

    def score_pair(cp, _):
        start = pair_start(cp)
        d = _dot_nt(ki_ref[pl.ds(start, 2 * chunk), :], q_stack)
        sc = w[0:1, :] * jnp.maximum(d[:, 0:BLOCK], 0.0)
        for h in range(1, IDX_HEADS):
            sc = sc + w[h:h + 1, :] * jnp.maximum(d[:, h * BLOCK:(h + 1) * BLOCK], 0.0)
        sc = jnp.where(sc == 0.0, 0.0, sc)
        sc_ref[pl.ds(start, 2 * chunk), :] = jnp.where(start + s_row2 <= t_lane2, sc, NEG_INF)
        return 0

    lax.fori_loop(0, npair, score_pair, 0)

    nacc = 4 * 8

    def count(trial, strict):
        def body(cp, acc):
            sc = sc_ref[pl.ds(pair_start(cp), 2 * chunk), :]
            hit = (sc > trial) if strict else (sc >= trial)
            return acc + jnp.sum(jnp.where(hit, 1.0, 0.0).reshape(2 * chunk // nacc, nacc, BLOCK), axis=0)

        acc = lax.fori_loop(0, npair, body, jnp.zeros((nacc, BLOCK), F32))
        return jnp.sum(acc, axis=0, keepdims=True)

    def decode(key):
        return lax.bitcast_convert_type(jnp.where(key < 0, key ^ jnp.int32(0x7FFFFFFF), key), F32)

    kf = float(top_k)
    key_neg_inf = jnp.int32(-2139095041)
    cand = jnp.where(count(jnp.zeros((1, BLOCK), F32), False) >= kf, 0, jnp.int32(-2 ** 31)).astype(I32)

    def bit_step(j, cand):
        trial = cand + lax.shift_left(jnp.int32(1), 30 - j)
        ok = (trial <= key_neg_inf) | (count(decode(trial), False) >= kf)
        return jnp.where(ok, trial, cand)

    thr = decode(lax.fori_loop(0, 31, bit_step, cand))
    need = kf - count(thr, True)

    r0 = lax.broadcasted_iota(I32, (chunk, chunk), 0)
    r1 = lax.broadcasted_iota(I32, (chunk, chunk), 1)
    below = jnp.where(r1 < r0, 1.0, 0.0).astype(BF16)
    need = jnp.where(thr == NEG_INF, 0.0, need)

    def mask_pair(cp, offset):
        start = pair_start(cp)
        sc = sc_ref[pl.ds(start, 2 * chunk), :]
        eq = sc == thr
        eqf = jnp.where(eq, 1.0, 0.0)
        eqb = eqf.astype(BF16)
        first = jnp.sum(eqf[0:chunk], axis=0, keepdims=True)
        prefix = jnp.concatenate([_dot(below, eqb[0:chunk]) + offset,
                                  _dot(below, eqb[chunk:2 * chunk]) + (offset + first)], axis=0)
        bias = jnp.where(sc > thr, 0.0, jnp.where(eq, jnp.where(prefix < need, 0.0, NEG_INF), NEG_INF))
        mb_ref[:, pl.ds(start, 2 * chunk)] = bias.T
        return offset + first + jnp.sum(eqf[chunk:2 * chunk], axis=0, keepdims=True)

    lax.fori_loop(0, npair, mask_pair, jnp.zeros((1, BLOCK), F32))

    def att_chunk(c, carry):
        start = chunk_start(c)
        qs = [q_ref[:, h * LANE:(h + 1) * LANE] for h in range(C_HEADS)]
        ks = [k_ref[pl.ds(start, chunk), (h % 2) * LANE:(h % 2 + 1) * LANE] for h in range(C_HEADS)]
        vs = [v_ref[pl.ds(start, chunk), (h % 2) * LANE:(h % 2 + 1) * LANE] for h in range(C_HEADS)]
        return _online_softmax_chains(qs, ks, vs, carry, bias=mb_ref[:, pl.ds(start, chunk)], guard_empty=True)

    init = tuple((jnp.full((BLOCK, 1), NEG_INF, F32), jnp.zeros((BLOCK, LANE), F32)) for _ in range(C_HEADS))
    carry = lax.fori_loop(0, nc, att_chunk, init)
    _normalize_pairs([c[1] for c in carry], o_ref)


def _dsa(qc, kc, vc, qi, ki, wit):
    b, s, _ = qc.shape
    top_k = min(DSA_TOPK_MAX, s // 4)
    chunk = 256
    assert s % (2 * chunk) == 0 and top_k <= chunk
    blk = lambda bi, i: (bi, i, 0)
    seq = lambda bi, i: (bi, 0, 0)
    return pl.pallas_call(
        functools.partial(_dsa_kernel, top_k=top_k, chunk=chunk),
        out_shape=jax.ShapeDtypeStruct((b, s, C_HEADS * HEAD_DIM), BF16),
        grid=(b, s // BLOCK),
        in_specs=[pl.BlockSpec((None, BLOCK, C_HEADS * LANE), blk),
                  pl.BlockSpec((None, s, 2 * LANE), seq),
                  pl.BlockSpec((None, s, 2 * LANE), seq),
                  pl.BlockSpec((None, BLOCK, 256), blk),
                  pl.BlockSpec((None, s, LANE), seq),
                  pl.BlockSpec((None, IDX_HEADS, BLOCK), lambda bi, i: (bi, 0, i))],
        out_specs=pl.BlockSpec((None, BLOCK, 512), blk),
        scratch_shapes=[pltpu.VMEM((s, BLOCK), F32), pltpu.VMEM((BLOCK, s), F32)],
        compiler_params=_cparams(("parallel", "arbitrary")),
        name="dsa_attn",
    )(qc, kc, vc, qi, ki, wit)


def _dup64(w):
    d, c = w.shape
    w = w.reshape(d, c // HEAD_DIM, 1, HEAD_DIM)
    return jnp.broadcast_to(w, (d, c // HEAD_DIM, 2, HEAD_DIM)).reshape(d, 2 * c)


def _pad_cols(w, width):
    return jnp.pad(w, ((0, 0), (0, width - w.shape[1])))


def _split_cols(w, sizes):
    out, start = [], 0
    for n in sizes:
        out.append(w[:, start:start + n])
        start += n
    return out


def _rope_partner(w):
    half = MLA_ROPE // 2
    return jnp.concatenate([-w[..., half:], w[..., :half]], axis=-1)


def _rope_block(w_rope):
    d = w_rope.shape[0]
    return jnp.concatenate([jnp.zeros((d, MLA_NOPE), F32), w_rope,
                            jnp.zeros((d, LANE - MLA_NOPE - MLA_ROPE), F32)], axis=1)


def _alibi_tables(s):
    slopes = 2.0 ** (-8.0 * jnp.arange(1, C_HEADS + 1, dtype=F32) / C_HEADS)
    a = slopes * LOG2E
    pos = jnp.arange(s, dtype=F32)
    a_sp = _split3(a)
    r_sp = _split3(-(a[None, :] * pos[:, None]))
    ones = jnp.ones((s, C_HEADS), F32)
    q_terms = [128.0 * t * ones for t in a_sp] + [t * ones for t in a_sp] + list(r_sp)
    q_aug = jnp.stack(q_terms, axis=-1)
    q_aug = jnp.pad(q_aug, ((0, 0), (0, 0), (0, HEAD_DIM - q_aug.shape[-1])))
    zeros = jnp.zeros_like(q_aug)
    even = jnp.concatenate([zeros, q_aug], axis=-1)
    odd = jnp.concatenate([q_aug, zeros], axis=-1)
    parity = (jnp.arange(C_HEADS) % 2 == 0)[None, :, None]
    tq_t = jnp.where(parity, even, odd).reshape(s, C_HEADS * LANE)
    s_hi = jnp.floor(pos / LANE)
    s_lo = pos - LANE * s_hi
    k_aug = jnp.stack([s_hi] * 3 + [s_lo] * 3 + [jnp.ones_like(pos)] * 3, axis=-1)
    k_aug = jnp.pad(k_aug, ((0, 0), (0, HEAD_DIM - k_aug.shape[-1])))
    kz = jnp.zeros_like(k_aug)
    tk_t = jnp.concatenate([kz, k_aug, k_aug, kz], axis=-1)
    return tq_t.astype(BF16), tk_t.astype(BF16)


def kernel(x, even_w_in, even_b_f, even_sinks, odd_w_in, odd_q_norm, odd_kv_norm, odd_w_uq, odd_w_ukv,
           w_o, ln_g, ln_b, router_w, router_b, moe_w_gate, moe_w_up, moe_w_down):
    b, s, d = x.shape
    n = b * s
    depth = w_o.shape[0]
    alpha = (2 * depth) ** 0.25
    tm = min(512, s)
    tq = min(256, s)
    tm_e = min(512, n)
    tm_c = min(256, n)
    att_scale = HEAD_DIM ** -0.5
    xt = x.reshape(n, d)
    rwt = router_w.T
    rb = router_b.reshape(N_EXPERTS, 1)

    for l in range(depth):
        j = l // 2
        if l % 2 == 0:
            qa, ka, va, qb, kb, vb, fg = _split_cols(
                even_w_in[j], (512, 128, 128, 512, 512, 512, B_HEADS))
            w_cat = jnp.concatenate([qa * (att_scale * LOG2E), _dup64(ka), _dup64(va), qb * (att_scale * LOG2E), kb, vb,
                                     _pad_cols(fg, LANE)], axis=1).astype(BF16)
            bf_pad = _pad_cols(even_b_f[j].reshape(1, B_HEADS), LANE)
            qa_, ka_, va_, qf, kf, vf = _even_proj(xt, w_cat, bf_pad, s, tm)
            r3 = lambda a: a.reshape(b, s, a.shape[1])
            out_a = _swa(even_sinks[j], r3(qa_), r3(ka_), r3(va_))
            out_b = _flash(r3(qf), r3(kf), r3(vf), tq=tq, nh=8, name="fox_attn")
        else:
            (qc, kc, vc, qi, ki, wi, cq, ckv, kr) = _split_cols(
                odd_w_in[j], (512, 64, 64, IDX_HEADS * IDX_DIM, IDX_DIM, IDX_HEADS,
                              MLA_Q_RANK, MLA_KV_RANK, MLA_ROPE))
            w_cat = jnp.concatenate([
                qc * (att_scale * LOG2E), _dup64(kc), _dup64(vc), qi * (IDX_DIM ** -0.5), _dup64(ki),
                _pad_cols(wi * (IDX_HEADS ** -0.5), LANE), cq, ckv,
                _rope_block(kr), _rope_block(_rope_partner(kr))], axis=1).astype(BF16)
            dq = MLA_NOPE + MLA_ROPE
            wuq = odd_w_uq[j].reshape(MLA_Q_RANK, D_HEADS, dq)
            pad_q = jnp.zeros((MLA_Q_RANK, D_HEADS, LANE - dq), F32)
            wqa = jnp.concatenate([wuq, pad_q], axis=2).reshape(MLA_Q_RANK, D_HEADS * LANE).astype(BF16)
            wqb = jnp.concatenate([jnp.zeros((MLA_Q_RANK, D_HEADS, MLA_NOPE), F32),
                                   _rope_partner(wuq[..., MLA_NOPE:]), pad_q],
                                  axis=2).reshape(MLA_Q_RANK, D_HEADS * LANE).astype(BF16)
            wukv = odd_w_ukv[j].reshape(MLA_KV_RANK, D_HEADS, MLA_NOPE + MLA_V)
            wk = jnp.concatenate([wukv[..., :MLA_NOPE], jnp.zeros((MLA_KV_RANK, D_HEADS, LANE - MLA_NOPE), F32)],
                                 axis=2).reshape(MLA_KV_RANK, D_HEADS * LANE).astype(BF16)
            wv = wukv[..., MLA_NOPE:].reshape(MLA_KV_RANK, D_HEADS * MLA_V).astype(BF16)
            half = MLA_ROPE // 2
            inv = ROPE_THETA ** (-jnp.arange(half, dtype=F32) / half)
            ang = jnp.arange(s, dtype=F32)[:, None] * inv[None, :]
            cos_h, sin_h = jnp.cos(ang), jnp.sin(ang)
            pad_t = jnp.zeros((s, LANE - dq), F32)
            cos_t = jnp.concatenate([jnp.ones((s, MLA_NOPE), F32), cos_h, cos_h, pad_t], axis=1)
            sin_t = jnp.concatenate([jnp.zeros((s, MLA_NOPE), F32), sin_h, sin_h, pad_t], axis=1)
            tq_t, tk_t = _alibi_tables(s)
            qc_, kc_, vc_, qi_, ki_, wi_, q8, k8, v8 = _odd_proj(
                xt, w_cat, odd_q_norm[j].reshape(1, -1), odd_kv_norm[j].reshape(1, -1),
                wqa, wqb, wk, wv, cos_t, sin_t, tq_t, tk_t, s, tm)
            r3 = lambda a: a.reshape(b, s, a.shape[1])
            out_a = _dsa(r3(qc_), r3(kc_), r3(vc_), r3(qi_), r3(ki_), r3(wi_).transpose(0, 2, 1))
            out_b = _flash(r3(q8), r3(k8), r3(v8), tq=tq, nh=8, name="mla_attn")
        half_w = out_a.shape[2]
        wo = w_o[l].astype(BF16)
        x1e, route_t = _outproj(out_a.reshape(n, half_w), out_b.reshape(n, -1), xt,
                                wo[:half_w], wo[half_w:], ln_g[l, 0].reshape(1, d), ln_b[l, 0].reshape(1, d),
                                rwt, rb, alpha, tm)
        f = moe_w_gate.shape[3]

        def group_in(w):
            w = w.astype(BF16).reshape(N_GROUPS, EXPERTS_PER_GROUP, d, f)
            return w.transpose(0, 2, 1, 3).reshape(N_GROUPS, d, EXPERTS_PER_GROUP * f)

        wd_g = moe_w_down[l].astype(BF16).reshape(N_GROUPS, EXPERTS_PER_GROUP * f, d)
        xt = _moe_layer(x1e, route_t, group_in(moe_w_gate[l]), group_in(moe_w_up[l]), wd_g,
                        ln_g[l, 1].reshape(1, d), ln_b[l, 1].reshape(1, d), alpha, tm_e, tm_c)
    return xt.reshape(b, s, d)
```

```python
import functools
import math

import jax
import jax.numpy as jnp
from jax import lax
from jax.experimental import pallas as pl
from jax.experimental.pallas import tpu as pltpu

F32 = jnp.float32
BF16 = jnp.bfloat16
I32 = jnp.int32

LANE = 128
HEAD_DIM = 64
BLOCK = 128
A_HEADS, A_KV_HEADS, WINDOW = 8, 2, 128
B_HEADS = 8
C_HEADS, IDX_HEADS, IDX_DIM, DSA_TOPK_MAX = 8, 4, 64, 256
D_HEADS, MLA_Q_RANK, MLA_KV_RANK, MLA_NOPE, MLA_ROPE, MLA_V = 8, 256, 128, 64, 32, 64
ROPE_THETA = 10000.0
N_EXPERTS, N_GROUPS = 16, 4
EXPERTS_PER_GROUP = N_EXPERTS // N_GROUPS
LN_EPS, RMS_EPS = 1e-5, 1e-6
NEG_INF = float("-inf")
LOG2E = math.log2(math.e)
VMEM_LIMIT = 56 * 1024 * 1024


def _cparams(sem):
    return pltpu.CompilerParams(dimension_semantics=sem, vmem_limit_bytes=VMEM_LIMIT)


def _dot(a, b):
    return jnp.dot(a, b, preferred_element_type=F32)


def _dot_nt(a, b):
    return lax.dot_general(a, b, (((1,), (1,)), ((), ())), preferred_element_type=F32)


def _lane_lo(shape):
    return lax.broadcasted_iota(I32, shape, len(shape) - 1) % LANE < HEAD_DIM


def _layer_norm(y, g, b):
    mu = jnp.mean(y, axis=-1, keepdims=True)
    yc = y - mu
    var = jnp.mean(yc * yc, axis=-1, keepdims=True)
    return yc * lax.rsqrt(var + LN_EPS) * g + b


def _top16(v):
    bits = lax.bitcast_convert_type(v, I32) & jnp.int32(-65536)
    return lax.bitcast_convert_type(bits, F32)


def _split3(x):
    s1 = _top16(x)
    r = x - s1
    s2 = _top16(r)
    return s1, s2, r - s2


def _aug_rel(shape, h):
    lane = lax.broadcasted_iota(I32, shape, 1)
    return lane - (HEAD_DIM if h % 2 == 0 else 0)


def _even_proj_kernel(x_ref, w_ref, bf_ref, qa_ref, ka_ref, va_ref, qf_ref, kf_ref, vf_ref,
                      carry_ref, *, tiles_per_seq):
    t = pl.program_id(0)
    tm = x_ref.shape[0]
    h = _dot(x_ref[...].astype(BF16), w_ref[...])
    qa_ref[...] = h[:, 0:512].astype(BF16)
    ka_ref[...] = h[:, 512:768].astype(BF16)
    va_ref[...] = h[:, 768:1024].astype(BF16)
    z = h[:, 2560:2688] + bf_ref[...]
    logf = jnp.minimum(z, 0.0) - jnp.log1p(jnp.exp(-jnp.abs(z)))

    @pl.when(t % tiles_per_seq == 0)
    def _():
        carry_ref[...] = jnp.zeros_like(carry_ref)

    row = lax.broadcasted_iota(I32, (tm, tm), 0)
    col = lax.broadcasted_iota(I32, (tm, tm), 1)
    tri = jnp.where(row >= col, 1.0, 0.0).astype(BF16)
    c = carry_ref[...]
    for part in _split3(logf):
        c = c + _dot(tri, part.astype(BF16))
    carry_ref[...] = c[tm - 1:tm, :]

    s1, s2, s3 = _split3(c * LOG2E)
    lo = _lane_lo((tm, LANE))
    for hd in range(B_HEADS):
        own = lo if hd % 2 == 0 else ~lo
        rel = _aug_rel((tm, LANE), hd)
        b1, b2, b3 = s1[:, hd:hd + 1], s2[:, hd:hd + 1], s3[:, hd:hd + 1]
        q_aug = jnp.where(rel == 3, b1, jnp.where(rel == 4, b2, jnp.where(rel == 5, b3,
                          jnp.where((rel >= 0) & (rel < 3), -1.0, 0.0))))
        k_aug = jnp.where(rel == 0, b1, jnp.where(rel == 1, b2, jnp.where(rel == 2, b3,
                          jnp.where((rel >= 3) & (rel < 6), 1.0, 0.0))))
        pj = (hd // 2) * LANE
        sl = slice(hd * LANE, (hd + 1) * LANE)
        qf_ref[:, sl] = jnp.where(own, h[:, 1024 + pj:1024 + pj + LANE], q_aug).astype(BF16)
        kf_ref[:, sl] = jnp.where(own, h[:, 1536 + pj:1536 + pj + LANE], k_aug).astype(BF16)
        vf_ref[:, sl] = jnp.where(own, h[:, 2048 + pj:2048 + pj + LANE], 1.0).astype(BF16)


def _even_proj(xt, w_cat, bf_pad, seq, tm):
    n, d = xt.shape
    wcols = w_cat.shape[1]
    widths = (512, 256, 256, 1024, 1024, 1024)
    out_shape = [jax.ShapeDtypeStruct((n, w), BF16) for w in widths]
    out_specs = [pl.BlockSpec((tm, w), lambda i: (i, 0)) for w in widths]
    return pl.pallas_call(
        functools.partial(_even_proj_kernel, tiles_per_seq=seq // tm),
        out_shape=out_shape,
        grid=(n // tm,),
        in_specs=[pl.BlockSpec((tm, d), lambda i: (i, 0)),
                  pl.BlockSpec((d, wcols), lambda i: (0, 0)),
                  pl.BlockSpec((1, LANE), lambda i: (0, 0))],
        out_specs=out_specs,
        scratch_shapes=[pltpu.VMEM((1, LANE), F32)],
        compiler_params=_cparams(("arbitrary",)),
        name="even_proj",
    )(xt, w_cat, bf_pad)


def _swa_kernel(sink_ref, q_ref, kp_ref, kc_ref, vp_ref, vc_ref, o_ref):
    i = pl.program_id(1)
    group = A_HEADS // A_KV_HEADS
    tl = lax.broadcasted_iota(I32, (BLOCK, 2 * BLOCK), 0)
    sl = lax.broadcasted_iota(I32, (BLOCK, 2 * BLOCK), 1)
    dist = BLOCK + tl - sl
    valid = (dist >= 0) & (dist < WINDOW) & ((sl >= BLOCK) | (i > 0))
    distf = dist.astype(F32)
    lo = _lane_lo((BLOCK, LANE))

    def band(prev_ref, cur_ref, g):
        return jnp.concatenate([prev_ref[:, g * LANE:(g + 1) * LANE], cur_ref[:, g * LANE:(g + 1) * LANE]], axis=0)

    stacked = []
    for g in range(A_KV_HEADS):
        qs = []
        for h in range(g * group, (g + 1) * group):
            qp = q_ref[:, (h // 2) * LANE:(h // 2 + 1) * LANE]
            qs.append(jnp.where(lo if h % 2 == 0 else ~lo, qp, jnp.zeros_like(qp)))
        stacked.append(_dot_nt(jnp.concatenate(qs, axis=0), band(kp_ref, kc_ref, g)))
    ps, denoms = [], []
    for h in range(A_HEADS):
        r = h % group
        slope = 2.0 ** (-8.0 * (h + 1) / A_HEADS) * LOG2E
        logits = jnp.where(valid, stacked[h // group][r * BLOCK:(r + 1) * BLOCK] - slope * distf, NEG_INF)
        sink = sink_ref[h] * LOG2E
        m = jnp.maximum(jnp.max(logits, axis=-1, keepdims=True), sink)
        e = jnp.exp2(logits - m)
        denoms.append(jnp.sum(e, axis=-1, keepdims=True) + jnp.exp2(sink - m))
        ps.append(e.astype(BF16))
    outs = []
    for g in range(A_KV_HEADS):
        pv = _dot(jnp.concatenate(ps[g * group:(g + 1) * group], axis=0), band(vp_ref, vc_ref, g))
        for r in range(group):
            outs.append(pv[r * BLOCK:(r + 1) * BLOCK] / denoms[g * group + r])
    for j in range(A_HEADS // 2):
        o_ref[:, j * LANE:(j + 1) * LANE] = jnp.where(lo, outs[2 * j], outs[2 * j + 1]).astype(BF16)


def _swa(sinks, qa, ka, va):
    b, s, _ = qa.shape
    nb = s // BLOCK
    prev = lambda bi, i: (bi, jnp.maximum(i - 1, 0), 0)
    cur = lambda bi, i: (bi, i, 0)
    return pl.pallas_call(
        _swa_kernel,
        out_shape=jax.ShapeDtypeStruct((b, s, A_HEADS * HEAD_DIM), BF16),
        grid=(b, nb),
        in_specs=[pl.BlockSpec(memory_space=pltpu.SMEM),
                  pl.BlockSpec((None, BLOCK, 512), cur),
                  pl.BlockSpec((None, BLOCK, 256), prev),
                  pl.BlockSpec((None, BLOCK, 256), cur),
                  pl.BlockSpec((None, BLOCK, 256), prev),
                  pl.BlockSpec((None, BLOCK, 256), cur)],
        out_specs=pl.BlockSpec((None, BLOCK, 512), cur),
        compiler_params=_cparams(("parallel", "arbitrary")),
        name="swa_attn",
    )(sinks, qa, ka, ka, va, va)


def _normalize_pairs(accs, o_ref):
    lo = _lane_lo(accs[0].shape)
    outs = [a / pltpu.roll(a, HEAD_DIM, 1) for a in accs]
    for j in range(len(accs) // 2):
        o_ref[:, j * LANE:(j + 1) * LANE] = jnp.where(lo, outs[2 * j], outs[2 * j + 1]).astype(BF16)


def _online_softmax_chains(qs, ks, vs, carry, bias=None, masks=None, guard_empty=False):
    n = len(qs)
    ss = [_dot_nt(qs[i], ks[i]).astype(BF16) for i in range(n)]
    if bias is not None:
        bias16 = bias.astype(BF16)
        ss = [s + bias16 for s in ss]
    if masks is not None:
        ss = [jnp.where(masks[i], ss[i], NEG_INF) for i in range(n)]
    ms, ps, alphas = [], [], []
    for i in range(n):
        m = carry[i][0]
        m_new = jnp.maximum(m, jnp.max(ss[i], axis=-1, keepdims=True).astype(F32))
        m_ref = jnp.where(m_new == NEG_INF, 0.0, m_new) if guard_empty else m_new
        ps.append(jnp.exp2(ss[i] - m_ref.astype(BF16)))
        alphas.append(jnp.exp2(m - m_ref))
        ms.append(m_new)
    pvs = [_dot(ps[i], vs[i]) for i in range(n)]
    return tuple((ms[i], alphas[i] * carry[i][1] + pvs[i]) for i in range(n))


def _flash_kernel(q_ref, k_ref, v_ref, o_ref, *, tq, nh):
    qi = pl.program_id(2)
    nslab = tq // BLOCK
    chains = [(hh, r) for hh in range(nh) for r in range(nslab)]
    row = lax.broadcasted_iota(I32, (BLOCK, tq), 0)
    col = lax.broadcasted_iota(I32, (BLOCK, tq), 1)

    def lanes(hh):
        return slice(hh * LANE, (hh + 1) * LANE)

    def step(kb, carry, diagonal):
        start = pl.multiple_of(kb * tq, tq)
        qs = [q_ref[r * BLOCK:(r + 1) * BLOCK, lanes(hh)] for hh, r in chains]
        ks = [k_ref[pl.ds(start, tq), lanes(hh)] for hh, r in chains]
        vs = [v_ref[pl.ds(start, tq), lanes(hh)] for hh, r in chains]
        masks = [col <= row + r * BLOCK for hh, r in chains] if diagonal else None
        return _online_softmax_chains(qs, ks, vs, carry, masks=masks)

    init = tuple((jnp.full((BLOCK, 1), NEG_INF, F32), jnp.zeros((BLOCK, LANE), F32))
                 for _ in range(nh * nslab))
    carry = lax.fori_loop(0, qi, lambda kb, c: step(kb, c, False), init)
    carry = step(qi, carry, True)
    accs = [jnp.concatenate([carry[hh * nslab + r][1] for r in range(nslab)], axis=0) for hh in range(nh)]
    _normalize_pairs(accs, o_ref)


def _flash(q, k, v, *, tq, nh, name):
    b, s, w = v.shape
    return pl.pallas_call(
        functools.partial(_flash_kernel, tq=tq, nh=nh),
        out_shape=jax.ShapeDtypeStruct((b, s, w // 2), BF16),
        grid=(b, w // (nh * LANE), s // tq),
        in_specs=[pl.BlockSpec((None, tq, nh * LANE), lambda bi, j, i: (bi, i, j)),
                  pl.BlockSpec((None, s, nh * LANE), lambda bi, j, i: (bi, 0, j)),
                  pl.BlockSpec((None, s, nh * LANE), lambda bi, j, i: (bi, 0, j))],
        out_specs=pl.BlockSpec((None, tq, nh * LANE // 2), lambda bi, j, i: (bi, i, j)),
        compiler_params=_cparams(("parallel", "parallel", "arbitrary")),
        name=name,
    )(q, k, v)


def _route_rows(aff, sel):
    gs = []
    for g in range(N_GROUPS):
        r = sel[g * EXPERTS_PER_GROUP:(g + 1) * EXPERTS_PER_GROUP]
        best = None
        for a in range(EXPERTS_PER_GROUP):
            for c in range(a + 1, EXPERTS_PER_GROUP):
                pair = r[a] + r[c]
                best = pair if best is None else jnp.maximum(best, pair)
        gs.append(best)
    gmax = functools.reduce(jnp.maximum, gs)
    gidx = jnp.full_like(gmax, float(N_GROUPS - 1))
    for g in range(N_GROUPS - 2, -1, -1):
        gidx = jnp.where(gs[g] == gmax, float(g), gidx)

    def pick(rows, k):
        out = rows[(N_GROUPS - 1) * EXPERTS_PER_GROUP + k]
        for g in range(N_GROUPS - 2, -1, -1):
            out = jnp.where(gidx == float(g), rows[g * EXPERTS_PER_GROUP + k], out)
        return out

    s = [pick(sel, k) for k in range(EXPERTS_PER_GROUP)]
    a = [pick(aff, k) for k in range(EXPERTS_PER_GROUP)]

    def first_max(vals):
        mx = functools.reduce(jnp.maximum, vals)
        idx = jnp.full_like(mx, float(len(vals) - 1))
        for k in range(len(vals) - 2, -1, -1):
            idx = jnp.where(vals[k] == mx, float(k), idx)
        return idx

    def take(vals, idx):
        out = vals[-1]
        for k in range(len(vals) - 2, -1, -1):
            out = jnp.where(idx == float(k), vals[k], out)
        return out

    i1 = first_max(s)
    s2 = [jnp.where(i1 == float(k), NEG_INF, s[k]) for k in range(EXPERTS_PER_GROUP)]
    i2 = first_max(s2)
    a1, a2 = take(a, i1), take(a, i2)
    tot = a1 + a2
    base = gidx * float(EXPERTS_PER_GROUP)
    return base + i1, base + i2, a1 / tot, a2 / tot


def _outproj_kernel(a_ref, b_ref, x_ref, wa_ref, wb_ref, g_ref, be_ref, rwt_ref, rb_ref,
                    x1_ref, route_ref, *, alpha):
    d = x_ref.shape[1]
    tm = x_ref.shape[0]
    mix = _dot(a_ref[...], wa_ref[...]) + _dot(b_ref[...], wb_ref[...])
    x1 = _layer_norm(alpha * x_ref[...] + mix, g_ref[...], be_ref[...])
    x1_ref[:, 0:d] = x1
    x_hi = x1.astype(BF16)
    x_lo = (x1 - x_hi.astype(F32)).astype(BF16)
    w = rwt_ref[...]
    w_hi = w.astype(BF16)
    w_lo = (w - w_hi.astype(F32)).astype(BF16)
    z_hi = _dot_nt(jnp.concatenate([w_hi, w_lo], axis=0), x_hi)
    zt = z_hi[0:N_EXPERTS] + z_hi[N_EXPERTS:2 * N_EXPERTS] + _dot_nt(w_hi, x_lo)
    afft = 1.0 / (1.0 + jnp.exp(-zt))
    selt = afft + rb_ref[...]
    aff = [afft[e:e + 1, :] for e in range(N_EXPERTS)]
    sel = [selt[e:e + 1, :] for e in range(N_EXPERTS)]
    e1, e2, w1, w2 = _route_rows(aff, sel)
    rec = jnp.concatenate([e1, e2, w1, w2, jnp.zeros((LANE - 4, tm), F32)], axis=0)
    route_ref[...] = rec[0:8]
    x1_ref[:, d:d + LANE] = rec.T


def _outproj(a, b, xt, wa, wb, g, be, rwt, rb, alpha, tm):
    n, d = xt.shape
    full = lambda i: (0, 0)
    return pl.pallas_call(
        functools.partial(_outproj_kernel, alpha=alpha),
        out_shape=[jax.ShapeDtypeStruct((n, d + LANE), F32), jax.ShapeDtypeStruct((8, n), F32)],
        grid=(n // tm,),
        in_specs=[pl.BlockSpec((tm, a.shape[1]), lambda i: (i, 0)),
                  pl.BlockSpec((tm, b.shape[1]), lambda i: (i, 0)),
                  pl.BlockSpec((tm, d), lambda i: (i, 0)),
                  pl.BlockSpec(wa.shape, full), pl.BlockSpec(wb.shape, full),
                  pl.BlockSpec((1, d), full), pl.BlockSpec((1, d), full),
                  pl.BlockSpec(rwt.shape, full), pl.BlockSpec(rb.shape, full)],
        out_specs=[pl.BlockSpec((tm, d + LANE), lambda i: (i, 0)), pl.BlockSpec((8, tm), lambda i: (0, i))],
        compiler_params=_cparams(("parallel",)),
        name="outproj_ln_route",
    )(a, b, xt, wa, wb, g, be, rwt, rb)


def _moe_kernel(bgrp_ref, bbase_ref, nact_ref, src_ref,
                x_hbm, wg_ref, wu_ref, wd_ref, o_ref, xbuf0, xbuf1, sem, *, tm, d):
    i = pl.program_id(0)
    nact = nact_ref[0]
    last = src_ref.shape[0] - 1
    bufs = (xbuf0, xbuf1)

    def issue(blk, slot):
        base = bbase_ref[blk]
        for r in range(tm):
            tok = src_ref[jnp.minimum(base + r, last)]
            pltpu.make_async_copy(x_hbm.at[pl.ds(tok, 1), :], bufs[slot].at[pl.ds(r, 1), :],
                                  sem.at[slot]).start(priority=r % 2)

    def wait(slot):
        pltpu.make_async_copy(x_hbm.at[pl.ds(0, tm), :], bufs[slot], sem.at[slot]).wait()

    @pl.when((i == 0) & (nact > 0))
    def _():
        issue(0, 0)

    def step(slot):
        wait(slot)

        @pl.when(i + 1 < nact)
        def _():
            issue(i + 1, 1 - slot)

        buf = bufs[slot]
        xb = buf[:, 0:d].astype(BF16)
        rec = buf[:, d:d + LANE]
        e1, e2, w1, w2 = rec[:, 0:1], rec[:, 1:2], rec[:, 2:3], rec[:, 3:4]
        fg = wg_ref.shape[1]
        lane_expert = lax.broadcasted_iota(I32, (tm, fg), 1) // (fg // EXPERTS_PER_GROUP)
        expert = (lane_expert + bgrp_ref[i] * EXPERTS_PER_GROUP).astype(F32)
        gate_w = jnp.where(expert == e1, w1, jnp.where(expert == e2, w2, 0.0))
        gate = _dot(xb, wg_ref[...])
        up = _dot(xb, wu_ref[...])
        h = gate * (1.0 / (1.0 + jnp.exp(-gate))) * up * gate_w
        o_ref[...] = _dot(h.astype(BF16), wd_ref[...])

    for slot in range(2):
        pl.when((i < nact) & (i % 2 == slot))(functools.partial(step, slot))

    @pl.when(i >= nact)
    def _():
        o_ref[...] = jnp.zeros_like(o_ref)


def _moe_experts(x1e, wg, wu, wd, bgrp, bbase, nact, src, tm):
    n, de = x1e.shape
    d = de - LANE
    nblk = bgrp.shape[0]
    fg = wg.shape[2]
    wmap = lambda i, bgrp, *_: (bgrp[i], 0, 0)
    grid_spec = pltpu.PrefetchScalarGridSpec(
        num_scalar_prefetch=4,
        grid=(nblk,),
        in_specs=[pl.BlockSpec(memory_space=pl.ANY),
                  pl.BlockSpec((None, d, fg), wmap),
                  pl.BlockSpec((None, d, fg), wmap),
                  pl.BlockSpec((None, fg, d), wmap)],
        out_specs=pl.BlockSpec((tm, d), lambda i, *_: (i, 0)),
        scratch_shapes=[pltpu.VMEM((tm, de), F32), pltpu.VMEM((tm, de), F32), pltpu.SemaphoreType.DMA((2,))],
    )
    return pl.pallas_call(
        functools.partial(_moe_kernel, tm=tm, d=d),
        out_shape=jax.ShapeDtypeStruct((nblk * tm, d), F32),
        grid_spec=grid_spec,
        compiler_params=_cparams(("arbitrary",)),
        name="moe_experts",
    )(bgrp, bbase, nact, src, x1e, wg, wu, wd)


def _combine_kernel(p_ref, x_ref, g_ref, be_ref, o_hbm, y_ref, obuf0, obuf1, sem, *, alpha, tm, d):
    i = pl.program_id(0)
    nsteps = pl.num_programs(0)

    bufs = (obuf0, obuf1)

    def issue(blk, slot):
        base = blk * tm
        for r in range(tm):
            pltpu.make_async_copy(o_hbm.at[pl.ds(p_ref[base + r], 1), :],
                                  bufs[slot].at[pl.ds(r, 1), :], sem.at[slot]).start(priority=r % 2)

    def wait(slot):
        pltpu.make_async_copy(o_hbm.at[pl.ds(0, tm), :], bufs[slot], sem.at[slot]).wait()

    @pl.when(i == 0)
    def _():
        issue(0, 0)

    def step(slot):
        wait(slot)

        @pl.when(i + 1 < nsteps)
        def _():
            issue(i + 1, 1 - slot)

        y = alpha * x_ref[:, 0:d] + bufs[slot][...]
        y_ref[...] = _layer_norm(y, g_ref[...], be_ref[...])

    for slot in range(2):
        pl.when(i % 2 == slot)(functools.partial(step, slot))


def _combine(x1e, g, be, o, pos, alpha, tm):
    n, de = x1e.shape
    d = de - LANE
    grid_spec = pltpu.PrefetchScalarGridSpec(
        num_scalar_prefetch=1,
        grid=(n // tm,),
        in_specs=[pl.BlockSpec((tm, de), lambda i, *_: (i, 0)),
                  pl.BlockSpec((1, d), lambda i, *_: (0, 0)),
                  pl.BlockSpec((1, d), lambda i, *_: (0, 0)),
                  pl.BlockSpec(memory_space=pl.ANY)],
        out_specs=pl.BlockSpec((tm, d), lambda i, *_: (i, 0)),
        scratch_shapes=[pltpu.VMEM((tm, d), F32), pltpu.VMEM((tm, d), F32), pltpu.SemaphoreType.DMA((2,))],
    )
    return pl.pallas_call(
        functools.partial(_combine_kernel, alpha=alpha, tm=tm, d=d),
        out_shape=jax.ShapeDtypeStruct((n, d), F32),
        grid_spec=grid_spec,
        compiler_params=_cparams(("arbitrary",)),
        name="moe_combine_ln",
    )(pos, x1e, g, be, o)


def _moe_layer(x1e, route_t, wg, wu, wd, g, be, alpha, tm_e, tm_c):
    n = x1e.shape[0]
    grp = route_t[0].astype(I32) // EXPERTS_PER_GROUP
    onehot = (grp[:, None] == jnp.arange(N_GROUPS, dtype=I32)[None, :]).astype(I32)
    rank = jnp.take_along_axis(jnp.cumsum(onehot, axis=0) - onehot, grp[:, None], axis=1)[:, 0]
    counts = jnp.sum(onehot, axis=0)
    nblk_g = (counts + tm_e - 1) // tm_e
    blk_start = jnp.cumsum(nblk_g) - nblk_g
    seg_start = jnp.cumsum(counts) - counts
    pos = (blk_start[grp] * tm_e + rank).astype(I32)
    src = jnp.argsort(grp, stable=True).astype(I32)
    nblk = n // tm_e + N_GROUPS
    blk = jnp.arange(nblk, dtype=I32)
    blk_end = jnp.cumsum(nblk_g)
    bgrp = jnp.minimum(jnp.sum((blk[:, None] >= blk_end[None, :]).astype(I32), axis=1), N_GROUPS - 1)
    bbase = (seg_start[bgrp] + (blk - blk_start[bgrp]) * tm_e).astype(I32)
    nact = jnp.sum(nblk_g).astype(I32).reshape(1)
    o = _moe_experts(x1e, wg, wu, wd, bgrp, bbase, nact, src, tm_e)
    return _combine(x1e, g, be, o, pos, alpha, tm_c)


def _rms(x, g):
    return x * lax.rsqrt(jnp.mean(x * x, axis=-1, keepdims=True) + RMS_EPS) * g

ODD_OFF = dict(qc=0, kc=512, vc=640, qi=768, ki=1024, wi=1152, cq=1280, ckv=1536, kra=1664, krb=1792, end=1920)


def _odd_proj_kernel(x_ref, w_ref, qng_ref, kvng_ref, wqa_ref, wqb_ref, wk_ref, wv_ref, cos_ref, sin_ref,
                     tq_ref, tk_ref,
                     qc_ref, kc_ref, vc_ref, qi_ref, ki_ref, wi_ref, q8_ref, k8_ref, v8_ref, *, q_scale):
    o = ODD_OFF
    tm = x_ref.shape[0]
    lo = _lane_lo((tm, LANE))
    h = _dot(x_ref[...].astype(BF16), w_ref[...])
    kdup = h[:, o["kc"]:o["vc"]]
    vdup = h[:, o["vc"]:o["qi"]]
    kc_ref[:, 0:LANE] = jnp.where(lo, kdup, tk_ref[:, 0:LANE].astype(F32)).astype(BF16)
    kc_ref[:, LANE:2 * LANE] = jnp.where(lo, tk_ref[:, LANE:2 * LANE].astype(F32), kdup).astype(BF16)
    vc_ref[:, 0:LANE] = jnp.where(lo, vdup, 1.0).astype(BF16)
    vc_ref[:, LANE:2 * LANE] = jnp.where(lo, 1.0, vdup).astype(BF16)
    qi_ref[...] = h[:, o["qi"]:o["ki"]].astype(BF16)
    ki_ref[...] = h[:, o["ki"]:o["wi"]].astype(BF16)
    wi_ref[...] = h[:, o["wi"]:o["wi"] + IDX_HEADS]
    cos = cos_ref[...]
    sin = sin_ref[...]
    cqn = _rms(h[:, o["cq"]:o["ckv"]], qng_ref[...]).astype(BF16)
    qa = _dot(cqn, wqa_ref[...])
    qb = _dot(cqn, wqb_ref[...])
    ckvn = _rms(h[:, o["ckv"]:o["kra"]], kvng_ref[...]).astype(BF16)
    kn = _dot(ckvn, wk_ref[...])
    v8 = _dot(ckvn, wv_ref[...])
    kr = h[:, o["kra"]:o["krb"]] * cos + h[:, o["krb"]:o["end"]] * sin
    for hd in range(D_HEADS):
        sl = slice(hd * LANE, (hd + 1) * LANE)
        pj = slice((hd // 2) * LANE, (hd // 2 + 1) * LANE)
        own = lo if hd % 2 == 0 else ~lo
        qc_ref[:, sl] = jnp.where(own, h[:, pj], tq_ref[:, sl].astype(F32)).astype(BF16)
        q8_ref[:, sl] = ((qa[:, sl] * cos + qb[:, sl] * sin) * q_scale).astype(BF16)
        k8_ref[:, sl] = (kn[:, sl] + kr).astype(BF16)
        v8_ref[:, sl] = jnp.where(own, v8[:, pj], 1.0).astype(BF16)


def _odd_proj(xt, w_cat, qng, kvng, wqa, wqb, wk, wv, cos_t, sin_t, tq_t, tk_t, seq, tm):
    n, d = xt.shape
    full = lambda i: (0, 0)
    tps = seq // tm
    pos = lambda i: (i % tps, 0)
    widths = (1024, 256, 256, 256, 128)
    out_shape = [jax.ShapeDtypeStruct((n, w), BF16) for w in widths]
    out_specs = [pl.BlockSpec((tm, w), lambda i: (i, 0)) for w in widths]
    out_shape.append(jax.ShapeDtypeStruct((n, IDX_HEADS), F32))
    out_specs.append(pl.BlockSpec((tm, IDX_HEADS), lambda i: (i, 0)))
    for w in (1024, 1024, 1024):
        out_shape.append(jax.ShapeDtypeStruct((n, w), BF16))
        out_specs.append(pl.BlockSpec((tm, w), lambda i: (i, 0)))
    return pl.pallas_call(
        functools.partial(_odd_proj_kernel, q_scale=(MLA_NOPE + MLA_ROPE) ** -0.5 * LOG2E),
        out_shape=out_shape,
        grid=(n // tm,),
        in_specs=[pl.BlockSpec((tm, d), lambda i: (i, 0)),
                  pl.BlockSpec(w_cat.shape, full),
                  pl.BlockSpec(qng.shape, full), pl.BlockSpec(kvng.shape, full),
                  pl.BlockSpec(wqa.shape, full), pl.BlockSpec(wqb.shape, full),
                  pl.BlockSpec(wk.shape, full), pl.BlockSpec(wv.shape, full),
                  pl.BlockSpec((tm, LANE), pos), pl.BlockSpec((tm, LANE), pos),
                  pl.BlockSpec((tm, C_HEADS * LANE), pos), pl.BlockSpec((tm, 2 * LANE), pos)],
        out_specs=out_specs,
        compiler_params=_cparams(("parallel",)),
        name="odd_proj",
    )(xt, w_cat, qng, kvng, wqa, wqb, wk, wv, cos_t, sin_t, tq_t, tk_t)


def _dsa_kernel(q_ref, k_ref, v_ref, qi_ref, ki_ref, wt_ref, o_ref, sc_ref, mb_ref, *, top_k, chunk):
    i = pl.program_id(1)
    nc = (i * BLOCK + BLOCK + chunk - 1) // chunk
    lo = _lane_lo((BLOCK, LANE))
    s_row2 = lax.broadcasted_iota(I32, (2 * chunk, BLOCK), 0)
    t_lane2 = i * BLOCK + lax.broadcasted_iota(I32, (2 * chunk, BLOCK), 1)

    def chunk_start(c):
        return pl.multiple_of(c * chunk, chunk)

    qidx = []
    for h in range(IDX_HEADS):
        qp = qi_ref[:, (h // 2) * LANE:(h // 2 + 1) * LANE]
        qidx.append(jnp.where(lo if h % 2 == 0 else ~lo, qp, jnp.zeros_like(qp)))
    q_stack = jnp.concatenate(qidx, axis=0)
    w = wt_ref[...]

    npair = (nc + 1) // 2

    def pair_start(cp):
        return cp * (2 * chunk)

    def select(pairs):
        def score_pair(cp):
            start = pair_start(cp)
            d = _dot_nt(ki_ref[pl.ds(start, 2 * chunk), :], q_stack)
            sc = w[0:1, :] * jnp.maximum(d[:, 0:BLOCK], 0.0)
            for h in range(1, IDX_HEADS):
                sc = sc + w[h:h + 1, :] * jnp.maximum(d[:, h * BLOCK:(h + 1) * BLOCK], 0.0)
            sc = jnp.where(sc == 0.0, 0.0, sc)
            sc_ref[pl.ds(start, 2 * chunk), :] = jnp.where(start + s_row2 <= t_lane2, sc, NEG_INF)

        for cp in range(pairs):
            score_pair(cp)

        nacc = 4 * 8

        def count(trial, strict):
            acc = jnp.zeros((nacc, BLOCK), F32)
            for cp in range(pairs):
                sc = sc_ref[pl.ds(pair_start(cp), 2 * chunk), :]
                hit = (sc > trial) if strict else (sc >= trial)
                acc = acc + jnp.sum(jnp.where(hit, 1.0, 0.0).reshape(2 * chunk // nacc, nacc, BLOCK), axis=0)
            return jnp.sum(acc, axis=0, keepdims=True)

        def decode(key):
            return lax.bitcast_convert_type(jnp.where(key < 0, key ^ jnp.int32(0x7FFFFFFF), key), F32)

        kf = float(top_k)
        key_neg_inf = jnp.int32(-2139095041)
        cand = jnp.where(count(jnp.zeros((1, BLOCK), F32), False) >= kf, 0, jnp.int32(-2 ** 31)).astype(I32)

        def bit_step(j, cand):
            trial = cand + lax.shift_left(jnp.int32(1), 30 - j)
            ok = (trial <= key_neg_inf) | (count(decode(trial), False) >= kf)
            return jnp.where(ok, trial, cand)

        thr = decode(lax.fori_loop(0, 31, bit_step, cand))
        need = kf - count(thr, True)

        r0 = lax.broadcasted_iota(I32, (chunk, chunk), 0)
        r1 = lax.broadcasted_iota(I32, (chunk, chunk), 1)
        below = jnp.where(r1 < r0, 1.0, 0.0).astype(BF16)
        need = jnp.where(thr == NEG_INF, 0.0, need)

        offset = jnp.zeros((1, BLOCK), F32)
        for cp in range(pairs):
            start = pair_start(cp)
            sc = sc_ref[pl.ds(start, 2 * chunk), :]
            eq = sc == thr
            eqf = jnp.where(eq, 1.0, 0.0)
            eqb = eqf.astype(BF16)
            first = jnp.sum(eqf[0:chunk], axis=0, keepdims=True)
            prefix = jnp.concatenate([_dot(below, eqb[0:chunk]) + offset,
                                      _dot(below, eqb[chunk:2 * chunk]) + (offset + first)], axis=0)
            bias = jnp.where(sc > thr, 0.0, jnp.where(eq, jnp.where(prefix < need, 0.0, NEG_INF), NEG_INF))
            mb_ref[:, pl.ds(start, 2 * chunk)] = bias.T
            offset = offset + first + jnp.sum(eqf[chunk:2 * chunk], axis=0, keepdims=True)

    for pairs in range(1, k_ref.shape[0] // (2 * chunk) + 1):
        pl.when(npair == pairs)(functools.partial(select, pairs))

    def att_chunk(c, carry):
        start = chunk_start(c)
        qs = [q_ref[:, h * LANE:(h + 1) * LANE] for h in range(C_HEADS)]
        ks = [k_ref[pl.ds(start, chunk), (h % 2) * LANE:(h % 2 + 1) * LANE] for h in range(C_HEADS)]
        vs = [v_ref[pl.ds(start, chunk), (h % 2) * LANE:(h % 2 + 1) * LANE] for h in range(C_HEADS)]
        return _online_softmax_chains(qs, ks, vs, carry, bias=mb_ref[:, pl.ds(start, chunk)], guard_empty=True)

    init = tuple((jnp.full((BLOCK, 1), NEG_INF, F32), jnp.zeros((BLOCK, LANE), F32)) for _ in range(C_HEADS))
    carry = lax.fori_loop(0, nc, att_chunk, init)
    _normalize_pairs([c[1] for c in carry], o_ref)


def _dsa(qc, kc, vc, qi, ki, wit):
    b, s, _ = qc.shape
    top_k = min(DSA_TOPK_MAX, s // 4)
    chunk = 256
    assert s % (2 * chunk) == 0 and top_k <= chunk
    blk = lambda bi, i: (bi, i, 0)
    seq = lambda bi, i: (bi, 0, 0)
    return pl.pallas_call(
        functools.partial(_dsa_kernel, top_k=top_k, chunk=chunk),
        out_shape=jax.ShapeDtypeStruct((b, s, C_HEADS * HEAD_DIM), BF16),
        grid=(b, s // BLOCK),
        in_specs=[pl.BlockSpec((None, BLOCK, C_HEADS * LANE), blk),
                  pl.BlockSpec((None, s, 2 * LANE), seq),
                  pl.BlockSpec((None, s, 2 * LANE), seq),
                  pl.BlockSpec((None, BLOCK, 256), blk),
                  pl.BlockSpec((None, s, LANE), seq),
                  pl.BlockSpec((None, IDX_HEADS, BLOCK), lambda bi, i: (bi, 0, i))],
        out_specs=pl.BlockSpec((None, BLOCK, 512), blk),
        scratch_shapes=[pltpu.VMEM((s, BLOCK), F32), pltpu.VMEM((BLOCK, s), F32)],
        compiler_params=_cparams(("parallel", "arbitrary")),
        name="dsa_attn",
    )(qc, kc, vc, qi, ki, wit)


def _dup64(w):
    d, c = w.shape
    w = w.reshape(d, c // HEAD_DIM, 1, HEAD_DIM)
    return jnp.broadcast_to(w, (d, c // HEAD_DIM, 2, HEAD_DIM)).reshape(d, 2 * c)


def _pad_cols(w, width):
    return jnp.pad(w, ((0, 0), (0, width - w.shape[1])))


def _split_cols(w, sizes):
    out, start = [], 0
    for n in sizes:
        out.append(w[:, start:start + n])
        start += n
    return out


def _rope_partner(w):
    half = MLA_ROPE // 2
    return jnp.concatenate([-w[..., half:], w[..., :half]], axis=-1)


def _rope_block(w_rope):
    d = w_rope.shape[0]
    return jnp.concatenate([jnp.zeros((d, MLA_NOPE), F32), w_rope,
                            jnp.zeros((d, LANE - MLA_NOPE - MLA_ROPE), F32)], axis=1)


def _alibi_tables(s):
    slopes = 2.0 ** (-8.0 * jnp.arange(1, C_HEADS + 1, dtype=F32) / C_HEADS)
    a = slopes * LOG2E
    pos = jnp.arange(s, dtype=F32)
    a_sp = _split3(a)
    r_sp = _split3(-(a[None, :] * pos[:, None]))
    ones = jnp.ones((s, C_HEADS), F32)
    q_terms = [128.0 * t * ones for t in a_sp] + [t * ones for t in a_sp] + list(r_sp)
    q_aug = jnp.stack(q_terms, axis=-1)
    q_aug = jnp.pad(q_aug, ((0, 0), (0, 0), (0, HEAD_DIM - q_aug.shape[-1])))
    zeros = jnp.zeros_like(q_aug)
    even = jnp.concatenate([zeros, q_aug], axis=-1)
    odd = jnp.concatenate([q_aug, zeros], axis=-1)
    parity = (jnp.arange(C_HEADS) % 2 == 0)[None, :, None]
    tq_t = jnp.where(parity, even, odd).reshape(s, C_HEADS * LANE)
    s_hi = jnp.floor(pos / LANE)
    s_lo = pos - LANE * s_hi
    k_aug = jnp.stack([s_hi] * 3 + [s_lo] * 3 + [jnp.ones_like(pos)] * 3, axis=-1)
    k_aug = jnp.pad(k_aug, ((0, 0), (0, HEAD_DIM - k_aug.shape[-1])))
    kz = jnp.zeros_like(k_aug)
    tk_t = jnp.concatenate([kz, k_aug, k_aug, kz], axis=-1)
    return tq_t.astype(BF16), tk_t.astype(BF16)


def kernel(x, even_w_in, even_b_f, even_sinks, odd_w_in, odd_q_norm, odd_kv_norm, odd_w_uq, odd_w_ukv,
           w_o, ln_g, ln_b, router_w, router_b, moe_w_gate, moe_w_up, moe_w_down):
    b, s, d = x.shape
    n = b * s
    depth = w_o.shape[0]
    alpha = (2 * depth) ** 0.25
    tm = min(512, s)
    tq = min(256, s)
    tm_e = min(512, n)
    tm_c = min(256, n)
    att_scale = HEAD_DIM ** -0.5
    xt = x.reshape(n, d)
    rwt = router_w.T
    rb = router_b.reshape(N_EXPERTS, 1)

    for l in range(depth):
        j = l // 2
        if l % 2 == 0:
            qa, ka, va, qb, kb, vb, fg = _split_cols(
                even_w_in[j], (512, 128, 128, 512, 512, 512, B_HEADS))
            w_cat = jnp.concatenate([qa * (att_scale * LOG2E), _dup64(ka), _dup64(va), qb * (att_scale * LOG2E), kb, vb,
                                     _pad_cols(fg, LANE)], axis=1).astype(BF16)
            bf_pad = _pad_cols(even_b_f[j].reshape(1, B_HEADS), LANE)
            qa_, ka_, va_, qf, kf, vf = _even_proj(xt, w_cat, bf_pad, s, tm)
            r3 = lambda a: a.reshape(b, s, a.shape[1])
            out_a = _swa(even_sinks[j], r3(qa_), r3(ka_), r3(va_))
            out_b = _flash(r3(qf), r3(kf), r3(vf), tq=tq, nh=8, name="fox_attn")
        else:
            (qc, kc, vc, qi, ki, wi, cq, ckv, kr) = _split_cols(
                odd_w_in[j], (512, 64, 64, IDX_HEADS * IDX_DIM, IDX_DIM, IDX_HEADS,
                              MLA_Q_RANK, MLA_KV_RANK, MLA_ROPE))
            w_cat = jnp.concatenate([
                qc * (att_scale * LOG2E), _dup64(kc), _dup64(vc), qi * (IDX_DIM ** -0.5), _dup64(ki),
                _pad_cols(wi * (IDX_HEADS ** -0.5), LANE), cq, ckv,
                _rope_block(kr), _rope_block(_rope_partner(kr))], axis=1).astype(BF16)
            dq = MLA_NOPE + MLA_ROPE
            wuq = odd_w_uq[j].reshape(MLA_Q_RANK, D_HEADS, dq)
            pad_q = jnp.zeros((MLA_Q_RANK, D_HEADS, LANE - dq), F32)
            wqa = jnp.concatenate([wuq, pad_q], axis=2).reshape(MLA_Q_RANK, D_HEADS * LANE).astype(BF16)
            wqb = jnp.concatenate([jnp.zeros((MLA_Q_RANK, D_HEADS, MLA_NOPE), F32),
                                   _rope_partner(wuq[..., MLA_NOPE:]), pad_q],
                                  axis=2).reshape(MLA_Q_RANK, D_HEADS * LANE).astype(BF16)
            wukv = odd_w_ukv[j].reshape(MLA_KV_RANK, D_HEADS, MLA_NOPE + MLA_V)
            wk = jnp.concatenate([wukv[..., :MLA_NOPE], jnp.zeros((MLA_KV_RANK, D_HEADS, LANE - MLA_NOPE), F32)],
                                 axis=2).reshape(MLA_KV_RANK, D_HEADS * LANE).astype(BF16)
            wv = wukv[..., MLA_NOPE:].reshape(MLA_KV_RANK, D_HEADS * MLA_V).astype(BF16)
            half = MLA_ROPE // 2
            inv = ROPE_THETA ** (-jnp.arange(half, dtype=F32) / half)
            ang = jnp.arange(s, dtype=F32)[:, None] * inv[None, :]
            cos_h, sin_h = jnp.cos(ang), jnp.sin(ang)
            pad_t = jnp.zeros((s, LANE - dq), F32)
            cos_t = jnp.concatenate([jnp.ones((s, MLA_NOPE), F32), cos_h, cos_h, pad_t], axis=1)
            sin_t = jnp.concatenate([jnp.zeros((s, MLA_NOPE), F32), sin_h, sin_h, pad_t], axis=1)
            tq_t, tk_t = _alibi_tables(s)
            qc_, kc_, vc_, qi_, ki_, wi_, q8, k8, v8 = _odd_proj(
                xt, w_cat, odd_q_norm[j].reshape(1, -1), odd_kv_norm[j].reshape(1, -1),
                wqa, wqb, wk, wv, cos_t, sin_t, tq_t, tk_t, s, tm)
            r3 = lambda a: a.reshape(b, s, a.shape[1])
            out_a = _dsa(r3(qc_), r3(kc_), r3(vc_), r3(qi_), r3(ki_), r3(wi_).transpose(0, 2, 1))
            out_b = _flash(r3(q8), r3(k8), r3(v8), tq=tq, nh=8, name="mla_attn")
        half_w = out_a.shape[2]
        wo = w_o[l].astype(BF16)
        x1e, route_t = _outproj(out_a.reshape(n, half_w), out_b.reshape(n, -1), xt,
                                wo[:half_w], wo[half_w:], ln_g[l, 0].reshape(1, d), ln_b[l, 0].reshape(1, d),
                                rwt, rb, alpha, tm)
        f = moe_w_gate.shape[3]

        def group_in(w):
            w = w.astype(BF16).reshape(N_GROUPS, EXPERTS_PER_GROUP, d, f)
            return w.transpose(0, 2, 1, 3).reshape(N_GROUPS, d, EXPERTS_PER_GROUP * f)

        wd_g = moe_w_down[l].astype(BF16).reshape(N_GROUPS, EXPERTS_PER_GROUP * f, d)
        xt = _moe_layer(x1e, route_t, group_in(moe_w_gate[l]), group_in(moe_w_up[l]), wd_g,
                        ln_g[l, 1].reshape(1, d), ln_b[l, 1].reshape(1, d), alpha, tm_e, tm_c)
    return xt.reshape(b, s, d)
```

```python
import functools
import math

import jax
import jax.numpy as jnp
from jax import lax
from jax.experimental import pallas as pl
from jax.experimental.pallas import tpu as pltpu

F32 = jnp.float32
BF16 = jnp.bfloat16
I32 = jnp.int32

LANE = 128
HEAD_DIM = 64
BLOCK = 128
A_HEADS, A_KV_HEADS, WINDOW = 8, 2, 128
B_HEADS = 8
C_HEADS, IDX_HEADS, IDX_DIM, DSA_TOPK_MAX = 8, 4, 64, 256
D_HEADS, MLA_Q_RANK, MLA_KV_RANK, MLA_NOPE, MLA_ROPE, MLA_V = 8, 256, 128, 64, 32, 64
ROPE_THETA = 10000.0
N_EXPERTS, N_GROUPS = 16, 4
EXPERTS_PER_GROUP = N_EXPERTS // N_GROUPS
LN_EPS, RMS_EPS = 1e-5, 1e-6
NEG_INF = float("-inf")
LOG2E = math.log2(math.e)
VMEM_LIMIT = 56 * 1024 * 1024


def _cparams(sem):
    return pltpu.CompilerParams(dimension_semantics=sem, vmem_limit_bytes=VMEM_LIMIT)


def _dot(a, b):
    return jnp.dot(a, b, preferred_element_type=F32)


def _dot_nt(a, b):
    return lax.dot_general(a, b, (((1,), (1,)), ((), ())), preferred_element_type=F32)


def _lane_lo(shape):
    return lax.broadcasted_iota(I32, shape, len(shape) - 1) % LANE < HEAD_DIM


def _layer_norm(y, g, b):
    mu = jnp.mean(y, axis=-1, keepdims=True)
    yc = y - mu
    var = jnp.mean(yc * yc, axis=-1, keepdims=True)
    return yc * lax.rsqrt(var + LN_EPS) * g + b


def _top16(v):
    bits = lax.bitcast_convert_type(v, I32) & jnp.int32(-65536)
    return lax.bitcast_convert_type(bits, F32)


def _split3(x):
    s1 = _top16(x)
    r = x - s1
    s2 = _top16(r)
    return s1, s2, r - s2


def _aug_rel(shape, h):
    lane = lax.broadcasted_iota(I32, shape, 1)
    return lane - (HEAD_DIM if h % 2 == 0 else 0)


def _even_proj_kernel(x_ref, w_ref, bf_ref, qa_ref, ka_ref, va_ref, qf_ref, kf_ref, vf_ref,
                      carry_ref, *, tiles_per_seq):
    t = pl.program_id(0)
    tm = x_ref.shape[0]
    h = _dot(x_ref[...].astype(BF16), w_ref[...])
    qa_ref[...] = h[:, 0:512].astype(BF16)
    ka_ref[...] = h[:, 512:768].astype(BF16)
    va_ref[...] = h[:, 768:1024].astype(BF16)
    z = h[:, 2560:2688] + bf_ref[...]
    logf = jnp.minimum(z, 0.0) - jnp.log1p(jnp.exp(-jnp.abs(z)))

    @pl.when(t % tiles_per_seq == 0)
    def _():
        carry_ref[...] = jnp.zeros_like(carry_ref)

    row = lax.broadcasted_iota(I32, (tm, tm), 0)
    col = lax.broadcasted_iota(I32, (tm, tm), 1)
    tri = jnp.where(row >= col, 1.0, 0.0).astype(BF16)
    c = carry_ref[...]
    for part in _split3(logf):
        c = c + _dot(tri, part.astype(BF16))
    carry_ref[...] = c[tm - 1:tm, :]

    s1, s2, s3 = _split3(c * LOG2E)
    lo = _lane_lo((tm, LANE))
    for hd in range(B_HEADS):
        own = lo if hd % 2 == 0 else ~lo
        rel = _aug_rel((tm, LANE), hd)
        b1, b2, b3 = s1[:, hd:hd + 1], s2[:, hd:hd + 1], s3[:, hd:hd + 1]
        q_aug = jnp.where(rel == 3, b1, jnp.where(rel == 4, b2, jnp.where(rel == 5, b3,
                          jnp.where((rel >= 0) & (rel < 3), -1.0, 0.0))))
        k_aug = jnp.where(rel == 0, b1, jnp.where(rel == 1, b2, jnp.where(rel == 2, b3,
                          jnp.where((rel >= 3) & (rel < 6), 1.0, 0.0))))
        pj = (hd // 2) * LANE
        sl = slice(hd * LANE, (hd + 1) * LANE)
        qf_ref[:, sl] = jnp.where(own, h[:, 1024 + pj:1024 + pj + LANE], q_aug).astype(BF16)
        kf_ref[:, sl] = jnp.where(own, h[:, 1536 + pj:1536 + pj + LANE], k_aug).astype(BF16)
        vf_ref[:, sl] = jnp.where(own, h[:, 2048 + pj:2048 + pj + LANE], 1.0).astype(BF16)


def _even_proj(xt, w_cat, bf_pad, seq, tm):
    n, d = xt.shape
    wcols = w_cat.shape[1]
    widths = (512, 256, 256, 1024, 1024, 1024)
    out_shape = [jax.ShapeDtypeStruct((n, w), BF16) for w in widths]
    out_specs = [pl.BlockSpec((tm, w), lambda i: (i, 0)) for w in widths]
    return pl.pallas_call(
        functools.partial(_even_proj_kernel, tiles_per_seq=seq // tm),
        out_shape=out_shape,
        grid=(n // tm,),
        in_specs=[pl.BlockSpec((tm, d), lambda i: (i, 0)),
                  pl.BlockSpec((d, wcols), lambda i: (0, 0)),
                  pl.BlockSpec((1, LANE), lambda i: (0, 0))],
        out_specs=out_specs,
        scratch_shapes=[pltpu.VMEM((1, LANE), F32)],
        compiler_params=_cparams(("arbitrary",)),
        name="even_proj",
    )(xt, w_cat, bf_pad)


def _swa_kernel(sink_ref, q_ref, kp_ref, kc_ref, vp_ref, vc_ref, o_ref):
    i = pl.program_id(1)
    group = A_HEADS // A_KV_HEADS
    tl = lax.broadcasted_iota(I32, (BLOCK, 2 * BLOCK), 0)
    sl = lax.broadcasted_iota(I32, (BLOCK, 2 * BLOCK), 1)
    dist = BLOCK + tl - sl
    valid = (dist >= 0) & (dist < WINDOW) & ((sl >= BLOCK) | (i > 0))
    distf = dist.astype(F32)
    lo = _lane_lo((BLOCK, LANE))

    def band(prev_ref, cur_ref, g):
        return jnp.concatenate([prev_ref[:, g * LANE:(g + 1) * LANE], cur_ref[:, g * LANE:(g + 1) * LANE]], axis=0)

    stacked = []
    for g in range(A_KV_HEADS):
        qs = []
        for h in range(g * group, (g + 1) * group):
            qp = q_ref[:, (h // 2) * LANE:(h // 2 + 1) * LANE]
            qs.append(jnp.where(lo if h % 2 == 0 else ~lo, qp, jnp.zeros_like(qp)))
        stacked.append(_dot_nt(jnp.concatenate(qs, axis=0), band(kp_ref, kc_ref, g)))
    ps, denoms = [], []
    for h in range(A_HEADS):
        r = h % group
        slope = 2.0 ** (-8.0 * (h + 1) / A_HEADS) * LOG2E
        logits = jnp.where(valid, stacked[h // group][r * BLOCK:(r + 1) * BLOCK] - slope * distf, NEG_INF)
        sink = sink_ref[h] * LOG2E
        m = jnp.maximum(jnp.max(logits, axis=-1, keepdims=True), sink)
        e = jnp.exp2(logits - m)
        denoms.append(jnp.sum(e, axis=-1, keepdims=True) + jnp.exp2(sink - m))
        ps.append(e.astype(BF16))
    outs = []
    for g in range(A_KV_HEADS):
        pv = _dot(jnp.concatenate(ps[g * group:(g + 1) * group], axis=0), band(vp_ref, vc_ref, g))
        for r in range(group):
            outs.append(pv[r * BLOCK:(r + 1) * BLOCK] / denoms[g * group + r])
    for j in range(A_HEADS // 2):
        o_ref[:, j * LANE:(j + 1) * LANE] = jnp.where(lo, outs[2 * j], outs[2 * j + 1]).astype(BF16)


def _swa(sinks, qa, ka, va):
    b, s, _ = qa.shape
    nb = s // BLOCK
    prev = lambda bi, i: (bi, jnp.maximum(i - 1, 0), 0)
    cur = lambda bi, i: (bi, i, 0)
    return pl.pallas_call(
        _swa_kernel,
        out_shape=jax.ShapeDtypeStruct((b, s, A_HEADS * HEAD_DIM), BF16),
        grid=(b, nb),
        in_specs=[pl.BlockSpec(memory_space=pltpu.SMEM),
                  pl.BlockSpec((None, BLOCK, 512), cur),
                  pl.BlockSpec((None, BLOCK, 256), prev),
                  pl.BlockSpec((None, BLOCK, 256), cur),
                  pl.BlockSpec((None, BLOCK, 256), prev),
                  pl.BlockSpec((None, BLOCK, 256), cur)],
        out_specs=pl.BlockSpec((None, BLOCK, 512), cur),
        compiler_params=_cparams(("parallel", "arbitrary")),
        name="swa_attn",
    )(sinks, qa, ka, ka, va, va)


def _normalize_pairs(accs, o_ref):
    lo = _lane_lo(accs[0].shape)
    outs = [a / pltpu.roll(a, HEAD_DIM, 1) for a in accs]
    for j in range(len(accs) // 2):
        o_ref[:, j * LANE:(j + 1) * LANE] = jnp.where(lo, outs[2 * j], outs[2 * j + 1]).astype(BF16)


def _online_softmax_chains(qs, ks, vs, carry, biases=None, masks=None, guard_empty=False):
    n = len(qs)
    ss = [_dot_nt(qs[i], ks[i]).astype(BF16) for i in range(n)]
    if biases is not None:
        ss = [ss[i] + biases[i] for i in range(n)]
    if masks is not None:
        ss = [jnp.where(masks[i], ss[i], NEG_INF) for i in range(n)]
    ms, ps, alphas = [], [], []
    for i in range(n):
        m = carry[i][0]
        m_new = jnp.maximum(m, jnp.max(ss[i], axis=-1, keepdims=True).astype(F32))
        m_ref = jnp.where(m_new == NEG_INF, 0.0, m_new) if guard_empty else m_new
        ps.append(jnp.exp2(ss[i] - m_ref.astype(BF16)))
        alphas.append(jnp.exp2(m - m_ref))
        ms.append(m_new)
    pvs = [_dot(ps[i], vs[i]) for i in range(n)]
    return tuple((ms[i], alphas[i] * carry[i][1] + pvs[i]) for i in range(n))


def _flash_kernel(q_ref, k_ref, v_ref, o_ref, *, tq, nh):
    qi = pl.program_id(2)
    nslab = tq // BLOCK
    chains = [(hh, r) for hh in range(nh) for r in range(nslab)]
    row = lax.broadcasted_iota(I32, (BLOCK, tq), 0)
    col = lax.broadcasted_iota(I32, (BLOCK, tq), 1)

    def lanes(hh):
        return slice(hh * LANE, (hh + 1) * LANE)

    def step(kb, carry, diagonal):
        start = pl.multiple_of(kb * tq, tq)
        qs = [q_ref[r * BLOCK:(r + 1) * BLOCK, lanes(hh)] for hh, r in chains]
        ks = [k_ref[pl.ds(start, tq), lanes(hh)] for hh, r in chains]
        vs = [v_ref[pl.ds(start, tq), lanes(hh)] for hh, r in chains]
        masks = [col <= row + r * BLOCK for hh, r in chains] if diagonal else None
        return _online_softmax_chains(qs, ks, vs, carry, masks=masks)

    init = tuple((jnp.full((BLOCK, 1), NEG_INF, F32), jnp.zeros((BLOCK, LANE), F32))
                 for _ in range(nh * nslab))
    carry = lax.fori_loop(0, qi, lambda kb, c: step(kb, c, False), init)
    carry = step(qi, carry, True)
    accs = [jnp.concatenate([carry[hh * nslab + r][1] for r in range(nslab)], axis=0) for hh in range(nh)]
    _normalize_pairs(accs, o_ref)


def _flash(q, k, v, *, tq, nh, name):
    b, s, w = v.shape
    return pl.pallas_call(
        functools.partial(_flash_kernel, tq=tq, nh=nh),
        out_shape=jax.ShapeDtypeStruct((b, s, w // 2), BF16),
        grid=(b, w // (nh * LANE), s // tq),
        in_specs=[pl.BlockSpec((None, tq, nh * LANE), lambda bi, j, i: (bi, i, j)),
                  pl.BlockSpec((None, s, nh * LANE), lambda bi, j, i: (bi, 0, j)),
                  pl.BlockSpec((None, s, nh * LANE), lambda bi, j, i: (bi, 0, j))],
        out_specs=pl.BlockSpec((None, tq, nh * LANE // 2), lambda bi, j, i: (bi, i, j)),
        compiler_params=_cparams(("parallel", "parallel", "arbitrary")),
        name=name,
    )(q, k, v)


def _route_rows(aff, sel):
    gs = []
    for g in range(N_GROUPS):
        r = sel[g * EXPERTS_PER_GROUP:(g + 1) * EXPERTS_PER_GROUP]
        best = None
        for a in range(EXPERTS_PER_GROUP):
            for c in range(a + 1, EXPERTS_PER_GROUP):
                pair = r[a] + r[c]
                best = pair if best is None else jnp.maximum(best, pair)
        gs.append(best)
    gmax = functools.reduce(jnp.maximum, gs)
    gidx = jnp.full_like(gmax, float(N_GROUPS - 1))
    for g in range(N_GROUPS - 2, -1, -1):
        gidx = jnp.where(gs[g] == gmax, float(g), gidx)

    def pick(rows, k):
        out = rows[(N_GROUPS - 1) * EXPERTS_PER_GROUP + k]
        for g in range(N_GROUPS - 2, -1, -1):
            out = jnp.where(gidx == float(g), rows[g * EXPERTS_PER_GROUP + k], out)
        return out

    s = [pick(sel, k) for k in range(EXPERTS_PER_GROUP)]
    a = [pick(aff, k) for k in range(EXPERTS_PER_GROUP)]

    def first_max(vals):
        mx = functools.reduce(jnp.maximum, vals)
        idx = jnp.full_like(mx, float(len(vals) - 1))
        for k in range(len(vals) - 2, -1, -1):
            idx = jnp.where(vals[k] == mx, float(k), idx)
        return idx

    def take(vals, idx):
        out = vals[-1]
        for k in range(len(vals) - 2, -1, -1):
            out = jnp.where(idx == float(k), vals[k], out)
        return out

    i1 = first_max(s)
    s2 = [jnp.where(i1 == float(k), NEG_INF, s[k]) for k in range(EXPERTS_PER_GROUP)]
    i2 = first_max(s2)
    a1, a2 = take(a, i1), take(a, i2)
    tot = a1 + a2
    base = gidx * float(EXPERTS_PER_GROUP)
    return base + i1, base + i2, a1 / tot, a2 / tot


def _outproj_kernel(a_ref, b_ref, x_ref, wa_ref, wb_ref, g_ref, be_ref, rwt_ref, rb_ref,
                    x1_ref, route_ref, *, alpha):
    d = x_ref.shape[1]
    tm = x_ref.shape[0]
    mix = _dot(a_ref[...], wa_ref[...]) + _dot(b_ref[...], wb_ref[...])
    x1 = _layer_norm(alpha * x_ref[...] + mix, g_ref[...], be_ref[...])
    x1_ref[:, 0:d] = x1
    x_hi = x1.astype(BF16)
    x_lo = (x1 - x_hi.astype(F32)).astype(BF16)
    w = rwt_ref[...]
    w_hi = w.astype(BF16)
    w_lo = (w - w_hi.astype(F32)).astype(BF16)
    z_hi = _dot_nt(jnp.concatenate([w_hi, w_lo], axis=0), x_hi)
    zt = z_hi[0:N_EXPERTS] + z_hi[N_EXPERTS:2 * N_EXPERTS] + _dot_nt(w_hi, x_lo)
    afft = 1.0 / (1.0 + jnp.exp(-zt))
    selt = afft + rb_ref[...]
    aff = [afft[e:e + 1, :] for e in range(N_EXPERTS)]
    sel = [selt[e:e + 1, :] for e in range(N_EXPERTS)]
    e1, e2, w1, w2 = _route_rows(aff, sel)
    rec = jnp.concatenate([e1, e2, w1, w2, jnp.zeros((LANE - 4, tm), F32)], axis=0)
    route_ref[...] = rec[0:8]
    x1_ref[:, d:d + LANE] = rec.T


def _outproj(a, b, xt, wa, wb, g, be, rwt, rb, alpha, tm):
    n, d = xt.shape
    full = lambda i: (0, 0)
    return pl.pallas_call(
        functools.partial(_outproj_kernel, alpha=alpha),
        out_shape=[jax.ShapeDtypeStruct((n, d + LANE), F32), jax.ShapeDtypeStruct((8, n), F32)],
        grid=(n // tm,),
        in_specs=[pl.BlockSpec((tm, a.shape[1]), lambda i: (i, 0)),
                  pl.BlockSpec((tm, b.shape[1]), lambda i: (i, 0)),
                  pl.BlockSpec((tm, d), lambda i: (i, 0)),
                  pl.BlockSpec(wa.shape, full), pl.BlockSpec(wb.shape, full),
                  pl.BlockSpec((1, d), full), pl.BlockSpec((1, d), full),
                  pl.BlockSpec(rwt.shape, full), pl.BlockSpec(rb.shape, full)],
        out_specs=[pl.BlockSpec((tm, d + LANE), lambda i: (i, 0)), pl.BlockSpec((8, tm), lambda i: (0, i))],
        compiler_params=_cparams(("parallel",)),
        name="outproj_ln_route",
    )(a, b, xt, wa, wb, g, be, rwt, rb)


def _moe_kernel(bgrp_ref, bbase_ref, nact_ref, src_ref,
                x_hbm, wg_ref, wu_ref, wd_ref, o_ref, xbuf0, xbuf1, sem, *, tm, d):
    i = pl.program_id(0)
    nact = nact_ref[0]
    last = src_ref.shape[0] - 1
    bufs = (xbuf0, xbuf1)

    def issue(blk, slot):
        base = bbase_ref[blk]
        for r in range(tm):
            tok = src_ref[jnp.minimum(base + r, last)]
            pltpu.make_async_copy(x_hbm.at[pl.ds(tok, 1), :], bufs[slot].at[pl.ds(r, 1), :],
                                  sem.at[slot]).start(priority=r % 2)

    def wait(slot):
        pltpu.make_async_copy(x_hbm.at[pl.ds(0, tm), :], bufs[slot], sem.at[slot]).wait()

    @pl.when((i == 0) & (nact > 0))
    def _():
        issue(0, 0)

    def step(slot):
        wait(slot)

        @pl.when(i + 1 < nact)
        def _():
            issue(i + 1, 1 - slot)

        buf = bufs[slot]
        xb = buf[:, 0:d].astype(BF16)
        rec = buf[:, d:d + LANE]
        e1, e2, w1, w2 = rec[:, 0:1], rec[:, 1:2], rec[:, 2:3], rec[:, 3:4]
        fg = wg_ref.shape[1]
        lane_expert = lax.broadcasted_iota(I32, (tm, fg), 1) // (fg // EXPERTS_PER_GROUP)
        expert = (lane_expert + bgrp_ref[i] * EXPERTS_PER_GROUP).astype(F32)
        gate_w = jnp.where(expert == e1, w1, jnp.where(expert == e2, w2, 0.0))
        gate = _dot(xb, wg_ref[...])
        up = _dot(xb, wu_ref[...])
        h = gate * (1.0 / (1.0 + jnp.exp(-gate))) * up * gate_w
        o_ref[...] = _dot(h.astype(BF16), wd_ref[...])

    for slot in range(2):
        pl.when((i < nact) & (i % 2 == slot))(functools.partial(step, slot))

    @pl.when(i >= nact)
    def _():
        o_ref[...] = jnp.zeros_like(o_ref)


def _moe_experts(x1e, wg, wu, wd, bgrp, bbase, nact, src, tm):
    n, de = x1e.shape
    d = de - LANE
    nblk = bgrp.shape[0]
    fg = wg.shape[2]
    wmap = lambda i, bgrp, *_: (bgrp[i], 0, 0)
    grid_spec = pltpu.PrefetchScalarGridSpec(
        num_scalar_prefetch=4,
        grid=(nblk,),
        in_specs=[pl.BlockSpec(memory_space=pl.ANY),
                  pl.BlockSpec((None, d, fg), wmap),
                  pl.BlockSpec((None, d, fg), wmap),
                  pl.BlockSpec((None, fg, d), wmap)],
        out_specs=pl.BlockSpec((tm, d), lambda i, *_: (i, 0)),
        scratch_shapes=[pltpu.VMEM((tm, de), F32), pltpu.VMEM((tm, de), F32), pltpu.SemaphoreType.DMA((2,))],
    )
    return pl.pallas_call(
        functools.partial(_moe_kernel, tm=tm, d=d),
        out_shape=jax.ShapeDtypeStruct((nblk * tm, d), F32),
        grid_spec=grid_spec,
        compiler_params=_cparams(("arbitrary",)),
        name="moe_experts",
    )(bgrp, bbase, nact, src, x1e, wg, wu, wd)


def _combine_kernel(p_ref, x_ref, g_ref, be_ref, o_hbm, y_ref, obuf0, obuf1, sem, *, alpha, tm, d):
    i = pl.program_id(0)
    nsteps = pl.num_programs(0)

    bufs = (obuf0, obuf1)

    def issue(blk, slot):
        base = blk * tm
        for r in range(tm):
            pltpu.make_async_copy(o_hbm.at[pl.ds(p_ref[base + r], 1), :],
                                  bufs[slot].at[pl.ds(r, 1), :], sem.at[slot]).start(priority=r % 2)

    def wait(slot):
        pltpu.make_async_copy(o_hbm.at[pl.ds(0, tm), :], bufs[slot], sem.at[slot]).wait()

    @pl.when(i == 0)
    def _():
        issue(0, 0)

    def step(slot):
        wait(slot)

        @pl.when(i + 1 < nsteps)
        def _():
            issue(i + 1, 1 - slot)

        y = alpha * x_ref[:, 0:d] + bufs[slot][...]
        y_ref[...] = _layer_norm(y, g_ref[...], be_ref[...])

    for slot in range(2):
        pl.when(i % 2 == slot)(functools.partial(step, slot))


def _combine(x1e, g, be, o, pos, alpha, tm):
    n, de = x1e.shape
    d = de - LANE
    grid_spec = pltpu.PrefetchScalarGridSpec(
        num_scalar_prefetch=1,
        grid=(n // tm,),
        in_specs=[pl.BlockSpec((tm, de), lambda i, *_: (i, 0)),
                  pl.BlockSpec((1, d), lambda i, *_: (0, 0)),
                  pl.BlockSpec((1, d), lambda i, *_: (0, 0)),
                  pl.BlockSpec(memory_space=pl.ANY)],
        out_specs=pl.BlockSpec((tm, d), lambda i, *_: (i, 0)),
        scratch_shapes=[pltpu.VMEM((tm, d), F32), pltpu.VMEM((tm, d), F32), pltpu.SemaphoreType.DMA((2,))],
    )
    return pl.pallas_call(
        functools.partial(_combine_kernel, alpha=alpha, tm=tm, d=d),
        out_shape=jax.ShapeDtypeStruct((n, d), F32),
        grid_spec=grid_spec,
        compiler_params=_cparams(("arbitrary",)),
        name="moe_combine_ln",
    )(pos, x1e, g, be, o)


def _moe_layer(x1e, route_t, wg, wu, wd, g, be, alpha, tm_e, tm_c):
    n = x1e.shape[0]
    grp = route_t[0].astype(I32) // EXPERTS_PER_GROUP
    onehot = (grp[:, None] == jnp.arange(N_GROUPS, dtype=I32)[None, :]).astype(I32)
    rank = jnp.take_along_axis(jnp.cumsum(onehot, axis=0) - onehot, grp[:, None], axis=1)[:, 0]
    counts = jnp.sum(onehot, axis=0)
    nblk_g = (counts + tm_e - 1) // tm_e
    blk_start = jnp.cumsum(nblk_g) - nblk_g
    seg_start = jnp.cumsum(counts) - counts
    pos = (blk_start[grp] * tm_e + rank).astype(I32)
    src = jnp.argsort(grp, stable=True).astype(I32)
    nblk = n // tm_e + N_GROUPS
    blk = jnp.arange(nblk, dtype=I32)
    blk_end = jnp.cumsum(nblk_g)
    bgrp = jnp.minimum(jnp.sum((blk[:, None] >= blk_end[None, :]).astype(I32), axis=1), N_GROUPS - 1)
    bbase = (seg_start[bgrp] + (blk - blk_start[bgrp]) * tm_e).astype(I32)
    nact = jnp.sum(nblk_g).astype(I32).reshape(1)
    o = _moe_experts(x1e, wg, wu, wd, bgrp, bbase, nact, src, tm_e)
    return _combine(x1e, g, be, o, pos, alpha, tm_c)


def _rms(x, g):
    return x * lax.rsqrt(jnp.mean(x * x, axis=-1, keepdims=True) + RMS_EPS) * g

ODD_OFF = dict(qc=0, kc=512, vc=640, qi=768, ki=1024, wi=1152, cq=1280, ckv=1536, kra=1664, krb=1792, end=1920)


def _odd_proj_kernel(x_ref, w_ref, qng_ref, kvng_ref, wqa_ref, wqb_ref, wk_ref, wv_ref, cos_ref, sin_ref,
                     tq_ref, tk_ref,
                     qc_ref, kc_ref, vc_ref, qi_ref, ki_ref, wi_ref, q8_ref, k8_ref, v8_ref, *, q_scale):
    o = ODD_OFF
    tm = x_ref.shape[0]
    lo = _lane_lo((tm, LANE))
    h = _dot(x_ref[...].astype(BF16), w_ref[...])
    kdup = h[:, o["kc"]:o["vc"]]
    vdup = h[:, o["vc"]:o["qi"]]
    kc_ref[:, 0:LANE] = jnp.where(lo, kdup, tk_ref[:, 0:LANE].astype(F32)).astype(BF16)
    kc_ref[:, LANE:2 * LANE] = jnp.where(lo, tk_ref[:, LANE:2 * LANE].astype(F32), kdup).astype(BF16)
    vc_ref[:, 0:LANE] = jnp.where(lo, vdup, 1.0).astype(BF16)
    vc_ref[:, LANE:2 * LANE] = jnp.where(lo, 1.0, vdup).astype(BF16)
    qi_ref[...] = h[:, o["qi"]:o["ki"]].astype(BF16)
    ki_ref[...] = h[:, o["ki"]:o["wi"]].astype(BF16)
    wi_ref[...] = h[:, o["wi"]:o["wi"] + IDX_HEADS]
    cos = cos_ref[...]
    sin = sin_ref[...]
    cqn = _rms(h[:, o["cq"]:o["ckv"]], qng_ref[...]).astype(BF16)
    qa = _dot(cqn, wqa_ref[...])
    qb = _dot(cqn, wqb_ref[...])
    ckvn = _rms(h[:, o["ckv"]:o["kra"]], kvng_ref[...]).astype(BF16)
    kn = _dot(ckvn, wk_ref[...])
    v8 = _dot(ckvn, wv_ref[...])
    kr = h[:, o["kra"]:o["krb"]] * cos + h[:, o["krb"]:o["end"]] * sin
    for hd in range(D_HEADS):
        sl = slice(hd * LANE, (hd + 1) * LANE)
        pj = slice((hd // 2) * LANE, (hd // 2 + 1) * LANE)
        own = lo if hd % 2 == 0 else ~lo
        qc_ref[:, sl] = jnp.where(own, h[:, pj], tq_ref[:, sl].astype(F32)).astype(BF16)
        q8_ref[:, sl] = ((qa[:, sl] * cos + qb[:, sl] * sin) * q_scale).astype(BF16)
        k8_ref[:, sl] = (kn[:, sl] + kr).astype(BF16)
        v8_ref[:, sl] = jnp.where(own, v8[:, pj], 1.0).astype(BF16)


def _odd_proj(xt, w_cat, qng, kvng, wqa, wqb, wk, wv, cos_t, sin_t, tq_t, tk_t, seq, tm):
    n, d = xt.shape
    full = lambda i: (0, 0)
    tps = seq // tm
    pos = lambda i: (i % tps, 0)
    widths = (1024, 256, 256, 256, 128)
    out_shape = [jax.ShapeDtypeStruct((n, w), BF16) for w in widths]
    out_specs = [pl.BlockSpec((tm, w), lambda i: (i, 0)) for w in widths]
    out_shape.append(jax.ShapeDtypeStruct((n, IDX_HEADS), F32))
    out_specs.append(pl.BlockSpec((tm, IDX_HEADS), lambda i: (i, 0)))
    for w in (1024, 1024, 1024):
        out_shape.append(jax.ShapeDtypeStruct((n, w), BF16))
        out_specs.append(pl.BlockSpec((tm, w), lambda i: (i, 0)))
    return pl.pallas_call(
        functools.partial(_odd_proj_kernel, q_scale=(MLA_NOPE + MLA_ROPE) ** -0.5 * LOG2E),
        out_shape=out_shape,
        grid=(n // tm,),
        in_specs=[pl.BlockSpec((tm, d), lambda i: (i, 0)),
                  pl.BlockSpec(w_cat.shape, full),
                  pl.BlockSpec(qng.shape, full), pl.BlockSpec(kvng.shape, full),
                  pl.BlockSpec(wqa.shape, full), pl.BlockSpec(wqb.shape, full),
                  pl.BlockSpec(wk.shape, full), pl.BlockSpec(wv.shape, full),
                  pl.BlockSpec((tm, LANE), pos), pl.BlockSpec((tm, LANE), pos),
                  pl.BlockSpec((tm, C_HEADS * LANE), pos), pl.BlockSpec((tm, 2 * LANE), pos)],
        out_specs=out_specs,
        compiler_params=_cparams(("parallel",)),
        name="odd_proj",
    )(xt, w_cat, qng, kvng, wqa, wqb, wk, wv, cos_t, sin_t, tq_t, tk_t)


def _dsa_kernel(q_ref, k_ref, v_ref, qi_ref, ki_ref, wt_ref, o_ref, sc_ref, mb_ref, *, top_k, chunk, qb):
    i = pl.program_id(1)
    nc = (i * qb + qb + chunk - 1) // chunk
    lo = _lane_lo((qb, LANE))
    s_row2 = lax.broadcasted_iota(I32, (2 * chunk, qb), 0)
    t_lane2 = i * qb + lax.broadcasted_iota(I32, (2 * chunk, qb), 1)

    def chunk_start(c):
        return pl.multiple_of(c * chunk, chunk)

    qidx = []
    for h in range(IDX_HEADS):
        qp = qi_ref[:, (h // 2) * LANE:(h // 2 + 1) * LANE]
        qidx.append(jnp.where(lo if h % 2 == 0 else ~lo, qp, jnp.zeros_like(qp)))
    q_stack = jnp.concatenate(qidx, axis=0)
    w = wt_ref[...]

    npair = (nc + 1) // 2

    def pair_start(cp):
        return cp * (2 * chunk)

    def select(pairs):
        def score_pair(cp):
            start = pair_start(cp)
            d = _dot_nt(ki_ref[pl.ds(start, 2 * chunk), :], q_stack)
            sc = w[0:1, :] * jnp.maximum(d[:, 0:qb], 0.0)
            for h in range(1, IDX_HEADS):
                sc = sc + w[h:h + 1, :] * jnp.maximum(d[:, h * qb:(h + 1) * qb], 0.0)
            sc = jnp.where(sc == 0.0, 0.0, sc)
            sc_ref[pl.ds(start, 2 * chunk), :] = jnp.where(start + s_row2 <= t_lane2, sc, NEG_INF)

        for cp in range(pairs):
            score_pair(cp)

        nacc = 4 * 8

        def count(trial, strict):
            acc = jnp.zeros((nacc, qb), F32)
            for cp in range(pairs):
                sc = sc_ref[pl.ds(pair_start(cp), 2 * chunk), :]
                hit = (sc > trial) if strict else (sc >= trial)
                acc = acc + jnp.sum(jnp.where(hit, 1.0, 0.0).reshape(2 * chunk // nacc, nacc, qb), axis=0)
            return jnp.sum(acc, axis=0, keepdims=True)

        def decode(key):
            return lax.bitcast_convert_type(jnp.where(key < 0, key ^ jnp.int32(0x7FFFFFFF), key), F32)

        kf = float(top_k)
        key_neg_inf = jnp.int32(-2139095041)
        cand = jnp.where(count(jnp.zeros((1, qb), F32), False) >= kf, 0, jnp.int32(-2 ** 31)).astype(I32)

        def bit_step(j, cand):
            trial = cand + lax.shift_left(jnp.int32(1), 30 - j)
            ok = (trial <= key_neg_inf) | (count(decode(trial), False) >= kf)
            return jnp.where(ok, trial, cand)

        thr = decode(lax.fori_loop(0, 31, bit_step, cand))
        need = kf - count(thr, True)

        r0 = lax.broadcasted_iota(I32, (chunk, chunk), 0)
        r1 = lax.broadcasted_iota(I32, (chunk, chunk), 1)
        below = jnp.where(r1 < r0, 1.0, 0.0).astype(BF16)
        need = jnp.where(thr == NEG_INF, 0.0, need)

        offset = jnp.zeros((1, qb), F32)
        for cp in range(pairs):
            start = pair_start(cp)
            sc = sc_ref[pl.ds(start, 2 * chunk), :]
            eq = sc == thr
            eqf = jnp.where(eq, 1.0, 0.0)
            eqb = eqf.astype(BF16)
            first = jnp.sum(eqf[0:chunk], axis=0, keepdims=True)
            prefix = jnp.concatenate([_dot(below, eqb[0:chunk]) + offset,
                                      _dot(below, eqb[chunk:2 * chunk]) + (offset + first)], axis=0)
            bias = jnp.where(sc > thr, 0.0, jnp.where(eq, jnp.where(prefix < need, 0.0, NEG_INF), NEG_INF))
            mb_ref[:, pl.ds(start, 2 * chunk)] = bias.T
            offset = offset + first + jnp.sum(eqf[chunk:2 * chunk], axis=0, keepdims=True)

    for pairs in range(1, k_ref.shape[0] // (2 * chunk) + 1):
        pl.when(npair == pairs)(functools.partial(select, pairs))

    nslab = qb // BLOCK
    chains = [(h, r) for h in range(C_HEADS) for r in range(nslab)]

    def att_chunk(c, carry):
        start = chunk_start(c)
        kp = [k_ref[pl.ds(start, chunk), par * LANE:(par + 1) * LANE] for par in range(2)]
        vp = [v_ref[pl.ds(start, chunk), par * LANE:(par + 1) * LANE] for par in range(2)]
        mb = [mb_ref[r * BLOCK:(r + 1) * BLOCK, pl.ds(start, chunk)].astype(BF16) for r in range(nslab)]
        qs = [q_ref[r * BLOCK:(r + 1) * BLOCK, h * LANE:(h + 1) * LANE] for h, r in chains]
        return _online_softmax_chains(qs, [kp[h % 2] for h, r in chains], [vp[h % 2] for h, r in chains], carry,
                                      biases=[mb[r] for h, r in chains], guard_empty=True)

    init = tuple((jnp.full((BLOCK, 1), NEG_INF, F32), jnp.zeros((BLOCK, LANE), F32)) for _ in chains)
    carry = lax.fori_loop(0, nc, att_chunk, init)
    accs = [jnp.concatenate([carry[h * nslab + r][1] for r in range(nslab)], axis=0) for h in range(C_HEADS)]
    _normalize_pairs(accs, o_ref)


def _dsa(qc, kc, vc, qi, ki, wit):
    b, s, _ = qc.shape
    top_k = min(DSA_TOPK_MAX, s // 4)
    chunk = 256
    qb = min(2 * BLOCK, s)
    assert s % (2 * chunk) == 0 and top_k <= chunk and chunk % qb == 0
    blk = lambda bi, i: (bi, i, 0)
    seq = lambda bi, i: (bi, 0, 0)
    return pl.pallas_call(
        functools.partial(_dsa_kernel, top_k=top_k, chunk=chunk, qb=qb),
        out_shape=jax.ShapeDtypeStruct((b, s, C_HEADS * HEAD_DIM), BF16),
        grid=(b, s // qb),
        in_specs=[pl.BlockSpec((None, qb, C_HEADS * LANE), blk),
                  pl.BlockSpec((None, s, 2 * LANE), seq),
                  pl.BlockSpec((None, s, 2 * LANE), seq),
                  pl.BlockSpec((None, qb, 256), blk),
                  pl.BlockSpec((None, s, LANE), seq),
                  pl.BlockSpec((None, IDX_HEADS, qb), lambda bi, i: (bi, 0, i))],
        out_specs=pl.BlockSpec((None, qb, 512), blk),
        scratch_shapes=[pltpu.VMEM((s, qb), F32), pltpu.VMEM((qb, s), F32)],
        compiler_params=_cparams(("parallel", "arbitrary")),
        name="dsa_attn",
    )(qc, kc, vc, qi, ki, wit)


def _dup64(w):
    d, c = w.shape
    w = w.reshape(d, c // HEAD_DIM, 1, HEAD_DIM)
    return jnp.broadcast_to(w, (d, c // HEAD_DIM, 2, HEAD_DIM)).reshape(d, 2 * c)


def _pad_cols(w, width):
    return jnp.pad(w, ((0, 0), (0, width - w.shape[1])))


def _split_cols(w, sizes):
    out, start = [], 0
    for n in sizes:
        out.append(w[:, start:start + n])
        start += n
    return out


def _rope_partner(w):
    half = MLA_ROPE // 2
    return jnp.concatenate([-w[..., half:], w[..., :half]], axis=-1)


def _rope_block(w_rope):
    d = w_rope.shape[0]
    return jnp.concatenate([jnp.zeros((d, MLA_NOPE), F32), w_rope,
                            jnp.zeros((d, LANE - MLA_NOPE - MLA_ROPE), F32)], axis=1)


def _alibi_tables(s):
    slopes = 2.0 ** (-8.0 * jnp.arange(1, C_HEADS + 1, dtype=F32) / C_HEADS)
    a = slopes * LOG2E
    pos = jnp.arange(s, dtype=F32)
    a_sp = _split3(a)
    r_sp = _split3(-(a[None, :] * pos[:, None]))
    ones = jnp.ones((s, C_HEADS), F32)
    q_terms = [128.0 * t * ones for t in a_sp] + [t * ones for t in a_sp] + list(r_sp)
    q_aug = jnp.stack(q_terms, axis=-1)
    q_aug = jnp.pad(q_aug, ((0, 0), (0, 0), (0, HEAD_DIM - q_aug.shape[-1])))
    zeros = jnp.zeros_like(q_aug)
    even = jnp.concatenate([zeros, q_aug], axis=-1)
    odd = jnp.concatenate([q_aug, zeros], axis=-1)
    parity = (jnp.arange(C_HEADS) % 2 == 0)[None, :, None]
    tq_t = jnp.where(parity, even, odd).reshape(s, C_HEADS * LANE)
    s_hi = jnp.floor(pos / LANE)
    s_lo = pos - LANE * s_hi
    k_aug = jnp.stack([s_hi] * 3 + [s_lo] * 3 + [jnp.ones_like(pos)] * 3, axis=-1)
    k_aug = jnp.pad(k_aug, ((0, 0), (0, HEAD_DIM - k_aug.shape[-1])))
    kz = jnp.zeros_like(k_aug)
    tk_t = jnp.concatenate([kz, k_aug, k_aug, kz], axis=-1)
    return tq_t.astype(BF16), tk_t.astype(BF16)


def kernel(x, even_w_in, even_b_f, even_sinks, odd_w_in, odd_q_norm, odd_kv_norm, odd_w_uq, odd_w_ukv,
           w_o, ln_g, ln_b, router_w, router_b, moe_w_gate, moe_w_up, moe_w_down):
    b, s, d = x.shape
    n = b * s
    depth = w_o.shape[0]
    alpha = (2 * depth) ** 0.25
    tm = min(512, s)
    tq = min(256, s)
    tm_e = min(512, n)
    tm_c = min(256, n)
    att_scale = HEAD_DIM ** -0.5
    xt = x.reshape(n, d)
    rwt = router_w.T
    rb = router_b.reshape(N_EXPERTS, 1)

    for l in range(depth):
        j = l // 2
        if l % 2 == 0:
            qa, ka, va, qb, kb, vb, fg = _split_cols(
                even_w_in[j], (512, 128, 128, 512, 512, 512, B_HEADS))
            w_cat = jnp.concatenate([qa * (att_scale * LOG2E), _dup64(ka), _dup64(va), qb * (att_scale * LOG2E), kb, vb,
                                     _pad_cols(fg, LANE)], axis=1).astype(BF16)
            bf_pad = _pad_cols(even_b_f[j].reshape(1, B_HEADS), LANE)
            qa_, ka_, va_, qf, kf, vf = _even_proj(xt, w_cat, bf_pad, s, tm)
            r3 = lambda a: a.reshape(b, s, a.shape[1])
            out_a = _swa(even_sinks[j], r3(qa_), r3(ka_), r3(va_))
            out_b = _flash(r3(qf), r3(kf), r3(vf), tq=tq, nh=8, name="fox_attn")
        else:
            (qc, kc, vc, qi, ki, wi, cq, ckv, kr) = _split_cols(
                odd_w_in[j], (512, 64, 64, IDX_HEADS * IDX_DIM, IDX_DIM, IDX_HEADS,
                              MLA_Q_RANK, MLA_KV_RANK, MLA_ROPE))
            w_cat = jnp.concatenate([
                qc * (att_scale * LOG2E), _dup64(kc), _dup64(vc), qi * (IDX_DIM ** -0.5), _dup64(ki),
                _pad_cols(wi * (IDX_HEADS ** -0.5), LANE), cq, ckv,
                _rope_block(kr), _rope_block(_rope_partner(kr))], axis=1).astype(BF16)
            dq = MLA_NOPE + MLA_ROPE
            wuq = odd_w_uq[j].reshape(MLA_Q_RANK, D_HEADS, dq)
            pad_q = jnp.zeros((MLA_Q_RANK, D_HEADS, LANE - dq), F32)
            wqa = jnp.concatenate([wuq, pad_q], axis=2).reshape(MLA_Q_RANK, D_HEADS * LANE).astype(BF16)
            wqb = jnp.concatenate([jnp.zeros((MLA_Q_RANK, D_HEADS, MLA_NOPE), F32),
                                   _rope_partner(wuq[..., MLA_NOPE:]), pad_q],
                                  axis=2).reshape(MLA_Q_RANK, D_HEADS * LANE).astype(BF16)
            wukv = odd_w_ukv[j].reshape(MLA_KV_RANK, D_HEADS, MLA_NOPE + MLA_V)
            wk = jnp.concatenate([wukv[..., :MLA_NOPE], jnp.zeros((MLA_KV_RANK, D_HEADS, LANE - MLA_NOPE), F32)],
                                 axis=2).reshape(MLA_KV_RANK, D_HEADS * LANE).astype(BF16)
            wv = wukv[..., MLA_NOPE:].reshape(MLA_KV_RANK, D_HEADS * MLA_V).astype(BF16)
            half = MLA_ROPE // 2
            inv = ROPE_THETA ** (-jnp.arange(half, dtype=F32) / half)
            ang = jnp.arange(s, dtype=F32)[:, None] * inv[None, :]
            cos_h, sin_h = jnp.cos(ang), jnp.sin(ang)
            pad_t = jnp.zeros((s, LANE - dq), F32)
            cos_t = jnp.concatenate([jnp.ones((s, MLA_NOPE), F32), cos_h, cos_h, pad_t], axis=1)
            sin_t = jnp.concatenate([jnp.zeros((s, MLA_NOPE), F32), sin_h, sin_h, pad_t], axis=1)
            tq_t, tk_t = _alibi_tables(s)
            qc_, kc_, vc_, qi_, ki_, wi_, q8, k8, v8 = _odd_proj(
                xt, w_cat, odd_q_norm[j].reshape(1, -1), odd_kv_norm[j].reshape(1, -1),
                wqa, wqb, wk, wv, cos_t, sin_t, tq_t, tk_t, s, tm)
            r3 = lambda a: a.reshape(b, s, a.shape[1])
            out_a = _dsa(r3(qc_), r3(kc_), r3(vc_), r3(qi_), r3(ki_), r3(wi_).transpose(0, 2, 1))
            out_b = _flash(r3(q8), r3(k8), r3(v8), tq=tq, nh=8, name="mla_attn")
        half_w = out_a.shape[2]
        wo = w_o[l].astype(BF16)
        x1e, route_t = _outproj(out_a.reshape(n, half_w), out_b.reshape(n, -1), xt,
                                wo[:half_w], wo[half_w:], ln_g[l, 0].reshape(1, d), ln_b[l, 0].reshape(1, d),
                                rwt, rb, alpha, tm)
        f = moe_w_gate.shape[3]

        def group_in(w):
            w = w.astype(BF16).reshape(N_GROUPS, EXPERTS_PER_GROUP, d, f)
            return w.transpose(0, 2, 1, 3).reshape(N_GROUPS, d, EXPERTS_PER_GROUP * f)

        wd_g = moe_w_down[l].astype(BF16).reshape(N_GROUPS, EXPERTS_PER_GROUP * f, d)
        xt = _moe_layer(x1e, route_t, group_in(moe_w_gate[l]), group_in(moe_w_up[l]), wd_g,
                        ln_g[l, 1].reshape(1, d), ln_b[l, 1].reshape(1, d), alpha, tm_e, tm_c)
    return xt.reshape(b, s, d)
```

```python
import functools
import math

import jax
import jax.numpy as jnp
from jax import lax
from jax.experimental import pallas as pl
from jax.experimental.pallas import tpu as pltpu

F32 = jnp.float32
BF16 = jnp.bfloat16
I32 = jnp.int32

LANE = 128
HEAD_DIM = 64
BLOCK = 128
A_HEADS, A_KV_HEADS, WINDOW = 8, 2, 128
B_HEADS = 8
C_HEADS, IDX_HEADS, IDX_DIM, DSA_TOPK_MAX = 8, 4, 64, 256
D_HEADS, MLA_Q_RANK, MLA_KV_RANK, MLA_NOPE, MLA_ROPE, MLA_V = 8, 256, 128, 64, 32, 64
ROPE_THETA = 10000.0
N_EXPERTS, N_GROUPS = 16, 4
EXPERTS_PER_GROUP = N_EXPERTS // N_GROUPS
LN_EPS, RMS_EPS = 1e-5, 1e-6
NEG_INF = float("-inf")
LOG2E = math.log2(math.e)
VMEM_LIMIT = 56 * 1024 * 1024


def _cparams(sem):
    return pltpu.CompilerParams(dimension_semantics=sem, vmem_limit_bytes=VMEM_LIMIT)


def _dot(a, b):
    return jnp.dot(a, b, preferred_element_type=F32)


def _dot_nt(a, b):
    return lax.dot_general(a, b, (((1,), (1,)), ((), ())), preferred_element_type=F32)


def _lane_lo(shape):
    return lax.broadcasted_iota(I32, shape, len(shape) - 1) % LANE < HEAD_DIM


def _layer_norm(y, g, b):
    mu = jnp.mean(y, axis=-1, keepdims=True)
    yc = y - mu
    var = jnp.mean(yc * yc, axis=-1, keepdims=True)
    return yc * lax.rsqrt(var + LN_EPS) * g + b


def _top16(v):
    bits = lax.bitcast_convert_type(v, I32) & jnp.int32(-65536)
    return lax.bitcast_convert_type(bits, F32)


def _split3(x):
    s1 = _top16(x)
    r = x - s1
    s2 = _top16(r)
    return s1, s2, r - s2


def _aug_rel(shape, h):
    lane = lax.broadcasted_iota(I32, shape, 1)
    return lane - (HEAD_DIM if h % 2 == 0 else 0)


def _even_proj_kernel(x_ref, w_ref, bf_ref, qa_ref, ka_ref, va_ref, qf_ref, kf_ref, vf_ref,
                      carry_ref, *, tiles_per_seq):
    t = pl.program_id(0)
    tm = x_ref.shape[0]
    h = _dot(x_ref[...].astype(BF16), w_ref[...])
    qa_ref[...] = h[:, 0:512].astype(BF16)
    ka_ref[...] = h[:, 512:768].astype(BF16)
    va_ref[...] = h[:, 768:1024].astype(BF16)
    z = h[:, 2560:2688] + bf_ref[...]
    logf = jnp.minimum(z, 0.0) - jnp.log1p(jnp.exp(-jnp.abs(z)))

    @pl.when(t % tiles_per_seq == 0)
    def _():
        carry_ref[...] = jnp.zeros_like(carry_ref)

    row = lax.broadcasted_iota(I32, (tm, tm), 0)
    col = lax.broadcasted_iota(I32, (tm, tm), 1)
    tri = jnp.where(row >= col, 1.0, 0.0).astype(BF16)
    c = carry_ref[...]
    for part in _split3(logf):
        c = c + _dot(tri, part.astype(BF16))
    carry_ref[...] = c[tm - 1:tm, :]

    s1, s2, s3 = _split3(c * LOG2E)
    lo = _lane_lo((tm, LANE))
    for hd in range(B_HEADS):
        own = lo if hd % 2 == 0 else ~lo
        rel = _aug_rel((tm, LANE), hd)
        b1, b2, b3 = s1[:, hd:hd + 1], s2[:, hd:hd + 1], s3[:, hd:hd + 1]
        q_aug = jnp.where(rel == 3, b1, jnp.where(rel == 4, b2, jnp.where(rel == 5, b3,
                          jnp.where((rel >= 0) & (rel < 3), -1.0, 0.0))))
        k_aug = jnp.where(rel == 0, b1, jnp.where(rel == 1, b2, jnp.where(rel == 2, b3,
                          jnp.where((rel >= 3) & (rel < 6), 1.0, 0.0))))
        pj = (hd // 2) * LANE
        sl = slice(hd * LANE, (hd + 1) * LANE)
        qf_ref[:, sl] = jnp.where(own, h[:, 1024 + pj:1024 + pj + LANE], q_aug).astype(BF16)
        kf_ref[:, sl] = jnp.where(own, h[:, 1536 + pj:1536 + pj + LANE], k_aug).astype(BF16)
        vf_ref[:, sl] = jnp.where(own, h[:, 2048 + pj:2048 + pj + LANE], 1.0).astype(BF16)


def _even_proj(xt, w_cat, bf_pad, seq, tm):
    n, d = xt.shape
    wcols = w_cat.shape[1]
    widths = (512, 256, 256, 1024, 1024, 1024)
    out_shape = [jax.ShapeDtypeStruct((n, w), BF16) for w in widths]
    out_specs = [pl.BlockSpec((tm, w), lambda i: (i, 0)) for w in widths]
    return pl.pallas_call(
        functools.partial(_even_proj_kernel, tiles_per_seq=seq // tm),
        out_shape=out_shape,
        grid=(n // tm,),
        in_specs=[pl.BlockSpec((tm, d), lambda i: (i, 0)),
                  pl.BlockSpec((d, wcols), lambda i: (0, 0)),
                  pl.BlockSpec((1, LANE), lambda i: (0, 0))],
        out_specs=out_specs,
        scratch_shapes=[pltpu.VMEM((1, LANE), F32)],
        compiler_params=_cparams(("arbitrary",)),
        name="even_proj",
    )(xt, w_cat, bf_pad)


def _swa_kernel(sink_ref, q_ref, kp_ref, kc_ref, vp_ref, vc_ref, o_ref):
    i = pl.program_id(1)
    group = A_HEADS // A_KV_HEADS
    tl = lax.broadcasted_iota(I32, (BLOCK, 2 * BLOCK), 0)
    sl = lax.broadcasted_iota(I32, (BLOCK, 2 * BLOCK), 1)
    dist = BLOCK + tl - sl
    valid = (dist >= 0) & (dist < WINDOW) & ((sl >= BLOCK) | (i > 0))
    distf = dist.astype(F32)
    lo = _lane_lo((BLOCK, LANE))

    def band(prev_ref, cur_ref, g):
        return jnp.concatenate([prev_ref[:, g * LANE:(g + 1) * LANE], cur_ref[:, g * LANE:(g + 1) * LANE]], axis=0)

    stacked = []
    for g in range(A_KV_HEADS):
        qs = []
        for h in range(g * group, (g + 1) * group):
            qp = q_ref[:, (h // 2) * LANE:(h // 2 + 1) * LANE]
            qs.append(jnp.where(lo if h % 2 == 0 else ~lo, qp, jnp.zeros_like(qp)))
        stacked.append(_dot_nt(jnp.concatenate(qs, axis=0), band(kp_ref, kc_ref, g)))
    ps, denoms = [], []
    for h in range(A_HEADS):
        r = h % group
        slope = 2.0 ** (-8.0 * (h + 1) / A_HEADS) * LOG2E
        logits = jnp.where(valid, stacked[h // group][r * BLOCK:(r + 1) * BLOCK] - slope * distf, NEG_INF)
        sink = sink_ref[h] * LOG2E
        m = jnp.maximum(jnp.max(logits, axis=-1, keepdims=True), sink)
        e = jnp.exp2(logits - m)
        denoms.append(jnp.sum(e, axis=-1, keepdims=True) + jnp.exp2(sink - m))
        ps.append(e.astype(BF16))
    outs = []
    for g in range(A_KV_HEADS):
        pv = _dot(jnp.concatenate(ps[g * group:(g + 1) * group], axis=0), band(vp_ref, vc_ref, g))
        for r in range(group):
            outs.append(pv[r * BLOCK:(r + 1) * BLOCK] / denoms[g * group + r])
    for j in range(A_HEADS // 2):
        o_ref[:, j * LANE:(j + 1) * LANE] = jnp.where(lo, outs[2 * j], outs[2 * j + 1]).astype(BF16)


def _swa(sinks, qa, ka, va):
    b, s, _ = qa.shape
    nb = s // BLOCK
    prev = lambda bi, i: (bi, jnp.maximum(i - 1, 0), 0)
    cur = lambda bi, i: (bi, i, 0)
    return pl.pallas_call(
        _swa_kernel,
        out_shape=jax.ShapeDtypeStruct((b, s, A_HEADS * HEAD_DIM), BF16),
        grid=(b, nb),
        in_specs=[pl.BlockSpec(memory_space=pltpu.SMEM),
                  pl.BlockSpec((None, BLOCK, 512), cur),
                  pl.BlockSpec((None, BLOCK, 256), prev),
                  pl.BlockSpec((None, BLOCK, 256), cur),
                  pl.BlockSpec((None, BLOCK, 256), prev),
                  pl.BlockSpec((None, BLOCK, 256), cur)],
        out_specs=pl.BlockSpec((None, BLOCK, 512), cur),
        compiler_params=_cparams(("parallel", "arbitrary")),
        name="swa_attn",
    )(sinks, qa, ka, ka, va, va)


def _normalize_pairs(accs, o_ref):
    lo = _lane_lo(accs[0].shape)
    outs = [a / pltpu.roll(a, HEAD_DIM, 1) for a in accs]
    for j in range(len(accs) // 2):
        o_ref[:, j * LANE:(j + 1) * LANE] = jnp.where(lo, outs[2 * j], outs[2 * j + 1]).astype(BF16)


def _online_softmax_chains(qs, ks, vs, carry, biases=None, masks=None, guard_empty=False):
    n = len(qs)
    ss = [_dot_nt(qs[i], ks[i]).astype(BF16) for i in range(n)]
    if biases is not None:
        ss = [ss[i] + biases[i] for i in range(n)]
    if masks is not None:
        ss = [jnp.where(masks[i], ss[i], NEG_INF) for i in range(n)]
    ms, ps, alphas = [], [], []
    for i in range(n):
        m = carry[i][0]
        m_new = jnp.maximum(m, jnp.max(ss[i], axis=-1, keepdims=True).astype(F32))
        m_ref = jnp.where(m_new == NEG_INF, 0.0, m_new) if guard_empty else m_new
        ps.append(jnp.exp2(ss[i] - m_ref.astype(BF16)))
        alphas.append(jnp.exp2(m - m_ref))
        ms.append(m_new)
    pvs = [_dot(ps[i], vs[i]) for i in range(n)]
    return tuple((ms[i], alphas[i] * carry[i][1] + pvs[i]) for i in range(n))


def _flash_kernel(q_ref, k_ref, v_ref, o_ref, *, tq, nh):
    qi = pl.program_id(2)
    nslab = tq // BLOCK
    chains = [(hh, r) for hh in range(nh) for r in range(nslab)]
    row = lax.broadcasted_iota(I32, (BLOCK, tq), 0)
    col = lax.broadcasted_iota(I32, (BLOCK, tq), 1)

    def lanes(hh):
        return slice(hh * LANE, (hh + 1) * LANE)

    def step(kb, carry, diagonal):
        start = pl.multiple_of(kb * tq, tq)
        qs = [q_ref[r * BLOCK:(r + 1) * BLOCK, lanes(hh)] for hh, r in chains]
        ks = [k_ref[pl.ds(start, tq), lanes(hh)] for hh, r in chains]
        vs = [v_ref[pl.ds(start, tq), lanes(hh)] for hh, r in chains]
        masks = [col <= row + r * BLOCK for hh, r in chains] if diagonal else None
        return _online_softmax_chains(qs, ks, vs, carry, masks=masks)

    init = tuple((jnp.full((BLOCK, 1), NEG_INF, F32), jnp.zeros((BLOCK, LANE), F32))
                 for _ in range(nh * nslab))
    carry = lax.fori_loop(0, qi, lambda kb, c: step(kb, c, False), init)
    carry = step(qi, carry, True)
    accs = [jnp.concatenate([carry[hh * nslab + r][1] for r in range(nslab)], axis=0) for hh in range(nh)]
    _normalize_pairs(accs, o_ref)


def _flash(q, k, v, *, tq, nh, name):
    b, s, w = v.shape
    return pl.pallas_call(
        functools.partial(_flash_kernel, tq=tq, nh=nh),
        out_shape=jax.ShapeDtypeStruct((b, s, w // 2), BF16),
        grid=(b, w // (nh * LANE), s // tq),
        in_specs=[pl.BlockSpec((None, tq, nh * LANE), lambda bi, j, i: (bi, i, j)),
                  pl.BlockSpec((None, s, nh * LANE), lambda bi, j, i: (bi, 0, j)),
                  pl.BlockSpec((None, s, nh * LANE), lambda bi, j, i: (bi, 0, j))],
        out_specs=pl.BlockSpec((None, tq, nh * LANE // 2), lambda bi, j, i: (bi, i, j)),
        compiler_params=_cparams(("parallel", "parallel", "arbitrary")),
        name=name,
    )(q, k, v)


def _route_rows(aff, sel):
    gs = []
    for g in range(N_GROUPS):
        r = sel[g * EXPERTS_PER_GROUP:(g + 1) * EXPERTS_PER_GROUP]
        best = None
        for a in range(EXPERTS_PER_GROUP):
            for c in range(a + 1, EXPERTS_PER_GROUP):
                pair = r[a] + r[c]
                best = pair if best is None else jnp.maximum(best, pair)
        gs.append(best)
    gmax = functools.reduce(jnp.maximum, gs)
    gidx = jnp.full_like(gmax, float(N_GROUPS - 1))
    for g in range(N_GROUPS - 2, -1, -1):
        gidx = jnp.where(gs[g] == gmax, float(g), gidx)

    def pick(rows, k):
        out = rows[(N_GROUPS - 1) * EXPERTS_PER_GROUP + k]
        for g in range(N_GROUPS - 2, -1, -1):
            out = jnp.where(gidx == float(g), rows[g * EXPERTS_PER_GROUP + k], out)
        return out

    s = [pick(sel, k) for k in range(EXPERTS_PER_GROUP)]
    a = [pick(aff, k) for k in range(EXPERTS_PER_GROUP)]

    def first_max(vals):
        mx = functools.reduce(jnp.maximum, vals)
        idx = jnp.full_like(mx, float(len(vals) - 1))
        for k in range(len(vals) - 2, -1, -1):
            idx = jnp.where(vals[k] == mx, float(k), idx)
        return idx

    def take(vals, idx):
        out = vals[-1]
        for k in range(len(vals) - 2, -1, -1):
            out = jnp.where(idx == float(k), vals[k], out)
        return out

    i1 = first_max(s)
    s2 = [jnp.where(i1 == float(k), NEG_INF, s[k]) for k in range(EXPERTS_PER_GROUP)]
    i2 = first_max(s2)
    a1, a2 = take(a, i1), take(a, i2)
    tot = a1 + a2
    base = gidx * float(EXPERTS_PER_GROUP)
    return base + i1, base + i2, a1 / tot, a2 / tot


def _outproj_kernel(a_ref, b_ref, x_ref, wa_ref, wb_ref, g_ref, be_ref, rwt_ref, rb_ref,
                    x1_ref, route_ref, *, alpha):
    d = x_ref.shape[1]
    tm = x_ref.shape[0]
    mix = _dot(a_ref[...], wa_ref[...]) + _dot(b_ref[...], wb_ref[...])
    x1 = _layer_norm(alpha * x_ref[...] + mix, g_ref[...], be_ref[...])
    x1_ref[:, 0:d] = x1
    x_hi = x1.astype(BF16)
    x_lo = (x1 - x_hi.astype(F32)).astype(BF16)
    w = rwt_ref[...]
    w_hi = w.astype(BF16)
    w_lo = (w - w_hi.astype(F32)).astype(BF16)
    z_hi = _dot_nt(jnp.concatenate([w_hi, w_lo], axis=0), x_hi)
    zt = z_hi[0:N_EXPERTS] + z_hi[N_EXPERTS:2 * N_EXPERTS] + _dot_nt(w_hi, x_lo)
    afft = 1.0 / (1.0 + jnp.exp(-zt))
    selt = afft + rb_ref[...]
    aff = [afft[e:e + 1, :] for e in range(N_EXPERTS)]
    sel = [selt[e:e + 1, :] for e in range(N_EXPERTS)]
    e1, e2, w1, w2 = _route_rows(aff, sel)
    rec = jnp.concatenate([e1, e2, w1, w2, jnp.zeros((LANE - 4, tm), F32)], axis=0)
    route_ref[...] = rec[0:8]
    x1_ref[:, d:d + LANE] = rec.T


def _outproj(a, b, xt, wa, wb, g, be, rwt, rb, alpha, tm):
    n, d = xt.shape
    full = lambda i: (0, 0)
    return pl.pallas_call(
        functools.partial(_outproj_kernel, alpha=alpha),
        out_shape=[jax.ShapeDtypeStruct((n, d + LANE), F32), jax.ShapeDtypeStruct((8, n), F32)],
        grid=(n // tm,),
        in_specs=[pl.BlockSpec((tm, a.shape[1]), lambda i: (i, 0)),
                  pl.BlockSpec((tm, b.shape[1]), lambda i: (i, 0)),
                  pl.BlockSpec((tm, d), lambda i: (i, 0)),
                  pl.BlockSpec(wa.shape, full), pl.BlockSpec(wb.shape, full),
                  pl.BlockSpec((1, d), full), pl.BlockSpec((1, d), full),
                  pl.BlockSpec(rwt.shape, full), pl.BlockSpec(rb.shape, full)],
        out_specs=[pl.BlockSpec((tm, d + LANE), lambda i: (i, 0)), pl.BlockSpec((8, tm), lambda i: (0, i))],
        compiler_params=_cparams(("parallel",)),
        name="outproj_ln_route",
    )(a, b, xt, wa, wb, g, be, rwt, rb)


def _moe_kernel(bea_ref, beb_ref, bbase_ref, nact_ref, src_ref,
                x_hbm, wga_ref, wgb_ref, wua_ref, wub_ref, wda_ref, wdb_ref, o_ref, xbuf0, xbuf1, sem, *, tm, d):
    i = pl.program_id(0)
    nact = nact_ref[0]
    last = src_ref.shape[0] - 1
    bufs = (xbuf0, xbuf1)

    def issue(blk, slot):
        base = bbase_ref[blk]
        for r in range(tm):
            tok = src_ref[jnp.minimum(base + r, last)]
            pltpu.make_async_copy(x_hbm.at[pl.ds(tok, 1), :], bufs[slot].at[pl.ds(r, 1), :],
                                  sem.at[slot]).start(priority=r % 2)

    def wait(slot):
        pltpu.make_async_copy(x_hbm.at[pl.ds(0, tm), :], bufs[slot], sem.at[slot]).wait()

    @pl.when((i == 0) & (nact > 0))
    def _():
        issue(0, 0)

    def step(slot):
        wait(slot)

        @pl.when(i + 1 < nact)
        def _():
            issue(i + 1, 1 - slot)

        buf = bufs[slot]
        xb = buf[:, 0:d].astype(BF16)
        rec = buf[:, d:d + LANE]
        e1, e2, w1, w2 = rec[:, 0:1], rec[:, 1:2], rec[:, 2:3], rec[:, 3:4]
        out = None
        for e_ref, wg_ref, wu_ref, wd_ref in ((bea_ref, wga_ref, wua_ref, wda_ref),
                                              (beb_ref, wgb_ref, wub_ref, wdb_ref)):
            expert = e_ref[i].astype(F32)
            gate_w = jnp.where(e1 == expert, w1, jnp.where(e2 == expert, w2, 0.0))
            gate = _dot(xb, wg_ref[...])
            up = _dot(xb, wu_ref[...])
            h = gate * (1.0 / (1.0 + jnp.exp(-gate))) * up * gate_w
            part = _dot(h.astype(BF16), wd_ref[...])
            out = part if out is None else out + part
        o_ref[...] = out

    for slot in range(2):
        pl.when((i < nact) & (i % 2 == slot))(functools.partial(step, slot))

    @pl.when(i >= nact)
    def _():
        o_ref[...] = jnp.zeros_like(o_ref)


def _moe_experts(x1e, wg, wu, wd, bea, beb, bbase, nact, src, tm):
    n, de = x1e.shape
    d = de - LANE
    nblk = bea.shape[0]
    f = wg.shape[2]
    amap = lambda i, bea, beb, *_: (bea[i], 0, 0)
    bmap = lambda i, bea, beb, *_: (beb[i], 0, 0)
    grid_spec = pltpu.PrefetchScalarGridSpec(
        num_scalar_prefetch=5,
        grid=(nblk,),
        in_specs=[pl.BlockSpec(memory_space=pl.ANY),
                  pl.BlockSpec((None, d, f), amap), pl.BlockSpec((None, d, f), bmap),
                  pl.BlockSpec((None, d, f), amap), pl.BlockSpec((None, d, f), bmap),
                  pl.BlockSpec((None, f, d), amap), pl.BlockSpec((None, f, d), bmap)],
        out_specs=pl.BlockSpec((tm, d), lambda i, *_: (i, 0)),
        scratch_shapes=[pltpu.VMEM((tm, de), F32), pltpu.VMEM((tm, de), F32), pltpu.SemaphoreType.DMA((2,))],
    )
    return pl.pallas_call(
        functools.partial(_moe_kernel, tm=tm, d=d),
        out_shape=jax.ShapeDtypeStruct((nblk * tm, d), F32),
        grid_spec=grid_spec,
        compiler_params=_cparams(("arbitrary",)),
        name="moe_experts",
    )(bea, beb, bbase, nact, src, x1e, wg, wg, wu, wu, wd, wd)


def _combine_kernel(p_ref, x_ref, g_ref, be_ref, o_hbm, y_ref, obuf0, obuf1, sem, *, alpha, tm, d):
    i = pl.program_id(0)
    nsteps = pl.num_programs(0)

    bufs = (obuf0, obuf1)

    def issue(blk, slot):
        base = blk * tm
        for r in range(tm):
            pltpu.make_async_copy(o_hbm.at[pl.ds(p_ref[base + r], 1), :],
                                  bufs[slot].at[pl.ds(r, 1), :], sem.at[slot]).start(priority=r % 2)

    def wait(slot):
        pltpu.make_async_copy(o_hbm.at[pl.ds(0, tm), :], bufs[slot], sem.at[slot]).wait()

    @pl.when(i == 0)
    def _():
        issue(0, 0)

    def step(slot):
        wait(slot)

        @pl.when(i + 1 < nsteps)
        def _():
            issue(i + 1, 1 - slot)

        y = alpha * x_ref[:, 0:d] + bufs[slot][...]
        y_ref[...] = _layer_norm(y, g_ref[...], be_ref[...])

    for slot in range(2):
        pl.when(i % 2 == slot)(functools.partial(step, slot))


def _combine(x1e, g, be, o, pos, alpha, tm):
    n, de = x1e.shape
    d = de - LANE
    grid_spec = pltpu.PrefetchScalarGridSpec(
        num_scalar_prefetch=1,
        grid=(n // tm,),
        in_specs=[pl.BlockSpec((tm, de), lambda i, *_: (i, 0)),
                  pl.BlockSpec((1, d), lambda i, *_: (0, 0)),
                  pl.BlockSpec((1, d), lambda i, *_: (0, 0)),
                  pl.BlockSpec(memory_space=pl.ANY)],
        out_specs=pl.BlockSpec((tm, d), lambda i, *_: (i, 0)),
        scratch_shapes=[pltpu.VMEM((tm, d), F32), pltpu.VMEM((tm, d), F32), pltpu.SemaphoreType.DMA((2,))],
    )
    return pl.pallas_call(
        functools.partial(_combine_kernel, alpha=alpha, tm=tm, d=d),
        out_shape=jax.ShapeDtypeStruct((n, d), F32),
        grid_spec=grid_spec,
        compiler_params=_cparams(("arbitrary",)),
        name="moe_combine_ln",
    )(pos, x1e, g, be, o)


def _moe_layer(x1e, route_t, wg, wu, wd, g, be, alpha, tm_e, tm_c):
    n = x1e.shape[0]
    pairs = [(a, b) for a in range(EXPERTS_PER_GROUP) for b in range(a + 1, EXPERTS_PER_GROUP)]
    nseg = N_GROUPS * len(pairs)
    seg_a = jnp.array([g * EXPERTS_PER_GROUP + a for g in range(N_GROUPS) for a, b in pairs], I32)
    seg_b = jnp.array([g * EXPERTS_PER_GROUP + b for g in range(N_GROUPS) for a, b in pairs], I32)
    e1, e2 = route_t[0].astype(I32), route_t[1].astype(I32)
    lo, hi = jnp.minimum(e1, e2), jnp.maximum(e1, e2)
    la, lb = lo % EXPERTS_PER_GROUP, hi % EXPERTS_PER_GROUP
    pair_id = la * (2 * EXPERTS_PER_GROUP - 1 - la) // 2 + (lb - la - 1)
    seg = (lo // EXPERTS_PER_GROUP) * len(pairs) + pair_id
    onehot = (seg[:, None] == jnp.arange(nseg, dtype=I32)[None, :]).astype(I32)
    rank = jnp.take_along_axis(jnp.cumsum(onehot, axis=0) - onehot, seg[:, None], axis=1)[:, 0]
    counts = jnp.sum(onehot, axis=0)
    nblk_s = (counts + tm_e - 1) // tm_e
    blk_start = jnp.cumsum(nblk_s) - nblk_s
    seg_start = jnp.cumsum(counts) - counts
    pos = (blk_start[seg] * tm_e + rank).astype(I32)
    src = jnp.argsort(seg, stable=True).astype(I32)
    nblk = n // tm_e + nseg
    blk = jnp.arange(nblk, dtype=I32)
    blk_end = jnp.cumsum(nblk_s)
    bseg = jnp.minimum(jnp.sum((blk[:, None] >= blk_end[None, :]).astype(I32), axis=1), nseg - 1)
    bbase = (seg_start[bseg] + (blk - blk_start[bseg]) * tm_e).astype(I32)
    nact = jnp.sum(nblk_s).astype(I32).reshape(1)
    o = _moe_experts(x1e, wg, wu, wd, seg_a[bseg], seg_b[bseg], bbase, nact, src, tm_e)
    return _combine(x1e, g, be, o, pos, alpha, tm_c)


def _rms(x, g):
    return x * lax.rsqrt(jnp.mean(x * x, axis=-1, keepdims=True) + RMS_EPS) * g

ODD_OFF = dict(qc=0, kc=512, vc=640, qi=768, ki=1024, wi=1152, cq=1280, ckv=1536, kra=1664, krb=1792, end=1920)


def _odd_proj_kernel(x_ref, w_ref, qng_ref, kvng_ref, wqa_ref, wqb_ref, wk_ref, wv_ref, cos_ref, sin_ref,
                     tq_ref, tk_ref,
                     qc_ref, kc_ref, vc_ref, qi_ref, ki_ref, wi_ref, q8_ref, k8_ref, v8_ref, *, q_scale):
    o = ODD_OFF
    tm = x_ref.shape[0]
    lo = _lane_lo((tm, LANE))
    h = _dot(x_ref[...].astype(BF16), w_ref[...])
    kdup = h[:, o["kc"]:o["vc"]]
    vdup = h[:, o["vc"]:o["qi"]]
    kc_ref[:, 0:LANE] = jnp.where(lo, kdup, tk_ref[:, 0:LANE].astype(F32)).astype(BF16)
    kc_ref[:, LANE:2 * LANE] = jnp.where(lo, tk_ref[:, LANE:2 * LANE].astype(F32), kdup).astype(BF16)
    vc_ref[:, 0:LANE] = jnp.where(lo, vdup, 1.0).astype(BF16)
    vc_ref[:, LANE:2 * LANE] = jnp.where(lo, 1.0, vdup).astype(BF16)
    qi_ref[...] = h[:, o["qi"]:o["ki"]].astype(BF16)
    ki_ref[...] = h[:, o["ki"]:o["wi"]].astype(BF16)
    wi_ref[...] = h[:, o["wi"]:o["wi"] + IDX_HEADS]
    cos = cos_ref[...]
    sin = sin_ref[...]
    cqn = _rms(h[:, o["cq"]:o["ckv"]], qng_ref[...]).astype(BF16)
    qa = _dot(cqn, wqa_ref[...])
    qb = _dot(cqn, wqb_ref[...])
    ckvn = _rms(h[:, o["ckv"]:o["kra"]], kvng_ref[...]).astype(BF16)
    kn = _dot(ckvn, wk_ref[...])
    v8 = _dot(ckvn, wv_ref[...])
    kr = h[:, o["kra"]:o["krb"]] * cos + h[:, o["krb"]:o["end"]] * sin
    for hd in range(D_HEADS):
        sl = slice(hd * LANE, (hd + 1) * LANE)
        pj = slice((hd // 2) * LANE, (hd // 2 + 1) * LANE)
        own = lo if hd % 2 == 0 else ~lo
        qc_ref[:, sl] = jnp.where(own, h[:, pj], tq_ref[:, sl].astype(F32)).astype(BF16)
        q8_ref[:, sl] = ((qa[:, sl] * cos + qb[:, sl] * sin) * q_scale).astype(BF16)
        k8_ref[:, sl] = (kn[:, sl] + kr).astype(BF16)
        v8_ref[:, sl] = jnp.where(own, v8[:, pj], 1.0).astype(BF16)


def _odd_proj(xt, w_cat, qng, kvng, wqa, wqb, wk, wv, cos_t, sin_t, tq_t, tk_t, seq, tm):
    n, d = xt.shape
    full = lambda i: (0, 0)
    tps = seq // tm
    pos = lambda i: (i % tps, 0)
    widths = (1024, 256, 256, 256, 128)
    out_shape = [jax.ShapeDtypeStruct((n, w), BF16) for w in widths]
    out_specs = [pl.BlockSpec((tm, w), lambda i: (i, 0)) for w in widths]
    out_shape.append(jax.ShapeDtypeStruct((n, IDX_HEADS), F32))
    out_specs.append(pl.BlockSpec((tm, IDX_HEADS), lambda i: (i, 0)))
    for w in (1024, 1024, 1024):
        out_shape.append(jax.ShapeDtypeStruct((n, w), BF16))
        out_specs.append(pl.BlockSpec((tm, w), lambda i: (i, 0)))
    return pl.pallas_call(
        functools.partial(_odd_proj_kernel, q_scale=(MLA_NOPE + MLA_ROPE) ** -0.5 * LOG2E),
        out_shape=out_shape,
        grid=(n // tm,),
        in_specs=[pl.BlockSpec((tm, d), lambda i: (i, 0)),
                  pl.BlockSpec(w_cat.shape, full),
                  pl.BlockSpec(qng.shape, full), pl.BlockSpec(kvng.shape, full),
                  pl.BlockSpec(wqa.shape, full), pl.BlockSpec(wqb.shape, full),
                  pl.BlockSpec(wk.shape, full), pl.BlockSpec(wv.shape, full),
                  pl.BlockSpec((tm, LANE), pos), pl.BlockSpec((tm, LANE), pos),
                  pl.BlockSpec((tm, C_HEADS * LANE), pos), pl.BlockSpec((tm, 2 * LANE), pos)],
        out_specs=out_specs,
        compiler_params=_cparams(("parallel",)),
        name="odd_proj",
    )(xt, w_cat, qng, kvng, wqa, wqb, wk, wv, cos_t, sin_t, tq_t, tk_t)


def _dsa_kernel(q_ref, k_ref, v_ref, qi_ref, ki_ref, wt_ref, o_ref, sc_ref, mb_ref, *, top_k, chunk, qb):
    i = pl.program_id(1)
    nc = (i * qb + qb + chunk - 1) // chunk
    lo = _lane_lo((qb, LANE))
    s_row2 = lax.broadcasted_iota(I32, (2 * chunk, qb), 0)
    t_lane2 = i * qb + lax.broadcasted_iota(I32, (2 * chunk, qb), 1)

    def chunk_start(c):
        return pl.multiple_of(c * chunk, chunk)

    qidx = []
    for h in range(IDX_HEADS):
        qp = qi_ref[:, (h // 2) * LANE:(h // 2 + 1) * LANE]
        qidx.append(jnp.where(lo if h % 2 == 0 else ~lo, qp, jnp.zeros_like(qp)))
    q_stack = jnp.concatenate(qidx, axis=0)
    w = wt_ref[...]

    npair = (nc + 1) // 2

    def pair_start(cp):
        return cp * (2 * chunk)

    def select(pairs):
        def score_pair(cp):
            start = pair_start(cp)
            d = _dot_nt(ki_ref[pl.ds(start, 2 * chunk), :], q_stack)
            sc = w[0:1, :] * jnp.maximum(d[:, 0:qb], 0.0)
            for h in range(1, IDX_HEADS):
                sc = sc + w[h:h + 1, :] * jnp.maximum(d[:, h * qb:(h + 1) * qb], 0.0)
            sc = jnp.where(sc == 0.0, 0.0, sc)
            sc_ref[pl.ds(start, 2 * chunk), :] = jnp.where(start + s_row2 <= t_lane2, sc, NEG_INF)

        for cp in range(pairs):
            score_pair(cp)

        nacc = 4 * 8

        def count(trial, strict):
            acc = jnp.zeros((nacc, qb), F32)
            for cp in range(pairs):
                sc = sc_ref[pl.ds(pair_start(cp), 2 * chunk), :]
                hit = (sc > trial) if strict else (sc >= trial)
                acc = acc + jnp.sum(jnp.where(hit, 1.0, 0.0).reshape(2 * chunk // nacc, nacc, qb), axis=0)
            return jnp.sum(acc, axis=0, keepdims=True)

        def decode(key):
            return lax.bitcast_convert_type(jnp.where(key < 0, key ^ jnp.int32(0x7FFFFFFF), key), F32)

        kf = float(top_k)
        key_neg_inf = jnp.int32(-2139095041)
        cand = jnp.where(count(jnp.zeros((1, qb), F32), False) >= kf, 0, jnp.int32(-2 ** 31)).astype(I32)

        def bit_step(j, cand):
            trial = cand + lax.shift_left(jnp.int32(1), 30 - j)
            ok = (trial <= key_neg_inf) | (count(decode(trial), False) >= kf)
            return jnp.where(ok, trial, cand)

        thr = decode(lax.fori_loop(0, 31, bit_step, cand))
        need = kf - count(thr, True)

        r0 = lax.broadcasted_iota(I32, (chunk, chunk), 0)
        r1 = lax.broadcasted_iota(I32, (chunk, chunk), 1)
        below = jnp.where(r1 < r0, 1.0, 0.0).astype(BF16)
        need = jnp.where(thr == NEG_INF, 0.0, need)

        offset = jnp.zeros((1, qb), F32)
        for cp in range(pairs):
            start = pair_start(cp)
            sc = sc_ref[pl.ds(start, 2 * chunk), :]
            eq = sc == thr
            eqf = jnp.where(eq, 1.0, 0.0)
            eqb = eqf.astype(BF16)
            first = jnp.sum(eqf[0:chunk], axis=0, keepdims=True)
            prefix = jnp.concatenate([_dot(below, eqb[0:chunk]) + offset,
                                      _dot(below, eqb[chunk:2 * chunk]) + (offset + first)], axis=0)
            bias = jnp.where(sc > thr, 0.0, jnp.where(eq, jnp.where(prefix < need, 0.0, NEG_INF), NEG_INF))
            mb_ref[:, pl.ds(start, 2 * chunk)] = bias.T
            offset = offset + first + jnp.sum(eqf[chunk:2 * chunk], axis=0, keepdims=True)

    for pairs in range(1, k_ref.shape[0] // (2 * chunk) + 1):
        pl.when(npair == pairs)(functools.partial(select, pairs))

    nslab = qb // BLOCK
    chains = [(h, r) for h in range(C_HEADS) for r in range(nslab)]

    def att_chunk(c, carry):
        start = chunk_start(c)
        kp = [k_ref[pl.ds(start, chunk), par * LANE:(par + 1) * LANE] for par in range(2)]
        vp = [v_ref[pl.ds(start, chunk), par * LANE:(par + 1) * LANE] for par in range(2)]
        mb = [mb_ref[r * BLOCK:(r + 1) * BLOCK, pl.ds(start, chunk)].astype(BF16) for r in range(nslab)]
        qs = [q_ref[r * BLOCK:(r + 1) * BLOCK, h * LANE:(h + 1) * LANE] for h, r in chains]
        return _online_softmax_chains(qs, [kp[h % 2] for h, r in chains], [vp[h % 2] for h, r in chains], carry,
                                      biases=[mb[r] for h, r in chains], guard_empty=True)

    init = tuple((jnp.full((BLOCK, 1), NEG_INF, F32), jnp.zeros((BLOCK, LANE), F32)) for _ in chains)
    carry = lax.fori_loop(0, nc, att_chunk, init)
    accs = [jnp.concatenate([carry[h * nslab + r][1] for r in range(nslab)], axis=0) for h in range(C_HEADS)]
    _normalize_pairs(accs, o_ref)


def _dsa(qc, kc, vc, qi, ki, wit):
    b, s, _ = qc.shape
    top_k = min(DSA_TOPK_MAX, s // 4)
    chunk = 256
    qb = min(2 * BLOCK, s)
    assert s % (2 * chunk) == 0 and top_k <= chunk and chunk % qb == 0
    blk = lambda bi, i: (bi, i, 0)
    seq = lambda bi, i: (bi, 0, 0)
    return pl.pallas_call(
        functools.partial(_dsa_kernel, top_k=top_k, chunk=chunk, qb=qb),
        out_shape=jax.ShapeDtypeStruct((b, s, C_HEADS * HEAD_DIM), BF16),
        grid=(b, s // qb),
        in_specs=[pl.BlockSpec((None, qb, C_HEADS * LANE), blk),
                  pl.BlockSpec((None, s, 2 * LANE), seq),
                  pl.BlockSpec((None, s, 2 * LANE), seq),
                  pl.BlockSpec((None, qb, 256), blk),
                  pl.BlockSpec((None, s, LANE), seq),
                  pl.BlockSpec((None, IDX_HEADS, qb), lambda bi, i: (bi, 0, i))],
        out_specs=pl.BlockSpec((None, qb, 512), blk),
        scratch_shapes=[pltpu.VMEM((s, qb), F32), pltpu.VMEM((qb, s), F32)],
        compiler_params=_cparams(("parallel", "arbitrary")),
        name="dsa_attn",
    )(qc, kc, vc, qi, ki, wit)


def _dup64(w):
    d, c = w.shape
    w = w.reshape(d, c // HEAD_DIM, 1, HEAD_DIM)
    return jnp.broadcast_to(w, (d, c // HEAD_DIM, 2, HEAD_DIM)).reshape(d, 2 * c)


def _pad_cols(w, width):
    return jnp.pad(w, ((0, 0), (0, width - w.shape[1])))


def _split_cols(w, sizes):
    out, start = [], 0
    for n in sizes:
        out.append(w[:, start:start + n])
        start += n
    return out


def _rope_partner(w):
    half = MLA_ROPE // 2
    return jnp.concatenate([-w[..., half:], w[..., :half]], axis=-1)


def _rope_block(w_rope):
    d = w_rope.shape[0]
    return jnp.concatenate([jnp.zeros((d, MLA_NOPE), F32), w_rope,
                            jnp.zeros((d, LANE - MLA_NOPE - MLA_ROPE), F32)], axis=1)


def _alibi_tables(s):
    slopes = 2.0 ** (-8.0 * jnp.arange(1, C_HEADS + 1, dtype=F32) / C_HEADS)
    a = slopes * LOG2E
    pos = jnp.arange(s, dtype=F32)
    a_sp = _split3(a)
    r_sp = _split3(-(a[None, :] * pos[:, None]))
    ones = jnp.ones((s, C_HEADS), F32)
    q_terms = [128.0 * t * ones for t in a_sp] + [t * ones for t in a_sp] + list(r_sp)
    q_aug = jnp.stack(q_terms, axis=-1)
    q_aug = jnp.pad(q_aug, ((0, 0), (0, 0), (0, HEAD_DIM - q_aug.shape[-1])))
    zeros = jnp.zeros_like(q_aug)
    even = jnp.concatenate([zeros, q_aug], axis=-1)
    odd = jnp.concatenate([q_aug, zeros], axis=-1)
    parity = (jnp.arange(C_HEADS) % 2 == 0)[None, :, None]
    tq_t = jnp.where(parity, even, odd).reshape(s, C_HEADS * LANE)
    s_hi = jnp.floor(pos / LANE)
    s_lo = pos - LANE * s_hi
    k_aug = jnp.stack([s_hi] * 3 + [s_lo] * 3 + [jnp.ones_like(pos)] * 3, axis=-1)
    k_aug = jnp.pad(k_aug, ((0, 0), (0, HEAD_DIM - k_aug.shape[-1])))
    kz = jnp.zeros_like(k_aug)
    tk_t = jnp.concatenate([kz, k_aug, k_aug, kz], axis=-1)
    return tq_t.astype(BF16), tk_t.astype(BF16)


def kernel(x, even_w_in, even_b_f, even_sinks, odd_w_in, odd_q_norm, odd_kv_norm, odd_w_uq, odd_w_ukv,
           w_o, ln_g, ln_b, router_w, router_b, moe_w_gate, moe_w_up, moe_w_down):
    b, s, d = x.shape
    n = b * s
    depth = w_o.shape[0]
    alpha = (2 * depth) ** 0.25
    tm = min(512, s)
    tq = min(256, s)
    tm_e = min(256, n)
    tm_c = min(256, n)
    att_scale = HEAD_DIM ** -0.5
    xt = x.reshape(n, d)
    rwt = router_w.T
    rb = router_b.reshape(N_EXPERTS, 1)

    for l in range(depth):
        j = l // 2
        if l % 2 == 0:
            qa, ka, va, qb, kb, vb, fg = _split_cols(
                even_w_in[j], (512, 128, 128, 512, 512, 512, B_HEADS))
            w_cat = jnp.concatenate([qa * (att_scale * LOG2E), _dup64(ka), _dup64(va), qb * (att_scale * LOG2E), kb, vb,
                                     _pad_cols(fg, LANE)], axis=1).astype(BF16)
            bf_pad = _pad_cols(even_b_f[j].reshape(1, B_HEADS), LANE)
            qa_, ka_, va_, qf, kf, vf = _even_proj(xt, w_cat, bf_pad, s, tm)
            r3 = lambda a: a.reshape(b, s, a.shape[1])
            out_a = _swa(even_sinks[j], r3(qa_), r3(ka_), r3(va_))
            out_b = _flash(r3(qf), r3(kf), r3(vf), tq=tq, nh=8, name="fox_attn")
        else:
            (qc, kc, vc, qi, ki, wi, cq, ckv, kr) = _split_cols(
                odd_w_in[j], (512, 64, 64, IDX_HEADS * IDX_DIM, IDX_DIM, IDX_HEADS,
                              MLA_Q_RANK, MLA_KV_RANK, MLA_ROPE))
            w_cat = jnp.concatenate([
                qc * (att_scale * LOG2E), _dup64(kc), _dup64(vc), qi * (IDX_DIM ** -0.5), _dup64(ki),
                _pad_cols(wi * (IDX_HEADS ** -0.5), LANE), cq, ckv,
                _rope_block(kr), _rope_block(_rope_partner(kr))], axis=1).astype(BF16)
            dq = MLA_NOPE + MLA_ROPE
            wuq = odd_w_uq[j].reshape(MLA_Q_RANK, D_HEADS, dq)
            pad_q = jnp.zeros((MLA_Q_RANK, D_HEADS, LANE - dq), F32)
            wqa = jnp.concatenate([wuq, pad_q], axis=2).reshape(MLA_Q_RANK, D_HEADS * LANE).astype(BF16)
            wqb = jnp.concatenate([jnp.zeros((MLA_Q_RANK, D_HEADS, MLA_NOPE), F32),
                                   _rope_partner(wuq[..., MLA_NOPE:]), pad_q],
                                  axis=2).reshape(MLA_Q_RANK, D_HEADS * LANE).astype(BF16)
            wukv = odd_w_ukv[j].reshape(MLA_KV_RANK, D_HEADS, MLA_NOPE + MLA_V)
            wk = jnp.concatenate([wukv[..., :MLA_NOPE], jnp.zeros((MLA_KV_RANK, D_HEADS, LANE - MLA_NOPE), F32)],
                                 axis=2).reshape(MLA_KV_RANK, D_HEADS * LANE).astype(BF16)
            wv = wukv[..., MLA_NOPE:].reshape(MLA_KV_RANK, D_HEADS * MLA_V).astype(BF16)
            half = MLA_ROPE // 2
            inv = ROPE_THETA ** (-jnp.arange(half, dtype=F32) / half)
            ang = jnp.arange(s, dtype=F32)[:, None] * inv[None, :]
            cos_h, sin_h = jnp.cos(ang), jnp.sin(ang)
            pad_t = jnp.zeros((s, LANE - dq), F32)
            cos_t = jnp.concatenate([jnp.ones((s, MLA_NOPE), F32), cos_h, cos_h, pad_t], axis=1)
            sin_t = jnp.concatenate([jnp.zeros((s, MLA_NOPE), F32), sin_h, sin_h, pad_t], axis=1)
            tq_t, tk_t = _alibi_tables(s)
            qc_, kc_, vc_, qi_, ki_, wi_, q8, k8, v8 = _odd_proj(
                xt, w_cat, odd_q_norm[j].reshape(1, -1), odd_kv_norm[j].reshape(1, -1),
                wqa, wqb, wk, wv, cos_t, sin_t, tq_t, tk_t, s, tm)
            r3 = lambda a: a.reshape(b, s, a.shape[1])
            out_a = _dsa(r3(qc_), r3(kc_), r3(vc_), r3(qi_), r3(ki_), r3(wi_).transpose(0, 2, 1))
            out_b = _flash(r3(q8), r3(k8), r3(v8), tq=tq, nh=8, name="mla_attn")
        half_w = out_a.shape[2]
        wo = w_o[l].astype(BF16)
        x1e, route_t = _outproj(out_a.reshape(n, half_w), out_b.reshape(n, -1), xt,
                                wo[:half_w], wo[half_w:], ln_g[l, 0].reshape(1, d), ln_b[l, 0].reshape(1, d),
                                rwt, rb, alpha, tm)
        xt = _moe_layer(x1e, route_t, moe_w_gate[l].astype(BF16), moe_w_up[l].astype(BF16),
                        moe_w_down[l].astype(BF16), ln_g[l, 1].reshape(1, d), ln_b[l, 1].reshape(1, d),
                        alpha, tm_e, tm_c)
    return xt.reshape(b, s, d)
```

```python
import functools
import math

import jax
import jax.numpy as jnp
from jax import lax
from jax.experimental import pallas as pl
from jax.experimental.pallas import tpu as pltpu

F32 = jnp.float32
BF16 = jnp.bfloat16
I32 = jnp.int32

LANE = 128
HEAD_DIM = 64
BLOCK = 128
A_HEADS, A_KV_HEADS, WINDOW = 8, 2, 128
B_HEADS = 8
C_HEADS, IDX_HEADS, IDX_DIM, DSA_TOPK_MAX = 8, 4, 64, 256
D_HEADS, MLA_Q_RANK, MLA_KV_RANK, MLA_NOPE, MLA_ROPE, MLA_V = 8, 256, 128, 64, 32, 64
ROPE_THETA = 10000.0
N_EXPERTS, N_GROUPS = 16, 4
EXPERTS_PER_GROUP = N_EXPERTS // N_GROUPS
LN_EPS, RMS_EPS = 1e-5, 1e-6
NEG_INF = float("-inf")
LOG2E = math.log2(math.e)
VMEM_LIMIT = 56 * 1024 * 1024


def _cparams(sem):
    return pltpu.CompilerParams(dimension_semantics=sem, vmem_limit_bytes=VMEM_LIMIT)


def _dot(a, b):
    return jnp.dot(a, b, preferred_element_type=F32)


def _dot_nt(a, b):
    return lax.dot_general(a, b, (((1,), (1,)), ((), ())), preferred_element_type=F32)


def _lane_lo(shape):
    return lax.broadcasted_iota(I32, shape, len(shape) - 1) % LANE < HEAD_DIM


def _layer_norm(y, g, b):
    mu = jnp.mean(y, axis=-1, keepdims=True)
    yc = y - mu
    var = jnp.mean(yc * yc, axis=-1, keepdims=True)
    return yc * lax.rsqrt(var + LN_EPS) * g + b


def _top16(v):
    bits = lax.bitcast_convert_type(v, I32) & jnp.int32(-65536)
    return lax.bitcast_convert_type(bits, F32)


def _split3(x):
    s1 = _top16(x)
    r = x - s1
    s2 = _top16(r)
    return s1, s2, r - s2


def _aug_rel(shape, h):
    lane = lax.broadcasted_iota(I32, shape, 1)
    return lane - (HEAD_DIM if h % 2 == 0 else 0)


def _even_proj_kernel(x_ref, w_ref, bf_ref, qa_ref, ka_ref, va_ref, qf_ref, kf_ref, vf_ref,
                      carry_ref, *, tiles_per_seq):
    t = pl.program_id(0)
    tm = x_ref.shape[0]
    h = _dot(x_ref[...].astype(BF16), w_ref[...])
    qa_ref[...] = h[:, 0:512].astype(BF16)
    ka_ref[...] = h[:, 512:768].astype(BF16)
    va_ref[...] = h[:, 768:1024].astype(BF16)
    z = h[:, 2560:2688] + bf_ref[...]
    logf = jnp.minimum(z, 0.0) - jnp.log1p(jnp.exp(-jnp.abs(z)))

    @pl.when(t % tiles_per_seq == 0)
    def _():
        carry_ref[...] = jnp.zeros_like(carry_ref)

    row = lax.broadcasted_iota(I32, (BLOCK, BLOCK), 0)
    col = lax.broadcasted_iota(I32, (BLOCK, BLOCK), 1)
    tri = jnp.where(row >= col, 1.0, 0.0).astype(BF16)
    parts = [p.astype(BF16) for p in _split3(logf)]
    run = carry_ref[...]
    blocks = []
    for blk in range(tm // BLOCK):
        cb = run
        for part in parts:
            cb = cb + _dot(tri, part[blk * BLOCK:(blk + 1) * BLOCK])
        blocks.append(cb)
        run = cb[BLOCK - 1:BLOCK, :]
    c = jnp.concatenate(blocks, axis=0)
    carry_ref[...] = run

    s1, s2, s3 = _split3(c * LOG2E)
    lo = _lane_lo((tm, LANE))
    for hd in range(B_HEADS):
        own = lo if hd % 2 == 0 else ~lo
        rel = _aug_rel((tm, LANE), hd)
        b1, b2, b3 = s1[:, hd:hd + 1], s2[:, hd:hd + 1], s3[:, hd:hd + 1]
        q_aug = jnp.where(rel == 3, b1, jnp.where(rel == 4, b2, jnp.where(rel == 5, b3,
                          jnp.where((rel >= 0) & (rel < 3), -1.0, 0.0))))
        k_aug = jnp.where(rel == 0, b1, jnp.where(rel == 1, b2, jnp.where(rel == 2, b3,
                          jnp.where((rel >= 3) & (rel < 6), 1.0, 0.0))))
        pj = (hd // 2) * LANE
        sl = slice(hd * LANE, (hd + 1) * LANE)
        qf_ref[:, sl] = jnp.where(own, h[:, 1024 + pj:1024 + pj + LANE], q_aug).astype(BF16)
        kf_ref[:, sl] = jnp.where(own, h[:, 1536 + pj:1536 + pj + LANE], k_aug).astype(BF16)
        vf_ref[:, sl] = jnp.where(own, h[:, 2048 + pj:2048 + pj + LANE], 1.0).astype(BF16)


def _even_proj(xt, w_cat, bf_pad, seq, tm):
    n, d = xt.shape
    wcols = w_cat.shape[1]
    widths = (512, 256, 256, 1024, 1024, 1024)
    out_shape = [jax.ShapeDtypeStruct((n, w), BF16) for w in widths]
    out_specs = [pl.BlockSpec((tm, w), lambda i: (i, 0)) for w in widths]
    return pl.pallas_call(
        functools.partial(_even_proj_kernel, tiles_per_seq=seq // tm),
        out_shape=out_shape,
        grid=(n // tm,),
        in_specs=[pl.BlockSpec((tm, d), lambda i: (i, 0)),
                  pl.BlockSpec((d, wcols), lambda i: (0, 0)),
                  pl.BlockSpec((1, LANE), lambda i: (0, 0))],
        out_specs=out_specs,
        scratch_shapes=[pltpu.VMEM((1, LANE), F32)],
        compiler_params=_cparams(("arbitrary",)),
        name="even_proj",
    )(xt, w_cat, bf_pad)


def _swa_kernel(sink_ref, q_ref, kp_ref, kc_ref, vp_ref, vc_ref, o_ref, *, nbq):
    i = pl.program_id(1)
    group = A_HEADS // A_KV_HEADS
    tl = lax.broadcasted_iota(I32, (BLOCK, 2 * BLOCK), 0)
    sl = lax.broadcasted_iota(I32, (BLOCK, 2 * BLOCK), 1)
    dist = BLOCK + tl - sl
    in_window = (dist >= 0) & (dist < WINDOW)
    distf = dist.astype(F32)
    lo = _lane_lo((BLOCK, LANE))

    def band(prev_ref, cur_ref, j, g):
        lanes = slice(g * LANE, (g + 1) * LANE)
        before = prev_ref[:, lanes] if j == 0 else cur_ref[(j - 1) * BLOCK:j * BLOCK, lanes]
        return jnp.concatenate([before, cur_ref[j * BLOCK:(j + 1) * BLOCK, lanes]], axis=0)

    stacked = {}
    for j in range(nbq):
        for g in range(A_KV_HEADS):
            qs = []
            for h in range(g * group, (g + 1) * group):
                qp = q_ref[j * BLOCK:(j + 1) * BLOCK, (h // 2) * LANE:(h // 2 + 1) * LANE]
                qs.append(jnp.where(lo if h % 2 == 0 else ~lo, qp, jnp.zeros_like(qp)))
            stacked[j, g] = _dot_nt(jnp.concatenate(qs, axis=0), band(kp_ref, kc_ref, j, g))
    ps, denoms = {}, {}
    for j in range(nbq):
        valid = (in_window & ((sl >= BLOCK) | (i > 0))) if j == 0 else in_window
        for h in range(A_HEADS):
            r = h % group
            slope = 2.0 ** (-8.0 * (h + 1) / A_HEADS) * LOG2E
            logits = jnp.where(valid, stacked[j, h // group][r * BLOCK:(r + 1) * BLOCK] - slope * distf, NEG_INF)
            sink = sink_ref[h] * LOG2E
            m = jnp.maximum(jnp.max(logits, axis=-1, keepdims=True), sink)
            e = jnp.exp2(logits - m)
            denoms[j, h] = jnp.sum(e, axis=-1, keepdims=True) + jnp.exp2(sink - m)
            ps[j, h] = e.astype(BF16)
    for j in range(nbq):
        outs = []
        for g in range(A_KV_HEADS):
            heads = range(g * group, (g + 1) * group)
            pv = _dot(jnp.concatenate([ps[j, h] for h in heads], axis=0), band(vp_ref, vc_ref, j, g))
            for r, h in enumerate(heads):
                outs.append(pv[r * BLOCK:(r + 1) * BLOCK] / denoms[j, h])
        for pj in range(A_HEADS // 2):
            o_ref[j * BLOCK:(j + 1) * BLOCK, pj * LANE:(pj + 1) * LANE] = jnp.where(
                lo, outs[2 * pj], outs[2 * pj + 1]).astype(BF16)


def _swa(sinks, qa, ka, va):
    b, s, _ = qa.shape
    nbq = min(4, s // BLOCK)
    prev = lambda bi, i: (bi, jnp.maximum(i * nbq - 1, 0), 0)
    cur = lambda bi, i: (bi, i, 0)
    return pl.pallas_call(
        functools.partial(_swa_kernel, nbq=nbq),
        out_shape=jax.ShapeDtypeStruct((b, s, A_HEADS * HEAD_DIM), BF16),
        grid=(b, s // (nbq * BLOCK)),
        in_specs=[pl.BlockSpec(memory_space=pltpu.SMEM),
                  pl.BlockSpec((None, nbq * BLOCK, 512), cur),
                  pl.BlockSpec((None, BLOCK, 256), prev),
                  pl.BlockSpec((None, nbq * BLOCK, 256), cur),
                  pl.BlockSpec((None, BLOCK, 256), prev),
                  pl.BlockSpec((None, nbq * BLOCK, 256), cur)],
        out_specs=pl.BlockSpec((None, nbq * BLOCK, 512), cur),
        compiler_params=_cparams(("parallel", "arbitrary")),
        name="swa_attn",
    )(sinks, qa, ka, ka, va, va)


def _normalize_pairs(accs, o_ref):
    lo = _lane_lo(accs[0].shape)
    outs = [a / pltpu.roll(a, HEAD_DIM, 1) for a in accs]
    for j in range(len(accs) // 2):
        o_ref[:, j * LANE:(j + 1) * LANE] = jnp.where(lo, outs[2 * j], outs[2 * j + 1]).astype(BF16)


def _online_softmax_chains(qs, ks, vs, carry, biases=None, masks=None, guard_empty=False):
    n = len(qs)
    ss = [_dot_nt(qs[i], ks[i]).astype(BF16) for i in range(n)]
    if biases is not None:
        ss = [ss[i] + biases[i] for i in range(n)]
    if masks is not None:
        ss = [jnp.where(masks[i], ss[i], NEG_INF) for i in range(n)]
    ms, ps, alphas = [], [], []
    for i in range(n):
        m = carry[i][0]
        m_new = jnp.maximum(m, jnp.max(ss[i], axis=-1, keepdims=True).astype(F32))
        m_ref = jnp.where(m_new == NEG_INF, 0.0, m_new) if guard_empty else m_new
        ps.append(jnp.exp2(ss[i] - m_ref.astype(BF16)))
        alphas.append(jnp.exp2(m - m_ref))
        ms.append(m_new)
    pvs = [_dot(ps[i], vs[i]) for i in range(n)]
    return tuple((ms[i], alphas[i] * carry[i][1] + pvs[i]) for i in range(n))


def _flash_kernel(q_ref, k_ref, v_ref, o_ref, *, tq, nh):
    qi = pl.program_id(2)
    nslab = tq // BLOCK
    chains = [(hh, r) for hh in range(nh) for r in range(nslab)]
    row = lax.broadcasted_iota(I32, (BLOCK, tq), 0)
    col = lax.broadcasted_iota(I32, (BLOCK, tq), 1)

    def lanes(hh):
        return slice(hh * LANE, (hh + 1) * LANE)

    def step(kb, carry, diagonal):
        start = pl.multiple_of(kb * tq, tq)
        qs = [q_ref[r * BLOCK:(r + 1) * BLOCK, lanes(hh)] for hh, r in chains]
        if diagonal:
            ks = [k_ref[pl.ds(start, (r + 1) * BLOCK), lanes(hh)] for hh, r in chains]
            vs = [v_ref[pl.ds(start, (r + 1) * BLOCK), lanes(hh)] for hh, r in chains]
            masks = [(col <= row + r * BLOCK)[:, 0:(r + 1) * BLOCK] for hh, r in chains]
        else:
            ks = [k_ref[pl.ds(start, tq), lanes(hh)] for hh, r in chains]
            vs = [v_ref[pl.ds(start, tq), lanes(hh)] for hh, r in chains]
            masks = None
        return _online_softmax_chains(qs, ks, vs, carry, masks=masks)

    init = tuple((jnp.full((BLOCK, 1), NEG_INF, F32), jnp.zeros((BLOCK, LANE), F32))
                 for _ in range(nh * nslab))
    carry = lax.fori_loop(0, qi, lambda kb, c: step(kb, c, False), init)
    carry = step(qi, carry, True)
    accs = [jnp.concatenate([carry[hh * nslab + r][1] for r in range(nslab)], axis=0) for hh in range(nh)]
    _normalize_pairs(accs, o_ref)


def _flash(q, k, v, *, tq, nh, name):
    b, s, w = v.shape
    return pl.pallas_call(
        functools.partial(_flash_kernel, tq=tq, nh=nh),
        out_shape=jax.ShapeDtypeStruct((b, s, w // 2), BF16),
        grid=(b, w // (nh * LANE), s // tq),
        in_specs=[pl.BlockSpec((None, tq, nh * LANE), lambda bi, j, i: (bi, i, j)),
                  pl.BlockSpec((None, s, nh * LANE), lambda bi, j, i: (bi, 0, j)),
                  pl.BlockSpec((None, s, nh * LANE), lambda bi, j, i: (bi, 0, j))],
        out_specs=pl.BlockSpec((None, tq, nh * LANE // 2), lambda bi, j, i: (bi, i, j)),
        compiler_params=_cparams(("parallel", "parallel", "arbitrary")),
        name=name,
    )(q, k, v)


def _route_rows(aff, sel):
    gs = []
    for g in range(N_GROUPS):
        r = sel[g * EXPERTS_PER_GROUP:(g + 1) * EXPERTS_PER_GROUP]
        best = None
        for a in range(EXPERTS_PER_GROUP):
            for c in range(a + 1, EXPERTS_PER_GROUP):
                pair = r[a] + r[c]
                best = pair if best is None else jnp.maximum(best, pair)
        gs.append(best)
    gmax = functools.reduce(jnp.maximum, gs)
    gidx = jnp.full_like(gmax, float(N_GROUPS - 1))
    for g in range(N_GROUPS - 2, -1, -1):
        gidx = jnp.where(gs[g] == gmax, float(g), gidx)

    def pick(rows, k):
        out = rows[(N_GROUPS - 1) * EXPERTS_PER_GROUP + k]
        for g in range(N_GROUPS - 2, -1, -1):
            out = jnp.where(gidx == float(g), rows[g * EXPERTS_PER_GROUP + k], out)
        return out

    s = [pick(sel, k) for k in range(EXPERTS_PER_GROUP)]
    a = [pick(aff, k) for k in range(EXPERTS_PER_GROUP)]

    def first_max(vals):
        mx = functools.reduce(jnp.maximum, vals)
        idx = jnp.full_like(mx, float(len(vals) - 1))
        for k in range(len(vals) - 2, -1, -1):
            idx = jnp.where(vals[k] == mx, float(k), idx)
        return idx

    def take(vals, idx):
        out = vals[-1]
        for k in range(len(vals) - 2, -1, -1):
            out = jnp.where(idx == float(k), vals[k], out)
        return out

    i1 = first_max(s)
    s2 = [jnp.where(i1 == float(k), NEG_INF, s[k]) for k in range(EXPERTS_PER_GROUP)]
    i2 = first_max(s2)
    a1, a2 = take(a, i1), take(a, i2)
    tot = a1 + a2
    base = gidx * float(EXPERTS_PER_GROUP)
    return base + i1, base + i2, a1 / tot, a2 / tot


def _outproj_kernel(a_ref, b_ref, x_ref, wa_ref, wb_ref, g_ref, be_ref, rwt_ref, rb_ref,
                    x1_ref, route_ref, *, alpha):
    d = x_ref.shape[1]
    tm = x_ref.shape[0]
    mix = _dot(a_ref[...], wa_ref[...]) + _dot(b_ref[...], wb_ref[...])
    x1 = _layer_norm(alpha * x_ref[...] + mix, g_ref[...], be_ref[...])
    x1_ref[:, 0:d] = x1
    x_hi = x1.astype(BF16)
    x_lo = (x1 - x_hi.astype(F32)).astype(BF16)
    w = rwt_ref[...]
    w_hi = w.astype(BF16)
    w_lo = (w - w_hi.astype(F32)).astype(BF16)
    z_hi = _dot_nt(jnp.concatenate([w_hi, w_lo], axis=0), x_hi)
    zt = z_hi[0:N_EXPERTS] + z_hi[N_EXPERTS:2 * N_EXPERTS] + _dot_nt(w_hi, x_lo)
    afft = 1.0 / (1.0 + jnp.exp(-zt))
    selt = afft + rb_ref[...]
    aff = [afft[e:e + 1, :] for e in range(N_EXPERTS)]
    sel = [selt[e:e + 1, :] for e in range(N_EXPERTS)]
    e1, e2, w1, w2 = _route_rows(aff, sel)
    rec = jnp.concatenate([e1, e2, w1, w2, jnp.zeros((LANE - 4, tm), F32)], axis=0)
    route_ref[...] = rec[0:8]
    x1_ref[:, d:d + LANE] = rec.T


def _outproj(a, b, xt, wa, wb, g, be, rwt, rb, alpha, tm):
    n, d = xt.shape
    full = lambda i: (0, 0)
    return pl.pallas_call(
        functools.partial(_outproj_kernel, alpha=alpha),
        out_shape=[jax.ShapeDtypeStruct((n, d + LANE), F32), jax.ShapeDtypeStruct((8, n), F32)],
        grid=(n // tm,),
        in_specs=[pl.BlockSpec((tm, a.shape[1]), lambda i: (i, 0)),
                  pl.BlockSpec((tm, b.shape[1]), lambda i: (i, 0)),
                  pl.BlockSpec((tm, d), lambda i: (i, 0)),
                  pl.BlockSpec(wa.shape, full), pl.BlockSpec(wb.shape, full),
                  pl.BlockSpec((1, d), full), pl.BlockSpec((1, d), full),
                  pl.BlockSpec(rwt.shape, full), pl.BlockSpec(rb.shape, full)],
        out_specs=[pl.BlockSpec((tm, d + LANE), lambda i: (i, 0)), pl.BlockSpec((8, tm), lambda i: (0, i))],
        compiler_params=_cparams(("parallel",)),
        name="outproj_ln_route",
    )(a, b, xt, wa, wb, g, be, rwt, rb)


def _moe_kernel(bea_ref, beb_ref, bbase_ref, nact_ref, src_ref,
                x_hbm, wga_ref, wgb_ref, wua_ref, wub_ref, wda_ref, wdb_ref, o_ref, xbuf0, xbuf1, sem, *, tm, d):
    i = pl.program_id(0)
    nact = nact_ref[0]
    last = src_ref.shape[0] - 1
    bufs = (xbuf0, xbuf1)

    def issue(blk, slot):
        base = bbase_ref[blk]
        for r in range(tm):
            tok = src_ref[jnp.minimum(base + r, last)]
            pltpu.make_async_copy(x_hbm.at[pl.ds(tok, 1), :], bufs[slot].at[pl.ds(r, 1), :],
                                  sem.at[slot]).start(priority=r % 2)

    def wait(slot):
        pltpu.make_async_copy(x_hbm.at[pl.ds(0, tm), :], bufs[slot], sem.at[slot]).wait()

    @pl.when((i == 0) & (nact > 0))
    def _():
        issue(0, 0)

    def step(slot):
        wait(slot)

        @pl.when(i + 1 < nact)
        def _():
            issue(i + 1, 1 - slot)

        buf = bufs[slot]
        xb = buf[:, 0:d].astype(BF16)
        rec = buf[:, d:d + LANE]
        e1, e2, w1, w2 = rec[:, 0:1], rec[:, 1:2], rec[:, 2:3], rec[:, 3:4]
        out = None
        for e_ref, wg_ref, wu_ref, wd_ref in ((bea_ref, wga_ref, wua_ref, wda_ref),
                                              (beb_ref, wgb_ref, wub_ref, wdb_ref)):
            expert = e_ref[i].astype(F32)
            gate_w = jnp.where(e1 == expert, w1, jnp.where(e2 == expert, w2, 0.0))
            gate = _dot(xb, wg_ref[...])
            up = _dot(xb, wu_ref[...])
            h = gate * (1.0 / (1.0 + jnp.exp(-gate))) * up * gate_w
            part = _dot(h.astype(BF16), wd_ref[...])
            out = part if out is None else out + part
        o_ref[...] = out

    for slot in range(2):
        pl.when((i < nact) & (i % 2 == slot))(functools.partial(step, slot))

    @pl.when(i >= nact)
    def _():
        o_ref[...] = jnp.zeros_like(o_ref)


def _moe_experts(x1e, wg, wu, wd, bea, beb, bbase, nact, src, tm):
    n, de = x1e.shape
    d = de - LANE
    nblk = bea.shape[0]
    f = wg.shape[2]
    amap = lambda i, bea, beb, *_: (bea[i], 0, 0)
    bmap = lambda i, bea, beb, *_: (beb[i], 0, 0)
    grid_spec = pltpu.PrefetchScalarGridSpec(
        num_scalar_prefetch=5,
        grid=(nblk,),
        in_specs=[pl.BlockSpec(memory_space=pl.ANY),
                  pl.BlockSpec((None, d, f), amap), pl.BlockSpec((None, d, f), bmap),
                  pl.BlockSpec((None, d, f), amap), pl.BlockSpec((None, d, f), bmap),
                  pl.BlockSpec((None, f, d), amap), pl.BlockSpec((None, f, d), bmap)],
        out_specs=pl.BlockSpec((tm, d), lambda i, *_: (i, 0)),
        scratch_shapes=[pltpu.VMEM((tm, de), F32), pltpu.VMEM((tm, de), F32), pltpu.SemaphoreType.DMA((2,))],
    )
    return pl.pallas_call(
        functools.partial(_moe_kernel, tm=tm, d=d),
        out_shape=jax.ShapeDtypeStruct((nblk * tm, d), F32),
        grid_spec=grid_spec,
        compiler_params=_cparams(("arbitrary",)),
        name="moe_experts",
    )(bea, beb, bbase, nact, src, x1e, wg, wg, wu, wu, wd, wd)


def _combine_kernel(p_ref, x_ref, g_ref, be_ref, o_hbm, y_ref, obuf0, obuf1, sem, *, alpha, tm, d):
    i = pl.program_id(0)
    nsteps = pl.num_programs(0)

    bufs = (obuf0, obuf1)

    def issue(blk, slot):
        base = blk * tm
        for r in range(tm):
            pltpu.make_async_copy(o_hbm.at[pl.ds(p_ref[base + r], 1), :],
                                  bufs[slot].at[pl.ds(r, 1), :], sem.at[slot]).start(priority=r % 2)

    def wait(slot):
        pltpu.make_async_copy(o_hbm.at[pl.ds(0, tm), :], bufs[slot], sem.at[slot]).wait()

    @pl.when(i == 0)
    def _():
        issue(0, 0)

    def step(slot):
        wait(slot)

        @pl.when(i + 1 < nsteps)
        def _():
            issue(i + 1, 1 - slot)

        y = alpha * x_ref[:, 0:d] + bufs[slot][...]
        y_ref[...] = _layer_norm(y, g_ref[...], be_ref[...])

    for slot in range(2):
        pl.when(i % 2 == slot)(functools.partial(step, slot))


def _combine(x1e, g, be, o, pos, alpha, tm):
    n, de = x1e.shape
    d = de - LANE
    grid_spec = pltpu.PrefetchScalarGridSpec(
        num_scalar_prefetch=1,
        grid=(n // tm,),
        in_specs=[pl.BlockSpec((tm, de), lambda i, *_: (i, 0)),
                  pl.BlockSpec((1, d), lambda i, *_: (0, 0)),
                  pl.BlockSpec((1, d), lambda i, *_: (0, 0)),
                  pl.BlockSpec(memory_space=pl.ANY)],
        out_specs=pl.BlockSpec((tm, d), lambda i, *_: (i, 0)),
        scratch_shapes=[pltpu.VMEM((tm, d), F32), pltpu.VMEM((tm, d), F32), pltpu.SemaphoreType.DMA((2,))],
    )
    return pl.pallas_call(
        functools.partial(_combine_kernel, alpha=alpha, tm=tm, d=d),
        out_shape=jax.ShapeDtypeStruct((n, d), F32),
        grid_spec=grid_spec,
        compiler_params=_cparams(("arbitrary",)),
        name="moe_combine_ln",
    )(pos, x1e, g, be, o)


def _moe_layer(x1e, route_t, wg, wu, wd, g, be, alpha, tm_e, tm_c):
    n = x1e.shape[0]
    pairs = [(a, b) for a in range(EXPERTS_PER_GROUP) for b in range(a + 1, EXPERTS_PER_GROUP)]
    nseg = N_GROUPS * len(pairs)
    seg_a = jnp.array([g * EXPERTS_PER_GROUP + a for g in range(N_GROUPS) for a, b in pairs], I32)
    seg_b = jnp.array([g * EXPERTS_PER_GROUP + b for g in range(N_GROUPS) for a, b in pairs], I32)
    e1, e2 = route_t[0].astype(I32), route_t[1].astype(I32)
    lo, hi = jnp.minimum(e1, e2), jnp.maximum(e1, e2)
    la, lb = lo % EXPERTS_PER_GROUP, hi % EXPERTS_PER_GROUP
    pair_id = la * (2 * EXPERTS_PER_GROUP - 1 - la) // 2 + (lb - la - 1)
    seg = (lo // EXPERTS_PER_GROUP) * len(pairs) + pair_id
    onehot = (seg[:, None] == jnp.arange(nseg, dtype=I32)[None, :]).astype(I32)
    rank = jnp.take_along_axis(jnp.cumsum(onehot, axis=0) - onehot, seg[:, None], axis=1)[:, 0]
    counts = jnp.sum(onehot, axis=0)
    nblk_s = (counts + tm_e - 1) // tm_e
    blk_start = jnp.cumsum(nblk_s) - nblk_s
    seg_start = jnp.cumsum(counts) - counts
    pos = (blk_start[seg] * tm_e + rank).astype(I32)
    src = jnp.argsort(seg, stable=True).astype(I32)
    nblk = n // tm_e + nseg
    blk = jnp.arange(nblk, dtype=I32)
    blk_end = jnp.cumsum(nblk_s)
    bseg = jnp.minimum(jnp.sum((blk[:, None] >= blk_end[None, :]).astype(I32), axis=1), nseg - 1)
    bbase = (seg_start[bseg] + (blk - blk_start[bseg]) * tm_e).astype(I32)
    nact = jnp.sum(nblk_s).astype(I32).reshape(1)
    o = _moe_experts(x1e, wg, wu, wd, seg_a[bseg], seg_b[bseg], bbase, nact, src, tm_e)
    return _combine(x1e, g, be, o, pos, alpha, tm_c)


def _rms(x, g):
    return x * lax.rsqrt(jnp.mean(x * x, axis=-1, keepdims=True) + RMS_EPS) * g

ODD_OFF = dict(qc=0, kc=512, vc=640, qi=768, ki=1024, wi=1152, cq=1280, ckv=1536, kra=1664, krb=1792, end=1920)


def _odd_proj_kernel(x_ref, w_ref, qng_ref, kvng_ref, wqa_ref, wqb_ref, wk_ref, wv_ref, cos_ref, sin_ref,
                     tq_ref, tk_ref,
                     qc_ref, kc_ref, vc_ref, qi_ref, ki_ref, wi_ref, q8_ref, k8_ref, v8_ref, *, q_scale):
    o = ODD_OFF
    tm = x_ref.shape[0]
    lo = _lane_lo((tm, LANE))
    h = _dot(x_ref[...].astype(BF16), w_ref[...])
    kdup = h[:, o["kc"]:o["vc"]]
    vdup = h[:, o["vc"]:o["qi"]]
    kc_ref[:, 0:LANE] = jnp.where(lo, kdup, tk_ref[:, 0:LANE].astype(F32)).astype(BF16)
    kc_ref[:, LANE:2 * LANE] = jnp.where(lo, tk_ref[:, LANE:2 * LANE].astype(F32), kdup).astype(BF16)
    vc_ref[:, 0:LANE] = jnp.where(lo, vdup, 1.0).astype(BF16)
    vc_ref[:, LANE:2 * LANE] = jnp.where(lo, 1.0, vdup).astype(BF16)
    qi_ref[...] = h[:, o["qi"]:o["ki"]].astype(BF16)
    ki_ref[...] = h[:, o["ki"]:o["wi"]].astype(BF16)
    wi_ref[...] = h[:, o["wi"]:o["wi"] + IDX_HEADS]
    cos = cos_ref[...]
    sin = sin_ref[...]
    cqn = _rms(h[:, o["cq"]:o["ckv"]], qng_ref[...]).astype(BF16)
    qa = _dot(cqn, wqa_ref[...])
    qb = _dot(cqn, wqb_ref[...])
    ckvn = _rms(h[:, o["ckv"]:o["kra"]], kvng_ref[...]).astype(BF16)
    kn = _dot(ckvn, wk_ref[...])
    v8 = _dot(ckvn, wv_ref[...])
    kr = h[:, o["kra"]:o["krb"]] * cos + h[:, o["krb"]:o["end"]] * sin
    for hd in range(D_HEADS):
        sl = slice(hd * LANE, (hd + 1) * LANE)
        pj = slice((hd // 2) * LANE, (hd // 2 + 1) * LANE)
        own = lo if hd % 2 == 0 else ~lo
        qc_ref[:, sl] = jnp.where(own, h[:, pj], tq_ref[:, sl].astype(F32)).astype(BF16)
        q8_ref[:, sl] = ((qa[:, sl] * cos + qb[:, sl] * sin) * q_scale).astype(BF16)
        k8_ref[:, sl] = (kn[:, sl] + kr).astype(BF16)
        v8_ref[:, sl] = jnp.where(own, v8[:, pj], 1.0).astype(BF16)


def _odd_proj(xt, w_cat, qng, kvng, wqa, wqb, wk, wv, cos_t, sin_t, tq_t, tk_t, seq, tm):
    n, d = xt.shape
    full = lambda i: (0, 0)
    tps = seq // tm
    pos = lambda i: (i % tps, 0)
    widths = (1024, 256, 256, 256, 128)
    out_shape = [jax.ShapeDtypeStruct((n, w), BF16) for w in widths]
    out_specs = [pl.BlockSpec((tm, w), lambda i: (i, 0)) for w in widths]
    out_shape.append(jax.ShapeDtypeStruct((n, IDX_HEADS), F32))
    out_specs.append(pl.BlockSpec((tm, IDX_HEADS), lambda i: (i, 0)))
    for w in (1024, 1024, 1024):
        out_shape.append(jax.ShapeDtypeStruct((n, w), BF16))
        out_specs.append(pl.BlockSpec((tm, w), lambda i: (i, 0)))
    return pl.pallas_call(
        functools.partial(_odd_proj_kernel, q_scale=(MLA_NOPE + MLA_ROPE) ** -0.5 * LOG2E),
        out_shape=out_shape,
        grid=(n // tm,),
        in_specs=[pl.BlockSpec((tm, d), lambda i: (i, 0)),
                  pl.BlockSpec(w_cat.shape, full),
                  pl.BlockSpec(qng.shape, full), pl.BlockSpec(kvng.shape, full),
                  pl.BlockSpec(wqa.shape, full), pl.BlockSpec(wqb.shape, full),
                  pl.BlockSpec(wk.shape, full), pl.BlockSpec(wv.shape, full),
                  pl.BlockSpec((tm, LANE), pos), pl.BlockSpec((tm, LANE), pos),
                  pl.BlockSpec((tm, C_HEADS * LANE), pos), pl.BlockSpec((tm, 2 * LANE), pos)],
        out_specs=out_specs,
        compiler_params=_cparams(("parallel",)),
        name="odd_proj",
    )(xt, w_cat, qng, kvng, wqa, wqb, wk, wv, cos_t, sin_t, tq_t, tk_t)


def _dsa_kernel(q_ref, k_ref, v_ref, qi_ref, ki_ref, wt_ref, o_ref, sc_ref, mb_ref, *, top_k, chunk, qb):
    i = pl.program_id(1)
    nc = (i * qb + qb + chunk - 1) // chunk
    lo = _lane_lo((qb, LANE))
    s_row2 = lax.broadcasted_iota(I32, (2 * chunk, qb), 0)
    t_lane2 = i * qb + lax.broadcasted_iota(I32, (2 * chunk, qb), 1)

    def chunk_start(c):
        return pl.multiple_of(c * chunk, chunk)

    qidx = []
    for h in range(IDX_HEADS):
        qp = qi_ref[:, (h // 2) * LANE:(h // 2 + 1) * LANE]
        qidx.append(jnp.where(lo if h % 2 == 0 else ~lo, qp, jnp.zeros_like(qp)))
    q_stack = jnp.concatenate(qidx, axis=0)
    w = wt_ref[...]

    npair = (nc + 1) // 2

    def pair_start(cp):
        return cp * (2 * chunk)

    def select(pairs):
        def score_pair(cp):
            start = pair_start(cp)
            d = _dot_nt(ki_ref[pl.ds(start, 2 * chunk), :], q_stack)
            sc = w[0:1, :] * jnp.maximum(d[:, 0:qb], 0.0)
            for h in range(1, IDX_HEADS):
                sc = sc + w[h:h + 1, :] * jnp.maximum(d[:, h * qb:(h + 1) * qb], 0.0)
            sc = jnp.where(sc == 0.0, 0.0, sc)
            sc_ref[pl.ds(start, 2 * chunk), :] = jnp.where(start + s_row2 <= t_lane2, sc, NEG_INF)

        for cp in range(pairs):
            score_pair(cp)

        nacc = 4 * 8

        def count(trial, strict):
            acc = jnp.zeros((nacc, qb), F32)
            for cp in range(pairs):
                sc = sc_ref[pl.ds(pair_start(cp), 2 * chunk), :]
                hit = (sc > trial) if strict else (sc >= trial)
                acc = acc + jnp.sum(jnp.where(hit, 1.0, 0.0).reshape(2 * chunk // nacc, nacc, qb), axis=0)
            return jnp.sum(acc, axis=0, keepdims=True)

        def decode(key):
            return lax.bitcast_convert_type(jnp.where(key < 0, key ^ jnp.int32(0x7FFFFFFF), key), F32)

        kf = float(top_k)
        key_neg_inf = jnp.int32(-2139095041)
        cand = jnp.where(count(jnp.zeros((1, qb), F32), False) >= kf, 0, jnp.int32(-2 ** 31)).astype(I32)

        def bit_step(j, cand):
            trial = cand + lax.shift_left(jnp.int32(1), 30 - j)
            ok = (trial <= key_neg_inf) | (count(decode(trial), False) >= kf)
            return jnp.where(ok, trial, cand)

        thr = decode(lax.fori_loop(0, 31, bit_step, cand))
        need = kf - count(thr, True)

        r0 = lax.broadcasted_iota(I32, (chunk, chunk), 0)
        r1 = lax.broadcasted_iota(I32, (chunk, chunk), 1)
        below = jnp.where(r1 < r0, 1.0, 0.0).astype(BF16)
        need = jnp.where(thr == NEG_INF, 0.0, need)

        offset = jnp.zeros((1, qb), F32)
        for cp in range(pairs):
            start = pair_start(cp)
            sc = sc_ref[pl.ds(start, 2 * chunk), :]
            eq = sc == thr
            eqf = jnp.where(eq, 1.0, 0.0)
            eqb = eqf.astype(BF16)
            first = jnp.sum(eqf[0:chunk], axis=0, keepdims=True)
            prefix = jnp.concatenate([_dot(below, eqb[0:chunk]) + offset,
                                      _dot(below, eqb[chunk:2 * chunk]) + (offset + first)], axis=0)
            bias = jnp.where(sc > thr, 0.0, jnp.where(eq, jnp.where(prefix < need, 0.0, NEG_INF), NEG_INF))
            mb_ref[:, pl.ds(start, 2 * chunk)] = bias.T
            offset = offset + first + jnp.sum(eqf[chunk:2 * chunk], axis=0, keepdims=True)

    for pairs in range(1, k_ref.shape[0] // (2 * chunk) + 1):
        pl.when(npair == pairs)(functools.partial(select, pairs))

    nslab = qb // BLOCK
    chains = [(h, r) for h in range(C_HEADS) for r in range(nslab)]

    def att_chunk(c, carry):
        start = chunk_start(c)
        kp = [k_ref[pl.ds(start, chunk), par * LANE:(par + 1) * LANE] for par in range(2)]
        vp = [v_ref[pl.ds(start, chunk), par * LANE:(par + 1) * LANE] for par in range(2)]
        mb = [mb_ref[r * BLOCK:(r + 1) * BLOCK, pl.ds(start, chunk)].astype(BF16) for r in range(nslab)]
        qs = [q_ref[r * BLOCK:(r + 1) * BLOCK, h * LANE:(h + 1) * LANE] for h, r in chains]
        return _online_softmax_chains(qs, [kp[h % 2] for h, r in chains], [vp[h % 2] for h, r in chains], carry,
                                      biases=[mb[r] for h, r in chains], guard_empty=True)

    init = tuple((jnp.full((BLOCK, 1), NEG_INF, F32), jnp.zeros((BLOCK, LANE), F32)) for _ in chains)
    carry = lax.fori_loop(0, nc, att_chunk, init)
    accs = [jnp.concatenate([carry[h * nslab + r][1] for r in range(nslab)], axis=0) for h in range(C_HEADS)]
    _normalize_pairs(accs, o_ref)


def _dsa(qc, kc, vc, qi, ki, wit):
    b, s, _ = qc.shape
    top_k = min(DSA_TOPK_MAX, s // 4)
    chunk = 256
    qb = min(2 * BLOCK, s)
    assert s % (2 * chunk) == 0 and top_k <= chunk and chunk % qb == 0
    blk = lambda bi, i: (bi, i, 0)
    seq = lambda bi, i: (bi, 0, 0)
    return pl.pallas_call(
        functools.partial(_dsa_kernel, top_k=top_k, chunk=chunk, qb=qb),
        out_shape=jax.ShapeDtypeStruct((b, s, C_HEADS * HEAD_DIM), BF16),
        grid=(b, s // qb),
        in_specs=[pl.BlockSpec((None, qb, C_HEADS * LANE), blk),
                  pl.BlockSpec((None, s, 2 * LANE), seq),
                  pl.BlockSpec((None, s, 2 * LANE), seq),
                  pl.BlockSpec((None, qb, 256), blk),
                  pl.BlockSpec((None, s, LANE), seq),
                  pl.BlockSpec((None, IDX_HEADS, qb), lambda bi, i: (bi, 0, i))],
        out_specs=pl.BlockSpec((None, qb, 512), blk),
        scratch_shapes=[pltpu.VMEM((s, qb), F32), pltpu.VMEM((qb, s), F32)],
        compiler_params=_cparams(("parallel", "arbitrary")),
        name="dsa_attn",
    )(qc, kc, vc, qi, ki, wit)


def _dup64(w):
    d, c = w.shape
    w = w.reshape(d, c // HEAD_DIM, 1, HEAD_DIM)
    return jnp.broadcast_to(w, (d, c // HEAD_DIM, 2, HEAD_DIM)).reshape(d, 2 * c)


def _pad_cols(w, width):
    return jnp.pad(w, ((0, 0), (0, width - w.shape[1])))


def _split_cols(w, sizes):
    out, start = [], 0
    for n in sizes:
        out.append(w[:, start:start + n])
        start += n
    return out


def _rope_partner(w):
    half = MLA_ROPE // 2
    return jnp.concatenate([-w[..., half:], w[..., :half]], axis=-1)


def _rope_block(w_rope):
    d = w_rope.shape[0]
    return jnp.concatenate([jnp.zeros((d, MLA_NOPE), F32), w_rope,
                            jnp.zeros((d, LANE - MLA_NOPE - MLA_ROPE), F32)], axis=1)


def _alibi_tables(s):
    slopes = 2.0 ** (-8.0 * jnp.arange(1, C_HEADS + 1, dtype=F32) / C_HEADS)
    a = slopes * LOG2E
    pos = jnp.arange(s, dtype=F32)
    a_sp = _split3(a)
    r_sp = _split3(-(a[None, :] * pos[:, None]))
    ones = jnp.ones((s, C_HEADS), F32)
    q_terms = [128.0 * t * ones for t in a_sp] + [t * ones for t in a_sp] + list(r_sp)
    q_aug = jnp.stack(q_terms, axis=-1)
    q_aug = jnp.pad(q_aug, ((0, 0), (0, 0), (0, HEAD_DIM - q_aug.shape[-1])))
    zeros = jnp.zeros_like(q_aug)
    even = jnp.concatenate([zeros, q_aug], axis=-1)
    odd = jnp.concatenate([q_aug, zeros], axis=-1)
    parity = (jnp.arange(C_HEADS) % 2 == 0)[None, :, None]
    tq_t = jnp.where(parity, even, odd).reshape(s, C_HEADS * LANE)
    s_hi = jnp.floor(pos / LANE)
    s_lo = pos - LANE * s_hi
    k_aug = jnp.stack([s_hi] * 3 + [s_lo] * 3 + [jnp.ones_like(pos)] * 3, axis=-1)
    k_aug = jnp.pad(k_aug, ((0, 0), (0, HEAD_DIM - k_aug.shape[-1])))
    kz = jnp.zeros_like(k_aug)
    tk_t = jnp.concatenate([kz, k_aug, k_aug, kz], axis=-1)
    return tq_t.astype(BF16), tk_t.astype(BF16)


def kernel(x, even_w_in, even_b_f, even_sinks, odd_w_in, odd_q_norm, odd_kv_norm, odd_w_uq, odd_w_ukv,
           w_o, ln_g, ln_b, router_w, router_b, moe_w_gate, moe_w_up, moe_w_down):
    b, s, d = x.shape
    n = b * s
    depth = w_o.shape[0]
    alpha = (2 * depth) ** 0.25
    tm = min(512, s)
    tq = min(256, s)
    tm_e = min(256, n)
    tm_c = min(256, n)
    att_scale = HEAD_DIM ** -0.5
    xt = x.reshape(n, d)
    rwt = router_w.T
    rb = router_b.reshape(N_EXPERTS, 1)

    for l in range(depth):
        j = l // 2
        if l % 2 == 0:
            qa, ka, va, qb, kb, vb, fg = _split_cols(
                even_w_in[j], (512, 128, 128, 512, 512, 512, B_HEADS))
            w_cat = jnp.concatenate([qa * (att_scale * LOG2E), _dup64(ka), _dup64(va), qb * (att_scale * LOG2E), kb, vb,
                                     _pad_cols(fg, LANE)], axis=1).astype(BF16)
            bf_pad = _pad_cols(even_b_f[j].reshape(1, B_HEADS), LANE)
            qa_, ka_, va_, qf, kf, vf = _even_proj(xt, w_cat, bf_pad, s, tm)
            r3 = lambda a: a.reshape(b, s, a.shape[1])
            out_a = _swa(even_sinks[j], r3(qa_), r3(ka_), r3(va_))
            out_b = _flash(r3(qf), r3(kf), r3(vf), tq=tq, nh=8, name="fox_attn")
        else:
            (qc, kc, vc, qi, ki, wi, cq, ckv, kr) = _split_cols(
                odd_w_in[j], (512, 64, 64, IDX_HEADS * IDX_DIM, IDX_DIM, IDX_HEADS,
                              MLA_Q_RANK, MLA_KV_RANK, MLA_ROPE))
            w_cat = jnp.concatenate([
                qc * (att_scale * LOG2E), _dup64(kc), _dup64(vc), qi * (IDX_DIM ** -0.5), _dup64(ki),
                _pad_cols(wi * (IDX_HEADS ** -0.5), LANE), cq, ckv,
                _rope_block(kr), _rope_block(_rope_partner(kr))], axis=1).astype(BF16)
            dq = MLA_NOPE + MLA_ROPE
            wuq = odd_w_uq[j].reshape(MLA_Q_RANK, D_HEADS, dq)
            pad_q = jnp.zeros((MLA_Q_RANK, D_HEADS, LANE - dq), F32)
            wqa = jnp.concatenate([wuq, pad_q], axis=2).reshape(MLA_Q_RANK, D_HEADS * LANE).astype(BF16)
            wqb = jnp.concatenate([jnp.zeros((MLA_Q_RANK, D_HEADS, MLA_NOPE), F32),
                                   _rope_partner(wuq[..., MLA_NOPE:]), pad_q],
                                  axis=2).reshape(MLA_Q_RANK, D_HEADS * LANE).astype(BF16)
            wukv = odd_w_ukv[j].reshape(MLA_KV_RANK, D_HEADS, MLA_NOPE + MLA_V)
            wk = jnp.concatenate([wukv[..., :MLA_NOPE], jnp.zeros((MLA_KV_RANK, D_HEADS, LANE - MLA_NOPE), F32)],
                                 axis=2).reshape(MLA_KV_RANK, D_HEADS * LANE).astype(BF16)
            wv = wukv[..., MLA_NOPE:].reshape(MLA_KV_RANK, D_HEADS * MLA_V).astype(BF16)
            half = MLA_ROPE // 2
            inv = ROPE_THETA ** (-jnp.arange(half, dtype=F32) / half)
            ang = jnp.arange(s, dtype=F32)[:, None] * inv[None, :]
            cos_h, sin_h = jnp.cos(ang), jnp.sin(ang)
            pad_t = jnp.zeros((s, LANE - dq), F32)
            cos_t = jnp.concatenate([jnp.ones((s, MLA_NOPE), F32), cos_h, cos_h, pad_t], axis=1)
            sin_t = jnp.concatenate([jnp.zeros((s, MLA_NOPE), F32), sin_h, sin_h, pad_t], axis=1)
            tq_t, tk_t = _alibi_tables(s)
            qc_, kc_, vc_, qi_, ki_, wi_, q8, k8, v8 = _odd_proj(
                xt, w_cat, odd_q_norm[j].reshape(1, -1), odd_kv_norm[j].reshape(1, -1),
                wqa, wqb, wk, wv, cos_t, sin_t, tq_t, tk_t, s, tm)
            r3 = lambda a: a.reshape(b, s, a.shape[1])
            out_a = _dsa(r3(qc_), r3(kc_), r3(vc_), r3(qi_), r3(ki_), r3(wi_).transpose(0, 2, 1))
            out_b = _flash(r3(q8), r3(k8), r3(v8), tq=tq, nh=8, name="mla_attn")
        half_w = out_a.shape[2]
        wo = w_o[l].astype(BF16)
        x1e, route_t = _outproj(out_a.reshape(n, half_w), out_b.reshape(n, -1), xt,
                                wo[:half_w], wo[half_w:], ln_g[l, 0].reshape(1, d), ln_b[l, 0].reshape(1, d),
                                rwt, rb, alpha, tm)
        xt = _moe_layer(x1e, route_t, moe_w_gate[l].astype(BF16), moe_w_up[l].astype(BF16),
                        moe_w_down[l].astype(BF16), ln_g[l, 1].reshape(1, d), ln_b[l, 1].reshape(1, d),
                        alpha, tm_e, tm_c)
    return xt.reshape(b, s, d)
```

```python
import functools
import math

import jax
import jax.numpy as jnp
from jax import lax
from jax.experimental import pallas as pl
from jax.experimental.pallas import tpu as pltpu

F32 = jnp.float32
BF16 = jnp.bfloat16
I32 = jnp.int32

LANE = 128
HEAD_DIM = 64
BLOCK = 128
A_HEADS, A_KV_HEADS, WINDOW = 8, 2, 128
B_HEADS = 8
C_HEADS, IDX_HEADS, IDX_DIM, DSA_TOPK_MAX = 8, 4, 64, 256
D_HEADS, MLA_Q_RANK, MLA_KV_RANK, MLA_NOPE, MLA_ROPE, MLA_V = 8, 256, 128, 64, 32, 64
ROPE_THETA = 10000.0
N_EXPERTS, N_GROUPS = 16, 4
EXPERTS_PER_GROUP = N_EXPERTS // N_GROUPS
LN_EPS, RMS_EPS = 1e-5, 1e-6
NEG_INF = float("-inf")
LOG2E = math.log2(math.e)
VMEM_LIMIT = 56 * 1024 * 1024


def _cparams(sem):
    return pltpu.CompilerParams(dimension_semantics=sem, vmem_limit_bytes=VMEM_LIMIT)


def _dot(a, b):
    return jnp.dot(a, b, preferred_element_type=F32)


def _dot_nt(a, b):
    return lax.dot_general(a, b, (((1,), (1,)), ((), ())), preferred_element_type=F32)


def _lane_lo(shape):
    return lax.broadcasted_iota(I32, shape, len(shape) - 1) % LANE < HEAD_DIM


def _layer_norm(y, g, b):
    mu = jnp.mean(y, axis=-1, keepdims=True)
    yc = y - mu
    var = jnp.mean(yc * yc, axis=-1, keepdims=True)
    return yc * lax.rsqrt(var + LN_EPS) * g + b


def _top16(v):
    bits = lax.bitcast_convert_type(v, I32) & jnp.int32(-65536)
    return lax.bitcast_convert_type(bits, F32)


def _split3(x):
    s1 = _top16(x)
    r = x - s1
    s2 = _top16(r)
    return s1, s2, r - s2


def _aug_rel(shape, h):
    lane = lax.broadcasted_iota(I32, shape, 1)
    return lane - (HEAD_DIM if h % 2 == 0 else 0)


def _even_proj_kernel(x_ref, w_ref, bf_ref, qa_ref, ka_ref, va_ref, qf_ref, kf_ref, vf_ref,
                      carry_ref, *, tiles_per_seq):
    t = pl.program_id(0)
    tm = x_ref.shape[0]
    h = _dot(x_ref[...].astype(BF16), w_ref[...])
    qa_ref[...] = h[:, 0:512].astype(BF16)
    ka_ref[...] = h[:, 512:768].astype(BF16)
    va_ref[...] = h[:, 768:1024].astype(BF16)
    z = h[:, 2560:2688] + bf_ref[...]
    logf = jnp.minimum(z, 0.0) - jnp.log1p(jnp.exp(-jnp.abs(z)))

    @pl.when(t % tiles_per_seq == 0)
    def _():
        carry_ref[...] = jnp.zeros_like(carry_ref)

    row = lax.broadcasted_iota(I32, (BLOCK, BLOCK), 0)
    col = lax.broadcasted_iota(I32, (BLOCK, BLOCK), 1)
    tri = jnp.where(row >= col, 1.0, 0.0).astype(BF16)
    parts = [p.astype(BF16) for p in _split3(logf)]
    run = carry_ref[...]
    blocks = []
    for blk in range(tm // BLOCK):
        cb = run
        for part in parts:
            cb = cb + _dot(tri, part[blk * BLOCK:(blk + 1) * BLOCK])
        blocks.append(cb)
        run = cb[BLOCK - 1:BLOCK, :]
    c = jnp.concatenate(blocks, axis=0)
    carry_ref[...] = run

    s1, s2, s3 = _split3(c * LOG2E)
    lo = _lane_lo((tm, LANE))
    for hd in range(B_HEADS):
        own = lo if hd % 2 == 0 else ~lo
        rel = _aug_rel((tm, LANE), hd)
        b1, b2, b3 = s1[:, hd:hd + 1], s2[:, hd:hd + 1], s3[:, hd:hd + 1]
        q_aug = jnp.where(rel == 3, b1, jnp.where(rel == 4, b2, jnp.where(rel == 5, b3,
                          jnp.where((rel >= 0) & (rel < 3), -1.0, 0.0))))
        k_aug = jnp.where(rel == 0, b1, jnp.where(rel == 1, b2, jnp.where(rel == 2, b3,
                          jnp.where((rel >= 3) & (rel < 6), 1.0, 0.0))))
        pj = (hd // 2) * LANE
        sl = slice(hd * LANE, (hd + 1) * LANE)
        qf_ref[:, sl] = jnp.where(own, h[:, 1024 + pj:1024 + pj + LANE], q_aug).astype(BF16)
        kf_ref[:, sl] = jnp.where(own, h[:, 1536 + pj:1536 + pj + LANE], k_aug).astype(BF16)
        vf_ref[:, sl] = jnp.where(own, h[:, 2048 + pj:2048 + pj + LANE], 1.0).astype(BF16)


def _even_proj(xt, w_cat, bf_pad, seq, tm):
    n, d = xt.shape
    wcols = w_cat.shape[1]
    widths = (512, 256, 256, 1024, 1024, 1024)
    out_shape = [jax.ShapeDtypeStruct((n, w), BF16) for w in widths]
    out_specs = [pl.BlockSpec((tm, w), lambda i: (i, 0)) for w in widths]
    return pl.pallas_call(
        functools.partial(_even_proj_kernel, tiles_per_seq=seq // tm),
        out_shape=out_shape,
        grid=(n // tm,),
        in_specs=[pl.BlockSpec((tm, d), lambda i: (i, 0)),
                  pl.BlockSpec((d, wcols), lambda i: (0, 0)),
                  pl.BlockSpec((1, LANE), lambda i: (0, 0))],
        out_specs=out_specs,
        scratch_shapes=[pltpu.VMEM((1, LANE), F32)],
        compiler_params=_cparams(("arbitrary",)),
        name="even_proj",
    )(xt, w_cat, bf_pad)


def _swa_kernel(sink_ref, q_ref, kp_ref, kc_ref, vp_ref, vc_ref, o_ref, *, nbq):
    i = pl.program_id(1)
    group = A_HEADS // A_KV_HEADS
    tl = lax.broadcasted_iota(I32, (BLOCK, 2 * BLOCK), 0)
    sl = lax.broadcasted_iota(I32, (BLOCK, 2 * BLOCK), 1)
    dist = BLOCK + tl - sl
    in_window = (dist >= 0) & (dist < WINDOW)
    distf = dist.astype(F32)
    lo = _lane_lo((BLOCK, LANE))

    def band(prev_ref, cur_ref, j, g):
        lanes = slice(g * LANE, (g + 1) * LANE)
        before = prev_ref[:, lanes] if j == 0 else cur_ref[(j - 1) * BLOCK:j * BLOCK, lanes]
        return jnp.concatenate([before, cur_ref[j * BLOCK:(j + 1) * BLOCK, lanes]], axis=0)

    stacked = {}
    for j in range(nbq):
        for g in range(A_KV_HEADS):
            qs = []
            for h in range(g * group, (g + 1) * group):
                qp = q_ref[j * BLOCK:(j + 1) * BLOCK, (h // 2) * LANE:(h // 2 + 1) * LANE]
                qs.append(jnp.where(lo if h % 2 == 0 else ~lo, qp, jnp.zeros_like(qp)))
            stacked[j, g] = _dot_nt(jnp.concatenate(qs, axis=0), band(kp_ref, kc_ref, j, g))
    ps, denoms = {}, {}
    for j in range(nbq):
        valid = (in_window & ((sl >= BLOCK) | (i > 0))) if j == 0 else in_window
        for h in range(A_HEADS):
            r = h % group
            slope = 2.0 ** (-8.0 * (h + 1) / A_HEADS) * LOG2E
            logits = jnp.where(valid, stacked[j, h // group][r * BLOCK:(r + 1) * BLOCK] - slope * distf, NEG_INF)
            sink = sink_ref[h] * LOG2E
            m = jnp.maximum(jnp.max(logits, axis=-1, keepdims=True), sink)
            e = jnp.exp2(logits - m)
            denoms[j, h] = jnp.sum(e, axis=-1, keepdims=True) + jnp.exp2(sink - m)
            ps[j, h] = e.astype(BF16)
    for j in range(nbq):
        outs = []
        for g in range(A_KV_HEADS):
            heads = range(g * group, (g + 1) * group)
            pv = _dot(jnp.concatenate([ps[j, h] for h in heads], axis=0), band(vp_ref, vc_ref, j, g))
            for r, h in enumerate(heads):
                outs.append(pv[r * BLOCK:(r + 1) * BLOCK] / denoms[j, h])
        for pj in range(A_HEADS // 2):
            o_ref[j * BLOCK:(j + 1) * BLOCK, pj * LANE:(pj + 1) * LANE] = jnp.where(
                lo, outs[2 * pj], outs[2 * pj + 1]).astype(BF16)


def _swa(sinks, qa, ka, va):
    b, s, _ = qa.shape
    nbq = min(4, s // BLOCK)
    prev = lambda bi, i: (bi, jnp.maximum(i * nbq - 1, 0), 0)
    cur = lambda bi, i: (bi, i, 0)
    return pl.pallas_call(
        functools.partial(_swa_kernel, nbq=nbq),
        out_shape=jax.ShapeDtypeStruct((b, s, A_HEADS * HEAD_DIM), BF16),
        grid=(b, s // (nbq * BLOCK)),
        in_specs=[pl.BlockSpec(memory_space=pltpu.SMEM),
                  pl.BlockSpec((None, nbq * BLOCK, 512), cur),
                  pl.BlockSpec((None, BLOCK, 256), prev),
                  pl.BlockSpec((None, nbq * BLOCK, 256), cur),
                  pl.BlockSpec((None, BLOCK, 256), prev),
                  pl.BlockSpec((None, nbq * BLOCK, 256), cur)],
        out_specs=pl.BlockSpec((None, nbq * BLOCK, 512), cur),
        compiler_params=_cparams(("parallel", "arbitrary")),
        name="swa_attn",
    )(sinks, qa, ka, ka, va, va)


def _normalize_pairs(accs, o_ref):
    lo = _lane_lo(accs[0].shape)
    outs = [a / pltpu.roll(a, HEAD_DIM, 1) for a in accs]
    for j in range(len(accs) // 2):
        o_ref[:, j * LANE:(j + 1) * LANE] = jnp.where(lo, outs[2 * j], outs[2 * j + 1]).astype(BF16)


def _online_softmax_chains(qs, ks, vs, carry, biases=None, masks=None, guard_empty=False):
    n = len(qs)
    ss = [_dot_nt(qs[i], ks[i]).astype(BF16) for i in range(n)]
    if biases is not None:
        ss = [ss[i] + biases[i] for i in range(n)]
    if masks is not None:
        ss = [jnp.where(masks[i], ss[i], NEG_INF) for i in range(n)]
    ms, ps, alphas = [], [], []
    for i in range(n):
        m = carry[i][0]
        m_new = jnp.maximum(m, jnp.max(ss[i], axis=-1, keepdims=True).astype(F32))
        m_ref = jnp.where(m_new == NEG_INF, 0.0, m_new) if guard_empty else m_new
        ps.append(jnp.exp2(ss[i] - m_ref.astype(BF16)))
        alphas.append(jnp.exp2(m - m_ref))
        ms.append(m_new)
    pvs = [_dot(ps[i], vs[i]) for i in range(n)]
    return tuple((ms[i], alphas[i] * carry[i][1] + pvs[i]) for i in range(n))


def _flash_kernel(q_ref, k_ref, v_ref, o_ref, *, tq, nh):
    qi = pl.program_id(2)
    nslab = tq // BLOCK
    chains = [(hh, r) for hh in range(nh) for r in range(nslab)]
    row = lax.broadcasted_iota(I32, (BLOCK, tq), 0)
    col = lax.broadcasted_iota(I32, (BLOCK, tq), 1)

    def lanes(hh):
        return slice(hh * LANE, (hh + 1) * LANE)

    def step(start, width, carry, diagonal):
        qs = [q_ref[r * BLOCK:(r + 1) * BLOCK, lanes(hh)] for hh, r in chains]
        if diagonal:
            ks = [k_ref[pl.ds(start, (r + 1) * BLOCK), lanes(hh)] for hh, r in chains]
            vs = [v_ref[pl.ds(start, (r + 1) * BLOCK), lanes(hh)] for hh, r in chains]
            masks = [(col <= row + r * BLOCK)[:, 0:(r + 1) * BLOCK] for hh, r in chains]
        else:
            ks = [k_ref[pl.ds(start, width), lanes(hh)] for hh, r in chains]
            vs = [v_ref[pl.ds(start, width), lanes(hh)] for hh, r in chains]
            masks = None
        return _online_softmax_chains(qs, ks, vs, carry, masks=masks)

    init = tuple((jnp.full((BLOCK, 1), NEG_INF, F32), jnp.zeros((BLOCK, LANE), F32))
                 for _ in range(nh * nslab))
    npairs = qi // 2
    carry = lax.fori_loop(
        0, npairs, lambda kp, c: step(pl.multiple_of(kp * 2 * tq, 2 * tq), 2 * tq, c, False), init)
    carry = lax.fori_loop(
        0, qi % 2, lambda _, c: step(pl.multiple_of(npairs * 2 * tq, tq), tq, c, False), carry)
    carry = step(pl.multiple_of(qi * tq, tq), tq, carry, True)
    accs = [jnp.concatenate([carry[hh * nslab + r][1] for r in range(nslab)], axis=0) for hh in range(nh)]
    _normalize_pairs(accs, o_ref)


def _flash(q, k, v, *, tq, nh, name):
    b, s, w = v.shape
    return pl.pallas_call(
        functools.partial(_flash_kernel, tq=tq, nh=nh),
        out_shape=jax.ShapeDtypeStruct((b, s, w // 2), BF16),
        grid=(b, w // (nh * LANE), s // tq),
        in_specs=[pl.BlockSpec((None, tq, nh * LANE), lambda bi, j, i: (bi, i, j)),
                  pl.BlockSpec((None, s, nh * LANE), lambda bi, j, i: (bi, 0, j)),
                  pl.BlockSpec((None, s, nh * LANE), lambda bi, j, i: (bi, 0, j))],
        out_specs=pl.BlockSpec((None, tq, nh * LANE // 2), lambda bi, j, i: (bi, i, j)),
        compiler_params=_cparams(("parallel", "parallel", "arbitrary")),
        name=name,
    )(q, k, v)


def _route_rows(aff, sel):
    gs = []
    for g in range(N_GROUPS):
        r = sel[g * EXPERTS_PER_GROUP:(g + 1) * EXPERTS_PER_GROUP]
        best = None
        for a in range(EXPERTS_PER_GROUP):
            for c in range(a + 1, EXPERTS_PER_GROUP):
                pair = r[a] + r[c]
                best = pair if best is None else jnp.maximum(best, pair)
        gs.append(best)
    gmax = functools.reduce(jnp.maximum, gs)
    gidx = jnp.full_like(gmax, float(N_GROUPS - 1))
    for g in range(N_GROUPS - 2, -1, -1):
        gidx = jnp.where(gs[g] == gmax, float(g), gidx)

    def pick(rows, k):
        out = rows[(N_GROUPS - 1) * EXPERTS_PER_GROUP + k]
        for g in range(N_GROUPS - 2, -1, -1):
            out = jnp.where(gidx == float(g), rows[g * EXPERTS_PER_GROUP + k], out)
        return out

    s = [pick(sel, k) for k in range(EXPERTS_PER_GROUP)]
    a = [pick(aff, k) for k in range(EXPERTS_PER_GROUP)]

    def first_max(vals):
        mx = functools.reduce(jnp.maximum, vals)
        idx = jnp.full_like(mx, float(len(vals) - 1))
        for k in range(len(vals) - 2, -1, -1):
            idx = jnp.where(vals[k] == mx, float(k), idx)
        return idx

    def take(vals, idx):
        out = vals[-1]
        for k in range(len(vals) - 2, -1, -1):
            out = jnp.where(idx == float(k), vals[k], out)
        return out

    i1 = first_max(s)
    s2 = [jnp.where(i1 == float(k), NEG_INF, s[k]) for k in range(EXPERTS_PER_GROUP)]
    i2 = first_max(s2)
    a1, a2 = take(a, i1), take(a, i2)
    tot = a1 + a2
    base = gidx * float(EXPERTS_PER_GROUP)
    return base + i1, base + i2, a1 / tot, a2 / tot


def _outproj_kernel(a_ref, b_ref, x_ref, wa_ref, wb_ref, g_ref, be_ref, rwt_ref, rb_ref,
                    x1_ref, route_ref, *, alpha):
    d = x_ref.shape[1]
    tm = x_ref.shape[0]
    mix = _dot(a_ref[...], wa_ref[...]) + _dot(b_ref[...], wb_ref[...])
    x1 = _layer_norm(alpha * x_ref[...] + mix, g_ref[...], be_ref[...])
    x1_ref[:, 0:d] = x1
    x_hi = x1.astype(BF16)
    x_lo = (x1 - x_hi.astype(F32)).astype(BF16)
    w = rwt_ref[...]
    w_hi = w.astype(BF16)
    w_lo = (w - w_hi.astype(F32)).astype(BF16)
    z_hi = _dot_nt(jnp.concatenate([w_hi, w_lo], axis=0), x_hi)
    zt = z_hi[0:N_EXPERTS] + z_hi[N_EXPERTS:2 * N_EXPERTS] + _dot_nt(w_hi, x_lo)
    afft = 1.0 / (1.0 + jnp.exp(-zt))
    selt = afft + rb_ref[...]
    aff = [afft[e:e + 1, :] for e in range(N_EXPERTS)]
    sel = [selt[e:e + 1, :] for e in range(N_EXPERTS)]
    e1, e2, w1, w2 = _route_rows(aff, sel)
    rec = jnp.concatenate([e1, e2, w1, w2, jnp.zeros((LANE - 4, tm), F32)], axis=0)
    route_ref[...] = rec[0:8]
    x1_ref[:, d:d + LANE] = rec.T


def _outproj(a, b, xt, wa, wb, g, be, rwt, rb, alpha, tm):
    n, d = xt.shape
    full = lambda i: (0, 0)
    return pl.pallas_call(
        functools.partial(_outproj_kernel, alpha=alpha),
        out_shape=[jax.ShapeDtypeStruct((n, d + LANE), F32), jax.ShapeDtypeStruct((8, n), F32)],
        grid=(n // tm,),
        in_specs=[pl.BlockSpec((tm, a.shape[1]), lambda i: (i, 0)),
                  pl.BlockSpec((tm, b.shape[1]), lambda i: (i, 0)),
                  pl.BlockSpec((tm, d), lambda i: (i, 0)),
                  pl.BlockSpec(wa.shape, full), pl.BlockSpec(wb.shape, full),
                  pl.BlockSpec((1, d), full), pl.BlockSpec((1, d), full),
                  pl.BlockSpec(rwt.shape, full), pl.BlockSpec(rb.shape, full)],
        out_specs=[pl.BlockSpec((tm, d + LANE), lambda i: (i, 0)), pl.BlockSpec((8, tm), lambda i: (0, i))],
        compiler_params=_cparams(("parallel",)),
        name="outproj_ln_route",
    )(a, b, xt, wa, wb, g, be, rwt, rb)


def _moe_kernel(bea_ref, beb_ref, bbase_ref, nact_ref, src_ref,
                x_hbm, wga_ref, wgb_ref, wua_ref, wub_ref, wda_ref, wdb_ref, o_ref, xbuf0, xbuf1, sem, *, tm, d):
    i = pl.program_id(0)
    nact = nact_ref[0]
    last = src_ref.shape[0] - 1
    bufs = (xbuf0, xbuf1)

    def issue(blk, slot):
        base = bbase_ref[blk]
        for r in range(tm):
            tok = src_ref[jnp.minimum(base + r, last)]
            pltpu.make_async_copy(x_hbm.at[pl.ds(tok, 1), :], bufs[slot].at[pl.ds(r, 1), :],
                                  sem.at[slot]).start(priority=r % 2)

    def wait(slot):
        pltpu.make_async_copy(x_hbm.at[pl.ds(0, tm), :], bufs[slot], sem.at[slot]).wait()

    @pl.when((i == 0) & (nact > 0))
    def _():
        issue(0, 0)

    def step(slot):
        wait(slot)

        @pl.when(i + 1 < nact)
        def _():
            issue(i + 1, 1 - slot)

        buf = bufs[slot]
        xb = buf[:, 0:d].astype(BF16)
        rec = buf[:, d:d + LANE]
        e1, e2, w1, w2 = rec[:, 0:1], rec[:, 1:2], rec[:, 2:3], rec[:, 3:4]
        out = None
        for e_ref, wg_ref, wu_ref, wd_ref in ((bea_ref, wga_ref, wua_ref, wda_ref),
                                              (beb_ref, wgb_ref, wub_ref, wdb_ref)):
            expert = e_ref[i].astype(F32)
            gate_w = jnp.where(e1 == expert, w1, jnp.where(e2 == expert, w2, 0.0))
            gate = _dot(xb, wg_ref[...])
            up = _dot(xb, wu_ref[...])
            h = gate * (1.0 / (1.0 + jnp.exp(-gate))) * up * gate_w
            part = _dot(h.astype(BF16), wd_ref[...])
            out = part if out is None else out + part
        o_ref[...] = out

    for slot in range(2):
        pl.when((i < nact) & (i % 2 == slot))(functools.partial(step, slot))

    @pl.when(i >= nact)
    def _():
        o_ref[...] = jnp.zeros_like(o_ref)


def _moe_experts(x1e, wg, wu, wd, bea, beb, bbase, nact, src, tm):
    n, de = x1e.shape
    d = de - LANE
    nblk = bea.shape[0]
    f = wg.shape[2]
    amap = lambda i, bea, beb, *_: (bea[i], 0, 0)
    bmap = lambda i, bea, beb, *_: (beb[i], 0, 0)
    grid_spec = pltpu.PrefetchScalarGridSpec(
        num_scalar_prefetch=5,
        grid=(nblk,),
        in_specs=[pl.BlockSpec(memory_space=pl.ANY),
                  pl.BlockSpec((None, d, f), amap), pl.BlockSpec((None, d, f), bmap),
                  pl.BlockSpec((None, d, f), amap), pl.BlockSpec((None, d, f), bmap),
                  pl.BlockSpec((None, f, d), amap), pl.BlockSpec((None, f, d), bmap)],
        out_specs=pl.BlockSpec((tm, d), lambda i, *_: (i, 0)),
        scratch_shapes=[pltpu.VMEM((tm, de), F32), pltpu.VMEM((tm, de), F32), pltpu.SemaphoreType.DMA((2,))],
    )
    return pl.pallas_call(
        functools.partial(_moe_kernel, tm=tm, d=d),
        out_shape=jax.ShapeDtypeStruct((nblk * tm, d), F32),
        grid_spec=grid_spec,
        compiler_params=_cparams(("arbitrary",)),
        name="moe_experts",
    )(bea, beb, bbase, nact, src, x1e, wg, wg, wu, wu, wd, wd)


def _combine_kernel(p_ref, x_ref, g_ref, be_ref, o_hbm, y_ref, obuf0, obuf1, sem, *, alpha, tm, d):
    i = pl.program_id(0)
    nsteps = pl.num_programs(0)

    bufs = (obuf0, obuf1)

    def issue(blk, slot):
        base = blk * tm
        for r in range(tm):
            pltpu.make_async_copy(o_hbm.at[pl.ds(p_ref[base + r], 1), :],
                                  bufs[slot].at[pl.ds(r, 1), :], sem.at[slot]).start(priority=r % 2)

    def wait(slot):
        pltpu.make_async_copy(o_hbm.at[pl.ds(0, tm), :], bufs[slot], sem.at[slot]).wait()

    @pl.when(i == 0)
    def _():
        issue(0, 0)

    def step(slot):
        wait(slot)

        @pl.when(i + 1 < nsteps)
        def _():
            issue(i + 1, 1 - slot)

        y = alpha * x_ref[:, 0:d] + bufs[slot][...]
        y_ref[...] = _layer_norm(y, g_ref[...], be_ref[...])

    for slot in range(2):
        pl.when(i % 2 == slot)(functools.partial(step, slot))


def _combine(x1e, g, be, o, pos, alpha, tm):
    n, de = x1e.shape
    d = de - LANE
    grid_spec = pltpu.PrefetchScalarGridSpec(
        num_scalar_prefetch=1,
        grid=(n // tm,),
        in_specs=[pl.BlockSpec((tm, de), lambda i, *_: (i, 0)),
                  pl.BlockSpec((1, d), lambda i, *_: (0, 0)),
                  pl.BlockSpec((1, d), lambda i, *_: (0, 0)),
                  pl.BlockSpec(memory_space=pl.ANY)],
        out_specs=pl.BlockSpec((tm, d), lambda i, *_: (i, 0)),
        scratch_shapes=[pltpu.VMEM((tm, d), F32), pltpu.VMEM((tm, d), F32), pltpu.SemaphoreType.DMA((2,))],
    )
    return pl.pallas_call(
        functools.partial(_combine_kernel, alpha=alpha, tm=tm, d=d),
        out_shape=jax.ShapeDtypeStruct((n, d), F32),
        grid_spec=grid_spec,
        compiler_params=_cparams(("arbitrary",)),
        name="moe_combine_ln",
    )(pos, x1e, g, be, o)


def _moe_layer(x1e, route_t, wg, wu, wd, g, be, alpha, tm_e, tm_c):
    n = x1e.shape[0]
    pairs = [(a, b) for a in range(EXPERTS_PER_GROUP) for b in range(a + 1, EXPERTS_PER_GROUP)]
    nseg = N_GROUPS * len(pairs)
    seg_a = jnp.array([g * EXPERTS_PER_GROUP + a for g in range(N_GROUPS) for a, b in pairs], I32)
    seg_b = jnp.array([g * EXPERTS_PER_GROUP + b for g in range(N_GROUPS) for a, b in pairs], I32)
    e1, e2 = route_t[0].astype(I32), route_t[1].astype(I32)
    lo, hi = jnp.minimum(e1, e2), jnp.maximum(e1, e2)
    la, lb = lo % EXPERTS_PER_GROUP, hi % EXPERTS_PER_GROUP
    pair_id = la * (2 * EXPERTS_PER_GROUP - 1 - la) // 2 + (lb - la - 1)
    seg = (lo // EXPERTS_PER_GROUP) * len(pairs) + pair_id
    onehot = (seg[:, None] == jnp.arange(nseg, dtype=I32)[None, :]).astype(I32)
    rank = jnp.take_along_axis(jnp.cumsum(onehot, axis=0) - onehot, seg[:, None], axis=1)[:, 0]
    counts = jnp.sum(onehot, axis=0)
    nblk_s = (counts + tm_e - 1) // tm_e
    blk_start = jnp.cumsum(nblk_s) - nblk_s
    seg_start = jnp.cumsum(counts) - counts
    pos = (blk_start[seg] * tm_e + rank).astype(I32)
    src = jnp.argsort(seg, stable=True).astype(I32)
    nblk = n // tm_e + nseg
    blk = jnp.arange(nblk, dtype=I32)
    blk_end = jnp.cumsum(nblk_s)
    bseg = jnp.minimum(jnp.sum((blk[:, None] >= blk_end[None, :]).astype(I32), axis=1), nseg - 1)
    bbase = (seg_start[bseg] + (blk - blk_start[bseg]) * tm_e).astype(I32)
    nact = jnp.sum(nblk_s).astype(I32).reshape(1)
    o = _moe_experts(x1e, wg, wu, wd, seg_a[bseg], seg_b[bseg], bbase, nact, src, tm_e)
    return _combine(x1e, g, be, o, pos, alpha, tm_c)


def _rms(x, g):
    return x * lax.rsqrt(jnp.mean(x * x, axis=-1, keepdims=True) + RMS_EPS) * g

ODD_OFF = dict(qc=0, kc=512, vc=640, qi=768, ki=1024, wi=1152, cq=1280, ckv=1536, kra=1664, krb=1792, end=1920)


def _odd_proj_kernel(x_ref, w_ref, qng_ref, kvng_ref, wqa_ref, wqb_ref, wk_ref, wv_ref, cos_ref, sin_ref,
                     tq_ref, tk_ref,
                     qc_ref, kc_ref, vc_ref, qi_ref, ki_ref, wi_ref, q8_ref, k8_ref, v8_ref, *, q_scale):
    o = ODD_OFF
    tm = x_ref.shape[0]
    lo = _lane_lo((tm, LANE))
    h = _dot(x_ref[...].astype(BF16), w_ref[...])
    kdup = h[:, o["kc"]:o["vc"]]
    vdup = h[:, o["vc"]:o["qi"]]
    kc_ref[:, 0:LANE] = jnp.where(lo, kdup, tk_ref[:, 0:LANE].astype(F32)).astype(BF16)
    kc_ref[:, LANE:2 * LANE] = jnp.where(lo, tk_ref[:, LANE:2 * LANE].astype(F32), kdup).astype(BF16)
    vc_ref[:, 0:LANE] = jnp.where(lo, vdup, 1.0).astype(BF16)
    vc_ref[:, LANE:2 * LANE] = jnp.where(lo, 1.0, vdup).astype(BF16)
    qi_ref[...] = h[:, o["qi"]:o["ki"]].astype(BF16)
    ki_ref[...] = h[:, o["ki"]:o["wi"]].astype(BF16)
    wi_ref[...] = h[:, o["wi"]:o["wi"] + IDX_HEADS]
    cos = cos_ref[...]
    sin = sin_ref[...]
    cqn = _rms(h[:, o["cq"]:o["ckv"]], qng_ref[...]).astype(BF16)
    qa = _dot(cqn, wqa_ref[...])
    qb = _dot(cqn, wqb_ref[...])
    ckvn = _rms(h[:, o["ckv"]:o["kra"]], kvng_ref[...]).astype(BF16)
    kn = _dot(ckvn, wk_ref[...])
    v8 = _dot(ckvn, wv_ref[...])
    kr = h[:, o["kra"]:o["krb"]] * cos + h[:, o["krb"]:o["end"]] * sin
    for hd in range(D_HEADS):
        sl = slice(hd * LANE, (hd + 1) * LANE)
        pj = slice((hd // 2) * LANE, (hd // 2 + 1) * LANE)
        own = lo if hd % 2 == 0 else ~lo
        qc_ref[:, sl] = jnp.where(own, h[:, pj], tq_ref[:, sl].astype(F32)).astype(BF16)
        q8_ref[:, sl] = ((qa[:, sl] * cos + qb[:, sl] * sin) * q_scale).astype(BF16)
        k8_ref[:, sl] = (kn[:, sl] + kr).astype(BF16)
        v8_ref[:, sl] = jnp.where(own, v8[:, pj], 1.0).astype(BF16)


def _odd_proj(xt, w_cat, qng, kvng, wqa, wqb, wk, wv, cos_t, sin_t, tq_t, tk_t, seq, tm):
    n, d = xt.shape
    full = lambda i: (0, 0)
    tps = seq // tm
    pos = lambda i: (i % tps, 0)
    widths = (1024, 256, 256, 256, 128)
    out_shape = [jax.ShapeDtypeStruct((n, w), BF16) for w in widths]
    out_specs = [pl.BlockSpec((tm, w), lambda i: (i, 0)) for w in widths]
    out_shape.append(jax.ShapeDtypeStruct((n, IDX_HEADS), F32))
    out_specs.append(pl.BlockSpec((tm, IDX_HEADS), lambda i: (i, 0)))
    for w in (1024, 1024, 1024):
        out_shape.append(jax.ShapeDtypeStruct((n, w), BF16))
        out_specs.append(pl.BlockSpec((tm, w), lambda i: (i, 0)))
    return pl.pallas_call(
        functools.partial(_odd_proj_kernel, q_scale=(MLA_NOPE + MLA_ROPE) ** -0.5 * LOG2E),
        out_shape=out_shape,
        grid=(n // tm,),
        in_specs=[pl.BlockSpec((tm, d), lambda i: (i, 0)),
                  pl.BlockSpec(w_cat.shape, full),
                  pl.BlockSpec(qng.shape, full), pl.BlockSpec(kvng.shape, full),
                  pl.BlockSpec(wqa.shape, full), pl.BlockSpec(wqb.shape, full),
                  pl.BlockSpec(wk.shape, full), pl.BlockSpec(wv.shape, full),
                  pl.BlockSpec((tm, LANE), pos), pl.BlockSpec((tm, LANE), pos),
                  pl.BlockSpec((tm, C_HEADS * LANE), pos), pl.BlockSpec((tm, 2 * LANE), pos)],
        out_specs=out_specs,
        compiler_params=_cparams(("parallel",)),
        name="odd_proj",
    )(xt, w_cat, qng, kvng, wqa, wqb, wk, wv, cos_t, sin_t, tq_t, tk_t)


def _dsa_kernel(q_ref, k_ref, v_ref, qi_ref, ki_ref, wt_ref, o_ref, sc_ref, mb_ref, *, top_k, chunk, qb):
    i = pl.program_id(1)
    nc = (i * qb + qb + chunk - 1) // chunk
    lo = _lane_lo((qb, LANE))
    s_row2 = lax.broadcasted_iota(I32, (2 * chunk, qb), 0)
    t_lane2 = i * qb + lax.broadcasted_iota(I32, (2 * chunk, qb), 1)

    qidx = []
    for h in range(IDX_HEADS):
        qp = qi_ref[:, (h // 2) * LANE:(h // 2 + 1) * LANE]
        qidx.append(jnp.where(lo if h % 2 == 0 else ~lo, qp, jnp.zeros_like(qp)))
    q_stack = jnp.concatenate(qidx, axis=0)
    w = wt_ref[...]

    npair = (nc + 1) // 2

    def pair_start(cp):
        return cp * (2 * chunk)

    def select(pairs):
        def score_pair(cp):
            start = pair_start(cp)
            d = _dot_nt(ki_ref[pl.ds(start, 2 * chunk), :], q_stack)
            sc = w[0:1, :] * jnp.maximum(d[:, 0:qb], 0.0)
            for h in range(1, IDX_HEADS):
                sc = sc + w[h:h + 1, :] * jnp.maximum(d[:, h * qb:(h + 1) * qb], 0.0)
            sc = jnp.where(sc == 0.0, 0.0, sc)
            sc_ref[pl.ds(start, 2 * chunk), :] = jnp.where(start + s_row2 <= t_lane2, sc, NEG_INF)

        for cp in range(pairs):
            score_pair(cp)

        nacc = 4 * 8

        def count(trial, strict):
            acc = jnp.zeros((nacc, qb), F32)
            for cp in range(pairs):
                sc = sc_ref[pl.ds(pair_start(cp), 2 * chunk), :]
                hit = (sc > trial) if strict else (sc >= trial)
                acc = acc + jnp.sum(jnp.where(hit, 1.0, 0.0).reshape(2 * chunk // nacc, nacc, qb), axis=0)
            return jnp.sum(acc, axis=0, keepdims=True)

        def decode(key):
            return lax.bitcast_convert_type(jnp.where(key < 0, key ^ jnp.int32(0x7FFFFFFF), key), F32)

        kf = float(top_k)
        key_neg_inf = jnp.int32(-2139095041)
        cand = jnp.where(count(jnp.zeros((1, qb), F32), False) >= kf, 0, jnp.int32(-2 ** 31)).astype(I32)

        def bit_step(j, cand):
            trial = cand + lax.shift_left(jnp.int32(1), 30 - j)
            ok = (trial <= key_neg_inf) | (count(decode(trial), False) >= kf)
            return jnp.where(ok, trial, cand)

        thr = decode(lax.fori_loop(0, 31, bit_step, cand))
        need = kf - count(thr, True)

        r0 = lax.broadcasted_iota(I32, (chunk, chunk), 0)
        r1 = lax.broadcasted_iota(I32, (chunk, chunk), 1)
        below = jnp.where(r1 < r0, 1.0, 0.0).astype(BF16)
        need = jnp.where(thr == NEG_INF, 0.0, need)

        offset = jnp.zeros((1, qb), F32)
        for cp in range(pairs):
            start = pair_start(cp)
            sc = sc_ref[pl.ds(start, 2 * chunk), :]
            eq = sc == thr
            eqf = jnp.where(eq, 1.0, 0.0)
            eqb = eqf.astype(BF16)
            first = jnp.sum(eqf[0:chunk], axis=0, keepdims=True)
            prefix = jnp.concatenate([_dot(below, eqb[0:chunk]) + offset,
                                      _dot(below, eqb[chunk:2 * chunk]) + (offset + first)], axis=0)
            bias = jnp.where(sc > thr, 0.0, jnp.where(eq, jnp.where(prefix < need, 0.0, NEG_INF), NEG_INF))
            mb_ref[:, pl.ds(start, 2 * chunk)] = bias.T
            offset = offset + first + jnp.sum(eqf[chunk:2 * chunk], axis=0, keepdims=True)

    for pairs in range(1, k_ref.shape[0] // (2 * chunk) + 1):
        pl.when(npair == pairs)(functools.partial(select, pairs))

    nslab = qb // BLOCK
    chains = [(h, r) for h in range(C_HEADS) for r in range(nslab)]

    def att_step(start, width, carry):
        kp = [k_ref[pl.ds(start, width), par * LANE:(par + 1) * LANE] for par in range(2)]
        vp = [v_ref[pl.ds(start, width), par * LANE:(par + 1) * LANE] for par in range(2)]
        mb = [mb_ref[r * BLOCK:(r + 1) * BLOCK, pl.ds(start, width)].astype(BF16) for r in range(nslab)]
        qs = [q_ref[r * BLOCK:(r + 1) * BLOCK, h * LANE:(h + 1) * LANE] for h, r in chains]
        return _online_softmax_chains(qs, [kp[h % 2] for h, r in chains], [vp[h % 2] for h, r in chains], carry,
                                      biases=[mb[r] for h, r in chains], guard_empty=True)

    init = tuple((jnp.full((BLOCK, 1), NEG_INF, F32), jnp.zeros((BLOCK, LANE), F32)) for _ in chains)
    carry = lax.fori_loop(
        0, nc // 2, lambda cp, c: att_step(pl.multiple_of(cp * 2 * chunk, 2 * chunk), 2 * chunk, c), init)
    carry = lax.fori_loop(
        0, nc % 2, lambda _, c: att_step(pl.multiple_of((nc // 2) * 2 * chunk, chunk), chunk, c), carry)
    accs = [jnp.concatenate([carry[h * nslab + r][1] for r in range(nslab)], axis=0) for h in range(C_HEADS)]
    _normalize_pairs(accs, o_ref)


def _dsa(qc, kc, vc, qi, ki, wit):
    b, s, _ = qc.shape
    top_k = min(DSA_TOPK_MAX, s // 4)
    chunk = 256
    qb = min(2 * BLOCK, s)
    assert s % (2 * chunk) == 0 and top_k <= chunk and chunk % qb == 0
    blk = lambda bi, i: (bi, i, 0)
    seq = lambda bi, i: (bi, 0, 0)
    return pl.pallas_call(
        functools.partial(_dsa_kernel, top_k=top_k, chunk=chunk, qb=qb),
        out_shape=jax.ShapeDtypeStruct((b, s, C_HEADS * HEAD_DIM), BF16),
        grid=(b, s // qb),
        in_specs=[pl.BlockSpec((None, qb, C_HEADS * LANE), blk),
                  pl.BlockSpec((None, s, 2 * LANE), seq),
                  pl.BlockSpec((None, s, 2 * LANE), seq),
                  pl.BlockSpec((None, qb, 256), blk),
                  pl.BlockSpec((None, s, LANE), seq),
                  pl.BlockSpec((None, IDX_HEADS, qb), lambda bi, i: (bi, 0, i))],
        out_specs=pl.BlockSpec((None, qb, 512), blk),
        scratch_shapes=[pltpu.VMEM((s, qb), F32), pltpu.VMEM((qb, s), F32)],
        compiler_params=_cparams(("parallel", "arbitrary")),
        name="dsa_attn",
    )(qc, kc, vc, qi, ki, wit)


def _dup64(w):
    d, c = w.shape
    w = w.reshape(d, c // HEAD_DIM, 1, HEAD_DIM)
    return jnp.broadcast_to(w, (d, c // HEAD_DIM, 2, HEAD_DIM)).reshape(d, 2 * c)


def _pad_cols(w, width):
    return jnp.pad(w, ((0, 0), (0, width - w.shape[1])))


def _split_cols(w, sizes):
    out, start = [], 0
    for n in sizes:
        out.append(w[:, start:start + n])
        start += n
    return out


def _rope_partner(w):
    half = MLA_ROPE // 2
    return jnp.concatenate([-w[..., half:], w[..., :half]], axis=-1)


def _rope_block(w_rope):
    d = w_rope.shape[0]
    return jnp.concatenate([jnp.zeros((d, MLA_NOPE), F32), w_rope,
                            jnp.zeros((d, LANE - MLA_NOPE - MLA_ROPE), F32)], axis=1)


def _alibi_tables(s):
    slopes = 2.0 ** (-8.0 * jnp.arange(1, C_HEADS + 1, dtype=F32) / C_HEADS)
    a = slopes * LOG2E
    pos = jnp.arange(s, dtype=F32)
    a_sp = _split3(a)
    r_sp = _split3(-(a[None, :] * pos[:, None]))
    ones = jnp.ones((s, C_HEADS), F32)
    q_terms = [128.0 * t * ones for t in a_sp] + [t * ones for t in a_sp] + list(r_sp)
    q_aug = jnp.stack(q_terms, axis=-1)
    q_aug = jnp.pad(q_aug, ((0, 0), (0, 0), (0, HEAD_DIM - q_aug.shape[-1])))
    zeros = jnp.zeros_like(q_aug)
    even = jnp.concatenate([zeros, q_aug], axis=-1)
    odd = jnp.concatenate([q_aug, zeros], axis=-1)
    parity = (jnp.arange(C_HEADS) % 2 == 0)[None, :, None]
    tq_t = jnp.where(parity, even, odd).reshape(s, C_HEADS * LANE)
    s_hi = jnp.floor(pos / LANE)
    s_lo = pos - LANE * s_hi
    k_aug = jnp.stack([s_hi] * 3 + [s_lo] * 3 + [jnp.ones_like(pos)] * 3, axis=-1)
    k_aug = jnp.pad(k_aug, ((0, 0), (0, HEAD_DIM - k_aug.shape[-1])))
    kz = jnp.zeros_like(k_aug)
    tk_t = jnp.concatenate([kz, k_aug, k_aug, kz], axis=-1)
    return tq_t.astype(BF16), tk_t.astype(BF16)


def kernel(x, even_w_in, even_b_f, even_sinks, odd_w_in, odd_q_norm, odd_kv_norm, odd_w_uq, odd_w_ukv,
           w_o, ln_g, ln_b, router_w, router_b, moe_w_gate, moe_w_up, moe_w_down):
    b, s, d = x.shape
    n = b * s
    depth = w_o.shape[0]
    alpha = (2 * depth) ** 0.25
    tm = min(512, s)
    tq = min(256, s)
    tm_e = min(256, n)
    tm_c = min(256, n)
    att_scale = HEAD_DIM ** -0.5
    xt = x.reshape(n, d)
    rwt = router_w.T
    rb = router_b.reshape(N_EXPERTS, 1)

    for l in range(depth):
        j = l // 2
        if l % 2 == 0:
            qa, ka, va, qb, kb, vb, fg = _split_cols(
                even_w_in[j], (512, 128, 128, 512, 512, 512, B_HEADS))
            w_cat = jnp.concatenate([qa * (att_scale * LOG2E), _dup64(ka), _dup64(va), qb * (att_scale * LOG2E), kb, vb,
                                     _pad_cols(fg, LANE)], axis=1).astype(BF16)
            bf_pad = _pad_cols(even_b_f[j].reshape(1, B_HEADS), LANE)
            qa_, ka_, va_, qf, kf, vf = _even_proj(xt, w_cat, bf_pad, s, tm)
            r3 = lambda a: a.reshape(b, s, a.shape[1])
            out_a = _swa(even_sinks[j], r3(qa_), r3(ka_), r3(va_))
            out_b = _flash(r3(qf), r3(kf), r3(vf), tq=tq, nh=8, name="fox_attn")
        else:
            (qc, kc, vc, qi, ki, wi, cq, ckv, kr) = _split_cols(
                odd_w_in[j], (512, 64, 64, IDX_HEADS * IDX_DIM, IDX_DIM, IDX_HEADS,
                              MLA_Q_RANK, MLA_KV_RANK, MLA_ROPE))
            w_cat = jnp.concatenate([
                qc * (att_scale * LOG2E), _dup64(kc), _dup64(vc), qi * (IDX_DIM ** -0.5), _dup64(ki),
                _pad_cols(wi * (IDX_HEADS ** -0.5), LANE), cq, ckv,
                _rope_block(kr), _rope_block(_rope_partner(kr))], axis=1).astype(BF16)
            dq = MLA_NOPE + MLA_ROPE
            wuq = odd_w_uq[j].reshape(MLA_Q_RANK, D_HEADS, dq)
            pad_q = jnp.zeros((MLA_Q_RANK, D_HEADS, LANE - dq), F32)
            wqa = jnp.concatenate([wuq, pad_q], axis=2).reshape(MLA_Q_RANK, D_HEADS * LANE).astype(BF16)
            wqb = jnp.concatenate([jnp.zeros((MLA_Q_RANK, D_HEADS, MLA_NOPE), F32),
                                   _rope_partner(wuq[..., MLA_NOPE:]), pad_q],
                                  axis=2).reshape(MLA_Q_RANK, D_HEADS * LANE).astype(BF16)
            wukv = odd_w_ukv[j].reshape(MLA_KV_RANK, D_HEADS, MLA_NOPE + MLA_V)
            wk = jnp.concatenate([wukv[..., :MLA_NOPE], jnp.zeros((MLA_KV_RANK, D_HEADS, LANE - MLA_NOPE), F32)],
                                 axis=2).reshape(MLA_KV_RANK, D_HEADS * LANE).astype(BF16)
            wv = wukv[..., MLA_NOPE:].reshape(MLA_KV_RANK, D_HEADS * MLA_V).astype(BF16)
            half = MLA_ROPE // 2
            inv = ROPE_THETA ** (-jnp.arange(half, dtype=F32) / half)
            ang = jnp.arange(s, dtype=F32)[:, None] * inv[None, :]
            cos_h, sin_h = jnp.cos(ang), jnp.sin(ang)
            pad_t = jnp.zeros((s, LANE - dq), F32)
            cos_t = jnp.concatenate([jnp.ones((s, MLA_NOPE), F32), cos_h, cos_h, pad_t], axis=1)
            sin_t = jnp.concatenate([jnp.zeros((s, MLA_NOPE), F32), sin_h, sin_h, pad_t], axis=1)
            tq_t, tk_t = _alibi_tables(s)
            qc_, kc_, vc_, qi_, ki_, wi_, q8, k8, v8 = _odd_proj(
                xt, w_cat, odd_q_norm[j].reshape(1, -1), odd_kv_norm[j].reshape(1, -1),
                wqa, wqb, wk, wv, cos_t, sin_t, tq_t, tk_t, s, tm)
            r3 = lambda a: a.reshape(b, s, a.shape[1])
            out_a = _dsa(r3(qc_), r3(kc_), r3(vc_), r3(qi_), r3(ki_), r3(wi_).transpose(0, 2, 1))
            out_b = _flash(r3(q8), r3(k8), r3(v8), tq=tq, nh=8, name="mla_attn")
        half_w = out_a.shape[2]
        wo = w_o[l].astype(BF16)
        x1e, route_t = _outproj(out_a.reshape(n, half_w), out_b.reshape(n, -1), xt,
                                wo[:half_w], wo[half_w:], ln_g[l, 0].reshape(1, d), ln_b[l, 0].reshape(1, d),
                                rwt, rb, alpha, tm)
        xt = _moe_layer(x1e, route_t, moe_w_gate[l].astype(BF16), moe_w_up[l].astype(BF16),
                        moe_w_down[l].astype(BF16), ln_g[l, 1].reshape(1, d), ln_b[l, 1].reshape(1, d),
                        alpha, tm_e, tm_c)
    return xt.reshape(b, s, d)
```

```python
import functools
import math

import jax
import jax.numpy as jnp
from jax import lax
from jax.experimental import pallas as pl
from jax.experimental.pallas import tpu as pltpu

F32 = jnp.float32
BF16 = jnp.bfloat16
I32 = jnp.int32

LANE = 128
HEAD_DIM = 64
BLOCK = 128
A_HEADS, A_KV_HEADS, WINDOW = 8, 2, 128
B_HEADS = 8
C_HEADS, IDX_HEADS, IDX_DIM, DSA_TOPK_MAX = 8, 4, 64, 256
D_HEADS, MLA_Q_RANK, MLA_KV_RANK, MLA_NOPE, MLA_ROPE, MLA_V = 8, 256, 128, 64, 32, 64
ROPE_THETA = 10000.0
N_EXPERTS, N_GROUPS = 16, 4
EXPERTS_PER_GROUP = N_EXPERTS // N_GROUPS
LN_EPS, RMS_EPS = 1e-5, 1e-6
NEG_INF = float("-inf")
LOG2E = math.log2(math.e)

A_Q = A_HEADS * HEAD_DIM
A_KV = A_KV_HEADS * LANE
B_W = B_HEADS * HEAD_DIM
C_Q = C_HEADS * HEAD_DIM
IDX_Q = IDX_HEADS * IDX_DIM


def _offsets(**widths):
    off, start = {}, 0
    for name, w in widths.items():
        off[name] = start
        start += w
    off["end"] = start
    return off


EVEN_OFF = _offsets(qa=A_Q, ka=A_KV, va=A_KV, qb=B_W, kb=B_W, vb=B_W, fg=LANE)
ODD_OFF = _offsets(qc=C_Q, kc=LANE, vc=LANE, qi=IDX_Q, ki=LANE, wi=LANE, cq=MLA_Q_RANK, ckv=MLA_KV_RANK,
                   kra=LANE, krb=LANE)
VMEM_LIMIT = 56 * 1024 * 1024


def _cparams(sem):
    return pltpu.CompilerParams(dimension_semantics=sem, vmem_limit_bytes=VMEM_LIMIT)


def _dot(a, b):
    return jnp.dot(a, b, preferred_element_type=F32)


def _dot_nt(a, b):
    return lax.dot_general(a, b, (((1,), (1,)), ((), ())), preferred_element_type=F32)


def _lane_lo(shape):
    return lax.broadcasted_iota(I32, shape, len(shape) - 1) % LANE < HEAD_DIM


def _layer_norm(y, g, b):
    mu = jnp.mean(y, axis=-1, keepdims=True)
    yc = y - mu
    var = jnp.mean(yc * yc, axis=-1, keepdims=True)
    return yc * lax.rsqrt(var + LN_EPS) * g + b


def _top16(v):
    bits = lax.bitcast_convert_type(v, I32) & jnp.int32(-65536)
    return lax.bitcast_convert_type(bits, F32)


def _split3(x):
    s1 = _top16(x)
    r = x - s1
    s2 = _top16(r)
    return s1, s2, r - s2


def _aug_rel(shape, h):
    lane = lax.broadcasted_iota(I32, shape, 1)
    return lane - (HEAD_DIM if h % 2 == 0 else 0)


def _even_proj_kernel(x_ref, w_ref, bf_ref, qa_ref, ka_ref, va_ref, qf_ref, kf_ref, vf_ref,
                      carry_ref, *, tiles_per_seq):
    t = pl.program_id(0)
    tm = x_ref.shape[0]
    h = _dot(x_ref[...].astype(BF16), w_ref[...])
    o = EVEN_OFF
    qa_ref[...] = h[:, o["qa"]:o["ka"]].astype(BF16)
    ka_ref[...] = h[:, o["ka"]:o["va"]].astype(BF16)
    va_ref[...] = h[:, o["va"]:o["qb"]].astype(BF16)
    z = h[:, o["fg"]:o["end"]] + bf_ref[...]
    logf = jnp.minimum(z, 0.0) - jnp.log1p(jnp.exp(-jnp.abs(z)))

    @pl.when(t % tiles_per_seq == 0)
    def _():
        carry_ref[...] = jnp.zeros_like(carry_ref)

    row = lax.broadcasted_iota(I32, (BLOCK, BLOCK), 0)
    col = lax.broadcasted_iota(I32, (BLOCK, BLOCK), 1)
    tri = jnp.where(row >= col, 1.0, 0.0).astype(BF16)
    parts = [p.astype(BF16) for p in _split3(logf)]
    run = carry_ref[...]
    blocks = []
    for blk in range(tm // BLOCK):
        cb = run
        for part in parts:
            cb = cb + _dot(tri, part[blk * BLOCK:(blk + 1) * BLOCK])
        blocks.append(cb)
        run = cb[BLOCK - 1:BLOCK, :]
    c = jnp.concatenate(blocks, axis=0)
    carry_ref[...] = run

    s1, s2, s3 = _split3(c * LOG2E)
    lo = _lane_lo((tm, LANE))
    for hd in range(B_HEADS):
        own = lo if hd % 2 == 0 else ~lo
        rel = _aug_rel((tm, LANE), hd)
        b1, b2, b3 = s1[:, hd:hd + 1], s2[:, hd:hd + 1], s3[:, hd:hd + 1]
        q_aug = jnp.where(rel == 3, b1, jnp.where(rel == 4, b2, jnp.where(rel == 5, b3,
                          jnp.where((rel >= 0) & (rel < 3), -1.0, 0.0))))
        k_aug = jnp.where(rel == 0, b1, jnp.where(rel == 1, b2, jnp.where(rel == 2, b3,
                          jnp.where((rel >= 3) & (rel < 6), 1.0, 0.0))))
        pj = (hd // 2) * LANE
        sl = slice(hd * LANE, (hd + 1) * LANE)
        qf_ref[:, sl] = jnp.where(own, h[:, o["qb"] + pj:o["qb"] + pj + LANE], q_aug).astype(BF16)
        kf_ref[:, sl] = jnp.where(own, h[:, o["kb"] + pj:o["kb"] + pj + LANE], k_aug).astype(BF16)
        vf_ref[:, sl] = jnp.where(own, h[:, o["vb"] + pj:o["vb"] + pj + LANE], 1.0).astype(BF16)


def _even_proj(xt, w_cat, bf_pad, seq, tm):
    n, d = xt.shape
    wcols = w_cat.shape[1]
    widths = (A_Q, A_KV, A_KV, B_HEADS * LANE, B_HEADS * LANE, B_HEADS * LANE)
    out_shape = [jax.ShapeDtypeStruct((n, w), BF16) for w in widths]
    out_specs = [pl.BlockSpec((tm, w), lambda i: (i, 0)) for w in widths]
    return pl.pallas_call(
        functools.partial(_even_proj_kernel, tiles_per_seq=seq // tm),
        out_shape=out_shape,
        grid=(n // tm,),
        in_specs=[pl.BlockSpec((tm, d), lambda i: (i, 0)),
                  pl.BlockSpec((d, wcols), lambda i: (0, 0)),
                  pl.BlockSpec((1, LANE), lambda i: (0, 0))],
        out_specs=out_specs,
        scratch_shapes=[pltpu.VMEM((1, LANE), F32)],
        compiler_params=_cparams(("arbitrary",)),
        name="even_proj",
    )(xt, w_cat, bf_pad)


def _swa_kernel(sink_ref, q_ref, kp_ref, kc_ref, vp_ref, vc_ref, o_ref, *, nbq):
    i = pl.program_id(1)
    group = A_HEADS // A_KV_HEADS
    tl = lax.broadcasted_iota(I32, (BLOCK, 2 * BLOCK), 0)
    sl = lax.broadcasted_iota(I32, (BLOCK, 2 * BLOCK), 1)
    dist = BLOCK + tl - sl
    in_window = (dist >= 0) & (dist < WINDOW)
    distf = dist.astype(F32)
    lo = _lane_lo((BLOCK, LANE))

    def band(prev_ref, cur_ref, j, g):
        lanes = slice(g * LANE, (g + 1) * LANE)
        before = prev_ref[:, lanes] if j == 0 else cur_ref[(j - 1) * BLOCK:j * BLOCK, lanes]
        return jnp.concatenate([before, cur_ref[j * BLOCK:(j + 1) * BLOCK, lanes]], axis=0)

    stacked = {}
    for j in range(nbq):
        for g in range(A_KV_HEADS):
            qs = []
            for h in range(g * group, (g + 1) * group):
                qp = q_ref[j * BLOCK:(j + 1) * BLOCK, (h // 2) * LANE:(h // 2 + 1) * LANE]
                qs.append(jnp.where(lo if h % 2 == 0 else ~lo, qp, jnp.zeros_like(qp)))
            stacked[j, g] = _dot_nt(jnp.concatenate(qs, axis=0), band(kp_ref, kc_ref, j, g))
    ps, denoms = {}, {}
    for j in range(nbq):
        valid = (in_window & ((sl >= BLOCK) | (i > 0))) if j == 0 else in_window
        for h in range(A_HEADS):
            r = h % group
            slope = 2.0 ** (-8.0 * (h + 1) / A_HEADS) * LOG2E
            logits = jnp.where(valid, stacked[j, h // group][r * BLOCK:(r + 1) * BLOCK] - slope * distf, NEG_INF)
            sink = sink_ref[h] * LOG2E
            m = jnp.maximum(jnp.max(logits, axis=-1, keepdims=True), sink)
            e = jnp.exp2(logits - m)
            denoms[j, h] = jnp.sum(e, axis=-1, keepdims=True) + jnp.exp2(sink - m)
            ps[j, h] = e.astype(BF16)
    for j in range(nbq):
        outs = []
        for g in range(A_KV_HEADS):
            heads = range(g * group, (g + 1) * group)
            pv = _dot(jnp.concatenate([ps[j, h] for h in heads], axis=0), band(vp_ref, vc_ref, j, g))
            for r, h in enumerate(heads):
                outs.append(pv[r * BLOCK:(r + 1) * BLOCK] / denoms[j, h])
        for pj in range(A_HEADS // 2):
            o_ref[j * BLOCK:(j + 1) * BLOCK, pj * LANE:(pj + 1) * LANE] = jnp.where(
                lo, outs[2 * pj], outs[2 * pj + 1]).astype(BF16)


def _swa(sinks, qa, ka, va):
    b, s, _ = qa.shape
    nbq = min(4, s // BLOCK)
    prev = lambda bi, i: (bi, jnp.maximum(i * nbq - 1, 0), 0)
    cur = lambda bi, i: (bi, i, 0)
    return pl.pallas_call(
        functools.partial(_swa_kernel, nbq=nbq),
        out_shape=jax.ShapeDtypeStruct((b, s, A_HEADS * HEAD_DIM), BF16),
        grid=(b, s // (nbq * BLOCK)),
        in_specs=[pl.BlockSpec(memory_space=pltpu.SMEM),
                  pl.BlockSpec((None, nbq * BLOCK, A_Q), cur),
                  pl.BlockSpec((None, BLOCK, A_KV), prev),
                  pl.BlockSpec((None, nbq * BLOCK, A_KV), cur),
                  pl.BlockSpec((None, BLOCK, A_KV), prev),
                  pl.BlockSpec((None, nbq * BLOCK, A_KV), cur)],
        out_specs=pl.BlockSpec((None, nbq * BLOCK, A_Q), cur),
        compiler_params=_cparams(("parallel", "arbitrary")),
        name="swa_attn",
    )(sinks, qa, ka, ka, va, va)


def _normalize_pairs(accs, o_ref):
    lo = _lane_lo(accs[0].shape)
    outs = [a / pltpu.roll(a, HEAD_DIM, 1) for a in accs]
    for j in range(len(accs) // 2):
        o_ref[:, j * LANE:(j + 1) * LANE] = jnp.where(lo, outs[2 * j], outs[2 * j + 1]).astype(BF16)


def _online_softmax_chains(qs, ks, vs, carry, biases=None, masks=None, guard_empty=False):
    n = len(qs)
    ss = [_dot_nt(qs[i], ks[i]).astype(BF16) for i in range(n)]
    if biases is not None:
        ss = [ss[i] + biases[i] for i in range(n)]
    if masks is not None:
        ss = [jnp.where(masks[i], ss[i], NEG_INF) for i in range(n)]
    ms, ps, alphas = [], [], []
    for i in range(n):
        m = carry[i][0]
        m_new = jnp.maximum(m, jnp.max(ss[i], axis=-1, keepdims=True).astype(F32))
        m_ref = jnp.where(m_new == NEG_INF, 0.0, m_new) if guard_empty else m_new
        ps.append(jnp.exp2(ss[i] - m_ref.astype(BF16)))
        alphas.append(jnp.exp2(m - m_ref))
        ms.append(m_new)
    pvs = [_dot(ps[i], vs[i]) for i in range(n)]
    return tuple((ms[i], alphas[i] * carry[i][1] + pvs[i]) for i in range(n))


def _flash_kernel(q_ref, k_ref, v_ref, o_ref, *, tq, nh):
    qi = pl.program_id(2)
    nslab = tq // BLOCK
    chains = [(hh, r) for hh in range(nh) for r in range(nslab)]
    row = lax.broadcasted_iota(I32, (BLOCK, tq), 0)
    col = lax.broadcasted_iota(I32, (BLOCK, tq), 1)

    def lanes(hh):
        return slice(hh * LANE, (hh + 1) * LANE)

    def step(start, width, carry, diagonal):
        qs = [q_ref[r * BLOCK:(r + 1) * BLOCK, lanes(hh)] for hh, r in chains]
        if diagonal:
            ks = [k_ref[pl.ds(start, (r + 1) * BLOCK), lanes(hh)] for hh, r in chains]
            vs = [v_ref[pl.ds(start, (r + 1) * BLOCK), lanes(hh)] for hh, r in chains]
            masks = [(col <= row + r * BLOCK)[:, 0:(r + 1) * BLOCK] for hh, r in chains]
        else:
            ks = [k_ref[pl.ds(start, width), lanes(hh)] for hh, r in chains]
            vs = [v_ref[pl.ds(start, width), lanes(hh)] for hh, r in chains]
            masks = None
        return _online_softmax_chains(qs, ks, vs, carry, masks=masks)

    init = tuple((jnp.full((BLOCK, 1), NEG_INF, F32), jnp.zeros((BLOCK, LANE), F32))
                 for _ in range(nh * nslab))
    npairs = qi // 2
    carry = lax.fori_loop(
        0, npairs, lambda kp, c: step(pl.multiple_of(kp * 2 * tq, 2 * tq), 2 * tq, c, False), init)
    carry = lax.fori_loop(
        0, qi % 2, lambda _, c: step(pl.multiple_of(npairs * 2 * tq, tq), tq, c, False), carry)
    carry = step(pl.multiple_of(qi * tq, tq), tq, carry, True)
    accs = [jnp.concatenate([carry[hh * nslab + r][1] for r in range(nslab)], axis=0) for hh in range(nh)]
    _normalize_pairs(accs, o_ref)


def _flash(q, k, v, *, tq, nh, name):
    b, s, w = v.shape
    return pl.pallas_call(
        functools.partial(_flash_kernel, tq=tq, nh=nh),
        out_shape=jax.ShapeDtypeStruct((b, s, w // 2), BF16),
        grid=(b, w // (nh * LANE), s // tq),
        in_specs=[pl.BlockSpec((None, tq, nh * LANE), lambda bi, j, i: (bi, i, j)),
                  pl.BlockSpec((None, s, nh * LANE), lambda bi, j, i: (bi, 0, j)),
                  pl.BlockSpec((None, s, nh * LANE), lambda bi, j, i: (bi, 0, j))],
        out_specs=pl.BlockSpec((None, tq, nh * LANE // 2), lambda bi, j, i: (bi, i, j)),
        compiler_params=_cparams(("parallel", "parallel", "arbitrary")),
        name=name,
    )(q, k, v)


def _route_rows(aff, sel):
    gs = []
    for g in range(N_GROUPS):
        r = sel[g * EXPERTS_PER_GROUP:(g + 1) * EXPERTS_PER_GROUP]
        best = None
        for a in range(EXPERTS_PER_GROUP):
            for c in range(a + 1, EXPERTS_PER_GROUP):
                pair = r[a] + r[c]
                best = pair if best is None else jnp.maximum(best, pair)
        gs.append(best)
    gmax = functools.reduce(jnp.maximum, gs)
    gidx = jnp.full_like(gmax, float(N_GROUPS - 1))
    for g in range(N_GROUPS - 2, -1, -1):
        gidx = jnp.where(gs[g] == gmax, float(g), gidx)

    def pick(rows, k):
        out = rows[(N_GROUPS - 1) * EXPERTS_PER_GROUP + k]
        for g in range(N_GROUPS - 2, -1, -1):
            out = jnp.where(gidx == float(g), rows[g * EXPERTS_PER_GROUP + k], out)
        return out

    s = [pick(sel, k) for k in range(EXPERTS_PER_GROUP)]
    a = [pick(aff, k) for k in range(EXPERTS_PER_GROUP)]

    def first_max(vals):
        mx = functools.reduce(jnp.maximum, vals)
        idx = jnp.full_like(mx, float(len(vals) - 1))
        for k in range(len(vals) - 2, -1, -1):
            idx = jnp.where(vals[k] == mx, float(k), idx)
        return idx

    def take(vals, idx):
        out = vals[-1]
        for k in range(len(vals) - 2, -1, -1):
            out = jnp.where(idx == float(k), vals[k], out)
        return out

    i1 = first_max(s)
    s2 = [jnp.where(i1 == float(k), NEG_INF, s[k]) for k in range(EXPERTS_PER_GROUP)]
    i2 = first_max(s2)
    a1, a2 = take(a, i1), take(a, i2)
    tot = a1 + a2
    base = gidx * float(EXPERTS_PER_GROUP)
    return base + i1, base + i2, a1 / tot, a2 / tot


def _outproj_kernel(a_ref, b_ref, x_ref, wa_ref, wb_ref, g_ref, be_ref, rwt_ref, rb_ref,
                    x1_ref, route_ref, *, alpha):
    d = x_ref.shape[1]
    tm = x_ref.shape[0]
    mix = _dot(a_ref[...], wa_ref[...]) + _dot(b_ref[...], wb_ref[...])
    x1 = _layer_norm(alpha * x_ref[...] + mix, g_ref[...], be_ref[...])
    x1_ref[:, 0:d] = x1
    x_hi = x1.astype(BF16)
    x_lo = (x1 - x_hi.astype(F32)).astype(BF16)
    w = rwt_ref[...]
    w_hi = w.astype(BF16)
    w_lo = (w - w_hi.astype(F32)).astype(BF16)
    z_hi = _dot_nt(jnp.concatenate([w_hi, w_lo], axis=0), x_hi)
    zt = z_hi[0:N_EXPERTS] + z_hi[N_EXPERTS:2 * N_EXPERTS] + _dot_nt(w_hi, x_lo)
    afft = 1.0 / (1.0 + jnp.exp(-zt))
    selt = afft + rb_ref[...]
    aff = [afft[e:e + 1, :] for e in range(N_EXPERTS)]
    sel = [selt[e:e + 1, :] for e in range(N_EXPERTS)]
    e1, e2, w1, w2 = _route_rows(aff, sel)
    rec = jnp.concatenate([e1, e2, w1, w2, jnp.zeros((LANE - 4, tm), F32)], axis=0)
    route_ref[...] = rec[0:8]
    x1_ref[:, d:d + LANE] = rec.T


def _outproj(a, b, xt, wa, wb, g, be, rwt, rb, alpha, tm):
    n, d = xt.shape
    full = lambda i: (0, 0)
    return pl.pallas_call(
        functools.partial(_outproj_kernel, alpha=alpha),
        out_shape=[jax.ShapeDtypeStruct((n, d + LANE), F32), jax.ShapeDtypeStruct((8, n), F32)],
        grid=(n // tm,),
        in_specs=[pl.BlockSpec((tm, a.shape[1]), lambda i: (i, 0)),
                  pl.BlockSpec((tm, b.shape[1]), lambda i: (i, 0)),
                  pl.BlockSpec((tm, d), lambda i: (i, 0)),
                  pl.BlockSpec(wa.shape, full), pl.BlockSpec(wb.shape, full),
                  pl.BlockSpec((1, d), full), pl.BlockSpec((1, d), full),
                  pl.BlockSpec(rwt.shape, full), pl.BlockSpec(rb.shape, full)],
        out_specs=[pl.BlockSpec((tm, d + LANE), lambda i: (i, 0)), pl.BlockSpec((8, tm), lambda i: (0, i))],
        compiler_params=_cparams(("parallel",)),
        name="outproj_ln_route",
    )(a, b, xt, wa, wb, g, be, rwt, rb)


def _moe_kernel(bea_ref, beb_ref, bbase_ref, nact_ref, src_ref,
                x_hbm, wga_ref, wgb_ref, wua_ref, wub_ref, wda_ref, wdb_ref, o_ref, xbuf0, xbuf1, sem, *, tm, d):
    i = pl.program_id(0)
    nact = nact_ref[0]
    last = src_ref.shape[0] - 1
    bufs = (xbuf0, xbuf1)

    def issue(blk, slot):
        base = bbase_ref[blk]
        for r in range(tm):
            tok = src_ref[jnp.minimum(base + r, last)]
            pltpu.make_async_copy(x_hbm.at[pl.ds(tok, 1), :], bufs[slot].at[pl.ds(r, 1), :],
                                  sem.at[slot]).start(priority=r % 2)

    def wait(slot):
        pltpu.make_async_copy(x_hbm.at[pl.ds(0, tm), :], bufs[slot], sem.at[slot]).wait()

    @pl.when((i == 0) & (nact > 0))
    def _():
        issue(0, 0)

    def step(slot):
        wait(slot)

        @pl.when(i + 1 < nact)
        def _():
            issue(i + 1, 1 - slot)

        buf = bufs[slot]
        xb = buf[:, 0:d].astype(BF16)
        rec = buf[:, d:d + LANE]
        e1, e2, w1, w2 = rec[:, 0:1], rec[:, 1:2], rec[:, 2:3], rec[:, 3:4]
        out = None
        for e_ref, wg_ref, wu_ref, wd_ref in ((bea_ref, wga_ref, wua_ref, wda_ref),
                                              (beb_ref, wgb_ref, wub_ref, wdb_ref)):
            expert = e_ref[i].astype(F32)
            gate_w = jnp.where(e1 == expert, w1, jnp.where(e2 == expert, w2, 0.0))
            gate = _dot(xb, wg_ref[...])
            up = _dot(xb, wu_ref[...])
            h = gate * (1.0 / (1.0 + jnp.exp(-gate))) * up * gate_w
            part = _dot(h.astype(BF16), wd_ref[...])
            out = part if out is None else out + part
        o_ref[...] = out

    for slot in range(2):
        pl.when((i < nact) & (i % 2 == slot))(functools.partial(step, slot))

    @pl.when(i >= nact)
    def _():
        o_ref[...] = jnp.zeros_like(o_ref)


def _moe_experts(x1e, wg, wu, wd, bea, beb, bbase, nact, src, tm):
    n, de = x1e.shape
    d = de - LANE
    nblk = bea.shape[0]
    f = wg.shape[2]
    amap = lambda i, bea, beb, *_: (bea[i], 0, 0)
    bmap = lambda i, bea, beb, *_: (beb[i], 0, 0)
    grid_spec = pltpu.PrefetchScalarGridSpec(
        num_scalar_prefetch=5,
        grid=(nblk,),
        in_specs=[pl.BlockSpec(memory_space=pl.ANY),
                  pl.BlockSpec((None, d, f), amap), pl.BlockSpec((None, d, f), bmap),
                  pl.BlockSpec((None, d, f), amap), pl.BlockSpec((None, d, f), bmap),
                  pl.BlockSpec((None, f, d), amap), pl.BlockSpec((None, f, d), bmap)],
        out_specs=pl.BlockSpec((tm, d), lambda i, *_: (i, 0)),
        scratch_shapes=[pltpu.VMEM((tm, de), F32), pltpu.VMEM((tm, de), F32), pltpu.SemaphoreType.DMA((2,))],
    )
    return pl.pallas_call(
        functools.partial(_moe_kernel, tm=tm, d=d),
        out_shape=jax.ShapeDtypeStruct((nblk * tm, d), F32),
        grid_spec=grid_spec,
        compiler_params=_cparams(("arbitrary",)),
        name="moe_experts",
    )(bea, beb, bbase, nact, src, x1e, wg, wg, wu, wu, wd, wd)


def _combine_kernel(p_ref, x_ref, g_ref, be_ref, o_hbm, y_ref, obuf0, obuf1, sem, *, alpha, tm, d):
    i = pl.program_id(0)
    nsteps = pl.num_programs(0)

    bufs = (obuf0, obuf1)

    def issue(blk, slot):
        base = blk * tm
        for r in range(tm):
            pltpu.make_async_copy(o_hbm.at[pl.ds(p_ref[base + r], 1), :],
                                  bufs[slot].at[pl.ds(r, 1), :], sem.at[slot]).start(priority=r % 2)

    def wait(slot):
        pltpu.make_async_copy(o_hbm.at[pl.ds(0, tm), :], bufs[slot], sem.at[slot]).wait()

    @pl.when(i == 0)
    def _():
        issue(0, 0)

    def step(slot):
        wait(slot)

        @pl.when(i + 1 < nsteps)
        def _():
            issue(i + 1, 1 - slot)

        y = alpha * x_ref[:, 0:d] + bufs[slot][...]
        y_ref[...] = _layer_norm(y, g_ref[...], be_ref[...])

    for slot in range(2):
        pl.when(i % 2 == slot)(functools.partial(step, slot))


def _combine(x1e, g, be, o, pos, alpha, tm):
    n, de = x1e.shape
    d = de - LANE
    grid_spec = pltpu.PrefetchScalarGridSpec(
        num_scalar_prefetch=1,
        grid=(n // tm,),
        in_specs=[pl.BlockSpec((tm, de), lambda i, *_: (i, 0)),
                  pl.BlockSpec((1, d), lambda i, *_: (0, 0)),
                  pl.BlockSpec((1, d), lambda i, *_: (0, 0)),
                  pl.BlockSpec(memory_space=pl.ANY)],
        out_specs=pl.BlockSpec((tm, d), lambda i, *_: (i, 0)),
        scratch_shapes=[pltpu.VMEM((tm, d), F32), pltpu.VMEM((tm, d), F32), pltpu.SemaphoreType.DMA((2,))],
    )
    return pl.pallas_call(
        functools.partial(_combine_kernel, alpha=alpha, tm=tm, d=d),
        out_shape=jax.ShapeDtypeStruct((n, d), F32),
        grid_spec=grid_spec,
        compiler_params=_cparams(("arbitrary",)),
        name="moe_combine_ln",
    )(pos, x1e, g, be, o)


def _moe_layer(x1e, route_t, wg, wu, wd, g, be, alpha, tm_e, tm_c):
    n = x1e.shape[0]
    pairs = [(a, b) for a in range(EXPERTS_PER_GROUP) for b in range(a + 1, EXPERTS_PER_GROUP)]
    nseg = N_GROUPS * len(pairs)
    seg_a = jnp.array([g * EXPERTS_PER_GROUP + a for g in range(N_GROUPS) for a, b in pairs], I32)
    seg_b = jnp.array([g * EXPERTS_PER_GROUP + b for g in range(N_GROUPS) for a, b in pairs], I32)
    e1, e2 = route_t[0].astype(I32), route_t[1].astype(I32)
    lo, hi = jnp.minimum(e1, e2), jnp.maximum(e1, e2)
    la, lb = lo % EXPERTS_PER_GROUP, hi % EXPERTS_PER_GROUP
    pair_id = la * (2 * EXPERTS_PER_GROUP - 1 - la) // 2 + (lb - la - 1)
    seg = (lo // EXPERTS_PER_GROUP) * len(pairs) + pair_id
    onehot = (seg[:, None] == jnp.arange(nseg, dtype=I32)[None, :]).astype(I32)
    rank = jnp.take_along_axis(jnp.cumsum(onehot, axis=0) - onehot, seg[:, None], axis=1)[:, 0]
    counts = jnp.sum(onehot, axis=0)
    nblk_s = (counts + tm_e - 1) // tm_e
    blk_start = jnp.cumsum(nblk_s) - nblk_s
    seg_start = jnp.cumsum(counts) - counts
    pos = (blk_start[seg] * tm_e + rank).astype(I32)
    src = jnp.argsort(seg, stable=True).astype(I32)
    nblk = n // tm_e + nseg
    blk = jnp.arange(nblk, dtype=I32)
    blk_end = jnp.cumsum(nblk_s)
    bseg = jnp.minimum(jnp.sum((blk[:, None] >= blk_end[None, :]).astype(I32), axis=1), nseg - 1)
    bbase = (seg_start[bseg] + (blk - blk_start[bseg]) * tm_e).astype(I32)
    nact = jnp.sum(nblk_s).astype(I32).reshape(1)
    o = _moe_experts(x1e, wg, wu, wd, seg_a[bseg], seg_b[bseg], bbase, nact, src, tm_e)
    return _combine(x1e, g, be, o, pos, alpha, tm_c)


def _rms(x, g):
    return x * lax.rsqrt(jnp.mean(x * x, axis=-1, keepdims=True) + RMS_EPS) * g


def _odd_proj_kernel(x_ref, w_ref, qng_ref, kvng_ref, wqa_ref, wqb_ref, wk_ref, wv_ref, cos_ref, sin_ref,
                     tq_ref, tk_ref,
                     qc_ref, kc_ref, vc_ref, qi_ref, ki_ref, wi_ref, q8_ref, k8_ref, v8_ref, *, q_scale):
    o = ODD_OFF
    tm = x_ref.shape[0]
    lo = _lane_lo((tm, LANE))
    h = _dot(x_ref[...].astype(BF16), w_ref[...])
    kdup = h[:, o["kc"]:o["vc"]]
    vdup = h[:, o["vc"]:o["qi"]]
    kc_ref[:, 0:LANE] = jnp.where(lo, kdup, tk_ref[:, 0:LANE].astype(F32)).astype(BF16)
    kc_ref[:, LANE:2 * LANE] = jnp.where(lo, tk_ref[:, LANE:2 * LANE].astype(F32), kdup).astype(BF16)
    vc_ref[:, 0:LANE] = jnp.where(lo, vdup, 1.0).astype(BF16)
    vc_ref[:, LANE:2 * LANE] = jnp.where(lo, 1.0, vdup).astype(BF16)
    qi_ref[...] = h[:, o["qi"]:o["ki"]].astype(BF16)
    ki_ref[...] = h[:, o["ki"]:o["wi"]].astype(BF16)
    wi_ref[...] = h[:, o["wi"]:o["wi"] + IDX_HEADS]
    cos = cos_ref[...]
    sin = sin_ref[...]
    cqn = _rms(h[:, o["cq"]:o["ckv"]], qng_ref[...]).astype(BF16)
    qa = _dot(cqn, wqa_ref[...])
    qb = _dot(cqn, wqb_ref[...])
    ckvn = _rms(h[:, o["ckv"]:o["kra"]], kvng_ref[...]).astype(BF16)
    kn = _dot(ckvn, wk_ref[...])
    v8 = _dot(ckvn, wv_ref[...])
    kr = h[:, o["kra"]:o["krb"]] * cos + h[:, o["krb"]:o["end"]] * sin
    for hd in range(D_HEADS):
        sl = slice(hd * LANE, (hd + 1) * LANE)
        pj = slice((hd // 2) * LANE, (hd // 2 + 1) * LANE)
        own = lo if hd % 2 == 0 else ~lo
        qc_ref[:, sl] = jnp.where(own, h[:, pj], tq_ref[:, sl].astype(F32)).astype(BF16)
        q8_ref[:, sl] = ((qa[:, sl] * cos + qb[:, sl] * sin) * q_scale).astype(BF16)
        k8_ref[:, sl] = (kn[:, sl] + kr).astype(BF16)
        v8_ref[:, sl] = jnp.where(own, v8[:, pj], 1.0).astype(BF16)


def _odd_proj(xt, w_cat, qng, kvng, wqa, wqb, wk, wv, cos_t, sin_t, tq_t, tk_t, seq, tm):
    n, d = xt.shape
    full = lambda i: (0, 0)
    tps = seq // tm
    pos = lambda i: (i % tps, 0)
    widths = (C_HEADS * LANE, 2 * LANE, 2 * LANE, IDX_Q, LANE)
    out_shape = [jax.ShapeDtypeStruct((n, w), BF16) for w in widths]
    out_specs = [pl.BlockSpec((tm, w), lambda i: (i, 0)) for w in widths]
    out_shape.append(jax.ShapeDtypeStruct((n, IDX_HEADS), F32))
    out_specs.append(pl.BlockSpec((tm, IDX_HEADS), lambda i: (i, 0)))
    for w in (D_HEADS * LANE,) * 3:
        out_shape.append(jax.ShapeDtypeStruct((n, w), BF16))
        out_specs.append(pl.BlockSpec((tm, w), lambda i: (i, 0)))
    return pl.pallas_call(
        functools.partial(_odd_proj_kernel, q_scale=(MLA_NOPE + MLA_ROPE) ** -0.5 * LOG2E),
        out_shape=out_shape,
        grid=(n // tm,),
        in_specs=[pl.BlockSpec((tm, d), lambda i: (i, 0)),
                  pl.BlockSpec(w_cat.shape, full),
                  pl.BlockSpec(qng.shape, full), pl.BlockSpec(kvng.shape, full),
                  pl.BlockSpec(wqa.shape, full), pl.BlockSpec(wqb.shape, full),
                  pl.BlockSpec(wk.shape, full), pl.BlockSpec(wv.shape, full),
                  pl.BlockSpec((tm, LANE), pos), pl.BlockSpec((tm, LANE), pos),
                  pl.BlockSpec((tm, C_HEADS * LANE), pos), pl.BlockSpec((tm, 2 * LANE), pos)],
        out_specs=out_specs,
        compiler_params=_cparams(("parallel",)),
        name="odd_proj",
    )(xt, w_cat, qng, kvng, wqa, wqb, wk, wv, cos_t, sin_t, tq_t, tk_t)


def _dsa_kernel(q_ref, k_ref, v_ref, qi_ref, ki_ref, wt_ref, o_ref, sc_ref, mb_ref, *, top_k, chunk, qb):
    i = pl.program_id(1)
    nc = (i * qb + qb + chunk - 1) // chunk
    lo = _lane_lo((qb, LANE))
    s_row2 = lax.broadcasted_iota(I32, (2 * chunk, qb), 0)
    t_lane2 = i * qb + lax.broadcasted_iota(I32, (2 * chunk, qb), 1)

    qidx = []
    for h in range(IDX_HEADS):
        qp = qi_ref[:, (h // 2) * LANE:(h // 2 + 1) * LANE]
        qidx.append(jnp.where(lo if h % 2 == 0 else ~lo, qp, jnp.zeros_like(qp)))
    q_stack = jnp.concatenate(qidx, axis=0)
    w = wt_ref[...]

    npair = (nc + 1) // 2

    def pair_start(cp):
        return cp * (2 * chunk)

    def select(pairs):
        def score_pair(cp):
            start = pair_start(cp)
            d = _dot_nt(ki_ref[pl.ds(start, 2 * chunk), :], q_stack)
            sc = w[0:1, :] * jnp.maximum(d[:, 0:qb], 0.0)
            for h in range(1, IDX_HEADS):
                sc = sc + w[h:h + 1, :] * jnp.maximum(d[:, h * qb:(h + 1) * qb], 0.0)
            sc = jnp.where(sc == 0.0, 0.0, sc)
            sc_ref[pl.ds(start, 2 * chunk), :] = jnp.where(start + s_row2 <= t_lane2, sc, NEG_INF)

        for cp in range(pairs):
            score_pair(cp)

        nacc = 4 * 8

        def count(trial, strict):
            acc = jnp.zeros((nacc, qb), F32)
            for cp in range(pairs):
                sc = sc_ref[pl.ds(pair_start(cp), 2 * chunk), :]
                hit = (sc > trial) if strict else (sc >= trial)
                acc = acc + jnp.sum(jnp.where(hit, 1.0, 0.0).reshape(2 * chunk // nacc, nacc, qb), axis=0)
            return jnp.sum(acc, axis=0, keepdims=True)

        def decode(key):
            return lax.bitcast_convert_type(jnp.where(key < 0, key ^ jnp.int32(0x7FFFFFFF), key), F32)

        kf = float(top_k)
        key_neg_inf = jnp.int32(-2139095041)
        cand = jnp.where(count(jnp.zeros((1, qb), F32), False) >= kf, 0, jnp.int32(-2 ** 31)).astype(I32)

        def bit_step(j, cand):
            trial = cand + lax.shift_left(jnp.int32(1), 30 - j)
            ok = (trial <= key_neg_inf) | (count(decode(trial), False) >= kf)
            return jnp.where(ok, trial, cand)

        thr = decode(lax.fori_loop(0, 31, bit_step, cand))
        need = kf - count(thr, True)

        r0 = lax.broadcasted_iota(I32, (chunk, chunk), 0)
        r1 = lax.broadcasted_iota(I32, (chunk, chunk), 1)
        below = jnp.where(r1 < r0, 1.0, 0.0).astype(BF16)
        need = jnp.where(thr == NEG_INF, 0.0, need)

        offset = jnp.zeros((1, qb), F32)
        for cp in range(pairs):
            start = pair_start(cp)
            sc = sc_ref[pl.ds(start, 2 * chunk), :]
            eq = sc == thr
            eqf = jnp.where(eq, 1.0, 0.0)
            eqb = eqf.astype(BF16)
            first = jnp.sum(eqf[0:chunk], axis=0, keepdims=True)
            prefix = jnp.concatenate([_dot(below, eqb[0:chunk]) + offset,
                                      _dot(below, eqb[chunk:2 * chunk]) + (offset + first)], axis=0)
            bias = jnp.where(sc > thr, 0.0, jnp.where(eq, jnp.where(prefix < need, 0.0, NEG_INF), NEG_INF))
            mb_ref[:, pl.ds(start, 2 * chunk)] = bias.T
            offset = offset + first + jnp.sum(eqf[chunk:2 * chunk], axis=0, keepdims=True)

    for pairs in range(1, k_ref.shape[0] // (2 * chunk) + 1):
        pl.when(npair == pairs)(functools.partial(select, pairs))

    nslab = qb // BLOCK
    chains = [(h, r) for h in range(C_HEADS) for r in range(nslab)]

    def att_step(start, width, carry):
        kp = [k_ref[pl.ds(start, width), par * LANE:(par + 1) * LANE] for par in range(2)]
        vp = [v_ref[pl.ds(start, width), par * LANE:(par + 1) * LANE] for par in range(2)]
        mb = [mb_ref[r * BLOCK:(r + 1) * BLOCK, pl.ds(start, width)].astype(BF16) for r in range(nslab)]
        qs = [q_ref[r * BLOCK:(r + 1) * BLOCK, h * LANE:(h + 1) * LANE] for h, r in chains]
        return _online_softmax_chains(qs, [kp[h % 2] for h, r in chains], [vp[h % 2] for h, r in chains], carry,
                                      biases=[mb[r] for h, r in chains], guard_empty=True)

    init = tuple((jnp.full((BLOCK, 1), NEG_INF, F32), jnp.zeros((BLOCK, LANE), F32)) for _ in chains)
    carry = lax.fori_loop(
        0, nc // 2, lambda cp, c: att_step(pl.multiple_of(cp * 2 * chunk, 2 * chunk), 2 * chunk, c), init)
    carry = lax.fori_loop(
        0, nc % 2, lambda _, c: att_step(pl.multiple_of((nc // 2) * 2 * chunk, chunk), chunk, c), carry)
    accs = [jnp.concatenate([carry[h * nslab + r][1] for r in range(nslab)], axis=0) for h in range(C_HEADS)]
    _normalize_pairs(accs, o_ref)


def _dsa(qc, kc, vc, qi, ki, wit):
    b, s, _ = qc.shape
    top_k = min(DSA_TOPK_MAX, s // 4)
    chunk = 256
    qb = min(2 * BLOCK, s)
    assert s % (2 * chunk) == 0 and top_k <= chunk and chunk % qb == 0
    blk = lambda bi, i: (bi, i, 0)
    seq = lambda bi, i: (bi, 0, 0)
    return pl.pallas_call(
        functools.partial(_dsa_kernel, top_k=top_k, chunk=chunk, qb=qb),
        out_shape=jax.ShapeDtypeStruct((b, s, C_HEADS * HEAD_DIM), BF16),
        grid=(b, s // qb),
        in_specs=[pl.BlockSpec((None, qb, C_HEADS * LANE), blk),
                  pl.BlockSpec((None, s, 2 * LANE), seq),
                  pl.BlockSpec((None, s, 2 * LANE), seq),
                  pl.BlockSpec((None, qb, IDX_Q), blk),
                  pl.BlockSpec((None, s, LANE), seq),
                  pl.BlockSpec((None, IDX_HEADS, qb), lambda bi, i: (bi, 0, i))],
        out_specs=pl.BlockSpec((None, qb, C_Q), blk),
        scratch_shapes=[pltpu.VMEM((s, qb), F32), pltpu.VMEM((qb, s), F32)],
        compiler_params=_cparams(("parallel", "arbitrary")),
        name="dsa_attn",
    )(qc, kc, vc, qi, ki, wit)


def _dup64(w):
    d, c = w.shape
    w = w.reshape(d, c // HEAD_DIM, 1, HEAD_DIM)
    return jnp.broadcast_to(w, (d, c // HEAD_DIM, 2, HEAD_DIM)).reshape(d, 2 * c)


def _pad_cols(w, width):
    return jnp.pad(w, ((0, 0), (0, width - w.shape[1])))


def _split_cols(w, sizes):
    out, start = [], 0
    for n in sizes:
        out.append(w[:, start:start + n])
        start += n
    return out


def _rope_partner(w):
    half = MLA_ROPE // 2
    return jnp.concatenate([-w[..., half:], w[..., :half]], axis=-1)


def _rope_block(w_rope):
    d = w_rope.shape[0]
    return jnp.concatenate([jnp.zeros((d, MLA_NOPE), F32), w_rope,
                            jnp.zeros((d, LANE - MLA_NOPE - MLA_ROPE), F32)], axis=1)


def _alibi_tables(s):
    slopes = 2.0 ** (-8.0 * jnp.arange(1, C_HEADS + 1, dtype=F32) / C_HEADS)
    a = slopes * LOG2E
    pos = jnp.arange(s, dtype=F32)
    a_sp = _split3(a)
    r_sp = _split3(-(a[None, :] * pos[:, None]))
    ones = jnp.ones((s, C_HEADS), F32)
    q_terms = [128.0 * t * ones for t in a_sp] + [t * ones for t in a_sp] + list(r_sp)
    q_aug = jnp.stack(q_terms, axis=-1)
    q_aug = jnp.pad(q_aug, ((0, 0), (0, 0), (0, HEAD_DIM - q_aug.shape[-1])))
    zeros = jnp.zeros_like(q_aug)
    even = jnp.concatenate([zeros, q_aug], axis=-1)
    odd = jnp.concatenate([q_aug, zeros], axis=-1)
    parity = (jnp.arange(C_HEADS) % 2 == 0)[None, :, None]
    tq_t = jnp.where(parity, even, odd).reshape(s, C_HEADS * LANE)
    s_hi = jnp.floor(pos / LANE)
    s_lo = pos - LANE * s_hi
    k_aug = jnp.stack([s_hi] * 3 + [s_lo] * 3 + [jnp.ones_like(pos)] * 3, axis=-1)
    k_aug = jnp.pad(k_aug, ((0, 0), (0, HEAD_DIM - k_aug.shape[-1])))
    kz = jnp.zeros_like(k_aug)
    tk_t = jnp.concatenate([kz, k_aug, k_aug, kz], axis=-1)
    return tq_t.astype(BF16), tk_t.astype(BF16)


def kernel(x, even_w_in, even_b_f, even_sinks, odd_w_in, odd_q_norm, odd_kv_norm, odd_w_uq, odd_w_ukv,
           w_o, ln_g, ln_b, router_w, router_b, moe_w_gate, moe_w_up, moe_w_down):
    b, s, d = x.shape
    n = b * s
    depth = w_o.shape[0]
    alpha = (2 * depth) ** 0.25
    tm = min(512, s)
    tq = min(256, s)
    tm_e = min(256, n)
    tm_c = min(512, n)
    tm_o = min(1024, n)
    att_scale = HEAD_DIM ** -0.5
    xt = x.reshape(n, d)
    rwt = router_w.T
    rb = router_b.reshape(N_EXPERTS, 1)

    for l in range(depth):
        j = l // 2
        if l % 2 == 0:
            qa, ka, va, qb, kb, vb, fg = _split_cols(
                even_w_in[j], (A_Q, A_KV_HEADS * HEAD_DIM, A_KV_HEADS * HEAD_DIM, B_W, B_W, B_W, B_HEADS))
            w_cat = jnp.concatenate([qa * (att_scale * LOG2E), _dup64(ka), _dup64(va), qb * (att_scale * LOG2E), kb, vb,
                                     _pad_cols(fg, LANE)], axis=1).astype(BF16)
            bf_pad = _pad_cols(even_b_f[j].reshape(1, B_HEADS), LANE)
            qa_, ka_, va_, qf, kf, vf = _even_proj(xt, w_cat, bf_pad, s, tm)
            r3 = lambda a: a.reshape(b, s, a.shape[1])
            out_a = _swa(even_sinks[j], r3(qa_), r3(ka_), r3(va_))
            out_b = _flash(r3(qf), r3(kf), r3(vf), tq=tq, nh=B_HEADS, name="fox_attn")
        else:
            (qc, kc, vc, qi, ki, wi, cq, ckv, kr) = _split_cols(
                odd_w_in[j], (C_Q, HEAD_DIM, HEAD_DIM, IDX_Q, IDX_DIM, IDX_HEADS,
                              MLA_Q_RANK, MLA_KV_RANK, MLA_ROPE))
            w_cat = jnp.concatenate([
                qc * (att_scale * LOG2E), _dup64(kc), _dup64(vc), qi * (IDX_DIM ** -0.5), _dup64(ki),
                _pad_cols(wi * (IDX_HEADS ** -0.5), LANE), cq, ckv,
                _rope_block(kr), _rope_block(_rope_partner(kr))], axis=1).astype(BF16)
            dq = MLA_NOPE + MLA_ROPE
            wuq = odd_w_uq[j].reshape(MLA_Q_RANK, D_HEADS, dq)
            pad_q = jnp.zeros((MLA_Q_RANK, D_HEADS, LANE - dq), F32)
            wqa = jnp.concatenate([wuq, pad_q], axis=2).reshape(MLA_Q_RANK, D_HEADS * LANE).astype(BF16)
            wqb = jnp.concatenate([jnp.zeros((MLA_Q_RANK, D_HEADS, MLA_NOPE), F32),
                                   _rope_partner(wuq[..., MLA_NOPE:]), pad_q],
                                  axis=2).reshape(MLA_Q_RANK, D_HEADS * LANE).astype(BF16)
            wukv = odd_w_ukv[j].reshape(MLA_KV_RANK, D_HEADS, MLA_NOPE + MLA_V)
            wk = jnp.concatenate([wukv[..., :MLA_NOPE], jnp.zeros((MLA_KV_RANK, D_HEADS, LANE - MLA_NOPE), F32)],
                                 axis=2).reshape(MLA_KV_RANK, D_HEADS * LANE).astype(BF16)
            wv = wukv[..., MLA_NOPE:].reshape(MLA_KV_RANK, D_HEADS * MLA_V).astype(BF16)
            half = MLA_ROPE // 2
            inv = ROPE_THETA ** (-jnp.arange(half, dtype=F32) / half)
            ang = jnp.arange(s, dtype=F32)[:, None] * inv[None, :]
            cos_h, sin_h = jnp.cos(ang), jnp.sin(ang)
            pad_t = jnp.zeros((s, LANE - dq), F32)
            cos_t = jnp.concatenate([jnp.ones((s, MLA_NOPE), F32), cos_h, cos_h, pad_t], axis=1)
            sin_t = jnp.concatenate([jnp.zeros((s, MLA_NOPE), F32), sin_h, sin_h, pad_t], axis=1)
            tq_t, tk_t = _alibi_tables(s)
            qc_, kc_, vc_, qi_, ki_, wi_, q8, k8, v8 = _odd_proj(
                xt, w_cat, odd_q_norm[j].reshape(1, -1), odd_kv_norm[j].reshape(1, -1),
                wqa, wqb, wk, wv, cos_t, sin_t, tq_t, tk_t, s, tm)
            r3 = lambda a: a.reshape(b, s, a.shape[1])
            out_a = _dsa(r3(qc_), r3(kc_), r3(vc_), r3(qi_), r3(ki_), r3(wi_).transpose(0, 2, 1))
            out_b = _flash(r3(q8), r3(k8), r3(v8), tq=tq, nh=D_HEADS, name="mla_attn")
        half_w = out_a.shape[2]
        wo = w_o[l].astype(BF16)
        x1e, route_t = _outproj(out_a.reshape(n, half_w), out_b.reshape(n, -1), xt,
                                wo[:half_w], wo[half_w:], ln_g[l, 0].reshape(1, d), ln_b[l, 0].reshape(1, d),
                                rwt, rb, alpha, tm_o)
        xt = _moe_layer(x1e, route_t, moe_w_gate[l].astype(BF16), moe_w_up[l].astype(BF16),
                        moe_w_down[l].astype(BF16), ln_g[l, 1].reshape(1, d), ln_b[l, 1].reshape(1, d),
                        alpha, tm_e, tm_c)
    return xt.reshape(b, s, d)
```

```python
import functools
import math

import jax
import jax.numpy as jnp
from jax import lax
from jax.experimental import pallas as pl
from jax.experimental.pallas import tpu as pltpu

F32 = jnp.float32
BF16 = jnp.bfloat16
I32 = jnp.int32

LANE = 128
HEAD_DIM = 64
BLOCK = 128
A_HEADS, A_KV_HEADS, WINDOW = 8, 2, 128
B_HEADS = 8
C_HEADS, IDX_HEADS, IDX_DIM, DSA_TOPK_MAX = 8, 4, 64, 256
D_HEADS, MLA_Q_RANK, MLA_KV_RANK, MLA_NOPE, MLA_ROPE, MLA_V = 8, 256, 128, 64, 32, 64
ROPE_THETA = 10000.0
N_EXPERTS, N_GROUPS = 16, 4
EXPERTS_PER_GROUP = N_EXPERTS // N_GROUPS
LN_EPS, RMS_EPS = 1e-5, 1e-6
NEG_INF = float("-inf")
LOG2E = math.log2(math.e)

A_Q = A_HEADS * HEAD_DIM
A_KV = A_KV_HEADS * LANE
B_W = B_HEADS * HEAD_DIM
C_Q = C_HEADS * HEAD_DIM
IDX_Q = IDX_HEADS * IDX_DIM


def _offsets(**widths):
    off, start = {}, 0
    for name, w in widths.items():
        off[name] = start
        start += w
    off["end"] = start
    return off


EVEN_OFF = _offsets(qa=A_Q, ka=A_KV, va=A_KV, qb=B_W, kb=B_W, vb=B_W, fg=LANE)
ODD_OFF = _offsets(qc=C_Q, kc=LANE, vc=LANE, qi=IDX_Q, ki=LANE, wi=LANE, cq=MLA_Q_RANK, ckv=MLA_KV_RANK,
                   kra=LANE, krb=LANE)
VMEM_LIMIT = 56 * 1024 * 1024


def _cparams(sem):
    return pltpu.CompilerParams(dimension_semantics=sem, vmem_limit_bytes=VMEM_LIMIT)


def _dot(a, b):
    return jnp.dot(a, b, preferred_element_type=F32)


def _dot_nt(a, b):
    return lax.dot_general(a, b, (((1,), (1,)), ((), ())), preferred_element_type=F32)


def _lane_lo(shape):
    return lax.broadcasted_iota(I32, shape, len(shape) - 1) % LANE < HEAD_DIM


def _layer_norm(y, g, b):
    mu = jnp.mean(y, axis=-1, keepdims=True)
    yc = y - mu
    var = jnp.mean(yc * yc, axis=-1, keepdims=True)
    return yc * lax.rsqrt(var + LN_EPS) * g + b


def _top16(v):
    bits = lax.bitcast_convert_type(v, I32) & jnp.int32(-65536)
    return lax.bitcast_convert_type(bits, F32)


def _split3(x):
    s1 = _top16(x)
    r = x - s1
    s2 = _top16(r)
    return s1, s2, r - s2


def _aug_rel(shape, h):
    lane = lax.broadcasted_iota(I32, shape, 1)
    return lane - (HEAD_DIM if h % 2 == 0 else 0)


def _even_proj_kernel(x_ref, w_ref, bf_ref, qa_ref, ka_ref, va_ref, qf_ref, kf_ref, vf_ref,
                      carry_ref, *, tiles_per_seq):
    t = pl.program_id(0)
    tm = x_ref.shape[0]
    h = _dot(x_ref[...].astype(BF16), w_ref[...])
    o = EVEN_OFF
    qa_ref[...] = h[:, o["qa"]:o["ka"]].astype(BF16)
    ka_ref[...] = h[:, o["ka"]:o["va"]].astype(BF16)
    va_ref[...] = h[:, o["va"]:o["qb"]].astype(BF16)
    z = h[:, o["fg"]:o["end"]] + bf_ref[...]
    logf = jnp.minimum(z, 0.0) - jnp.log1p(jnp.exp(-jnp.abs(z)))

    @pl.when(t % tiles_per_seq == 0)
    def _():
        carry_ref[...] = jnp.zeros_like(carry_ref)

    row = lax.broadcasted_iota(I32, (BLOCK, BLOCK), 0)
    col = lax.broadcasted_iota(I32, (BLOCK, BLOCK), 1)
    tri = jnp.where(row >= col, 1.0, 0.0).astype(BF16)
    parts = [p.astype(BF16) for p in _split3(logf)]
    run = carry_ref[...]
    blocks = []
    for blk in range(tm // BLOCK):
        cb = run
        for part in parts:
            cb = cb + _dot(tri, part[blk * BLOCK:(blk + 1) * BLOCK])
        blocks.append(cb)
        run = cb[BLOCK - 1:BLOCK, :]
    c = jnp.concatenate(blocks, axis=0)
    carry_ref[...] = run

    s1, s2, s3 = _split3(c * LOG2E)
    lo = _lane_lo((tm, LANE))
    for hd in range(B_HEADS):
        own = lo if hd % 2 == 0 else ~lo
        rel = _aug_rel((tm, LANE), hd)
        b1, b2, b3 = s1[:, hd:hd + 1], s2[:, hd:hd + 1], s3[:, hd:hd + 1]
        q_aug = jnp.where(rel == 3, b1, jnp.where(rel == 4, b2, jnp.where(rel == 5, b3,
                          jnp.where((rel >= 0) & (rel < 3), -1.0, 0.0))))
        k_aug = jnp.where(rel == 0, b1, jnp.where(rel == 1, b2, jnp.where(rel == 2, b3,
                          jnp.where((rel >= 3) & (rel < 6), 1.0, 0.0))))
        pj = (hd // 2) * LANE
        sl = slice(hd * LANE, (hd + 1) * LANE)
        qf_ref[:, sl] = jnp.where(own, h[:, o["qb"] + pj:o["qb"] + pj + LANE], q_aug).astype(BF16)
        kf_ref[:, sl] = jnp.where(own, h[:, o["kb"] + pj:o["kb"] + pj + LANE], k_aug).astype(BF16)
        vf_ref[:, sl] = jnp.where(own, h[:, o["vb"] + pj:o["vb"] + pj + LANE], 1.0).astype(BF16)


def _even_proj(xt, w_cat, bf_pad, seq, tm):
    n, d = xt.shape
    wcols = w_cat.shape[1]
    widths = (A_Q, A_KV, A_KV, B_HEADS * LANE, B_HEADS * LANE, B_HEADS * LANE)
    out_shape = [jax.ShapeDtypeStruct((n, w), BF16) for w in widths]
    out_specs = [pl.BlockSpec((tm, w), lambda i: (i, 0)) for w in widths]
    return pl.pallas_call(
        functools.partial(_even_proj_kernel, tiles_per_seq=seq // tm),
        out_shape=out_shape,
        grid=(n // tm,),
        in_specs=[pl.BlockSpec((tm, d), lambda i: (i, 0)),
                  pl.BlockSpec((d, wcols), lambda i: (0, 0)),
                  pl.BlockSpec((1, LANE), lambda i: (0, 0))],
        out_specs=out_specs,
        scratch_shapes=[pltpu.VMEM((1, LANE), F32)],
        compiler_params=_cparams(("arbitrary",)),
        name="even_proj",
    )(xt, w_cat, bf_pad)


def _swa_kernel(sink_ref, q_ref, kp_ref, kc_ref, vp_ref, vc_ref, o_ref, *, nbq):
    i = pl.program_id(1)
    group = A_HEADS // A_KV_HEADS
    tl = lax.broadcasted_iota(I32, (BLOCK, 2 * BLOCK), 0)
    sl = lax.broadcasted_iota(I32, (BLOCK, 2 * BLOCK), 1)
    dist = BLOCK + tl - sl
    in_window = (dist >= 0) & (dist < WINDOW)
    distf = dist.astype(F32)
    lo = _lane_lo((BLOCK, LANE))

    def band(prev_ref, cur_ref, j, g):
        lanes = slice(g * LANE, (g + 1) * LANE)
        before = prev_ref[:, lanes] if j == 0 else cur_ref[(j - 1) * BLOCK:j * BLOCK, lanes]
        return jnp.concatenate([before, cur_ref[j * BLOCK:(j + 1) * BLOCK, lanes]], axis=0)

    stacked = {}
    for j in range(nbq):
        for g in range(A_KV_HEADS):
            qs = []
            for h in range(g * group, (g + 1) * group):
                qp = q_ref[j * BLOCK:(j + 1) * BLOCK, (h // 2) * LANE:(h // 2 + 1) * LANE]
                qs.append(jnp.where(lo if h % 2 == 0 else ~lo, qp, jnp.zeros_like(qp)))
            stacked[j, g] = _dot_nt(jnp.concatenate(qs, axis=0), band(kp_ref, kc_ref, j, g))
    ps, denoms = {}, {}
    for j in range(nbq):
        valid = (in_window & ((sl >= BLOCK) | (i > 0))) if j == 0 else in_window
        for h in range(A_HEADS):
            r = h % group
            slope = 2.0 ** (-8.0 * (h + 1) / A_HEADS) * LOG2E
            logits = jnp.where(valid, stacked[j, h // group][r * BLOCK:(r + 1) * BLOCK] - slope * distf, NEG_INF)
            sink = sink_ref[h] * LOG2E
            m = jnp.maximum(jnp.max(logits, axis=-1, keepdims=True), sink)
            e = jnp.exp2(logits - m)
            denoms[j, h] = jnp.sum(e, axis=-1, keepdims=True) + jnp.exp2(sink - m)
            ps[j, h] = e.astype(BF16)
    for j in range(nbq):
        outs = []
        for g in range(A_KV_HEADS):
            heads = range(g * group, (g + 1) * group)
            pv = _dot(jnp.concatenate([ps[j, h] for h in heads], axis=0), band(vp_ref, vc_ref, j, g))
            for r, h in enumerate(heads):
                outs.append(pv[r * BLOCK:(r + 1) * BLOCK] / denoms[j, h])
        for pj in range(A_HEADS // 2):
            o_ref[j * BLOCK:(j + 1) * BLOCK, pj * LANE:(pj + 1) * LANE] = jnp.where(
                lo, outs[2 * pj], outs[2 * pj + 1]).astype(BF16)


def _swa(sinks, qa, ka, va):
    b, s, _ = qa.shape
    nbq = min(4, s // BLOCK)
    prev = lambda bi, i: (bi, jnp.maximum(i * nbq - 1, 0), 0)
    cur = lambda bi, i: (bi, i, 0)
    return pl.pallas_call(
        functools.partial(_swa_kernel, nbq=nbq),
        out_shape=jax.ShapeDtypeStruct((b, s, A_HEADS * HEAD_DIM), BF16),
        grid=(b, s // (nbq * BLOCK)),
        in_specs=[pl.BlockSpec(memory_space=pltpu.SMEM),
                  pl.BlockSpec((None, nbq * BLOCK, A_Q), cur),
                  pl.BlockSpec((None, BLOCK, A_KV), prev),
                  pl.BlockSpec((None, nbq * BLOCK, A_KV), cur),
                  pl.BlockSpec((None, BLOCK, A_KV), prev),
                  pl.BlockSpec((None, nbq * BLOCK, A_KV), cur)],
        out_specs=pl.BlockSpec((None, nbq * BLOCK, A_Q), cur),
        compiler_params=_cparams(("parallel", "arbitrary")),
        name="swa_attn",
    )(sinks, qa, ka, ka, va, va)


def _normalize_pairs(accs, o_ref):
    lo = _lane_lo(accs[0].shape)
    outs = [a / pltpu.roll(a, HEAD_DIM, 1) for a in accs]
    for j in range(len(accs) // 2):
        o_ref[:, j * LANE:(j + 1) * LANE] = jnp.where(lo, outs[2 * j], outs[2 * j + 1]).astype(BF16)


def _online_softmax_chains(qs, ks, vs, carry, biases=None, masks=None, guard_empty=False):
    n = len(qs)
    ss = [_dot_nt(qs[i], ks[i]).astype(BF16) for i in range(n)]
    if biases is not None:
        ss = [ss[i] + biases[i] for i in range(n)]
    if masks is not None:
        ss = [jnp.where(masks[i], ss[i], NEG_INF) for i in range(n)]
    ms, ps, alphas = [], [], []
    for i in range(n):
        m = carry[i][0]
        m_new = jnp.maximum(m, jnp.max(ss[i], axis=-1, keepdims=True).astype(F32))
        m_ref = jnp.where(m_new == NEG_INF, 0.0, m_new) if guard_empty else m_new
        ps.append(jnp.exp2(ss[i] - m_ref.astype(BF16)))
        alphas.append(jnp.exp2(m - m_ref))
        ms.append(m_new)
    pvs = [_dot(ps[i], vs[i]) for i in range(n)]
    return tuple((ms[i], alphas[i] * carry[i][1] + pvs[i]) for i in range(n))


def _flash_kernel(q_ref, k_ref, v_ref, o_ref, *, tq, nh):
    qi = pl.program_id(2)
    nslab = tq // BLOCK
    chains = [(hh, r) for hh in range(nh) for r in range(nslab)]
    row = lax.broadcasted_iota(I32, (BLOCK, tq), 0)
    col = lax.broadcasted_iota(I32, (BLOCK, tq), 1)

    def lanes(hh):
        return slice(hh * LANE, (hh + 1) * LANE)

    def step(start, width, carry, diagonal):
        qs = [q_ref[r * BLOCK:(r + 1) * BLOCK, lanes(hh)] for hh, r in chains]
        if diagonal:
            ks = [k_ref[pl.ds(start, (r + 1) * BLOCK), lanes(hh)] for hh, r in chains]
            vs = [v_ref[pl.ds(start, (r + 1) * BLOCK), lanes(hh)] for hh, r in chains]
            masks = [(col <= row + r * BLOCK)[:, 0:(r + 1) * BLOCK] for hh, r in chains]
        else:
            ks = [k_ref[pl.ds(start, width), lanes(hh)] for hh, r in chains]
            vs = [v_ref[pl.ds(start, width), lanes(hh)] for hh, r in chains]
            masks = None
        return _online_softmax_chains(qs, ks, vs, carry, masks=masks)

    init = tuple((jnp.full((BLOCK, 1), NEG_INF, F32), jnp.zeros((BLOCK, LANE), F32))
                 for _ in range(nh * nslab))
    npairs = qi // 2
    carry = lax.fori_loop(
        0, npairs, lambda kp, c: step(pl.multiple_of(kp * 2 * tq, 2 * tq), 2 * tq, c, False), init)
    carry = lax.fori_loop(
        0, qi % 2, lambda _, c: step(pl.multiple_of(npairs * 2 * tq, tq), tq, c, False), carry)
    carry = step(pl.multiple_of(qi * tq, tq), tq, carry, True)
    accs = [jnp.concatenate([carry[hh * nslab + r][1] for r in range(nslab)], axis=0) for hh in range(nh)]
    _normalize_pairs(accs, o_ref)


def _flash(q, k, v, *, tq, nh, name):
    b, s, w = v.shape
    return pl.pallas_call(
        functools.partial(_flash_kernel, tq=tq, nh=nh),
        out_shape=jax.ShapeDtypeStruct((b, s, w // 2), BF16),
        grid=(b, w // (nh * LANE), s // tq),
        in_specs=[pl.BlockSpec((None, tq, nh * LANE), lambda bi, j, i: (bi, i, j)),
                  pl.BlockSpec((None, s, nh * LANE), lambda bi, j, i: (bi, 0, j)),
                  pl.BlockSpec((None, s, nh * LANE), lambda bi, j, i: (bi, 0, j))],
        out_specs=pl.BlockSpec((None, tq, nh * LANE // 2), lambda bi, j, i: (bi, i, j)),
        compiler_params=_cparams(("parallel", "parallel", "arbitrary")),
        name=name,
    )(q, k, v)


def _route_rows(aff, sel):
    gs = []
    for g in range(N_GROUPS):
        r = sel[g * EXPERTS_PER_GROUP:(g + 1) * EXPERTS_PER_GROUP]
        best = None
        for a in range(EXPERTS_PER_GROUP):
            for c in range(a + 1, EXPERTS_PER_GROUP):
                pair = r[a] + r[c]
                best = pair if best is None else jnp.maximum(best, pair)
        gs.append(best)
    gmax = functools.reduce(jnp.maximum, gs)
    gidx = jnp.full_like(gmax, float(N_GROUPS - 1))
    for g in range(N_GROUPS - 2, -1, -1):
        gidx = jnp.where(gs[g] == gmax, float(g), gidx)

    def pick(rows, k):
        out = rows[(N_GROUPS - 1) * EXPERTS_PER_GROUP + k]
        for g in range(N_GROUPS - 2, -1, -1):
            out = jnp.where(gidx == float(g), rows[g * EXPERTS_PER_GROUP + k], out)
        return out

    s = [pick(sel, k) for k in range(EXPERTS_PER_GROUP)]
    a = [pick(aff, k) for k in range(EXPERTS_PER_GROUP)]

    def first_max(vals):
        mx = functools.reduce(jnp.maximum, vals)
        idx = jnp.full_like(mx, float(len(vals) - 1))
        for k in range(len(vals) - 2, -1, -1):
            idx = jnp.where(vals[k] == mx, float(k), idx)
        return idx

    def take(vals, idx):
        out = vals[-1]
        for k in range(len(vals) - 2, -1, -1):
            out = jnp.where(idx == float(k), vals[k], out)
        return out

    i1 = first_max(s)
    s2 = [jnp.where(i1 == float(k), NEG_INF, s[k]) for k in range(EXPERTS_PER_GROUP)]
    i2 = first_max(s2)
    a1, a2 = take(a, i1), take(a, i2)
    tot = a1 + a2
    base = gidx * float(EXPERTS_PER_GROUP)
    return base + i1, base + i2, a1 / tot, a2 / tot


def _outproj_kernel(a_ref, b_ref, x_ref, wa_ref, wb_ref, g_ref, be_ref, rwt_ref, rb_ref,
                    x1_ref, route_ref, *, alpha):
    d = x_ref.shape[1]
    tm = x_ref.shape[0]
    mix = _dot(a_ref[...], wa_ref[...]) + _dot(b_ref[...], wb_ref[...])
    x1 = _layer_norm(alpha * x_ref[...] + mix, g_ref[...], be_ref[...])
    x1_ref[:, 0:d] = x1
    x_hi = x1.astype(BF16)
    x_lo = (x1 - x_hi.astype(F32)).astype(BF16)
    w = rwt_ref[...]
    w_hi = w.astype(BF16)
    w_lo = (w - w_hi.astype(F32)).astype(BF16)
    z_hi = _dot_nt(jnp.concatenate([w_hi, w_lo], axis=0), x_hi)
    zt = z_hi[0:N_EXPERTS] + z_hi[N_EXPERTS:2 * N_EXPERTS] + _dot_nt(w_hi, x_lo)
    afft = 1.0 / (1.0 + jnp.exp(-zt))
    selt = afft + rb_ref[...]
    aff = [afft[e:e + 1, :] for e in range(N_EXPERTS)]
    sel = [selt[e:e + 1, :] for e in range(N_EXPERTS)]
    e1, e2, w1, w2 = _route_rows(aff, sel)
    rec = jnp.concatenate([e1, e2, w1, w2, jnp.zeros((LANE - 4, tm), F32)], axis=0)
    route_ref[...] = rec[0:8]
    x1_ref[:, d:d + LANE] = rec.T


def _outproj(a, b, xt, wa, wb, g, be, rwt, rb, alpha, tm):
    n, d = xt.shape
    full = lambda i: (0, 0)
    return pl.pallas_call(
        functools.partial(_outproj_kernel, alpha=alpha),
        out_shape=[jax.ShapeDtypeStruct((n, d + LANE), F32), jax.ShapeDtypeStruct((8, n), F32)],
        grid=(n // tm,),
        in_specs=[pl.BlockSpec((tm, a.shape[1]), lambda i: (i, 0)),
                  pl.BlockSpec((tm, b.shape[1]), lambda i: (i, 0)),
                  pl.BlockSpec((tm, d), lambda i: (i, 0)),
                  pl.BlockSpec(wa.shape, full), pl.BlockSpec(wb.shape, full),
                  pl.BlockSpec((1, d), full), pl.BlockSpec((1, d), full),
                  pl.BlockSpec(rwt.shape, full), pl.BlockSpec(rb.shape, full)],
        out_specs=[pl.BlockSpec((tm, d + LANE), lambda i: (i, 0)), pl.BlockSpec((8, tm), lambda i: (0, i))],
        compiler_params=_cparams(("parallel",)),
        name="outproj_ln_route",
    )(a, b, xt, wa, wb, g, be, rwt, rb)


def _moe_kernel(bea_ref, beb_ref, bbase_ref, nact_ref, src_ref,
                x_hbm, wga_ref, wgb_ref, wua_ref, wub_ref, wda_ref, wdb_ref, o_ref, xbuf0, xbuf1, sem, *, tm, d):
    i = pl.program_id(0)
    nact = nact_ref[0]
    last = src_ref.shape[0] - 1
    bufs = (xbuf0, xbuf1)

    def issue(blk, slot):
        base = bbase_ref[blk]
        for r in range(tm):
            tok = src_ref[jnp.minimum(base + r, last)]
            pltpu.make_async_copy(x_hbm.at[pl.ds(tok, 1), :], bufs[slot].at[pl.ds(r, 1), :],
                                  sem.at[slot]).start(priority=r % 2)

    def wait(slot):
        pltpu.make_async_copy(x_hbm.at[pl.ds(0, tm), :], bufs[slot], sem.at[slot]).wait()

    @pl.when((i == 0) & (nact > 0))
    def _():
        issue(0, 0)

    def step(slot):
        wait(slot)

        @pl.when(i + 1 < nact)
        def _():
            issue(i + 1, 1 - slot)

        buf = bufs[slot]
        xb = buf[:, 0:d].astype(BF16)
        rec = buf[:, d:d + LANE]
        e1, e2, w1, w2 = rec[:, 0:1], rec[:, 1:2], rec[:, 2:3], rec[:, 3:4]
        out = None
        for e_ref, wg_ref, wu_ref, wd_ref in ((bea_ref, wga_ref, wua_ref, wda_ref),
                                              (beb_ref, wgb_ref, wub_ref, wdb_ref)):
            expert = e_ref[i].astype(F32)
            gate_w = jnp.where(e1 == expert, w1, jnp.where(e2 == expert, w2, 0.0))
            gate = _dot(xb, wg_ref[...])
            up = _dot(xb, wu_ref[...])
            h = gate * (1.0 / (1.0 + jnp.exp(-gate))) * up * gate_w
            part = _dot(h.astype(BF16), wd_ref[...])
            out = part if out is None else out + part
        o_ref[...] = out

    for slot in range(2):
        pl.when((i < nact) & (i % 2 == slot))(functools.partial(step, slot))

    @pl.when(i >= nact)
    def _():
        o_ref[...] = jnp.zeros_like(o_ref)


def _moe_experts(x1e, wg, wu, wd, bea, beb, bbase, nact, src, tm):
    n, de = x1e.shape
    d = de - LANE
    nblk = bea.shape[0]
    f = wg.shape[2]
    amap = lambda i, bea, beb, *_: (bea[i], 0, 0)
    bmap = lambda i, bea, beb, *_: (beb[i], 0, 0)
    grid_spec = pltpu.PrefetchScalarGridSpec(
        num_scalar_prefetch=5,
        grid=(nblk,),
        in_specs=[pl.BlockSpec(memory_space=pl.ANY),
                  pl.BlockSpec((None, d, f), amap), pl.BlockSpec((None, d, f), bmap),
                  pl.BlockSpec((None, d, f), amap), pl.BlockSpec((None, d, f), bmap),
                  pl.BlockSpec((None, f, d), amap), pl.BlockSpec((None, f, d), bmap)],
        out_specs=pl.BlockSpec((tm, d), lambda i, *_: (i, 0)),
        scratch_shapes=[pltpu.VMEM((tm, de), F32), pltpu.VMEM((tm, de), F32), pltpu.SemaphoreType.DMA((2,))],
    )
    return pl.pallas_call(
        functools.partial(_moe_kernel, tm=tm, d=d),
        out_shape=jax.ShapeDtypeStruct((nblk * tm, d), F32),
        grid_spec=grid_spec,
        compiler_params=_cparams(("arbitrary",)),
        name="moe_experts",
    )(bea, beb, bbase, nact, src, x1e, wg, wg, wu, wu, wd, wd)


def _combine_kernel(p_ref, x_ref, g_ref, be_ref, o_hbm, y_ref, obuf0, obuf1, sem, *, alpha, tm, d):
    i = pl.program_id(0)
    nsteps = pl.num_programs(0)

    bufs = (obuf0, obuf1)

    def issue(blk, slot):
        base = blk * tm
        for r in range(tm):
            pltpu.make_async_copy(o_hbm.at[pl.ds(p_ref[base + r], 1), :],
                                  bufs[slot].at[pl.ds(r, 1), :], sem.at[slot]).start(priority=r % 2)

    def wait(slot):
        pltpu.make_async_copy(o_hbm.at[pl.ds(0, tm), :], bufs[slot], sem.at[slot]).wait()

    @pl.when(i == 0)
    def _():
        issue(0, 0)

    def step(slot):
        wait(slot)

        @pl.when(i + 1 < nsteps)
        def _():
            issue(i + 1, 1 - slot)

        y = alpha * x_ref[:, 0:d] + bufs[slot][...]
        y_ref[...] = _layer_norm(y, g_ref[...], be_ref[...])

    for slot in range(2):
        pl.when(i % 2 == slot)(functools.partial(step, slot))


def _combine(x1e, g, be, o, pos, alpha, tm):
    n, de = x1e.shape
    d = de - LANE
    grid_spec = pltpu.PrefetchScalarGridSpec(
        num_scalar_prefetch=1,
        grid=(n // tm,),
        in_specs=[pl.BlockSpec((tm, de), lambda i, *_: (i, 0)),
                  pl.BlockSpec((1, d), lambda i, *_: (0, 0)),
                  pl.BlockSpec((1, d), lambda i, *_: (0, 0)),
                  pl.BlockSpec(memory_space=pl.ANY)],
        out_specs=pl.BlockSpec((tm, d), lambda i, *_: (i, 0)),
        scratch_shapes=[pltpu.VMEM((tm, d), F32), pltpu.VMEM((tm, d), F32), pltpu.SemaphoreType.DMA((2,))],
    )
    return pl.pallas_call(
        functools.partial(_combine_kernel, alpha=alpha, tm=tm, d=d),
        out_shape=jax.ShapeDtypeStruct((n, d), F32),
        grid_spec=grid_spec,
        compiler_params=_cparams(("arbitrary",)),
        name="moe_combine_ln",
    )(pos, x1e, g, be, o)


def _moe_layer(x1e, route_t, wg, wu, wd, g, be, alpha, tm_e, tm_c):
    n = x1e.shape[0]
    pairs = [(a, b) for a in range(EXPERTS_PER_GROUP) for b in range(a + 1, EXPERTS_PER_GROUP)]
    nseg = N_GROUPS * len(pairs)
    seg_a = jnp.array([g * EXPERTS_PER_GROUP + a for g in range(N_GROUPS) for a, b in pairs], I32)
    seg_b = jnp.array([g * EXPERTS_PER_GROUP + b for g in range(N_GROUPS) for a, b in pairs], I32)
    e1, e2 = route_t[0].astype(I32), route_t[1].astype(I32)
    lo, hi = jnp.minimum(e1, e2), jnp.maximum(e1, e2)
    la, lb = lo % EXPERTS_PER_GROUP, hi % EXPERTS_PER_GROUP
    pair_id = la * (2 * EXPERTS_PER_GROUP - 1 - la) // 2 + (lb - la - 1)
    seg = (lo // EXPERTS_PER_GROUP) * len(pairs) + pair_id
    onehot = (seg[:, None] == jnp.arange(nseg, dtype=I32)[None, :]).astype(I32)
    rank = jnp.take_along_axis(jnp.cumsum(onehot, axis=0) - onehot, seg[:, None], axis=1)[:, 0]
    counts = jnp.sum(onehot, axis=0)
    nblk_s = (counts + tm_e - 1) // tm_e
    blk_start = jnp.cumsum(nblk_s) - nblk_s
    seg_start = jnp.cumsum(counts) - counts
    pos = (blk_start[seg] * tm_e + rank).astype(I32)
    src = jnp.argsort(seg, stable=True).astype(I32)
    nblk = n // tm_e + nseg
    blk = jnp.arange(nblk, dtype=I32)
    blk_end = jnp.cumsum(nblk_s)
    bseg = jnp.minimum(jnp.sum((blk[:, None] >= blk_end[None, :]).astype(I32), axis=1), nseg - 1)
    bbase = (seg_start[bseg] + (blk - blk_start[bseg]) * tm_e).astype(I32)
    nact = jnp.sum(nblk_s).astype(I32).reshape(1)
    o = _moe_experts(x1e, wg, wu, wd, seg_a[bseg], seg_b[bseg], bbase, nact, src, tm_e)
    return _combine(x1e, g, be, o, pos, alpha, tm_c)


def _rms(x, g):
    return x * lax.rsqrt(jnp.mean(x * x, axis=-1, keepdims=True) + RMS_EPS) * g


def _odd_proj_kernel(x_ref, w_ref, qng_ref, kvng_ref, wqa_ref, wqb_ref, wk_ref, wv_ref, cos_ref, sin_ref,
                     tq_ref, tk_ref,
                     qc_ref, kc_ref, vc_ref, qi_ref, ki_ref, wi_ref, q8_ref, k8_ref, v8_ref, *, q_scale):
    o = ODD_OFF
    tm = x_ref.shape[0]
    lo = _lane_lo((tm, LANE))
    h = _dot(x_ref[...].astype(BF16), w_ref[...])
    kdup = h[:, o["kc"]:o["vc"]]
    vdup = h[:, o["vc"]:o["qi"]]
    kc_ref[:, 0:LANE] = jnp.where(lo, kdup, tk_ref[:, 0:LANE].astype(F32)).astype(BF16)
    kc_ref[:, LANE:2 * LANE] = jnp.where(lo, tk_ref[:, LANE:2 * LANE].astype(F32), kdup).astype(BF16)
    vc_ref[:, 0:LANE] = jnp.where(lo, vdup, 1.0).astype(BF16)
    vc_ref[:, LANE:2 * LANE] = jnp.where(lo, 1.0, vdup).astype(BF16)
    qi_ref[...] = h[:, o["qi"]:o["ki"]].astype(BF16)
    ki_ref[...] = h[:, o["ki"]:o["wi"]].astype(BF16)
    wi_ref[...] = h[:, o["wi"]:o["wi"] + IDX_HEADS]
    cos = cos_ref[...]
    sin = sin_ref[...]
    cqn = _rms(h[:, o["cq"]:o["ckv"]], qng_ref[...]).astype(BF16)
    qa = _dot(cqn, wqa_ref[...])
    qb = _dot(cqn, wqb_ref[...])
    ckvn = _rms(h[:, o["ckv"]:o["kra"]], kvng_ref[...]).astype(BF16)
    kn = _dot(ckvn, wk_ref[...])
    v8 = _dot(ckvn, wv_ref[...])
    kr = h[:, o["kra"]:o["krb"]] * cos + h[:, o["krb"]:o["end"]] * sin
    for hd in range(D_HEADS):
        sl = slice(hd * LANE, (hd + 1) * LANE)
        pj = slice((hd // 2) * LANE, (hd // 2 + 1) * LANE)
        own = lo if hd % 2 == 0 else ~lo
        qc_ref[:, sl] = jnp.where(own, h[:, pj], tq_ref[:, sl].astype(F32)).astype(BF16)
        q8_ref[:, sl] = ((qa[:, sl] * cos + qb[:, sl] * sin) * q_scale).astype(BF16)
        k8_ref[:, sl] = (kn[:, sl] + kr).astype(BF16)
        v8_ref[:, sl] = jnp.where(own, v8[:, pj], 1.0).astype(BF16)


def _odd_proj(xt, w_cat, qng, kvng, wqa, wqb, wk, wv, cos_t, sin_t, tq_t, tk_t, seq, tm):
    n, d = xt.shape
    full = lambda i: (0, 0)
    tps = seq // tm
    pos = lambda i: (i % tps, 0)
    widths = (C_HEADS * LANE, 2 * LANE, 2 * LANE, IDX_Q, LANE)
    out_shape = [jax.ShapeDtypeStruct((n, w), BF16) for w in widths]
    out_specs = [pl.BlockSpec((tm, w), lambda i: (i, 0)) for w in widths]
    out_shape.append(jax.ShapeDtypeStruct((n, IDX_HEADS), F32))
    out_specs.append(pl.BlockSpec((tm, IDX_HEADS), lambda i: (i, 0)))
    for w in (D_HEADS * LANE,) * 3:
        out_shape.append(jax.ShapeDtypeStruct((n, w), BF16))
        out_specs.append(pl.BlockSpec((tm, w), lambda i: (i, 0)))
    return pl.pallas_call(
        functools.partial(_odd_proj_kernel, q_scale=(MLA_NOPE + MLA_ROPE) ** -0.5 * LOG2E),
        out_shape=out_shape,
        grid=(n // tm,),
        in_specs=[pl.BlockSpec((tm, d), lambda i: (i, 0)),
                  pl.BlockSpec(w_cat.shape, full),
                  pl.BlockSpec(qng.shape, full), pl.BlockSpec(kvng.shape, full),
                  pl.BlockSpec(wqa.shape, full), pl.BlockSpec(wqb.shape, full),
                  pl.BlockSpec(wk.shape, full), pl.BlockSpec(wv.shape, full),
                  pl.BlockSpec((tm, LANE), pos), pl.BlockSpec((tm, LANE), pos),
                  pl.BlockSpec((tm, C_HEADS * LANE), pos), pl.BlockSpec((tm, 2 * LANE), pos)],
        out_specs=out_specs,
        compiler_params=_cparams(("parallel",)),
        name="odd_proj",
    )(xt, w_cat, qng, kvng, wqa, wqb, wk, wv, cos_t, sin_t, tq_t, tk_t)


def _dsa_kernel(q_ref, k_ref, v_ref, qi_ref, ki_ref, wt_ref, o_ref, sc_ref, mb_ref, *, top_k, chunk, qb):
    i = pl.program_id(1)
    nc = (i * qb + qb + chunk - 1) // chunk
    lo = _lane_lo((qb, LANE))
    s_row2 = lax.broadcasted_iota(I32, (2 * chunk, qb), 0)
    t_lane2 = i * qb + lax.broadcasted_iota(I32, (2 * chunk, qb), 1)

    qidx = []
    for h in range(IDX_HEADS):
        qp = qi_ref[:, (h // 2) * LANE:(h // 2 + 1) * LANE]
        qidx.append(jnp.where(lo if h % 2 == 0 else ~lo, qp, jnp.zeros_like(qp)))
    q_stack = jnp.concatenate(qidx, axis=0)
    w = wt_ref[...]

    npair = (nc + 1) // 2

    def pair_start(cp):
        return cp * (2 * chunk)

    def select(pairs):
        def score_pair(cp):
            start = pair_start(cp)
            d = _dot_nt(ki_ref[pl.ds(start, 2 * chunk), :], q_stack)
            sc = w[0:1, :] * jnp.maximum(d[:, 0:qb], 0.0)
            for h in range(1, IDX_HEADS):
                sc = sc + w[h:h + 1, :] * jnp.maximum(d[:, h * qb:(h + 1) * qb], 0.0)
            sc = jnp.where(sc == 0.0, 0.0, sc)
            sc_ref[pl.ds(start, 2 * chunk), :] = jnp.where(start + s_row2 <= t_lane2, sc, NEG_INF)

        for cp in range(pairs):
            score_pair(cp)

        nacc = 4 * 8

        def count(trial, strict):
            acc = jnp.zeros((nacc, qb), F32)
            for cp in range(pairs):
                sc = sc_ref[pl.ds(pair_start(cp), 2 * chunk), :]
                hit = (sc > trial) if strict else (sc >= trial)
                acc = acc + jnp.sum(jnp.where(hit, 1.0, 0.0).reshape(2 * chunk // nacc, nacc, qb), axis=0)
            return jnp.sum(acc, axis=0, keepdims=True)

        def decode(key):
            return lax.bitcast_convert_type(jnp.where(key < 0, key ^ jnp.int32(0x7FFFFFFF), key), F32)

        kf = float(top_k)
        key_neg_inf = jnp.int32(-2139095041)
        cand = jnp.where(count(jnp.zeros((1, qb), F32), False) >= kf, 0, jnp.int32(-2 ** 31)).astype(I32)

        def bit_step(j, cand):
            trial = cand + lax.shift_left(jnp.int32(1), 30 - j)
            ok = (trial <= key_neg_inf) | (count(decode(trial), False) >= kf)
            return jnp.where(ok, trial, cand)

        thr = decode(lax.fori_loop(0, 31, bit_step, cand))
        need = kf - count(thr, True)

        r0 = lax.broadcasted_iota(I32, (chunk, chunk), 0)
        r1 = lax.broadcasted_iota(I32, (chunk, chunk), 1)
        below = jnp.where(r1 < r0, 1.0, 0.0).astype(BF16)
        need = jnp.where(thr == NEG_INF, 0.0, need)

        offset = jnp.zeros((1, qb), F32)
        for cp in range(pairs):
            start = pair_start(cp)
            sc = sc_ref[pl.ds(start, 2 * chunk), :]
            eq = sc == thr
            eqf = jnp.where(eq, 1.0, 0.0)
            eqb = eqf.astype(BF16)
            first = jnp.sum(eqf[0:chunk], axis=0, keepdims=True)
            prefix = jnp.concatenate([_dot(below, eqb[0:chunk]) + offset,
                                      _dot(below, eqb[chunk:2 * chunk]) + (offset + first)], axis=0)
            bias = jnp.where(sc > thr, 0.0, jnp.where(eq, jnp.where(prefix < need, 0.0, NEG_INF), NEG_INF))
            mb_ref[:, pl.ds(start, 2 * chunk)] = bias.T
            offset = offset + first + jnp.sum(eqf[chunk:2 * chunk], axis=0, keepdims=True)

    for pairs in range(1, k_ref.shape[0] // (2 * chunk) + 1):
        pl.when(npair == pairs)(functools.partial(select, pairs))

    nslab = qb // BLOCK
    chains = [(h, r) for h in range(C_HEADS) for r in range(nslab)]

    def att_step(start, widths, carry):
        kp = [[k_ref[pl.ds(start, w), par * LANE:(par + 1) * LANE] for w in widths] for par in range(2)]
        vp = [[v_ref[pl.ds(start, w), par * LANE:(par + 1) * LANE] for w in widths] for par in range(2)]
        mb = [mb_ref[r * BLOCK:(r + 1) * BLOCK, pl.ds(start, widths[r])].astype(BF16) for r in range(nslab)]
        qs = [q_ref[r * BLOCK:(r + 1) * BLOCK, h * LANE:(h + 1) * LANE] for h, r in chains]
        return _online_softmax_chains(qs, [kp[h % 2][r] for h, r in chains], [vp[h % 2][r] for h, r in chains],
                                      carry, biases=[mb[r] for h, r in chains], guard_empty=True)

    init = tuple((jnp.full((BLOCK, 1), NEG_INF, F32), jnp.zeros((BLOCK, LANE), F32)) for _ in chains)
    trim_last = qb == chunk
    nloop = nc - 1 if trim_last else nc
    carry = lax.fori_loop(
        0, nloop // 2,
        lambda cp, c: att_step(pl.multiple_of(cp * 2 * chunk, 2 * chunk), [2 * chunk] * nslab, c), init)
    carry = lax.fori_loop(
        0, nloop % 2,
        lambda _, c: att_step(pl.multiple_of((nloop // 2) * 2 * chunk, chunk), [chunk] * nslab, c), carry)
    if trim_last:
        carry = att_step(pl.multiple_of((nc - 1) * chunk, chunk), [(r + 1) * BLOCK for r in range(nslab)], carry)
    accs = [jnp.concatenate([carry[h * nslab + r][1] for r in range(nslab)], axis=0) for h in range(C_HEADS)]
    _normalize_pairs(accs, o_ref)


def _dsa(qc, kc, vc, qi, ki, wit):
    b, s, _ = qc.shape
    top_k = min(DSA_TOPK_MAX, s // 4)
    chunk = 256
    qb = min(2 * BLOCK, s)
    assert s % (2 * chunk) == 0 and top_k <= chunk and chunk % qb == 0
    blk = lambda bi, i: (bi, i, 0)
    seq = lambda bi, i: (bi, 0, 0)
    return pl.pallas_call(
        functools.partial(_dsa_kernel, top_k=top_k, chunk=chunk, qb=qb),
        out_shape=jax.ShapeDtypeStruct((b, s, C_HEADS * HEAD_DIM), BF16),
        grid=(b, s // qb),
        in_specs=[pl.BlockSpec((None, qb, C_HEADS * LANE), blk),
                  pl.BlockSpec((None, s, 2 * LANE), seq),
                  pl.BlockSpec((None, s, 2 * LANE), seq),
                  pl.BlockSpec((None, qb, IDX_Q), blk),
                  pl.BlockSpec((None, s, LANE), seq),
                  pl.BlockSpec((None, IDX_HEADS, qb), lambda bi, i: (bi, 0, i))],
        out_specs=pl.BlockSpec((None, qb, C_Q), blk),
        scratch_shapes=[pltpu.VMEM((s, qb), F32), pltpu.VMEM((qb, s), F32)],
        compiler_params=_cparams(("parallel", "arbitrary")),
        name="dsa_attn",
    )(qc, kc, vc, qi, ki, wit)


def _dup64(w):
    d, c = w.shape
    w = w.reshape(d, c // HEAD_DIM, 1, HEAD_DIM)
    return jnp.broadcast_to(w, (d, c // HEAD_DIM, 2, HEAD_DIM)).reshape(d, 2 * c)


def _pad_cols(w, width):
    return jnp.pad(w, ((0, 0), (0, width - w.shape[1])))


def _split_cols(w, sizes):
    out, start = [], 0
    for n in sizes:
        out.append(w[:, start:start + n])
        start += n
    return out


def _rope_partner(w):
    half = MLA_ROPE // 2
    return jnp.concatenate([-w[..., half:], w[..., :half]], axis=-1)


def _rope_block(w_rope):
    d = w_rope.shape[0]
    return jnp.concatenate([jnp.zeros((d, MLA_NOPE), F32), w_rope,
                            jnp.zeros((d, LANE - MLA_NOPE - MLA_ROPE), F32)], axis=1)


def _alibi_tables(s):
    slopes = 2.0 ** (-8.0 * jnp.arange(1, C_HEADS + 1, dtype=F32) / C_HEADS)
    a = slopes * LOG2E
    pos = jnp.arange(s, dtype=F32)
    a_sp = _split3(a)
    r_sp = _split3(-(a[None, :] * pos[:, None]))
    ones = jnp.ones((s, C_HEADS), F32)
    q_terms = [128.0 * t * ones for t in a_sp] + [t * ones for t in a_sp] + list(r_sp)
    q_aug = jnp.stack(q_terms, axis=-1)
    q_aug = jnp.pad(q_aug, ((0, 0), (0, 0), (0, HEAD_DIM - q_aug.shape[-1])))
    zeros = jnp.zeros_like(q_aug)
    even = jnp.concatenate([zeros, q_aug], axis=-1)
    odd = jnp.concatenate([q_aug, zeros], axis=-1)
    parity = (jnp.arange(C_HEADS) % 2 == 0)[None, :, None]
    tq_t = jnp.where(parity, even, odd).reshape(s, C_HEADS * LANE)
    s_hi = jnp.floor(pos / LANE)
    s_lo = pos - LANE * s_hi
    k_aug = jnp.stack([s_hi] * 3 + [s_lo] * 3 + [jnp.ones_like(pos)] * 3, axis=-1)
    k_aug = jnp.pad(k_aug, ((0, 0), (0, HEAD_DIM - k_aug.shape[-1])))
    kz = jnp.zeros_like(k_aug)
    tk_t = jnp.concatenate([kz, k_aug, k_aug, kz], axis=-1)
    return tq_t.astype(BF16), tk_t.astype(BF16)


def kernel(x, even_w_in, even_b_f, even_sinks, odd_w_in, odd_q_norm, odd_kv_norm, odd_w_uq, odd_w_ukv,
           w_o, ln_g, ln_b, router_w, router_b, moe_w_gate, moe_w_up, moe_w_down):
    b, s, d = x.shape
    n = b * s
    depth = w_o.shape[0]
    alpha = (2 * depth) ** 0.25
    tm = min(512, s)
    tq = min(256, s)
    tm_e = min(256, n)
    tm_c = min(512, n)
    tm_o = min(1024, n)
    att_scale = HEAD_DIM ** -0.5
    xt = x.reshape(n, d)
    rwt = router_w.T
    rb = router_b.reshape(N_EXPERTS, 1)

    for l in range(depth):
        j = l // 2
        if l % 2 == 0:
            qa, ka, va, qb, kb, vb, fg = _split_cols(
                even_w_in[j], (A_Q, A_KV_HEADS * HEAD_DIM, A_KV_HEADS * HEAD_DIM, B_W, B_W, B_W, B_HEADS))
            w_cat = jnp.concatenate([qa * (att_scale * LOG2E), _dup64(ka), _dup64(va), qb * (att_scale * LOG2E), kb, vb,
                                     _pad_cols(fg, LANE)], axis=1).astype(BF16)
            bf_pad = _pad_cols(even_b_f[j].reshape(1, B_HEADS), LANE)
            qa_, ka_, va_, qf, kf, vf = _even_proj(xt, w_cat, bf_pad, s, tm)
            r3 = lambda a: a.reshape(b, s, a.shape[1])
            out_a = _swa(even_sinks[j], r3(qa_), r3(ka_), r3(va_))
            out_b = _flash(r3(qf), r3(kf), r3(vf), tq=tq, nh=B_HEADS, name="fox_attn")
        else:
            (qc, kc, vc, qi, ki, wi, cq, ckv, kr) = _split_cols(
                odd_w_in[j], (C_Q, HEAD_DIM, HEAD_DIM, IDX_Q, IDX_DIM, IDX_HEADS,
                              MLA_Q_RANK, MLA_KV_RANK, MLA_ROPE))
            w_cat = jnp.concatenate([
                qc * (att_scale * LOG2E), _dup64(kc), _dup64(vc), qi * (IDX_DIM ** -0.5), _dup64(ki),
                _pad_cols(wi * (IDX_HEADS ** -0.5), LANE), cq, ckv,
                _rope_block(kr), _rope_block(_rope_partner(kr))], axis=1).astype(BF16)
            dq = MLA_NOPE + MLA_ROPE
            wuq = odd_w_uq[j].reshape(MLA_Q_RANK, D_HEADS, dq)
            pad_q = jnp.zeros((MLA_Q_RANK, D_HEADS, LANE - dq), F32)
            wqa = jnp.concatenate([wuq, pad_q], axis=2).reshape(MLA_Q_RANK, D_HEADS * LANE).astype(BF16)
            wqb = jnp.concatenate([jnp.zeros((MLA_Q_RANK, D_HEADS, MLA_NOPE), F32),
                                   _rope_partner(wuq[..., MLA_NOPE:]), pad_q],
                                  axis=2).reshape(MLA_Q_RANK, D_HEADS * LANE).astype(BF16)
            wukv = odd_w_ukv[j].reshape(MLA_KV_RANK, D_HEADS, MLA_NOPE + MLA_V)
            wk = jnp.concatenate([wukv[..., :MLA_NOPE], jnp.zeros((MLA_KV_RANK, D_HEADS, LANE - MLA_NOPE), F32)],
                                 axis=2).reshape(MLA_KV_RANK, D_HEADS * LANE).astype(BF16)
            wv = wukv[..., MLA_NOPE:].reshape(MLA_KV_RANK, D_HEADS * MLA_V).astype(BF16)
            half = MLA_ROPE // 2
            inv = ROPE_THETA ** (-jnp.arange(half, dtype=F32) / half)
            ang = jnp.arange(s, dtype=F32)[:, None] * inv[None, :]
            cos_h, sin_h = jnp.cos(ang), jnp.sin(ang)
            pad_t = jnp.zeros((s, LANE - dq), F32)
            cos_t = jnp.concatenate([jnp.ones((s, MLA_NOPE), F32), cos_h, cos_h, pad_t], axis=1)
            sin_t = jnp.concatenate([jnp.zeros((s, MLA_NOPE), F32), sin_h, sin_h, pad_t], axis=1)
            tq_t, tk_t = _alibi_tables(s)
            qc_, kc_, vc_, qi_, ki_, wi_, q8, k8, v8 = _odd_proj(
                xt, w_cat, odd_q_norm[j].reshape(1, -1), odd_kv_norm[j].reshape(1, -1),
                wqa, wqb, wk, wv, cos_t, sin_t, tq_t, tk_t, s, tm)
            r3 = lambda a: a.reshape(b, s, a.shape[1])
            out_a = _dsa(r3(qc_), r3(kc_), r3(vc_), r3(qi_), r3(ki_), r3(wi_).transpose(0, 2, 1))
            out_b = _flash(r3(q8), r3(k8), r3(v8), tq=tq, nh=D_HEADS, name="mla_attn")
        half_w = out_a.shape[2]
        wo = w_o[l].astype(BF16)
        x1e, route_t = _outproj(out_a.reshape(n, half_w), out_b.reshape(n, -1), xt,
                                wo[:half_w], wo[half_w:], ln_g[l, 0].reshape(1, d), ln_b[l, 0].reshape(1, d),
                                rwt, rb, alpha, tm_o)
        xt = _moe_layer(x1e, route_t, moe_w_gate[l].astype(BF16), moe_w_up[l].astype(BF16),
                        moe_w_down[l].astype(BF16), ln_g[l, 1].reshape(1, d), ln_b[l, 1].reshape(1, d),
                        alpha, tm_e, tm_c)
    return xt.reshape(b, s, d)
```
